```python
import math
import jax, jax.numpy as jnp
from jax import lax
import numpy as np

D_MODEL = 2048
BATCH = 1
SEQ = 16384
DEPTH = 2

DIFF_HEADS = 8
DIFF_QK_DIM = 64
DIFF_V_DIM = 2 * DIFF_QK_DIM
DIFF_DIM = DIFF_HEADS * DIFF_V_DIM
Q_BLOCK = 128
RWKV_HEADS = 16
RWKV_HEAD_DIM = 64
RWKV_DIM = RWKV_HEADS * RWKV_HEAD_DIM
DECAY_LORA = 64
AAA_LORA = 64
GATE_LORA = 160
RWKV_IN = 3 * RWKV_DIM + 2 * DECAY_LORA + 2 * AAA_LORA + GATE_LORA
N_MEM = 256
MEM_HEADS = 4
MEM_HEAD_DIM = 256
MEM_DIM = MEM_HEADS * MEM_HEAD_DIM
N_BRANCH = 3
BRANCH_DIM = 1024
D_FF = 5632
CONV_WIDTH = 3
N_IN = 3 * DIFF_DIM + RWKV_IN + MEM_DIM + N_BRANCH * D_MODEL
NORM_EPS = 1e-6
LNX_EPS = 64e-5

kernel_name = "hybrid_diffattn_rwkv7_memxattn_convglu_encoder"


def _rms(x, g):
    xf = x.astype(jnp.float32)
    y = xf * lax.rsqrt(jnp.mean(xf * xf, axis=-1, keepdims=True) + NORM_EPS)
    return y.astype(x.dtype) * g


def _shift_prev(x):
    return jnp.pad(x[:, :-1], ((0, 0), (1, 0), (0, 0)))


def _shift_next(x):
    return jnp.pad(x[:, 1:], ((0, 0), (0, 1), (0, 0)))


def _diff_attention(q, k, v, lam):
    B, S, H = q.shape[:3]
    nb = S // Q_BLOCK
    slopes = jnp.exp2(-8.0 * jnp.arange(1, H + 1, dtype=jnp.float32) / H)
    kpos = jnp.arange(S, dtype=jnp.float32)
    qb = jnp.moveaxis(q.reshape(B, nb, Q_BLOCK, H, 2, q.shape[-1]), 1, 0)
    starts = jnp.arange(nb, dtype=jnp.float32) * Q_BLOCK

    def block(args):
        qblk, start = args
        qpos = start + jnp.arange(Q_BLOCK, dtype=jnp.float32)
        bias = -slopes[:, None, None] * jnp.abs(qpos[:, None] - kpos[None, :])
        s = jnp.einsum('bqhmd,bshmd->bhmqs', qblk, k).astype(jnp.float32) + bias[None, :, None]
        p = jax.nn.softmax(s, axis=-1)
        a = p[:, :, 0] - lam * p[:, :, 1]
        return jnp.einsum('bhqs,bshd->bqhd', a.astype(v.dtype), v)

    o = lax.map(block, (qb, starts))
    return jnp.moveaxis(o, 0, 1).reshape(B, S, H, v.shape[-1])


def _diff_branch(p, qk_g, lam_p, subln_g, lam_init):
    B, S, _ = p.shape
    q, k, v = jnp.split(p, 3, axis=-1)
    q = _rms(q.reshape(B, S, DIFF_HEADS, 2, DIFF_QK_DIM), qk_g[0]) * (DIFF_QK_DIM ** -0.5)
    k = _rms(k.reshape(B, S, DIFF_HEADS, 2, DIFF_QK_DIM), qk_g[1])
    v = v.reshape(B, S, DIFF_HEADS, DIFF_V_DIM)
    lp = lam_p.astype(jnp.float32)
    lam = jnp.exp(jnp.sum(lp[0] * lp[1])) - jnp.exp(jnp.sum(lp[2] * lp[3])) + lam_init
    o = _diff_attention(q, k, v, lam)
    o = _rms(o, subln_g) * (1.0 - lam_init)
    return o.reshape(B, S, DIFF_DIM)


def _wkv7(r, w, k, v, kk, b, reverse):
    B, S, H, N = r.shape

    def step(state, inp):
        r_t, w_t, k_t, v_t, kk_t, b_t = inp
        sa = jnp.einsum('bhvk,bhk->bhv', state, kk_t)
        state = (state * w_t[:, :, None, :] - sa[..., None] * b_t[:, :, None, :]
                 + v_t[..., None] * k_t[:, :, None, :])
        return state, jnp.einsum('bhvk,bhk->bhv', state, r_t)

    xs = tuple(jnp.moveaxis(t, 1, 0) for t in (r, w, k, v, kk, b))
    s0 = jnp.zeros((B, H, N, N), jnp.float32)
    _, ys = lax.scan(step, s0, xs, reverse=reverse)
    return jnp.moveaxis(ys, 0, 1)


def _rwkv_branch(p, mu, w0, w2, a0, a2, g2, k_k, k_a, r_k, lnx_g, lnx_b):
    B, S, _ = p.shape
    H, N = RWKV_HEADS, RWKV_HEAD_DIM
    f32 = jnp.float32
    p = p + mu[0] * (_shift_prev(p) - p) + mu[1] * (_shift_next(p) - p)
    c0 = 3 * RWKV_DIM
    r, k, v, lw, la, lg = jnp.split(
        p, [RWKV_DIM, 2 * RWKV_DIM, c0, c0 + 2 * DECAY_LORA, c0 + 2 * DECAY_LORA + 2 * AAA_LORA], axis=-1)
    lw = jnp.tanh(lw.reshape(B, S, 2, DECAY_LORA))
    w_log = -jax.nn.softplus(-(w0 + jnp.einsum('bsdr,drc->bsdc', lw, w2)).astype(f32)) - 0.5
    decay = jnp.exp(-jnp.exp(w_log))
    a = jax.nn.sigmoid((a0 + jnp.einsum('bsdr,drc->bsdc', la.reshape(B, S, 2, AAA_LORA), a2)).astype(f32))
    g = jax.nn.sigmoid(lg) @ g2
    r, k, v = r.astype(f32), k.astype(f32), v.astype(f32)
    kk = (k * k_k).reshape(B, S, H, N)
    kk = kk / jnp.maximum(jnp.sqrt(jnp.sum(kk * kk, axis=-1, keepdims=True)), 1e-12)
    k_dir = k[:, :, None, :] * (1.0 + (a - 1.0) * k_a)
    b_dir = kk.reshape(B, S, 1, RWKV_DIM) * a

    def heads(t):
        return t.reshape(B, S, H, N)

    rh, vh = heads(r), heads(v)
    y = (_wkv7(rh, heads(decay[:, :, 0]), heads(k_dir[:, :, 0]), vh, kk, heads(b_dir[:, :, 0]), reverse=False)
         + _wkv7(rh, heads(decay[:, :, 1]), heads(k_dir[:, :, 1]), vh, kk, heads(b_dir[:, :, 1]), reverse=True))
    mean = jnp.mean(y, axis=-1, keepdims=True)
    var = jnp.mean(jnp.square(y - mean), axis=-1, keepdims=True)
    y = ((y - mean) * lax.rsqrt(var + LNX_EPS)).reshape(B, S, RWKV_DIM) * lnx_g + lnx_b
    bonus = jnp.sum(rh * heads(k) * r_k, axis=-1, keepdims=True) * vh
    y = (y + bonus.reshape(B, S, RWKV_DIM)) * g
    return y.astype(p.dtype)


def _mem_branch(p, mem, mem_norm_g, w_kv, qk_g):
    B, S, _ = p.shape
    M = mem.shape[1]
    q = _rms(p.reshape(B, S, MEM_HEADS, MEM_HEAD_DIM), qk_g[0]) * (MEM_HEAD_DIM ** -0.5)
    kv = (_rms(mem, mem_norm_g) @ w_kv).reshape(B, M, 2, MEM_HEADS, MEM_HEAD_DIM)
    km = _rms(kv[:, :, 0], qk_g[1])
    vm = kv[:, :, 1]
    s = jnp.einsum('bshd,bmhd->bhsm', q, km).astype(jnp.float32)
    pr = jax.nn.softmax(s, axis=-1)
    o = jnp.einsum('bhsm,bmhd->bshd', pr.astype(vm.dtype), vm)
    return o.reshape(B, S, MEM_DIM)


def setup_inputs(seed: int = 0) -> dict:
    key = jax.random.key(seed)
    ks = iter(jax.random.split(key, 32))

    def nrm(shape, scale):
        return jax.random.normal(next(ks), shape, jnp.float32) * scale

    def unif(shape, lo, hi):
        return jax.random.uniform(next(ks), shape, jnp.float32, lo, hi)

    L, D = DEPTH, D_MODEL
    return {
        "x": nrm((BATCH, SEQ, D), 1.0),
        "mem": nrm((BATCH, N_MEM, D), 1.0),
        "attn_norm_g": 1.0 + nrm((L, D), 0.02),
        "w_in": nrm((L, D, N_IN), D ** -0.5),
        "diff_qk_g": 1.0 + nrm((L, 2, 2, DIFF_QK_DIM), 0.02),
        "diff_lambda": nrm((L, 4, DIFF_QK_DIM), 0.1),
        "diff_subln_g": 1.0 + nrm((L, DIFF_V_DIM), 0.02),
        "rwkv_mu": unif((L, 2, RWKV_IN), 0.0, 0.5),
        "rwkv_w0": unif((L, 2, RWKV_DIM), -5.5, 0.5),
        "rwkv_w2": nrm((L, 2, DECAY_LORA, RWKV_DIM), 0.1),
        "rwkv_a0": nrm((L, 2, RWKV_DIM), 0.1),
        "rwkv_a2": nrm((L, 2, AAA_LORA, RWKV_DIM), 0.1),
        "rwkv_g2": nrm((L, GATE_LORA, RWKV_DIM), GATE_LORA ** -0.5),
        "rwkv_k_k": 0.85 + nrm((L, RWKV_DIM), 0.02),
        "rwkv_k_a": 1.0 + nrm((L, RWKV_DIM), 0.02),
        "rwkv_r_k": nrm((L, RWKV_HEADS, RWKV_HEAD_DIM), 0.1),
        "rwkv_lnx_g": 1.0 + nrm((L, RWKV_DIM), 0.02),
        "rwkv_lnx_b": nrm((L, RWKV_DIM), 0.02),
        "mem_norm_g": 1.0 + nrm((L, D), 0.02),
        "w_mem_kv": nrm((L, D, 2 * MEM_DIM), D ** -0.5),
        "mem_qk_g": 1.0 + nrm((L, 2, MEM_HEAD_DIM), 0.02),
        "w_branch": nrm((L, N_BRANCH, BRANCH_DIM, D), BRANCH_DIM ** -0.5),
        "w_out": nrm((L, D, D), D ** -0.5),
        "ffn_norm_g": 1.0 + nrm((L, D), 0.02),
        "w_ffn_up": nrm((L, D, 2 * D_FF), D ** -0.5),
        "ffn_conv_w": nrm((L, CONV_WIDTH, D_FF), CONV_WIDTH ** -0.5),
        "ffn_conv_b": nrm((L, D_FF), 0.02),
        "w_ffn_down": nrm((L, D_FF, D), D_FF ** -0.5),
    }


def reference(x, mem, attn_norm_g, w_in, diff_qk_g, diff_lambda, diff_subln_g,
              rwkv_mu, rwkv_w0, rwkv_w2, rwkv_a0, rwkv_a2, rwkv_g2, rwkv_k_k, rwkv_k_a,
              rwkv_r_k, rwkv_lnx_g, rwkv_lnx_b, mem_norm_g, w_mem_kv, mem_qk_g,
              w_branch, w_out, ffn_norm_g, w_ffn_up, ffn_conv_w, ffn_conv_b, w_ffn_down):
    B, S, _ = x.shape
    splits = [3 * DIFF_DIM, 3 * DIFF_DIM + RWKV_IN, 3 * DIFF_DIM + RWKV_IN + MEM_DIM]
    for l in range(DEPTH):
        lam_init = 0.8 - 0.6 * math.exp(-0.3 * l)
        h = _rms(x, attn_norm_g[l])
        p = h @ w_in[l]
        p_diff, p_rwkv, p_mem, p_gate = jnp.split(p, splits, axis=-1)
        o_diff = _diff_branch(p_diff, diff_qk_g[l], diff_lambda[l], diff_subln_g[l], lam_init)
        o_rwkv = _rwkv_branch(p_rwkv, rwkv_mu[l], rwkv_w0[l], rwkv_w2[l], rwkv_a0[l], rwkv_a2[l],
                              rwkv_g2[l], rwkv_k_k[l], rwkv_k_a[l], rwkv_r_k[l],
                              rwkv_lnx_g[l], rwkv_lnx_b[l])
        o_mem = _mem_branch(p_mem, mem, mem_norm_g[l], w_mem_kv[l], mem_qk_g[l])
        o = jnp.stack([o_diff, o_rwkv, o_mem], axis=2)
        proj = jnp.einsum('bsnc,ncd->bsnd', o, w_branch[l])
        gate = jax.nn.sigmoid(p_gate.reshape(B, S, N_BRANCH, D_MODEL))
        x = x + jnp.sum(gate * proj, axis=2) @ w_out[l]
        h2 = _rms(x, ffn_norm_g[l])
        u_gate, u_val = jnp.split(h2 @ w_ffn_up[l], 2, axis=-1)
        cw = ffn_conv_w[l]
        u_gate = cw[0] * _shift_prev(u_gate) + cw[1] * u_gate + cw[2] * _shift_next(u_gate) + ffn_conv_b[l]
        x = x + (jax.nn.silu(u_gate) * u_val) @ w_ffn_down[l]
    return x
```

```python
import functools
import math

import jax
import jax.numpy as jnp
from jax import lax
from jax.experimental import pallas as pl
from jax.experimental.pallas import tpu as pltpu

F32 = jnp.float32
BF16 = jnp.bfloat16

D_MODEL = 2048
DEPTH = 2
DIFF_HEADS = 8
DIFF_QK_DIM = 64
DIFF_V_DIM = 128
DIFF_DIM = 1024
RWKV_HEADS = 16
RWKV_HEAD_DIM = 64
RWKV_DIM = 1024
DECAY_LORA = 64
AAA_LORA = 64
GATE_LORA = 160
LORA_COLS = 2 * DECAY_LORA + 2 * AAA_LORA + GATE_LORA
LORA_PAD = 512
N_MEM = 256
MEM_HEADS = 4
MEM_HEAD_DIM = 256
MEM_DIM = 1024
N_BRANCH = 3
D_FF = 5632
NORM_EPS = 1e-6
LNX_EPS = 64e-5

COL_DIFF = 0
COL_RKV = 3 * DIFF_DIM
COL_MEM = COL_RKV + 3 * RWKV_DIM
COL_GATE = COL_MEM + MEM_DIM
COL_LORA = COL_GATE + N_BRANCH * D_MODEL
N_IN_PAD = COL_LORA + LORA_PAD

CHUNK = 64
POS_PERIOD = 512
ATT_BQ = 512
ATT_BK = 256
V_AUG = 144
VMEM_LIMIT = 48 * 1024 * 1024


def _cparams(sem):
    return pltpu.CompilerParams(dimension_semantics=sem, vmem_limit_bytes=VMEM_LIMIT)


def _split3(x):
    hi = x.astype(BF16)
    r1 = x - hi.astype(F32)
    mid = r1.astype(BF16)
    lo = (r1 - mid.astype(F32)).astype(BF16)
    return hi, mid, lo


def _dot(a, b):
    return jnp.dot(a, b, preferred_element_type=F32)


def _dot_exactish(a_bf16_exact, x_f32):
    hi, mid, lo = _split3(x_f32)
    return _dot(a_bf16_exact, hi) + _dot(a_bf16_exact, mid) + _dot(a_bf16_exact, lo)


def _dot_exactish_r(x_f32, b_bf16_exact):
    hi, mid, lo = _split3(x_f32)
    return _dot(hi, b_bf16_exact) + _dot(mid, b_bf16_exact) + _dot(lo, b_bf16_exact)


def _rms_mm_kernel(x_ref, g_ref, w_ref, o_ref, h_ref):
    @pl.when(pl.program_id(1) == 0)
    def _():
        x = x_ref[...]
        ms = jnp.mean(x * x, axis=-1, keepdims=True)
        h_ref[...] = (x * lax.rsqrt(ms + NORM_EPS) * g_ref[...]).astype(BF16)

    o_ref[...] = _dot(h_ref[...], w_ref[...]).astype(o_ref.dtype)


def _rms_mm(x, g, w, *, tm, tn, name):
    m, k = x.shape
    n = w.shape[1]
    return pl.pallas_call(
        _rms_mm_kernel,
        grid=(m // tm, n // tn),
        in_specs=[
            pl.BlockSpec((tm, k), lambda i, j: (i, 0)),
            pl.BlockSpec((1, k), lambda i, j: (0, 0)),
            pl.BlockSpec((k, tn), lambda i, j: (0, j)),
        ],
        out_specs=pl.BlockSpec((tm, tn), lambda i, j: (i, j)),
        out_shape=jax.ShapeDtypeStruct((m, n), BF16),
        scratch_shapes=[pltpu.VMEM((tm, k), BF16)],
        compiler_params=_cparams(("parallel", "arbitrary")),
        name=name,
    )(x, g.reshape(1, k), w)


def _mm_res_kernel(a_ref, w_ref, r_ref, o_ref):
    o_ref[...] = r_ref[...] + _dot(a_ref[...], w_ref[...])


def _mm_res(a, w, res, *, tm, tn, name):
    m, k = a.shape
    n = w.shape[1]
    return pl.pallas_call(
        _mm_res_kernel,
        grid=(m // tm, n // tn),
        in_specs=[
            pl.BlockSpec((tm, k), lambda i, j: (i, 0)),
            pl.BlockSpec((k, tn), lambda i, j: (0, j)),
            pl.BlockSpec((tm, tn), lambda i, j: (i, j)),
        ],
        out_specs=pl.BlockSpec((tm, tn), lambda i, j: (i, j)),
        out_shape=jax.ShapeDtypeStruct((m, n), F32),
        compiler_params=_cparams(("parallel", "arbitrary")),
        name=name,
    )(a, w, res)


def _group_mean_sq(x, bd_ref, group):
    hi, mid, _ = _split3(x * x)
    bd = bd_ref[...]
    return (_dot(hi, bd) + _dot(mid, bd)) * (1.0 / group)


def _diff_prep_kernel(q_ref, k_ref, v_ref, gq_ref, gk_ref, bd_ref, qa_ref, ka_ref, vt_ref):
    tm = q_ref.shape[0]
    row0 = pl.program_id(0) * tm
    lane = lax.broadcasted_iota(jnp.int32, (tm, 128), 1)
    pos = (row0 + lax.broadcasted_iota(jnp.int32, (tm, 128), 0)) & (POS_PERIOD - 1)
    pos_lo = (pos & 255).astype(F32)
    pos_hi = (pos & 256).astype(F32)

    q = q_ref[...].astype(F32)
    qn = q * lax.rsqrt(_group_mean_sq(q, bd_ref, DIFF_QK_DIM) + NORM_EPS)
    qn = qn * gq_ref[...] * (DIFF_QK_DIM ** -0.5)
    k = k_ref[...].astype(F32)
    kn = k * lax.rsqrt(_group_mean_sq(k, bd_ref, DIFF_QK_DIM) + NORM_EPS)
    kn = kn * gk_ref[...]

    for h in range(DIFF_HEADS):
        slope = 2.0 ** (-(h + 1))
        qh = qn[:, 128 * h:128 * (h + 1)]
        kh = kn[:, 128 * h:128 * (h + 1)]
        for mp in range(2):
            a0 = 64 if mp == 0 else 0
            is_data = (lane < 64) if mp == 0 else (lane >= 64)
            aug_q = jnp.where(
                lane < a0 + 2, 1.0,
                jnp.where(lane == a0 + 2, -slope * pos_lo,
                          jnp.where(lane == a0 + 3, -slope * pos_hi, 0.0)))
            aug_k = jnp.where(
                lane == a0, slope * pos_lo,
                jnp.where(lane == a0 + 1, slope * pos_hi,
                          jnp.where(lane < a0 + 4, 1.0, 0.0)))
            in_aug = (lane >= a0) & (lane < a0 + 4)
            qa_ref[h, mp] = jnp.where(is_data, qh, jnp.where(in_aug, aug_q, 0.0)).astype(BF16)
            ka_ref[h, mp] = jnp.where(is_data, kh, jnp.where(in_aug, aug_k, 0.0)).astype(BF16)

    vt = v_ref[...].astype(F32).T
    sub = lax.broadcasted_iota(jnp.int32, (V_AUG - DIFF_V_DIM, tm), 0)
    ones_rows = jnp.where(sub == 0, 1.0, 0.0).astype(BF16)
    for h in range(DIFF_HEADS):
        vt_ref[h, 0:DIFF_V_DIM, :] = vt[128 * h:128 * (h + 1), :].astype(BF16)
        vt_ref[h, DIFF_V_DIM:V_AUG, :] = ones_rows


def _diff_prep(p, gq, gk, bd, *, tm):
    s = p.shape[0]
    nb = DIFF_DIM // 1024
    return pl.pallas_call(
        _diff_prep_kernel,
        grid=(s // tm,),
        in_specs=[
            pl.BlockSpec((tm, DIFF_DIM), lambda i: (i, 0)),
            pl.BlockSpec((tm, DIFF_DIM), lambda i: (i, 1)),
            pl.BlockSpec((tm, DIFF_DIM), lambda i: (i, 2)),
            pl.BlockSpec((1, DIFF_DIM), lambda i: (0, 0)),
            pl.BlockSpec((1, DIFF_DIM), lambda i: (0, 0)),
            pl.BlockSpec((DIFF_DIM, DIFF_DIM), lambda i: (0, 0)),
        ],
        out_specs=[
            pl.BlockSpec((DIFF_HEADS, 2, tm, 128), lambda i: (0, 0, i, 0)),
            pl.BlockSpec((DIFF_HEADS, 2, tm, 128), lambda i: (0, 0, i, 0)),
            pl.BlockSpec((DIFF_HEADS, V_AUG, tm), lambda i: (0, 0, i)),
        ],
        out_shape=[
            jax.ShapeDtypeStruct((DIFF_HEADS, 2, s, 128), BF16),
            jax.ShapeDtypeStruct((DIFF_HEADS, 2, s, 128), BF16),
            jax.ShapeDtypeStruct((DIFF_HEADS, V_AUG, s), BF16),
        ],
        compiler_params=_cparams(("parallel",)),
        name="diff_prep",
    )(p, p, p, gq, gk, bd)


def _diff_attn_kernel(lam_ref, sg_ref, qa_ref, ka_ref, vt_ref, o_ref, acc_ref, m_ref, *, lam_init):
    h = pl.program_id(0)
    i = pl.program_id(1)
    bq = qa_ref.shape[1]
    s_len = ka_ref.shape[1]
    bk = ATT_BK
    n_chunks = s_len // bk
    per_q = bq // bk
    slope = jnp.exp2(-jnp.full((1, 1), h + 1, jnp.int32).astype(F32))
    q0 = i * bq
    q_base = (q0 // POS_PERIOD) * POS_PERIOD

    acc_ref[...] = jnp.zeros_like(acc_ref)
    m_ref[...] = jnp.full_like(m_ref, -1e30)

    lane = lax.broadcasted_iota(jnp.int32, (1, 128), 1)
    aug_lane = ((lane >= 64) & (lane < 68), (lane >= 0) & (lane < 4))

    def accumulate(mp, s, vt_c, off):
        m_old = m_ref[mp]
        m_new = jnp.maximum(m_old, jnp.max(s, axis=0, keepdims=True) + off)
        p = jnp.exp(s - (m_new - off)).astype(BF16)
        alpha = jnp.exp(m_old - m_new)
        acc_ref[mp] = acc_ref[mp] * alpha + _dot(vt_c, p)
        m_ref[mp] = m_new

    def far_chunk(j, sign):
        k0 = pl.multiple_of(j * bk, bk)
        k_base = (k0 // POS_PERIOD) * POS_PERIOD
        off = -slope * jnp.abs(q_base - k_base).astype(F32)
        vt_c = vt_ref[:, pl.ds(k0, bk)]
        for mp in range(2):
            q = qa_ref[mp]
            if sign < 0:
                q = jnp.where(aug_lane[mp], -q, q)
            kc = ka_ref[mp, pl.ds(k0, bk), :]
            s = lax.dot_general(kc, q, (((1,), (1,)), ((), ())), preferred_element_type=F32)
            accumulate(mp, s, vt_c, off)

    def near_chunk(t):
        k0 = pl.multiple_of(q0 + t * bk, bk)
        vt_c = vt_ref[:, pl.ds(k0, bk)]
        kpos = k0 + lax.broadcasted_iota(jnp.int32, (bk, bq), 0)
        qpos = q0 + lax.broadcasted_iota(jnp.int32, (bk, bq), 1)
        bias = -slope * jnp.abs(qpos - kpos).astype(F32)
        for mp in range(2):
            q = qa_ref[mp]
            q = jnp.where(aug_lane[mp], jnp.zeros_like(q), q)
            kc = ka_ref[mp, pl.ds(k0, bk), :]
            s = lax.dot_general(kc, q, (((1,), (1,)), ((), ())), preferred_element_type=F32)
            accumulate(mp, s + bias, vt_c, 0.0)

    j_lo = i * per_q

    def before_body(j, c):
        far_chunk(j, +1)
        return c

    def after_body(j, c):
        far_chunk(j, -1)
        return c

    lax.fori_loop(0, j_lo, before_body, 0)
    for t in range(per_q):
        near_chunk(t)
    lax.fori_loop(j_lo + per_q, n_chunks, after_body, 0)

    lp = lam_ref[...]
    lam = (jnp.exp(jnp.sum(lp[0:1] * lp[1:2], axis=-1, keepdims=True))
           - jnp.exp(jnp.sum(lp[2:3] * lp[3:4], axis=-1, keepdims=True)) + lam_init)
    a0 = acc_ref[0]
    a1 = acc_ref[1]
    o = (a0[0:DIFF_V_DIM] / a0[DIFF_V_DIM:DIFF_V_DIM + 1]
         - lam * (a1[0:DIFF_V_DIM] / a1[DIFF_V_DIM:DIFF_V_DIM + 1]))
    ms = jnp.mean(o * o, axis=0, keepdims=True)
    o = o * lax.rsqrt(ms + NORM_EPS) * sg_ref[...] * (1.0 - lam_init)
    o_ref[...] = o.T.astype(o_ref.dtype)


def _diff_attn(qa, ka, vt, lam_p, subln_g, *, lam_init):
    s = qa.shape[2]
    bq = ATT_BQ
    kern = functools.partial(_diff_attn_kernel, lam_init=lam_init)
    return pl.pallas_call(
        kern,
        grid=(DIFF_HEADS, s // bq),
        in_specs=[
            pl.BlockSpec((4, DIFF_QK_DIM), lambda h, i: (0, 0)),
            pl.BlockSpec((DIFF_V_DIM, 1), lambda h, i: (0, 0)),
            pl.BlockSpec((None, 2, bq, 128), lambda h, i: (h, 0, i, 0)),
            pl.BlockSpec((None, 2, s, 128), lambda h, i: (h, 0, 0, 0)),
            pl.BlockSpec((None, V_AUG, s), lambda h, i: (h, 0, 0)),
        ],
        out_specs=pl.BlockSpec((bq, DIFF_V_DIM), lambda h, i: (i, h)),
        out_shape=jax.ShapeDtypeStruct((s, DIFF_DIM), BF16),
        scratch_shapes=[
            pltpu.VMEM((2, V_AUG, bq), F32),
            pltpu.VMEM((2, 1, bq), F32),
        ],
        compiler_params=_cparams(("parallel", "arbitrary")),
        name="diff_attn",
    )(lam_p, subln_g.reshape(DIFF_V_DIM, 1), qa, ka, vt)


def _shift_rows(x, prev_row, next_row):
    n = x.shape[0]
    row = lax.broadcasted_iota(jnp.int32, x.shape, 0)
    xp = jnp.where(row == 0, prev_row, pltpu.roll(x, 1, 0))
    xn = jnp.where(row == n - 1, next_row, pltpu.roll(x, n - 1, 0))
    return xp, xn


def _halo_rows(prev_ref, next_ref, i, n_blocks):
    hp = prev_ref.shape[0]
    prev_row = prev_ref[hp - 1:hp, :].astype(F32)
    next_row = next_ref[0:1, :].astype(F32)
    prev_row = jnp.where(i > 0, prev_row, 0.0)
    next_row = jnp.where(i < n_blocks - 1, next_row, 0.0)
    return prev_row, next_row


def _rwkv_prep_kernel(
        x_ref, xp_ref, xn_ref, l_ref, lp_ref, ln_ref,
        mu_ref, mul_ref, w0_ref, w2_ref, a0_ref, a2_ref, g2_ref, kk_ref, ka_ref, rk_ref,
        bd_ref, tri_ref, triu_ref,
        v_out, g_out, bonus_out, rb_out, kb_out, kt_out, bt_out, kh_out, bh_out, gc_out):
    i = pl.program_id(0)
    nb = pl.num_programs(0)
    tm = x_ref.shape[0]

    x = x_ref[...].astype(F32)
    prev_row, next_row = _halo_rows(xp_ref, xn_ref, i, nb)
    xp, xn = _shift_rows(x, prev_row, next_row)
    mu = mu_ref[...]
    x = x + mu[0:1] * (xp - x) + mu[1:2] * (xn - x)
    lo = l_ref[...].astype(F32)
    prev_row, next_row = _halo_rows(lp_ref, ln_ref, i, nb)
    lop, lon = _shift_rows(lo, prev_row, next_row)
    mul = mul_ref[...]
    lo = lo + mul[0:1] * (lop - lo) + mul[1:2] * (lon - lo)

    r = x[:, 0:RWKV_DIM]
    k = x[:, RWKV_DIM:2 * RWKV_DIM]
    v = x[:, 2 * RWKV_DIM:3 * RWKV_DIM]
    tw = jnp.tanh(lo[:, 0:128]).astype(BF16)
    la = lo[:, 128:256].astype(BF16)
    lg = jax.nn.sigmoid(lo[:, 256:512]).astype(BF16)

    g = _dot(lg, g2_ref[...])
    kk = k * kk_ref[...]
    ss = _group_mean_sq(kk, bd_ref, 1.0)
    kk = kk * lax.rsqrt(jnp.maximum(ss, 1e-24))
    hi, mid, _ = _split3(r * k * rk_ref[...])
    bonus = (_dot(hi, bd_ref[...]) + _dot(mid, bd_ref[...])) * v

    g_out[...] = g.astype(g_out.dtype)
    bonus_out[...] = bonus.astype(bonus_out.dtype)
    for h in range(RWKV_HEADS):
        v_out[h] = v[:, 64 * h:64 * (h + 1)].astype(BF16)

    tri = tri_ref[...]
    triu = triu_ref[...]
    for d in range(2):
        wl = w0_ref[d:d + 1, :] + _dot(tw, w2_ref[d])
        z = -wl
        softplus = jnp.maximum(z, 0.0) + jnp.log(1.0 + jnp.exp(-jnp.abs(z)))
        logdec = -jnp.exp(-softplus - 0.5)
        a = jax.nn.sigmoid(a0_ref[d:d + 1, :] + _dot(la, a2_ref[d]))
        k_d = k * (1.0 + (a - 1.0) * ka_ref[...])
        b_d = kk * a
        pre = _dot_exactish(tri, logdec)
        suf = _dot_exactish(triu, logdec)
        if d == 0:
            lc, ex, rem = pre, pre - logdec, suf
        else:
            lc, ex, rem = suf + logdec, suf, pre - logdec
        e_neg = jnp.exp(-lc)
        e_rem = jnp.exp(rem)
        outs = (
            (rb_out, r * jnp.exp(lc)),
            (kb_out, kk * jnp.exp(ex)),
            (kt_out, k_d * e_neg),
            (bt_out, b_d * e_neg),
            (kh_out, k_d * e_rem),
            (bh_out, b_d * e_rem),
        )
        for ref, val in outs:
            val = val.astype(BF16)
            for h in range(RWKV_HEADS):
                ref[d, h] = val[:, 64 * h:64 * (h + 1)]
        tot = jnp.exp(lc + rem).reshape(tm // 8, 8, RWKV_DIM)[:, 0, :]
        for h in range(RWKV_HEADS):
            gc_out[d, h] = tot[:, 64 * h:64 * (h + 1)]


def _rwkv_prep(p, mu_rkv, mu_lora, w0, w2p, a0, a2p, g2p, k_k, k_a, r_k, bd, tri, triu, *, tm):
    s = p.shape[0]
    nb = s // tm
    hb = tm // 16
    last16 = s // 16 - 1
    rkv_w = 3 * RWKV_DIM
    c_rkv = COL_RKV // rkv_w
    c_lora = COL_LORA // LORA_PAD

    def prev_map(c):
        return lambda i: (jnp.maximum(i * hb - 1, 0), c)

    def next_map(c):
        return lambda i: (jnp.minimum((i + 1) * hb, last16), c)

    full = lambda *shape: pl.BlockSpec(shape, lambda i: (0,) * len(shape))
    hm = lambda: pl.BlockSpec((2, RWKV_HEADS, tm, 64), lambda i: (0, 0, i, 0))
    hm_shape = jax.ShapeDtypeStruct((2, RWKV_HEADS, s, 64), BF16)
    return pl.pallas_call(
        _rwkv_prep_kernel,
        grid=(nb,),
        in_specs=[
            pl.BlockSpec((tm, rkv_w), lambda i: (i, c_rkv)),
            pl.BlockSpec((16, rkv_w), prev_map(c_rkv)),
            pl.BlockSpec((16, rkv_w), next_map(c_rkv)),
            pl.BlockSpec((tm, LORA_PAD), lambda i: (i, c_lora)),
            pl.BlockSpec((16, LORA_PAD), prev_map(c_lora)),
            pl.BlockSpec((16, LORA_PAD), next_map(c_lora)),
            full(2, rkv_w), full(2, LORA_PAD),
            full(2, RWKV_DIM), full(2, 128, RWKV_DIM),
            full(2, RWKV_DIM), full(2, 128, RWKV_DIM),
            full(256, RWKV_DIM),
            full(1, RWKV_DIM), full(1, RWKV_DIM), full(1, RWKV_DIM),
            full(RWKV_DIM, RWKV_DIM), full(tm, tm), full(tm, tm),
        ],
        out_specs=[
            pl.BlockSpec((RWKV_HEADS, tm, 64), lambda i: (0, i, 0)),
            pl.BlockSpec((tm, RWKV_DIM), lambda i: (i, 0)),
            pl.BlockSpec((tm, RWKV_DIM), lambda i: (i, 0)),
            hm(), hm(), hm(), hm(), hm(), hm(),
            pl.BlockSpec((2, RWKV_HEADS, tm // 8, 64), lambda i: (0, 0, i, 0)),
        ],
        out_shape=[
            jax.ShapeDtypeStruct((RWKV_HEADS, s, 64), BF16),
            jax.ShapeDtypeStruct((s, RWKV_DIM), BF16),
            jax.ShapeDtypeStruct((s, RWKV_DIM), BF16),
            hm_shape, hm_shape, hm_shape, hm_shape, hm_shape, hm_shape,
            jax.ShapeDtypeStruct((2, RWKV_HEADS, s // 8, 64), F32),
        ],
        compiler_params=_cparams(("parallel",)),
        name="rwkv_prep",
    )(p, p, p, p, p, p, mu_rkv, mu_lora, w0, w2p, a0, a2p, g2p, k_k, k_a, r_k, bd, tri, triu)


def _bdot(a, b):
    return lax.dot_general(a, b, (((2,), (1,)), ((0,), (0,))), preferred_element_type=F32)


def _bdot_nt(a, b):
    return lax.dot_general(a, b, (((2,), (2,)), ((0,), (0,))), preferred_element_type=F32)


def _bdot_tn(a, b):
    return lax.dot_general(a, b, (((1,), (1,)), ((0,), (0,))), preferred_element_type=F32)


def _bdot_hi(a, b):
    return lax.dot_general(a, b, (((2,), (1,)), ((0,), (0,))), preferred_element_type=F32,
                           precision=lax.Precision.HIGHEST)


def _wkv_kernel(v_ref, rb_ref, kb_ref, kt_ref, bt_ref, kh_ref, bh_ref, gc_ref, y_ref, s_ref):
    d = pl.program_id(0)
    c = pl.program_id(1)

    @pl.when(c == 0)
    def _():
        s_ref[...] = jnp.zeros_like(s_ref)

    nh = RWKV_HEADS
    row = lax.broadcasted_iota(jnp.int32, (nh, CHUNK, CHUNK), 1)
    col = lax.broadcasted_iota(jnp.int32, (nh, CHUNK, CHUNK), 2)
    ahead = jnp.where(d == 0, row - col, col - row)
    strict = ahead > 0
    incl = ahead >= 0

    v = v_ref[...]
    rb, kb, kt, bt = rb_ref[...], kb_ref[...], kt_ref[...], bt_ref[...]
    kh, bh = kh_ref[...], bh_ref[...]

    a_kk = jnp.where(strict, _bdot_nt(kb, kt), 0.0)
    l_mat = jnp.where(strict, _bdot_nt(kb, bt), 0.0)
    a_rk = jnp.where(incl, _bdot_nt(rb, kt), 0.0)
    a_rb = jnp.where(incl, _bdot_nt(rb, bt), 0.0)

    eye = jnp.where(row == col, 1.0, 0.0)
    t_inv = eye - l_mat
    pw = _bdot_hi(l_mat, l_mat)
    for step in range(5):
        t_inv = t_inv + _bdot_hi(t_inv, pw)
        if step < 4:
            pw = _bdot_hi(pw, pw)

    q1 = _bdot(a_kk.astype(BF16), v)
    q2 = _bdot(a_rk.astype(BF16), v)
    t_b = t_inv.astype(BF16)
    wk = _bdot(t_b, kb)
    uv = _bdot(t_b, q1.astype(BF16))
    a_rb_b = a_rb.astype(BF16)
    wk_b = wk.astype(BF16)
    uv_b = uv.astype(BF16)
    rw = rb.astype(F32) - _bdot(a_rb_b, wk_b)
    y0 = q2 - _bdot(a_rb_b, uv_b)
    m_mat = _bdot_tn(wk_b, bh)
    j_t = _bdot_tn(v, kh) - _bdot_tn(uv_b, bh)

    s_old = s_ref[...]
    y = lax.dot_general(rw, s_old, (((2,), (2,)), ((0,), (0,))), preferred_element_type=F32,
                        precision=lax.Precision.HIGHEST) + y0
    s_ref[...] = s_old * gc_ref[:, 0:1, :] - _bdot_hi(s_old, m_mat) + j_t
    y_ref[...] = y


def _wkv(v, rb, kb, kt, bt, kh, bh, gc):
    s = v.shape[1]
    nc = s // CHUNK

    def cidx(d, c):
        return jnp.where(d == 0, c, nc - 1 - c)

    dir_spec = lambda: pl.BlockSpec((None, RWKV_HEADS, CHUNK, 64), lambda d, c: (d, 0, cidx(d, c), 0))
    return pl.pallas_call(
        _wkv_kernel,
        grid=(2, nc),
        in_specs=[
            pl.BlockSpec((RWKV_HEADS, CHUNK, 64), lambda d, c: (0, cidx(d, c), 0)),
            dir_spec(), dir_spec(), dir_spec(), dir_spec(), dir_spec(), dir_spec(),
            pl.BlockSpec((None, RWKV_HEADS, 8, 64), lambda d, c: (d, 0, cidx(d, c), 0)),
        ],
        out_specs=dir_spec(),
        out_shape=jax.ShapeDtypeStruct((2, RWKV_HEADS, s, 64), F32),
        scratch_shapes=[pltpu.VMEM((RWKV_HEADS, 64, 64), F32)],
        compiler_params=_cparams(("arbitrary", "arbitrary")),
        name="wkv7_chunked",
    )(v, rb, kb, kt, bt, kh, bh, gc)


def _rwkv_post_kernel(y_ref, g_ref, bonus_ref, lg_ref, lb_ref, o_ref):
    y = y_ref[0] + y_ref[1]
    mean = jnp.mean(y, axis=-1, keepdims=True)
    yc = y - mean
    var = jnp.mean(yc * yc, axis=-1, keepdims=True)
    yn = yc * lax.rsqrt(var + LNX_EPS)
    yt = jnp.concatenate([yn[h] for h in range(RWKV_HEADS)], axis=-1)
    out = (yt * lg_ref[...] + lb_ref[...] + bonus_ref[...].astype(F32)) * g_ref[...].astype(F32)
    o_ref[...] = out.astype(o_ref.dtype)


def _rwkv_post(y, g, bonus, lnx_g, lnx_b, *, tm):
    s = g.shape[0]
    return pl.pallas_call(
        _rwkv_post_kernel,
        grid=(s // tm,),
        in_specs=[
            pl.BlockSpec((2, RWKV_HEADS, tm, 64), lambda i: (0, 0, i, 0)),
            pl.BlockSpec((tm, RWKV_DIM), lambda i: (i, 0)),
            pl.BlockSpec((tm, RWKV_DIM), lambda i: (i, 0)),
            pl.BlockSpec((1, RWKV_DIM), lambda i: (0, 0)),
            pl.BlockSpec((1, RWKV_DIM), lambda i: (0, 0)),
        ],
        out_specs=pl.BlockSpec((tm, RWKV_DIM), lambda i: (i, 0)),
        out_shape=jax.ShapeDtypeStruct((s, RWKV_DIM), BF16),
        compiler_params=_cparams(("parallel",)),
        name="rwkv_post",
    )(y, g, bonus, lnx_g.reshape(1, RWKV_DIM), lnx_b.reshape(1, RWKV_DIM))


def _mem_attn_kernel(q_ref, kv_ref, gq_ref, gk_ref, o_ref):
    for h in range(MEM_HEADS):
        sl = slice(MEM_HEAD_DIM * h, MEM_HEAD_DIM * (h + 1))
        q = q_ref[:, sl].astype(F32)
        q = q * lax.rsqrt(jnp.mean(q * q, axis=-1, keepdims=True) + NORM_EPS)
        q = q * gq_ref[...] * (MEM_HEAD_DIM ** -0.5)
        km = kv_ref[:, sl].astype(F32)
        km = km * lax.rsqrt(jnp.mean(km * km, axis=-1, keepdims=True) + NORM_EPS)
        km = km * gk_ref[...]
        vm = kv_ref[:, MEM_DIM + MEM_HEAD_DIM * h:MEM_DIM + MEM_HEAD_DIM * (h + 1)]
        s = lax.dot_general(q.astype(BF16), km.astype(BF16), (((1,), (1,)), ((), ())),
                            preferred_element_type=F32)
        s = s - jnp.max(s, axis=-1, keepdims=True)
        e = jnp.exp(s)
        pr = e / jnp.sum(e, axis=-1, keepdims=True)
        o_ref[:, sl] = _dot(pr.astype(BF16), vm).astype(o_ref.dtype)


def _mem_attn(p, kv, gq, gk, *, tm):
    s = p.shape[0]
    return pl.pallas_call(
        _mem_attn_kernel,
        grid=(s // tm,),
        in_specs=[
            pl.BlockSpec((tm, MEM_DIM), lambda i: (i, COL_MEM // MEM_DIM)),
            pl.BlockSpec((N_MEM, 2 * MEM_DIM), lambda i: (0, 0)),
            pl.BlockSpec((1, MEM_HEAD_DIM), lambda i: (0, 0)),
            pl.BlockSpec((1, MEM_HEAD_DIM), lambda i: (0, 0)),
        ],
        out_specs=pl.BlockSpec((tm, MEM_DIM), lambda i: (i, 0)),
        out_shape=jax.ShapeDtypeStruct((s, MEM_DIM), BF16),
        compiler_params=_cparams(("parallel",)),
        name="mem_attn",
    )(p, kv, gq, gk)


def _merge_kernel(o0_ref, o1_ref, o2_ref, g0_ref, g1_ref, g2_ref, w_ref, m_ref):
    acc = jax.nn.sigmoid(g0_ref[...].astype(F32)) * _dot(o0_ref[...], w_ref[0])
    acc = acc + jax.nn.sigmoid(g1_ref[...].astype(F32)) * _dot(o1_ref[...], w_ref[1])
    acc = acc + jax.nn.sigmoid(g2_ref[...].astype(F32)) * _dot(o2_ref[...], w_ref[2])
    m_ref[...] = acc.astype(m_ref.dtype)


def _merge(o_diff, o_rwkv, o_mem, p, w_branch, *, tm, tn):
    s = p.shape[0]
    gate_blk = COL_GATE // tn
    per = D_MODEL // tn
    o_spec = lambda: pl.BlockSpec((tm, 1024), lambda i, j: (i, 0))
    g_spec = lambda b: pl.BlockSpec((tm, tn), lambda i, j: (i, gate_blk + b * per + j))
    return pl.pallas_call(
        _merge_kernel,
        grid=(s // tm, per),
        in_specs=[
            o_spec(), o_spec(), o_spec(),
            g_spec(0), g_spec(1), g_spec(2),
            pl.BlockSpec((N_BRANCH, 1024, tn), lambda i, j: (0, 0, j)),
        ],
        out_specs=pl.BlockSpec((tm, tn), lambda i, j: (i, j)),
        out_shape=jax.ShapeDtypeStruct((s, D_MODEL), BF16),
        compiler_params=_cparams(("parallel", "arbitrary")),
        name="merge",
    )(o_diff, o_rwkv, o_mem, p, p, p, w_branch)


def _glu_kernel(ug_ref, ugp_ref, ugn_ref, uv_ref, cw_ref, cb_ref, o_ref):
    i = pl.program_id(0)
    nb = pl.num_programs(0)
    g = ug_ref[...].astype(F32)
    prev_row, next_row = _halo_rows(ugp_ref, ugn_ref, i, nb)
    gp, gn = _shift_rows(g, prev_row, next_row)
    cw = cw_ref[...]
    conv = cw[0:1] * gp + cw[1:2] * g + cw[2:3] * gn + cb_ref[...]
    act = conv * jax.nn.sigmoid(conv) * uv_ref[...].astype(F32)
    o_ref[...] = act.astype(o_ref.dtype)


def _glu(u, conv_w, conv_b, *, tm, tc):
    s = u.shape[0]
    nj = D_FF // tc
    hb = tm // 16
    last16 = s // 16 - 1
    return pl.pallas_call(
        _glu_kernel,
        grid=(s // tm, nj),
        in_specs=[
            pl.BlockSpec((tm, tc), lambda i, j: (i, j)),
            pl.BlockSpec((16, tc), lambda i, j: (jnp.maximum(i * hb - 1, 0), j)),
            pl.BlockSpec((16, tc), lambda i, j: (jnp.minimum((i + 1) * hb, last16), j)),
            pl.BlockSpec((tm, tc), lambda i, j: (i, nj + j)),
            pl.BlockSpec((3, tc), lambda i, j: (0, j)),
            pl.BlockSpec((1, tc), lambda i, j: (0, j)),
        ],
        out_specs=pl.BlockSpec((tm, tc), lambda i, j: (i, j)),
        out_shape=jax.ShapeDtypeStruct((s, D_FF), BF16),
        compiler_params=_cparams(("parallel", "parallel")),
        name="conv_glu",
    )(u, u, u, u, conv_w, conv_b.reshape(1, D_FF))


def _block_ones(n, group):
    idx = jnp.arange(n) // group
    return (idx[:, None] == idx[None, :]).astype(BF16)


def _chunk_tri(n):
    idx = jnp.arange(n)
    same = (idx[:, None] // CHUNK) == (idx[None, :] // CHUNK)
    lower_incl = same & (idx[:, None] >= idx[None, :])
    upper_strict = same & (idx[:, None] < idx[None, :])
    return lower_incl.astype(BF16), upper_strict.astype(BF16)


def _pad_rows(w, rows_before, total):
    n = w.shape[-1]
    out = jnp.zeros((total, n), w.dtype)
    return lax.dynamic_update_slice(out, w, (rows_before, 0))


def _pad_w_in(w):
    lora0 = COL_RKV + 3 * RWKV_DIM
    lora1 = lora0 + LORA_COLS
    pad = jnp.zeros((w.shape[0], LORA_PAD - LORA_COLS), w.dtype)
    return jnp.concatenate([w[:, :lora0], w[:, lora1:], w[:, lora0:lora1], pad], axis=1).astype(BF16)


def kernel(x, mem, attn_norm_g, w_in, diff_qk_g, diff_lambda, diff_subln_g, rwkv_mu, rwkv_w0,
           rwkv_w2, rwkv_a0, rwkv_a2, rwkv_g2, rwkv_k_k, rwkv_k_a, rwkv_r_k, rwkv_lnx_g,
           rwkv_lnx_b, mem_norm_g, w_mem_kv, mem_qk_g, w_branch, w_out, ffn_norm_g, w_ffn_up,
           ffn_conv_w, ffn_conv_b, w_ffn_down):
    b, s, d = x.shape
    assert b == 1 and d == D_MODEL and s % ATT_BQ == 0
    xs = x.reshape(s, d)
    mem2 = mem.reshape(N_MEM, d)
    prep_tm = 256
    bd64 = _block_ones(1024, 64)
    tri, triu = _chunk_tri(prep_tm)
    n_rwkv_main = 3 * RWKV_DIM

    for l in range(DEPTH):
        lam_init = 0.8 - 0.6 * math.exp(-0.3 * l)
        p = _rms_mm(xs, attn_norm_g[l], _pad_w_in(w_in[l]), tm=512, tn=1536, name="rms_w_in")

        gq = jnp.tile(diff_qk_g[l, 0].reshape(1, 128), (1, DIFF_HEADS))
        gk = jnp.tile(diff_qk_g[l, 1].reshape(1, 128), (1, DIFF_HEADS))
        qa, ka, vt = _diff_prep(p, gq, gk, bd64, tm=256)
        o_diff = _diff_attn(qa, ka, vt, diff_lambda[l], diff_subln_g[l], lam_init=lam_init)

        mu = rwkv_mu[l]
        mu_rkv = mu[:, :n_rwkv_main]
        mu_lora = jnp.pad(mu[:, n_rwkv_main:], ((0, 0), (0, LORA_PAD - LORA_COLS)))
        w2p = jnp.stack([_pad_rows(rwkv_w2[l, dd], 64 * dd, 128) for dd in range(2)]).astype(BF16)
        a2p = jnp.stack([_pad_rows(rwkv_a2[l, dd], 64 * dd, 128) for dd in range(2)]).astype(BF16)
        g2p = _pad_rows(rwkv_g2[l], 0, 256).astype(BF16)
        (v_h, g_tok, bonus, rb, kb, kt, bt, kh, bh, gc) = _rwkv_prep(
            p, mu_rkv, mu_lora, rwkv_w0[l], w2p, rwkv_a0[l], a2p, g2p,
            rwkv_k_k[l].reshape(1, RWKV_DIM), rwkv_k_a[l].reshape(1, RWKV_DIM),
            rwkv_r_k[l].reshape(1, RWKV_DIM), bd64, tri, triu, tm=prep_tm)
        y = _wkv(v_h, rb, kb, kt, bt, kh, bh, gc)
        o_rwkv = _rwkv_post(y, g_tok, bonus, rwkv_lnx_g[l], rwkv_lnx_b[l], tm=256)

        kv = _rms_mm(mem2, mem_norm_g[l], w_mem_kv[l].astype(BF16), tm=N_MEM, tn=1024,
                     name="rms_mem_kv")
        o_mem = _mem_attn(p, kv, mem_qk_g[l, 0].reshape(1, MEM_HEAD_DIM),
                          mem_qk_g[l, 1].reshape(1, MEM_HEAD_DIM), tm=512)

        merged = _merge(o_diff, o_rwkv, o_mem, p, w_branch[l].astype(BF16), tm=512, tn=512)
        xs = _mm_res(merged, w_out[l].astype(BF16), xs, tm=512, tn=1024, name="w_out_res")

        u = _rms_mm(xs, ffn_norm_g[l], w_ffn_up[l].astype(BF16), tm=512, tn=1024, name="rms_ffn_up")
        act = _glu(u, ffn_conv_w[l], ffn_conv_b[l], tm=512, tc=512)
        xs = _mm_res(act, w_ffn_down[l].astype(BF16), xs, tm=512, tn=512, name="ffn_down_res")

    return xs.reshape(b, s, d)
```

```python
import functools
import math

import jax
import jax.numpy as jnp
import numpy as np
from jax import lax
from jax.experimental import pallas as pl
from jax.experimental.pallas import tpu as pltpu

F32 = jnp.float32
BF16 = jnp.bfloat16

D_MODEL = 2048
DEPTH = 2
DIFF_HEADS = 8
DIFF_QK_DIM = 64
DIFF_V_DIM = 128
DIFF_DIM = 1024
RWKV_HEADS = 16
RWKV_HEAD_DIM = 64
RWKV_DIM = 1024
DECAY_LORA = 64
AAA_LORA = 64
GATE_LORA = 160
LORA_COLS = 2 * DECAY_LORA + 2 * AAA_LORA + GATE_LORA
LORA_PAD = 512
N_MEM = 256
MEM_HEADS = 4
MEM_HEAD_DIM = 256
MEM_DIM = 1024
N_BRANCH = 3
D_FF = 5632
NORM_EPS = 1e-6
LNX_EPS = 64e-5

COL_DIFF = 0
COL_RKV = 3 * DIFF_DIM
COL_MEM = COL_RKV + 3 * RWKV_DIM
COL_GATE = COL_MEM + MEM_DIM
COL_LORA = COL_GATE + N_BRANCH * D_MODEL
N_IN_PAD = COL_LORA + LORA_PAD

CHUNK = 64
LOG2E = 1.4426950408889634
N_POS_COLS = 12
SCORE_BOUND_MAX = 30.0
ATT_BQ = 512
ATT_BK = 256
FAR_UNROLL = 4
V_AUG = 144
VMEM_LIMIT = 48 * 1024 * 1024


def _cparams(sem):
    return pltpu.CompilerParams(dimension_semantics=sem, vmem_limit_bytes=VMEM_LIMIT)


def _split3(x):
    hi = x.astype(BF16)
    r1 = x - hi.astype(F32)
    mid = r1.astype(BF16)
    lo = (r1 - mid.astype(F32)).astype(BF16)
    return hi, mid, lo


def _dot(a, b):
    return jnp.dot(a, b, preferred_element_type=F32)


def _dot_exactish(a_bf16_exact, x_f32):
    hi, mid, lo = _split3(x_f32)
    return _dot(a_bf16_exact, hi) + _dot(a_bf16_exact, mid) + _dot(a_bf16_exact, lo)


def _dot_exactish_r(x_f32, b_bf16_exact):
    hi, mid, lo = _split3(x_f32)
    return _dot(hi, b_bf16_exact) + _dot(mid, b_bf16_exact) + _dot(lo, b_bf16_exact)


def _rms_mm_kernel(x_ref, g_ref, w_ref, o_ref, h_ref):
    @pl.when(pl.program_id(1) == 0)
    def _():
        x = x_ref[...]
        ms = jnp.mean(x * x, axis=-1, keepdims=True)
        h_ref[...] = (x * lax.rsqrt(ms + NORM_EPS) * g_ref[...]).astype(BF16)

    o_ref[...] = _dot(h_ref[...], w_ref[...]).astype(o_ref.dtype)


def _rms_mm(x, g, w, *, tm, tn, name):
    m, k = x.shape
    n = w.shape[1]
    return pl.pallas_call(
        _rms_mm_kernel,
        grid=(m // tm, n // tn),
        in_specs=[
            pl.BlockSpec((tm, k), lambda i, j: (i, 0)),
            pl.BlockSpec((1, k), lambda i, j: (0, 0)),
            pl.BlockSpec((k, tn), lambda i, j: (0, j)),
        ],
        out_specs=pl.BlockSpec((tm, tn), lambda i, j: (i, j)),
        out_shape=jax.ShapeDtypeStruct((m, n), BF16),
        scratch_shapes=[pltpu.VMEM((tm, k), BF16)],
        compiler_params=_cparams(("parallel", "arbitrary")),
        name=name,
    )(x, g.reshape(1, k), w)


def _mm_res_kernel(a_ref, w_ref, r_ref, o_ref):
    o_ref[...] = r_ref[...] + _dot(a_ref[...], w_ref[...])


def _mm_res(a, w, res, *, tm, tn, name):
    m, k = a.shape
    n = w.shape[1]
    return pl.pallas_call(
        _mm_res_kernel,
        grid=(m // tm, n // tn),
        in_specs=[
            pl.BlockSpec((tm, k), lambda i, j: (i, 0)),
            pl.BlockSpec((k, tn), lambda i, j: (0, j)),
            pl.BlockSpec((tm, tn), lambda i, j: (i, j)),
        ],
        out_specs=pl.BlockSpec((tm, tn), lambda i, j: (i, j)),
        out_shape=jax.ShapeDtypeStruct((m, n), F32),
        compiler_params=_cparams(("parallel", "arbitrary")),
        name=name,
    )(a, w, res)


def _group_mean_sq(x, bd_ref, group):
    hi, mid, _ = _split3(x * x)
    bd = bd_ref[...]
    return (_dot(hi, bd) + _dot(mid, bd)) * (1.0 / group)


def _aug_base(mp):
    return 64 if mp == 0 else 0


def _diff_prep_kernel(q_ref, k_ref, v_ref, gq_ref, gk_ref, qc_ref, kc_ref, bd_ref,
                      qa_ref, ka_ref, vt_ref):
    tm = q_ref.shape[0]
    row0 = pl.program_id(0) * tm
    lane = lax.broadcasted_iota(jnp.int32, (tm, 128), 1)
    pos = row0 + lax.broadcasted_iota(jnp.int32, (tm, 128), 0)
    pos_lo = (pos & 127).astype(F32)
    pos_hi = (pos >> 7).astype(F32)

    q = q_ref[...].astype(F32)
    qn = q * lax.rsqrt(_group_mean_sq(q, bd_ref, DIFF_QK_DIM) + NORM_EPS)
    qn = qn * gq_ref[...] * (DIFF_QK_DIM ** -0.5 * LOG2E)
    k = k_ref[...].astype(F32)
    kn = k * lax.rsqrt(_group_mean_sq(k, bd_ref, DIFF_QK_DIM) + NORM_EPS)
    kn = kn * gk_ref[...]

    for h in range(DIFF_HEADS):
        qh = qn[:, 128 * h:128 * (h + 1)]
        kh = kn[:, 128 * h:128 * (h + 1)]
        for mp in range(2):
            a0 = _aug_base(mp)
            g = 2 * h + mp
            is_data = (lane < 64) if mp == 0 else (lane >= 64)
            aug_q = jnp.where(lane < a0 + 3, pos_lo,
                              jnp.where(lane < a0 + 6, pos_hi, qc_ref[g:g + 1, :]))
            aug_k = jnp.where((lane >= a0 + 6) & (lane < a0 + 9), pos_lo,
                              jnp.where((lane >= a0 + 9) & (lane < a0 + 12), pos_hi,
                                        kc_ref[g:g + 1, :]))
            qa_ref[h, mp] = jnp.where(is_data, qh, aug_q).astype(BF16)
            ka_ref[h, mp] = jnp.where(is_data, kh, aug_k).astype(BF16)

    vt = v_ref[...].astype(F32).T
    sub = lax.broadcasted_iota(jnp.int32, (V_AUG - DIFF_V_DIM, tm), 0)
    ones_rows = jnp.where(sub == 0, 1.0, 0.0).astype(BF16)
    for h in range(DIFF_HEADS):
        vt_ref[h, 0:DIFF_V_DIM, :] = vt[128 * h:128 * (h + 1), :].astype(BF16)
        vt_ref[h, DIFF_V_DIM:V_AUG, :] = ones_rows


def _bf16_split3_const(x):
    parts = []
    for _ in range(3):
        part = float(np.asarray(x, np.float32).astype(BF16).astype(np.float32))
        parts.append(part)
        x = x - part
    return parts


def _attn_consts(qk_g):
    l_parts = _bf16_split3_const(LOG2E)
    m_nat = 8.0 * jnp.max(jnp.abs(qk_g[0]), axis=-1) * jnp.max(jnp.abs(qk_g[1]), axis=-1)
    qc = jnp.zeros((2 * DIFF_HEADS, 128), F32)
    kc = jnp.zeros((2 * DIFF_HEADS, 128), F32)
    for h in range(DIFF_HEADS):
        slope = 2.0 ** (-(h + 1))
        for mp in range(2):
            a0 = _aug_base(mp)
            g = 2 * h + mp
            for t, lp in enumerate(l_parts):
                qc = qc.at[g, a0 + 6 + t].set(slope * lp)
                qc = qc.at[g, a0 + 9 + t].set(128.0 * slope * lp)
                kc = kc.at[g, a0 + t].set(-slope * lp)
                kc = kc.at[g, a0 + 3 + t].set(-128.0 * slope * lp)
            qc = qc.at[g, a0 + 12].set(-m_nat[mp] * LOG2E)
            kc = kc.at[g, a0 + 12].set(1.0)
    return qc, kc, jnp.max(m_nat)


def _diff_prep(p, gq, gk, qc, kc, bd, *, tm):
    s = p.shape[0]
    return pl.pallas_call(
        _diff_prep_kernel,
        grid=(s // tm,),
        in_specs=[
            pl.BlockSpec((tm, DIFF_DIM), lambda i: (i, 0)),
            pl.BlockSpec((tm, DIFF_DIM), lambda i: (i, 1)),
            pl.BlockSpec((tm, DIFF_DIM), lambda i: (i, 2)),
            pl.BlockSpec((1, DIFF_DIM), lambda i: (0, 0)),
            pl.BlockSpec((1, DIFF_DIM), lambda i: (0, 0)),
            pl.BlockSpec((2 * DIFF_HEADS, 128), lambda i: (0, 0)),
            pl.BlockSpec((2 * DIFF_HEADS, 128), lambda i: (0, 0)),
            pl.BlockSpec((DIFF_DIM, DIFF_DIM), lambda i: (0, 0)),
        ],
        out_specs=[
            pl.BlockSpec((DIFF_HEADS, 2, tm, 128), lambda i: (0, 0, i, 0)),
            pl.BlockSpec((DIFF_HEADS, 2, tm, 128), lambda i: (0, 0, i, 0)),
            pl.BlockSpec((DIFF_HEADS, V_AUG, tm), lambda i: (0, 0, i)),
        ],
        out_shape=[
            jax.ShapeDtypeStruct((DIFF_HEADS, 2, s, 128), BF16),
            jax.ShapeDtypeStruct((DIFF_HEADS, 2, s, 128), BF16),
            jax.ShapeDtypeStruct((DIFF_HEADS, V_AUG, s), BF16),
        ],
        compiler_params=_cparams(("parallel",)),
        name="diff_prep",
    )(p, p, p, gq, gk, qc, kc, bd)


def _diff_attn_kernel(lam_ref, sg_ref, qa_ref, ka_ref, vt_ref, o_ref,
                      acc_ref, qv_ref, *mode_refs, lam_init, online):
    h = pl.program_id(0)
    i = pl.program_id(1)
    bq = qa_ref.shape[1]
    s_len = ka_ref.shape[1]
    bk = ATT_BK
    n_chunks = s_len // bk
    per_q = bq // bk
    n_far = n_chunks - per_q
    j_lo = i * per_q
    q0 = i * bq
    slope2 = jnp.exp2(-jnp.full((1, 1), h + 1, jnp.int32).astype(F32)) * LOG2E
    if online:
        m_ref, sa_ref, sb_ref, mxa_ref, mxb_ref = mode_refs
        m_ref[...] = jnp.full_like(m_ref, -1e30)
    else:
        fixed_refs = mode_refs
    acc_ref[...] = jnp.zeros_like(acc_ref)

    lane = lax.broadcasted_iota(jnp.int32, (1, 128), 1)
    pos_lane = [(lane >= _aug_base(mp)) & (lane < _aug_base(mp) + N_POS_COLS) for mp in range(2)]
    for mp in range(2):
        q = qa_ref[mp]
        qv_ref[mp, 0] = q
        qv_ref[mp, 1] = jnp.where(pos_lane[mp], -q, q)

    def scores(mp, k0, q):
        kc = ka_ref[mp, pl.ds(k0, bk), :]
        return lax.dot_general(kc, q, (((1,), (1,)), ((), ())), preferred_element_type=F32)

    def accumulate(mp, s, mx, vt_c):
        if online:
            m_old = m_ref[mp]
            m_new = jnp.maximum(m_old, mx)
            p = jnp.exp2(s - m_new).astype(BF16)
            acc_ref[mp] = acc_ref[mp] * jnp.exp2(m_old - m_new) + _dot(vt_c, p)
            m_ref[mp] = m_new
        else:
            acc_ref[mp] += _dot(vt_c, jnp.exp2(s).astype(BF16))

    def far_start(t):
        t = jnp.minimum(t, n_far - 1)
        j = jnp.where(t < j_lo, t, t + per_q)
        return pl.multiple_of(j * bk, bk), (t >= j_lo).astype(jnp.int32)

    def near_chunk(t):
        k0 = pl.multiple_of(q0 + t * bk, bk)
        vt_c = vt_ref[:, pl.ds(k0, bk)]
        kpos = k0 + lax.broadcasted_iota(jnp.int32, (bk, bq), 0)
        qpos = q0 + lax.broadcasted_iota(jnp.int32, (bk, bq), 1)
        bias = -slope2 * jnp.abs(qpos - kpos).astype(F32)
        for mp in range(2):
            q = qa_ref[mp]
            q = jnp.where(pos_lane[mp], jnp.zeros_like(q), q)
            s = scores(mp, k0, q) + bias
            accumulate(mp, s, jnp.max(s, axis=0, keepdims=True) if online else None, vt_c)

    for t in range(per_q):
        near_chunk(t)

    if online:
        bufs = ((sa_ref, mxa_ref), (sb_ref, mxb_ref))

        def stage_one(t, buf):
            s_buf, mx_buf = buf
            k0, after = far_start(t)
            for mp in range(2):
                s = scores(mp, k0, qv_ref[mp, after])
                s_buf[mp] = s
                mx_buf[mp] = jnp.max(s, axis=0, keepdims=True)

        def stage_two(t, buf):
            s_buf, mx_buf = buf
            k0, _ = far_start(t)
            vt_c = vt_ref[:, pl.ds(k0, bk)]
            for mp in range(2):
                accumulate(mp, s_buf[mp], mx_buf[mp], vt_c)
    else:
        bufs = fixed_refs

        def stage_one(t, p_buf):
            k0, after = far_start(t)
            for mp in range(2):
                p_buf[mp] = jnp.exp2(scores(mp, k0, qv_ref[mp, after])).astype(BF16)

        def stage_two(t, p_buf):
            k0, _ = far_start(t)
            vt_c = vt_ref[:, pl.ds(k0, bk)]
            for mp in range(2):
                acc_ref[mp] += _dot(vt_c, p_buf[mp])

    stage_one(0, bufs[0])

    def far_body(u, c):
        for r in range(FAR_UNROLL):
            t = FAR_UNROLL * u + r
            stage_one(t + 1, bufs[(r + 1) % 2])
            stage_two(t, bufs[r % 2])
        return c

    lax.fori_loop(0, n_far // FAR_UNROLL, far_body, 0)
    for t in range(n_far - n_far % FAR_UNROLL, n_far):
        if t + 1 < n_far:
            stage_one(t + 1, bufs[(t + 1) % 2])
        stage_two(t, bufs[t % 2])

    lp = lam_ref[...]
    lam = (jnp.exp(jnp.sum(lp[0:1] * lp[1:2], axis=-1, keepdims=True))
           - jnp.exp(jnp.sum(lp[2:3] * lp[3:4], axis=-1, keepdims=True)) + lam_init)
    a0 = acc_ref[0]
    a1 = acc_ref[1]
    o = (a0[0:DIFF_V_DIM] / a0[DIFF_V_DIM:DIFF_V_DIM + 1]
         - lam * (a1[0:DIFF_V_DIM] / a1[DIFF_V_DIM:DIFF_V_DIM + 1]))
    ms = jnp.mean(o * o, axis=0, keepdims=True)
    o = o * lax.rsqrt(ms + NORM_EPS) * sg_ref[...] * (1.0 - lam_init)
    o_ref[...] = o.T.astype(o_ref.dtype)


def _diff_attn(qa, ka, vt, lam_p, subln_g, *, lam_init, online):
    s = qa.shape[2]
    bq = ATT_BQ
    assert bq % ATT_BK == 0 and s > bq
    kern = functools.partial(_diff_attn_kernel, lam_init=lam_init, online=online)
    scratch = [
        pltpu.VMEM((2, V_AUG, bq), F32),
        pltpu.VMEM((2, 2, bq, 128), BF16),
    ]
    if online:
        scratch += [
            pltpu.VMEM((2, 1, bq), F32),
            pltpu.VMEM((2, ATT_BK, bq), F32),
            pltpu.VMEM((2, ATT_BK, bq), F32),
            pltpu.VMEM((2, 1, bq), F32),
            pltpu.VMEM((2, 1, bq), F32),
        ]
    else:
        scratch += [pltpu.VMEM((2, ATT_BK, bq), BF16), pltpu.VMEM((2, ATT_BK, bq), BF16)]
    return pl.pallas_call(
        kern,
        grid=(DIFF_HEADS, s // bq),
        in_specs=[
            pl.BlockSpec((4, DIFF_QK_DIM), lambda h, i: (0, 0)),
            pl.BlockSpec((DIFF_V_DIM, 1), lambda h, i: (0, 0)),
            pl.BlockSpec((None, 2, bq, 128), lambda h, i: (h, 0, i, 0)),
            pl.BlockSpec((None, 2, s, 128), lambda h, i: (h, 0, 0, 0)),
            pl.BlockSpec((None, V_AUG, s), lambda h, i: (h, 0, 0)),
        ],
        out_specs=pl.BlockSpec((bq, DIFF_V_DIM), lambda h, i: (i, h)),
        out_shape=jax.ShapeDtypeStruct((s, DIFF_DIM), BF16),
        scratch_shapes=scratch,
        compiler_params=_cparams(("parallel", "arbitrary")),
        name="diff_attn_online" if online else "diff_attn",
    )(lam_p, subln_g.reshape(DIFF_V_DIM, 1), qa, ka, vt)


def _shift_rows(x, prev_row, next_row):
    n = x.shape[0]
    row = lax.broadcasted_iota(jnp.int32, x.shape, 0)
    xp = jnp.where(row == 0, prev_row, pltpu.roll(x, 1, 0))
    xn = jnp.where(row == n - 1, next_row, pltpu.roll(x, n - 1, 0))
    return xp, xn


def _halo_rows(prev_ref, next_ref, i, n_blocks):
    hp = prev_ref.shape[0]
    prev_row = prev_ref[hp - 1:hp, :].astype(F32)
    next_row = next_ref[0:1, :].astype(F32)
    prev_row = jnp.where(i > 0, prev_row, 0.0)
    next_row = jnp.where(i < n_blocks - 1, next_row, 0.0)
    return prev_row, next_row


def _rwkv_prep_kernel(
        x_ref, xp_ref, xn_ref, l_ref, lp_ref, ln_ref,
        mu_ref, mul_ref, w0_ref, w2_ref, a0_ref, a2_ref, g2_ref, kk_ref, ka_ref, rk_ref,
        bd_ref, tri_ref, triu_ref,
        v_out, g_out, bonus_out, rb_out, kb_out, kt_out, bt_out, kh_out, bh_out, gc_out):
    i = pl.program_id(0)
    nb = pl.num_programs(0)
    tm = x_ref.shape[0]

    x = x_ref[...].astype(F32)
    prev_row, next_row = _halo_rows(xp_ref, xn_ref, i, nb)
    xp, xn = _shift_rows(x, prev_row, next_row)
    mu = mu_ref[...]
    x = x + mu[0:1] * (xp - x) + mu[1:2] * (xn - x)
    lo = l_ref[...].astype(F32)
    prev_row, next_row = _halo_rows(lp_ref, ln_ref, i, nb)
    lop, lon = _shift_rows(lo, prev_row, next_row)
    mul = mul_ref[...]
    lo = lo + mul[0:1] * (lop - lo) + mul[1:2] * (lon - lo)

    r = x[:, 0:RWKV_DIM]
    k = x[:, RWKV_DIM:2 * RWKV_DIM]
    v = x[:, 2 * RWKV_DIM:3 * RWKV_DIM]
    tw = jnp.tanh(lo[:, 0:128]).astype(BF16)
    la = lo[:, 128:256].astype(BF16)
    lg = jax.nn.sigmoid(lo[:, 256:512]).astype(BF16)

    g = _dot(lg, g2_ref[...])
    kk = k * kk_ref[...]
    ss = _group_mean_sq(kk, bd_ref, 1.0)
    kk = kk * lax.rsqrt(jnp.maximum(ss, 1e-24))
    hi, mid, _ = _split3(r * k * rk_ref[...])
    bonus = (_dot(hi, bd_ref[...]) + _dot(mid, bd_ref[...])) * v

    g_out[...] = g.astype(g_out.dtype)
    bonus_out[...] = bonus.astype(bonus_out.dtype)
    for h in range(RWKV_HEADS):
        v_out[h] = v[:, 64 * h:64 * (h + 1)].astype(BF16)

    tri = tri_ref[...]
    triu = triu_ref[...]
    for d in range(2):
        wl = w0_ref[d:d + 1, :] + _dot(tw, w2_ref[d])
        z = -wl
        softplus = jnp.maximum(z, 0.0) + jnp.log(1.0 + jnp.exp(-jnp.abs(z)))
        logdec = -jnp.exp(-softplus - 0.5)
        a = jax.nn.sigmoid(a0_ref[d:d + 1, :] + _dot(la, a2_ref[d]))
        k_d = k * (1.0 + (a - 1.0) * ka_ref[...])
        b_d = kk * a
        pre = _dot_exactish(tri, logdec)
        suf = _dot_exactish(triu, logdec)
        if d == 0:
            lc, ex, rem = pre, pre - logdec, suf
        else:
            lc, ex, rem = suf + logdec, suf, pre - logdec
        e_neg = jnp.exp(-lc)
        e_rem = jnp.exp(rem)
        outs = (
            (rb_out, r * jnp.exp(lc)),
            (kb_out, kk * jnp.exp(ex)),
            (kt_out, k_d * e_neg),
            (bt_out, b_d * e_neg),
            (kh_out, k_d * e_rem),
            (bh_out, b_d * e_rem),
        )
        for ref, val in outs:
            val = val.astype(BF16)
            for h in range(RWKV_HEADS):
                ref[d, h] = val[:, 64 * h:64 * (h + 1)]
        tot = jnp.exp(lc + rem).reshape(tm // 8, 8, RWKV_DIM)[:, 0, :]
        for h in range(RWKV_HEADS):
            gc_out[d, h] = tot[:, 64 * h:64 * (h + 1)]


def _rwkv_prep(p, mu_rkv, mu_lora, w0, w2p, a0, a2p, g2p, k_k, k_a, r_k, bd, tri, triu, *, tm):
    s = p.shape[0]
    nb = s // tm
    hb = tm // 16
    last16 = s // 16 - 1
    rkv_w = 3 * RWKV_DIM
    c_rkv = COL_RKV // rkv_w
    c_lora = COL_LORA // LORA_PAD

    def prev_map(c):
        return lambda i: (jnp.maximum(i * hb - 1, 0), c)

    def next_map(c):
        return lambda i: (jnp.minimum((i + 1) * hb, last16), c)

    full = lambda *shape: pl.BlockSpec(shape, lambda i: (0,) * len(shape))
    hm = lambda: pl.BlockSpec((2, RWKV_HEADS, tm, 64), lambda i: (0, 0, i, 0))
    hm_shape = jax.ShapeDtypeStruct((2, RWKV_HEADS, s, 64), BF16)
    return pl.pallas_call(
        _rwkv_prep_kernel,
        grid=(nb,),
        in_specs=[
            pl.BlockSpec((tm, rkv_w), lambda i: (i, c_rkv)),
            pl.BlockSpec((16, rkv_w), prev_map(c_rkv)),
            pl.BlockSpec((16, rkv_w), next_map(c_rkv)),
            pl.BlockSpec((tm, LORA_PAD), lambda i: (i, c_lora)),
            pl.BlockSpec((16, LORA_PAD), prev_map(c_lora)),
            pl.BlockSpec((16, LORA_PAD), next_map(c_lora)),
            full(2, rkv_w), full(2, LORA_PAD),
            full(2, RWKV_DIM), full(2, 128, RWKV_DIM),
            full(2, RWKV_DIM), full(2, 128, RWKV_DIM),
            full(256, RWKV_DIM),
            full(1, RWKV_DIM), full(1, RWKV_DIM), full(1, RWKV_DIM),
            full(RWKV_DIM, RWKV_DIM), full(tm, tm), full(tm, tm),
        ],
        out_specs=[
            pl.BlockSpec((RWKV_HEADS, tm, 64), lambda i: (0, i, 0)),
            pl.BlockSpec((tm, RWKV_DIM), lambda i: (i, 0)),
            pl.BlockSpec((tm, RWKV_DIM), lambda i: (i, 0)),
            hm(), hm(), hm(), hm(), hm(), hm(),
            pl.BlockSpec((2, RWKV_HEADS, tm // 8, 64), lambda i: (0, 0, i, 0)),
        ],
        out_shape=[
            jax.ShapeDtypeStruct((RWKV_HEADS, s, 64), BF16),
            jax.ShapeDtypeStruct((s, RWKV_DIM), BF16),
            jax.ShapeDtypeStruct((s, RWKV_DIM), BF16),
            hm_shape, hm_shape, hm_shape, hm_shape, hm_shape, hm_shape,
            jax.ShapeDtypeStruct((2, RWKV_HEADS, s // 8, 64), F32),
        ],
        compiler_params=_cparams(("parallel",)),
        name="rwkv_prep",
    )(p, p, p, p, p, p, mu_rkv, mu_lora, w0, w2p, a0, a2p, g2p, k_k, k_a, r_k, bd, tri, triu)


def _bdot(a, b):
    return lax.dot_general(a, b, (((2,), (1,)), ((0,), (0,))), preferred_element_type=F32)


def _bdot_nt(a, b):
    return lax.dot_general(a, b, (((2,), (2,)), ((0,), (0,))), preferred_element_type=F32)


def _bdot_tn(a, b):
    return lax.dot_general(a, b, (((1,), (1,)), ((0,), (0,))), preferred_element_type=F32)


def _bdot_inv(a, b):
    return _bdot(a.astype(BF16), b.astype(BF16))


def _wkv_kernel(v_ref, rb_ref, kb_ref, kt_ref, bt_ref, kh_ref, bh_ref, gc_ref, y_ref, s_ref):
    d = pl.program_id(0)
    c = pl.program_id(1)

    @pl.when(c == 0)
    def _():
        s_ref[...] = jnp.zeros_like(s_ref)

    nh = RWKV_HEADS
    row = lax.broadcasted_iota(jnp.int32, (nh, CHUNK, CHUNK), 1)
    col = lax.broadcasted_iota(jnp.int32, (nh, CHUNK, CHUNK), 2)
    ahead = jnp.where(d == 0, row - col, col - row)
    strict = ahead > 0
    incl = ahead >= 0

    v = v_ref[...]
    rb, kb, kt, bt = rb_ref[...], kb_ref[...], kt_ref[...], bt_ref[...]
    kh, bh = kh_ref[...], bh_ref[...]

    a_kk = jnp.where(strict, _bdot_nt(kb, kt), 0.0)
    l_mat = jnp.where(strict, _bdot_nt(kb, bt), 0.0)
    a_rk = jnp.where(incl, _bdot_nt(rb, kt), 0.0)
    a_rb = jnp.where(incl, _bdot_nt(rb, bt), 0.0)

    eye = jnp.where(row == col, 1.0, 0.0)
    t_inv = eye - l_mat
    pw = _bdot_inv(l_mat, l_mat)
    for step in range(5):
        t_inv = t_inv + _bdot_inv(t_inv, pw)
        if step < 4:
            pw = _bdot_inv(pw, pw)

    q1 = _bdot(a_kk.astype(BF16), v)
    q2 = _bdot(a_rk.astype(BF16), v)
    t_b = t_inv.astype(BF16)
    wk = _bdot(t_b, kb)
    uv = _bdot(t_b, q1.astype(BF16))
    a_rb_b = a_rb.astype(BF16)
    wk_b = wk.astype(BF16)
    uv_b = uv.astype(BF16)
    rw = rb.astype(F32) - _bdot(a_rb_b, wk_b)
    y0 = q2 - _bdot(a_rb_b, uv_b)
    m_mat = _bdot_tn(wk_b, bh)
    j_t = _bdot_tn(v, kh) - _bdot_tn(uv_b, bh)

    s_old = s_ref[...]
    s_hi = s_old.astype(BF16)
    s_lo = (s_old - s_hi.astype(F32)).astype(BF16)
    rw_b = rw.astype(BF16)
    m_b = m_mat.astype(BF16)
    y_ref[...] = _bdot_nt(rw_b, s_hi) + _bdot_nt(rw_b, s_lo) + y0
    s_ref[...] = s_old * gc_ref[:, 0:1, :] - (_bdot(s_hi, m_b) + _bdot(s_lo, m_b)) + j_t


def _wkv(v, rb, kb, kt, bt, kh, bh, gc):
    s = v.shape[1]
    nc = s // CHUNK

    def cidx(d, c):
        return jnp.where(d == 0, c, nc - 1 - c)

    dir_spec = lambda: pl.BlockSpec((None, RWKV_HEADS, CHUNK, 64), lambda d, c: (d, 0, cidx(d, c), 0))
    return pl.pallas_call(
        _wkv_kernel,
        grid=(2, nc),
        in_specs=[
            pl.BlockSpec((RWKV_HEADS, CHUNK, 64), lambda d, c: (0, cidx(d, c), 0)),
            dir_spec(), dir_spec(), dir_spec(), dir_spec(), dir_spec(), dir_spec(),
            pl.BlockSpec((None, RWKV_HEADS, 8, 64), lambda d, c: (d, 0, cidx(d, c), 0)),
        ],
        out_specs=dir_spec(),
        out_shape=jax.ShapeDtypeStruct((2, RWKV_HEADS, s, 64), F32),
        scratch_shapes=[pltpu.VMEM((RWKV_HEADS, 64, 64), F32)],
        compiler_params=_cparams(("arbitrary", "arbitrary")),
        name="wkv7_chunked",
    )(v, rb, kb, kt, bt, kh, bh, gc)


def _rwkv_post_kernel(y_ref, g_ref, bonus_ref, lg_ref, lb_ref, o_ref):
    y = y_ref[0] + y_ref[1]
    mean = jnp.mean(y, axis=-1, keepdims=True)
    yc = y - mean
    var = jnp.mean(yc * yc, axis=-1, keepdims=True)
    yn = yc * lax.rsqrt(var + LNX_EPS)
    yt = jnp.concatenate([yn[h] for h in range(RWKV_HEADS)], axis=-1)
    out = (yt * lg_ref[...] + lb_ref[...] + bonus_ref[...].astype(F32)) * g_ref[...].astype(F32)
    o_ref[...] = out.astype(o_ref.dtype)


def _rwkv_post(y, g, bonus, lnx_g, lnx_b, *, tm):
    s = g.shape[0]
    return pl.pallas_call(
        _rwkv_post_kernel,
        grid=(s // tm,),
        in_specs=[
            pl.BlockSpec((2, RWKV_HEADS, tm, 64), lambda i: (0, 0, i, 0)),
            pl.BlockSpec((tm, RWKV_DIM), lambda i: (i, 0)),
            pl.BlockSpec((tm, RWKV_DIM), lambda i: (i, 0)),
            pl.BlockSpec((1, RWKV_DIM), lambda i: (0, 0)),
            pl.BlockSpec((1, RWKV_DIM), lambda i: (0, 0)),
        ],
        out_specs=pl.BlockSpec((tm, RWKV_DIM), lambda i: (i, 0)),
        out_shape=jax.ShapeDtypeStruct((s, RWKV_DIM), BF16),
        compiler_params=_cparams(("parallel",)),
        name="rwkv_post",
    )(y, g, bonus, lnx_g.reshape(1, RWKV_DIM), lnx_b.reshape(1, RWKV_DIM))


def _mem_attn_kernel(q_ref, kv_ref, gq_ref, gk_ref, o_ref):
    for h in range(MEM_HEADS):
        sl = slice(MEM_HEAD_DIM * h, MEM_HEAD_DIM * (h + 1))
        q = q_ref[:, sl].astype(F32)
        q = q * lax.rsqrt(jnp.mean(q * q, axis=-1, keepdims=True) + NORM_EPS)
        q = q * gq_ref[...] * (MEM_HEAD_DIM ** -0.5)
        km = kv_ref[:, sl].astype(F32)
        km = km * lax.rsqrt(jnp.mean(km * km, axis=-1, keepdims=True) + NORM_EPS)
        km = km * gk_ref[...]
        vm = kv_ref[:, MEM_DIM + MEM_HEAD_DIM * h:MEM_DIM + MEM_HEAD_DIM * (h + 1)]
        s = lax.dot_general(q.astype(BF16), km.astype(BF16), (((1,), (1,)), ((), ())),
                            preferred_element_type=F32)
        s = s - jnp.max(s, axis=-1, keepdims=True)
        e = jnp.exp(s)
        pr = e / jnp.sum(e, axis=-1, keepdims=True)
        o_ref[:, sl] = _dot(pr.astype(BF16), vm).astype(o_ref.dtype)


def _mem_attn(p, kv, gq, gk, *, tm):
    s = p.shape[0]
    return pl.pallas_call(
        _mem_attn_kernel,
        grid=(s // tm,),
        in_specs=[
            pl.BlockSpec((tm, MEM_DIM), lambda i: (i, COL_MEM // MEM_DIM)),
            pl.BlockSpec((N_MEM, 2 * MEM_DIM), lambda i: (0, 0)),
            pl.BlockSpec((1, MEM_HEAD_DIM), lambda i: (0, 0)),
            pl.BlockSpec((1, MEM_HEAD_DIM), lambda i: (0, 0)),
        ],
        out_specs=pl.BlockSpec((tm, MEM_DIM), lambda i: (i, 0)),
        out_shape=jax.ShapeDtypeStruct((s, MEM_DIM), BF16),
        compiler_params=_cparams(("parallel",)),
        name="mem_attn",
    )(p, kv, gq, gk)


def _merge_kernel(o0_ref, o1_ref, o2_ref, g0_ref, g1_ref, g2_ref, w_ref, m_ref):
    acc = jax.nn.sigmoid(g0_ref[...].astype(F32)) * _dot(o0_ref[...], w_ref[0])
    acc = acc + jax.nn.sigmoid(g1_ref[...].astype(F32)) * _dot(o1_ref[...], w_ref[1])
    acc = acc + jax.nn.sigmoid(g2_ref[...].astype(F32)) * _dot(o2_ref[...], w_ref[2])
    m_ref[...] = acc.astype(m_ref.dtype)


def _merge(o_diff, o_rwkv, o_mem, p, w_branch, *, tm, tn):
    s = p.shape[0]
    gate_blk = COL_GATE // tn
    per = D_MODEL // tn
    o_spec = lambda: pl.BlockSpec((tm, 1024), lambda i, j: (i, 0))
    g_spec = lambda b: pl.BlockSpec((tm, tn), lambda i, j: (i, gate_blk + b * per + j))
    return pl.pallas_call(
        _merge_kernel,
        grid=(s // tm, per),
        in_specs=[
            o_spec(), o_spec(), o_spec(),
            g_spec(0), g_spec(1), g_spec(2),
            pl.BlockSpec((N_BRANCH, 1024, tn), lambda i, j: (0, 0, j)),
        ],
        out_specs=pl.BlockSpec((tm, tn), lambda i, j: (i, j)),
        out_shape=jax.ShapeDtypeStruct((s, D_MODEL), BF16),
        compiler_params=_cparams(("parallel", "arbitrary")),
        name="merge",
    )(o_diff, o_rwkv, o_mem, p, p, p, w_branch)


def _glu_kernel(ug_ref, ugp_ref, ugn_ref, uv_ref, cw_ref, cb_ref, o_ref):
    i = pl.program_id(0)
    nb = pl.num_programs(0)
    g = ug_ref[...].astype(F32)
    prev_row, next_row = _halo_rows(ugp_ref, ugn_ref, i, nb)
    gp, gn = _shift_rows(g, prev_row, next_row)
    cw = cw_ref[...]
    conv = cw[0:1] * gp + cw[1:2] * g + cw[2:3] * gn + cb_ref[...]
    act = conv * jax.nn.sigmoid(conv) * uv_ref[...].astype(F32)
    o_ref[...] = act.astype(o_ref.dtype)


def _glu(u, conv_w, conv_b, *, tm, tc):
    s = u.shape[0]
    nj = D_FF // tc
    hb = tm // 16
    last16 = s // 16 - 1
    return pl.pallas_call(
        _glu_kernel,
        grid=(s // tm, nj),
        in_specs=[
            pl.BlockSpec((tm, tc), lambda i, j: (i, j)),
            pl.BlockSpec((16, tc), lambda i, j: (jnp.maximum(i * hb - 1, 0), j)),
            pl.BlockSpec((16, tc), lambda i, j: (jnp.minimum((i + 1) * hb, last16), j)),
            pl.BlockSpec((tm, tc), lambda i, j: (i, nj + j)),
            pl.BlockSpec((3, tc), lambda i, j: (0, j)),
            pl.BlockSpec((1, tc), lambda i, j: (0, j)),
        ],
        out_specs=pl.BlockSpec((tm, tc), lambda i, j: (i, j)),
        out_shape=jax.ShapeDtypeStruct((s, D_FF), BF16),
        compiler_params=_cparams(("parallel", "parallel")),
        name="conv_glu",
    )(u, u, u, u, conv_w, conv_b.reshape(1, D_FF))


def _block_ones(n, group):
    idx = jnp.arange(n) // group
    return (idx[:, None] == idx[None, :]).astype(BF16)


def _chunk_tri(n):
    idx = jnp.arange(n)
    same = (idx[:, None] // CHUNK) == (idx[None, :] // CHUNK)
    lower_incl = same & (idx[:, None] >= idx[None, :])
    upper_strict = same & (idx[:, None] < idx[None, :])
    return lower_incl.astype(BF16), upper_strict.astype(BF16)


def _pad_rows(w, rows_before, total):
    n = w.shape[-1]
    out = jnp.zeros((total, n), w.dtype)
    return lax.dynamic_update_slice(out, w, (rows_before, 0))


def _pad_w_in(w):
    lora0 = COL_RKV + 3 * RWKV_DIM
    lora1 = lora0 + LORA_COLS
    pad = jnp.zeros((w.shape[0], LORA_PAD - LORA_COLS), w.dtype)
    return jnp.concatenate([w[:, :lora0], w[:, lora1:], w[:, lora0:lora1], pad], axis=1).astype(BF16)


def kernel(x, mem, attn_norm_g, w_in, diff_qk_g, diff_lambda, diff_subln_g, rwkv_mu, rwkv_w0,
           rwkv_w2, rwkv_a0, rwkv_a2, rwkv_g2, rwkv_k_k, rwkv_k_a, rwkv_r_k, rwkv_lnx_g,
           rwkv_lnx_b, mem_norm_g, w_mem_kv, mem_qk_g, w_branch, w_out, ffn_norm_g, w_ffn_up,
           ffn_conv_w, ffn_conv_b, w_ffn_down):
    b, s, d = x.shape
    assert b == 1 and d == D_MODEL and s % ATT_BQ == 0
    xs = x.reshape(s, d)
    mem2 = mem.reshape(N_MEM, d)
    prep_tm = 256
    bd64 = _block_ones(1024, 64)
    tri, triu = _chunk_tri(prep_tm)
    n_rwkv_main = 3 * RWKV_DIM

    for l in range(DEPTH):
        lam_init = 0.8 - 0.6 * math.exp(-0.3 * l)
        p = _rms_mm(xs, attn_norm_g[l], _pad_w_in(w_in[l]), tm=512, tn=1536, name="rms_w_in")

        gq = jnp.tile(diff_qk_g[l, 0].reshape(1, 128), (1, DIFF_HEADS))
        gk = jnp.tile(diff_qk_g[l, 1].reshape(1, 128), (1, DIFF_HEADS))
        qc, kc, score_bound = _attn_consts(diff_qk_g[l])
        qa, ka, vt = _diff_prep(p, gq, gk, qc, kc, bd64, tm=256)
        o_diff = lax.cond(
            score_bound <= SCORE_BOUND_MAX,
            functools.partial(_diff_attn, lam_init=lam_init, online=False),
            functools.partial(_diff_attn, lam_init=lam_init, online=True),
            qa, ka, vt, diff_lambda[l], diff_subln_g[l])

        mu = rwkv_mu[l]
        mu_rkv = mu[:, :n_rwkv_main]
        mu_lora = jnp.pad(mu[:, n_rwkv_main:], ((0, 0), (0, LORA_PAD - LORA_COLS)))
        w2p = jnp.stack([_pad_rows(rwkv_w2[l, dd], 64 * dd, 128) for dd in range(2)]).astype(BF16)
        a2p = jnp.stack([_pad_rows(rwkv_a2[l, dd], 64 * dd, 128) for dd in range(2)]).astype(BF16)
        g2p = _pad_rows(rwkv_g2[l], 0, 256).astype(BF16)
        (v_h, g_tok, bonus, rb, kb, kt, bt, kh, bh, gc) = _rwkv_prep(
            p, mu_rkv, mu_lora, rwkv_w0[l], w2p, rwkv_a0[l], a2p, g2p,
            rwkv_k_k[l].reshape(1, RWKV_DIM), rwkv_k_a[l].reshape(1, RWKV_DIM),
            rwkv_r_k[l].reshape(1, RWKV_DIM), bd64, tri, triu, tm=prep_tm)
        y = _wkv(v_h, rb, kb, kt, bt, kh, bh, gc)
        o_rwkv = _rwkv_post(y, g_tok, bonus, rwkv_lnx_g[l], rwkv_lnx_b[l], tm=256)

        kv = _rms_mm(mem2, mem_norm_g[l], w_mem_kv[l].astype(BF16), tm=N_MEM, tn=1024,
                     name="rms_mem_kv")
        o_mem = _mem_attn(p, kv, mem_qk_g[l, 0].reshape(1, MEM_HEAD_DIM),
                          mem_qk_g[l, 1].reshape(1, MEM_HEAD_DIM), tm=512)

        merged = _merge(o_diff, o_rwkv, o_mem, p, w_branch[l].astype(BF16), tm=512, tn=512)
        xs = _mm_res(merged, w_out[l].astype(BF16), xs, tm=512, tn=1024, name="w_out_res")

        u = _rms_mm(xs, ffn_norm_g[l], w_ffn_up[l].astype(BF16), tm=512, tn=1024, name="rms_ffn_up")
        act = _glu(u, ffn_conv_w[l], ffn_conv_b[l], tm=512, tc=512)
        xs = _mm_res(act, w_ffn_down[l].astype(BF16), xs, tm=512, tn=512, name="ffn_down_res")

    return xs.reshape(b, s, d)
```

```python
import functools
import math

import jax
import jax.numpy as jnp
import numpy as np
from jax import lax
from jax.experimental import pallas as pl
from jax.experimental.pallas import tpu as pltpu

F32 = jnp.float32
BF16 = jnp.bfloat16

D_MODEL = 2048
DEPTH = 2
DIFF_HEADS = 8
DIFF_QK_DIM = 64
DIFF_V_DIM = 128
DIFF_DIM = 1024
RWKV_HEADS = 16
RWKV_HEAD_DIM = 64
RWKV_DIM = 1024
DECAY_LORA = 64
AAA_LORA = 64
GATE_LORA = 160
LORA_COLS = 2 * DECAY_LORA + 2 * AAA_LORA + GATE_LORA
LORA_PAD = 512
N_MEM = 256
MEM_HEADS = 4
MEM_HEAD_DIM = 256
MEM_DIM = 1024
N_BRANCH = 3
D_FF = 5632
NORM_EPS = 1e-6
LNX_EPS = 64e-5

COL_DIFF = 0
COL_RKV = 3 * DIFF_DIM
COL_MEM = COL_RKV + 3 * RWKV_DIM
COL_GATE = COL_MEM + MEM_DIM
COL_LORA = COL_GATE + N_BRANCH * D_MODEL
N_IN_PAD = COL_LORA + LORA_PAD

CHUNK = 64
LOG2E = 1.4426950408889634
N_POS_COLS = 12
SCORE_BOUND_MAX = 30.0
ATT_BQ = 512
ATT_BK = 256
FAR_UNROLL = 6
V_AUG = 144
VMEM_LIMIT = 48 * 1024 * 1024


def _cparams(sem):
    return pltpu.CompilerParams(dimension_semantics=sem, vmem_limit_bytes=VMEM_LIMIT)


def _split3(x):
    hi = x.astype(BF16)
    r1 = x - hi.astype(F32)
    mid = r1.astype(BF16)
    lo = (r1 - mid.astype(F32)).astype(BF16)
    return hi, mid, lo


def _dot(a, b):
    return jnp.dot(a, b, preferred_element_type=F32)


def _dot_exactish(a_bf16_exact, x_f32):
    hi, mid, lo = _split3(x_f32)
    return _dot(a_bf16_exact, hi) + _dot(a_bf16_exact, mid) + _dot(a_bf16_exact, lo)


def _dot_exactish_r(x_f32, b_bf16_exact):
    hi, mid, lo = _split3(x_f32)
    return _dot(hi, b_bf16_exact) + _dot(mid, b_bf16_exact) + _dot(lo, b_bf16_exact)


def _rms_mm_kernel(x_ref, g_ref, w_ref, o_ref, h_ref):
    @pl.when(pl.program_id(1) == 0)
    def _():
        x = x_ref[...]
        ms = jnp.mean(x * x, axis=-1, keepdims=True)
        h_ref[...] = (x * lax.rsqrt(ms + NORM_EPS) * g_ref[...]).astype(BF16)

    o_ref[...] = _dot(h_ref[...], w_ref[...]).astype(o_ref.dtype)


def _rms_mm(x, g, w, *, tm, tn, name):
    m, k = x.shape
    n = w.shape[1]
    return pl.pallas_call(
        _rms_mm_kernel,
        grid=(m // tm, n // tn),
        in_specs=[
            pl.BlockSpec((tm, k), lambda i, j: (i, 0)),
            pl.BlockSpec((1, k), lambda i, j: (0, 0)),
            pl.BlockSpec((k, tn), lambda i, j: (0, j)),
        ],
        out_specs=pl.BlockSpec((tm, tn), lambda i, j: (i, j)),
        out_shape=jax.ShapeDtypeStruct((m, n), BF16),
        scratch_shapes=[pltpu.VMEM((tm, k), BF16)],
        compiler_params=_cparams(("parallel", "arbitrary")),
        name=name,
    )(x, g.reshape(1, k), w)


def _mm_res_kernel(a_ref, w_ref, r_ref, o_ref):
    o_ref[...] = r_ref[...] + _dot(a_ref[...], w_ref[...])


def _mm_res(a, w, res, *, tm, tn, name):
    m, k = a.shape
    n = w.shape[1]
    return pl.pallas_call(
        _mm_res_kernel,
        grid=(m // tm, n // tn),
        in_specs=[
            pl.BlockSpec((tm, k), lambda i, j: (i, 0)),
            pl.BlockSpec((k, tn), lambda i, j: (0, j)),
            pl.BlockSpec((tm, tn), lambda i, j: (i, j)),
        ],
        out_specs=pl.BlockSpec((tm, tn), lambda i, j: (i, j)),
        out_shape=jax.ShapeDtypeStruct((m, n), F32),
        compiler_params=_cparams(("parallel", "arbitrary")),
        name=name,
    )(a, w, res)


def _group_mean_sq(x, bd_ref, group):
    hi, mid, _ = _split3(x * x)
    bd = bd_ref[...]
    return (_dot(hi, bd) + _dot(mid, bd)) * (1.0 / group)


def _aug_base(mp):
    return 64 if mp == 0 else 0


def _diff_prep_kernel(q_ref, k_ref, v_ref, gq_ref, gk_ref, qc_ref, kc_ref, bd_ref,
                      qa_ref, ka_ref, vt_ref):
    tm = q_ref.shape[0]
    row0 = pl.program_id(0) * tm
    lane = lax.broadcasted_iota(jnp.int32, (tm, 128), 1)
    pos = row0 + lax.broadcasted_iota(jnp.int32, (tm, 128), 0)
    pos_lo = (pos & 127).astype(F32)
    pos_hi = (pos >> 7).astype(F32)

    q = q_ref[...].astype(F32)
    qn = q * lax.rsqrt(_group_mean_sq(q, bd_ref, DIFF_QK_DIM) + NORM_EPS)
    qn = qn * gq_ref[...] * (DIFF_QK_DIM ** -0.5 * LOG2E)
    k = k_ref[...].astype(F32)
    kn = k * lax.rsqrt(_group_mean_sq(k, bd_ref, DIFF_QK_DIM) + NORM_EPS)
    kn = kn * gk_ref[...]

    for h in range(DIFF_HEADS):
        qh = qn[:, 128 * h:128 * (h + 1)]
        kh = kn[:, 128 * h:128 * (h + 1)]
        for mp in range(2):
            a0 = _aug_base(mp)
            g = 2 * h + mp
            is_data = (lane < 64) if mp == 0 else (lane >= 64)
            aug_q = jnp.where(lane < a0 + 3, pos_lo,
                              jnp.where(lane < a0 + 6, pos_hi, qc_ref[g:g + 1, :]))
            aug_k = jnp.where((lane >= a0 + 6) & (lane < a0 + 9), pos_lo,
                              jnp.where((lane >= a0 + 9) & (lane < a0 + 12), pos_hi,
                                        kc_ref[g:g + 1, :]))
            qa_ref[h, mp] = jnp.where(is_data, qh, aug_q).astype(BF16)
            ka_ref[h, mp] = jnp.where(is_data, kh, aug_k).astype(BF16)

    vt = v_ref[...].astype(F32).T
    sub = lax.broadcasted_iota(jnp.int32, (V_AUG - DIFF_V_DIM, tm), 0)
    ones_rows = jnp.where(sub == 0, 1.0, 0.0).astype(BF16)
    for h in range(DIFF_HEADS):
        vt_ref[h, 0:DIFF_V_DIM, :] = vt[128 * h:128 * (h + 1), :].astype(BF16)
        vt_ref[h, DIFF_V_DIM:V_AUG, :] = ones_rows


def _bf16_split3_const(x):
    parts = []
    for _ in range(3):
        part = float(np.asarray(x, np.float32).astype(BF16).astype(np.float32))
        parts.append(part)
        x = x - part
    return parts


def _attn_consts(qk_g):
    l_parts = _bf16_split3_const(LOG2E)
    m_nat = 8.0 * jnp.max(jnp.abs(qk_g[0]), axis=-1) * jnp.max(jnp.abs(qk_g[1]), axis=-1)
    qc = np.zeros((2 * DIFF_HEADS, 128), np.float32)
    kc = np.zeros((2 * DIFF_HEADS, 128), np.float32)
    bound_lane = np.zeros((2, 2 * DIFF_HEADS, 128), np.float32)
    for h in range(DIFF_HEADS):
        slope = 2.0 ** (-(h + 1))
        for mp in range(2):
            a0 = _aug_base(mp)
            g = 2 * h + mp
            for t, lp in enumerate(l_parts):
                qc[g, a0 + 6 + t] = slope * lp
                qc[g, a0 + 9 + t] = 128.0 * slope * lp
                kc[g, a0 + t] = -slope * lp
                kc[g, a0 + 3 + t] = -128.0 * slope * lp
            bound_lane[mp, g, a0 + N_POS_COLS] = 1.0
            kc[g, a0 + N_POS_COLS] = 1.0
    m2 = -m_nat * LOG2E
    qc = qc + m2[0] * bound_lane[0] + m2[1] * bound_lane[1]
    return qc, jnp.asarray(kc), jnp.max(m_nat)


def _diff_prep(p, gq, gk, qc, kc, bd, *, tm):
    s = p.shape[0]
    return pl.pallas_call(
        _diff_prep_kernel,
        grid=(s // tm,),
        in_specs=[
            pl.BlockSpec((tm, DIFF_DIM), lambda i: (i, 0)),
            pl.BlockSpec((tm, DIFF_DIM), lambda i: (i, 1)),
            pl.BlockSpec((tm, DIFF_DIM), lambda i: (i, 2)),
            pl.BlockSpec((1, DIFF_DIM), lambda i: (0, 0)),
            pl.BlockSpec((1, DIFF_DIM), lambda i: (0, 0)),
            pl.BlockSpec((2 * DIFF_HEADS, 128), lambda i: (0, 0)),
            pl.BlockSpec((2 * DIFF_HEADS, 128), lambda i: (0, 0)),
            pl.BlockSpec((DIFF_DIM, DIFF_DIM), lambda i: (0, 0)),
        ],
        out_specs=[
            pl.BlockSpec((DIFF_HEADS, 2, tm, 128), lambda i: (0, 0, i, 0)),
            pl.BlockSpec((DIFF_HEADS, 2, tm, 128), lambda i: (0, 0, i, 0)),
            pl.BlockSpec((DIFF_HEADS, V_AUG, tm), lambda i: (0, 0, i)),
        ],
        out_shape=[
            jax.ShapeDtypeStruct((DIFF_HEADS, 2, s, 128), BF16),
            jax.ShapeDtypeStruct((DIFF_HEADS, 2, s, 128), BF16),
            jax.ShapeDtypeStruct((DIFF_HEADS, V_AUG, s), BF16),
        ],
        compiler_params=_cparams(("parallel",)),
        name="diff_prep",
    )(p, p, p, gq, gk, qc, kc, bd)


def _diff_attn_kernel(lam_ref, sg_ref, dnear_ref, qa_ref, ka_ref, vt_ref, o_ref,
                      acc_ref, qv_ref, *mode_refs, lam_init, online):
    h = pl.program_id(0)
    i = pl.program_id(1)
    bq = qa_ref.shape[1]
    s_len = ka_ref.shape[1]
    bk = ATT_BK
    n_chunks = s_len // bk
    per_q = bq // bk
    n_far = n_chunks - per_q
    j_lo = i * per_q
    q0 = i * bq
    slope2 = jnp.exp2(-jnp.full((1, 1), h + 1, jnp.int32).astype(F32)) * LOG2E
    if online:
        m_ref, sa_ref, sb_ref, mxa_ref, mxb_ref = mode_refs
        m_ref[...] = jnp.full_like(m_ref, -1e30)
        bufs = ((sa_ref, mxa_ref), (sb_ref, mxb_ref))
    else:
        bufs = mode_refs
    acc_ref[...] = jnp.zeros_like(acc_ref)

    lane = lax.broadcasted_iota(jnp.int32, (1, 128), 1)
    for mp in range(2):
        q = qa_ref[mp]
        pos_lane = (lane >= _aug_base(mp)) & (lane < _aug_base(mp) + N_POS_COLS)
        qv_ref[mp, 0] = q
        qv_ref[mp, 1] = jnp.where(pos_lane, -q, q)
        qv_ref[mp, 2] = jnp.where(pos_lane, jnp.zeros_like(q), q)

    def stage_one(k0, variant, bias, buf):
        for mp in range(2):
            kc = ka_ref[mp, pl.ds(k0, bk), :]
            s = lax.dot_general(kc, qv_ref[mp, variant], (((1,), (1,)), ((), ())),
                                preferred_element_type=F32)
            if bias is not None:
                s = s + bias
            if online:
                buf[0][mp] = s
                buf[1][mp] = jnp.max(s, axis=0, keepdims=True)
            else:
                buf[mp] = jnp.exp2(s).astype(BF16)

    def stage_two(k0, buf):
        vt_c = vt_ref[:, pl.ds(k0, bk)]
        for mp in range(2):
            if online:
                m_old = m_ref[mp]
                m_new = jnp.maximum(m_old, buf[1][mp])
                p = jnp.exp2(buf[0][mp] - m_new).astype(BF16)
                acc_ref[mp] = acc_ref[mp] * jnp.exp2(m_old - m_new) + _dot(vt_c, p)
                m_ref[mp] = m_new
            else:
                acc_ref[mp] += _dot(vt_c, buf[mp])

    def near_start(t):
        return pl.multiple_of(q0 + t * bk, bk)

    def far_start(t):
        t = jnp.minimum(t, n_far - 1)
        j = jnp.where(t < j_lo, t, t + per_q)
        return pl.multiple_of(j * bk, bk), (t >= j_lo).astype(jnp.int32)

    def near_one(t):
        stage_one(near_start(t), 2, slope2 * dnear_ref[t], bufs[t % 2])

    def far_one(t, parity):
        k0, after = far_start(t)
        stage_one(k0, after, None, bufs[parity])

    def far_two(t, parity):
        stage_two(far_start(t)[0], bufs[parity])

    near_one(0)
    for t in range(1, per_q):
        near_one(t)
        stage_two(near_start(t - 1), bufs[(t - 1) % 2])
    far_one(0, per_q % 2)
    stage_two(near_start(per_q - 1), bufs[(per_q - 1) % 2])

    def far_body(u, c):
        for r in range(FAR_UNROLL):
            far_one(FAR_UNROLL * u + r + 1, (per_q + r + 1) % 2)
            far_two(FAR_UNROLL * u + r, (per_q + r) % 2)
        return c

    lax.fori_loop(0, n_far // FAR_UNROLL, far_body, 0)
    for t in range(n_far - n_far % FAR_UNROLL, n_far):
        if t + 1 < n_far:
            far_one(t + 1, (per_q + t + 1) % 2)
        far_two(t, (per_q + t) % 2)

    lp = lam_ref[...]
    lam = (jnp.exp(jnp.sum(lp[0:1] * lp[1:2], axis=-1, keepdims=True))
           - jnp.exp(jnp.sum(lp[2:3] * lp[3:4], axis=-1, keepdims=True)) + lam_init)
    a0 = acc_ref[0]
    a1 = acc_ref[1]
    o = (a0[0:DIFF_V_DIM] / a0[DIFF_V_DIM:DIFF_V_DIM + 1]
         - lam * (a1[0:DIFF_V_DIM] / a1[DIFF_V_DIM:DIFF_V_DIM + 1]))
    ms = jnp.mean(o * o, axis=0, keepdims=True)
    o = o * lax.rsqrt(ms + NORM_EPS) * sg_ref[...] * (1.0 - lam_init)
    o_ref[...] = o.T.astype(o_ref.dtype)


def _diff_attn(qa, ka, vt, lam_p, subln_g, *, lam_init, online):
    s = qa.shape[2]
    bq = ATT_BQ
    assert bq % ATT_BK == 0 and s > bq
    kern = functools.partial(_diff_attn_kernel, lam_init=lam_init, online=online)
    per_q = bq // ATT_BK
    key = np.arange(per_q * ATT_BK).reshape(per_q, ATT_BK, 1)
    dnear = jnp.asarray(-np.abs(np.arange(bq).reshape(1, 1, bq) - key), F32)
    scratch = [
        pltpu.VMEM((2, V_AUG, bq), F32),
        pltpu.VMEM((2, 3, bq, 128), BF16),
    ]
    if online:
        scratch += [
            pltpu.VMEM((2, 1, bq), F32),
            pltpu.VMEM((2, ATT_BK, bq), F32),
            pltpu.VMEM((2, ATT_BK, bq), F32),
            pltpu.VMEM((2, 1, bq), F32),
            pltpu.VMEM((2, 1, bq), F32),
        ]
    else:
        scratch += [pltpu.VMEM((2, ATT_BK, bq), BF16), pltpu.VMEM((2, ATT_BK, bq), BF16)]
    return pl.pallas_call(
        kern,
        grid=(DIFF_HEADS, s // bq),
        in_specs=[
            pl.BlockSpec((4, DIFF_QK_DIM), lambda h, i: (0, 0)),
            pl.BlockSpec((DIFF_V_DIM, 1), lambda h, i: (0, 0)),
            pl.BlockSpec((per_q, ATT_BK, bq), lambda h, i: (0, 0, 0)),
            pl.BlockSpec((None, 2, bq, 128), lambda h, i: (h, 0, i, 0)),
            pl.BlockSpec((None, 2, s, 128), lambda h, i: (h, 0, 0, 0)),
            pl.BlockSpec((None, V_AUG, s), lambda h, i: (h, 0, 0)),
        ],
        out_specs=pl.BlockSpec((bq, DIFF_V_DIM), lambda h, i: (i, h)),
        out_shape=jax.ShapeDtypeStruct((s, DIFF_DIM), BF16),
        scratch_shapes=scratch,
        compiler_params=_cparams(("parallel", "arbitrary")),
        name="diff_attn_online" if online else "diff_attn",
    )(lam_p, subln_g.reshape(DIFF_V_DIM, 1), dnear, qa, ka, vt)


def _shift_rows(x, prev_row, next_row):
    n = x.shape[0]
    row = lax.broadcasted_iota(jnp.int32, x.shape, 0)
    xp = jnp.where(row == 0, prev_row, pltpu.roll(x, 1, 0))
    xn = jnp.where(row == n - 1, next_row, pltpu.roll(x, n - 1, 0))
    return xp, xn


def _halo_rows(prev_ref, next_ref, i, n_blocks):
    hp = prev_ref.shape[0]
    prev_row = prev_ref[hp - 1:hp, :].astype(F32)
    next_row = next_ref[0:1, :].astype(F32)
    prev_row = jnp.where(i > 0, prev_row, 0.0)
    next_row = jnp.where(i < n_blocks - 1, next_row, 0.0)
    return prev_row, next_row


def _rwkv_prep_kernel(
        x_ref, xp_ref, xn_ref, l_ref, lp_ref, ln_ref,
        mu_ref, mul_ref, w0_ref, w2_ref, a0_ref, a2_ref, g2_ref, kk_ref, ka_ref, rk_ref,
        bd_ref, tri_ref, triu_ref,
        v_out, g_out, bonus_out, rb_out, kb_out, kt_out, bt_out, kh_out, bh_out, gc_out):
    i = pl.program_id(0)
    nb = pl.num_programs(0)
    tm = x_ref.shape[0]

    x = x_ref[...].astype(F32)
    prev_row, next_row = _halo_rows(xp_ref, xn_ref, i, nb)
    xp, xn = _shift_rows(x, prev_row, next_row)
    mu = mu_ref[...]
    x = x + mu[0:1] * (xp - x) + mu[1:2] * (xn - x)
    lo = l_ref[...].astype(F32)
    prev_row, next_row = _halo_rows(lp_ref, ln_ref, i, nb)
    lop, lon = _shift_rows(lo, prev_row, next_row)
    mul = mul_ref[...]
    lo = lo + mul[0:1] * (lop - lo) + mul[1:2] * (lon - lo)

    r = x[:, 0:RWKV_DIM]
    k = x[:, RWKV_DIM:2 * RWKV_DIM]
    v = x[:, 2 * RWKV_DIM:3 * RWKV_DIM]
    tw = jnp.tanh(lo[:, 0:128]).astype(BF16)
    la = lo[:, 128:256].astype(BF16)
    lg = jax.nn.sigmoid(lo[:, 256:512]).astype(BF16)

    g = _dot(lg, g2_ref[...])
    kk = k * kk_ref[...]
    ss = _group_mean_sq(kk, bd_ref, 1.0)
    kk = kk * lax.rsqrt(jnp.maximum(ss, 1e-24))
    hi, mid, _ = _split3(r * k * rk_ref[...])
    bonus = (_dot(hi, bd_ref[...]) + _dot(mid, bd_ref[...])) * v

    g_out[...] = g.astype(g_out.dtype)
    bonus_out[...] = bonus.astype(bonus_out.dtype)
    for h in range(RWKV_HEADS):
        v_out[h] = v[:, 64 * h:64 * (h + 1)].astype(BF16)

    tri = tri_ref[...]
    triu = triu_ref[...]
    for d in range(2):
        wl = w0_ref[d:d + 1, :] + _dot(tw, w2_ref[d])
        z = -wl
        softplus = jnp.maximum(z, 0.0) + jnp.log(1.0 + jnp.exp(-jnp.abs(z)))
        logdec = -jnp.exp(-softplus - 0.5)
        a = jax.nn.sigmoid(a0_ref[d:d + 1, :] + _dot(la, a2_ref[d]))
        k_d = k * (1.0 + (a - 1.0) * ka_ref[...])
        b_d = kk * a
        pre = _dot_exactish(tri, logdec)
        suf = _dot_exactish(triu, logdec)
        if d == 0:
            lc, ex, rem = pre, pre - logdec, suf
        else:
            lc, ex, rem = suf + logdec, suf, pre - logdec
        e_neg = jnp.exp(-lc)
        e_rem = jnp.exp(rem)
        outs = (
            (rb_out, r * jnp.exp(lc)),
            (kb_out, kk * jnp.exp(ex)),
            (kt_out, k_d * e_neg),
            (bt_out, b_d * e_neg),
            (kh_out, k_d * e_rem),
            (bh_out, b_d * e_rem),
        )
        for ref, val in outs:
            val = val.astype(BF16)
            for h in range(RWKV_HEADS):
                ref[d, h] = val[:, 64 * h:64 * (h + 1)]
        tot = jnp.exp(lc + rem).reshape(tm // 8, 8, RWKV_DIM)[:, 0, :]
        for h in range(RWKV_HEADS):
            gc_out[d, h] = tot[:, 64 * h:64 * (h + 1)]


def _rwkv_prep(p, mu_rkv, mu_lora, w0, w2p, a0, a2p, g2p, k_k, k_a, r_k, bd, tri, triu, *, tm):
    s = p.shape[0]
    nb = s // tm
    hb = tm // 16
    last16 = s // 16 - 1
    rkv_w = 3 * RWKV_DIM
    c_rkv = COL_RKV // rkv_w
    c_lora = COL_LORA // LORA_PAD

    def prev_map(c):
        return lambda i: (jnp.maximum(i * hb - 1, 0), c)

    def next_map(c):
        return lambda i: (jnp.minimum((i + 1) * hb, last16), c)

    full = lambda *shape: pl.BlockSpec(shape, lambda i: (0,) * len(shape))
    hm = lambda: pl.BlockSpec((2, RWKV_HEADS, tm, 64), lambda i: (0, 0, i, 0))
    hm_shape = jax.ShapeDtypeStruct((2, RWKV_HEADS, s, 64), BF16)
    return pl.pallas_call(
        _rwkv_prep_kernel,
        grid=(nb,),
        in_specs=[
            pl.BlockSpec((tm, rkv_w), lambda i: (i, c_rkv)),
            pl.BlockSpec((16, rkv_w), prev_map(c_rkv)),
            pl.BlockSpec((16, rkv_w), next_map(c_rkv)),
            pl.BlockSpec((tm, LORA_PAD), lambda i: (i, c_lora)),
            pl.BlockSpec((16, LORA_PAD), prev_map(c_lora)),
            pl.BlockSpec((16, LORA_PAD), next_map(c_lora)),
            full(2, rkv_w), full(2, LORA_PAD),
            full(2, RWKV_DIM), full(2, 128, RWKV_DIM),
            full(2, RWKV_DIM), full(2, 128, RWKV_DIM),
            full(256, RWKV_DIM),
            full(1, RWKV_DIM), full(1, RWKV_DIM), full(1, RWKV_DIM),
            full(RWKV_DIM, RWKV_DIM), full(tm, tm), full(tm, tm),
        ],
        out_specs=[
            pl.BlockSpec((RWKV_HEADS, tm, 64), lambda i: (0, i, 0)),
            pl.BlockSpec((tm, RWKV_DIM), lambda i: (i, 0)),
            pl.BlockSpec((tm, RWKV_DIM), lambda i: (i, 0)),
            hm(), hm(), hm(), hm(), hm(), hm(),
            pl.BlockSpec((2, RWKV_HEADS, tm // 8, 64), lambda i: (0, 0, i, 0)),
        ],
        out_shape=[
            jax.ShapeDtypeStruct((RWKV_HEADS, s, 64), BF16),
            jax.ShapeDtypeStruct((s, RWKV_DIM), BF16),
            jax.ShapeDtypeStruct((s, RWKV_DIM), BF16),
            hm_shape, hm_shape, hm_shape, hm_shape, hm_shape, hm_shape,
            jax.ShapeDtypeStruct((2, RWKV_HEADS, s // 8, 64), F32),
        ],
        compiler_params=_cparams(("parallel",)),
        name="rwkv_prep",
    )(p, p, p, p, p, p, mu_rkv, mu_lora, w0, w2p, a0, a2p, g2p, k_k, k_a, r_k, bd, tri, triu)


def _bdot(a, b):
    return lax.dot_general(a, b, (((2,), (1,)), ((0,), (0,))), preferred_element_type=F32)


def _bdot_nt(a, b):
    return lax.dot_general(a, b, (((2,), (2,)), ((0,), (0,))), preferred_element_type=F32)


def _bdot_tn(a, b):
    return lax.dot_general(a, b, (((1,), (1,)), ((0,), (0,))), preferred_element_type=F32)


def _bdot_inv(a, b):
    return _bdot(a.astype(BF16), b.astype(BF16))


def _wkv_kernel(*refs):
    (vf_ref, vr_ref), ins, (gcf_ref, gcr_ref, yf_ref, yr_ref, s_ref) = refs[:2], refs[2:14], refs[14:]

    @pl.when(pl.program_id(0) == 0)
    def _():
        s_ref[...] = jnp.zeros_like(s_ref)

    _wkv_chunk(False, vf_ref, ins[0:6], gcf_ref, yf_ref, s_ref.at[0])
    _wkv_chunk(True, vr_ref, ins[6:12], gcr_ref, yr_ref, s_ref.at[1])


def _wkv_chunk(reverse, v_ref, scaled_refs, gc_ref, y_ref, s_ref):
    nh = RWKV_HEADS
    row = lax.broadcasted_iota(jnp.int32, (nh, CHUNK, CHUNK), 1)
    col = lax.broadcasted_iota(jnp.int32, (nh, CHUNK, CHUNK), 2)
    strict = (row < col) if reverse else (row > col)
    incl = (row <= col) if reverse else (row >= col)

    v = v_ref[...]
    rb, kb, kt, bt, kh, bh = [r[...] for r in scaled_refs]

    a_kk = jnp.where(strict, _bdot_nt(kb, kt), 0.0)
    l_mat = jnp.where(strict, _bdot_nt(kb, bt), 0.0)
    a_rk = jnp.where(incl, _bdot_nt(rb, kt), 0.0)
    a_rb = jnp.where(incl, _bdot_nt(rb, bt), 0.0)

    eye = jnp.where(row == col, 1.0, 0.0)
    t_inv = eye - l_mat
    pw = _bdot_inv(l_mat, l_mat)
    for step in range(5):
        t_inv = t_inv + _bdot_inv(t_inv, pw)
        if step < 4:
            pw = _bdot_inv(pw, pw)

    q1 = _bdot(a_kk.astype(BF16), v)
    q2 = _bdot(a_rk.astype(BF16), v)
    t_b = t_inv.astype(BF16)
    wk = _bdot(t_b, kb)
    uv = _bdot(t_b, q1.astype(BF16))
    a_rb_b = a_rb.astype(BF16)
    wk_b = wk.astype(BF16)
    uv_b = uv.astype(BF16)
    rw = rb.astype(F32) - _bdot(a_rb_b, wk_b)
    y0 = q2 - _bdot(a_rb_b, uv_b)
    m_mat = _bdot_tn(wk_b, bh)
    j_t = _bdot_tn(v, kh) - _bdot_tn(uv_b, bh)

    s_old = s_ref[...]
    s_hi = s_old.astype(BF16)
    s_lo = (s_old - s_hi.astype(F32)).astype(BF16)
    rw_b = rw.astype(BF16)
    m_b = m_mat.astype(BF16)
    y_ref[...] = _bdot_nt(rw_b, s_hi) + _bdot_nt(rw_b, s_lo) + y0
    s_ref[...] = s_old * gc_ref[:, 0:1, :] - (_bdot(s_hi, m_b) + _bdot(s_lo, m_b)) + j_t


def _wkv(v, rb, kb, kt, bt, kh, bh, gc):
    s = v.shape[1]
    nc = s // CHUNK
    scaled = (rb, kb, kt, bt, kh, bh)
    fwd = lambda rows: pl.BlockSpec((None, RWKV_HEADS, rows, 64), lambda c: (0, 0, c, 0))
    rev = lambda rows: pl.BlockSpec((None, RWKV_HEADS, rows, 64), lambda c: (1, 0, nc - 1 - c, 0))
    y_shape = jax.ShapeDtypeStruct((RWKV_HEADS, s, 64), F32)
    return pl.pallas_call(
        _wkv_kernel,
        grid=(nc,),
        in_specs=[
            pl.BlockSpec((RWKV_HEADS, CHUNK, 64), lambda c: (0, c, 0)),
            pl.BlockSpec((RWKV_HEADS, CHUNK, 64), lambda c: (0, nc - 1 - c, 0)),
            *[fwd(CHUNK) for _ in scaled], *[rev(CHUNK) for _ in scaled],
            fwd(8), rev(8),
        ],
        out_specs=[
            pl.BlockSpec((RWKV_HEADS, CHUNK, 64), lambda c: (0, c, 0)),
            pl.BlockSpec((RWKV_HEADS, CHUNK, 64), lambda c: (0, nc - 1 - c, 0)),
        ],
        out_shape=[y_shape, y_shape],
        scratch_shapes=[pltpu.VMEM((2, RWKV_HEADS, 64, 64), F32)],
        compiler_params=_cparams(("arbitrary",)),
        name="wkv7_chunked",
    )(v, v, *scaled, *scaled, gc, gc)


def _rwkv_post_kernel(yf_ref, yr_ref, g_ref, bonus_ref, lg_ref, lb_ref, o_ref):
    y = yf_ref[...] + yr_ref[...]
    mean = jnp.mean(y, axis=-1, keepdims=True)
    yc = y - mean
    var = jnp.mean(yc * yc, axis=-1, keepdims=True)
    yn = yc * lax.rsqrt(var + LNX_EPS)
    yt = jnp.concatenate([yn[h] for h in range(RWKV_HEADS)], axis=-1)
    out = (yt * lg_ref[...] + lb_ref[...] + bonus_ref[...].astype(F32)) * g_ref[...].astype(F32)
    o_ref[...] = out.astype(o_ref.dtype)


def _rwkv_post(y_fwd, y_rev, g, bonus, lnx_g, lnx_b, *, tm):
    s = g.shape[0]
    return pl.pallas_call(
        _rwkv_post_kernel,
        grid=(s // tm,),
        in_specs=[
            pl.BlockSpec((RWKV_HEADS, tm, 64), lambda i: (0, i, 0)),
            pl.BlockSpec((RWKV_HEADS, tm, 64), lambda i: (0, i, 0)),
            pl.BlockSpec((tm, RWKV_DIM), lambda i: (i, 0)),
            pl.BlockSpec((tm, RWKV_DIM), lambda i: (i, 0)),
            pl.BlockSpec((1, RWKV_DIM), lambda i: (0, 0)),
            pl.BlockSpec((1, RWKV_DIM), lambda i: (0, 0)),
        ],
        out_specs=pl.BlockSpec((tm, RWKV_DIM), lambda i: (i, 0)),
        out_shape=jax.ShapeDtypeStruct((s, RWKV_DIM), BF16),
        compiler_params=_cparams(("parallel",)),
        name="rwkv_post",
    )(y_fwd, y_rev, g, bonus, lnx_g.reshape(1, RWKV_DIM), lnx_b.reshape(1, RWKV_DIM))


def _mem_attn_kernel(q_ref, kv_ref, gq_ref, gk_ref, o_ref):
    for h in range(MEM_HEADS):
        sl = slice(MEM_HEAD_DIM * h, MEM_HEAD_DIM * (h + 1))
        q = q_ref[:, sl].astype(F32)
        q = q * lax.rsqrt(jnp.mean(q * q, axis=-1, keepdims=True) + NORM_EPS)
        q = q * gq_ref[...] * (MEM_HEAD_DIM ** -0.5)
        km = kv_ref[:, sl].astype(F32)
        km = km * lax.rsqrt(jnp.mean(km * km, axis=-1, keepdims=True) + NORM_EPS)
        km = km * gk_ref[...]
        vm = kv_ref[:, MEM_DIM + MEM_HEAD_DIM * h:MEM_DIM + MEM_HEAD_DIM * (h + 1)]
        s = lax.dot_general(q.astype(BF16), km.astype(BF16), (((1,), (1,)), ((), ())),
                            preferred_element_type=F32)
        s = s - jnp.max(s, axis=-1, keepdims=True)
        e = jnp.exp(s)
        pr = e / jnp.sum(e, axis=-1, keepdims=True)
        o_ref[:, sl] = _dot(pr.astype(BF16), vm).astype(o_ref.dtype)


def _mem_attn(p, kv, gq, gk, *, tm):
    s = p.shape[0]
    return pl.pallas_call(
        _mem_attn_kernel,
        grid=(s // tm,),
        in_specs=[
            pl.BlockSpec((tm, MEM_DIM), lambda i: (i, COL_MEM // MEM_DIM)),
            pl.BlockSpec((N_MEM, 2 * MEM_DIM), lambda i: (0, 0)),
            pl.BlockSpec((1, MEM_HEAD_DIM), lambda i: (0, 0)),
            pl.BlockSpec((1, MEM_HEAD_DIM), lambda i: (0, 0)),
        ],
        out_specs=pl.BlockSpec((tm, MEM_DIM), lambda i: (i, 0)),
        out_shape=jax.ShapeDtypeStruct((s, MEM_DIM), BF16),
        compiler_params=_cparams(("parallel",)),
        name="mem_attn",
    )(p, kv, gq, gk)


def _merge_kernel(o0_ref, o1_ref, o2_ref, g0_ref, g1_ref, g2_ref, w_ref, m_ref):
    acc = jax.nn.sigmoid(g0_ref[...].astype(F32)) * _dot(o0_ref[...], w_ref[0])
    acc = acc + jax.nn.sigmoid(g1_ref[...].astype(F32)) * _dot(o1_ref[...], w_ref[1])
    acc = acc + jax.nn.sigmoid(g2_ref[...].astype(F32)) * _dot(o2_ref[...], w_ref[2])
    m_ref[...] = acc.astype(m_ref.dtype)


def _merge(o_diff, o_rwkv, o_mem, p, w_branch, *, tm, tn):
    s = p.shape[0]
    gate_blk = COL_GATE // tn
    per = D_MODEL // tn
    o_spec = lambda: pl.BlockSpec((tm, 1024), lambda i, j: (i, 0))
    g_spec = lambda b: pl.BlockSpec((tm, tn), lambda i, j: (i, gate_blk + b * per + j))
    return pl.pallas_call(
        _merge_kernel,
        grid=(s // tm, per),
        in_specs=[
            o_spec(), o_spec(), o_spec(),
            g_spec(0), g_spec(1), g_spec(2),
            pl.BlockSpec((N_BRANCH, 1024, tn), lambda i, j: (0, 0, j)),
        ],
        out_specs=pl.BlockSpec((tm, tn), lambda i, j: (i, j)),
        out_shape=jax.ShapeDtypeStruct((s, D_MODEL), BF16),
        compiler_params=_cparams(("parallel", "arbitrary")),
        name="merge",
    )(o_diff, o_rwkv, o_mem, p, p, p, w_branch)


def _glu_kernel(ug_ref, ugp_ref, ugn_ref, uv_ref, cw_ref, cb_ref, o_ref):
    i = pl.program_id(0)
    nb = pl.num_programs(0)
    g = ug_ref[...].astype(F32)
    prev_row, next_row = _halo_rows(ugp_ref, ugn_ref, i, nb)
    gp, gn = _shift_rows(g, prev_row, next_row)
    cw = cw_ref[...]
    conv = cw[0:1] * gp + cw[1:2] * g + cw[2:3] * gn + cb_ref[...]
    act = conv * jax.nn.sigmoid(conv) * uv_ref[...].astype(F32)
    o_ref[...] = act.astype(o_ref.dtype)


def _glu(u, conv_w, conv_b, *, tm, tc):
    s = u.shape[0]
    nj = D_FF // tc
    hb = tm // 16
    last16 = s // 16 - 1
    return pl.pallas_call(
        _glu_kernel,
        grid=(s // tm, nj),
        in_specs=[
            pl.BlockSpec((tm, tc), lambda i, j: (i, j)),
            pl.BlockSpec((16, tc), lambda i, j: (jnp.maximum(i * hb - 1, 0), j)),
            pl.BlockSpec((16, tc), lambda i, j: (jnp.minimum((i + 1) * hb, last16), j)),
            pl.BlockSpec((tm, tc), lambda i, j: (i, nj + j)),
            pl.BlockSpec((3, tc), lambda i, j: (0, j)),
            pl.BlockSpec((1, tc), lambda i, j: (0, j)),
        ],
        out_specs=pl.BlockSpec((tm, tc), lambda i, j: (i, j)),
        out_shape=jax.ShapeDtypeStruct((s, D_FF), BF16),
        compiler_params=_cparams(("parallel", "parallel")),
        name="conv_glu",
    )(u, u, u, u, conv_w, conv_b.reshape(1, D_FF))


def _block_ones(n, group):
    idx = jnp.arange(n) // group
    return (idx[:, None] == idx[None, :]).astype(BF16)


def _chunk_tri(n):
    idx = jnp.arange(n)
    same = (idx[:, None] // CHUNK) == (idx[None, :] // CHUNK)
    lower_incl = same & (idx[:, None] >= idx[None, :])
    upper_strict = same & (idx[:, None] < idx[None, :])
    return lower_incl.astype(BF16), upper_strict.astype(BF16)


def _pad_rows(w, rows_before, total):
    n = w.shape[-1]
    out = jnp.zeros((total, n), w.dtype)
    return lax.dynamic_update_slice(out, w, (rows_before, 0))


def _pad_w_in(w):
    lora0 = COL_RKV + 3 * RWKV_DIM
    lora1 = lora0 + LORA_COLS
    pad = jnp.zeros((w.shape[0], LORA_PAD - LORA_COLS), w.dtype)
    return jnp.concatenate([w[:, :lora0], w[:, lora1:], w[:, lora0:lora1], pad], axis=1).astype(BF16)


def kernel(x, mem, attn_norm_g, w_in, diff_qk_g, diff_lambda, diff_subln_g, rwkv_mu, rwkv_w0,
           rwkv_w2, rwkv_a0, rwkv_a2, rwkv_g2, rwkv_k_k, rwkv_k_a, rwkv_r_k, rwkv_lnx_g,
           rwkv_lnx_b, mem_norm_g, w_mem_kv, mem_qk_g, w_branch, w_out, ffn_norm_g, w_ffn_up,
           ffn_conv_w, ffn_conv_b, w_ffn_down):
    b, s, d = x.shape
    assert b == 1 and d == D_MODEL and s % ATT_BQ == 0
    xs = x.reshape(s, d)
    mem2 = mem.reshape(N_MEM, d)
    prep_tm = 256
    bd64 = _block_ones(1024, 64)
    tri, triu = _chunk_tri(prep_tm)
    n_rwkv_main = 3 * RWKV_DIM

    for l in range(DEPTH):
        lam_init = 0.8 - 0.6 * math.exp(-0.3 * l)
        p = _rms_mm(xs, attn_norm_g[l], _pad_w_in(w_in[l]), tm=512, tn=1536, name="rms_w_in")

        gq = jnp.tile(diff_qk_g[l, 0].reshape(1, 128), (1, DIFF_HEADS))
        gk = jnp.tile(diff_qk_g[l, 1].reshape(1, 128), (1, DIFF_HEADS))
        qc, kc, score_bound = _attn_consts(diff_qk_g[l])
        qa, ka, vt = _diff_prep(p, gq, gk, qc, kc, bd64, tm=256)
        o_diff = lax.cond(
            score_bound <= SCORE_BOUND_MAX,
            functools.partial(_diff_attn, lam_init=lam_init, online=False),
            functools.partial(_diff_attn, lam_init=lam_init, online=True),
            qa, ka, vt, diff_lambda[l], diff_subln_g[l])

        mu = rwkv_mu[l]
        mu_rkv = mu[:, :n_rwkv_main]
        mu_lora = jnp.pad(mu[:, n_rwkv_main:], ((0, 0), (0, LORA_PAD - LORA_COLS)))
        w2p = jnp.stack([_pad_rows(rwkv_w2[l, dd], 64 * dd, 128) for dd in range(2)]).astype(BF16)
        a2p = jnp.stack([_pad_rows(rwkv_a2[l, dd], 64 * dd, 128) for dd in range(2)]).astype(BF16)
        g2p = _pad_rows(rwkv_g2[l], 0, 256).astype(BF16)
        (v_h, g_tok, bonus, rb, kb, kt, bt, kh, bh, gc) = _rwkv_prep(
            p, mu_rkv, mu_lora, rwkv_w0[l], w2p, rwkv_a0[l], a2p, g2p,
            rwkv_k_k[l].reshape(1, RWKV_DIM), rwkv_k_a[l].reshape(1, RWKV_DIM),
            rwkv_r_k[l].reshape(1, RWKV_DIM), bd64, tri, triu, tm=prep_tm)
        y_fwd, y_rev = _wkv(v_h, rb, kb, kt, bt, kh, bh, gc)
        o_rwkv = _rwkv_post(y_fwd, y_rev, g_tok, bonus, rwkv_lnx_g[l], rwkv_lnx_b[l], tm=256)

        kv = _rms_mm(mem2, mem_norm_g[l], w_mem_kv[l].astype(BF16), tm=N_MEM, tn=1024,
                     name="rms_mem_kv")
        o_mem = _mem_attn(p, kv, mem_qk_g[l, 0].reshape(1, MEM_HEAD_DIM),
                          mem_qk_g[l, 1].reshape(1, MEM_HEAD_DIM), tm=512)

        merged = _merge(o_diff, o_rwkv, o_mem, p, w_branch[l].astype(BF16), tm=512, tn=512)
        xs = _mm_res(merged, w_out[l].astype(BF16), xs, tm=512, tn=1024, name="w_out_res")

        u = _rms_mm(xs, ffn_norm_g[l], w_ffn_up[l].astype(BF16), tm=512, tn=1024, name="rms_ffn_up")
        act = _glu(u, ffn_conv_w[l], ffn_conv_b[l], tm=512, tc=512)
        xs = _mm_res(act, w_ffn_down[l].astype(BF16), xs, tm=512, tn=512, name="ffn_down_res")

    return xs.reshape(b, s, d)
```

```python
import functools
import math

import jax
import jax.numpy as jnp
import numpy as np
from jax import lax
from jax.experimental import pallas as pl
from jax.experimental.pallas import tpu as pltpu

F32 = jnp.float32
BF16 = jnp.bfloat16

D_MODEL = 2048
DEPTH = 2
DIFF_HEADS = 8
DIFF_QK_DIM = 64
DIFF_V_DIM = 128
DIFF_DIM = 1024
RWKV_HEADS = 16
RWKV_HEAD_DIM = 64
RWKV_DIM = 1024
DECAY_LORA = 64
AAA_LORA = 64
GATE_LORA = 160
LORA_COLS = 2 * DECAY_LORA + 2 * AAA_LORA + GATE_LORA
LORA_PAD = 512
N_MEM = 256
MEM_HEADS = 4
MEM_HEAD_DIM = 256
MEM_DIM = 1024
N_BRANCH = 3
D_FF = 5632
NORM_EPS = 1e-6
LNX_EPS = 64e-5

COL_DIFF = 0
COL_RKV = 3 * DIFF_DIM
COL_MEM = COL_RKV + 3 * RWKV_DIM
COL_GATE = COL_MEM + MEM_DIM
COL_LORA = COL_GATE + N_BRANCH * D_MODEL
N_IN_PAD = COL_LORA + LORA_PAD

CHUNK = 64
LOG2E = 1.4426950408889634
N_POS_COLS = 12
SCORE_BOUND_MAX = 30.0
ATT_BQ = 512
ATT_BK = 256
FAR_UNROLL = 6
VMEM_LIMIT = 48 * 1024 * 1024


def _cparams(sem):
    return pltpu.CompilerParams(dimension_semantics=sem, vmem_limit_bytes=VMEM_LIMIT)


def _split2(x):
    hi = x.astype(BF16)
    return hi, (x - hi.astype(F32)).astype(BF16)


def _dot(a, b):
    return jnp.dot(a, b, preferred_element_type=F32)


def _dot_exactish(a_bf16_exact, x_f32):
    hi, lo = _split2(x_f32)
    return _dot(a_bf16_exact, hi) + _dot(a_bf16_exact, lo)


def _rms_mm_kernel(x_ref, g_ref, w_ref, o_ref, h_ref):
    @pl.when(pl.program_id(1) == 0)
    def _():
        x = x_ref[...]
        ms = jnp.mean(x * x, axis=-1, keepdims=True)
        h_ref[...] = (x * lax.rsqrt(ms + NORM_EPS) * g_ref[...]).astype(BF16)

    o_ref[...] = _dot(h_ref[...], w_ref[...]).astype(o_ref.dtype)


def _rms_mm(x, g, w, *, tm, tn, name):
    m, k = x.shape
    n = w.shape[1]
    return pl.pallas_call(
        _rms_mm_kernel,
        grid=(m // tm, n // tn),
        in_specs=[
            pl.BlockSpec((tm, k), lambda i, j: (i, 0)),
            pl.BlockSpec((1, k), lambda i, j: (0, 0)),
            pl.BlockSpec((k, tn), lambda i, j: (0, j)),
        ],
        out_specs=pl.BlockSpec((tm, tn), lambda i, j: (i, j)),
        out_shape=jax.ShapeDtypeStruct((m, n), BF16),
        scratch_shapes=[pltpu.VMEM((tm, k), BF16)],
        compiler_params=_cparams(("parallel", "arbitrary")),
        name=name,
    )(x, g.reshape(1, k), w)


def _mm_res_kernel(a_ref, w_ref, r_ref, o_ref):
    o_ref[...] = r_ref[...] + _dot(a_ref[...], w_ref[...])


def _mm_res(a, w, res, *, tm, tn, name):
    m, k = a.shape
    n = w.shape[1]
    return pl.pallas_call(
        _mm_res_kernel,
        grid=(m // tm, n // tn),
        in_specs=[
            pl.BlockSpec((tm, k), lambda i, j: (i, 0)),
            pl.BlockSpec((k, tn), lambda i, j: (0, j)),
            pl.BlockSpec((tm, tn), lambda i, j: (i, j)),
        ],
        out_specs=pl.BlockSpec((tm, tn), lambda i, j: (i, j)),
        out_shape=jax.ShapeDtypeStruct((m, n), F32),
        compiler_params=_cparams(("parallel", "arbitrary")),
        name=name,
    )(a, w, res)


def _group_mean_sq(x, bd_ref, group):
    hi, lo = _split2(x * x)
    bd = bd_ref[...]
    return (_dot(hi, bd) + _dot(lo, bd)) * (1.0 / group)


def _aug_base(mp):
    return 64 if mp == 0 else 0


def _diff_prep_kernel(q_ref, k_ref, v_ref, gq_ref, gk_ref, qc_ref, kc_ref, bd_ref,
                      qa_ref, ka_ref, vt_ref):
    tm = q_ref.shape[0]
    row0 = pl.program_id(0) * tm
    lane = lax.broadcasted_iota(jnp.int32, (tm, 128), 1)
    pos = row0 + lax.broadcasted_iota(jnp.int32, (tm, 128), 0)
    pos_lo = (pos & 127).astype(F32)
    pos_hi = (pos >> 7).astype(F32)

    q = q_ref[...].astype(F32)
    qn = q * lax.rsqrt(_group_mean_sq(q, bd_ref, DIFF_QK_DIM) + NORM_EPS)
    qn = qn * gq_ref[...] * (DIFF_QK_DIM ** -0.5 * LOG2E)
    k = k_ref[...].astype(F32)
    kn = k * lax.rsqrt(_group_mean_sq(k, bd_ref, DIFF_QK_DIM) + NORM_EPS)
    kn = kn * gk_ref[...]

    for h in range(DIFF_HEADS):
        qh = qn[:, 128 * h:128 * (h + 1)]
        kh = kn[:, 128 * h:128 * (h + 1)]
        for mp in range(2):
            a0 = _aug_base(mp)
            g = 2 * h + mp
            is_data = (lane < 64) if mp == 0 else (lane >= 64)
            aug_q = jnp.where(lane < a0 + 3, pos_lo,
                              jnp.where(lane < a0 + 6, pos_hi, qc_ref[g:g + 1, :]))
            aug_k = jnp.where((lane >= a0 + 6) & (lane < a0 + 9), pos_lo,
                              jnp.where((lane >= a0 + 9) & (lane < a0 + 12), pos_hi,
                                        kc_ref[g:g + 1, :]))
            qa_ref[h, mp] = jnp.where(is_data, qh, aug_q).astype(BF16)
            ka_ref[h, mp] = jnp.where(is_data, kh, aug_k).astype(BF16)

    vt = v_ref[...].astype(F32).T
    for h in range(DIFF_HEADS):
        vt_ref[h] = vt[128 * h:128 * (h + 1), :].astype(BF16)


def _bf16_split3_const(x):
    parts = []
    for _ in range(3):
        part = float(np.asarray(x, np.float32).astype(BF16).astype(np.float32))
        parts.append(part)
        x = x - part
    return parts


def _attn_consts(qk_g):
    l_parts = _bf16_split3_const(LOG2E)
    m_nat = 8.0 * jnp.max(jnp.abs(qk_g[0]), axis=-1) * jnp.max(jnp.abs(qk_g[1]), axis=-1)
    qc = np.zeros((2 * DIFF_HEADS, 128), np.float32)
    kc = np.zeros((2 * DIFF_HEADS, 128), np.float32)
    bound_lane = np.zeros((2, 2 * DIFF_HEADS, 128), np.float32)
    for h in range(DIFF_HEADS):
        slope = 2.0 ** (-(h + 1))
        for mp in range(2):
            a0 = _aug_base(mp)
            g = 2 * h + mp
            for t, lp in enumerate(l_parts):
                qc[g, a0 + 6 + t] = slope * lp
                qc[g, a0 + 9 + t] = 128.0 * slope * lp
                kc[g, a0 + t] = -slope * lp
                kc[g, a0 + 3 + t] = -128.0 * slope * lp
            bound_lane[mp, g, a0 + N_POS_COLS] = 1.0
            kc[g, a0 + N_POS_COLS] = 1.0
    m2 = -m_nat * LOG2E
    qc = qc + m2[0] * bound_lane[0] + m2[1] * bound_lane[1]
    return qc, jnp.asarray(kc), jnp.max(m_nat)


def _diff_prep(p, gq, gk, qc, kc, bd, *, tm):
    s = p.shape[0]
    return pl.pallas_call(
        _diff_prep_kernel,
        grid=(s // tm,),
        in_specs=[
            pl.BlockSpec((tm, DIFF_DIM), lambda i: (i, 0)),
            pl.BlockSpec((tm, DIFF_DIM), lambda i: (i, 1)),
            pl.BlockSpec((tm, DIFF_DIM), lambda i: (i, 2)),
            pl.BlockSpec((1, DIFF_DIM), lambda i: (0, 0)),
            pl.BlockSpec((1, DIFF_DIM), lambda i: (0, 0)),
            pl.BlockSpec((2 * DIFF_HEADS, 128), lambda i: (0, 0)),
            pl.BlockSpec((2 * DIFF_HEADS, 128), lambda i: (0, 0)),
            pl.BlockSpec((DIFF_DIM, DIFF_DIM), lambda i: (0, 0)),
        ],
        out_specs=[
            pl.BlockSpec((DIFF_HEADS, 2, tm, 128), lambda i: (0, 0, i, 0)),
            pl.BlockSpec((DIFF_HEADS, 2, tm, 128), lambda i: (0, 0, i, 0)),
            pl.BlockSpec((DIFF_HEADS, DIFF_V_DIM, tm), lambda i: (0, 0, i)),
        ],
        out_shape=[
            jax.ShapeDtypeStruct((DIFF_HEADS, 2, s, 128), BF16),
            jax.ShapeDtypeStruct((DIFF_HEADS, 2, s, 128), BF16),
            jax.ShapeDtypeStruct((DIFF_HEADS, DIFF_V_DIM, s), BF16),
        ],
        compiler_params=_cparams(("parallel",)),
        name="diff_prep",
    )(p, p, p, gq, gk, qc, kc, bd)


def _diff_attn_kernel(lam_ref, sg_ref, dnear_ref, qa_ref, ka_ref, vt_ref, o_ref,
                      acc_ref, l_ref, qv_ref, *mode_refs, lam_init, online):
    h = pl.program_id(0)
    i = pl.program_id(1)
    bq = qa_ref.shape[1]
    s_len = ka_ref.shape[1]
    bk = ATT_BK
    n_chunks = s_len // bk
    per_q = bq // bk
    n_far = n_chunks - per_q
    j_lo = i * per_q
    q0 = i * bq
    slope2 = jnp.exp2(-jnp.full((1, 1), h + 1, jnp.int32).astype(F32)) * LOG2E
    if online:
        m_ref, sa_ref, sb_ref, mxa_ref, mxb_ref = mode_refs
        m_ref[...] = jnp.full_like(m_ref, -1e30)
        bufs = ((sa_ref, mxa_ref), (sb_ref, mxb_ref))
    else:
        bufs = mode_refs
    acc_ref[...] = jnp.zeros_like(acc_ref)
    l_ref[...] = jnp.zeros_like(l_ref)

    def sum8(p):
        return jnp.sum(p.reshape(bk // 8, 8, bq), axis=0)

    lane = lax.broadcasted_iota(jnp.int32, (1, 128), 1)
    for mp in range(2):
        q = qa_ref[mp]
        pos_lane = (lane >= _aug_base(mp)) & (lane < _aug_base(mp) + N_POS_COLS)
        qv_ref[mp, 0] = q
        qv_ref[mp, 1] = jnp.where(pos_lane, -q, q)
        qv_ref[mp, 2] = jnp.where(pos_lane, jnp.zeros_like(q), q)

    def stage_one(k0, variant, bias, buf):
        for mp in range(2):
            kc = ka_ref[mp, pl.ds(k0, bk), :]
            s = lax.dot_general(kc, qv_ref[mp, variant], (((1,), (1,)), ((), ())),
                                preferred_element_type=F32)
            if bias is not None:
                s = s + bias
            if online:
                buf[0][mp] = s
                buf[1][mp] = jnp.max(s, axis=0, keepdims=True)
            else:
                buf[mp] = jnp.exp2(s).astype(BF16)

    def stage_two(k0, buf):
        vt_c = vt_ref[:, pl.ds(k0, bk)]
        for mp in range(2):
            if online:
                m_old = m_ref[mp]
                m_new = jnp.maximum(m_old, buf[1][mp])
                alpha = jnp.exp2(m_old - m_new)
                p = jnp.exp2(buf[0][mp] - m_new).astype(BF16)
                l_ref[mp] = l_ref[mp] * alpha + sum8(p.astype(F32))
                acc_ref[mp] = acc_ref[mp] * alpha + _dot(vt_c, p)
                m_ref[mp] = m_new
            else:
                p = buf[mp]
                l_ref[mp] += sum8(p.astype(F32))
                acc_ref[mp] += _dot(vt_c, p)

    def near_start(t):
        return pl.multiple_of(q0 + t * bk, bk)

    def far_start(t):
        t = jnp.minimum(t, n_far - 1)
        j = jnp.where(t < j_lo, t, t + per_q)
        return pl.multiple_of(j * bk, bk), (t >= j_lo).astype(jnp.int32)

    def near_one(t):
        stage_one(near_start(t), 2, slope2 * dnear_ref[t], bufs[t % 2])

    def far_one(t, parity):
        k0, after = far_start(t)
        stage_one(k0, after, None, bufs[parity])

    def far_two(t, parity):
        stage_two(far_start(t)[0], bufs[parity])

    near_one(0)
    for t in range(1, per_q):
        near_one(t)
        stage_two(near_start(t - 1), bufs[(t - 1) % 2])
    far_one(0, per_q % 2)
    stage_two(near_start(per_q - 1), bufs[(per_q - 1) % 2])

    def far_body(u, c):
        for r in range(FAR_UNROLL):
            far_one(FAR_UNROLL * u + r + 1, (per_q + r + 1) % 2)
            far_two(FAR_UNROLL * u + r, (per_q + r) % 2)
        return c

    lax.fori_loop(0, n_far // FAR_UNROLL, far_body, 0)
    for t in range(n_far - n_far % FAR_UNROLL, n_far):
        if t + 1 < n_far:
            far_one(t + 1, (per_q + t + 1) % 2)
        far_two(t, (per_q + t) % 2)

    lp = lam_ref[...]
    lam = (jnp.exp(jnp.sum(lp[0:1] * lp[1:2], axis=-1, keepdims=True))
           - jnp.exp(jnp.sum(lp[2:3] * lp[3:4], axis=-1, keepdims=True)) + lam_init)
    l0 = jnp.sum(l_ref[0], axis=0, keepdims=True)
    l1 = jnp.sum(l_ref[1], axis=0, keepdims=True)
    o = acc_ref[0] / l0 - lam * (acc_ref[1] / l1)
    ms = jnp.mean(o * o, axis=0, keepdims=True)
    o = o * lax.rsqrt(ms + NORM_EPS) * sg_ref[...] * (1.0 - lam_init)
    o_ref[...] = o.T.astype(o_ref.dtype)


def _diff_attn(qa, ka, vt, lam_p, subln_g, *, lam_init, online):
    s = qa.shape[2]
    bq = ATT_BQ
    assert bq % ATT_BK == 0 and s > bq
    kern = functools.partial(_diff_attn_kernel, lam_init=lam_init, online=online)
    per_q = bq // ATT_BK
    key = np.arange(per_q * ATT_BK).reshape(per_q, ATT_BK, 1)
    dnear = jnp.asarray(-np.abs(np.arange(bq).reshape(1, 1, bq) - key), F32)
    scratch = [
        pltpu.VMEM((2, DIFF_V_DIM, bq), F32),
        pltpu.VMEM((2, 8, bq), F32),
        pltpu.VMEM((2, 3, bq, 128), BF16),
    ]
    if online:
        scratch += [
            pltpu.VMEM((2, 1, bq), F32),
            pltpu.VMEM((2, ATT_BK, bq), F32),
            pltpu.VMEM((2, ATT_BK, bq), F32),
            pltpu.VMEM((2, 1, bq), F32),
            pltpu.VMEM((2, 1, bq), F32),
        ]
    else:
        scratch += [pltpu.VMEM((2, ATT_BK, bq), BF16), pltpu.VMEM((2, ATT_BK, bq), BF16)]
    return pl.pallas_call(
        kern,
        grid=(DIFF_HEADS, s // bq),
        in_specs=[
            pl.BlockSpec((4, DIFF_QK_DIM), lambda h, i: (0, 0)),
            pl.BlockSpec((DIFF_V_DIM, 1), lambda h, i: (0, 0)),
            pl.BlockSpec((per_q, ATT_BK, bq), lambda h, i: (0, 0, 0)),
            pl.BlockSpec((None, 2, bq, 128), lambda h, i: (h, 0, i, 0)),
            pl.BlockSpec((None, 2, s, 128), lambda h, i: (h, 0, 0, 0)),
            pl.BlockSpec((None, DIFF_V_DIM, s), lambda h, i: (h, 0, 0)),
        ],
        out_specs=pl.BlockSpec((bq, DIFF_V_DIM), lambda h, i: (i, h)),
        out_shape=jax.ShapeDtypeStruct((s, DIFF_DIM), BF16),
        scratch_shapes=scratch,
        compiler_params=_cparams(("parallel", "arbitrary")),
        name="diff_attn_online" if online else "diff_attn",
    )(lam_p, subln_g.reshape(DIFF_V_DIM, 1), dnear, qa, ka, vt)


def _shift_rows(x, prev_row, next_row):
    n = x.shape[0]
    row = lax.broadcasted_iota(jnp.int32, x.shape, 0)
    xp = jnp.where(row == 0, prev_row, pltpu.roll(x, 1, 0))
    xn = jnp.where(row == n - 1, next_row, pltpu.roll(x, n - 1, 0))
    return xp, xn


def _halo_rows(prev_ref, next_ref, i, n_blocks):
    hp = prev_ref.shape[0]
    prev_row = prev_ref[hp - 1:hp, :].astype(F32)
    next_row = next_ref[0:1, :].astype(F32)
    prev_row = jnp.where(i > 0, prev_row, 0.0)
    next_row = jnp.where(i < n_blocks - 1, next_row, 0.0)
    return prev_row, next_row


def _rwkv_prep_kernel(
        x_ref, xp_ref, xn_ref, l_ref, lp_ref, ln_ref,
        mu_ref, mul_ref, w0_ref, w2_ref, a0_ref, a2_ref, g2_ref, kk_ref, ka_ref, rk_ref,
        bd_ref, tri_ref,
        v_out, g_out, bonus_out, rb_out, kb_out, kt_out, bt_out, kh_out, bh_out, gc_out):
    i = pl.program_id(0)
    nb = pl.num_programs(0)
    tm = x_ref.shape[0]

    x = x_ref[...].astype(F32)
    prev_row, next_row = _halo_rows(xp_ref, xn_ref, i, nb)
    xp, xn = _shift_rows(x, prev_row, next_row)
    mu = mu_ref[...]
    x = x + mu[0:1] * (xp - x) + mu[1:2] * (xn - x)
    lo = l_ref[...].astype(F32)
    prev_row, next_row = _halo_rows(lp_ref, ln_ref, i, nb)
    lop, lon = _shift_rows(lo, prev_row, next_row)
    mul = mul_ref[...]
    lo = lo + mul[0:1] * (lop - lo) + mul[1:2] * (lon - lo)

    r = x[:, 0:RWKV_DIM]
    k = x[:, RWKV_DIM:2 * RWKV_DIM]
    v = x[:, 2 * RWKV_DIM:3 * RWKV_DIM]
    tw = jnp.tanh(lo[:, 0:128]).astype(BF16)
    la = lo[:, 128:256].astype(BF16)
    lg = jax.nn.sigmoid(lo[:, 256:512]).astype(BF16)

    g = _dot(lg, g2_ref[...])
    kk = k * kk_ref[...]
    ss = _group_mean_sq(kk, bd_ref, 1.0)
    kk = kk * lax.rsqrt(jnp.maximum(ss, 1e-24))
    hi, lo_part = _split2(r * k * rk_ref[...])
    bonus = (_dot(hi, bd_ref[...]) + _dot(lo_part, bd_ref[...])) * v

    g_out[...] = g.astype(g_out.dtype)
    bonus_out[...] = bonus.astype(bonus_out.dtype)
    for h in range(RWKV_HEADS):
        v_out[h] = v[:, 64 * h:64 * (h + 1)].astype(BF16)

    tri = tri_ref[...]
    for d in range(2):
        wl = w0_ref[d:d + 1, :] + _dot(tw, w2_ref[d])
        logdec = -math.exp(-0.5) * jax.nn.sigmoid(wl)
        a = jax.nn.sigmoid(a0_ref[d:d + 1, :] + _dot(la, a2_ref[d]))
        k_d = k * (1.0 + (a - 1.0) * ka_ref[...])
        b_d = kk * a
        pre = _dot_exactish(tri, logdec)
        chunk_tot = jnp.broadcast_to(
            pre.reshape(tm // CHUNK, CHUNK, RWKV_DIM)[:, CHUNK - 1:CHUNK, :],
            (tm // CHUNK, CHUNK, RWKV_DIM)).reshape(tm, RWKV_DIM)
        suf = chunk_tot - pre
        if d == 0:
            lc, ex, rem = pre, pre - logdec, suf
        else:
            lc, ex, rem = suf + logdec, suf, pre - logdec
        e_neg = jnp.exp(-lc)
        e_rem = jnp.exp(rem)
        outs = (
            (rb_out, r * jnp.exp(lc)),
            (kb_out, kk * jnp.exp(ex)),
            (kt_out, k_d * e_neg),
            (bt_out, b_d * e_neg),
            (kh_out, k_d * e_rem),
            (bh_out, b_d * e_rem),
        )
        for ref, val in outs:
            val = val.astype(BF16)
            for h in range(RWKV_HEADS):
                ref[d, h] = val[:, 64 * h:64 * (h + 1)]
        tot = jnp.exp(chunk_tot).reshape(tm // 8, 8, RWKV_DIM)[:, 0, :]
        for h in range(RWKV_HEADS):
            gc_out[d, h] = tot[:, 64 * h:64 * (h + 1)]


def _rwkv_prep(p, mu_rkv, mu_lora, w0, w2p, a0, a2p, g2p, k_k, k_a, r_k, bd, tri, *, tm):
    s = p.shape[0]
    nb = s // tm
    hb = tm // 16
    last16 = s // 16 - 1
    rkv_w = 3 * RWKV_DIM
    c_rkv = COL_RKV // rkv_w
    c_lora = COL_LORA // LORA_PAD

    def prev_map(c):
        return lambda i: (jnp.maximum(i * hb - 1, 0), c)

    def next_map(c):
        return lambda i: (jnp.minimum((i + 1) * hb, last16), c)

    full = lambda *shape: pl.BlockSpec(shape, lambda i: (0,) * len(shape))
    hm = lambda: pl.BlockSpec((2, RWKV_HEADS, tm, 64), lambda i: (0, 0, i, 0))
    hm_shape = jax.ShapeDtypeStruct((2, RWKV_HEADS, s, 64), BF16)
    return pl.pallas_call(
        _rwkv_prep_kernel,
        grid=(nb,),
        in_specs=[
            pl.BlockSpec((tm, rkv_w), lambda i: (i, c_rkv)),
            pl.BlockSpec((16, rkv_w), prev_map(c_rkv)),
            pl.BlockSpec((16, rkv_w), next_map(c_rkv)),
            pl.BlockSpec((tm, LORA_PAD), lambda i: (i, c_lora)),
            pl.BlockSpec((16, LORA_PAD), prev_map(c_lora)),
            pl.BlockSpec((16, LORA_PAD), next_map(c_lora)),
            full(2, rkv_w), full(2, LORA_PAD),
            full(2, RWKV_DIM), full(2, 128, RWKV_DIM),
            full(2, RWKV_DIM), full(2, 128, RWKV_DIM),
            full(256, RWKV_DIM),
            full(1, RWKV_DIM), full(1, RWKV_DIM), full(1, RWKV_DIM),
            full(RWKV_DIM, RWKV_DIM), full(tm, tm),
        ],
        out_specs=[
            pl.BlockSpec((RWKV_HEADS, tm, 64), lambda i: (0, i, 0)),
            pl.BlockSpec((tm, RWKV_DIM), lambda i: (i, 0)),
            pl.BlockSpec((tm, RWKV_DIM), lambda i: (i, 0)),
            hm(), hm(), hm(), hm(), hm(), hm(),
            pl.BlockSpec((2, RWKV_HEADS, tm // 8, 64), lambda i: (0, 0, i, 0)),
        ],
        out_shape=[
            jax.ShapeDtypeStruct((RWKV_HEADS, s, 64), BF16),
            jax.ShapeDtypeStruct((s, RWKV_DIM), BF16),
            jax.ShapeDtypeStruct((s, RWKV_DIM), BF16),
            hm_shape, hm_shape, hm_shape, hm_shape, hm_shape, hm_shape,
            jax.ShapeDtypeStruct((2, RWKV_HEADS, s // 8, 64), F32),
        ],
        compiler_params=_cparams(("parallel",)),
        name="rwkv_prep",
    )(p, p, p, p, p, p, mu_rkv, mu_lora, w0, w2p, a0, a2p, g2p, k_k, k_a, r_k, bd, tri)


def _bdot(a, b):
    return lax.dot_general(a, b, (((2,), (1,)), ((0,), (0,))), preferred_element_type=F32)


def _bdot_nt(a, b):
    return lax.dot_general(a, b, (((2,), (2,)), ((0,), (0,))), preferred_element_type=F32)


def _bdot_tn(a, b):
    return lax.dot_general(a, b, (((1,), (1,)), ((0,), (0,))), preferred_element_type=F32)


def _bdot_inv(a, b):
    return _bdot(a.astype(BF16), b.astype(BF16))


def _wkv_kernel(*refs):
    (vf_ref, vr_ref), ins, (gcf_ref, gcr_ref, yf_ref, yr_ref, s_ref) = refs[:2], refs[2:14], refs[14:]

    @pl.when(pl.program_id(0) == 0)
    def _():
        s_ref[...] = jnp.zeros_like(s_ref)

    _wkv_chunk(False, vf_ref, ins[0:6], gcf_ref, yf_ref, s_ref.at[0])
    _wkv_chunk(True, vr_ref, ins[6:12], gcr_ref, yr_ref, s_ref.at[1])


def _wkv_chunk(reverse, v_ref, scaled_refs, gc_ref, y_ref, s_ref):
    nh = RWKV_HEADS
    row = lax.broadcasted_iota(jnp.int32, (nh, CHUNK, CHUNK), 1)
    col = lax.broadcasted_iota(jnp.int32, (nh, CHUNK, CHUNK), 2)
    strict = (row < col) if reverse else (row > col)
    incl = (row <= col) if reverse else (row >= col)

    v = v_ref[...]
    rb, kb, kt, bt, kh, bh = [r[...] for r in scaled_refs]

    a_kk = jnp.where(strict, _bdot_nt(kb, kt), 0.0)
    l_mat = jnp.where(strict, _bdot_nt(kb, bt), 0.0)
    a_rk = jnp.where(incl, _bdot_nt(rb, kt), 0.0)
    a_rb = jnp.where(incl, _bdot_nt(rb, bt), 0.0)

    eye = jnp.where(row == col, 1.0, 0.0)
    t_inv = eye - l_mat
    pw = _bdot_inv(l_mat, l_mat)
    for step in range(5):
        t_inv = t_inv + _bdot_inv(t_inv, pw)
        if step < 4:
            pw = _bdot_inv(pw, pw)

    q1 = _bdot(a_kk.astype(BF16), v)
    q2 = _bdot(a_rk.astype(BF16), v)
    t_b = t_inv.astype(BF16)
    wk = _bdot(t_b, kb)
    uv = _bdot(t_b, q1.astype(BF16))
    a_rb_b = a_rb.astype(BF16)
    wk_b = wk.astype(BF16)
    uv_b = uv.astype(BF16)
    rw = rb.astype(F32) - _bdot(a_rb_b, wk_b)
    y0 = q2 - _bdot(a_rb_b, uv_b)
    m_mat = _bdot_tn(wk_b, bh)
    j_t = _bdot_tn(v, kh) - _bdot_tn(uv_b, bh)

    s_old = s_ref[...]
    s_hi = s_old.astype(BF16)
    s_lo = (s_old - s_hi.astype(F32)).astype(BF16)
    m_b = m_mat.astype(BF16)
    y_ref[...] = _bdot_nt(rw.astype(BF16), s_hi) + y0
    s_ref[...] = s_old * gc_ref[:, 0:1, :] - (_bdot(s_hi, m_b) + _bdot(s_lo, m_b)) + j_t


def _wkv(v, rb, kb, kt, bt, kh, bh, gc):
    s = v.shape[1]
    nc = s // CHUNK
    scaled = (rb, kb, kt, bt, kh, bh)
    fwd = lambda rows: pl.BlockSpec((None, RWKV_HEADS, rows, 64), lambda c: (0, 0, c, 0))
    rev = lambda rows: pl.BlockSpec((None, RWKV_HEADS, rows, 64), lambda c: (1, 0, nc - 1 - c, 0))
    y_shape = jax.ShapeDtypeStruct((RWKV_HEADS, s, 64), F32)
    return pl.pallas_call(
        _wkv_kernel,
        grid=(nc,),
        in_specs=[
            pl.BlockSpec((RWKV_HEADS, CHUNK, 64), lambda c: (0, c, 0)),
            pl.BlockSpec((RWKV_HEADS, CHUNK, 64), lambda c: (0, nc - 1 - c, 0)),
            *[fwd(CHUNK) for _ in scaled], *[rev(CHUNK) for _ in scaled],
            fwd(8), rev(8),
        ],
        out_specs=[
            pl.BlockSpec((RWKV_HEADS, CHUNK, 64), lambda c: (0, c, 0)),
            pl.BlockSpec((RWKV_HEADS, CHUNK, 64), lambda c: (0, nc - 1 - c, 0)),
        ],
        out_shape=[y_shape, y_shape],
        scratch_shapes=[pltpu.VMEM((2, RWKV_HEADS, 64, 64), F32)],
        compiler_params=_cparams(("arbitrary",)),
        name="wkv7_chunked",
    )(v, v, *scaled, *scaled, gc, gc)


def _rwkv_post_kernel(yf_ref, yr_ref, g_ref, bonus_ref, lg_ref, lb_ref, o_ref):
    y = yf_ref[...] + yr_ref[...]
    mean = jnp.mean(y, axis=-1, keepdims=True)
    yc = y - mean
    var = jnp.mean(yc * yc, axis=-1, keepdims=True)
    yn = yc * lax.rsqrt(var + LNX_EPS)
    yt = jnp.concatenate([yn[h] for h in range(RWKV_HEADS)], axis=-1)
    out = (yt * lg_ref[...] + lb_ref[...] + bonus_ref[...].astype(F32)) * g_ref[...].astype(F32)
    o_ref[...] = out.astype(o_ref.dtype)


def _rwkv_post(y_fwd, y_rev, g, bonus, lnx_g, lnx_b, *, tm):
    s = g.shape[0]
    return pl.pallas_call(
        _rwkv_post_kernel,
        grid=(s // tm,),
        in_specs=[
            pl.BlockSpec((RWKV_HEADS, tm, 64), lambda i: (0, i, 0)),
            pl.BlockSpec((RWKV_HEADS, tm, 64), lambda i: (0, i, 0)),
            pl.BlockSpec((tm, RWKV_DIM), lambda i: (i, 0)),
            pl.BlockSpec((tm, RWKV_DIM), lambda i: (i, 0)),
            pl.BlockSpec((1, RWKV_DIM), lambda i: (0, 0)),
            pl.BlockSpec((1, RWKV_DIM), lambda i: (0, 0)),
        ],
        out_specs=pl.BlockSpec((tm, RWKV_DIM), lambda i: (i, 0)),
        out_shape=jax.ShapeDtypeStruct((s, RWKV_DIM), BF16),
        compiler_params=_cparams(("parallel",)),
        name="rwkv_post",
    )(y_fwd, y_rev, g, bonus, lnx_g.reshape(1, RWKV_DIM), lnx_b.reshape(1, RWKV_DIM))


def _mem_attn_kernel(q_ref, kv_ref, gq_ref, gk_ref, o_ref):
    for h in range(MEM_HEADS):
        sl = slice(MEM_HEAD_DIM * h, MEM_HEAD_DIM * (h + 1))
        q = q_ref[:, sl].astype(F32)
        q = q * lax.rsqrt(jnp.mean(q * q, axis=-1, keepdims=True) + NORM_EPS)
        q = q * gq_ref[...] * (MEM_HEAD_DIM ** -0.5)
        km = kv_ref[:, sl].astype(F32)
        km = km * lax.rsqrt(jnp.mean(km * km, axis=-1, keepdims=True) + NORM_EPS)
        km = km * gk_ref[...]
        vm = kv_ref[:, MEM_DIM + MEM_HEAD_DIM * h:MEM_DIM + MEM_HEAD_DIM * (h + 1)]
        s = lax.dot_general(q.astype(BF16), km.astype(BF16), (((1,), (1,)), ((), ())),
                            preferred_element_type=F32)
        s = s - jnp.max(s, axis=-1, keepdims=True)
        e = jnp.exp(s)
        pr = e / jnp.sum(e, axis=-1, keepdims=True)
        o_ref[:, sl] = _dot(pr.astype(BF16), vm).astype(o_ref.dtype)


def _mem_attn(p, kv, gq, gk, *, tm):
    s = p.shape[0]
    return pl.pallas_call(
        _mem_attn_kernel,
        grid=(s // tm,),
        in_specs=[
            pl.BlockSpec((tm, MEM_DIM), lambda i: (i, COL_MEM // MEM_DIM)),
            pl.BlockSpec((N_MEM, 2 * MEM_DIM), lambda i: (0, 0)),
            pl.BlockSpec((1, MEM_HEAD_DIM), lambda i: (0, 0)),
            pl.BlockSpec((1, MEM_HEAD_DIM), lambda i: (0, 0)),
        ],
        out_specs=pl.BlockSpec((tm, MEM_DIM), lambda i: (i, 0)),
        out_shape=jax.ShapeDtypeStruct((s, MEM_DIM), BF16),
        compiler_params=_cparams(("parallel",)),
        name="mem_attn",
    )(p, kv, gq, gk)


def _merge_kernel(o0_ref, o1_ref, o2_ref, g0_ref, g1_ref, g2_ref, w_ref, m_ref):
    acc = jax.nn.sigmoid(g0_ref[...].astype(F32)) * _dot(o0_ref[...], w_ref[0])
    acc = acc + jax.nn.sigmoid(g1_ref[...].astype(F32)) * _dot(o1_ref[...], w_ref[1])
    acc = acc + jax.nn.sigmoid(g2_ref[...].astype(F32)) * _dot(o2_ref[...], w_ref[2])
    m_ref[...] = acc.astype(m_ref.dtype)


def _merge(o_diff, o_rwkv, o_mem, p, w_branch, *, tm, tn):
    s = p.shape[0]
    gate_blk = COL_GATE // tn
    per = D_MODEL // tn
    o_spec = lambda: pl.BlockSpec((tm, 1024), lambda i, j: (i, 0))
    g_spec = lambda b: pl.BlockSpec((tm, tn), lambda i, j: (i, gate_blk + b * per + j))
    return pl.pallas_call(
        _merge_kernel,
        grid=(s // tm, per),
        in_specs=[
            o_spec(), o_spec(), o_spec(),
            g_spec(0), g_spec(1), g_spec(2),
            pl.BlockSpec((N_BRANCH, 1024, tn), lambda i, j: (0, 0, j)),
        ],
        out_specs=pl.BlockSpec((tm, tn), lambda i, j: (i, j)),
        out_shape=jax.ShapeDtypeStruct((s, D_MODEL), BF16),
        compiler_params=_cparams(("parallel", "arbitrary")),
        name="merge",
    )(o_diff, o_rwkv, o_mem, p, p, p, w_branch)


GLU_TILE = 256


def _ffn_up_glu_kernel(x_ref, xp_ref, xn_ref, g_ref, wg_ref, wv_ref, cw_ref, cb_ref, o_ref,
                       h_ref, halo_ref):
    i = pl.program_id(0)
    nb = pl.num_programs(0)

    def norm(x):
        ms = jnp.mean(x * x, axis=-1, keepdims=True)
        return x * lax.rsqrt(ms + NORM_EPS) * g_ref[...]

    @pl.when(pl.program_id(1) == 0)
    def _():
        h_ref[...] = norm(x_ref[...]).astype(BF16)
        row = lax.broadcasted_iota(jnp.int32, xp_ref.shape, 0)
        before = jnp.where(i > 0, pltpu.roll(norm(xp_ref[...]), 1, 0), 0.0)
        after = jnp.where(i < nb - 1, pltpu.roll(norm(xn_ref[...]), 1, 0), 0.0)
        halo_ref[...] = jnp.where(row == 0, before, jnp.where(row == 1, after, 0.0)).astype(BF16)

    h = h_ref[...]
    halo = halo_ref[...]
    for c in range(o_ref.shape[1] // GLU_TILE):
        sl = slice(c * GLU_TILE, (c + 1) * GLU_TILE)
        wg = wg_ref[:, sl]
        ug = _dot(h, wg)
        uv = _dot(h, wv_ref[:, sl])
        edge = _dot(halo, wg)
        gp, gn = _shift_rows(ug, edge[0:1], edge[1:2])
        cw = cw_ref[:, sl]
        conv = cw[0:1] * gp + cw[1:2] * ug + cw[2:3] * gn + cb_ref[:, sl]
        o_ref[:, sl] = (conv * jax.nn.sigmoid(conv) * uv).astype(o_ref.dtype)


def _ffn_up_glu(x, g, w_up, conv_w, conv_b, *, tm, tn):
    s, k = x.shape
    nj = D_FF // tn
    hb = tm // 16
    last16 = s // 16 - 1
    return pl.pallas_call(
        _ffn_up_glu_kernel,
        grid=(s // tm, nj),
        in_specs=[
            pl.BlockSpec((tm, k), lambda i, j: (i, 0)),
            pl.BlockSpec((16, k), lambda i, j: (jnp.maximum(i * hb - 1, 0), 0)),
            pl.BlockSpec((16, k), lambda i, j: (jnp.minimum((i + 1) * hb, last16), 0)),
            pl.BlockSpec((1, k), lambda i, j: (0, 0)),
            pl.BlockSpec((k, tn), lambda i, j: (0, j)),
            pl.BlockSpec((k, tn), lambda i, j: (0, nj + j)),
            pl.BlockSpec((3, tn), lambda i, j: (0, j)),
            pl.BlockSpec((1, tn), lambda i, j: (0, j)),
        ],
        out_specs=pl.BlockSpec((tm, tn), lambda i, j: (i, j)),
        out_shape=jax.ShapeDtypeStruct((s, D_FF), BF16),
        scratch_shapes=[pltpu.VMEM((tm, k), BF16), pltpu.VMEM((16, k), BF16)],
        compiler_params=_cparams(("parallel", "arbitrary")),
        name="ffn_up_glu",
    )(x, x, x, g.reshape(1, k), w_up, w_up, conv_w, conv_b.reshape(1, D_FF))


def _block_ones(n, group):
    idx = np.arange(n) // group
    return jnp.asarray(idx[:, None] == idx[None, :], BF16)


def _chunk_tri(n):
    idx = np.arange(n)
    same = (idx[:, None] // CHUNK) == (idx[None, :] // CHUNK)
    return jnp.asarray(same & (idx[:, None] >= idx[None, :]), BF16)


def _pad_rows(w, rows_before, total):
    n = w.shape[-1]
    out = jnp.zeros((total, n), w.dtype)
    return lax.dynamic_update_slice(out, w, (rows_before, 0))


def _pad_w_in(w):
    lora0 = COL_RKV + 3 * RWKV_DIM
    lora1 = lora0 + LORA_COLS
    pad = jnp.zeros((w.shape[0], LORA_PAD - LORA_COLS), w.dtype)
    return jnp.concatenate([w[:, :lora0], w[:, lora1:], w[:, lora0:lora1], pad], axis=1).astype(BF16)


def kernel(x, mem, attn_norm_g, w_in, diff_qk_g, diff_lambda, diff_subln_g, rwkv_mu, rwkv_w0,
           rwkv_w2, rwkv_a0, rwkv_a2, rwkv_g2, rwkv_k_k, rwkv_k_a, rwkv_r_k, rwkv_lnx_g,
           rwkv_lnx_b, mem_norm_g, w_mem_kv, mem_qk_g, w_branch, w_out, ffn_norm_g, w_ffn_up,
           ffn_conv_w, ffn_conv_b, w_ffn_down):
    b, s, d = x.shape
    assert b == 1 and d == D_MODEL and s % ATT_BQ == 0
    xs = x.reshape(s, d)
    mem2 = mem.reshape(N_MEM, d)
    prep_tm = 256
    bd64 = _block_ones(1024, 64)
    tri = _chunk_tri(prep_tm)
    n_rwkv_main = 3 * RWKV_DIM

    for l in range(DEPTH):
        lam_init = 0.8 - 0.6 * math.exp(-0.3 * l)
        p = _rms_mm(xs, attn_norm_g[l], _pad_w_in(w_in[l]), tm=512, tn=1536, name="rms_w_in")

        gq = jnp.tile(diff_qk_g[l, 0].reshape(1, 128), (1, DIFF_HEADS))
        gk = jnp.tile(diff_qk_g[l, 1].reshape(1, 128), (1, DIFF_HEADS))
        qc, kc, score_bound = _attn_consts(diff_qk_g[l])
        qa, ka, vt = _diff_prep(p, gq, gk, qc, kc, bd64, tm=256)
        o_diff = lax.cond(
            score_bound <= SCORE_BOUND_MAX,
            functools.partial(_diff_attn, lam_init=lam_init, online=False),
            functools.partial(_diff_attn, lam_init=lam_init, online=True),
            qa, ka, vt, diff_lambda[l], diff_subln_g[l])

        mu = rwkv_mu[l]
        mu_rkv = mu[:, :n_rwkv_main]
        mu_lora = jnp.pad(mu[:, n_rwkv_main:], ((0, 0), (0, LORA_PAD - LORA_COLS)))
        w2p = jnp.stack([_pad_rows(rwkv_w2[l, dd], 64 * dd, 128) for dd in range(2)]).astype(BF16)
        a2p = jnp.stack([_pad_rows(rwkv_a2[l, dd], 64 * dd, 128) for dd in range(2)]).astype(BF16)
        g2p = _pad_rows(rwkv_g2[l], 0, 256).astype(BF16)
        (v_h, g_tok, bonus, rb, kb, kt, bt, kh, bh, gc) = _rwkv_prep(
            p, mu_rkv, mu_lora, rwkv_w0[l], w2p, rwkv_a0[l], a2p, g2p,
            rwkv_k_k[l].reshape(1, RWKV_DIM), rwkv_k_a[l].reshape(1, RWKV_DIM),
            rwkv_r_k[l].reshape(1, RWKV_DIM), bd64, tri, tm=prep_tm)
        y_fwd, y_rev = _wkv(v_h, rb, kb, kt, bt, kh, bh, gc)
        o_rwkv = _rwkv_post(y_fwd, y_rev, g_tok, bonus, rwkv_lnx_g[l], rwkv_lnx_b[l], tm=256)

        kv = _rms_mm(mem2, mem_norm_g[l], w_mem_kv[l].astype(BF16), tm=N_MEM, tn=1024,
                     name="rms_mem_kv")
        o_mem = _mem_attn(p, kv, mem_qk_g[l, 0].reshape(1, MEM_HEAD_DIM),
                          mem_qk_g[l, 1].reshape(1, MEM_HEAD_DIM), tm=512)

        merged = _merge(o_diff, o_rwkv, o_mem, p, w_branch[l].astype(BF16), tm=512, tn=512)
        xs = _mm_res(merged, w_out[l].astype(BF16), xs, tm=512, tn=1024, name="w_out_res")

        act = _ffn_up_glu(xs, ffn_norm_g[l], w_ffn_up[l].astype(BF16), ffn_conv_w[l],
                          ffn_conv_b[l], tm=512, tn=512)
        xs = _mm_res(act, w_ffn_down[l].astype(BF16), xs, tm=512, tn=512, name="ffn_down_res")

    return xs.reshape(b, s, d)
```

```python
import functools
import math

import jax
import jax.numpy as jnp
import numpy as np
from jax import lax
from jax.experimental import pallas as pl
from jax.experimental.pallas import tpu as pltpu

F32 = jnp.float32
BF16 = jnp.bfloat16

D_MODEL = 2048
DEPTH = 2
DIFF_HEADS = 8
DIFF_QK_DIM = 64
DIFF_V_DIM = 128
DIFF_DIM = 1024
RWKV_HEADS = 16
RWKV_HEAD_DIM = 64
RWKV_DIM = 1024
DECAY_LORA = 64
AAA_LORA = 64
GATE_LORA = 160
LORA_COLS = 2 * DECAY_LORA + 2 * AAA_LORA + GATE_LORA
LORA_PAD = 512
N_MEM = 256
MEM_HEADS = 4
MEM_HEAD_DIM = 256
MEM_DIM = 1024
N_BRANCH = 3
D_FF = 5632
NORM_EPS = 1e-6
LNX_EPS = 64e-5

COL_DIFF = 0
COL_RKV = 3 * DIFF_DIM
COL_MEM = COL_RKV + 3 * RWKV_DIM
COL_GATE = COL_MEM + MEM_DIM
COL_LORA = COL_GATE + N_BRANCH * D_MODEL
N_IN_PAD = COL_LORA + LORA_PAD

CHUNK = 64
LOG2E = 1.4426950408889634
N_POS_COLS = 12
SCORE_BOUND_MAX = 30.0
ATT_BQ = 512
ATT_BK = 512
V_AUG = 144
FAR_UNROLL = 4
VMEM_LIMIT = 48 * 1024 * 1024


def _cparams(sem):
    return pltpu.CompilerParams(dimension_semantics=sem, vmem_limit_bytes=VMEM_LIMIT)


def _split2(x):
    hi = x.astype(BF16)
    return hi, (x - hi.astype(F32)).astype(BF16)


def _dot(a, b):
    return jnp.dot(a, b, preferred_element_type=F32)


def _dot_exactish(a_bf16_exact, x_f32):
    hi, lo = _split2(x_f32)
    return _dot(a_bf16_exact, hi) + _dot(a_bf16_exact, lo)


def _rms_mm_kernel(x_ref, g_ref, w_ref, o_ref, h_ref):
    @pl.when(pl.program_id(1) == 0)
    def _():
        x = x_ref[...]
        ms = jnp.mean(x * x, axis=-1, keepdims=True)
        h_ref[...] = (x * lax.rsqrt(ms + NORM_EPS) * g_ref[...]).astype(BF16)

    o_ref[...] = _dot(h_ref[...], w_ref[...]).astype(o_ref.dtype)


def _rms_mm(x, g, w, *, tm, tn, name):
    m, k = x.shape
    n = w.shape[1]
    return pl.pallas_call(
        _rms_mm_kernel,
        grid=(m // tm, n // tn),
        in_specs=[
            pl.BlockSpec((tm, k), lambda i, j: (i, 0)),
            pl.BlockSpec((1, k), lambda i, j: (0, 0)),
            pl.BlockSpec((k, tn), lambda i, j: (0, j)),
        ],
        out_specs=pl.BlockSpec((tm, tn), lambda i, j: (i, j)),
        out_shape=jax.ShapeDtypeStruct((m, n), BF16),
        scratch_shapes=[pltpu.VMEM((tm, k), BF16)],
        compiler_params=_cparams(("parallel", "arbitrary")),
        name=name,
    )(x, g.reshape(1, k), w)


def _mm_res_kernel(a_ref, w_ref, r_ref, o_ref):
    o_ref[...] = r_ref[...] + _dot(a_ref[...], w_ref[...])


def _mm_res(a, w, res, *, tm, tn, name):
    m, k = a.shape
    n = w.shape[1]
    return pl.pallas_call(
        _mm_res_kernel,
        grid=(m // tm, n // tn),
        in_specs=[
            pl.BlockSpec((tm, k), lambda i, j: (i, 0)),
            pl.BlockSpec((k, tn), lambda i, j: (0, j)),
            pl.BlockSpec((tm, tn), lambda i, j: (i, j)),
        ],
        out_specs=pl.BlockSpec((tm, tn), lambda i, j: (i, j)),
        out_shape=jax.ShapeDtypeStruct((m, n), F32),
        compiler_params=_cparams(("parallel", "arbitrary")),
        name=name,
    )(a, w, res)


def _group_mean_sq(x, bd_ref, group):
    hi, lo = _split2(x * x)
    bd = bd_ref[...]
    return (_dot(hi, bd) + _dot(lo, bd)) * (1.0 / group)


def _aug_base(mp):
    return 64 if mp == 0 else 0


def _diff_prep_kernel(q_ref, k_ref, v_ref, gq_ref, gk_ref, qc_ref, kc_ref, bd_ref,
                      qa_ref, ka_ref, vt_ref):
    tm = q_ref.shape[0]
    row0 = pl.program_id(0) * tm
    lane = lax.broadcasted_iota(jnp.int32, (tm, 128), 1)
    pos = row0 + lax.broadcasted_iota(jnp.int32, (tm, 128), 0)
    pos_lo = (pos & 127).astype(F32)
    pos_hi = (pos >> 7).astype(F32)

    q = q_ref[...].astype(F32)
    qn = q * lax.rsqrt(_group_mean_sq(q, bd_ref, DIFF_QK_DIM) + NORM_EPS)
    qn = qn * gq_ref[...] * (DIFF_QK_DIM ** -0.5 * LOG2E)
    k = k_ref[...].astype(F32)
    kn = k * lax.rsqrt(_group_mean_sq(k, bd_ref, DIFF_QK_DIM) + NORM_EPS)
    kn = kn * gk_ref[...]

    for h in range(DIFF_HEADS):
        qh = qn[:, 128 * h:128 * (h + 1)]
        kh = kn[:, 128 * h:128 * (h + 1)]
        for mp in range(2):
            a0 = _aug_base(mp)
            g = 2 * h + mp
            is_data = (lane < 64) if mp == 0 else (lane >= 64)
            aug_q = jnp.where(lane < a0 + 3, pos_lo,
                              jnp.where(lane < a0 + 6, pos_hi, qc_ref[g:g + 1, :]))
            aug_k = jnp.where((lane >= a0 + 6) & (lane < a0 + 9), pos_lo,
                              jnp.where((lane >= a0 + 9) & (lane < a0 + 12), pos_hi,
                                        kc_ref[g:g + 1, :]))
            qa_ref[h, mp] = jnp.where(is_data, qh, aug_q).astype(BF16)
            ka_ref[h, mp] = jnp.where(is_data, kh, aug_k).astype(BF16)

    vt = v_ref[...].astype(F32).T
    sub = lax.broadcasted_iota(jnp.int32, (V_AUG - DIFF_V_DIM, tm), 0)
    ones_rows = jnp.where(sub == 0, 1.0, 0.0).astype(BF16)
    for h in range(DIFF_HEADS):
        vt_ref[h, 0:DIFF_V_DIM, :] = vt[128 * h:128 * (h + 1), :].astype(BF16)
        vt_ref[h, DIFF_V_DIM:V_AUG, :] = ones_rows


def _bf16_split3_const(x):
    parts = []
    for _ in range(3):
        part = float(np.asarray(x, np.float32).astype(BF16).astype(np.float32))
        parts.append(part)
        x = x - part
    return parts


def _attn_consts(qk_g):
    l_parts = _bf16_split3_const(LOG2E)
    m_nat = 8.0 * jnp.max(jnp.abs(qk_g[0]), axis=-1) * jnp.max(jnp.abs(qk_g[1]), axis=-1)
    qc = np.zeros((2 * DIFF_HEADS, 128), np.float32)
    kc = np.zeros((2 * DIFF_HEADS, 128), np.float32)
    bound_lane = np.zeros((2, 2 * DIFF_HEADS, 128), np.float32)
    for h in range(DIFF_HEADS):
        slope = 2.0 ** (-(h + 1))
        for mp in range(2):
            a0 = _aug_base(mp)
            g = 2 * h + mp
            for t, lp in enumerate(l_parts):
                qc[g, a0 + 6 + t] = slope * lp
                qc[g, a0 + 9 + t] = 128.0 * slope * lp
                kc[g, a0 + t] = -slope * lp
                kc[g, a0 + 3 + t] = -128.0 * slope * lp
            bound_lane[mp, g, a0 + N_POS_COLS] = 1.0
            kc[g, a0 + N_POS_COLS] = 1.0
    m2 = -m_nat * LOG2E
    qc = qc + m2[0] * bound_lane[0] + m2[1] * bound_lane[1]
    return qc, jnp.asarray(kc), jnp.max(m_nat)


def _diff_prep(p, gq, gk, qc, kc, bd, *, tm):
    s = p.shape[0]
    return pl.pallas_call(
        _diff_prep_kernel,
        grid=(s // tm,),
        in_specs=[
            pl.BlockSpec((tm, DIFF_DIM), lambda i: (i, 0)),
            pl.BlockSpec((tm, DIFF_DIM), lambda i: (i, 1)),
            pl.BlockSpec((tm, DIFF_DIM), lambda i: (i, 2)),
            pl.BlockSpec((1, DIFF_DIM), lambda i: (0, 0)),
            pl.BlockSpec((1, DIFF_DIM), lambda i: (0, 0)),
            pl.BlockSpec((2 * DIFF_HEADS, 128), lambda i: (0, 0)),
            pl.BlockSpec((2 * DIFF_HEADS, 128), lambda i: (0, 0)),
            pl.BlockSpec((DIFF_DIM, DIFF_DIM), lambda i: (0, 0)),
        ],
        out_specs=[
            pl.BlockSpec((DIFF_HEADS, 2, tm, 128), lambda i: (0, 0, i, 0)),
            pl.BlockSpec((DIFF_HEADS, 2, tm, 128), lambda i: (0, 0, i, 0)),
            pl.BlockSpec((DIFF_HEADS, V_AUG, tm), lambda i: (0, 0, i)),
        ],
        out_shape=[
            jax.ShapeDtypeStruct((DIFF_HEADS, 2, s, 128), BF16),
            jax.ShapeDtypeStruct((DIFF_HEADS, 2, s, 128), BF16),
            jax.ShapeDtypeStruct((DIFF_HEADS, V_AUG, s), BF16),
        ],
        compiler_params=_cparams(("parallel",)),
        name="diff_prep",
    )(p, p, p, gq, gk, qc, kc, bd)


def _diff_attn_kernel(lam_ref, sg_ref, dnear_ref, qa_ref, ka_ref, vt_ref, o_ref,
                      acc_ref, qv_ref, *mode_refs, lam_init, online):
    h = pl.program_id(0)
    i = pl.program_id(1)
    bq = qa_ref.shape[1]
    s_len = ka_ref.shape[1]
    bk = ATT_BK
    n_chunks = s_len // bk
    per_q = bq // bk
    n_far = n_chunks - per_q
    j_lo = i * per_q
    q0 = i * bq
    slope2 = jnp.exp2(-jnp.full((1, 1), h + 1, jnp.int32).astype(F32)) * LOG2E
    if online:
        m_ref, sa_ref, sb_ref, mxa_ref, mxb_ref = mode_refs
        m_ref[...] = jnp.full_like(m_ref, -1e30)
        bufs = ((sa_ref, mxa_ref), (sb_ref, mxb_ref))
    else:
        bufs = mode_refs
    acc_ref[...] = jnp.zeros_like(acc_ref)

    lane = lax.broadcasted_iota(jnp.int32, (1, 128), 1)
    for mp in range(2):
        q = qa_ref[mp]
        pos_lane = (lane >= _aug_base(mp)) & (lane < _aug_base(mp) + N_POS_COLS)
        qv_ref[mp, 0] = q
        qv_ref[mp, 1] = jnp.where(pos_lane, -q, q)
        qv_ref[mp, 2] = jnp.where(pos_lane, jnp.zeros_like(q), q)

    def stage_one(k0, variant, bias, buf):
        for mp in range(2):
            kc = ka_ref[mp, pl.ds(k0, bk), :]
            s = lax.dot_general(kc, qv_ref[mp, variant], (((1,), (1,)), ((), ())),
                                preferred_element_type=F32)
            if bias is not None:
                s = s + bias
            if online:
                buf[0][mp] = s
                buf[1][mp] = jnp.max(s, axis=0, keepdims=True)
            else:
                buf[mp] = jnp.exp2(s).astype(BF16)

    def stage_two(k0, buf):
        vt_c = vt_ref[:, pl.ds(k0, bk)]
        for mp in range(2):
            if online:
                m_old = m_ref[mp]
                m_new = jnp.maximum(m_old, buf[1][mp])
                p = jnp.exp2(buf[0][mp] - m_new).astype(BF16)
                acc_ref[mp] = acc_ref[mp] * jnp.exp2(m_old - m_new) + _dot(vt_c, p)
                m_ref[mp] = m_new
            else:
                acc_ref[mp] += _dot(vt_c, buf[mp])

    def near_start(t):
        return pl.multiple_of(q0 + t * bk, bk)

    def far_start(t):
        t = jnp.minimum(t, n_far - 1)
        j = jnp.where(t < j_lo, t, t + per_q)
        return pl.multiple_of(j * bk, bk), (t >= j_lo).astype(jnp.int32)

    def near_one(t):
        stage_one(near_start(t), 2, slope2 * dnear_ref[t], bufs[t % 2])

    def far_one(t, parity):
        k0, after = far_start(t)
        stage_one(k0, after, None, bufs[parity])

    def far_two(t, parity):
        stage_two(far_start(t)[0], bufs[parity])

    near_one(0)
    for t in range(1, per_q):
        near_one(t)
        stage_two(near_start(t - 1), bufs[(t - 1) % 2])
    far_one(0, per_q % 2)
    stage_two(near_start(per_q - 1), bufs[(per_q - 1) % 2])

    def far_body(u, c):
        for r in range(FAR_UNROLL):
            far_one(FAR_UNROLL * u + r + 1, (per_q + r + 1) % 2)
            far_two(FAR_UNROLL * u + r, (per_q + r) % 2)
        return c

    lax.fori_loop(0, n_far // FAR_UNROLL, far_body, 0)
    for t in range(n_far - n_far % FAR_UNROLL, n_far):
        if t + 1 < n_far:
            far_one(t + 1, (per_q + t + 1) % 2)
        far_two(t, (per_q + t) % 2)

    lp = lam_ref[...]
    lam = (jnp.exp(jnp.sum(lp[0:1] * lp[1:2], axis=-1, keepdims=True))
           - jnp.exp(jnp.sum(lp[2:3] * lp[3:4], axis=-1, keepdims=True)) + lam_init)
    a0 = acc_ref[0]
    a1 = acc_ref[1]
    o = (a0[0:DIFF_V_DIM] / a0[DIFF_V_DIM:DIFF_V_DIM + 1]
         - lam * (a1[0:DIFF_V_DIM] / a1[DIFF_V_DIM:DIFF_V_DIM + 1]))
    ms = jnp.mean(o * o, axis=0, keepdims=True)
    o = o * lax.rsqrt(ms + NORM_EPS) * sg_ref[...] * (1.0 - lam_init)
    o_ref[...] = o.T.astype(o_ref.dtype)


def _diff_attn(qa, ka, vt, lam_p, subln_g, *, lam_init, online):
    s = qa.shape[2]
    bq = ATT_BQ
    assert bq % ATT_BK == 0 and s > bq
    kern = functools.partial(_diff_attn_kernel, lam_init=lam_init, online=online)
    per_q = bq // ATT_BK
    key = np.arange(per_q * ATT_BK).reshape(per_q, ATT_BK, 1)
    dnear = jnp.asarray(-np.abs(np.arange(bq).reshape(1, 1, bq) - key), F32)
    scratch = [
        pltpu.VMEM((2, V_AUG, bq), F32),
        pltpu.VMEM((2, 3, bq, 128), BF16),
    ]
    if online:
        scratch += [
            pltpu.VMEM((2, 1, bq), F32),
            pltpu.VMEM((2, ATT_BK, bq), F32),
            pltpu.VMEM((2, ATT_BK, bq), F32),
            pltpu.VMEM((2, 1, bq), F32),
            pltpu.VMEM((2, 1, bq), F32),
        ]
    else:
        scratch += [pltpu.VMEM((2, ATT_BK, bq), BF16), pltpu.VMEM((2, ATT_BK, bq), BF16)]
    return pl.pallas_call(
        kern,
        grid=(DIFF_HEADS, s // bq),
        in_specs=[
            pl.BlockSpec((4, DIFF_QK_DIM), lambda h, i: (0, 0)),
            pl.BlockSpec((DIFF_V_DIM, 1), lambda h, i: (0, 0)),
            pl.BlockSpec((per_q, ATT_BK, bq), lambda h, i: (0, 0, 0)),
            pl.BlockSpec((None, 2, bq, 128), lambda h, i: (h, 0, i, 0)),
            pl.BlockSpec((None, 2, s, 128), lambda h, i: (h, 0, 0, 0)),
            pl.BlockSpec((None, V_AUG, s), lambda h, i: (h, 0, 0)),
        ],
        out_specs=pl.BlockSpec((bq, DIFF_V_DIM), lambda h, i: (i, h)),
        out_shape=jax.ShapeDtypeStruct((s, DIFF_DIM), BF16),
        scratch_shapes=scratch,
        compiler_params=_cparams(("parallel", "arbitrary")),
        name="diff_attn_online" if online else "diff_attn",
    )(lam_p, subln_g.reshape(DIFF_V_DIM, 1), dnear, qa, ka, vt)


def _shift_rows(x, prev_row, next_row):
    n = x.shape[0]
    row = lax.broadcasted_iota(jnp.int32, x.shape, 0)
    xp = jnp.where(row == 0, prev_row, pltpu.roll(x, 1, 0))
    xn = jnp.where(row == n - 1, next_row, pltpu.roll(x, n - 1, 0))
    return xp, xn


def _halo_rows(prev_ref, next_ref, i, n_blocks):
    hp = prev_ref.shape[0]
    prev_row = prev_ref[hp - 1:hp, :].astype(F32)
    next_row = next_ref[0:1, :].astype(F32)
    prev_row = jnp.where(i > 0, prev_row, 0.0)
    next_row = jnp.where(i < n_blocks - 1, next_row, 0.0)
    return prev_row, next_row


def _rwkv_prep_kernel(
        x_ref, xp_ref, xn_ref, l_ref, lp_ref, ln_ref,
        mu_ref, mul_ref, w0_ref, w2_ref, a0_ref, a2_ref, g2_ref, kk_ref, ka_ref, rk_ref,
        bd_ref, tri_ref,
        v_out, g_out, bonus_out, rb_out, kb_out, kt_out, bt_out, kh_out, bh_out, gc_out):
    i = pl.program_id(0)
    nb = pl.num_programs(0)
    tm = x_ref.shape[0]

    x = x_ref[...].astype(F32)
    prev_row, next_row = _halo_rows(xp_ref, xn_ref, i, nb)
    xp, xn = _shift_rows(x, prev_row, next_row)
    mu = mu_ref[...]
    x = x + mu[0:1] * (xp - x) + mu[1:2] * (xn - x)
    lo = l_ref[...].astype(F32)
    prev_row, next_row = _halo_rows(lp_ref, ln_ref, i, nb)
    lop, lon = _shift_rows(lo, prev_row, next_row)
    mul = mul_ref[...]
    lo = lo + mul[0:1] * (lop - lo) + mul[1:2] * (lon - lo)

    r = x[:, 0:RWKV_DIM]
    k = x[:, RWKV_DIM:2 * RWKV_DIM]
    v = x[:, 2 * RWKV_DIM:3 * RWKV_DIM]
    tw = jnp.tanh(lo[:, 0:128]).astype(BF16)
    la = lo[:, 128:256].astype(BF16)
    lg = jax.nn.sigmoid(lo[:, 256:512]).astype(BF16)

    g = _dot(lg, g2_ref[...])
    kk = k * kk_ref[...]
    ss = _group_mean_sq(kk, bd_ref, 1.0)
    kk = kk * lax.rsqrt(jnp.maximum(ss, 1e-24))
    hi, lo_part = _split2(r * k * rk_ref[...])
    bonus = (_dot(hi, bd_ref[...]) + _dot(lo_part, bd_ref[...])) * v

    g_out[...] = g.astype(g_out.dtype)
    bonus_out[...] = bonus.astype(bonus_out.dtype)
    for h in range(RWKV_HEADS):
        v_out[h] = v[:, 64 * h:64 * (h + 1)].astype(BF16)

    tri = tri_ref[...]
    for d in range(2):
        wl = w0_ref[d:d + 1, :] + _dot(tw, w2_ref[d])
        logdec = -math.exp(-0.5) * jax.nn.sigmoid(wl)
        a = jax.nn.sigmoid(a0_ref[d:d + 1, :] + _dot(la, a2_ref[d]))
        k_d = k * (1.0 + (a - 1.0) * ka_ref[...])
        b_d = kk * a
        pre = _dot_exactish(tri, logdec)
        chunk_tot = jnp.broadcast_to(
            pre.reshape(tm // CHUNK, CHUNK, RWKV_DIM)[:, CHUNK - 1:CHUNK, :],
            (tm // CHUNK, CHUNK, RWKV_DIM)).reshape(tm, RWKV_DIM)
        suf = chunk_tot - pre
        if d == 0:
            lc, ex, rem = pre, pre - logdec, suf
        else:
            lc, ex, rem = suf + logdec, suf, pre - logdec
        e_neg = jnp.exp(-lc)
        e_rem = jnp.exp(rem)
        outs = (
            (rb_out, r * jnp.exp(lc)),
            (kb_out, kk * jnp.exp(ex)),
            (kt_out, k_d * e_neg),
            (bt_out, b_d * e_neg),
            (kh_out, k_d * e_rem),
            (bh_out, b_d * e_rem),
        )
        for ref, val in outs:
            val = val.astype(BF16)
            for h in range(RWKV_HEADS):
                ref[d, h] = val[:, 64 * h:64 * (h + 1)]
        tot = jnp.exp(chunk_tot).reshape(tm // 8, 8, RWKV_DIM)[:, 0, :]
        for h in range(RWKV_HEADS):
            gc_out[d, h] = tot[:, 64 * h:64 * (h + 1)]


def _rwkv_prep(p, mu_rkv, mu_lora, w0, w2p, a0, a2p, g2p, k_k, k_a, r_k, bd, tri, *, tm):
    s = p.shape[0]
    nb = s // tm
    hb = tm // 16
    last16 = s // 16 - 1
    rkv_w = 3 * RWKV_DIM
    c_rkv = COL_RKV // rkv_w
    c_lora = COL_LORA // LORA_PAD

    def prev_map(c):
        return lambda i: (jnp.maximum(i * hb - 1, 0), c)

    def next_map(c):
        return lambda i: (jnp.minimum((i + 1) * hb, last16), c)

    full = lambda *shape: pl.BlockSpec(shape, lambda i: (0,) * len(shape))
    hm = lambda: pl.BlockSpec((2, RWKV_HEADS, tm, 64), lambda i: (0, 0, i, 0))
    hm_shape = jax.ShapeDtypeStruct((2, RWKV_HEADS, s, 64), BF16)
    return pl.pallas_call(
        _rwkv_prep_kernel,
        grid=(nb,),
        in_specs=[
            pl.BlockSpec((tm, rkv_w), lambda i: (i, c_rkv)),
            pl.BlockSpec((16, rkv_w), prev_map(c_rkv)),
            pl.BlockSpec((16, rkv_w), next_map(c_rkv)),
            pl.BlockSpec((tm, LORA_PAD), lambda i: (i, c_lora)),
            pl.BlockSpec((16, LORA_PAD), prev_map(c_lora)),
            pl.BlockSpec((16, LORA_PAD), next_map(c_lora)),
            full(2, rkv_w), full(2, LORA_PAD),
            full(2, RWKV_DIM), full(2, 128, RWKV_DIM),
            full(2, RWKV_DIM), full(2, 128, RWKV_DIM),
            full(256, RWKV_DIM),
            full(1, RWKV_DIM), full(1, RWKV_DIM), full(1, RWKV_DIM),
            full(RWKV_DIM, RWKV_DIM), full(tm, tm),
        ],
        out_specs=[
            pl.BlockSpec((RWKV_HEADS, tm, 64), lambda i: (0, i, 0)),
            pl.BlockSpec((tm, RWKV_DIM), lambda i: (i, 0)),
            pl.BlockSpec((tm, RWKV_DIM), lambda i: (i, 0)),
            hm(), hm(), hm(), hm(), hm(), hm(),
            pl.BlockSpec((2, RWKV_HEADS, tm // 8, 64), lambda i: (0, 0, i, 0)),
        ],
        out_shape=[
            jax.ShapeDtypeStruct((RWKV_HEADS, s, 64), BF16),
            jax.ShapeDtypeStruct((s, RWKV_DIM), BF16),
            jax.ShapeDtypeStruct((s, RWKV_DIM), BF16),
            hm_shape, hm_shape, hm_shape, hm_shape, hm_shape, hm_shape,
            jax.ShapeDtypeStruct((2, RWKV_HEADS, s // 8, 64), F32),
        ],
        compiler_params=_cparams(("parallel",)),
        name="rwkv_prep",
    )(p, p, p, p, p, p, mu_rkv, mu_lora, w0, w2p, a0, a2p, g2p, k_k, k_a, r_k, bd, tri)


def _bdot(a, b):
    return lax.dot_general(a, b, (((2,), (1,)), ((0,), (0,))), preferred_element_type=F32)


def _bdot_nt(a, b):
    return lax.dot_general(a, b, (((2,), (2,)), ((0,), (0,))), preferred_element_type=F32)


def _bdot_tn(a, b):
    return lax.dot_general(a, b, (((1,), (1,)), ((0,), (0,))), preferred_element_type=F32)


def _bdot_inv(a, b):
    return _bdot(a.astype(BF16), b.astype(BF16))


def _wkv_kernel(*refs):
    (vf_ref, vr_ref), ins, (gcf_ref, gcr_ref, yf_ref, yr_ref, s_ref) = refs[:2], refs[2:14], refs[14:]

    @pl.when(pl.program_id(0) == 0)
    def _():
        s_ref[...] = jnp.zeros_like(s_ref)

    _wkv_chunk(False, vf_ref, ins[0:6], gcf_ref, yf_ref, s_ref.at[0])
    _wkv_chunk(True, vr_ref, ins[6:12], gcr_ref, yr_ref, s_ref.at[1])


def _wkv_chunk(reverse, v_ref, scaled_refs, gc_ref, y_ref, s_ref):
    nh = RWKV_HEADS
    row = lax.broadcasted_iota(jnp.int32, (nh, CHUNK, CHUNK), 1)
    col = lax.broadcasted_iota(jnp.int32, (nh, CHUNK, CHUNK), 2)
    strict = (row < col) if reverse else (row > col)
    incl = (row <= col) if reverse else (row >= col)

    v = v_ref[...]
    rb, kb, kt, bt, kh, bh = [r[...] for r in scaled_refs]

    a_kk = jnp.where(strict, _bdot_nt(kb, kt), 0.0)
    l_mat = jnp.where(strict, _bdot_nt(kb, bt), 0.0)
    a_rk = jnp.where(incl, _bdot_nt(rb, kt), 0.0)
    a_rb = jnp.where(incl, _bdot_nt(rb, bt), 0.0)

    eye = jnp.where(row == col, 1.0, 0.0)
    t_inv = eye - l_mat
    pw = _bdot_inv(l_mat, l_mat)
    for step in range(5):
        t_inv = t_inv + _bdot_inv(t_inv, pw)
        if step < 4:
            pw = _bdot_inv(pw, pw)

    q1 = _bdot(a_kk.astype(BF16), v)
    q2 = _bdot(a_rk.astype(BF16), v)
    t_b = t_inv.astype(BF16)
    wk = _bdot(t_b, kb)
    uv = _bdot(t_b, q1.astype(BF16))
    a_rb_b = a_rb.astype(BF16)
    wk_b = wk.astype(BF16)
    uv_b = uv.astype(BF16)
    rw = rb.astype(F32) - _bdot(a_rb_b, wk_b)
    y0 = q2 - _bdot(a_rb_b, uv_b)
    m_mat = _bdot_tn(wk_b, bh)
    j_t = _bdot_tn(v, kh) - _bdot_tn(uv_b, bh)

    s_old = s_ref[...]
    s_hi = s_old.astype(BF16)
    s_lo = (s_old - s_hi.astype(F32)).astype(BF16)
    m_b = m_mat.astype(BF16)
    y_ref[...] = _bdot_nt(rw.astype(BF16), s_hi) + y0
    s_ref[...] = s_old * gc_ref[:, 0:1, :] - (_bdot(s_hi, m_b) + _bdot(s_lo, m_b)) + j_t


def _wkv(v, rb, kb, kt, bt, kh, bh, gc):
    s = v.shape[1]
    nc = s // CHUNK
    scaled = (rb, kb, kt, bt, kh, bh)
    fwd = lambda rows: pl.BlockSpec((None, RWKV_HEADS, rows, 64), lambda c: (0, 0, c, 0))
    rev = lambda rows: pl.BlockSpec((None, RWKV_HEADS, rows, 64), lambda c: (1, 0, nc - 1 - c, 0))
    y_shape = jax.ShapeDtypeStruct((RWKV_HEADS, s, 64), F32)
    return pl.pallas_call(
        _wkv_kernel,
        grid=(nc,),
        in_specs=[
            pl.BlockSpec((RWKV_HEADS, CHUNK, 64), lambda c: (0, c, 0)),
            pl.BlockSpec((RWKV_HEADS, CHUNK, 64), lambda c: (0, nc - 1 - c, 0)),
            *[fwd(CHUNK) for _ in scaled], *[rev(CHUNK) for _ in scaled],
            fwd(8), rev(8),
        ],
        out_specs=[
            pl.BlockSpec((RWKV_HEADS, CHUNK, 64), lambda c: (0, c, 0)),
            pl.BlockSpec((RWKV_HEADS, CHUNK, 64), lambda c: (0, nc - 1 - c, 0)),
        ],
        out_shape=[y_shape, y_shape],
        scratch_shapes=[pltpu.VMEM((2, RWKV_HEADS, 64, 64), F32)],
        compiler_params=_cparams(("arbitrary",)),
        name="wkv7_chunked",
    )(v, v, *scaled, *scaled, gc, gc)


def _rwkv_post_kernel(yf_ref, yr_ref, g_ref, bonus_ref, lg_ref, lb_ref, o_ref):
    y = yf_ref[...] + yr_ref[...]
    mean = jnp.mean(y, axis=-1, keepdims=True)
    yc = y - mean
    var = jnp.mean(yc * yc, axis=-1, keepdims=True)
    yn = yc * lax.rsqrt(var + LNX_EPS)
    yt = jnp.concatenate([yn[h] for h in range(RWKV_HEADS)], axis=-1)
    out = (yt * lg_ref[...] + lb_ref[...] + bonus_ref[...].astype(F32)) * g_ref[...].astype(F32)
    o_ref[...] = out.astype(o_ref.dtype)


def _rwkv_post(y_fwd, y_rev, g, bonus, lnx_g, lnx_b, *, tm):
    s = g.shape[0]
    return pl.pallas_call(
        _rwkv_post_kernel,
        grid=(s // tm,),
        in_specs=[
            pl.BlockSpec((RWKV_HEADS, tm, 64), lambda i: (0, i, 0)),
            pl.BlockSpec((RWKV_HEADS, tm, 64), lambda i: (0, i, 0)),
            pl.BlockSpec((tm, RWKV_DIM), lambda i: (i, 0)),
            pl.BlockSpec((tm, RWKV_DIM), lambda i: (i, 0)),
            pl.BlockSpec((1, RWKV_DIM), lambda i: (0, 0)),
            pl.BlockSpec((1, RWKV_DIM), lambda i: (0, 0)),
        ],
        out_specs=pl.BlockSpec((tm, RWKV_DIM), lambda i: (i, 0)),
        out_shape=jax.ShapeDtypeStruct((s, RWKV_DIM), BF16),
        compiler_params=_cparams(("parallel",)),
        name="rwkv_post",
    )(y_fwd, y_rev, g, bonus, lnx_g.reshape(1, RWKV_DIM), lnx_b.reshape(1, RWKV_DIM))


def _mem_attn_kernel(q_ref, kv_ref, gq_ref, gk_ref, o_ref):
    for h in range(MEM_HEADS):
        sl = slice(MEM_HEAD_DIM * h, MEM_HEAD_DIM * (h + 1))
        q = q_ref[:, sl].astype(F32)
        q = q * lax.rsqrt(jnp.mean(q * q, axis=-1, keepdims=True) + NORM_EPS)
        q = q * gq_ref[...] * (MEM_HEAD_DIM ** -0.5)
        km = kv_ref[:, sl].astype(F32)
        km = km * lax.rsqrt(jnp.mean(km * km, axis=-1, keepdims=True) + NORM_EPS)
        km = km * gk_ref[...]
        vm = kv_ref[:, MEM_DIM + MEM_HEAD_DIM * h:MEM_DIM + MEM_HEAD_DIM * (h + 1)]
        s = lax.dot_general(q.astype(BF16), km.astype(BF16), (((1,), (1,)), ((), ())),
                            preferred_element_type=F32)
        s = s - jnp.max(s, axis=-1, keepdims=True)
        e = jnp.exp(s)
        pr = e / jnp.sum(e, axis=-1, keepdims=True)
        o_ref[:, sl] = _dot(pr.astype(BF16), vm).astype(o_ref.dtype)


def _mem_attn(p, kv, gq, gk, *, tm):
    s = p.shape[0]
    return pl.pallas_call(
        _mem_attn_kernel,
        grid=(s // tm,),
        in_specs=[
            pl.BlockSpec((tm, MEM_DIM), lambda i: (i, COL_MEM // MEM_DIM)),
            pl.BlockSpec((N_MEM, 2 * MEM_DIM), lambda i: (0, 0)),
            pl.BlockSpec((1, MEM_HEAD_DIM), lambda i: (0, 0)),
            pl.BlockSpec((1, MEM_HEAD_DIM), lambda i: (0, 0)),
        ],
        out_specs=pl.BlockSpec((tm, MEM_DIM), lambda i: (i, 0)),
        out_shape=jax.ShapeDtypeStruct((s, MEM_DIM), BF16),
        compiler_params=_cparams(("parallel",)),
        name="mem_attn",
    )(p, kv, gq, gk)


def _merge_kernel(o0_ref, o1_ref, o2_ref, g0_ref, g1_ref, g2_ref, w_ref, m_ref):
    acc = jax.nn.sigmoid(g0_ref[...].astype(F32)) * _dot(o0_ref[...], w_ref[0])
    acc = acc + jax.nn.sigmoid(g1_ref[...].astype(F32)) * _dot(o1_ref[...], w_ref[1])
    acc = acc + jax.nn.sigmoid(g2_ref[...].astype(F32)) * _dot(o2_ref[...], w_ref[2])
    m_ref[...] = acc.astype(m_ref.dtype)


def _merge(o_diff, o_rwkv, o_mem, p, w_branch, *, tm, tn):
    s = p.shape[0]
    gate_blk = COL_GATE // tn
    per = D_MODEL // tn
    o_spec = lambda: pl.BlockSpec((tm, 1024), lambda i, j: (i, 0))
    g_spec = lambda b: pl.BlockSpec((tm, tn), lambda i, j: (i, gate_blk + b * per + j))
    return pl.pallas_call(
        _merge_kernel,
        grid=(s // tm, per),
        in_specs=[
            o_spec(), o_spec(), o_spec(),
            g_spec(0), g_spec(1), g_spec(2),
            pl.BlockSpec((N_BRANCH, 1024, tn), lambda i, j: (0, 0, j)),
        ],
        out_specs=pl.BlockSpec((tm, tn), lambda i, j: (i, j)),
        out_shape=jax.ShapeDtypeStruct((s, D_MODEL), BF16),
        compiler_params=_cparams(("parallel", "arbitrary")),
        name="merge",
    )(o_diff, o_rwkv, o_mem, p, p, p, w_branch)


GLU_TILE = 256


def _ffn_up_glu_kernel(x_ref, xp_ref, xn_ref, g_ref, wg_ref, wv_ref, cw_ref, cb_ref, o_ref,
                       h_ref, halo_ref):
    i = pl.program_id(0)
    nb = pl.num_programs(0)

    def norm(x):
        ms = jnp.mean(x * x, axis=-1, keepdims=True)
        return x * lax.rsqrt(ms + NORM_EPS) * g_ref[...]

    @pl.when(pl.program_id(1) == 0)
    def _():
        h_ref[...] = norm(x_ref[...]).astype(BF16)
        row = lax.broadcasted_iota(jnp.int32, xp_ref.shape, 0)
        before = jnp.where(i > 0, pltpu.roll(norm(xp_ref[...]), 1, 0), 0.0)
        after = jnp.where(i < nb - 1, pltpu.roll(norm(xn_ref[...]), 1, 0), 0.0)
        halo_ref[...] = jnp.where(row == 0, before, jnp.where(row == 1, after, 0.0)).astype(BF16)

    h = h_ref[...]
    halo = halo_ref[...]
    for c in range(o_ref.shape[1] // GLU_TILE):
        sl = slice(c * GLU_TILE, (c + 1) * GLU_TILE)
        wg = wg_ref[:, sl]
        ug = _dot(h, wg)
        uv = _dot(h, wv_ref[:, sl])
        edge = _dot(halo, wg)
        gp, gn = _shift_rows(ug, edge[0:1], edge[1:2])
        cw = cw_ref[:, sl]
        conv = cw[0:1] * gp + cw[1:2] * ug + cw[2:3] * gn + cb_ref[:, sl]
        o_ref[:, sl] = (conv * jax.nn.sigmoid(conv) * uv).astype(o_ref.dtype)


def _ffn_up_glu(x, g, w_up, conv_w, conv_b, *, tm, tn):
    s, k = x.shape
    nj = D_FF // tn
    hb = tm // 16
    last16 = s // 16 - 1
    return pl.pallas_call(
        _ffn_up_glu_kernel,
        grid=(s // tm, nj),
        in_specs=[
            pl.BlockSpec((tm, k), lambda i, j: (i, 0)),
            pl.BlockSpec((16, k), lambda i, j: (jnp.maximum(i * hb - 1, 0), 0)),
            pl.BlockSpec((16, k), lambda i, j: (jnp.minimum((i + 1) * hb, last16), 0)),
            pl.BlockSpec((1, k), lambda i, j: (0, 0)),
            pl.BlockSpec((k, tn), lambda i, j: (0, j)),
            pl.BlockSpec((k, tn), lambda i, j: (0, nj + j)),
            pl.BlockSpec((3, tn), lambda i, j: (0, j)),
            pl.BlockSpec((1, tn), lambda i, j: (0, j)),
        ],
        out_specs=pl.BlockSpec((tm, tn), lambda i, j: (i, j)),
        out_shape=jax.ShapeDtypeStruct((s, D_FF), BF16),
        scratch_shapes=[pltpu.VMEM((tm, k), BF16), pltpu.VMEM((16, k), BF16)],
        compiler_params=_cparams(("parallel", "arbitrary")),
        name="ffn_up_glu",
    )(x, x, x, g.reshape(1, k), w_up, w_up, conv_w, conv_b.reshape(1, D_FF))


def _block_ones(n, group):
    idx = np.arange(n) // group
    return jnp.asarray(idx[:, None] == idx[None, :], BF16)


def _chunk_tri(n):
    idx = np.arange(n)
    same = (idx[:, None] // CHUNK) == (idx[None, :] // CHUNK)
    return jnp.asarray(same & (idx[:, None] >= idx[None, :]), BF16)


def _pad_rows(w, rows_before, total):
    n = w.shape[-1]
    out = jnp.zeros((total, n), w.dtype)
    return lax.dynamic_update_slice(out, w, (rows_before, 0))


def _pad_w_in(w):
    lora0 = COL_RKV + 3 * RWKV_DIM
    lora1 = lora0 + LORA_COLS
    pad = jnp.zeros((w.shape[0], LORA_PAD - LORA_COLS), w.dtype)
    return jnp.concatenate([w[:, :lora0], w[:, lora1:], w[:, lora0:lora1], pad], axis=1).astype(BF16)


def kernel(x, mem, attn_norm_g, w_in, diff_qk_g, diff_lambda, diff_subln_g, rwkv_mu, rwkv_w0,
           rwkv_w2, rwkv_a0, rwkv_a2, rwkv_g2, rwkv_k_k, rwkv_k_a, rwkv_r_k, rwkv_lnx_g,
           rwkv_lnx_b, mem_norm_g, w_mem_kv, mem_qk_g, w_branch, w_out, ffn_norm_g, w_ffn_up,
           ffn_conv_w, ffn_conv_b, w_ffn_down):
    b, s, d = x.shape
    assert b == 1 and d == D_MODEL and s % ATT_BQ == 0
    xs = x.reshape(s, d)
    mem2 = mem.reshape(N_MEM, d)
    prep_tm = 256
    bd64 = _block_ones(1024, 64)
    tri = _chunk_tri(prep_tm)
    n_rwkv_main = 3 * RWKV_DIM

    for l in range(DEPTH):
        lam_init = 0.8 - 0.6 * math.exp(-0.3 * l)
        p = _rms_mm(xs, attn_norm_g[l], _pad_w_in(w_in[l]), tm=512, tn=1536, name="rms_w_in")

        gq = jnp.tile(diff_qk_g[l, 0].reshape(1, 128), (1, DIFF_HEADS))
        gk = jnp.tile(diff_qk_g[l, 1].reshape(1, 128), (1, DIFF_HEADS))
        qc, kc, score_bound = _attn_consts(diff_qk_g[l])
        qa, ka, vt = _diff_prep(p, gq, gk, qc, kc, bd64, tm=256)
        o_diff = lax.cond(
            score_bound <= SCORE_BOUND_MAX,
            functools.partial(_diff_attn, lam_init=lam_init, online=False),
            functools.partial(_diff_attn, lam_init=lam_init, online=True),
            qa, ka, vt, diff_lambda[l], diff_subln_g[l])

        mu = rwkv_mu[l]
        mu_rkv = mu[:, :n_rwkv_main]
        mu_lora = jnp.pad(mu[:, n_rwkv_main:], ((0, 0), (0, LORA_PAD - LORA_COLS)))
        w2p = jnp.stack([_pad_rows(rwkv_w2[l, dd], 64 * dd, 128) for dd in range(2)]).astype(BF16)
        a2p = jnp.stack([_pad_rows(rwkv_a2[l, dd], 64 * dd, 128) for dd in range(2)]).astype(BF16)
        g2p = _pad_rows(rwkv_g2[l], 0, 256).astype(BF16)
        (v_h, g_tok, bonus, rb, kb, kt, bt, kh, bh, gc) = _rwkv_prep(
            p, mu_rkv, mu_lora, rwkv_w0[l], w2p, rwkv_a0[l], a2p, g2p,
            rwkv_k_k[l].reshape(1, RWKV_DIM), rwkv_k_a[l].reshape(1, RWKV_DIM),
            rwkv_r_k[l].reshape(1, RWKV_DIM), bd64, tri, tm=prep_tm)
        y_fwd, y_rev = _wkv(v_h, rb, kb, kt, bt, kh, bh, gc)
        o_rwkv = _rwkv_post(y_fwd, y_rev, g_tok, bonus, rwkv_lnx_g[l], rwkv_lnx_b[l], tm=256)

        kv = _rms_mm(mem2, mem_norm_g[l], w_mem_kv[l].astype(BF16), tm=N_MEM, tn=1024,
                     name="rms_mem_kv")
        o_mem = _mem_attn(p, kv, mem_qk_g[l, 0].reshape(1, MEM_HEAD_DIM),
                          mem_qk_g[l, 1].reshape(1, MEM_HEAD_DIM), tm=512)

        merged = _merge(o_diff, o_rwkv, o_mem, p, w_branch[l].astype(BF16), tm=512, tn=512)
        xs = _mm_res(merged, w_out[l].astype(BF16), xs, tm=512, tn=1024, name="w_out_res")

        act = _ffn_up_glu(xs, ffn_norm_g[l], w_ffn_up[l].astype(BF16), ffn_conv_w[l],
                          ffn_conv_b[l], tm=512, tn=512)
        xs = _mm_res(act, w_ffn_down[l].astype(BF16), xs, tm=512, tn=512, name="ffn_down_res")

    return xs.reshape(b, s, d)
```

```python
import functools
import math

import jax
import jax.numpy as jnp
import numpy as np
from jax import lax
from jax.experimental import pallas as pl
from jax.experimental.pallas import tpu as pltpu

F32 = jnp.float32
BF16 = jnp.bfloat16

D_MODEL = 2048
DEPTH = 2
DIFF_HEADS = 8
DIFF_QK_DIM = 64
DIFF_V_DIM = 128
DIFF_DIM = 1024
RWKV_HEADS = 16
RWKV_HEAD_DIM = 64
RWKV_DIM = 1024
DECAY_LORA = 64
AAA_LORA = 64
GATE_LORA = 160
LORA_COLS = 2 * DECAY_LORA + 2 * AAA_LORA + GATE_LORA
LORA_PAD = 512
N_MEM = 256
MEM_HEADS = 4
MEM_HEAD_DIM = 256
MEM_DIM = 1024
N_BRANCH = 3
D_FF = 5632
NORM_EPS = 1e-6
LNX_EPS = 64e-5

COL_DIFF = 0
COL_RKV = 3 * DIFF_DIM
COL_MEM = COL_RKV + 3 * RWKV_DIM
COL_GATE = COL_MEM + MEM_DIM
COL_LORA = COL_GATE + N_BRANCH * D_MODEL
N_IN_PAD = COL_LORA + LORA_PAD

CHUNK = 64
LOG2E = 1.4426950408889634
N_POS_COLS = 12
SCORE_BOUND_MAX = 30.0
ATT_BQ = 512
ATT_BK = 512
V_AUG = 144
FAR_UNROLL = 6
VMEM_LIMIT = 48 * 1024 * 1024


def _cparams(sem):
    return pltpu.CompilerParams(dimension_semantics=sem, vmem_limit_bytes=VMEM_LIMIT)


def _split2(x):
    hi = x.astype(BF16)
    return hi, (x - hi.astype(F32)).astype(BF16)


def _dot(a, b):
    return jnp.dot(a, b, preferred_element_type=F32)


def _dot_exactish(a_bf16_exact, x_f32):
    hi, lo = _split2(x_f32)
    return _dot(a_bf16_exact, hi) + _dot(a_bf16_exact, lo)


def _rms_mm_kernel(x_ref, g_ref, w_ref, o_ref, h_ref):
    @pl.when(pl.program_id(1) == 0)
    def _():
        x = x_ref[...]
        ms = jnp.mean(x * x, axis=-1, keepdims=True)
        h_ref[...] = (x * lax.rsqrt(ms + NORM_EPS) * g_ref[...]).astype(BF16)

    o_ref[...] = _dot(h_ref[...], w_ref[...]).astype(o_ref.dtype)


def _rms_mm(x, g, w, layer, *, tm, tn, name):
    m, k = x.shape
    n = w.shape[2]
    return pl.pallas_call(
        _rms_mm_kernel,
        grid=(m // tm, n // tn),
        in_specs=[
            pl.BlockSpec((tm, k), lambda i, j: (i, 0)),
            pl.BlockSpec((1, k), lambda i, j: (0, 0)),
            pl.BlockSpec((None, k, tn), lambda i, j: (layer, 0, j)),
        ],
        out_specs=pl.BlockSpec((tm, tn), lambda i, j: (i, j)),
        out_shape=jax.ShapeDtypeStruct((m, n), BF16),
        scratch_shapes=[pltpu.VMEM((tm, k), BF16)],
        compiler_params=_cparams(("parallel", "arbitrary")),
        name=name,
    )(x, g.reshape(1, k), w)


def _mm_res_kernel(a_ref, w_ref, r_ref, o_ref):
    o_ref[...] = r_ref[...] + _dot(a_ref[...], w_ref[...])


def _mm_res(a, w, layer, res, *, tm, tn, name):
    m, k = a.shape
    n = w.shape[2]
    return pl.pallas_call(
        _mm_res_kernel,
        grid=(m // tm, n // tn),
        in_specs=[
            pl.BlockSpec((tm, k), lambda i, j: (i, 0)),
            pl.BlockSpec((None, k, tn), lambda i, j: (layer, 0, j)),
            pl.BlockSpec((tm, tn), lambda i, j: (i, j)),
        ],
        out_specs=pl.BlockSpec((tm, tn), lambda i, j: (i, j)),
        out_shape=jax.ShapeDtypeStruct((m, n), F32),
        compiler_params=_cparams(("parallel", "arbitrary")),
        name=name,
    )(a, w, res)


def _group_mean_sq(x, bd_ref, group):
    hi, lo = _split2(x * x)
    bd = bd_ref[...]
    return (_dot(hi, bd) + _dot(lo, bd)) * (1.0 / group)


def _aug_base(mp):
    return 64 if mp == 0 else 0


def _diff_prep_kernel(q_ref, k_ref, v_ref, gq_ref, gk_ref, qc_ref, kc_ref, bd_ref,
                      qa_ref, ka_ref, vt_ref):
    tm = q_ref.shape[0]
    row0 = pl.program_id(0) * tm
    lane = lax.broadcasted_iota(jnp.int32, (tm, 128), 1)
    pos = row0 + lax.broadcasted_iota(jnp.int32, (tm, 128), 0)
    pos_lo = (pos & 127).astype(F32)
    pos_hi = (pos >> 7).astype(F32)

    q = q_ref[...].astype(F32)
    qn = q * lax.rsqrt(_group_mean_sq(q, bd_ref, DIFF_QK_DIM) + NORM_EPS)
    qn = qn * gq_ref[...] * (DIFF_QK_DIM ** -0.5 * LOG2E)
    k = k_ref[...].astype(F32)
    kn = k * lax.rsqrt(_group_mean_sq(k, bd_ref, DIFF_QK_DIM) + NORM_EPS)
    kn = kn * gk_ref[...]

    for h in range(DIFF_HEADS):
        qh = qn[:, 128 * h:128 * (h + 1)]
        kh = kn[:, 128 * h:128 * (h + 1)]
        for mp in range(2):
            a0 = _aug_base(mp)
            g = 2 * h + mp
            is_data = (lane < 64) if mp == 0 else (lane >= 64)
            aug_q = jnp.where(lane < a0 + 3, pos_lo,
                              jnp.where(lane < a0 + 6, pos_hi, qc_ref[g:g + 1, :]))
            aug_k = jnp.where((lane >= a0 + 6) & (lane < a0 + 9), pos_lo,
                              jnp.where((lane >= a0 + 9) & (lane < a0 + 12), pos_hi,
                                        kc_ref[g:g + 1, :]))
            qa_ref[h, mp] = jnp.where(is_data, qh, aug_q).astype(BF16)
            ka_ref[h, mp] = jnp.where(is_data, kh, aug_k).astype(BF16)

    vt = v_ref[...].astype(F32).T
    sub = lax.broadcasted_iota(jnp.int32, (V_AUG - DIFF_V_DIM, tm), 0)
    ones_rows = jnp.where(sub == 0, 1.0, 0.0).astype(BF16)
    for h in range(DIFF_HEADS):
        vt_ref[h, 0:DIFF_V_DIM, :] = vt[128 * h:128 * (h + 1), :].astype(BF16)
        vt_ref[h, DIFF_V_DIM:V_AUG, :] = ones_rows


def _bf16_split3_const(x):
    parts = []
    for _ in range(3):
        part = float(np.asarray(x, np.float32).astype(BF16).astype(np.float32))
        parts.append(part)
        x = x - part
    return parts


def _attn_consts(qk_g):
    l_parts = _bf16_split3_const(LOG2E)
    m_nat = 8.0 * jnp.max(jnp.abs(qk_g[0]), axis=-1) * jnp.max(jnp.abs(qk_g[1]), axis=-1)
    qc = np.zeros((2 * DIFF_HEADS, 128), np.float32)
    kc = np.zeros((2 * DIFF_HEADS, 128), np.float32)
    bound_lane = np.zeros((2, 2 * DIFF_HEADS, 128), np.float32)
    for h in range(DIFF_HEADS):
        slope = 2.0 ** (-(h + 1))
        for mp in range(2):
            a0 = _aug_base(mp)
            g = 2 * h + mp
            for t, lp in enumerate(l_parts):
                qc[g, a0 + 6 + t] = slope * lp
                qc[g, a0 + 9 + t] = 128.0 * slope * lp
                kc[g, a0 + t] = -slope * lp
                kc[g, a0 + 3 + t] = -128.0 * slope * lp
            bound_lane[mp, g, a0 + N_POS_COLS] = 1.0
            kc[g, a0 + N_POS_COLS] = 1.0
    m2 = -m_nat * LOG2E
    qc = qc + m2[0] * bound_lane[0] + m2[1] * bound_lane[1]
    return qc, jnp.asarray(kc), jnp.max(m_nat)


def _diff_prep(p, gq, gk, qc, kc, bd, *, tm):
    s = p.shape[0]
    return pl.pallas_call(
        _diff_prep_kernel,
        grid=(s // tm,),
        in_specs=[
            pl.BlockSpec((tm, DIFF_DIM), lambda i: (i, 0)),
            pl.BlockSpec((tm, DIFF_DIM), lambda i: (i, 1)),
            pl.BlockSpec((tm, DIFF_DIM), lambda i: (i, 2)),
            pl.BlockSpec((1, DIFF_DIM), lambda i: (0, 0)),
            pl.BlockSpec((1, DIFF_DIM), lambda i: (0, 0)),
            pl.BlockSpec((2 * DIFF_HEADS, 128), lambda i: (0, 0)),
            pl.BlockSpec((2 * DIFF_HEADS, 128), lambda i: (0, 0)),
            pl.BlockSpec((DIFF_DIM, DIFF_DIM), lambda i: (0, 0)),
        ],
        out_specs=[
            pl.BlockSpec((DIFF_HEADS, 2, tm, 128), lambda i: (0, 0, i, 0)),
            pl.BlockSpec((DIFF_HEADS, 2, tm, 128), lambda i: (0, 0, i, 0)),
            pl.BlockSpec((DIFF_HEADS, V_AUG, tm), lambda i: (0, 0, i)),
        ],
        out_shape=[
            jax.ShapeDtypeStruct((DIFF_HEADS, 2, s, 128), BF16),
            jax.ShapeDtypeStruct((DIFF_HEADS, 2, s, 128), BF16),
            jax.ShapeDtypeStruct((DIFF_HEADS, V_AUG, s), BF16),
        ],
        compiler_params=_cparams(("parallel",)),
        name="diff_prep",
    )(p, p, p, gq, gk, qc, kc, bd)


def _diff_attn_kernel(lam_ref, sg_ref, dnear_ref, qa_ref, ka_ref, vt_ref, o_ref,
                      acc_ref, qv_ref, *mode_refs, lam_init, online):
    h = pl.program_id(0)
    i = pl.program_id(1)
    bq = qa_ref.shape[1]
    s_len = ka_ref.shape[1]
    bk = ATT_BK
    n_chunks = s_len // bk
    per_q = bq // bk
    n_far = n_chunks - per_q
    j_lo = i * per_q
    q0 = i * bq
    slope2 = jnp.exp2(-jnp.full((1, 1), h + 1, jnp.int32).astype(F32)) * LOG2E
    if online:
        m_ref, sa_ref, sb_ref, mxa_ref, mxb_ref = mode_refs
        m_ref[...] = jnp.full_like(m_ref, -1e30)
        bufs = ((sa_ref, mxa_ref), (sb_ref, mxb_ref))
    else:
        bufs = mode_refs
    acc_ref[...] = jnp.zeros_like(acc_ref)

    lane = lax.broadcasted_iota(jnp.int32, (1, 128), 1)
    for mp in range(2):
        q = qa_ref[mp]
        pos_lane = (lane >= _aug_base(mp)) & (lane < _aug_base(mp) + N_POS_COLS)
        qv_ref[mp, 0] = q
        qv_ref[mp, 1] = jnp.where(pos_lane, -q, q)
        qv_ref[mp, 2] = jnp.where(pos_lane, jnp.zeros_like(q), q)

    def stage_one(k0, variant, bias, buf):
        for mp in range(2):
            kc = ka_ref[mp, pl.ds(k0, bk), :]
            s = lax.dot_general(kc, qv_ref[mp, variant], (((1,), (1,)), ((), ())),
                                preferred_element_type=F32)
            if bias is not None:
                s = s + bias
            if online:
                buf[0][mp] = s
                buf[1][mp] = jnp.max(s, axis=0, keepdims=True)
            else:
                buf[mp] = jnp.exp2(s).astype(BF16)

    def stage_two(k0, buf):
        vt_c = vt_ref[:, pl.ds(k0, bk)]
        for mp in range(2):
            if online:
                m_old = m_ref[mp]
                m_new = jnp.maximum(m_old, buf[1][mp])
                p = jnp.exp2(buf[0][mp] - m_new).astype(BF16)
                acc_ref[mp] = acc_ref[mp] * jnp.exp2(m_old - m_new) + _dot(vt_c, p)
                m_ref[mp] = m_new
            else:
                acc_ref[mp] += _dot(vt_c, buf[mp])

    def near_start(t):
        return pl.multiple_of(q0 + t * bk, bk)

    def far_start(t):
        t = jnp.minimum(t, n_far - 1)
        j = jnp.where(t < j_lo, t, t + per_q)
        return pl.multiple_of(j * bk, bk), (t >= j_lo).astype(jnp.int32)

    def near_one(t):
        stage_one(near_start(t), 2, slope2 * dnear_ref[t], bufs[t % 2])

    def far_one(t, parity):
        k0, after = far_start(t)
        stage_one(k0, after, None, bufs[parity])

    def far_two(t, parity):
        stage_two(far_start(t)[0], bufs[parity])

    near_one(0)
    for t in range(1, per_q):
        near_one(t)
        stage_two(near_start(t - 1), bufs[(t - 1) % 2])
    far_one(0, per_q % 2)
    stage_two(near_start(per_q - 1), bufs[(per_q - 1) % 2])

    def far_body(u, c):
        for r in range(FAR_UNROLL):
            far_one(FAR_UNROLL * u + r + 1, (per_q + r + 1) % 2)
            far_two(FAR_UNROLL * u + r, (per_q + r) % 2)
        return c

    lax.fori_loop(0, n_far // FAR_UNROLL, far_body, 0)
    for t in range(n_far - n_far % FAR_UNROLL, n_far):
        if t + 1 < n_far:
            far_one(t + 1, (per_q + t + 1) % 2)
        far_two(t, (per_q + t) % 2)

    lp = lam_ref[...]
    lam = (jnp.exp(jnp.sum(lp[0:1] * lp[1:2], axis=-1, keepdims=True))
           - jnp.exp(jnp.sum(lp[2:3] * lp[3:4], axis=-1, keepdims=True)) + lam_init)
    a0 = acc_ref[0]
    a1 = acc_ref[1]
    o = (a0[0:DIFF_V_DIM] / a0[DIFF_V_DIM:DIFF_V_DIM + 1]
         - lam * (a1[0:DIFF_V_DIM] / a1[DIFF_V_DIM:DIFF_V_DIM + 1]))
    ms = jnp.mean(o * o, axis=0, keepdims=True)
    o = o * lax.rsqrt(ms + NORM_EPS) * sg_ref[...] * (1.0 - lam_init)
    o_ref[...] = o.T.astype(o_ref.dtype)


def _diff_attn(qa, ka, vt, lam_p, subln_g, *, lam_init, online):
    s = qa.shape[2]
    bq = ATT_BQ
    assert bq % ATT_BK == 0 and s > bq
    kern = functools.partial(_diff_attn_kernel, lam_init=lam_init, online=online)
    per_q = bq // ATT_BK
    key = np.arange(per_q * ATT_BK).reshape(per_q, ATT_BK, 1)
    dnear = jnp.asarray(-np.abs(np.arange(bq).reshape(1, 1, bq) - key), F32)
    scratch = [
        pltpu.VMEM((2, V_AUG, bq), F32),
        pltpu.VMEM((2, 3, bq, 128), BF16),
    ]
    if online:
        scratch += [
            pltpu.VMEM((2, 1, bq), F32),
            pltpu.VMEM((2, ATT_BK, bq), F32),
            pltpu.VMEM((2, ATT_BK, bq), F32),
            pltpu.VMEM((2, 1, bq), F32),
            pltpu.VMEM((2, 1, bq), F32),
        ]
    else:
        scratch += [pltpu.VMEM((2, ATT_BK, bq), BF16), pltpu.VMEM((2, ATT_BK, bq), BF16)]
    return pl.pallas_call(
        kern,
        grid=(DIFF_HEADS, s // bq),
        in_specs=[
            pl.BlockSpec((4, DIFF_QK_DIM), lambda h, i: (0, 0)),
            pl.BlockSpec((DIFF_V_DIM, 1), lambda h, i: (0, 0)),
            pl.BlockSpec((per_q, ATT_BK, bq), lambda h, i: (0, 0, 0)),
            pl.BlockSpec((None, 2, bq, 128), lambda h, i: (h, 0, i, 0)),
            pl.BlockSpec((None, 2, s, 128), lambda h, i: (h, 0, 0, 0)),
            pl.BlockSpec((None, V_AUG, s), lambda h, i: (h, 0, 0)),
        ],
        out_specs=pl.BlockSpec((bq, DIFF_V_DIM), lambda h, i: (i, h)),
        out_shape=jax.ShapeDtypeStruct((s, DIFF_DIM), BF16),
        scratch_shapes=scratch,
        compiler_params=_cparams(("parallel", "arbitrary")),
        name="diff_attn_online" if online else "diff_attn",
    )(lam_p, subln_g.reshape(DIFF_V_DIM, 1), dnear, qa, ka, vt)


def _shift_rows(x, prev_row, next_row):
    n = x.shape[0]
    row = lax.broadcasted_iota(jnp.int32, x.shape, 0)
    xp = jnp.where(row == 0, prev_row, pltpu.roll(x, 1, 0))
    xn = jnp.where(row == n - 1, next_row, pltpu.roll(x, n - 1, 0))
    return xp, xn


def _halo_rows(prev_ref, next_ref, i, n_blocks):
    hp = prev_ref.shape[0]
    prev_row = prev_ref[hp - 1:hp, :].astype(F32)
    next_row = next_ref[0:1, :].astype(F32)
    prev_row = jnp.where(i > 0, prev_row, 0.0)
    next_row = jnp.where(i < n_blocks - 1, next_row, 0.0)
    return prev_row, next_row


def _rwkv_prep_kernel(
        x_ref, xp_ref, xn_ref, l_ref, lp_ref, ln_ref,
        mu_ref, mul_ref, w0_ref, w2_ref, a0_ref, a2_ref, g2_ref, kk_ref, ka_ref, rk_ref,
        bd_ref, tri_ref,
        v_out, g_out, bonus_out, rb_out, kb_out, kt_out, bt_out, kh_out, bh_out, gc_out):
    i = pl.program_id(0)
    nb = pl.num_programs(0)
    tm = x_ref.shape[0]

    x = x_ref[...].astype(F32)
    prev_row, next_row = _halo_rows(xp_ref, xn_ref, i, nb)
    xp, xn = _shift_rows(x, prev_row, next_row)
    mu = mu_ref[...]
    x = x + mu[0:1] * (xp - x) + mu[1:2] * (xn - x)
    lo = l_ref[...].astype(F32)
    prev_row, next_row = _halo_rows(lp_ref, ln_ref, i, nb)
    lop, lon = _shift_rows(lo, prev_row, next_row)
    mul = mul_ref[...]
    lo = lo + mul[0:1] * (lop - lo) + mul[1:2] * (lon - lo)

    r = x[:, 0:RWKV_DIM]
    k = x[:, RWKV_DIM:2 * RWKV_DIM]
    v = x[:, 2 * RWKV_DIM:3 * RWKV_DIM]
    tw = jnp.tanh(lo[:, 0:128]).astype(BF16)
    la = lo[:, 128:256].astype(BF16)
    lg = jax.nn.sigmoid(lo[:, 256:512]).astype(BF16)

    g = _dot(lg, g2_ref[...])
    kk = k * kk_ref[...]
    ss = _group_mean_sq(kk, bd_ref, 1.0)
    kk = kk * lax.rsqrt(jnp.maximum(ss, 1e-24))
    hi, lo_part = _split2(r * k * rk_ref[...])
    bonus = (_dot(hi, bd_ref[...]) + _dot(lo_part, bd_ref[...])) * v

    g_out[...] = g.astype(g_out.dtype)
    bonus_out[...] = bonus.astype(bonus_out.dtype)
    for h in range(RWKV_HEADS):
        v_out[h] = v[:, 64 * h:64 * (h + 1)].astype(BF16)

    tri = tri_ref[...]
    for d in range(2):
        wl = w0_ref[d:d + 1, :] + _dot(tw, w2_ref[d])
        logdec = -math.exp(-0.5) * jax.nn.sigmoid(wl)
        a = jax.nn.sigmoid(a0_ref[d:d + 1, :] + _dot(la, a2_ref[d]))
        k_d = k * (1.0 + (a - 1.0) * ka_ref[...])
        b_d = kk * a
        pre = _dot_exactish(tri, logdec)
        chunk_tot = jnp.broadcast_to(
            pre.reshape(tm // CHUNK, CHUNK, RWKV_DIM)[:, CHUNK - 1:CHUNK, :],
            (tm // CHUNK, CHUNK, RWKV_DIM)).reshape(tm, RWKV_DIM)
        suf = chunk_tot - pre
        if d == 0:
            lc, ex, rem = pre, pre - logdec, suf
        else:
            lc, ex, rem = suf + logdec, suf, pre - logdec
        e_neg = jnp.exp(-lc)
        e_rem = jnp.exp(rem)
        outs = (
            (rb_out, r * jnp.exp(lc)),
            (kb_out, kk * jnp.exp(ex)),
            (kt_out, k_d * e_neg),
            (bt_out, b_d * e_neg),
            (kh_out, k_d * e_rem),
            (bh_out, b_d * e_rem),
        )
        for ref, val in outs:
            val = val.astype(BF16)
            for h in range(RWKV_HEADS):
                ref[d, h] = val[:, 64 * h:64 * (h + 1)]
        tot = jnp.exp(chunk_tot).reshape(tm // 8, 8, RWKV_DIM)[:, 0, :]
        for h in range(RWKV_HEADS):
            gc_out[d, h] = tot[:, 64 * h:64 * (h + 1)]


def _rwkv_prep(p, mu_rkv, mu_lora, w0, w2p, a0, a2p, g2p, k_k, k_a, r_k, bd, tri, *, tm):
    s = p.shape[0]
    nb = s // tm
    hb = tm // 16
    last16 = s // 16 - 1
    rkv_w = 3 * RWKV_DIM
    c_rkv = COL_RKV // rkv_w
    c_lora = COL_LORA // LORA_PAD

    def prev_map(c):
        return lambda i: (jnp.maximum(i * hb - 1, 0), c)

    def next_map(c):
        return lambda i: (jnp.minimum((i + 1) * hb, last16), c)

    full = lambda *shape: pl.BlockSpec(shape, lambda i: (0,) * len(shape))
    hm = lambda: pl.BlockSpec((2, RWKV_HEADS, tm, 64), lambda i: (0, 0, i, 0))
    hm_shape = jax.ShapeDtypeStruct((2, RWKV_HEADS, s, 64), BF16)
    return pl.pallas_call(
        _rwkv_prep_kernel,
        grid=(nb,),
        in_specs=[
            pl.BlockSpec((tm, rkv_w), lambda i: (i, c_rkv)),
            pl.BlockSpec((16, rkv_w), prev_map(c_rkv)),
            pl.BlockSpec((16, rkv_w), next_map(c_rkv)),
            pl.BlockSpec((tm, LORA_PAD), lambda i: (i, c_lora)),
            pl.BlockSpec((16, LORA_PAD), prev_map(c_lora)),
            pl.BlockSpec((16, LORA_PAD), next_map(c_lora)),
            full(2, rkv_w), full(2, LORA_PAD),
            full(2, RWKV_DIM), full(2, 128, RWKV_DIM),
            full(2, RWKV_DIM), full(2, 128, RWKV_DIM),
            full(256, RWKV_DIM),
            full(1, RWKV_DIM), full(1, RWKV_DIM), full(1, RWKV_DIM),
            full(RWKV_DIM, RWKV_DIM), full(tm, tm),
        ],
        out_specs=[
            pl.BlockSpec((RWKV_HEADS, tm, 64), lambda i: (0, i, 0)),
            pl.BlockSpec((tm, RWKV_DIM), lambda i: (i, 0)),
            pl.BlockSpec((tm, RWKV_DIM), lambda i: (i, 0)),
            hm(), hm(), hm(), hm(), hm(), hm(),
            pl.BlockSpec((2, RWKV_HEADS, tm // 8, 64), lambda i: (0, 0, i, 0)),
        ],
        out_shape=[
            jax.ShapeDtypeStruct((RWKV_HEADS, s, 64), BF16),
            jax.ShapeDtypeStruct((s, RWKV_DIM), BF16),
            jax.ShapeDtypeStruct((s, RWKV_DIM), BF16),
            hm_shape, hm_shape, hm_shape, hm_shape, hm_shape, hm_shape,
            jax.ShapeDtypeStruct((2, RWKV_HEADS, s // 8, 64), F32),
        ],
        compiler_params=_cparams(("parallel",)),
        name="rwkv_prep",
    )(p, p, p, p, p, p, mu_rkv, mu_lora, w0, w2p, a0, a2p, g2p, k_k, k_a, r_k, bd, tri)


def _bdot(a, b):
    return lax.dot_general(a, b, (((2,), (1,)), ((0,), (0,))), preferred_element_type=F32)


def _bdot_nt(a, b):
    return lax.dot_general(a, b, (((2,), (2,)), ((0,), (0,))), preferred_element_type=F32)


def _bdot_tn(a, b):
    return lax.dot_general(a, b, (((1,), (1,)), ((0,), (0,))), preferred_element_type=F32)


def _bdot_inv(a, b):
    return _bdot(a.astype(BF16), b.astype(BF16))


def _wkv_kernel(*refs):
    (vf_ref, vr_ref), ins, (gcf_ref, gcr_ref, yf_ref, yr_ref, s_ref) = refs[:2], refs[2:14], refs[14:]

    @pl.when(pl.program_id(0) == 0)
    def _():
        s_ref[...] = jnp.zeros_like(s_ref)

    _wkv_chunk(False, vf_ref, ins[0:6], gcf_ref, yf_ref, s_ref.at[0])
    _wkv_chunk(True, vr_ref, ins[6:12], gcr_ref, yr_ref, s_ref.at[1])


def _wkv_chunk(reverse, v_ref, scaled_refs, gc_ref, y_ref, s_ref):
    nh = RWKV_HEADS
    row = lax.broadcasted_iota(jnp.int32, (nh, CHUNK, CHUNK), 1)
    col = lax.broadcasted_iota(jnp.int32, (nh, CHUNK, CHUNK), 2)
    strict = (row < col) if reverse else (row > col)
    incl = (row <= col) if reverse else (row >= col)

    v = v_ref[...]
    rb, kb, kt, bt, kh, bh = [r[...] for r in scaled_refs]

    a_kk = jnp.where(strict, _bdot_nt(kb, kt), 0.0)
    l_mat = jnp.where(strict, _bdot_nt(kb, bt), 0.0)
    a_rk = jnp.where(incl, _bdot_nt(rb, kt), 0.0)
    a_rb = jnp.where(incl, _bdot_nt(rb, bt), 0.0)

    eye = jnp.where(row == col, 1.0, 0.0)
    t_inv = eye - l_mat
    pw = _bdot_inv(l_mat, l_mat)
    for step in range(5):
        t_inv = t_inv + _bdot_inv(t_inv, pw)
        if step < 4:
            pw = _bdot_inv(pw, pw)

    q1 = _bdot(a_kk.astype(BF16), v)
    q2 = _bdot(a_rk.astype(BF16), v)
    t_b = t_inv.astype(BF16)
    wk = _bdot(t_b, kb)
    uv = _bdot(t_b, q1.astype(BF16))
    a_rb_b = a_rb.astype(BF16)
    wk_b = wk.astype(BF16)
    uv_b = uv.astype(BF16)
    rw = rb.astype(F32) - _bdot(a_rb_b, wk_b)
    y0 = q2 - _bdot(a_rb_b, uv_b)
    m_mat = _bdot_tn(wk_b, bh)
    j_t = _bdot_tn(v, kh) - _bdot_tn(uv_b, bh)

    s_old = s_ref[...]
    s_hi = s_old.astype(BF16)
    s_lo = (s_old - s_hi.astype(F32)).astype(BF16)
    m_b = m_mat.astype(BF16)
    y_ref[...] = _bdot_nt(rw.astype(BF16), s_hi) + y0
    s_ref[...] = s_old * gc_ref[:, 0:1, :] - (_bdot(s_hi, m_b) + _bdot(s_lo, m_b)) + j_t


def _wkv(v, rb, kb, kt, bt, kh, bh, gc):
    s = v.shape[1]
    nc = s // CHUNK
    scaled = (rb, kb, kt, bt, kh, bh)
    fwd = lambda rows: pl.BlockSpec((None, RWKV_HEADS, rows, 64), lambda c: (0, 0, c, 0))
    rev = lambda rows: pl.BlockSpec((None, RWKV_HEADS, rows, 64), lambda c: (1, 0, nc - 1 - c, 0))
    y_shape = jax.ShapeDtypeStruct((RWKV_HEADS, s, 64), F32)
    return pl.pallas_call(
        _wkv_kernel,
        grid=(nc,),
        in_specs=[
            pl.BlockSpec((RWKV_HEADS, CHUNK, 64), lambda c: (0, c, 0)),
            pl.BlockSpec((RWKV_HEADS, CHUNK, 64), lambda c: (0, nc - 1 - c, 0)),
            *[fwd(CHUNK) for _ in scaled], *[rev(CHUNK) for _ in scaled],
            fwd(8), rev(8),
        ],
        out_specs=[
            pl.BlockSpec((RWKV_HEADS, CHUNK, 64), lambda c: (0, c, 0)),
            pl.BlockSpec((RWKV_HEADS, CHUNK, 64), lambda c: (0, nc - 1 - c, 0)),
        ],
        out_shape=[y_shape, y_shape],
        scratch_shapes=[pltpu.VMEM((2, RWKV_HEADS, 64, 64), F32)],
        compiler_params=_cparams(("arbitrary",)),
        name="wkv7_chunked",
    )(v, v, *scaled, *scaled, gc, gc)


def _rwkv_post_kernel(yf_ref, yr_ref, g_ref, bonus_ref, lg_ref, lb_ref, o_ref):
    y = yf_ref[...] + yr_ref[...]
    mean = jnp.mean(y, axis=-1, keepdims=True)
    yc = y - mean
    var = jnp.mean(yc * yc, axis=-1, keepdims=True)
    yn = yc * lax.rsqrt(var + LNX_EPS)
    yt = jnp.concatenate([yn[h] for h in range(RWKV_HEADS)], axis=-1)
    out = (yt * lg_ref[...] + lb_ref[...] + bonus_ref[...].astype(F32)) * g_ref[...].astype(F32)
    o_ref[...] = out.astype(o_ref.dtype)


def _rwkv_post(y_fwd, y_rev, g, bonus, lnx_g, lnx_b, *, tm):
    s = g.shape[0]
    return pl.pallas_call(
        _rwkv_post_kernel,
        grid=(s // tm,),
        in_specs=[
            pl.BlockSpec((RWKV_HEADS, tm, 64), lambda i: (0, i, 0)),
            pl.BlockSpec((RWKV_HEADS, tm, 64), lambda i: (0, i, 0)),
            pl.BlockSpec((tm, RWKV_DIM), lambda i: (i, 0)),
            pl.BlockSpec((tm, RWKV_DIM), lambda i: (i, 0)),
            pl.BlockSpec((1, RWKV_DIM), lambda i: (0, 0)),
            pl.BlockSpec((1, RWKV_DIM), lambda i: (0, 0)),
        ],
        out_specs=pl.BlockSpec((tm, RWKV_DIM), lambda i: (i, 0)),
        out_shape=jax.ShapeDtypeStruct((s, RWKV_DIM), BF16),
        compiler_params=_cparams(("parallel",)),
        name="rwkv_post",
    )(y_fwd, y_rev, g, bonus, lnx_g.reshape(1, RWKV_DIM), lnx_b.reshape(1, RWKV_DIM))


def _mem_attn_kernel(q_ref, kv_ref, gq_ref, gk_ref, o_ref):
    for h in range(MEM_HEADS):
        sl = slice(MEM_HEAD_DIM * h, MEM_HEAD_DIM * (h + 1))
        q = q_ref[:, sl].astype(F32)
        q = q * lax.rsqrt(jnp.mean(q * q, axis=-1, keepdims=True) + NORM_EPS)
        q = q * gq_ref[...] * (MEM_HEAD_DIM ** -0.5)
        km = kv_ref[:, sl].astype(F32)
        km = km * lax.rsqrt(jnp.mean(km * km, axis=-1, keepdims=True) + NORM_EPS)
        km = km * gk_ref[...]
        vm = kv_ref[:, MEM_DIM + MEM_HEAD_DIM * h:MEM_DIM + MEM_HEAD_DIM * (h + 1)]
        s = lax.dot_general(q.astype(BF16), km.astype(BF16), (((1,), (1,)), ((), ())),
                            preferred_element_type=F32)
        s = s - jnp.max(s, axis=-1, keepdims=True)
        e = jnp.exp(s)
        pr = e / jnp.sum(e, axis=-1, keepdims=True)
        o_ref[:, sl] = _dot(pr.astype(BF16), vm).astype(o_ref.dtype)


def _mem_attn(p, kv, gq, gk, *, tm):
    s = p.shape[0]
    return pl.pallas_call(
        _mem_attn_kernel,
        grid=(s // tm,),
        in_specs=[
            pl.BlockSpec((tm, MEM_DIM), lambda i: (i, COL_MEM // MEM_DIM)),
            pl.BlockSpec((N_MEM, 2 * MEM_DIM), lambda i: (0, 0)),
            pl.BlockSpec((1, MEM_HEAD_DIM), lambda i: (0, 0)),
            pl.BlockSpec((1, MEM_HEAD_DIM), lambda i: (0, 0)),
        ],
        out_specs=pl.BlockSpec((tm, MEM_DIM), lambda i: (i, 0)),
        out_shape=jax.ShapeDtypeStruct((s, MEM_DIM), BF16),
        compiler_params=_cparams(("parallel",)),
        name="mem_attn",
    )(p, kv, gq, gk)


def _merge_kernel(o0_ref, o1_ref, o2_ref, g0_ref, g1_ref, g2_ref, w_ref, m_ref):
    acc = jax.nn.sigmoid(g0_ref[...].astype(F32)) * _dot(o0_ref[...], w_ref[0])
    acc = acc + jax.nn.sigmoid(g1_ref[...].astype(F32)) * _dot(o1_ref[...], w_ref[1])
    acc = acc + jax.nn.sigmoid(g2_ref[...].astype(F32)) * _dot(o2_ref[...], w_ref[2])
    m_ref[...] = acc.astype(m_ref.dtype)


def _merge(o_diff, o_rwkv, o_mem, p, w_branch, layer, *, tm, tn):
    s = p.shape[0]
    gate_blk = COL_GATE // tn
    per = D_MODEL // tn
    o_spec = lambda: pl.BlockSpec((tm, 1024), lambda i, j: (i, 0))
    g_spec = lambda b: pl.BlockSpec((tm, tn), lambda i, j: (i, gate_blk + b * per + j))
    return pl.pallas_call(
        _merge_kernel,
        grid=(s // tm, per),
        in_specs=[
            o_spec(), o_spec(), o_spec(),
            g_spec(0), g_spec(1), g_spec(2),
            pl.BlockSpec((None, N_BRANCH, 1024, tn), lambda i, j: (layer, 0, 0, j)),
        ],
        out_specs=pl.BlockSpec((tm, tn), lambda i, j: (i, j)),
        out_shape=jax.ShapeDtypeStruct((s, D_MODEL), BF16),
        compiler_params=_cparams(("parallel", "arbitrary")),
        name="merge",
    )(o_diff, o_rwkv, o_mem, p, p, p, w_branch)


GLU_TILE = 256


def _ffn_up_glu_kernel(x_ref, xp_ref, xn_ref, g_ref, wg_ref, wv_ref, cw_ref, cb_ref, o_ref,
                       h_ref, halo_ref):
    i = pl.program_id(0)
    nb = pl.num_programs(0)

    def norm(x):
        ms = jnp.mean(x * x, axis=-1, keepdims=True)
        return x * lax.rsqrt(ms + NORM_EPS) * g_ref[...]

    @pl.when(pl.program_id(1) == 0)
    def _():
        h_ref[...] = norm(x_ref[...]).astype(BF16)
        row = lax.broadcasted_iota(jnp.int32, xp_ref.shape, 0)
        before = jnp.where(i > 0, pltpu.roll(norm(xp_ref[...]), 1, 0), 0.0)
        after = jnp.where(i < nb - 1, pltpu.roll(norm(xn_ref[...]), 1, 0), 0.0)
        halo_ref[...] = jnp.where(row == 0, before, jnp.where(row == 1, after, 0.0)).astype(BF16)

    h = h_ref[...]
    halo = halo_ref[...]
    for c in range(o_ref.shape[1] // GLU_TILE):
        sl = slice(c * GLU_TILE, (c + 1) * GLU_TILE)
        wg = wg_ref[:, sl]
        ug = _dot(h, wg)
        uv = _dot(h, wv_ref[:, sl])
        edge = _dot(halo, wg)
        gp, gn = _shift_rows(ug, edge[0:1], edge[1:2])
        cw = cw_ref[:, sl]
        conv = cw[0:1] * gp + cw[1:2] * ug + cw[2:3] * gn + cb_ref[:, sl]
        half = 0.5 * conv
        o_ref[:, sl] = ((half + half * jnp.tanh(half)) * uv).astype(o_ref.dtype)


def _ffn_up_glu(x, g, w_up, layer, conv_w, conv_b, *, tm, tn):
    s, k = x.shape
    nj = D_FF // tn
    hb = tm // 16
    last16 = s // 16 - 1
    return pl.pallas_call(
        _ffn_up_glu_kernel,
        grid=(s // tm, nj),
        in_specs=[
            pl.BlockSpec((tm, k), lambda i, j: (i, 0)),
            pl.BlockSpec((16, k), lambda i, j: (jnp.maximum(i * hb - 1, 0), 0)),
            pl.BlockSpec((16, k), lambda i, j: (jnp.minimum((i + 1) * hb, last16), 0)),
            pl.BlockSpec((1, k), lambda i, j: (0, 0)),
            pl.BlockSpec((None, k, tn), lambda i, j: (layer, 0, j)),
            pl.BlockSpec((None, k, tn), lambda i, j: (layer, 0, nj + j)),
            pl.BlockSpec((3, tn), lambda i, j: (0, j)),
            pl.BlockSpec((1, tn), lambda i, j: (0, j)),
        ],
        out_specs=pl.BlockSpec((tm, tn), lambda i, j: (i, j)),
        out_shape=jax.ShapeDtypeStruct((s, D_FF), BF16),
        scratch_shapes=[pltpu.VMEM((tm, k), BF16), pltpu.VMEM((16, k), BF16)],
        compiler_params=_cparams(("parallel", "arbitrary")),
        name="ffn_up_glu",
    )(x, x, x, g.reshape(1, k), w_up, w_up, conv_w, conv_b.reshape(1, D_FF))


def _block_ones(n, group):
    idx = np.arange(n) // group
    return jnp.asarray(idx[:, None] == idx[None, :], BF16)


def _chunk_tri(n):
    idx = np.arange(n)
    same = (idx[:, None] // CHUNK) == (idx[None, :] // CHUNK)
    return jnp.asarray(same & (idx[:, None] >= idx[None, :]), BF16)


def _pad_rows(w, rows_before, total):
    n = w.shape[-1]
    out = jnp.zeros((total, n), w.dtype)
    return lax.dynamic_update_slice(out, w, (rows_before, 0))


def _pad_w_in(w):
    lora0 = COL_RKV + 3 * RWKV_DIM
    lora1 = lora0 + LORA_COLS
    pad = jnp.zeros((w.shape[0], LORA_PAD - LORA_COLS), w.dtype)
    return jnp.concatenate([w[:, :lora0], w[:, lora1:], w[:, lora0:lora1], pad], axis=1).astype(BF16)


def kernel(x, mem, attn_norm_g, w_in, diff_qk_g, diff_lambda, diff_subln_g, rwkv_mu, rwkv_w0,
           rwkv_w2, rwkv_a0, rwkv_a2, rwkv_g2, rwkv_k_k, rwkv_k_a, rwkv_r_k, rwkv_lnx_g,
           rwkv_lnx_b, mem_norm_g, w_mem_kv, mem_qk_g, w_branch, w_out, ffn_norm_g, w_ffn_up,
           ffn_conv_w, ffn_conv_b, w_ffn_down):
    b, s, d = x.shape
    assert b == 1 and d == D_MODEL and s % ATT_BQ == 0
    xs = x.reshape(s, d)
    mem2 = mem.reshape(N_MEM, d)
    prep_tm = 256
    bd64 = _block_ones(1024, 64)
    tri = _chunk_tri(prep_tm)
    n_rwkv_main = 3 * RWKV_DIM
    w_mem_kv_b = w_mem_kv.astype(BF16)
    w_branch_b = w_branch.astype(BF16)
    w_out_b = w_out.astype(BF16)
    w_ffn_up_b = w_ffn_up.astype(BF16)
    w_ffn_down_b = w_ffn_down.astype(BF16)

    for l in range(DEPTH):
        lam_init = 0.8 - 0.6 * math.exp(-0.3 * l)
        p = _rms_mm(xs, attn_norm_g[l], _pad_w_in(w_in[l])[None], 0, tm=512, tn=1536,
                    name="rms_w_in")

        gq = jnp.tile(diff_qk_g[l, 0].reshape(1, 128), (1, DIFF_HEADS))
        gk = jnp.tile(diff_qk_g[l, 1].reshape(1, 128), (1, DIFF_HEADS))
        qc, kc, score_bound = _attn_consts(diff_qk_g[l])
        qa, ka, vt = _diff_prep(p, gq, gk, qc, kc, bd64, tm=256)
        o_diff = lax.cond(
            score_bound <= SCORE_BOUND_MAX,
            functools.partial(_diff_attn, lam_init=lam_init, online=False),
            functools.partial(_diff_attn, lam_init=lam_init, online=True),
            qa, ka, vt, diff_lambda[l], diff_subln_g[l])

        mu = rwkv_mu[l]
        mu_rkv = mu[:, :n_rwkv_main]
        mu_lora = jnp.pad(mu[:, n_rwkv_main:], ((0, 0), (0, LORA_PAD - LORA_COLS)))
        w2p = jnp.stack([_pad_rows(rwkv_w2[l, dd], 64 * dd, 128) for dd in range(2)]).astype(BF16)
        a2p = jnp.stack([_pad_rows(rwkv_a2[l, dd], 64 * dd, 128) for dd in range(2)]).astype(BF16)
        g2p = _pad_rows(rwkv_g2[l], 0, 256).astype(BF16)
        (v_h, g_tok, bonus, rb, kb, kt, bt, kh, bh, gc) = _rwkv_prep(
            p, mu_rkv, mu_lora, rwkv_w0[l], w2p, rwkv_a0[l], a2p, g2p,
            rwkv_k_k[l].reshape(1, RWKV_DIM), rwkv_k_a[l].reshape(1, RWKV_DIM),
            rwkv_r_k[l].reshape(1, RWKV_DIM), bd64, tri, tm=prep_tm)
        y_fwd, y_rev = _wkv(v_h, rb, kb, kt, bt, kh, bh, gc)
        o_rwkv = _rwkv_post(y_fwd, y_rev, g_tok, bonus, rwkv_lnx_g[l], rwkv_lnx_b[l], tm=256)

        kv = _rms_mm(mem2, mem_norm_g[l], w_mem_kv_b, l, tm=N_MEM, tn=1024, name="rms_mem_kv")
        o_mem = _mem_attn(p, kv, mem_qk_g[l, 0].reshape(1, MEM_HEAD_DIM),
                          mem_qk_g[l, 1].reshape(1, MEM_HEAD_DIM), tm=512)

        merged = _merge(o_diff, o_rwkv, o_mem, p, w_branch_b, l, tm=512, tn=512)
        xs = _mm_res(merged, w_out_b, l, xs, tm=512, tn=1024, name="w_out_res")

        act = _ffn_up_glu(xs, ffn_norm_g[l], w_ffn_up_b, l, ffn_conv_w[l], ffn_conv_b[l],
                          tm=512, tn=512)
        xs = _mm_res(act, w_ffn_down_b, l, xs, tm=512, tn=512, name="ffn_down_res")

    return xs.reshape(b, s, d)
```

```python
import functools
import math

import jax
import jax.numpy as jnp
import numpy as np
from jax import lax
from jax.experimental import pallas as pl
from jax.experimental.pallas import tpu as pltpu

F32 = jnp.float32
BF16 = jnp.bfloat16

D_MODEL = 2048
DEPTH = 2
DIFF_HEADS = 8
DIFF_QK_DIM = 64
DIFF_V_DIM = 128
DIFF_DIM = 1024
RWKV_HEADS = 16
RWKV_HEAD_DIM = 64
RWKV_DIM = 1024
DECAY_LORA = 64
AAA_LORA = 64
GATE_LORA = 160
LORA_COLS = 2 * DECAY_LORA + 2 * AAA_LORA + GATE_LORA
LORA_PAD = 512
N_MEM = 256
MEM_HEADS = 4
MEM_HEAD_DIM = 256
MEM_DIM = 1024
N_BRANCH = 3
D_FF = 5632
NORM_EPS = 1e-6
LNX_EPS = 64e-5

COL_DIFF = 0
COL_RKV = 3 * DIFF_DIM
COL_MEM = COL_RKV + 3 * RWKV_DIM
COL_GATE = COL_MEM + MEM_DIM
COL_LORA = COL_GATE + N_BRANCH * D_MODEL
N_IN_PAD = COL_LORA + LORA_PAD

CHUNK = 64
LOG2E = 1.4426950408889634
N_POS_COLS = 12
SCORE_BOUND_MAX = 30.0
ATT_BQ = 512
ATT_BK = 512
V_AUG = 144
FAR_UNROLL = 6
VMEM_LIMIT = 48 * 1024 * 1024


def _cparams(sem):
    return pltpu.CompilerParams(dimension_semantics=sem, vmem_limit_bytes=VMEM_LIMIT)


def _split2(x):
    hi = x.astype(BF16)
    return hi, (x - hi.astype(F32)).astype(BF16)


def _dot(a, b):
    return jnp.dot(a, b, preferred_element_type=F32)


def _dot_exactish(a_bf16_exact, x_f32):
    hi, lo = _split2(x_f32)
    return _dot(a_bf16_exact, hi) + _dot(a_bf16_exact, lo)


def _rms_mm_kernel(x_ref, g_ref, w_ref, o_ref, h_ref):
    @pl.when(pl.program_id(1) == 0)
    def _():
        x = x_ref[...]
        ms = jnp.mean(x * x, axis=-1, keepdims=True)
        h_ref[...] = (x * lax.rsqrt(ms + NORM_EPS) * g_ref[...]).astype(BF16)

    o_ref[...] = _dot(h_ref[...], w_ref[...]).astype(o_ref.dtype)


def _rms_mm(x, g, w, layer, *, tm, tn, name):
    m, k = x.shape
    n = w.shape[2]
    return pl.pallas_call(
        _rms_mm_kernel,
        grid=(m // tm, n // tn),
        in_specs=[
            pl.BlockSpec((tm, k), lambda i, j: (i, 0)),
            pl.BlockSpec((1, k), lambda i, j: (0, 0)),
            pl.BlockSpec((None, k, tn), lambda i, j: (layer, 0, j)),
        ],
        out_specs=pl.BlockSpec((tm, tn), lambda i, j: (i, j)),
        out_shape=jax.ShapeDtypeStruct((m, n), BF16),
        scratch_shapes=[pltpu.VMEM((tm, k), BF16)],
        compiler_params=_cparams(("parallel", "arbitrary")),
        name=name,
    )(x, g.reshape(1, k), w)


def _mm_res_kernel(a_ref, w_ref, r_ref, o_ref):
    o_ref[...] = r_ref[...] + _dot(a_ref[...], w_ref[...])


def _mm_res(a, w, layer, res, *, tm, tn, name):
    m, k = a.shape
    n = w.shape[2]
    return pl.pallas_call(
        _mm_res_kernel,
        grid=(m // tm, n // tn),
        in_specs=[
            pl.BlockSpec((tm, k), lambda i, j: (i, 0)),
            pl.BlockSpec((None, k, tn), lambda i, j: (layer, 0, j)),
            pl.BlockSpec((tm, tn), lambda i, j: (i, j)),
        ],
        out_specs=pl.BlockSpec((tm, tn), lambda i, j: (i, j)),
        out_shape=jax.ShapeDtypeStruct((m, n), F32),
        compiler_params=_cparams(("parallel", "arbitrary")),
        name=name,
    )(a, w, res)


def _group_mean_sq(x, bd_ref, group):
    hi, lo = _split2(x * x)
    bd = bd_ref[...]
    return (_dot(hi, bd) + _dot(lo, bd)) * (1.0 / group)


def _aug_base(mp):
    return 64 if mp == 0 else 0


def _diff_prep_kernel(q_ref, k_ref, v_ref, gq_ref, gk_ref, qc_ref, kc_ref, bd_ref,
                      qa_ref, ka_ref, vt_ref):
    tm = q_ref.shape[0]
    row0 = pl.program_id(0) * tm
    lane = lax.broadcasted_iota(jnp.int32, (tm, 128), 1)
    pos = row0 + lax.broadcasted_iota(jnp.int32, (tm, 128), 0)
    pos_lo = (pos & 127).astype(F32)
    pos_hi = (pos >> 7).astype(F32)

    q = q_ref[...].astype(F32)
    qn = q * lax.rsqrt(_group_mean_sq(q, bd_ref, DIFF_QK_DIM) + NORM_EPS)
    qn = qn * gq_ref[...] * (DIFF_QK_DIM ** -0.5 * LOG2E)
    k = k_ref[...].astype(F32)
    kn = k * lax.rsqrt(_group_mean_sq(k, bd_ref, DIFF_QK_DIM) + NORM_EPS)
    kn = kn * gk_ref[...]

    for h in range(DIFF_HEADS):
        qh = qn[:, 128 * h:128 * (h + 1)]
        kh = kn[:, 128 * h:128 * (h + 1)]
        for mp in range(2):
            a0 = _aug_base(mp)
            g = 2 * h + mp
            is_data = (lane < 64) if mp == 0 else (lane >= 64)
            aug_q = jnp.where(lane < a0 + 3, pos_lo,
                              jnp.where(lane < a0 + 6, pos_hi, qc_ref[g:g + 1, :]))
            aug_k = jnp.where((lane >= a0 + 6) & (lane < a0 + 9), pos_lo,
                              jnp.where((lane >= a0 + 9) & (lane < a0 + 12), pos_hi,
                                        kc_ref[g:g + 1, :]))
            qa_ref[h, mp] = jnp.where(is_data, qh, aug_q).astype(BF16)
            ka_ref[h, mp] = jnp.where(is_data, kh, aug_k).astype(BF16)

    vt = v_ref[...].astype(F32).T
    sub = lax.broadcasted_iota(jnp.int32, (V_AUG - DIFF_V_DIM, tm), 0)
    ones_rows = jnp.where(sub == 0, 1.0, 0.0).astype(BF16)
    for h in range(DIFF_HEADS):
        vt_ref[h, 0:DIFF_V_DIM, :] = vt[128 * h:128 * (h + 1), :].astype(BF16)
        vt_ref[h, DIFF_V_DIM:V_AUG, :] = ones_rows


def _bf16_split3_const(x):
    parts = []
    for _ in range(3):
        part = float(np.asarray(x, np.float32).astype(BF16).astype(np.float32))
        parts.append(part)
        x = x - part
    return parts


def _attn_consts(qk_g):
    l_parts = _bf16_split3_const(LOG2E)
    m_nat = 8.0 * jnp.max(jnp.abs(qk_g[0]), axis=-1) * jnp.max(jnp.abs(qk_g[1]), axis=-1)
    qc = np.zeros((2 * DIFF_HEADS, 128), np.float32)
    kc = np.zeros((2 * DIFF_HEADS, 128), np.float32)
    bound_lane = np.zeros((2, 2 * DIFF_HEADS, 128), np.float32)
    for h in range(DIFF_HEADS):
        slope = 2.0 ** (-(h + 1))
        for mp in range(2):
            a0 = _aug_base(mp)
            g = 2 * h + mp
            for t, lp in enumerate(l_parts):
                qc[g, a0 + 6 + t] = slope * lp
                qc[g, a0 + 9 + t] = 128.0 * slope * lp
                kc[g, a0 + t] = -slope * lp
                kc[g, a0 + 3 + t] = -128.0 * slope * lp
            bound_lane[mp, g, a0 + N_POS_COLS] = 1.0
            kc[g, a0 + N_POS_COLS] = 1.0
    m2 = -m_nat * LOG2E
    qc = qc + m2[0] * bound_lane[0] + m2[1] * bound_lane[1]
    return qc, jnp.asarray(kc), jnp.max(m_nat)


def _diff_prep(p, gq, gk, qc, kc, bd, *, tm):
    s = p.shape[0]
    return pl.pallas_call(
        _diff_prep_kernel,
        grid=(s // tm,),
        in_specs=[
            pl.BlockSpec((tm, DIFF_DIM), lambda i: (i, 0)),
            pl.BlockSpec((tm, DIFF_DIM), lambda i: (i, 1)),
            pl.BlockSpec((tm, DIFF_DIM), lambda i: (i, 2)),
            pl.BlockSpec((1, DIFF_DIM), lambda i: (0, 0)),
            pl.BlockSpec((1, DIFF_DIM), lambda i: (0, 0)),
            pl.BlockSpec((2 * DIFF_HEADS, 128), lambda i: (0, 0)),
            pl.BlockSpec((2 * DIFF_HEADS, 128), lambda i: (0, 0)),
            pl.BlockSpec((DIFF_DIM, DIFF_DIM), lambda i: (0, 0)),
        ],
        out_specs=[
            pl.BlockSpec((DIFF_HEADS, 2, tm, 128), lambda i: (0, 0, i, 0)),
            pl.BlockSpec((DIFF_HEADS, 2, tm, 128), lambda i: (0, 0, i, 0)),
            pl.BlockSpec((DIFF_HEADS, V_AUG, tm), lambda i: (0, 0, i)),
        ],
        out_shape=[
            jax.ShapeDtypeStruct((DIFF_HEADS, 2, s, 128), BF16),
            jax.ShapeDtypeStruct((DIFF_HEADS, 2, s, 128), BF16),
            jax.ShapeDtypeStruct((DIFF_HEADS, V_AUG, s), BF16),
        ],
        compiler_params=_cparams(("parallel",)),
        name="diff_prep",
    )(p, p, p, gq, gk, qc, kc, bd)


def _diff_attn_kernel(lam_ref, sg_ref, dnear_ref, qa_ref, ka_ref, vt_ref, o_ref,
                      acc_ref, qv_ref, *mode_refs, lam_init, online):
    h = pl.program_id(0)
    i = pl.program_id(1)
    bq = qa_ref.shape[1]
    s_len = ka_ref.shape[1]
    bk = ATT_BK
    n_chunks = s_len // bk
    per_q = bq // bk
    n_far = n_chunks - per_q
    j_lo = i * per_q
    q0 = i * bq
    slope2 = jnp.exp2(-jnp.full((1, 1), h + 1, jnp.int32).astype(F32)) * LOG2E
    if online:
        m_ref, sa_ref, sb_ref, mxa_ref, mxb_ref = mode_refs
        m_ref[...] = jnp.full_like(m_ref, -1e30)
        bufs = ((sa_ref, mxa_ref), (sb_ref, mxb_ref))
    else:
        bufs = mode_refs
    acc_ref[...] = jnp.zeros_like(acc_ref)

    lane = lax.broadcasted_iota(jnp.int32, (1, 128), 1)
    for mp in range(2):
        q = qa_ref[mp]
        pos_lane = (lane >= _aug_base(mp)) & (lane < _aug_base(mp) + N_POS_COLS)
        qv_ref[mp, 0] = q
        qv_ref[mp, 1] = jnp.where(pos_lane, -q, q)
        qv_ref[mp, 2] = jnp.where(pos_lane, jnp.zeros_like(q), q)

    def stage_one(k0, variant, bias, buf):
        for mp in range(2):
            kc = ka_ref[mp, pl.ds(k0, bk), :]
            s = lax.dot_general(kc, qv_ref[mp, variant], (((1,), (1,)), ((), ())),
                                preferred_element_type=F32)
            if bias is not None:
                s = s + bias
            if online:
                buf[0][mp] = s
                buf[1][mp] = jnp.max(s, axis=0, keepdims=True)
            else:
                buf[mp] = jnp.exp2(s).astype(BF16)

    def stage_two(k0, buf):
        vt_c = vt_ref[:, pl.ds(k0, bk)]
        for mp in range(2):
            if online:
                m_old = m_ref[mp]
                m_new = jnp.maximum(m_old, buf[1][mp])
                p = jnp.exp2(buf[0][mp] - m_new).astype(BF16)
                acc_ref[mp] = acc_ref[mp] * jnp.exp2(m_old - m_new) + _dot(vt_c, p)
                m_ref[mp] = m_new
            else:
                acc_ref[mp] += _dot(vt_c, buf[mp])

    def near_start(t):
        return pl.multiple_of(q0 + t * bk, bk)

    def far_start(t):
        t = jnp.minimum(t, n_far - 1)
        j = jnp.where(t < j_lo, t, t + per_q)
        return pl.multiple_of(j * bk, bk), (t >= j_lo).astype(jnp.int32)

    def near_one(t):
        stage_one(near_start(t), 2, slope2 * dnear_ref[t], bufs[t % 2])

    def far_one(t, parity):
        k0, after = far_start(t)
        stage_one(k0, after, None, bufs[parity])

    def far_two(t, parity):
        stage_two(far_start(t)[0], bufs[parity])

    near_one(0)
    for t in range(1, per_q):
        near_one(t)
        stage_two(near_start(t - 1), bufs[(t - 1) % 2])
    far_one(0, per_q % 2)
    stage_two(near_start(per_q - 1), bufs[(per_q - 1) % 2])

    def far_body(u, c):
        for r in range(FAR_UNROLL):
            far_one(FAR_UNROLL * u + r + 1, (per_q + r + 1) % 2)
            far_two(FAR_UNROLL * u + r, (per_q + r) % 2)
        return c

    lax.fori_loop(0, n_far // FAR_UNROLL, far_body, 0)
    for t in range(n_far - n_far % FAR_UNROLL, n_far):
        if t + 1 < n_far:
            far_one(t + 1, (per_q + t + 1) % 2)
        far_two(t, (per_q + t) % 2)

    lp = lam_ref[...]
    lam = (jnp.exp(jnp.sum(lp[0:1] * lp[1:2], axis=-1, keepdims=True))
           - jnp.exp(jnp.sum(lp[2:3] * lp[3:4], axis=-1, keepdims=True)) + lam_init)
    a0 = acc_ref[0]
    a1 = acc_ref[1]
    o = (a0[0:DIFF_V_DIM] / a0[DIFF_V_DIM:DIFF_V_DIM + 1]
         - lam * (a1[0:DIFF_V_DIM] / a1[DIFF_V_DIM:DIFF_V_DIM + 1]))
    ms = jnp.mean(o * o, axis=0, keepdims=True)
    o = o * lax.rsqrt(ms + NORM_EPS) * sg_ref[...] * (1.0 - lam_init)
    o_ref[...] = o.T.astype(o_ref.dtype)


def _diff_attn(qa, ka, vt, lam_p, subln_g, *, lam_init, online):
    s = qa.shape[2]
    bq = ATT_BQ
    assert bq % ATT_BK == 0 and s > bq
    kern = functools.partial(_diff_attn_kernel, lam_init=lam_init, online=online)
    per_q = bq // ATT_BK
    key = np.arange(per_q * ATT_BK).reshape(per_q, ATT_BK, 1)
    dnear = jnp.asarray(-np.abs(np.arange(bq).reshape(1, 1, bq) - key), F32)
    scratch = [
        pltpu.VMEM((2, V_AUG, bq), F32),
        pltpu.VMEM((2, 3, bq, 128), BF16),
    ]
    if online:
        scratch += [
            pltpu.VMEM((2, 1, bq), F32),
            pltpu.VMEM((2, ATT_BK, bq), F32),
            pltpu.VMEM((2, ATT_BK, bq), F32),
            pltpu.VMEM((2, 1, bq), F32),
            pltpu.VMEM((2, 1, bq), F32),
        ]
    else:
        scratch += [pltpu.VMEM((2, ATT_BK, bq), BF16), pltpu.VMEM((2, ATT_BK, bq), BF16)]
    return pl.pallas_call(
        kern,
        grid=(DIFF_HEADS, s // bq),
        in_specs=[
            pl.BlockSpec((4, DIFF_QK_DIM), lambda h, i: (0, 0)),
            pl.BlockSpec((DIFF_V_DIM, 1), lambda h, i: (0, 0)),
            pl.BlockSpec((per_q, ATT_BK, bq), lambda h, i: (0, 0, 0)),
            pl.BlockSpec((None, 2, bq, 128), lambda h, i: (h, 0, i, 0)),
            pl.BlockSpec((None, 2, s, 128), lambda h, i: (h, 0, 0, 0)),
            pl.BlockSpec((None, V_AUG, s), lambda h, i: (h, 0, 0)),
        ],
        out_specs=pl.BlockSpec((bq, DIFF_V_DIM), lambda h, i: (i, h)),
        out_shape=jax.ShapeDtypeStruct((s, DIFF_DIM), BF16),
        scratch_shapes=scratch,
        compiler_params=_cparams(("parallel", "arbitrary")),
        name="diff_attn_online" if online else "diff_attn",
    )(lam_p, subln_g.reshape(DIFF_V_DIM, 1), dnear, qa, ka, vt)


def _shift_rows(x, prev_row, next_row):
    n = x.shape[0]
    row = lax.broadcasted_iota(jnp.int32, x.shape, 0)
    xp = jnp.where(row == 0, prev_row, pltpu.roll(x, 1, 0))
    xn = jnp.where(row == n - 1, next_row, pltpu.roll(x, n - 1, 0))
    return xp, xn


def _halo_rows(prev_ref, next_ref, i, n_blocks):
    hp = prev_ref.shape[0]
    prev_row = prev_ref[hp - 1:hp, :].astype(F32)
    next_row = next_ref[0:1, :].astype(F32)
    prev_row = jnp.where(i > 0, prev_row, 0.0)
    next_row = jnp.where(i < n_blocks - 1, next_row, 0.0)
    return prev_row, next_row


def _rwkv_prep_kernel(
        x_ref, xp_ref, xn_ref, l_ref, lp_ref, ln_ref,
        mu_ref, mul_ref, w0_ref, w2_ref, a0_ref, a2_ref, g2_ref, kk_ref, ka_ref, rk_ref,
        bd_ref, tri_ref,
        v_out, g_out, bonus_out, rb_out, kb_out, kt_out, bt_out, kh_out, bh_out, gc_out):
    i = pl.program_id(0)
    nb = pl.num_programs(0)
    tm = x_ref.shape[0]

    x = x_ref[...].astype(F32)
    prev_row, next_row = _halo_rows(xp_ref, xn_ref, i, nb)
    xp, xn = _shift_rows(x, prev_row, next_row)
    mu = mu_ref[...]
    x = x + mu[0:1] * (xp - x) + mu[1:2] * (xn - x)
    lo = l_ref[...].astype(F32)
    prev_row, next_row = _halo_rows(lp_ref, ln_ref, i, nb)
    lop, lon = _shift_rows(lo, prev_row, next_row)
    mul = mul_ref[...]
    lo = lo + mul[0:1] * (lop - lo) + mul[1:2] * (lon - lo)

    r = x[:, 0:RWKV_DIM]
    k = x[:, RWKV_DIM:2 * RWKV_DIM]
    v = x[:, 2 * RWKV_DIM:3 * RWKV_DIM]
    tw = jnp.tanh(lo[:, 0:128]).astype(BF16)
    la = lo[:, 128:256].astype(BF16)
    lg = jax.nn.sigmoid(lo[:, 256:512]).astype(BF16)

    g = _dot(lg, g2_ref[...])
    kk = k * kk_ref[...]
    ss = _group_mean_sq(kk, bd_ref, 1.0)
    kk = kk * lax.rsqrt(jnp.maximum(ss, 1e-24))
    hi, lo_part = _split2(r * k * rk_ref[...])
    bonus = (_dot(hi, bd_ref[...]) + _dot(lo_part, bd_ref[...])) * v

    g_out[...] = g.astype(g_out.dtype)
    bonus_out[...] = bonus.astype(bonus_out.dtype)
    for h in range(RWKV_HEADS):
        v_out[h] = v[:, 64 * h:64 * (h + 1)].astype(BF16)

    tri = tri_ref[...]
    for d in range(2):
        wl = w0_ref[d:d + 1, :] + _dot(tw, w2_ref[d])
        logdec = -math.exp(-0.5) * jax.nn.sigmoid(wl)
        a = jax.nn.sigmoid(a0_ref[d:d + 1, :] + _dot(la, a2_ref[d]))
        k_d = k * (1.0 + (a - 1.0) * ka_ref[...])
        b_d = kk * a
        pre = _dot_exactish(tri, logdec)
        chunk_tot = jnp.broadcast_to(
            pre.reshape(tm // CHUNK, CHUNK, RWKV_DIM)[:, CHUNK - 1:CHUNK, :],
            (tm // CHUNK, CHUNK, RWKV_DIM)).reshape(tm, RWKV_DIM)
        suf = chunk_tot - pre
        if d == 0:
            lc, ex, rem = pre, pre - logdec, suf
        else:
            lc, ex, rem = suf + logdec, suf, pre - logdec
        e_neg = jnp.exp(-lc)
        e_rem = jnp.exp(rem)
        outs = (
            (rb_out, r * jnp.exp(lc)),
            (kb_out, kk * jnp.exp(ex)),
            (kt_out, k_d * e_neg),
            (bt_out, b_d * e_neg),
            (kh_out, k_d * e_rem),
            (bh_out, b_d * e_rem),
        )
        for ref, val in outs:
            val = val.astype(BF16)
            for h in range(RWKV_HEADS):
                ref[d, h] = val[:, 64 * h:64 * (h + 1)]
        tot = jnp.exp(chunk_tot).reshape(tm // 8, 8, RWKV_DIM)[:, 0, :]
        for h in range(RWKV_HEADS):
            gc_out[d, h] = tot[:, 64 * h:64 * (h + 1)]


def _rwkv_prep(p, mu_rkv, mu_lora, w0, w2p, a0, a2p, g2p, k_k, k_a, r_k, bd, tri, *, tm):
    s = p.shape[0]
    nb = s // tm
    hb = tm // 16
    last16 = s // 16 - 1
    rkv_w = 3 * RWKV_DIM
    c_rkv = COL_RKV // rkv_w
    c_lora = COL_LORA // LORA_PAD

    def prev_map(c):
        return lambda i: (jnp.maximum(i * hb - 1, 0), c)

    def next_map(c):
        return lambda i: (jnp.minimum((i + 1) * hb, last16), c)

    full = lambda *shape: pl.BlockSpec(shape, lambda i: (0,) * len(shape))
    hm = lambda: pl.BlockSpec((2, RWKV_HEADS, tm, 64), lambda i: (0, 0, i, 0))
    hm_shape = jax.ShapeDtypeStruct((2, RWKV_HEADS, s, 64), BF16)
    return pl.pallas_call(
        _rwkv_prep_kernel,
        grid=(nb,),
        in_specs=[
            pl.BlockSpec((tm, rkv_w), lambda i: (i, c_rkv)),
            pl.BlockSpec((16, rkv_w), prev_map(c_rkv)),
            pl.BlockSpec((16, rkv_w), next_map(c_rkv)),
            pl.BlockSpec((tm, LORA_PAD), lambda i: (i, c_lora)),
            pl.BlockSpec((16, LORA_PAD), prev_map(c_lora)),
            pl.BlockSpec((16, LORA_PAD), next_map(c_lora)),
            full(2, rkv_w), full(2, LORA_PAD),
            full(2, RWKV_DIM), full(2, 128, RWKV_DIM),
            full(2, RWKV_DIM), full(2, 128, RWKV_DIM),
            full(256, RWKV_DIM),
            full(1, RWKV_DIM), full(1, RWKV_DIM), full(1, RWKV_DIM),
            full(RWKV_DIM, RWKV_DIM), full(tm, tm),
        ],
        out_specs=[
            pl.BlockSpec((RWKV_HEADS, tm, 64), lambda i: (0, i, 0)),
            pl.BlockSpec((tm, RWKV_DIM), lambda i: (i, 0)),
            pl.BlockSpec((tm, RWKV_DIM), lambda i: (i, 0)),
            hm(), hm(), hm(), hm(), hm(), hm(),
            pl.BlockSpec((2, RWKV_HEADS, tm // 8, 64), lambda i: (0, 0, i, 0)),
        ],
        out_shape=[
            jax.ShapeDtypeStruct((RWKV_HEADS, s, 64), BF16),
            jax.ShapeDtypeStruct((s, RWKV_DIM), BF16),
            jax.ShapeDtypeStruct((s, RWKV_DIM), BF16),
            hm_shape, hm_shape, hm_shape, hm_shape, hm_shape, hm_shape,
            jax.ShapeDtypeStruct((2, RWKV_HEADS, s // 8, 64), F32),
        ],
        compiler_params=_cparams(("parallel",)),
        name="rwkv_prep",
    )(p, p, p, p, p, p, mu_rkv, mu_lora, w0, w2p, a0, a2p, g2p, k_k, k_a, r_k, bd, tri)


def _bdot(a, b):
    return lax.dot_general(a, b, (((2,), (1,)), ((0,), (0,))), preferred_element_type=F32)


def _bdot_nt(a, b):
    return lax.dot_general(a, b, (((2,), (2,)), ((0,), (0,))), preferred_element_type=F32)


def _bdot_tn(a, b):
    return lax.dot_general(a, b, (((1,), (1,)), ((0,), (0,))), preferred_element_type=F32)


def _bdot_inv(a, b):
    return _bdot(a.astype(BF16), b.astype(BF16))


def _wkv_kernel(*refs):
    (vf_ref, vr_ref), ins, (gcf_ref, gcr_ref, yf_ref, yr_ref, s_ref) = refs[:2], refs[2:14], refs[14:]

    @pl.when(pl.program_id(0) == 0)
    def _():
        s_ref[...] = jnp.zeros_like(s_ref)

    _wkv_chunk(False, vf_ref, ins[0:6], gcf_ref, yf_ref, s_ref.at[0])
    _wkv_chunk(True, vr_ref, ins[6:12], gcr_ref, yr_ref, s_ref.at[1])


def _wkv_chunk(reverse, v_ref, scaled_refs, gc_ref, y_ref, s_ref):
    nh = RWKV_HEADS
    row = lax.broadcasted_iota(jnp.int32, (nh, CHUNK, CHUNK), 1)
    col = lax.broadcasted_iota(jnp.int32, (nh, CHUNK, CHUNK), 2)
    strict = (row < col) if reverse else (row > col)
    incl = (row <= col) if reverse else (row >= col)

    v = v_ref[...]
    rb, kb, kt, bt, kh, bh = [r[...] for r in scaled_refs]

    c = CHUNK
    kr = jnp.concatenate([kb, rb], axis=1)
    sk = _bdot_nt(kr, kt)
    sb = _bdot_nt(kr, bt)
    a_kk = jnp.where(strict, sk[:, :c], 0.0)
    a_rk = jnp.where(incl, sk[:, c:], 0.0)
    l_mat = jnp.where(strict, sb[:, :c], 0.0)
    a_rb = jnp.where(incl, sb[:, c:], 0.0)

    eye = jnp.where(row == col, 1.0, 0.0)
    t_inv = eye - l_mat
    pw = _bdot_inv(l_mat, l_mat)
    for step in range(5):
        t_inv = t_inv + _bdot_inv(t_inv, pw)
        if step < 4:
            pw = _bdot_inv(pw, pw)

    q12 = _bdot(jnp.concatenate([a_kk, a_rk], axis=1).astype(BF16), v)
    q1, q2 = q12[:, :c], q12[:, c:]
    t_b = t_inv.astype(BF16)
    wkuv = _bdot(t_b, jnp.concatenate([kb, q1.astype(BF16)], axis=2)).astype(BF16)
    corr = _bdot(a_rb.astype(BF16), wkuv)
    rw = rb.astype(F32) - corr[:, :, :RWKV_HEAD_DIM]
    y0 = q2 - corr[:, :, RWKV_HEAD_DIM:]
    tn = _bdot_tn(wkuv, bh)
    m_mat = tn[:, :RWKV_HEAD_DIM]
    j_t = _bdot_tn(v, kh) - tn[:, RWKV_HEAD_DIM:]

    s_old = s_ref[...]
    s_hi = s_old.astype(BF16)
    s_lo = (s_old - s_hi.astype(F32)).astype(BF16)
    y_ref[...] = _bdot_nt(rw.astype(BF16), s_hi) + y0
    sm = _bdot(jnp.concatenate([s_hi, s_lo], axis=1), m_mat.astype(BF16))
    n = RWKV_HEAD_DIM
    s_ref[...] = s_old * gc_ref[:, 0:1, :] - (sm[:, :n] + sm[:, n:]) + j_t


def _wkv(v, rb, kb, kt, bt, kh, bh, gc):
    s = v.shape[1]
    nc = s // CHUNK
    scaled = (rb, kb, kt, bt, kh, bh)
    fwd = lambda rows: pl.BlockSpec((None, RWKV_HEADS, rows, 64), lambda c: (0, 0, c, 0))
    rev = lambda rows: pl.BlockSpec((None, RWKV_HEADS, rows, 64), lambda c: (1, 0, nc - 1 - c, 0))
    y_shape = jax.ShapeDtypeStruct((RWKV_HEADS, s, 64), F32)
    return pl.pallas_call(
        _wkv_kernel,
        grid=(nc,),
        in_specs=[
            pl.BlockSpec((RWKV_HEADS, CHUNK, 64), lambda c: (0, c, 0)),
            pl.BlockSpec((RWKV_HEADS, CHUNK, 64), lambda c: (0, nc - 1 - c, 0)),
            *[fwd(CHUNK) for _ in scaled], *[rev(CHUNK) for _ in scaled],
            fwd(8), rev(8),
        ],
        out_specs=[
            pl.BlockSpec((RWKV_HEADS, CHUNK, 64), lambda c: (0, c, 0)),
            pl.BlockSpec((RWKV_HEADS, CHUNK, 64), lambda c: (0, nc - 1 - c, 0)),
        ],
        out_shape=[y_shape, y_shape],
        scratch_shapes=[pltpu.VMEM((2, RWKV_HEADS, 64, 64), F32)],
        compiler_params=_cparams(("arbitrary",)),
        name="wkv7_chunked",
    )(v, v, *scaled, *scaled, gc, gc)


def _rwkv_post_kernel(yf_ref, yr_ref, g_ref, bonus_ref, lg_ref, lb_ref, o_ref):
    y = yf_ref[...] + yr_ref[...]
    mean = jnp.mean(y, axis=-1, keepdims=True)
    yc = y - mean
    var = jnp.mean(yc * yc, axis=-1, keepdims=True)
    yn = yc * lax.rsqrt(var + LNX_EPS)
    yt = jnp.concatenate([yn[h] for h in range(RWKV_HEADS)], axis=-1)
    out = (yt * lg_ref[...] + lb_ref[...] + bonus_ref[...].astype(F32)) * g_ref[...].astype(F32)
    o_ref[...] = out.astype(o_ref.dtype)


def _rwkv_post(y_fwd, y_rev, g, bonus, lnx_g, lnx_b, *, tm):
    s = g.shape[0]
    return pl.pallas_call(
        _rwkv_post_kernel,
        grid=(s // tm,),
        in_specs=[
            pl.BlockSpec((RWKV_HEADS, tm, 64), lambda i: (0, i, 0)),
            pl.BlockSpec((RWKV_HEADS, tm, 64), lambda i: (0, i, 0)),
            pl.BlockSpec((tm, RWKV_DIM), lambda i: (i, 0)),
            pl.BlockSpec((tm, RWKV_DIM), lambda i: (i, 0)),
            pl.BlockSpec((1, RWKV_DIM), lambda i: (0, 0)),
            pl.BlockSpec((1, RWKV_DIM), lambda i: (0, 0)),
        ],
        out_specs=pl.BlockSpec((tm, RWKV_DIM), lambda i: (i, 0)),
        out_shape=jax.ShapeDtypeStruct((s, RWKV_DIM), BF16),
        compiler_params=_cparams(("parallel",)),
        name="rwkv_post",
    )(y_fwd, y_rev, g, bonus, lnx_g.reshape(1, RWKV_DIM), lnx_b.reshape(1, RWKV_DIM))


def _mem_attn_kernel(q_ref, kv_ref, gq_ref, gk_ref, o_ref):
    for h in range(MEM_HEADS):
        sl = slice(MEM_HEAD_DIM * h, MEM_HEAD_DIM * (h + 1))
        q = q_ref[:, sl].astype(F32)
        q = q * lax.rsqrt(jnp.mean(q * q, axis=-1, keepdims=True) + NORM_EPS)
        q = q * gq_ref[...] * (MEM_HEAD_DIM ** -0.5)
        km = kv_ref[:, sl].astype(F32)
        km = km * lax.rsqrt(jnp.mean(km * km, axis=-1, keepdims=True) + NORM_EPS)
        km = km * gk_ref[...]
        vm = kv_ref[:, MEM_DIM + MEM_HEAD_DIM * h:MEM_DIM + MEM_HEAD_DIM * (h + 1)]
        s = lax.dot_general(q.astype(BF16), km.astype(BF16), (((1,), (1,)), ((), ())),
                            preferred_element_type=F32)
        s = s - jnp.max(s, axis=-1, keepdims=True)
        e = jnp.exp(s)
        pr = e / jnp.sum(e, axis=-1, keepdims=True)
        o_ref[:, sl] = _dot(pr.astype(BF16), vm).astype(o_ref.dtype)


def _mem_attn(p, kv, gq, gk, *, tm):
    s = p.shape[0]
    return pl.pallas_call(
        _mem_attn_kernel,
        grid=(s // tm,),
        in_specs=[
            pl.BlockSpec((tm, MEM_DIM), lambda i: (i, COL_MEM // MEM_DIM)),
            pl.BlockSpec((N_MEM, 2 * MEM_DIM), lambda i: (0, 0)),
            pl.BlockSpec((1, MEM_HEAD_DIM), lambda i: (0, 0)),
            pl.BlockSpec((1, MEM_HEAD_DIM), lambda i: (0, 0)),
        ],
        out_specs=pl.BlockSpec((tm, MEM_DIM), lambda i: (i, 0)),
        out_shape=jax.ShapeDtypeStruct((s, MEM_DIM), BF16),
        compiler_params=_cparams(("parallel",)),
        name="mem_attn",
    )(p, kv, gq, gk)


def _merge_kernel(o0_ref, o1_ref, o2_ref, g0_ref, g1_ref, g2_ref, w_ref, m_ref):
    def gate(g_ref):
        return 0.5 * jnp.tanh(0.5 * g_ref[...].astype(F32)) + 0.5

    acc = gate(g0_ref) * _dot(o0_ref[...], w_ref[0])
    acc = acc + gate(g1_ref) * _dot(o1_ref[...], w_ref[1])
    acc = acc + gate(g2_ref) * _dot(o2_ref[...], w_ref[2])
    m_ref[...] = acc.astype(m_ref.dtype)


def _merge(o_diff, o_rwkv, o_mem, p, w_branch, layer, *, tm, tn):
    s = p.shape[0]
    gate_blk = COL_GATE // tn
    per = D_MODEL // tn
    o_spec = lambda: pl.BlockSpec((tm, 1024), lambda i, j: (i, 0))
    g_spec = lambda b: pl.BlockSpec((tm, tn), lambda i, j: (i, gate_blk + b * per + j))
    return pl.pallas_call(
        _merge_kernel,
        grid=(s // tm, per),
        in_specs=[
            o_spec(), o_spec(), o_spec(),
            g_spec(0), g_spec(1), g_spec(2),
            pl.BlockSpec((None, N_BRANCH, 1024, tn), lambda i, j: (layer, 0, 0, j)),
        ],
        out_specs=pl.BlockSpec((tm, tn), lambda i, j: (i, j)),
        out_shape=jax.ShapeDtypeStruct((s, D_MODEL), BF16),
        compiler_params=_cparams(("parallel", "arbitrary")),
        name="merge",
    )(o_diff, o_rwkv, o_mem, p, p, p, w_branch)


GLU_TILE = 256


def _ffn_up_glu_kernel(x_ref, xp_ref, xn_ref, g_ref, wg_ref, wv_ref, cw_ref, cb_ref, o_ref,
                       h_ref, halo_ref):
    i = pl.program_id(0)
    nb = pl.num_programs(0)

    def norm(x):
        ms = jnp.mean(x * x, axis=-1, keepdims=True)
        return x * lax.rsqrt(ms + NORM_EPS) * g_ref[...]

    @pl.when(pl.program_id(1) == 0)
    def _():
        h_ref[...] = norm(x_ref[...]).astype(BF16)
        row = lax.broadcasted_iota(jnp.int32, xp_ref.shape, 0)
        before = jnp.where(i > 0, pltpu.roll(norm(xp_ref[...]), 1, 0), 0.0)
        after = jnp.where(i < nb - 1, pltpu.roll(norm(xn_ref[...]), 1, 0), 0.0)
        halo_ref[...] = jnp.where(row == 0, before, jnp.where(row == 1, after, 0.0)).astype(BF16)

    h = h_ref[...]
    halo = halo_ref[...]
    for c in range(o_ref.shape[1] // GLU_TILE):
        sl = slice(c * GLU_TILE, (c + 1) * GLU_TILE)
        wg = wg_ref[:, sl]
        ug = _dot(h, wg)
        uv = _dot(h, wv_ref[:, sl])
        edge = _dot(halo, wg)
        gp, gn = _shift_rows(ug, edge[0:1], edge[1:2])
        cw = cw_ref[:, sl]
        conv = cw[0:1] * gp + cw[1:2] * ug + cw[2:3] * gn + cb_ref[:, sl]
        half = 0.5 * conv
        o_ref[:, sl] = ((half + half * jnp.tanh(half)) * uv).astype(o_ref.dtype)


def _ffn_up_glu(x, g, w_up, layer, conv_w, conv_b, *, tm, tn):
    s, k = x.shape
    nj = D_FF // tn
    hb = tm // 16
    last16 = s // 16 - 1
    return pl.pallas_call(
        _ffn_up_glu_kernel,
        grid=(s // tm, nj),
        in_specs=[
            pl.BlockSpec((tm, k), lambda i, j: (i, 0)),
            pl.BlockSpec((16, k), lambda i, j: (jnp.maximum(i * hb - 1, 0), 0)),
            pl.BlockSpec((16, k), lambda i, j: (jnp.minimum((i + 1) * hb, last16), 0)),
            pl.BlockSpec((1, k), lambda i, j: (0, 0)),
            pl.BlockSpec((None, k, tn), lambda i, j: (layer, 0, j)),
            pl.BlockSpec((None, k, tn), lambda i, j: (layer, 0, nj + j)),
            pl.BlockSpec((3, tn), lambda i, j: (0, j)),
            pl.BlockSpec((1, tn), lambda i, j: (0, j)),
        ],
        out_specs=pl.BlockSpec((tm, tn), lambda i, j: (i, j)),
        out_shape=jax.ShapeDtypeStruct((s, D_FF), BF16),
        scratch_shapes=[pltpu.VMEM((tm, k), BF16), pltpu.VMEM((16, k), BF16)],
        compiler_params=_cparams(("parallel", "arbitrary")),
        name="ffn_up_glu",
    )(x, x, x, g.reshape(1, k), w_up, w_up, conv_w, conv_b.reshape(1, D_FF))


def _block_ones(n, group):
    idx = np.arange(n) // group
    return jnp.asarray(idx[:, None] == idx[None, :], BF16)


def _chunk_tri(n):
    idx = np.arange(n)
    same = (idx[:, None] // CHUNK) == (idx[None, :] // CHUNK)
    return jnp.asarray(same & (idx[:, None] >= idx[None, :]), BF16)


def _pad_rows(w, rows_before, total):
    n = w.shape[-1]
    out = jnp.zeros((total, n), w.dtype)
    return lax.dynamic_update_slice(out, w, (rows_before, 0))


def _pad_w_in(w):
    lora0 = COL_RKV + 3 * RWKV_DIM
    lora1 = lora0 + LORA_COLS
    pad = jnp.zeros((w.shape[0], LORA_PAD - LORA_COLS), w.dtype)
    return jnp.concatenate([w[:, :lora0], w[:, lora1:], w[:, lora0:lora1], pad], axis=1).astype(BF16)


def kernel(x, mem, attn_norm_g, w_in, diff_qk_g, diff_lambda, diff_subln_g, rwkv_mu, rwkv_w0,
           rwkv_w2, rwkv_a0, rwkv_a2, rwkv_g2, rwkv_k_k, rwkv_k_a, rwkv_r_k, rwkv_lnx_g,
           rwkv_lnx_b, mem_norm_g, w_mem_kv, mem_qk_g, w_branch, w_out, ffn_norm_g, w_ffn_up,
           ffn_conv_w, ffn_conv_b, w_ffn_down):
    b, s, d = x.shape
    assert b == 1 and d == D_MODEL and s % ATT_BQ == 0
    xs = x.reshape(s, d)
    mem2 = mem.reshape(N_MEM, d)
    prep_tm = 256
    bd64 = _block_ones(1024, 64)
    tri = _chunk_tri(prep_tm)
    n_rwkv_main = 3 * RWKV_DIM
    w_mem_kv_b = w_mem_kv.astype(BF16)
    w_branch_b = w_branch.astype(BF16)
    w_out_b = w_out.astype(BF16)
    w_ffn_up_b = w_ffn_up.astype(BF16)
    w_ffn_down_b = w_ffn_down.astype(BF16)

    for l in range(DEPTH):
        lam_init = 0.8 - 0.6 * math.exp(-0.3 * l)
        p = _rms_mm(xs, attn_norm_g[l], _pad_w_in(w_in[l])[None], 0, tm=512, tn=1536,
                    name="rms_w_in")

        gq = jnp.tile(diff_qk_g[l, 0].reshape(1, 128), (1, DIFF_HEADS))
        gk = jnp.tile(diff_qk_g[l, 1].reshape(1, 128), (1, DIFF_HEADS))
        qc, kc, score_bound = _attn_consts(diff_qk_g[l])
        qa, ka, vt = _diff_prep(p, gq, gk, qc, kc, bd64, tm=256)
        o_diff = lax.cond(
            score_bound <= SCORE_BOUND_MAX,
            functools.partial(_diff_attn, lam_init=lam_init, online=False),
            functools.partial(_diff_attn, lam_init=lam_init, online=True),
            qa, ka, vt, diff_lambda[l], diff_subln_g[l])

        mu = rwkv_mu[l]
        mu_rkv = mu[:, :n_rwkv_main]
        mu_lora = jnp.pad(mu[:, n_rwkv_main:], ((0, 0), (0, LORA_PAD - LORA_COLS)))
        w2p = jnp.stack([_pad_rows(rwkv_w2[l, dd], 64 * dd, 128) for dd in range(2)]).astype(BF16)
        a2p = jnp.stack([_pad_rows(rwkv_a2[l, dd], 64 * dd, 128) for dd in range(2)]).astype(BF16)
        g2p = _pad_rows(rwkv_g2[l], 0, 256).astype(BF16)
        (v_h, g_tok, bonus, rb, kb, kt, bt, kh, bh, gc) = _rwkv_prep(
            p, mu_rkv, mu_lora, rwkv_w0[l], w2p, rwkv_a0[l], a2p, g2p,
            rwkv_k_k[l].reshape(1, RWKV_DIM), rwkv_k_a[l].reshape(1, RWKV_DIM),
            rwkv_r_k[l].reshape(1, RWKV_DIM), bd64, tri, tm=prep_tm)
        y_fwd, y_rev = _wkv(v_h, rb, kb, kt, bt, kh, bh, gc)
        o_rwkv = _rwkv_post(y_fwd, y_rev, g_tok, bonus, rwkv_lnx_g[l], rwkv_lnx_b[l], tm=256)

        kv = _rms_mm(mem2, mem_norm_g[l], w_mem_kv_b, l, tm=N_MEM, tn=1024, name="rms_mem_kv")
        o_mem = _mem_attn(p, kv, mem_qk_g[l, 0].reshape(1, MEM_HEAD_DIM),
                          mem_qk_g[l, 1].reshape(1, MEM_HEAD_DIM), tm=512)

        merged = _merge(o_diff, o_rwkv, o_mem, p, w_branch_b, l, tm=512, tn=512)
        xs = _mm_res(merged, w_out_b, l, xs, tm=512, tn=D_MODEL, name="w_out_res")

        act = _ffn_up_glu(xs, ffn_norm_g[l], w_ffn_up_b, l, ffn_conv_w[l], ffn_conv_b[l],
                          tm=512, tn=512)
        xs = _mm_res(act, w_ffn_down_b, l, xs, tm=512, tn=512, name="ffn_down_res")

    return xs.reshape(b, s, d)
```

```python
import functools
import math

import jax
import jax.numpy as jnp
import numpy as np
from jax import lax
from jax.experimental import pallas as pl
from jax.experimental.pallas import tpu as pltpu

F32 = jnp.float32
BF16 = jnp.bfloat16

D_MODEL = 2048
DEPTH = 2
DIFF_HEADS = 8
DIFF_QK_DIM = 64
DIFF_V_DIM = 128
DIFF_DIM = 1024
RWKV_HEADS = 16
RWKV_HEAD_DIM = 64
RWKV_DIM = 1024
DECAY_LORA = 64
AAA_LORA = 64
GATE_LORA = 160
LORA_COLS = 2 * DECAY_LORA + 2 * AAA_LORA + GATE_LORA
LORA_PAD = 512
N_MEM = 256
MEM_HEADS = 4
MEM_HEAD_DIM = 256
MEM_DIM = 1024
N_BRANCH = 3
D_FF = 5632
NORM_EPS = 1e-6
LNX_EPS = 64e-5

COL_DIFF = 0
COL_RKV = 3 * DIFF_DIM
COL_GATE = COL_RKV + 3 * RWKV_DIM
COL_MEM = COL_GATE + N_BRANCH * D_MODEL
COL_LORA = COL_MEM + MEM_DIM
N_IN_PAD = COL_LORA + LORA_PAD

CHUNK = 64
LOG2E = 1.4426950408889634
N_POS_COLS = 12
SCORE_BOUND_MAX = 30.0
ATT_BQ = 512
ATT_BK = 512
V_AUG = 144
FAR_UNROLL = 6
VMEM_LIMIT = 56 * 1024 * 1024


def _cparams(sem):
    return pltpu.CompilerParams(dimension_semantics=sem, vmem_limit_bytes=VMEM_LIMIT)


def _split2(x):
    hi = x.astype(BF16)
    return hi, (x - hi.astype(F32)).astype(BF16)


def _dot(a, b):
    return jnp.dot(a, b, preferred_element_type=F32)


def _dot_exactish(a_bf16_exact, x_f32):
    hi, lo = _split2(x_f32)
    return _dot(a_bf16_exact, hi) + _dot(a_bf16_exact, lo)


def _rms_mm_kernel(x_ref, g_ref, w_ref, o_ref, h_ref):
    @pl.when(pl.program_id(1) == 0)
    def _():
        x = x_ref[...]
        ms = jnp.mean(x * x, axis=-1, keepdims=True)
        h_ref[...] = (x * lax.rsqrt(ms + NORM_EPS) * g_ref[...]).astype(BF16)

    o_ref[...] = _dot(h_ref[...], w_ref[...]).astype(o_ref.dtype)


def _rms_mm(x, g, w, layer, *, tm, tn, name):
    m, k = x.shape
    n = w.shape[2]
    return pl.pallas_call(
        _rms_mm_kernel,
        grid=(m // tm, n // tn),
        in_specs=[
            pl.BlockSpec((tm, k), lambda i, j: (i, 0)),
            pl.BlockSpec((1, k), lambda i, j: (0, 0)),
            pl.BlockSpec((None, k, tn), lambda i, j: (layer, 0, j)),
        ],
        out_specs=pl.BlockSpec((tm, tn), lambda i, j: (i, j)),
        out_shape=jax.ShapeDtypeStruct((m, n), BF16),
        scratch_shapes=[pltpu.VMEM((tm, k), BF16)],
        compiler_params=_cparams(("parallel", "arbitrary")),
        name=name,
    )(x, g.reshape(1, k), w)


def _mm_res_kernel(a_ref, w_ref, r_ref, o_ref):
    o_ref[...] = r_ref[...] + _dot(a_ref[...], w_ref[...])


def _mm_res(a, w, layer, res, *, tm, tn, name):
    m, k = a.shape
    n = w.shape[2]
    return pl.pallas_call(
        _mm_res_kernel,
        grid=(m // tm, n // tn),
        in_specs=[
            pl.BlockSpec((tm, k), lambda i, j: (i, 0)),
            pl.BlockSpec((None, k, tn), lambda i, j: (layer, 0, j)),
            pl.BlockSpec((tm, tn), lambda i, j: (i, j)),
        ],
        out_specs=pl.BlockSpec((tm, tn), lambda i, j: (i, j)),
        out_shape=jax.ShapeDtypeStruct((m, n), F32),
        compiler_params=_cparams(("parallel", "arbitrary")),
        name=name,
    )(a, w, res)


def _group_mean_sq(x, bd_ref, group):
    hi, lo = _split2(x * x)
    bd = bd_ref[...]
    return (_dot(hi, bd) + _dot(lo, bd)) * (1.0 / group)


def _aug_base(mp):
    return 64 if mp == 0 else 0


def _diff_prep_kernel(q_ref, k_ref, v_ref, gq_ref, gk_ref, qc_ref, kc_ref, bd_ref,
                      qa_ref, ka_ref, vt_ref):
    tm = q_ref.shape[0]
    row0 = pl.program_id(0) * tm
    lane = lax.broadcasted_iota(jnp.int32, (tm, 128), 1)
    pos = row0 + lax.broadcasted_iota(jnp.int32, (tm, 128), 0)
    pos_lo = (pos & 127).astype(F32)
    pos_hi = (pos >> 7).astype(F32)

    q = q_ref[...].astype(F32)
    qn = q * lax.rsqrt(_group_mean_sq(q, bd_ref, DIFF_QK_DIM) + NORM_EPS)
    qn = qn * gq_ref[...] * (DIFF_QK_DIM ** -0.5 * LOG2E)
    k = k_ref[...].astype(F32)
    kn = k * lax.rsqrt(_group_mean_sq(k, bd_ref, DIFF_QK_DIM) + NORM_EPS)
    kn = kn * gk_ref[...]

    for h in range(DIFF_HEADS):
        qh = qn[:, 128 * h:128 * (h + 1)]
        kh = kn[:, 128 * h:128 * (h + 1)]
        for mp in range(2):
            a0 = _aug_base(mp)
            g = 2 * h + mp
            is_data = (lane < 64) if mp == 0 else (lane >= 64)
            aug_q = jnp.where(lane < a0 + 3, pos_lo,
                              jnp.where(lane < a0 + 6, pos_hi, qc_ref[g:g + 1, :]))
            aug_k = jnp.where((lane >= a0 + 6) & (lane < a0 + 9), pos_lo,
                              jnp.where((lane >= a0 + 9) & (lane < a0 + 12), pos_hi,
                                        kc_ref[g:g + 1, :]))
            qa_ref[h, mp] = jnp.where(is_data, qh, aug_q).astype(BF16)
            ka_ref[h, mp] = jnp.where(is_data, kh, aug_k).astype(BF16)

    vt = v_ref[...].astype(F32).T
    sub = lax.broadcasted_iota(jnp.int32, (V_AUG - DIFF_V_DIM, tm), 0)
    ones_rows = jnp.where(sub == 0, 1.0, 0.0).astype(BF16)
    for h in range(DIFF_HEADS):
        vt_ref[h, 0:DIFF_V_DIM, :] = vt[128 * h:128 * (h + 1), :].astype(BF16)
        vt_ref[h, DIFF_V_DIM:V_AUG, :] = ones_rows


def _bf16_split3_const(x):
    parts = []
    for _ in range(3):
        part = float(np.asarray(x, np.float32).astype(BF16).astype(np.float32))
        parts.append(part)
        x = x - part
    return parts


def _attn_consts(qk_g):
    l_parts = _bf16_split3_const(LOG2E)
    m_nat = 8.0 * jnp.max(jnp.abs(qk_g[0]), axis=-1) * jnp.max(jnp.abs(qk_g[1]), axis=-1)
    qc = np.zeros((2 * DIFF_HEADS, 128), np.float32)
    kc = np.zeros((2 * DIFF_HEADS, 128), np.float32)
    bound_lane = np.zeros((2, 2 * DIFF_HEADS, 128), np.float32)
    for h in range(DIFF_HEADS):
        slope = 2.0 ** (-(h + 1))
        for mp in range(2):
            a0 = _aug_base(mp)
            g = 2 * h + mp
            for t, lp in enumerate(l_parts):
                qc[g, a0 + 6 + t] = slope * lp
                qc[g, a0 + 9 + t] = 128.0 * slope * lp
                kc[g, a0 + t] = -slope * lp
                kc[g, a0 + 3 + t] = -128.0 * slope * lp
            bound_lane[mp, g, a0 + N_POS_COLS] = 1.0
            kc[g, a0 + N_POS_COLS] = 1.0
    m2 = -m_nat * LOG2E
    qc = qc + m2[0] * bound_lane[0] + m2[1] * bound_lane[1]
    return qc, jnp.asarray(kc), jnp.max(m_nat)


def _diff_prep(p, gq, gk, qc, kc, bd, *, tm):
    s = p.shape[0]
    return pl.pallas_call(
        _diff_prep_kernel,
        grid=(s // tm,),
        in_specs=[
            pl.BlockSpec((tm, DIFF_DIM), lambda i: (i, 0)),
            pl.BlockSpec((tm, DIFF_DIM), lambda i: (i, 1)),
            pl.BlockSpec((tm, DIFF_DIM), lambda i: (i, 2)),
            pl.BlockSpec((1, DIFF_DIM), lambda i: (0, 0)),
            pl.BlockSpec((1, DIFF_DIM), lambda i: (0, 0)),
            pl.BlockSpec((2 * DIFF_HEADS, 128), lambda i: (0, 0)),
            pl.BlockSpec((2 * DIFF_HEADS, 128), lambda i: (0, 0)),
            pl.BlockSpec((DIFF_DIM, DIFF_DIM), lambda i: (0, 0)),
        ],
        out_specs=[
            pl.BlockSpec((DIFF_HEADS, 2, tm, 128), lambda i: (0, 0, i, 0)),
            pl.BlockSpec((DIFF_HEADS, 2, tm, 128), lambda i: (0, 0, i, 0)),
            pl.BlockSpec((DIFF_HEADS, V_AUG, tm), lambda i: (0, 0, i)),
        ],
        out_shape=[
            jax.ShapeDtypeStruct((DIFF_HEADS, 2, s, 128), BF16),
            jax.ShapeDtypeStruct((DIFF_HEADS, 2, s, 128), BF16),
            jax.ShapeDtypeStruct((DIFF_HEADS, V_AUG, s), BF16),
        ],
        compiler_params=_cparams(("parallel",)),
        name="diff_prep",
    )(p, p, p, gq, gk, qc, kc, bd)


def _diff_attn_kernel(lam_ref, sg_ref, dnear_ref, qa_ref, ka_ref, vt_ref, o_ref,
                      acc_ref, qv_ref, *mode_refs, lam_init, online):
    h = pl.program_id(0)
    i = pl.program_id(1)
    bq = qa_ref.shape[1]
    s_len = ka_ref.shape[1]
    bk = ATT_BK
    n_chunks = s_len // bk
    per_q = bq // bk
    n_far = n_chunks - per_q
    j_lo = i * per_q
    q0 = i * bq
    slope2 = jnp.exp2(-jnp.full((1, 1), h + 1, jnp.int32).astype(F32)) * LOG2E
    if online:
        m_ref, sa_ref, sb_ref, mxa_ref, mxb_ref = mode_refs
        m_ref[...] = jnp.full_like(m_ref, -1e30)
        bufs = ((sa_ref, mxa_ref), (sb_ref, mxb_ref))
    else:
        bufs = mode_refs
    acc_ref[...] = jnp.zeros_like(acc_ref)

    lane = lax.broadcasted_iota(jnp.int32, (1, 128), 1)
    for mp in range(2):
        q = qa_ref[mp]
        pos_lane = (lane >= _aug_base(mp)) & (lane < _aug_base(mp) + N_POS_COLS)
        qv_ref[mp, 0] = q
        qv_ref[mp, 1] = jnp.where(pos_lane, -q, q)
        qv_ref[mp, 2] = jnp.where(pos_lane, jnp.zeros_like(q), q)

    def stage_one(k0, variant, bias, buf):
        for mp in range(2):
            kc = ka_ref[mp, pl.ds(k0, bk), :]
            s = lax.dot_general(kc, qv_ref[mp, variant], (((1,), (1,)), ((), ())),
                                preferred_element_type=F32)
            if bias is not None:
                s = s + bias
            if online:
                buf[0][mp] = s
                buf[1][mp] = jnp.max(s, axis=0, keepdims=True)
            else:
                buf[mp] = jnp.exp2(s).astype(BF16)

    def stage_two(k0, buf):
        vt_c = vt_ref[:, pl.ds(k0, bk)]
        for mp in range(2):
            if online:
                m_old = m_ref[mp]
                m_new = jnp.maximum(m_old, buf[1][mp])
                p = jnp.exp2(buf[0][mp] - m_new).astype(BF16)
                acc_ref[mp] = acc_ref[mp] * jnp.exp2(m_old - m_new) + _dot(vt_c, p)
                m_ref[mp] = m_new
            else:
                acc_ref[mp] += _dot(vt_c, buf[mp])

    def near_start(t):
        return pl.multiple_of(q0 + t * bk, bk)

    def far_start(t):
        t = jnp.minimum(t, n_far - 1)
        j = jnp.where(t < j_lo, t, t + per_q)
        return pl.multiple_of(j * bk, bk), (t >= j_lo).astype(jnp.int32)

    def near_one(t):
        stage_one(near_start(t), 2, slope2 * dnear_ref[t], bufs[t % 2])

    def far_one(t, parity):
        k0, after = far_start(t)
        stage_one(k0, after, None, bufs[parity])

    def far_two(t, parity):
        stage_two(far_start(t)[0], bufs[parity])

    near_one(0)
    for t in range(1, per_q):
        near_one(t)
        stage_two(near_start(t - 1), bufs[(t - 1) % 2])
    far_one(0, per_q % 2)
    stage_two(near_start(per_q - 1), bufs[(per_q - 1) % 2])

    def far_body(u, c):
        for r in range(FAR_UNROLL):
            far_one(FAR_UNROLL * u + r + 1, (per_q + r + 1) % 2)
            far_two(FAR_UNROLL * u + r, (per_q + r) % 2)
        return c

    lax.fori_loop(0, n_far // FAR_UNROLL, far_body, 0)
    for t in range(n_far - n_far % FAR_UNROLL, n_far):
        if t + 1 < n_far:
            far_one(t + 1, (per_q + t + 1) % 2)
        far_two(t, (per_q + t) % 2)

    lp = lam_ref[...]
    lam = (jnp.exp(jnp.sum(lp[0:1] * lp[1:2], axis=-1, keepdims=True))
           - jnp.exp(jnp.sum(lp[2:3] * lp[3:4], axis=-1, keepdims=True)) + lam_init)
    a0 = acc_ref[0]
    a1 = acc_ref[1]
    o = (a0[0:DIFF_V_DIM] / a0[DIFF_V_DIM:DIFF_V_DIM + 1]
         - lam * (a1[0:DIFF_V_DIM] / a1[DIFF_V_DIM:DIFF_V_DIM + 1]))
    ms = jnp.mean(o * o, axis=0, keepdims=True)
    o = o * lax.rsqrt(ms + NORM_EPS) * sg_ref[...] * (1.0 - lam_init)
    o_ref[...] = o.T.astype(o_ref.dtype)


def _diff_attn(qa, ka, vt, lam_p, subln_g, *, lam_init, online):
    s = qa.shape[2]
    bq = ATT_BQ
    assert bq % ATT_BK == 0 and s > bq
    kern = functools.partial(_diff_attn_kernel, lam_init=lam_init, online=online)
    per_q = bq // ATT_BK
    key = np.arange(per_q * ATT_BK).reshape(per_q, ATT_BK, 1)
    dnear = jnp.asarray(-np.abs(np.arange(bq).reshape(1, 1, bq) - key), F32)
    scratch = [
        pltpu.VMEM((2, V_AUG, bq), F32),
        pltpu.VMEM((2, 3, bq, 128), BF16),
    ]
    if online:
        scratch += [
            pltpu.VMEM((2, 1, bq), F32),
            pltpu.VMEM((2, ATT_BK, bq), F32),
            pltpu.VMEM((2, ATT_BK, bq), F32),
            pltpu.VMEM((2, 1, bq), F32),
            pltpu.VMEM((2, 1, bq), F32),
        ]
    else:
        scratch += [pltpu.VMEM((2, ATT_BK, bq), BF16), pltpu.VMEM((2, ATT_BK, bq), BF16)]
    return pl.pallas_call(
        kern,
        grid=(DIFF_HEADS, s // bq),
        in_specs=[
            pl.BlockSpec((4, DIFF_QK_DIM), lambda h, i: (0, 0)),
            pl.BlockSpec((DIFF_V_DIM, 1), lambda h, i: (0, 0)),
            pl.BlockSpec((per_q, ATT_BK, bq), lambda h, i: (0, 0, 0)),
            pl.BlockSpec((None, 2, bq, 128), lambda h, i: (h, 0, i, 0)),
            pl.BlockSpec((None, 2, s, 128), lambda h, i: (h, 0, 0, 0)),
            pl.BlockSpec((None, V_AUG, s), lambda h, i: (h, 0, 0)),
        ],
        out_specs=pl.BlockSpec((bq, DIFF_V_DIM), lambda h, i: (i, h)),
        out_shape=jax.ShapeDtypeStruct((s, DIFF_DIM), BF16),
        scratch_shapes=scratch,
        compiler_params=_cparams(("parallel", "arbitrary")),
        name="diff_attn_online" if online else "diff_attn",
    )(lam_p, subln_g.reshape(DIFF_V_DIM, 1), dnear, qa, ka, vt)


def _shift_rows(x, prev_row, next_row):
    n = x.shape[0]
    row = lax.broadcasted_iota(jnp.int32, x.shape, 0)
    xp = jnp.where(row == 0, prev_row, pltpu.roll(x, 1, 0))
    xn = jnp.where(row == n - 1, next_row, pltpu.roll(x, n - 1, 0))
    return xp, xn


def _halo_rows(prev_ref, next_ref, i, n_blocks):
    hp = prev_ref.shape[0]
    prev_row = prev_ref[hp - 1:hp, :].astype(F32)
    next_row = next_ref[0:1, :].astype(F32)
    prev_row = jnp.where(i > 0, prev_row, 0.0)
    next_row = jnp.where(i < n_blocks - 1, next_row, 0.0)
    return prev_row, next_row


def _rwkv_prep_kernel(
        x_ref, xp_ref, xn_ref, l_ref, lp_ref, ln_ref,
        mu_ref, mul_ref, w0_ref, w2_ref, a0_ref, a2_ref, g2_ref, kk_ref, ka_ref, rk_ref,
        bd_ref, tri_ref,
        v_out, g_out, bonus_out, rb_out, kb_out, kt_out, bt_out, kh_out, bh_out, gc_out):
    i = pl.program_id(0)
    nb = pl.num_programs(0)
    tm = x_ref.shape[0]

    x = x_ref[...].astype(F32)
    prev_row, next_row = _halo_rows(xp_ref, xn_ref, i, nb)
    xp, xn = _shift_rows(x, prev_row, next_row)
    mu = mu_ref[...]
    x = x + mu[0:1] * (xp - x) + mu[1:2] * (xn - x)
    lo = l_ref[...].astype(F32)
    prev_row, next_row = _halo_rows(lp_ref, ln_ref, i, nb)
    lop, lon = _shift_rows(lo, prev_row, next_row)
    mul = mul_ref[...]
    lo = lo + mul[0:1] * (lop - lo) + mul[1:2] * (lon - lo)

    r = x[:, 0:RWKV_DIM]
    k = x[:, RWKV_DIM:2 * RWKV_DIM]
    v = x[:, 2 * RWKV_DIM:3 * RWKV_DIM]
    tw = jnp.tanh(lo[:, 0:128]).astype(BF16)
    la = lo[:, 128:256].astype(BF16)
    lg = jax.nn.sigmoid(lo[:, 256:512]).astype(BF16)

    g = _dot(lg, g2_ref[...])
    kk = k * kk_ref[...]
    ss = _group_mean_sq(kk, bd_ref, 1.0)
    kk = kk * lax.rsqrt(jnp.maximum(ss, 1e-24))
    hi, lo_part = _split2(r * k * rk_ref[...])
    bonus = (_dot(hi, bd_ref[...]) + _dot(lo_part, bd_ref[...])) * v

    g_out[...] = g.astype(g_out.dtype)
    bonus_out[...] = bonus.astype(bonus_out.dtype)
    for h in range(RWKV_HEADS):
        v_out[h] = v[:, 64 * h:64 * (h + 1)].astype(BF16)

    tri = tri_ref[...]
    for d in range(2):
        wl = w0_ref[d:d + 1, :] + _dot(tw, w2_ref[d])
        logdec = -math.exp(-0.5) * jax.nn.sigmoid(wl)
        a = jax.nn.sigmoid(a0_ref[d:d + 1, :] + _dot(la, a2_ref[d]))
        k_d = k * (1.0 + (a - 1.0) * ka_ref[...])
        b_d = kk * a
        pre = _dot_exactish(tri, logdec)
        chunk_tot = jnp.broadcast_to(
            pre.reshape(tm // CHUNK, CHUNK, RWKV_DIM)[:, CHUNK - 1:CHUNK, :],
            (tm // CHUNK, CHUNK, RWKV_DIM)).reshape(tm, RWKV_DIM)
        suf = chunk_tot - pre
        if d == 0:
            lc, ex, rem = pre, pre - logdec, suf
        else:
            lc, ex, rem = suf + logdec, suf, pre - logdec
        e_neg = jnp.exp(-lc)
        e_rem = jnp.exp(rem)
        outs = (
            (rb_out, r * jnp.exp(lc)),
            (kb_out, kk * jnp.exp(ex)),
            (kt_out, k_d * e_neg),
            (bt_out, b_d * e_neg),
            (kh_out, k_d * e_rem),
            (bh_out, b_d * e_rem),
        )
        for ref, val in outs:
            val = val.astype(BF16)
            for h in range(RWKV_HEADS):
                ref[d, h] = val[:, 64 * h:64 * (h + 1)]
        tot = jnp.exp(chunk_tot).reshape(tm // 8, 8, RWKV_DIM)[:, 0, :]
        for h in range(RWKV_HEADS):
            gc_out[d, h] = tot[:, 64 * h:64 * (h + 1)]


def _rwkv_prep(p, mu_rkv, mu_lora, w0, w2p, a0, a2p, g2p, k_k, k_a, r_k, bd, tri, *, tm):
    s = p.shape[0]
    nb = s // tm
    hb = tm // 16
    last16 = s // 16 - 1
    rkv_w = 3 * RWKV_DIM
    assert COL_RKV % rkv_w == 0 and COL_LORA % LORA_PAD == 0
    c_rkv = COL_RKV // rkv_w
    c_lora = COL_LORA // LORA_PAD

    def prev_map(c):
        return lambda i: (jnp.maximum(i * hb - 1, 0), c)

    def next_map(c):
        return lambda i: (jnp.minimum((i + 1) * hb, last16), c)

    full = lambda *shape: pl.BlockSpec(shape, lambda i: (0,) * len(shape))
    hm = lambda: pl.BlockSpec((2, RWKV_HEADS, tm, 64), lambda i: (0, 0, i, 0))
    hm_shape = jax.ShapeDtypeStruct((2, RWKV_HEADS, s, 64), BF16)
    return pl.pallas_call(
        _rwkv_prep_kernel,
        grid=(nb,),
        in_specs=[
            pl.BlockSpec((tm, rkv_w), lambda i: (i, c_rkv)),
            pl.BlockSpec((16, rkv_w), prev_map(c_rkv)),
            pl.BlockSpec((16, rkv_w), next_map(c_rkv)),
            pl.BlockSpec((tm, LORA_PAD), lambda i: (i, c_lora)),
            pl.BlockSpec((16, LORA_PAD), prev_map(c_lora)),
            pl.BlockSpec((16, LORA_PAD), next_map(c_lora)),
            full(2, rkv_w), full(2, LORA_PAD),
            full(2, RWKV_DIM), full(2, 128, RWKV_DIM),
            full(2, RWKV_DIM), full(2, 128, RWKV_DIM),
            full(256, RWKV_DIM),
            full(1, RWKV_DIM), full(1, RWKV_DIM), full(1, RWKV_DIM),
            full(RWKV_DIM, RWKV_DIM), full(tm, tm),
        ],
        out_specs=[
            pl.BlockSpec((RWKV_HEADS, tm, 64), lambda i: (0, i, 0)),
            pl.BlockSpec((tm, RWKV_DIM), lambda i: (i, 0)),
            pl.BlockSpec((tm, RWKV_DIM), lambda i: (i, 0)),
            hm(), hm(), hm(), hm(), hm(), hm(),
            pl.BlockSpec((2, RWKV_HEADS, tm // 8, 64), lambda i: (0, 0, i, 0)),
        ],
        out_shape=[
            jax.ShapeDtypeStruct((RWKV_HEADS, s, 64), BF16),
            jax.ShapeDtypeStruct((s, RWKV_DIM), BF16),
            jax.ShapeDtypeStruct((s, RWKV_DIM), BF16),
            hm_shape, hm_shape, hm_shape, hm_shape, hm_shape, hm_shape,
            jax.ShapeDtypeStruct((2, RWKV_HEADS, s // 8, 64), F32),
        ],
        compiler_params=_cparams(("parallel",)),
        name="rwkv_prep",
    )(p, p, p, p, p, p, mu_rkv, mu_lora, w0, w2p, a0, a2p, g2p, k_k, k_a, r_k, bd, tri)


def _bdot(a, b):
    return lax.dot_general(a, b, (((2,), (1,)), ((0,), (0,))), preferred_element_type=F32)


def _bdot_nt(a, b):
    return lax.dot_general(a, b, (((2,), (2,)), ((0,), (0,))), preferred_element_type=F32)


def _bdot_tn(a, b):
    return lax.dot_general(a, b, (((1,), (1,)), ((0,), (0,))), preferred_element_type=F32)


def _bdot_inv(a, b):
    return _bdot(a.astype(BF16), b.astype(BF16))


def _wkv_kernel(*refs):
    (vf_ref, vr_ref), ins, (gcf_ref, gcr_ref, yf_ref, yr_ref, s_ref) = refs[:2], refs[2:14], refs[14:]

    @pl.when(pl.program_id(0) == 0)
    def _():
        s_ref[...] = jnp.zeros_like(s_ref)

    _wkv_chunk(False, vf_ref, ins[0:6], gcf_ref, yf_ref, s_ref.at[0])
    _wkv_chunk(True, vr_ref, ins[6:12], gcr_ref, yr_ref, s_ref.at[1])


def _wkv_chunk(reverse, v_ref, scaled_refs, gc_ref, y_ref, s_ref):
    nh = RWKV_HEADS
    row = lax.broadcasted_iota(jnp.int32, (nh, CHUNK, CHUNK), 1)
    col = lax.broadcasted_iota(jnp.int32, (nh, CHUNK, CHUNK), 2)
    strict = (row < col) if reverse else (row > col)
    incl = (row <= col) if reverse else (row >= col)

    v = v_ref[...]
    rb, kb, kt, bt, kh, bh = [r[...] for r in scaled_refs]

    c = CHUNK
    kr = jnp.concatenate([kb, rb], axis=1)
    sk = _bdot_nt(kr, kt)
    sb = _bdot_nt(kr, bt)
    a_kk = jnp.where(strict, sk[:, :c], 0.0)
    a_rk = jnp.where(incl, sk[:, c:], 0.0)
    l_mat = jnp.where(strict, sb[:, :c], 0.0)
    a_rb = jnp.where(incl, sb[:, c:], 0.0)

    eye = jnp.where(row == col, 1.0, 0.0)
    t_inv = eye - l_mat
    pw = _bdot_inv(l_mat, l_mat)
    for step in range(5):
        t_inv = t_inv + _bdot_inv(t_inv, pw)
        if step < 4:
            pw = _bdot_inv(pw, pw)

    q12 = _bdot(jnp.concatenate([a_kk, a_rk], axis=1).astype(BF16), v)
    q1, q2 = q12[:, :c], q12[:, c:]
    t_b = t_inv.astype(BF16)
    wkuv = _bdot(t_b, jnp.concatenate([kb, q1.astype(BF16)], axis=2)).astype(BF16)
    corr = _bdot(a_rb.astype(BF16), wkuv)
    rw = rb.astype(F32) - corr[:, :, :RWKV_HEAD_DIM]
    y0 = q2 - corr[:, :, RWKV_HEAD_DIM:]
    tn = _bdot_tn(wkuv, bh)
    m_mat = tn[:, :RWKV_HEAD_DIM]
    j_t = _bdot_tn(v, kh) - tn[:, RWKV_HEAD_DIM:]

    s_old = s_ref[...]
    s_hi = s_old.astype(BF16)
    s_lo = (s_old - s_hi.astype(F32)).astype(BF16)
    y_ref[...] = _bdot_nt(rw.astype(BF16), s_hi) + y0
    sm = _bdot(jnp.concatenate([s_hi, s_lo], axis=1), m_mat.astype(BF16))
    n = RWKV_HEAD_DIM
    s_ref[...] = s_old * gc_ref[:, 0:1, :] - (sm[:, :n] + sm[:, n:]) + j_t


def _wkv(v, rb, kb, kt, bt, kh, bh, gc):
    s = v.shape[1]
    nc = s // CHUNK
    scaled = (rb, kb, kt, bt, kh, bh)
    fwd = lambda rows: pl.BlockSpec((None, RWKV_HEADS, rows, 64), lambda c: (0, 0, c, 0))
    rev = lambda rows: pl.BlockSpec((None, RWKV_HEADS, rows, 64), lambda c: (1, 0, nc - 1 - c, 0))
    y_shape = jax.ShapeDtypeStruct((RWKV_HEADS, s, 64), F32)
    return pl.pallas_call(
        _wkv_kernel,
        grid=(nc,),
        in_specs=[
            pl.BlockSpec((RWKV_HEADS, CHUNK, 64), lambda c: (0, c, 0)),
            pl.BlockSpec((RWKV_HEADS, CHUNK, 64), lambda c: (0, nc - 1 - c, 0)),
            *[fwd(CHUNK) for _ in scaled], *[rev(CHUNK) for _ in scaled],
            fwd(8), rev(8),
        ],
        out_specs=[
            pl.BlockSpec((RWKV_HEADS, CHUNK, 64), lambda c: (0, c, 0)),
            pl.BlockSpec((RWKV_HEADS, CHUNK, 64), lambda c: (0, nc - 1 - c, 0)),
        ],
        out_shape=[y_shape, y_shape],
        scratch_shapes=[pltpu.VMEM((2, RWKV_HEADS, 64, 64), F32)],
        compiler_params=_cparams(("arbitrary",)),
        name="wkv7_chunked",
    )(v, v, *scaled, *scaled, gc, gc)


def _rwkv_post_kernel(yf_ref, yr_ref, g_ref, bonus_ref, lg_ref, lb_ref, o_ref):
    y = yf_ref[...] + yr_ref[...]
    mean = jnp.mean(y, axis=-1, keepdims=True)
    yc = y - mean
    var = jnp.mean(yc * yc, axis=-1, keepdims=True)
    yn = yc * lax.rsqrt(var + LNX_EPS)
    yt = jnp.concatenate([yn[h] for h in range(RWKV_HEADS)], axis=-1)
    out = (yt * lg_ref[...] + lb_ref[...] + bonus_ref[...].astype(F32)) * g_ref[...].astype(F32)
    o_ref[...] = out.astype(o_ref.dtype)


def _rwkv_post(y_fwd, y_rev, g, bonus, lnx_g, lnx_b, *, tm):
    s = g.shape[0]
    return pl.pallas_call(
        _rwkv_post_kernel,
        grid=(s // tm,),
        in_specs=[
            pl.BlockSpec((RWKV_HEADS, tm, 64), lambda i: (0, i, 0)),
            pl.BlockSpec((RWKV_HEADS, tm, 64), lambda i: (0, i, 0)),
            pl.BlockSpec((tm, RWKV_DIM), lambda i: (i, 0)),
            pl.BlockSpec((tm, RWKV_DIM), lambda i: (i, 0)),
            pl.BlockSpec((1, RWKV_DIM), lambda i: (0, 0)),
            pl.BlockSpec((1, RWKV_DIM), lambda i: (0, 0)),
        ],
        out_specs=pl.BlockSpec((tm, RWKV_DIM), lambda i: (i, 0)),
        out_shape=jax.ShapeDtypeStruct((s, RWKV_DIM), BF16),
        compiler_params=_cparams(("parallel",)),
        name="rwkv_post",
    )(y_fwd, y_rev, g, bonus, lnx_g.reshape(1, RWKV_DIM), lnx_b.reshape(1, RWKV_DIM))


def _mem_attn_kernel(q_ref, kv_ref, gq_ref, gk_ref, o_ref):
    for h in range(MEM_HEADS):
        sl = slice(MEM_HEAD_DIM * h, MEM_HEAD_DIM * (h + 1))
        q = q_ref[:, sl].astype(F32)
        q = q * lax.rsqrt(jnp.mean(q * q, axis=-1, keepdims=True) + NORM_EPS)
        q = q * gq_ref[...] * (MEM_HEAD_DIM ** -0.5)
        km = kv_ref[:, sl].astype(F32)
        km = km * lax.rsqrt(jnp.mean(km * km, axis=-1, keepdims=True) + NORM_EPS)
        km = km * gk_ref[...]
        vm = kv_ref[:, MEM_DIM + MEM_HEAD_DIM * h:MEM_DIM + MEM_HEAD_DIM * (h + 1)]
        s = lax.dot_general(q.astype(BF16), km.astype(BF16), (((1,), (1,)), ((), ())),
                            preferred_element_type=F32)
        s = s - jnp.max(s, axis=-1, keepdims=True)
        e = jnp.exp(s)
        pr = e / jnp.sum(e, axis=-1, keepdims=True)
        o_ref[:, sl] = _dot(pr.astype(BF16), vm).astype(o_ref.dtype)


def _mem_attn(p, kv, gq, gk, *, tm):
    s = p.shape[0]
    assert COL_MEM % MEM_DIM == 0
    return pl.pallas_call(
        _mem_attn_kernel,
        grid=(s // tm,),
        in_specs=[
            pl.BlockSpec((tm, MEM_DIM), lambda i: (i, COL_MEM // MEM_DIM)),
            pl.BlockSpec((N_MEM, 2 * MEM_DIM), lambda i: (0, 0)),
            pl.BlockSpec((1, MEM_HEAD_DIM), lambda i: (0, 0)),
            pl.BlockSpec((1, MEM_HEAD_DIM), lambda i: (0, 0)),
        ],
        out_specs=pl.BlockSpec((tm, MEM_DIM), lambda i: (i, 0)),
        out_shape=jax.ShapeDtypeStruct((s, MEM_DIM), BF16),
        compiler_params=_cparams(("parallel",)),
        name="mem_attn",
    )(p, kv, gq, gk)


def _merge_kernel(o0_ref, o1_ref, o2_ref, g0_ref, g1_ref, g2_ref, w_ref, m_ref):
    def gate(g_ref):
        return 0.5 * jnp.tanh(0.5 * g_ref[...].astype(F32)) + 0.5

    acc = gate(g0_ref) * _dot(o0_ref[...], w_ref[0])
    acc = acc + gate(g1_ref) * _dot(o1_ref[...], w_ref[1])
    acc = acc + gate(g2_ref) * _dot(o2_ref[...], w_ref[2])
    m_ref[...] = acc.astype(m_ref.dtype)


def _merge(o_diff, o_rwkv, o_mem, p, w_branch, layer, *, tm, tn):
    s = p.shape[0]
    assert COL_GATE % tn == 0 and D_MODEL % tn == 0
    gate_blk = COL_GATE // tn
    per = D_MODEL // tn
    o_spec = lambda: pl.BlockSpec((tm, 1024), lambda i, j: (i, 0))
    g_spec = lambda b: pl.BlockSpec((tm, tn), lambda i, j: (i, gate_blk + b * per + j))
    return pl.pallas_call(
        _merge_kernel,
        grid=(s // tm, per),
        in_specs=[
            o_spec(), o_spec(), o_spec(),
            g_spec(0), g_spec(1), g_spec(2),
            pl.BlockSpec((None, N_BRANCH, 1024, tn), lambda i, j: (layer, 0, 0, j)),
        ],
        out_specs=pl.BlockSpec((tm, tn), lambda i, j: (i, j)),
        out_shape=jax.ShapeDtypeStruct((s, D_MODEL), BF16),
        compiler_params=_cparams(("parallel", "arbitrary")),
        name="merge",
    )(o_diff, o_rwkv, o_mem, p, p, p, w_branch)


GLU_TILE = 256


def _ffn_up_glu_kernel(x_ref, xp_ref, xn_ref, g_ref, wg_ref, wv_ref, cw_ref, cb_ref, o_ref,
                       h_ref, halo_ref):
    i = pl.program_id(0)
    nb = pl.num_programs(0)

    def norm(x):
        ms = jnp.mean(x * x, axis=-1, keepdims=True)
        return x * lax.rsqrt(ms + NORM_EPS) * g_ref[...]

    @pl.when(pl.program_id(1) == 0)
    def _():
        h_ref[...] = norm(x_ref[...]).astype(BF16)
        row = lax.broadcasted_iota(jnp.int32, xp_ref.shape, 0)
        before = jnp.where(i > 0, pltpu.roll(norm(xp_ref[...]), 1, 0), 0.0)
        after = jnp.where(i < nb - 1, pltpu.roll(norm(xn_ref[...]), 1, 0), 0.0)
        halo_ref[...] = jnp.where(row == 0, before, jnp.where(row == 1, after, 0.0)).astype(BF16)

    h = h_ref[...]
    halo = halo_ref[...]
    for c in range(o_ref.shape[1] // GLU_TILE):
        sl = slice(c * GLU_TILE, (c + 1) * GLU_TILE)
        wg = wg_ref[:, sl]
        ug = _dot(h, wg)
        uv = _dot(h, wv_ref[:, sl])
        edge = _dot(halo, wg)
        gp, gn = _shift_rows(ug, edge[0:1], edge[1:2])
        cw = cw_ref[:, sl]
        conv = cw[0:1] * gp + cw[1:2] * ug + cw[2:3] * gn + cb_ref[:, sl]
        half = 0.5 * conv
        o_ref[:, sl] = ((half + half * jnp.tanh(half)) * uv).astype(o_ref.dtype)


def _ffn_up_glu(x, g, w_up, layer, conv_w, conv_b, *, tm, tn):
    s, k = x.shape
    nj = D_FF // tn
    hb = tm // 16
    last16 = s // 16 - 1
    return pl.pallas_call(
        _ffn_up_glu_kernel,
        grid=(s // tm, nj),
        in_specs=[
            pl.BlockSpec((tm, k), lambda i, j: (i, 0)),
            pl.BlockSpec((16, k), lambda i, j: (jnp.maximum(i * hb - 1, 0), 0)),
            pl.BlockSpec((16, k), lambda i, j: (jnp.minimum((i + 1) * hb, last16), 0)),
            pl.BlockSpec((1, k), lambda i, j: (0, 0)),
            pl.BlockSpec((None, k, tn), lambda i, j: (layer, 0, j)),
            pl.BlockSpec((None, k, tn), lambda i, j: (layer, 0, nj + j)),
            pl.BlockSpec((3, tn), lambda i, j: (0, j)),
            pl.BlockSpec((1, tn), lambda i, j: (0, j)),
        ],
        out_specs=pl.BlockSpec((tm, tn), lambda i, j: (i, j)),
        out_shape=jax.ShapeDtypeStruct((s, D_FF), BF16),
        scratch_shapes=[pltpu.VMEM((tm, k), BF16), pltpu.VMEM((16, k), BF16)],
        compiler_params=_cparams(("parallel", "arbitrary")),
        name="ffn_up_glu",
    )(x, x, x, g.reshape(1, k), w_up, w_up, conv_w, conv_b.reshape(1, D_FF))


def _block_ones(n, group):
    idx = np.arange(n) // group
    return jnp.asarray(idx[:, None] == idx[None, :], BF16)


def _chunk_tri(n):
    idx = np.arange(n)
    same = (idx[:, None] // CHUNK) == (idx[None, :] // CHUNK)
    return jnp.asarray(same & (idx[:, None] >= idx[None, :]), BF16)


def _pad_rows(w, rows_before, total):
    n = w.shape[-1]
    out = jnp.zeros((total, n), w.dtype)
    return lax.dynamic_update_slice(out, w, (rows_before, 0))


def _pad_w_in(w):
    lora0 = COL_RKV + 3 * RWKV_DIM
    mem0 = lora0 + LORA_COLS
    gate0 = mem0 + MEM_DIM
    pad = jnp.zeros((w.shape[0], LORA_PAD - LORA_COLS), w.dtype)
    parts = [w[:, :lora0], w[:, gate0:], w[:, mem0:gate0], w[:, lora0:mem0], pad]
    return jnp.concatenate(parts, axis=1).astype(BF16)


def kernel(x, mem, attn_norm_g, w_in, diff_qk_g, diff_lambda, diff_subln_g, rwkv_mu, rwkv_w0,
           rwkv_w2, rwkv_a0, rwkv_a2, rwkv_g2, rwkv_k_k, rwkv_k_a, rwkv_r_k, rwkv_lnx_g,
           rwkv_lnx_b, mem_norm_g, w_mem_kv, mem_qk_g, w_branch, w_out, ffn_norm_g, w_ffn_up,
           ffn_conv_w, ffn_conv_b, w_ffn_down):
    b, s, d = x.shape
    assert b == 1 and d == D_MODEL and s % ATT_BQ == 0
    xs = x.reshape(s, d)
    mem2 = mem.reshape(N_MEM, d)
    prep_tm = 256
    bd64 = _block_ones(1024, 64)
    tri = _chunk_tri(prep_tm)
    n_rwkv_main = 3 * RWKV_DIM
    big_tm = 1024 if s % 1024 == 0 else 512
    w_mem_kv_b = w_mem_kv.astype(BF16)
    w_branch_b = w_branch.astype(BF16)
    w_out_b = w_out.astype(BF16)
    w_ffn_up_b = w_ffn_up.astype(BF16)
    w_ffn_down_b = w_ffn_down.astype(BF16)

    for l in range(DEPTH):
        lam_init = 0.8 - 0.6 * math.exp(-0.3 * l)
        p = _rms_mm(xs, attn_norm_g[l], _pad_w_in(w_in[l])[None], 0, tm=big_tm, tn=1536,
                    name="rms_w_in")

        gq = jnp.tile(diff_qk_g[l, 0].reshape(1, 128), (1, DIFF_HEADS))
        gk = jnp.tile(diff_qk_g[l, 1].reshape(1, 128), (1, DIFF_HEADS))
        qc, kc, score_bound = _attn_consts(diff_qk_g[l])
        qa, ka, vt = _diff_prep(p, gq, gk, qc, kc, bd64, tm=256)
        o_diff = lax.cond(
            score_bound <= SCORE_BOUND_MAX,
            functools.partial(_diff_attn, lam_init=lam_init, online=False),
            functools.partial(_diff_attn, lam_init=lam_init, online=True),
            qa, ka, vt, diff_lambda[l], diff_subln_g[l])

        mu = rwkv_mu[l]
        mu_rkv = mu[:, :n_rwkv_main]
        mu_lora = jnp.pad(mu[:, n_rwkv_main:], ((0, 0), (0, LORA_PAD - LORA_COLS)))
        w2p = jnp.stack([_pad_rows(rwkv_w2[l, dd], 64 * dd, 128) for dd in range(2)]).astype(BF16)
        a2p = jnp.stack([_pad_rows(rwkv_a2[l, dd], 64 * dd, 128) for dd in range(2)]).astype(BF16)
        g2p = _pad_rows(rwkv_g2[l], 0, 256).astype(BF16)
        (v_h, g_tok, bonus, rb, kb, kt, bt, kh, bh, gc) = _rwkv_prep(
            p, mu_rkv, mu_lora, rwkv_w0[l], w2p, rwkv_a0[l], a2p, g2p,
            rwkv_k_k[l].reshape(1, RWKV_DIM), rwkv_k_a[l].reshape(1, RWKV_DIM),
            rwkv_r_k[l].reshape(1, RWKV_DIM), bd64, tri, tm=prep_tm)
        y_fwd, y_rev = _wkv(v_h, rb, kb, kt, bt, kh, bh, gc)
        o_rwkv = _rwkv_post(y_fwd, y_rev, g_tok, bonus, rwkv_lnx_g[l], rwkv_lnx_b[l], tm=256)

        kv = _rms_mm(mem2, mem_norm_g[l], w_mem_kv_b, l, tm=N_MEM, tn=1024, name="rms_mem_kv")
        o_mem = _mem_attn(p, kv, mem_qk_g[l, 0].reshape(1, MEM_HEAD_DIM),
                          mem_qk_g[l, 1].reshape(1, MEM_HEAD_DIM), tm=512)

        merged = _merge(o_diff, o_rwkv, o_mem, p, w_branch_b, l, tm=256, tn=D_MODEL)
        xs = _mm_res(merged, w_out_b, l, xs, tm=512, tn=D_MODEL, name="w_out_res")

        act = _ffn_up_glu(xs, ffn_norm_g[l], w_ffn_up_b, l, ffn_conv_w[l], ffn_conv_b[l],
                          tm=512, tn=512)
        xs = _mm_res(act, w_ffn_down_b, l, xs, tm=big_tm, tn=512, name="ffn_down_res")

    return xs.reshape(b, s, d)
```

```python
import functools
import math

import jax
import jax.numpy as jnp
import numpy as np
from jax import lax
from jax.experimental import pallas as pl
from jax.experimental.pallas import tpu as pltpu

F32 = jnp.float32
BF16 = jnp.bfloat16

D_MODEL = 2048
DEPTH = 2
DIFF_HEADS = 8
DIFF_QK_DIM = 64
DIFF_V_DIM = 128
DIFF_DIM = 1024
RWKV_HEADS = 16
RWKV_HEAD_DIM = 64
RWKV_DIM = 1024
DECAY_LORA = 64
AAA_LORA = 64
GATE_LORA = 160
LORA_COLS = 2 * DECAY_LORA + 2 * AAA_LORA + GATE_LORA
LORA_PAD = 512
N_MEM = 256
MEM_HEADS = 4
MEM_HEAD_DIM = 256
MEM_DIM = 1024
N_BRANCH = 3
D_FF = 5632
NORM_EPS = 1e-6
LNX_EPS = 64e-5

COL_DIFF = 0
COL_RKV = 3 * DIFF_DIM
COL_GATE = COL_RKV + 3 * RWKV_DIM
COL_MEM = COL_GATE + N_BRANCH * D_MODEL
COL_LORA = COL_MEM + MEM_DIM
N_IN_PAD = COL_LORA + LORA_PAD

CHUNK = 64
LOG2E = 1.4426950408889634
N_POS_COLS = 12
SCORE_BOUND_MAX = 30.0
ATT_BQ = 512
ATT_BK = 512
V_AUG = 144
FAR_UNROLL = 6
VMEM_LIMIT = 56 * 1024 * 1024


def _cparams(sem):
    return pltpu.CompilerParams(dimension_semantics=sem, vmem_limit_bytes=VMEM_LIMIT)


def _split2(x):
    hi = x.astype(BF16)
    return hi, (x - hi.astype(F32)).astype(BF16)


def _dot(a, b):
    return jnp.dot(a, b, preferred_element_type=F32)


def _dot_exactish(a_bf16_exact, x_f32):
    hi, lo = _split2(x_f32)
    return _dot(a_bf16_exact, hi) + _dot(a_bf16_exact, lo)


def _rms_mm_kernel(x_ref, g_ref, w_ref, o_ref, h_ref):
    @pl.when(pl.program_id(1) == 0)
    def _():
        x = x_ref[...]
        ms = jnp.mean(x * x, axis=-1, keepdims=True)
        h_ref[...] = (x * lax.rsqrt(ms + NORM_EPS) * g_ref[...]).astype(BF16)

    o_ref[...] = _dot(h_ref[...], w_ref[...]).astype(o_ref.dtype)


def _rms_mm(x, g, w, layer, *, tm, tn, name):
    m, k = x.shape
    n = w.shape[2]
    return pl.pallas_call(
        _rms_mm_kernel,
        grid=(m // tm, n // tn),
        in_specs=[
            pl.BlockSpec((tm, k), lambda i, j: (i, 0)),
            pl.BlockSpec((1, k), lambda i, j: (0, 0)),
            pl.BlockSpec((None, k, tn), lambda i, j: (layer, 0, j)),
        ],
        out_specs=pl.BlockSpec((tm, tn), lambda i, j: (i, j)),
        out_shape=jax.ShapeDtypeStruct((m, n), BF16),
        scratch_shapes=[pltpu.VMEM((tm, k), BF16)],
        compiler_params=_cparams(("parallel", "arbitrary")),
        name=name,
    )(x, g.reshape(1, k), w)


def _mm_res_kernel(a_ref, w_ref, r_ref, o_ref):
    o_ref[...] = r_ref[...] + _dot(a_ref[...], w_ref[...])


def _mm_res(a, w, layer, res, *, tm, tn, name):
    m, k = a.shape
    n = w.shape[2]
    return pl.pallas_call(
        _mm_res_kernel,
        grid=(m // tm, n // tn),
        in_specs=[
            pl.BlockSpec((tm, k), lambda i, j: (i, 0)),
            pl.BlockSpec((None, k, tn), lambda i, j: (layer, 0, j)),
            pl.BlockSpec((tm, tn), lambda i, j: (i, j)),
        ],
        out_specs=pl.BlockSpec((tm, tn), lambda i, j: (i, j)),
        out_shape=jax.ShapeDtypeStruct((m, n), F32),
        compiler_params=_cparams(("parallel", "arbitrary")),
        name=name,
    )(a, w, res)


def _group_mean_sq(x, bd_ref, group):
    return _dot((x * x).astype(BF16), bd_ref[...]) * (1.0 / group)


def _aug_base(mp):
    return 64 if mp == 0 else 0


def _diff_prep_kernel(q_ref, k_ref, v_ref, gq_ref, gk_ref, qc_ref, kc_ref, bd_ref,
                      qa_ref, ka_ref, vt_ref):
    tm = q_ref.shape[0]
    row0 = pl.program_id(0) * tm
    lane = lax.broadcasted_iota(jnp.int32, (tm, 128), 1)
    pos = row0 + lax.broadcasted_iota(jnp.int32, (tm, 128), 0)
    pos_lo = (pos & 127).astype(F32)
    pos_hi = (pos >> 7).astype(F32)

    q = q_ref[...].astype(F32)
    qn = q * lax.rsqrt(_group_mean_sq(q, bd_ref, DIFF_QK_DIM) + NORM_EPS)
    qn = qn * gq_ref[...] * (DIFF_QK_DIM ** -0.5 * LOG2E)
    k = k_ref[...].astype(F32)
    kn = k * lax.rsqrt(_group_mean_sq(k, bd_ref, DIFF_QK_DIM) + NORM_EPS)
    kn = kn * gk_ref[...]

    for h in range(DIFF_HEADS):
        qh = qn[:, 128 * h:128 * (h + 1)]
        kh = kn[:, 128 * h:128 * (h + 1)]
        for mp in range(2):
            a0 = _aug_base(mp)
            g = 2 * h + mp
            is_data = (lane < 64) if mp == 0 else (lane >= 64)
            aug_q = jnp.where(lane < a0 + 3, pos_lo,
                              jnp.where(lane < a0 + 6, pos_hi, qc_ref[g:g + 1, :]))
            aug_k = jnp.where((lane >= a0 + 6) & (lane < a0 + 9), pos_lo,
                              jnp.where((lane >= a0 + 9) & (lane < a0 + 12), pos_hi,
                                        kc_ref[g:g + 1, :]))
            qa_ref[h, mp] = jnp.where(is_data, qh, aug_q).astype(BF16)
            ka_ref[h, mp] = jnp.where(is_data, kh, aug_k).astype(BF16)

    vt = v_ref[...].astype(F32).T
    sub = lax.broadcasted_iota(jnp.int32, (V_AUG - DIFF_V_DIM, tm), 0)
    ones_rows = jnp.where(sub == 0, 1.0, 0.0).astype(BF16)
    for h in range(DIFF_HEADS):
        vt_ref[h, 0:DIFF_V_DIM, :] = vt[128 * h:128 * (h + 1), :].astype(BF16)
        vt_ref[h, DIFF_V_DIM:V_AUG, :] = ones_rows


def _bf16_split3_const(x):
    parts = []
    for _ in range(3):
        part = float(np.asarray(x, np.float32).astype(BF16).astype(np.float32))
        parts.append(part)
        x = x - part
    return parts


def _attn_consts(qk_g):
    l_parts = _bf16_split3_const(LOG2E)
    m_nat = 8.0 * jnp.max(jnp.abs(qk_g[0]), axis=-1) * jnp.max(jnp.abs(qk_g[1]), axis=-1)
    qc = np.zeros((2 * DIFF_HEADS, 128), np.float32)
    kc = np.zeros((2 * DIFF_HEADS, 128), np.float32)
    bound_lane = np.zeros((2, 2 * DIFF_HEADS, 128), np.float32)
    for h in range(DIFF_HEADS):
        slope = 2.0 ** (-(h + 1))
        for mp in range(2):
            a0 = _aug_base(mp)
            g = 2 * h + mp
            for t, lp in enumerate(l_parts):
                qc[g, a0 + 6 + t] = slope * lp
                qc[g, a0 + 9 + t] = 128.0 * slope * lp
                kc[g, a0 + t] = -slope * lp
                kc[g, a0 + 3 + t] = -128.0 * slope * lp
            bound_lane[mp, g, a0 + N_POS_COLS] = 1.0
            kc[g, a0 + N_POS_COLS] = 1.0
    m2 = -m_nat * LOG2E
    qc = qc + m2[0] * bound_lane[0] + m2[1] * bound_lane[1]
    return qc, jnp.asarray(kc), jnp.max(m_nat)


def _diff_prep(p, gq, gk, qc, kc, bd, *, tm):
    s = p.shape[0]
    return pl.pallas_call(
        _diff_prep_kernel,
        grid=(s // tm,),
        in_specs=[
            pl.BlockSpec((tm, DIFF_DIM), lambda i: (i, 0)),
            pl.BlockSpec((tm, DIFF_DIM), lambda i: (i, 1)),
            pl.BlockSpec((tm, DIFF_DIM), lambda i: (i, 2)),
            pl.BlockSpec((1, DIFF_DIM), lambda i: (0, 0)),
            pl.BlockSpec((1, DIFF_DIM), lambda i: (0, 0)),
            pl.BlockSpec((2 * DIFF_HEADS, 128), lambda i: (0, 0)),
            pl.BlockSpec((2 * DIFF_HEADS, 128), lambda i: (0, 0)),
            pl.BlockSpec((DIFF_DIM, DIFF_DIM), lambda i: (0, 0)),
        ],
        out_specs=[
            pl.BlockSpec((DIFF_HEADS, 2, tm, 128), lambda i: (0, 0, i, 0)),
            pl.BlockSpec((DIFF_HEADS, 2, tm, 128), lambda i: (0, 0, i, 0)),
            pl.BlockSpec((DIFF_HEADS, V_AUG, tm), lambda i: (0, 0, i)),
        ],
        out_shape=[
            jax.ShapeDtypeStruct((DIFF_HEADS, 2, s, 128), BF16),
            jax.ShapeDtypeStruct((DIFF_HEADS, 2, s, 128), BF16),
            jax.ShapeDtypeStruct((DIFF_HEADS, V_AUG, s), BF16),
        ],
        compiler_params=_cparams(("parallel",)),
        name="diff_prep",
    )(p, p, p, gq, gk, qc, kc, bd)


def _diff_attn_kernel(lam_ref, sg_ref, dnear_ref, qa_ref, ka_ref, vt_ref, o_ref,
                      acc_ref, qv_ref, *mode_refs, lam_init, online):
    h = pl.program_id(0)
    i = pl.program_id(1)
    bq = qa_ref.shape[1]
    s_len = ka_ref.shape[1]
    bk = ATT_BK
    n_chunks = s_len // bk
    per_q = bq // bk
    n_far = n_chunks - per_q
    j_lo = i * per_q
    q0 = i * bq
    slope2 = jnp.exp2(-jnp.full((1, 1), h + 1, jnp.int32).astype(F32)) * LOG2E
    if online:
        m_ref, sa_ref, sb_ref, mxa_ref, mxb_ref = mode_refs
        m_ref[...] = jnp.full_like(m_ref, -1e30)
        bufs = ((sa_ref, mxa_ref), (sb_ref, mxb_ref))
    else:
        bufs = mode_refs
    acc_ref[...] = jnp.zeros_like(acc_ref)

    lane = lax.broadcasted_iota(jnp.int32, (1, 128), 1)
    for mp in range(2):
        q = qa_ref[mp]
        pos_lane = (lane >= _aug_base(mp)) & (lane < _aug_base(mp) + N_POS_COLS)
        qv_ref[mp, 0] = q
        qv_ref[mp, 1] = jnp.where(pos_lane, -q, q)
        qv_ref[mp, 2] = jnp.where(pos_lane, jnp.zeros_like(q), q)

    def stage_one(k0, variant, bias, buf):
        for mp in range(2):
            kc = ka_ref[mp, pl.ds(k0, bk), :]
            s = lax.dot_general(kc, qv_ref[mp, variant], (((1,), (1,)), ((), ())),
                                preferred_element_type=F32)
            if bias is not None:
                s = s + bias
            if online:
                buf[0][mp] = s
                buf[1][mp] = jnp.max(s, axis=0, keepdims=True)
            else:
                buf[mp] = jnp.exp2(s).astype(BF16)

    def stage_two(k0, buf):
        vt_c = vt_ref[:, pl.ds(k0, bk)]
        for mp in range(2):
            if online:
                m_old = m_ref[mp]
                m_new = jnp.maximum(m_old, buf[1][mp])
                p = jnp.exp2(buf[0][mp] - m_new).astype(BF16)
                acc_ref[mp] = acc_ref[mp] * jnp.exp2(m_old - m_new) + _dot(vt_c, p)
                m_ref[mp] = m_new
            else:
                acc_ref[mp] += _dot(vt_c, buf[mp])

    def near_start(t):
        return pl.multiple_of(q0 + t * bk, bk)

    def far_start(t):
        t = jnp.minimum(t, n_far - 1)
        j = jnp.where(t < j_lo, t, t + per_q)
        return pl.multiple_of(j * bk, bk), (t >= j_lo).astype(jnp.int32)

    def near_one(t):
        stage_one(near_start(t), 2, slope2 * dnear_ref[t], bufs[t % 2])

    def far_one(t, parity):
        k0, after = far_start(t)
        stage_one(k0, after, None, bufs[parity])

    def far_two(t, parity):
        stage_two(far_start(t)[0], bufs[parity])

    near_one(0)
    for t in range(1, per_q):
        near_one(t)
        stage_two(near_start(t - 1), bufs[(t - 1) % 2])
    far_one(0, per_q % 2)
    stage_two(near_start(per_q - 1), bufs[(per_q - 1) % 2])

    def far_body(u, c):
        for r in range(FAR_UNROLL):
            far_one(FAR_UNROLL * u + r + 1, (per_q + r + 1) % 2)
            far_two(FAR_UNROLL * u + r, (per_q + r) % 2)
        return c

    lax.fori_loop(0, n_far // FAR_UNROLL, far_body, 0)
    for t in range(n_far - n_far % FAR_UNROLL, n_far):
        if t + 1 < n_far:
            far_one(t + 1, (per_q + t + 1) % 2)
        far_two(t, (per_q + t) % 2)

    lp = lam_ref[...]
    lam = (jnp.exp(jnp.sum(lp[0:1] * lp[1:2], axis=-1, keepdims=True))
           - jnp.exp(jnp.sum(lp[2:3] * lp[3:4], axis=-1, keepdims=True)) + lam_init)
    a0 = acc_ref[0]
    a1 = acc_ref[1]
    o = (a0[0:DIFF_V_DIM] / a0[DIFF_V_DIM:DIFF_V_DIM + 1]
         - lam * (a1[0:DIFF_V_DIM] / a1[DIFF_V_DIM:DIFF_V_DIM + 1]))
    ms = jnp.mean(o * o, axis=0, keepdims=True)
    o = o * lax.rsqrt(ms + NORM_EPS) * sg_ref[...] * (1.0 - lam_init)
    o_ref[...] = o.T.astype(o_ref.dtype)


def _diff_attn(qa, ka, vt, lam_p, subln_g, *, lam_init, online):
    s = qa.shape[2]
    bq = ATT_BQ
    assert bq % ATT_BK == 0 and s > bq
    kern = functools.partial(_diff_attn_kernel, lam_init=lam_init, online=online)
    per_q = bq // ATT_BK
    key = np.arange(per_q * ATT_BK).reshape(per_q, ATT_BK, 1)
    dnear = jnp.asarray(-np.abs(np.arange(bq).reshape(1, 1, bq) - key), F32)
    scratch = [
        pltpu.VMEM((2, V_AUG, bq), F32),
        pltpu.VMEM((2, 3, bq, 128), BF16),
    ]
    if online:
        scratch += [
            pltpu.VMEM((2, 1, bq), F32),
            pltpu.VMEM((2, ATT_BK, bq), F32),
            pltpu.VMEM((2, ATT_BK, bq), F32),
            pltpu.VMEM((2, 1, bq), F32),
            pltpu.VMEM((2, 1, bq), F32),
        ]
    else:
        scratch += [pltpu.VMEM((2, ATT_BK, bq), BF16), pltpu.VMEM((2, ATT_BK, bq), BF16)]
    return pl.pallas_call(
        kern,
        grid=(DIFF_HEADS, s // bq),
        in_specs=[
            pl.BlockSpec((4, DIFF_QK_DIM), lambda h, i: (0, 0)),
            pl.BlockSpec((DIFF_V_DIM, 1), lambda h, i: (0, 0)),
            pl.BlockSpec((per_q, ATT_BK, bq), lambda h, i: (0, 0, 0)),
            pl.BlockSpec((None, 2, bq, 128), lambda h, i: (h, 0, i, 0)),
            pl.BlockSpec((None, 2, s, 128), lambda h, i: (h, 0, 0, 0)),
            pl.BlockSpec((None, V_AUG, s), lambda h, i: (h, 0, 0)),
        ],
        out_specs=pl.BlockSpec((bq, DIFF_V_DIM), lambda h, i: (i, h)),
        out_shape=jax.ShapeDtypeStruct((s, DIFF_DIM), BF16),
        scratch_shapes=scratch,
        compiler_params=_cparams(("parallel", "arbitrary")),
        name="diff_attn_online" if online else "diff_attn",
    )(lam_p, subln_g.reshape(DIFF_V_DIM, 1), dnear, qa, ka, vt)


def _shift_rows(x, prev_row, next_row):
    n = x.shape[0]
    row = lax.broadcasted_iota(jnp.int32, x.shape, 0)
    xp = jnp.where(row == 0, prev_row, pltpu.roll(x, 1, 0))
    xn = jnp.where(row == n - 1, next_row, pltpu.roll(x, n - 1, 0))
    return xp, xn


def _halo_rows(prev_ref, next_ref, i, n_blocks):
    hp = prev_ref.shape[0]
    prev_row = prev_ref[hp - 1:hp, :].astype(F32)
    next_row = next_ref[0:1, :].astype(F32)
    prev_row = jnp.where(i > 0, prev_row, 0.0)
    next_row = jnp.where(i < n_blocks - 1, next_row, 0.0)
    return prev_row, next_row


def _rwkv_prep_kernel(
        x_ref, xp_ref, xn_ref, l_ref, lp_ref, ln_ref,
        mu_ref, mul_ref, w0_ref, w2_ref, a0_ref, a2_ref, g2_ref, kk_ref, ka_ref, rk_ref,
        bd_ref, tri_ref,
        v_out, g_out, bonus_out, rb_out, kb_out, kt_out, bt_out, kh_out, bh_out, gc_out):
    i = pl.program_id(0)
    nb = pl.num_programs(0)
    tm = x_ref.shape[0]

    x = x_ref[...].astype(F32)
    prev_row, next_row = _halo_rows(xp_ref, xn_ref, i, nb)
    xp, xn = _shift_rows(x, prev_row, next_row)
    mu = mu_ref[...]
    x = x + mu[0:1] * (xp - x) + mu[1:2] * (xn - x)
    lo = l_ref[...].astype(F32)
    prev_row, next_row = _halo_rows(lp_ref, ln_ref, i, nb)
    lop, lon = _shift_rows(lo, prev_row, next_row)
    mul = mul_ref[...]
    lo = lo + mul[0:1] * (lop - lo) + mul[1:2] * (lon - lo)

    r = x[:, 0:RWKV_DIM]
    k = x[:, RWKV_DIM:2 * RWKV_DIM]
    v = x[:, 2 * RWKV_DIM:3 * RWKV_DIM]
    tw = jnp.tanh(lo[:, 0:128]).astype(BF16)
    la = lo[:, 128:256].astype(BF16)
    lg = jax.nn.sigmoid(lo[:, 256:512]).astype(BF16)

    g = _dot(lg, g2_ref[...])
    kk = k * kk_ref[...]
    ss = _group_mean_sq(kk, bd_ref, 1.0)
    kk = kk * lax.rsqrt(jnp.maximum(ss, 1e-24))
    bonus = _dot((r * k * rk_ref[...]).astype(BF16), bd_ref[...]) * v

    g_out[...] = g.astype(g_out.dtype)
    bonus_out[...] = bonus.astype(bonus_out.dtype)
    for h in range(RWKV_HEADS):
        v_out[h] = v[:, 64 * h:64 * (h + 1)].astype(BF16)

    tri = tri_ref[...]
    for d in range(2):
        wl = w0_ref[d:d + 1, :] + _dot(tw, w2_ref[d])
        logdec = -math.exp(-0.5) * jax.nn.sigmoid(wl)
        a = jax.nn.sigmoid(a0_ref[d:d + 1, :] + _dot(la, a2_ref[d]))
        k_d = k * (1.0 + (a - 1.0) * ka_ref[...])
        b_d = kk * a
        pre = _dot_exactish(tri, logdec)
        chunk_tot = jnp.broadcast_to(
            pre.reshape(tm // CHUNK, CHUNK, RWKV_DIM)[:, CHUNK - 1:CHUNK, :],
            (tm // CHUNK, CHUNK, RWKV_DIM)).reshape(tm, RWKV_DIM)
        suf = chunk_tot - pre
        if d == 0:
            lc, ex, rem = pre, pre - logdec, suf
        else:
            lc, ex, rem = suf + logdec, suf, pre - logdec
        e_neg = jnp.exp(-lc)
        e_rem = jnp.exp(rem)
        outs = (
            (rb_out, r * jnp.exp(lc)),
            (kb_out, kk * jnp.exp(ex)),
            (kt_out, k_d * e_neg),
            (bt_out, b_d * e_neg),
            (kh_out, k_d * e_rem),
            (bh_out, b_d * e_rem),
        )
        for ref, val in outs:
            val = val.astype(BF16)
            for h in range(RWKV_HEADS):
                ref[d, h] = val[:, 64 * h:64 * (h + 1)]
        tot = jnp.exp(chunk_tot).reshape(tm // 8, 8, RWKV_DIM)[:, 0, :]
        for h in range(RWKV_HEADS):
            gc_out[d, h] = tot[:, 64 * h:64 * (h + 1)]


def _rwkv_prep(p, mu_rkv, mu_lora, w0, w2p, a0, a2p, g2p, k_k, k_a, r_k, bd, tri, *, tm):
    s = p.shape[0]
    nb = s // tm
    hb = tm // 16
    last16 = s // 16 - 1
    rkv_w = 3 * RWKV_DIM
    assert COL_RKV % rkv_w == 0 and COL_LORA % LORA_PAD == 0
    c_rkv = COL_RKV // rkv_w
    c_lora = COL_LORA // LORA_PAD

    def prev_map(c):
        return lambda i: (jnp.maximum(i * hb - 1, 0), c)

    def next_map(c):
        return lambda i: (jnp.minimum((i + 1) * hb, last16), c)

    full = lambda *shape: pl.BlockSpec(shape, lambda i: (0,) * len(shape))
    hm = lambda: pl.BlockSpec((2, RWKV_HEADS, tm, 64), lambda i: (0, 0, i, 0))
    hm_shape = jax.ShapeDtypeStruct((2, RWKV_HEADS, s, 64), BF16)
    return pl.pallas_call(
        _rwkv_prep_kernel,
        grid=(nb,),
        in_specs=[
            pl.BlockSpec((tm, rkv_w), lambda i: (i, c_rkv)),
            pl.BlockSpec((16, rkv_w), prev_map(c_rkv)),
            pl.BlockSpec((16, rkv_w), next_map(c_rkv)),
            pl.BlockSpec((tm, LORA_PAD), lambda i: (i, c_lora)),
            pl.BlockSpec((16, LORA_PAD), prev_map(c_lora)),
            pl.BlockSpec((16, LORA_PAD), next_map(c_lora)),
            full(2, rkv_w), full(2, LORA_PAD),
            full(2, RWKV_DIM), full(2, 128, RWKV_DIM),
            full(2, RWKV_DIM), full(2, 128, RWKV_DIM),
            full(256, RWKV_DIM),
            full(1, RWKV_DIM), full(1, RWKV_DIM), full(1, RWKV_DIM),
            full(RWKV_DIM, RWKV_DIM), full(tm, tm),
        ],
        out_specs=[
            pl.BlockSpec((RWKV_HEADS, tm, 64), lambda i: (0, i, 0)),
            pl.BlockSpec((tm, RWKV_DIM), lambda i: (i, 0)),
            pl.BlockSpec((tm, RWKV_DIM), lambda i: (i, 0)),
            hm(), hm(), hm(), hm(), hm(), hm(),
            pl.BlockSpec((2, RWKV_HEADS, tm // 8, 64), lambda i: (0, 0, i, 0)),
        ],
        out_shape=[
            jax.ShapeDtypeStruct((RWKV_HEADS, s, 64), BF16),
            jax.ShapeDtypeStruct((s, RWKV_DIM), BF16),
            jax.ShapeDtypeStruct((s, RWKV_DIM), BF16),
            hm_shape, hm_shape, hm_shape, hm_shape, hm_shape, hm_shape,
            jax.ShapeDtypeStruct((2, RWKV_HEADS, s // 8, 64), F32),
        ],
        compiler_params=_cparams(("parallel",)),
        name="rwkv_prep",
    )(p, p, p, p, p, p, mu_rkv, mu_lora, w0, w2p, a0, a2p, g2p, k_k, k_a, r_k, bd, tri)


def _bdot(a, b):
    return lax.dot_general(a, b, (((2,), (1,)), ((0,), (0,))), preferred_element_type=F32)


def _bdot_nt(a, b):
    return lax.dot_general(a, b, (((2,), (2,)), ((0,), (0,))), preferred_element_type=F32)


def _bdot_tn(a, b):
    return lax.dot_general(a, b, (((1,), (1,)), ((0,), (0,))), preferred_element_type=F32)


def _bdot_inv(a, b):
    return _bdot(a.astype(BF16), b.astype(BF16))


def _wkv_kernel(*refs):
    (vf_ref, vr_ref), ins, (gcf_ref, gcr_ref, yf_ref, yr_ref, s_ref) = refs[:2], refs[2:14], refs[14:]

    @pl.when(pl.program_id(0) == 0)
    def _():
        s_ref[...] = jnp.zeros_like(s_ref)

    _wkv_chunk(False, vf_ref, ins[0:6], gcf_ref, yf_ref, s_ref.at[0])
    _wkv_chunk(True, vr_ref, ins[6:12], gcr_ref, yr_ref, s_ref.at[1])


def _wkv_chunk(reverse, v_ref, scaled_refs, gc_ref, y_ref, s_ref):
    nh = RWKV_HEADS
    row = lax.broadcasted_iota(jnp.int32, (nh, CHUNK, CHUNK), 1)
    col = lax.broadcasted_iota(jnp.int32, (nh, CHUNK, CHUNK), 2)
    strict = (row < col) if reverse else (row > col)
    incl = (row <= col) if reverse else (row >= col)

    v = v_ref[...]
    rb, kb, kt, bt, kh, bh = [r[...] for r in scaled_refs]

    c = CHUNK
    kr = jnp.concatenate([kb, rb], axis=1)
    sk = _bdot_nt(kr, kt)
    sb = _bdot_nt(kr, bt)
    a_kk = jnp.where(strict, sk[:, :c], 0.0)
    a_rk = jnp.where(incl, sk[:, c:], 0.0)
    l_mat = jnp.where(strict, sb[:, :c], 0.0)
    a_rb = jnp.where(incl, sb[:, c:], 0.0)

    eye = jnp.where(row == col, 1.0, 0.0)
    t_inv = eye - l_mat
    pw = _bdot_inv(l_mat, l_mat)
    for step in range(5):
        t_inv = t_inv + _bdot_inv(t_inv, pw)
        if step < 4:
            pw = _bdot_inv(pw, pw)

    q12 = _bdot(jnp.concatenate([a_kk, a_rk], axis=1).astype(BF16), v)
    q1, q2 = q12[:, :c], q12[:, c:]
    t_b = t_inv.astype(BF16)
    wkuv = _bdot(t_b, jnp.concatenate([kb, q1.astype(BF16)], axis=2)).astype(BF16)
    corr = _bdot(a_rb.astype(BF16), wkuv)
    rw = rb.astype(F32) - corr[:, :, :RWKV_HEAD_DIM]
    y0 = q2 - corr[:, :, RWKV_HEAD_DIM:]
    tn = _bdot_tn(wkuv, bh)
    m_mat = tn[:, :RWKV_HEAD_DIM]
    j_t = _bdot_tn(v, kh) - tn[:, RWKV_HEAD_DIM:]

    s_old = s_ref[...]
    s_hi = s_old.astype(BF16)
    s_lo = (s_old - s_hi.astype(F32)).astype(BF16)
    y_ref[...] = _bdot_nt(rw.astype(BF16), s_hi) + y0
    sm = _bdot(jnp.concatenate([s_hi, s_lo], axis=1), m_mat.astype(BF16))
    n = RWKV_HEAD_DIM
    s_ref[...] = s_old * gc_ref[:, 0:1, :] - (sm[:, :n] + sm[:, n:]) + j_t


def _wkv(v, rb, kb, kt, bt, kh, bh, gc):
    s = v.shape[1]
    nc = s // CHUNK
    scaled = (rb, kb, kt, bt, kh, bh)
    fwd = lambda rows: pl.BlockSpec((None, RWKV_HEADS, rows, 64), lambda c: (0, 0, c, 0))
    rev = lambda rows: pl.BlockSpec((None, RWKV_HEADS, rows, 64), lambda c: (1, 0, nc - 1 - c, 0))
    y_shape = jax.ShapeDtypeStruct((RWKV_HEADS, s, 64), F32)
    return pl.pallas_call(
        _wkv_kernel,
        grid=(nc,),
        in_specs=[
            pl.BlockSpec((RWKV_HEADS, CHUNK, 64), lambda c: (0, c, 0)),
            pl.BlockSpec((RWKV_HEADS, CHUNK, 64), lambda c: (0, nc - 1 - c, 0)),
            *[fwd(CHUNK) for _ in scaled], *[rev(CHUNK) for _ in scaled],
            fwd(8), rev(8),
        ],
        out_specs=[
            pl.BlockSpec((RWKV_HEADS, CHUNK, 64), lambda c: (0, c, 0)),
            pl.BlockSpec((RWKV_HEADS, CHUNK, 64), lambda c: (0, nc - 1 - c, 0)),
        ],
        out_shape=[y_shape, y_shape],
        scratch_shapes=[pltpu.VMEM((2, RWKV_HEADS, 64, 64), F32)],
        compiler_params=_cparams(("arbitrary",)),
        name="wkv7_chunked",
    )(v, v, *scaled, *scaled, gc, gc)


def _rwkv_post_kernel(yf_ref, yr_ref, g_ref, bonus_ref, lg_ref, lb_ref, o_ref):
    y = yf_ref[...] + yr_ref[...]
    mean = jnp.mean(y, axis=-1, keepdims=True)
    yc = y - mean
    var = jnp.mean(yc * yc, axis=-1, keepdims=True)
    yn = yc * lax.rsqrt(var + LNX_EPS)
    yt = jnp.concatenate([yn[h] for h in range(RWKV_HEADS)], axis=-1)
    out = (yt * lg_ref[...] + lb_ref[...] + bonus_ref[...].astype(F32)) * g_ref[...].astype(F32)
    o_ref[...] = out.astype(o_ref.dtype)


def _rwkv_post(y_fwd, y_rev, g, bonus, lnx_g, lnx_b, *, tm):
    s = g.shape[0]
    return pl.pallas_call(
        _rwkv_post_kernel,
        grid=(s // tm,),
        in_specs=[
            pl.BlockSpec((RWKV_HEADS, tm, 64), lambda i: (0, i, 0)),
            pl.BlockSpec((RWKV_HEADS, tm, 64), lambda i: (0, i, 0)),
            pl.BlockSpec((tm, RWKV_DIM), lambda i: (i, 0)),
            pl.BlockSpec((tm, RWKV_DIM), lambda i: (i, 0)),
            pl.BlockSpec((1, RWKV_DIM), lambda i: (0, 0)),
            pl.BlockSpec((1, RWKV_DIM), lambda i: (0, 0)),
        ],
        out_specs=pl.BlockSpec((tm, RWKV_DIM), lambda i: (i, 0)),
        out_shape=jax.ShapeDtypeStruct((s, RWKV_DIM), BF16),
        compiler_params=_cparams(("parallel",)),
        name="rwkv_post",
    )(y_fwd, y_rev, g, bonus, lnx_g.reshape(1, RWKV_DIM), lnx_b.reshape(1, RWKV_DIM))


def _mem_attn_kernel(q_ref, kv_ref, gq_ref, gk_ref, o_ref):
    for h in range(MEM_HEADS):
        sl = slice(MEM_HEAD_DIM * h, MEM_HEAD_DIM * (h + 1))
        q = q_ref[:, sl].astype(F32)
        q = q * lax.rsqrt(jnp.mean(q * q, axis=-1, keepdims=True) + NORM_EPS)
        q = q * gq_ref[...] * (MEM_HEAD_DIM ** -0.5)
        km = kv_ref[:, sl].astype(F32)
        km = km * lax.rsqrt(jnp.mean(km * km, axis=-1, keepdims=True) + NORM_EPS)
        km = km * gk_ref[...]
        vm = kv_ref[:, MEM_DIM + MEM_HEAD_DIM * h:MEM_DIM + MEM_HEAD_DIM * (h + 1)]
        s = lax.dot_general(q.astype(BF16), km.astype(BF16), (((1,), (1,)), ((), ())),
                            preferred_element_type=F32)
        s = s - jnp.max(s, axis=-1, keepdims=True)
        e = jnp.exp(s)
        pr = e / jnp.sum(e, axis=-1, keepdims=True)
        o_ref[:, sl] = _dot(pr.astype(BF16), vm).astype(o_ref.dtype)


def _mem_attn(p, kv, gq, gk, *, tm):
    s = p.shape[0]
    assert COL_MEM % MEM_DIM == 0
    return pl.pallas_call(
        _mem_attn_kernel,
        grid=(s // tm,),
        in_specs=[
            pl.BlockSpec((tm, MEM_DIM), lambda i: (i, COL_MEM // MEM_DIM)),
            pl.BlockSpec((N_MEM, 2 * MEM_DIM), lambda i: (0, 0)),
            pl.BlockSpec((1, MEM_HEAD_DIM), lambda i: (0, 0)),
            pl.BlockSpec((1, MEM_HEAD_DIM), lambda i: (0, 0)),
        ],
        out_specs=pl.BlockSpec((tm, MEM_DIM), lambda i: (i, 0)),
        out_shape=jax.ShapeDtypeStruct((s, MEM_DIM), BF16),
        compiler_params=_cparams(("parallel",)),
        name="mem_attn",
    )(p, kv, gq, gk)


def _merge_kernel(o0_ref, o1_ref, o2_ref, g0_ref, g1_ref, g2_ref, w_ref, m_ref):
    def gate(g_ref):
        return 0.5 * jnp.tanh(0.5 * g_ref[...].astype(F32)) + 0.5

    acc = gate(g0_ref) * _dot(o0_ref[...], w_ref[0])
    acc = acc + gate(g1_ref) * _dot(o1_ref[...], w_ref[1])
    acc = acc + gate(g2_ref) * _dot(o2_ref[...], w_ref[2])
    m_ref[...] = acc.astype(m_ref.dtype)


def _merge(o_diff, o_rwkv, o_mem, p, w_branch, layer, *, tm, tn):
    s = p.shape[0]
    assert COL_GATE % tn == 0 and D_MODEL % tn == 0
    gate_blk = COL_GATE // tn
    per = D_MODEL // tn
    o_spec = lambda: pl.BlockSpec((tm, 1024), lambda i, j: (i, 0))
    g_spec = lambda b: pl.BlockSpec((tm, tn), lambda i, j: (i, gate_blk + b * per + j))
    return pl.pallas_call(
        _merge_kernel,
        grid=(s // tm, per),
        in_specs=[
            o_spec(), o_spec(), o_spec(),
            g_spec(0), g_spec(1), g_spec(2),
            pl.BlockSpec((None, N_BRANCH, 1024, tn), lambda i, j: (layer, 0, 0, j)),
        ],
        out_specs=pl.BlockSpec((tm, tn), lambda i, j: (i, j)),
        out_shape=jax.ShapeDtypeStruct((s, D_MODEL), BF16),
        compiler_params=_cparams(("parallel", "arbitrary")),
        name="merge",
    )(o_diff, o_rwkv, o_mem, p, p, p, w_branch)


GLU_TILE = 256


def _ffn_up_glu_kernel(x_ref, xp_ref, xn_ref, g_ref, wg_ref, wv_ref, cw_ref, cb_ref, o_ref,
                       h_ref, halo_ref):
    i = pl.program_id(0)
    nb = pl.num_programs(0)

    def norm(x):
        ms = jnp.mean(x * x, axis=-1, keepdims=True)
        return x * lax.rsqrt(ms + NORM_EPS) * g_ref[...]

    @pl.when(pl.program_id(1) == 0)
    def _():
        h_ref[...] = norm(x_ref[...]).astype(BF16)
        row = lax.broadcasted_iota(jnp.int32, xp_ref.shape, 0)
        before = jnp.where(i > 0, pltpu.roll(norm(xp_ref[...]), 1, 0), 0.0)
        after = jnp.where(i < nb - 1, pltpu.roll(norm(xn_ref[...]), 1, 0), 0.0)
        halo_ref[...] = jnp.where(row == 0, before, jnp.where(row == 1, after, 0.0)).astype(BF16)

    h = h_ref[...]
    halo = halo_ref[...]
    tn = o_ref.shape[1]
    for c0 in range(0, tn, GLU_TILE):
        sl = slice(c0, c0 + GLU_TILE)
        wg = wg_ref[:, sl]
        ug = _dot(h, wg)
        uv = _dot(h, wv_ref[:, sl])
        edge = _dot(halo, wg)
        gp, gn = _shift_rows(ug, edge[0:1], edge[1:2])
        cw = cw_ref[:, sl]
        conv = cw[0:1] * gp + cw[1:2] * ug + cw[2:3] * gn + cb_ref[:, sl]
        half = 0.5 * conv
        o_ref[:, sl] = ((half + half * jnp.tanh(half)) * uv).astype(o_ref.dtype)


def _ffn_up_glu(x, g, w_up, layer, conv_w, conv_b, *, tm, tn):
    s, k = x.shape
    assert tn % GLU_TILE == 0
    nj = D_FF // tn
    hb = tm // 16
    last16 = s // 16 - 1
    return pl.pallas_call(
        _ffn_up_glu_kernel,
        grid=(s // tm, nj),
        in_specs=[
            pl.BlockSpec((tm, k), lambda i, j: (i, 0)),
            pl.BlockSpec((16, k), lambda i, j: (jnp.maximum(i * hb - 1, 0), 0)),
            pl.BlockSpec((16, k), lambda i, j: (jnp.minimum((i + 1) * hb, last16), 0)),
            pl.BlockSpec((1, k), lambda i, j: (0, 0)),
            pl.BlockSpec((None, k, tn), lambda i, j: (layer, 0, j)),
            pl.BlockSpec((None, k, tn), lambda i, j: (layer, 0, nj + j)),
            pl.BlockSpec((3, tn), lambda i, j: (0, j)),
            pl.BlockSpec((1, tn), lambda i, j: (0, j)),
        ],
        out_specs=pl.BlockSpec((tm, tn), lambda i, j: (i, j)),
        out_shape=jax.ShapeDtypeStruct((s, D_FF), BF16),
        scratch_shapes=[pltpu.VMEM((tm, k), BF16), pltpu.VMEM((16, k), BF16)],
        compiler_params=_cparams(("parallel", "arbitrary")),
        name="ffn_up_glu",
    )(x, x, x, g.reshape(1, k), w_up, w_up, conv_w, conv_b.reshape(1, D_FF))


def _block_ones(n, group):
    idx = np.arange(n) // group
    return jnp.asarray(idx[:, None] == idx[None, :], BF16)


def _chunk_tri(n):
    idx = np.arange(n)
    same = (idx[:, None] // CHUNK) == (idx[None, :] // CHUNK)
    return jnp.asarray(same & (idx[:, None] >= idx[None, :]), BF16)


def _pad_rows(w, rows_before, total):
    n = w.shape[-1]
    out = jnp.zeros((total, n), w.dtype)
    return lax.dynamic_update_slice(out, w, (rows_before, 0))


def _pad_w_in(w):
    lora0 = COL_RKV + 3 * RWKV_DIM
    mem0 = lora0 + LORA_COLS
    gate0 = mem0 + MEM_DIM
    pad = jnp.zeros((w.shape[0], LORA_PAD - LORA_COLS), w.dtype)
    parts = [w[:, :lora0], w[:, gate0:], w[:, mem0:gate0], w[:, lora0:mem0], pad]
    return jnp.concatenate(parts, axis=1).astype(BF16)


def kernel(x, mem, attn_norm_g, w_in, diff_qk_g, diff_lambda, diff_subln_g, rwkv_mu, rwkv_w0,
           rwkv_w2, rwkv_a0, rwkv_a2, rwkv_g2, rwkv_k_k, rwkv_k_a, rwkv_r_k, rwkv_lnx_g,
           rwkv_lnx_b, mem_norm_g, w_mem_kv, mem_qk_g, w_branch, w_out, ffn_norm_g, w_ffn_up,
           ffn_conv_w, ffn_conv_b, w_ffn_down):
    b, s, d = x.shape
    assert b == 1 and d == D_MODEL and s % ATT_BQ == 0
    xs = x.reshape(s, d)
    mem2 = mem.reshape(N_MEM, d)
    prep_tm = 256
    bd64 = _block_ones(1024, 64)
    tri = _chunk_tri(prep_tm)
    n_rwkv_main = 3 * RWKV_DIM
    big_tm = 1024 if s % 1024 == 0 else 512
    w_mem_kv_b = w_mem_kv.astype(BF16)
    w_branch_b = w_branch.astype(BF16)
    w_out_b = w_out.astype(BF16)
    w_ffn_up_b = w_ffn_up.astype(BF16)
    w_ffn_down_b = w_ffn_down.astype(BF16)

    for l in range(DEPTH):
        lam_init = 0.8 - 0.6 * math.exp(-0.3 * l)
        p = _rms_mm(xs, attn_norm_g[l], _pad_w_in(w_in[l])[None], 0, tm=big_tm, tn=1536,
                    name="rms_w_in")

        gq = jnp.tile(diff_qk_g[l, 0].reshape(1, 128), (1, DIFF_HEADS))
        gk = jnp.tile(diff_qk_g[l, 1].reshape(1, 128), (1, DIFF_HEADS))
        qc, kc, score_bound = _attn_consts(diff_qk_g[l])
        qa, ka, vt = _diff_prep(p, gq, gk, qc, kc, bd64, tm=256)
        o_diff = lax.cond(
            score_bound <= SCORE_BOUND_MAX,
            functools.partial(_diff_attn, lam_init=lam_init, online=False),
            functools.partial(_diff_attn, lam_init=lam_init, online=True),
            qa, ka, vt, diff_lambda[l], diff_subln_g[l])

        mu = rwkv_mu[l]
        mu_rkv = mu[:, :n_rwkv_main]
        mu_lora = jnp.pad(mu[:, n_rwkv_main:], ((0, 0), (0, LORA_PAD - LORA_COLS)))
        w2p = jnp.stack([_pad_rows(rwkv_w2[l, dd], 64 * dd, 128) for dd in range(2)]).astype(BF16)
        a2p = jnp.stack([_pad_rows(rwkv_a2[l, dd], 64 * dd, 128) for dd in range(2)]).astype(BF16)
        g2p = _pad_rows(rwkv_g2[l], 0, 256).astype(BF16)
        (v_h, g_tok, bonus, rb, kb, kt, bt, kh, bh, gc) = _rwkv_prep(
            p, mu_rkv, mu_lora, rwkv_w0[l], w2p, rwkv_a0[l], a2p, g2p,
            rwkv_k_k[l].reshape(1, RWKV_DIM), rwkv_k_a[l].reshape(1, RWKV_DIM),
            rwkv_r_k[l].reshape(1, RWKV_DIM), bd64, tri, tm=prep_tm)
        y_fwd, y_rev = _wkv(v_h, rb, kb, kt, bt, kh, bh, gc)
        o_rwkv = _rwkv_post(y_fwd, y_rev, g_tok, bonus, rwkv_lnx_g[l], rwkv_lnx_b[l], tm=256)

        kv = _rms_mm(mem2, mem_norm_g[l], w_mem_kv_b, l, tm=N_MEM, tn=1024, name="rms_mem_kv")
        o_mem = _mem_attn(p, kv, mem_qk_g[l, 0].reshape(1, MEM_HEAD_DIM),
                          mem_qk_g[l, 1].reshape(1, MEM_HEAD_DIM), tm=512)

        merged = _merge(o_diff, o_rwkv, o_mem, p, w_branch_b, l, tm=256, tn=D_MODEL)
        xs = _mm_res(merged, w_out_b, l, xs, tm=512, tn=D_MODEL, name="w_out_res")

        act = _ffn_up_glu(xs, ffn_norm_g[l], w_ffn_up_b, l, ffn_conv_w[l], ffn_conv_b[l],
                          tm=big_tm, tn=512)
        xs = _mm_res(act, w_ffn_down_b, l, xs, tm=big_tm, tn=512, name="ffn_down_res")

    return xs.reshape(b, s, d)
```

```python
import functools
import math

import jax
import jax.numpy as jnp
import numpy as np
from jax import lax
from jax.experimental import pallas as pl
from jax.experimental.pallas import tpu as pltpu

F32 = jnp.float32
BF16 = jnp.bfloat16

D_MODEL = 2048
DEPTH = 2
DIFF_HEADS = 8
DIFF_QK_DIM = 64
DIFF_V_DIM = 128
DIFF_DIM = 1024
RWKV_HEADS = 16
RWKV_HEAD_DIM = 64
RWKV_DIM = 1024
DECAY_LORA = 64
AAA_LORA = 64
GATE_LORA = 160
LORA_COLS = 2 * DECAY_LORA + 2 * AAA_LORA + GATE_LORA
LORA_PAD = 512
N_MEM = 256
MEM_HEADS = 4
MEM_HEAD_DIM = 256
MEM_DIM = 1024
N_BRANCH = 3
D_FF = 5632
NORM_EPS = 1e-6
LNX_EPS = 64e-5

COL_DIFF = 0
COL_RKV = 3 * DIFF_DIM
COL_GATE = COL_RKV + 3 * RWKV_DIM
COL_MEM = COL_GATE + N_BRANCH * D_MODEL
COL_LORA = COL_MEM + MEM_DIM
N_IN_PAD = COL_LORA + LORA_PAD

CHUNK = 64
LOG2E = 1.4426950408889634
N_POS_COLS = 12
SCORE_BOUND_MAX = 30.0
ATT_BQ = 1024
ATT_BK = 512
V_AUG = 144
FAR_UNROLL = 6
VMEM_LIMIT = 56 * 1024 * 1024


def _cparams(sem):
    return pltpu.CompilerParams(dimension_semantics=sem, vmem_limit_bytes=VMEM_LIMIT)


def _split2(x):
    hi = x.astype(BF16)
    return hi, (x - hi.astype(F32)).astype(BF16)


def _dot(a, b):
    return jnp.dot(a, b, preferred_element_type=F32)


def _dot_exactish(a_bf16_exact, x_f32):
    hi, lo = _split2(x_f32)
    return _dot(a_bf16_exact, hi) + _dot(a_bf16_exact, lo)


def _rms_mm_kernel(x_ref, g_ref, w_ref, o_ref, h_ref):
    @pl.when(pl.program_id(1) == 0)
    def _():
        x = x_ref[...]
        ms = jnp.mean(x * x, axis=-1, keepdims=True)
        h_ref[...] = (x * lax.rsqrt(ms + NORM_EPS) * g_ref[...]).astype(BF16)

    o_ref[...] = _dot(h_ref[...], w_ref[...]).astype(o_ref.dtype)


def _rms_mm(x, g, w, layer, *, tm, tn, name):
    m, k = x.shape
    n = w.shape[2]
    return pl.pallas_call(
        _rms_mm_kernel,
        grid=(m // tm, n // tn),
        in_specs=[
            pl.BlockSpec((tm, k), lambda i, j: (i, 0)),
            pl.BlockSpec((1, k), lambda i, j: (0, 0)),
            pl.BlockSpec((None, k, tn), lambda i, j: (layer, 0, j)),
        ],
        out_specs=pl.BlockSpec((tm, tn), lambda i, j: (i, j)),
        out_shape=jax.ShapeDtypeStruct((m, n), BF16),
        scratch_shapes=[pltpu.VMEM((tm, k), BF16)],
        compiler_params=_cparams(("parallel", "arbitrary")),
        name=name,
    )(x, g.reshape(1, k), w)


def _mm_res_kernel(a_ref, w_ref, r_ref, o_ref):
    o_ref[...] = r_ref[...] + _dot(a_ref[...], w_ref[...])


def _mm_res(a, w, layer, res, *, tm, tn, name):
    m, k = a.shape
    n = w.shape[2]
    return pl.pallas_call(
        _mm_res_kernel,
        grid=(m // tm, n // tn),
        in_specs=[
            pl.BlockSpec((tm, k), lambda i, j: (i, 0)),
            pl.BlockSpec((None, k, tn), lambda i, j: (layer, 0, j)),
            pl.BlockSpec((tm, tn), lambda i, j: (i, j)),
        ],
        out_specs=pl.BlockSpec((tm, tn), lambda i, j: (i, j)),
        out_shape=jax.ShapeDtypeStruct((m, n), F32),
        compiler_params=_cparams(("parallel", "arbitrary")),
        name=name,
    )(a, w, res)


def _group_mean_sq(x, bd_ref, group):
    return _dot((x * x).astype(BF16), bd_ref[...]) * (1.0 / group)


def _aug_base(mp):
    return 64 if mp == 0 else 0


def _diff_prep_kernel(q_ref, k_ref, v_ref, gq_ref, gk_ref, qc_ref, kc_ref, bd_ref,
                      qa_ref, ka_ref, vt_ref):
    tm = q_ref.shape[0]
    row0 = pl.program_id(0) * tm
    lane = lax.broadcasted_iota(jnp.int32, (tm, 128), 1)
    pos = row0 + lax.broadcasted_iota(jnp.int32, (tm, 128), 0)
    pos_lo = (pos & 127).astype(F32)
    pos_hi = (pos >> 7).astype(F32)

    q = q_ref[...].astype(F32)
    qn = q * lax.rsqrt(_group_mean_sq(q, bd_ref, DIFF_QK_DIM) + NORM_EPS)
    qn = qn * gq_ref[...] * (DIFF_QK_DIM ** -0.5 * LOG2E)
    k = k_ref[...].astype(F32)
    kn = k * lax.rsqrt(_group_mean_sq(k, bd_ref, DIFF_QK_DIM) + NORM_EPS)
    kn = kn * gk_ref[...]

    for h in range(DIFF_HEADS):
        qh = qn[:, 128 * h:128 * (h + 1)]
        kh = kn[:, 128 * h:128 * (h + 1)]
        for mp in range(2):
            a0 = _aug_base(mp)
            g = 2 * h + mp
            is_data = (lane < 64) if mp == 0 else (lane >= 64)
            aug_q = jnp.where(lane < a0 + 3, pos_lo,
                              jnp.where(lane < a0 + 6, pos_hi, qc_ref[g:g + 1, :]))
            aug_k = jnp.where((lane >= a0 + 6) & (lane < a0 + 9), pos_lo,
                              jnp.where((lane >= a0 + 9) & (lane < a0 + 12), pos_hi,
                                        kc_ref[g:g + 1, :]))
            qa_ref[h, mp] = jnp.where(is_data, qh, aug_q).astype(BF16)
            ka_ref[h, mp] = jnp.where(is_data, kh, aug_k).astype(BF16)

    vt = v_ref[...].astype(F32).T
    sub = lax.broadcasted_iota(jnp.int32, (V_AUG - DIFF_V_DIM, tm), 0)
    ones_rows = jnp.where(sub == 0, 1.0, 0.0).astype(BF16)
    for h in range(DIFF_HEADS):
        vt_ref[h, 0:DIFF_V_DIM, :] = vt[128 * h:128 * (h + 1), :].astype(BF16)
        vt_ref[h, DIFF_V_DIM:V_AUG, :] = ones_rows


def _bf16_split3_const(x):
    parts = []
    for _ in range(3):
        part = float(np.asarray(x, np.float32).astype(BF16).astype(np.float32))
        parts.append(part)
        x = x - part
    return parts


def _attn_consts(qk_g):
    l_parts = _bf16_split3_const(LOG2E)
    m_nat = 8.0 * jnp.max(jnp.abs(qk_g[0]), axis=-1) * jnp.max(jnp.abs(qk_g[1]), axis=-1)
    qc = np.zeros((2 * DIFF_HEADS, 128), np.float32)
    kc = np.zeros((2 * DIFF_HEADS, 128), np.float32)
    bound_lane = np.zeros((2, 2 * DIFF_HEADS, 128), np.float32)
    for h in range(DIFF_HEADS):
        slope = 2.0 ** (-(h + 1))
        for mp in range(2):
            a0 = _aug_base(mp)
            g = 2 * h + mp
            for t, lp in enumerate(l_parts):
                qc[g, a0 + 6 + t] = slope * lp
                qc[g, a0 + 9 + t] = 128.0 * slope * lp
                kc[g, a0 + t] = -slope * lp
                kc[g, a0 + 3 + t] = -128.0 * slope * lp
            bound_lane[mp, g, a0 + N_POS_COLS] = 1.0
            kc[g, a0 + N_POS_COLS] = 1.0
    m2 = -m_nat * LOG2E
    qc = qc + m2[0] * bound_lane[0] + m2[1] * bound_lane[1]
    return qc, jnp.asarray(kc), jnp.max(m_nat)


def _diff_prep(p, gq, gk, qc, kc, bd, *, tm):
    s = p.shape[0]
    return pl.pallas_call(
        _diff_prep_kernel,
        grid=(s // tm,),
        in_specs=[
            pl.BlockSpec((tm, DIFF_DIM), lambda i: (i, 0)),
            pl.BlockSpec((tm, DIFF_DIM), lambda i: (i, 1)),
            pl.BlockSpec((tm, DIFF_DIM), lambda i: (i, 2)),
            pl.BlockSpec((1, DIFF_DIM), lambda i: (0, 0)),
            pl.BlockSpec((1, DIFF_DIM), lambda i: (0, 0)),
            pl.BlockSpec((2 * DIFF_HEADS, 128), lambda i: (0, 0)),
            pl.BlockSpec((2 * DIFF_HEADS, 128), lambda i: (0, 0)),
            pl.BlockSpec((DIFF_DIM, DIFF_DIM), lambda i: (0, 0)),
        ],
        out_specs=[
            pl.BlockSpec((DIFF_HEADS, 2, tm, 128), lambda i: (0, 0, i, 0)),
            pl.BlockSpec((DIFF_HEADS, 2, tm, 128), lambda i: (0, 0, i, 0)),
            pl.BlockSpec((DIFF_HEADS, V_AUG, tm), lambda i: (0, 0, i)),
        ],
        out_shape=[
            jax.ShapeDtypeStruct((DIFF_HEADS, 2, s, 128), BF16),
            jax.ShapeDtypeStruct((DIFF_HEADS, 2, s, 128), BF16),
            jax.ShapeDtypeStruct((DIFF_HEADS, V_AUG, s), BF16),
        ],
        compiler_params=_cparams(("parallel",)),
        name="diff_prep",
    )(p, p, p, gq, gk, qc, kc, bd)


def _diff_attn_kernel(lam_ref, sg_ref, dnear_ref, qa_ref, ka_ref, vt_ref, o_ref,
                      acc_ref, qv_ref, *mode_refs, lam_init, online):
    h = pl.program_id(0)
    i = pl.program_id(1)
    bq = qa_ref.shape[1]
    s_len = ka_ref.shape[1]
    bk = ATT_BK
    n_chunks = s_len // bk
    per_q = bq // bk
    n_far = n_chunks - per_q
    j_lo = i * per_q
    q0 = i * bq
    slope2 = jnp.exp2(-jnp.full((1, 1), h + 1, jnp.int32).astype(F32)) * LOG2E
    if online:
        m_ref, sa_ref, sb_ref, mxa_ref, mxb_ref = mode_refs
        m_ref[...] = jnp.full_like(m_ref, -1e30)
        bufs = ((sa_ref, mxa_ref), (sb_ref, mxb_ref))
    else:
        bufs = mode_refs
    acc_ref[...] = jnp.zeros_like(acc_ref)

    lane = lax.broadcasted_iota(jnp.int32, (1, 128), 1)
    for mp in range(2):
        q = qa_ref[mp]
        pos_lane = (lane >= _aug_base(mp)) & (lane < _aug_base(mp) + N_POS_COLS)
        qv_ref[mp, 0] = q
        qv_ref[mp, 1] = jnp.where(pos_lane, -q, q)
        qv_ref[mp, 2] = jnp.where(pos_lane, jnp.zeros_like(q), q)

    def stage_one(k0, variant, bias, buf):
        for mp in range(2):
            kc = ka_ref[mp, pl.ds(k0, bk), :]
            s = lax.dot_general(kc, qv_ref[mp, variant], (((1,), (1,)), ((), ())),
                                preferred_element_type=F32)
            if bias is not None:
                s = s + bias
            if online:
                buf[0][mp] = s
                buf[1][mp] = jnp.max(s, axis=0, keepdims=True)
            else:
                buf[mp] = jnp.exp2(s).astype(BF16)

    def stage_two(k0, buf):
        vt_c = vt_ref[:, pl.ds(k0, bk)]
        for mp in range(2):
            if online:
                m_old = m_ref[mp]
                m_new = jnp.maximum(m_old, buf[1][mp])
                p = jnp.exp2(buf[0][mp] - m_new).astype(BF16)
                acc_ref[mp] = acc_ref[mp] * jnp.exp2(m_old - m_new) + _dot(vt_c, p)
                m_ref[mp] = m_new
            else:
                acc_ref[mp] += _dot(vt_c, buf[mp])

    def near_start(t):
        return pl.multiple_of(q0 + t * bk, bk)

    def far_start(t):
        t = jnp.minimum(t, n_far - 1)
        j = jnp.where(t < j_lo, t, t + per_q)
        return pl.multiple_of(j * bk, bk), (t >= j_lo).astype(jnp.int32)

    def near_one(t):
        stage_one(near_start(t), 2, slope2 * dnear_ref[t], bufs[t % 2])

    def far_one(t, parity):
        k0, after = far_start(t)
        stage_one(k0, after, None, bufs[parity])

    def far_two(t, parity):
        stage_two(far_start(t)[0], bufs[parity])

    near_one(0)
    for t in range(1, per_q):
        near_one(t)
        stage_two(near_start(t - 1), bufs[(t - 1) % 2])
    far_one(0, per_q % 2)
    stage_two(near_start(per_q - 1), bufs[(per_q - 1) % 2])

    def far_body(u, c):
        for r in range(FAR_UNROLL):
            far_one(FAR_UNROLL * u + r + 1, (per_q + r + 1) % 2)
            far_two(FAR_UNROLL * u + r, (per_q + r) % 2)
        return c

    lax.fori_loop(0, n_far // FAR_UNROLL, far_body, 0)
    for t in range(n_far - n_far % FAR_UNROLL, n_far):
        if t + 1 < n_far:
            far_one(t + 1, (per_q + t + 1) % 2)
        far_two(t, (per_q + t) % 2)

    lp = lam_ref[...]
    lam = (jnp.exp(jnp.sum(lp[0:1] * lp[1:2], axis=-1, keepdims=True))
           - jnp.exp(jnp.sum(lp[2:3] * lp[3:4], axis=-1, keepdims=True)) + lam_init)
    a0 = acc_ref[0]
    a1 = acc_ref[1]
    o = (a0[0:DIFF_V_DIM] / a0[DIFF_V_DIM:DIFF_V_DIM + 1]
         - lam * (a1[0:DIFF_V_DIM] / a1[DIFF_V_DIM:DIFF_V_DIM + 1]))
    ms = jnp.mean(o * o, axis=0, keepdims=True)
    o = o * lax.rsqrt(ms + NORM_EPS) * sg_ref[...] * (1.0 - lam_init)
    o_ref[...] = o.T.astype(o_ref.dtype)


def _diff_attn(qa, ka, vt, lam_p, subln_g, *, lam_init, online):
    s = qa.shape[2]
    bq = ATT_BQ
    assert bq % ATT_BK == 0 and s > bq
    kern = functools.partial(_diff_attn_kernel, lam_init=lam_init, online=online)
    per_q = bq // ATT_BK
    key = np.arange(per_q * ATT_BK).reshape(per_q, ATT_BK, 1)
    dnear = jnp.asarray(-np.abs(np.arange(bq).reshape(1, 1, bq) - key), F32)
    scratch = [
        pltpu.VMEM((2, V_AUG, bq), F32),
        pltpu.VMEM((2, 3, bq, 128), BF16),
    ]
    if online:
        scratch += [
            pltpu.VMEM((2, 1, bq), F32),
            pltpu.VMEM((2, ATT_BK, bq), F32),
            pltpu.VMEM((2, ATT_BK, bq), F32),
            pltpu.VMEM((2, 1, bq), F32),
            pltpu.VMEM((2, 1, bq), F32),
        ]
    else:
        scratch += [pltpu.VMEM((2, ATT_BK, bq), BF16), pltpu.VMEM((2, ATT_BK, bq), BF16)]
    return pl.pallas_call(
        kern,
        grid=(DIFF_HEADS, s // bq),
        in_specs=[
            pl.BlockSpec((4, DIFF_QK_DIM), lambda h, i: (0, 0)),
            pl.BlockSpec((DIFF_V_DIM, 1), lambda h, i: (0, 0)),
            pl.BlockSpec((per_q, ATT_BK, bq), lambda h, i: (0, 0, 0)),
            pl.BlockSpec((None, 2, bq, 128), lambda h, i: (h, 0, i, 0)),
            pl.BlockSpec((None, 2, s, 128), lambda h, i: (h, 0, 0, 0)),
            pl.BlockSpec((None, V_AUG, s), lambda h, i: (h, 0, 0)),
        ],
        out_specs=pl.BlockSpec((bq, DIFF_V_DIM), lambda h, i: (i, h)),
        out_shape=jax.ShapeDtypeStruct((s, DIFF_DIM), BF16),
        scratch_shapes=scratch,
        compiler_params=_cparams(("parallel", "arbitrary")),
        name="diff_attn_online" if online else "diff_attn",
    )(lam_p, subln_g.reshape(DIFF_V_DIM, 1), dnear, qa, ka, vt)


def _shift_rows(x, prev_row, next_row):
    n = x.shape[0]
    row = lax.broadcasted_iota(jnp.int32, x.shape, 0)
    xp = jnp.where(row == 0, prev_row, pltpu.roll(x, 1, 0))
    xn = jnp.where(row == n - 1, next_row, pltpu.roll(x, n - 1, 0))
    return xp, xn


def _halo_rows(prev_ref, next_ref, i, n_blocks):
    hp = prev_ref.shape[0]
    prev_row = prev_ref[hp - 1:hp, :].astype(F32)
    next_row = next_ref[0:1, :].astype(F32)
    prev_row = jnp.where(i > 0, prev_row, 0.0)
    next_row = jnp.where(i < n_blocks - 1, next_row, 0.0)
    return prev_row, next_row


def _rwkv_prep_kernel(
        x_ref, xp_ref, xn_ref, l_ref, lp_ref, ln_ref,
        mu_ref, mul_ref, w0_ref, w2_ref, a0_ref, a2_ref, g2_ref, kk_ref, ka_ref, rk_ref,
        bd_ref, tri_ref,
        v_out, g_out, bonus_out, rb_out, kb_out, kt_out, bt_out, kh_out, bh_out, gc_out):
    i = pl.program_id(0)
    nb = pl.num_programs(0)
    tm = x_ref.shape[0]

    x = x_ref[...].astype(F32)
    prev_row, next_row = _halo_rows(xp_ref, xn_ref, i, nb)
    xp, xn = _shift_rows(x, prev_row, next_row)
    mu = mu_ref[...]
    x = x + mu[0:1] * (xp - x) + mu[1:2] * (xn - x)
    lo = l_ref[...].astype(F32)
    prev_row, next_row = _halo_rows(lp_ref, ln_ref, i, nb)
    lop, lon = _shift_rows(lo, prev_row, next_row)
    mul = mul_ref[...]
    lo = lo + mul[0:1] * (lop - lo) + mul[1:2] * (lon - lo)

    r = x[:, 0:RWKV_DIM]
    k = x[:, RWKV_DIM:2 * RWKV_DIM]
    v = x[:, 2 * RWKV_DIM:3 * RWKV_DIM]
    tw = jnp.tanh(lo[:, 0:128]).astype(BF16)
    la = lo[:, 128:256].astype(BF16)
    lg = jax.nn.sigmoid(lo[:, 256:512]).astype(BF16)

    g = _dot(lg, g2_ref[...])
    kk = k * kk_ref[...]
    ss = _group_mean_sq(kk, bd_ref, 1.0)
    kk = kk * lax.rsqrt(jnp.maximum(ss, 1e-24))
    bonus = _dot((r * k * rk_ref[...]).astype(BF16), bd_ref[...]) * v

    g_out[...] = g.astype(g_out.dtype)
    bonus_out[...] = bonus.astype(bonus_out.dtype)
    for h in range(RWKV_HEADS):
        v_out[h] = v[:, 64 * h:64 * (h + 1)].astype(BF16)

    tri = tri_ref[...]
    for d in range(2):
        wl = w0_ref[d:d + 1, :] + _dot(tw, w2_ref[d])
        logdec = -math.exp(-0.5) * jax.nn.sigmoid(wl)
        a = jax.nn.sigmoid(a0_ref[d:d + 1, :] + _dot(la, a2_ref[d]))
        k_d = k * (1.0 + (a - 1.0) * ka_ref[...])
        b_d = kk * a
        pre = _dot_exactish(tri, logdec)
        chunk_tot = jnp.broadcast_to(
            pre.reshape(tm // CHUNK, CHUNK, RWKV_DIM)[:, CHUNK - 1:CHUNK, :],
            (tm // CHUNK, CHUNK, RWKV_DIM)).reshape(tm, RWKV_DIM)
        suf = chunk_tot - pre
        if d == 0:
            lc, ex, rem = pre, pre - logdec, suf
        else:
            lc, ex, rem = suf + logdec, suf, pre - logdec
        e_neg = jnp.exp(-lc)
        e_rem = jnp.exp(rem)
        outs = (
            (rb_out, r * jnp.exp(lc)),
            (kb_out, kk * jnp.exp(ex)),
            (kt_out, k_d * e_neg),
            (bt_out, b_d * e_neg),
            (kh_out, k_d * e_rem),
            (bh_out, b_d * e_rem),
        )
        for ref, val in outs:
            val = val.astype(BF16)
            for h in range(RWKV_HEADS):
                ref[d, h] = val[:, 64 * h:64 * (h + 1)]
        tot = jnp.exp(chunk_tot).reshape(tm // 8, 8, RWKV_DIM)[:, 0, :]
        for h in range(RWKV_HEADS):
            gc_out[d, h] = tot[:, 64 * h:64 * (h + 1)]


def _rwkv_prep(p, mu_rkv, mu_lora, w0, w2p, a0, a2p, g2p, k_k, k_a, r_k, bd, tri, *, tm):
    s = p.shape[0]
    nb = s // tm
    hb = tm // 16
    last16 = s // 16 - 1
    rkv_w = 3 * RWKV_DIM
    assert COL_RKV % rkv_w == 0 and COL_LORA % LORA_PAD == 0
    c_rkv = COL_RKV // rkv_w
    c_lora = COL_LORA // LORA_PAD

    def prev_map(c):
        return lambda i: (jnp.maximum(i * hb - 1, 0), c)

    def next_map(c):
        return lambda i: (jnp.minimum((i + 1) * hb, last16), c)

    full = lambda *shape: pl.BlockSpec(shape, lambda i: (0,) * len(shape))
    hm = lambda: pl.BlockSpec((2, RWKV_HEADS, tm, 64), lambda i: (0, 0, i, 0))
    hm_shape = jax.ShapeDtypeStruct((2, RWKV_HEADS, s, 64), BF16)
    return pl.pallas_call(
        _rwkv_prep_kernel,
        grid=(nb,),
        in_specs=[
            pl.BlockSpec((tm, rkv_w), lambda i: (i, c_rkv)),
            pl.BlockSpec((16, rkv_w), prev_map(c_rkv)),
            pl.BlockSpec((16, rkv_w), next_map(c_rkv)),
            pl.BlockSpec((tm, LORA_PAD), lambda i: (i, c_lora)),
            pl.BlockSpec((16, LORA_PAD), prev_map(c_lora)),
            pl.BlockSpec((16, LORA_PAD), next_map(c_lora)),
            full(2, rkv_w), full(2, LORA_PAD),
            full(2, RWKV_DIM), full(2, 128, RWKV_DIM),
            full(2, RWKV_DIM), full(2, 128, RWKV_DIM),
            full(256, RWKV_DIM),
            full(1, RWKV_DIM), full(1, RWKV_DIM), full(1, RWKV_DIM),
            full(RWKV_DIM, RWKV_DIM), full(tm, tm),
        ],
        out_specs=[
            pl.BlockSpec((RWKV_HEADS, tm, 64), lambda i: (0, i, 0)),
            pl.BlockSpec((tm, RWKV_DIM), lambda i: (i, 0)),
            pl.BlockSpec((tm, RWKV_DIM), lambda i: (i, 0)),
            hm(), hm(), hm(), hm(), hm(), hm(),
            pl.BlockSpec((2, RWKV_HEADS, tm // 8, 64), lambda i: (0, 0, i, 0)),
        ],
        out_shape=[
            jax.ShapeDtypeStruct((RWKV_HEADS, s, 64), BF16),
            jax.ShapeDtypeStruct((s, RWKV_DIM), BF16),
            jax.ShapeDtypeStruct((s, RWKV_DIM), BF16),
            hm_shape, hm_shape, hm_shape, hm_shape, hm_shape, hm_shape,
            jax.ShapeDtypeStruct((2, RWKV_HEADS, s // 8, 64), F32),
        ],
        compiler_params=_cparams(("parallel",)),
        name="rwkv_prep",
    )(p, p, p, p, p, p, mu_rkv, mu_lora, w0, w2p, a0, a2p, g2p, k_k, k_a, r_k, bd, tri)


def _bdot(a, b):
    return lax.dot_general(a, b, (((2,), (1,)), ((0,), (0,))), preferred_element_type=F32)


def _bdot_nt(a, b):
    return lax.dot_general(a, b, (((2,), (2,)), ((0,), (0,))), preferred_element_type=F32)


def _bdot_tn(a, b):
    return lax.dot_general(a, b, (((1,), (1,)), ((0,), (0,))), preferred_element_type=F32)


def _bdot_inv(a, b):
    return _bdot(a.astype(BF16), b.astype(BF16))


def _wkv_kernel(*refs):
    (vf_ref, vr_ref), ins, (gcf_ref, gcr_ref, yf_ref, yr_ref, s_ref) = refs[:2], refs[2:14], refs[14:]

    @pl.when(pl.program_id(0) == 0)
    def _():
        s_ref[...] = jnp.zeros_like(s_ref)

    _wkv_chunk(False, vf_ref, ins[0:6], gcf_ref, yf_ref, s_ref.at[0])
    _wkv_chunk(True, vr_ref, ins[6:12], gcr_ref, yr_ref, s_ref.at[1])


def _wkv_chunk(reverse, v_ref, scaled_refs, gc_ref, y_ref, s_ref):
    nh = RWKV_HEADS
    row = lax.broadcasted_iota(jnp.int32, (nh, CHUNK, CHUNK), 1)
    col = lax.broadcasted_iota(jnp.int32, (nh, CHUNK, CHUNK), 2)
    strict = (row < col) if reverse else (row > col)
    incl = (row <= col) if reverse else (row >= col)

    v = v_ref[...]
    rb, kb, kt, bt, kh, bh = [r[...] for r in scaled_refs]

    c = CHUNK
    kr = jnp.concatenate([kb, rb], axis=1)
    sk = _bdot_nt(kr, kt)
    sb = _bdot_nt(kr, bt)
    a_kk = jnp.where(strict, sk[:, :c], 0.0)
    a_rk = jnp.where(incl, sk[:, c:], 0.0)
    l_mat = jnp.where(strict, sb[:, :c], 0.0)
    a_rb = jnp.where(incl, sb[:, c:], 0.0)

    eye = jnp.where(row == col, 1.0, 0.0)
    t_inv = eye - l_mat
    pw = _bdot_inv(l_mat, l_mat)
    for step in range(5):
        t_inv = t_inv + _bdot_inv(t_inv, pw)
        if step < 4:
            pw = _bdot_inv(pw, pw)

    q12 = _bdot(jnp.concatenate([a_kk, a_rk], axis=1).astype(BF16), v)
    q1, q2 = q12[:, :c], q12[:, c:]
    t_b = t_inv.astype(BF16)
    wkuv = _bdot(t_b, jnp.concatenate([kb, q1.astype(BF16)], axis=2)).astype(BF16)
    corr = _bdot(a_rb.astype(BF16), wkuv)
    rw = rb.astype(F32) - corr[:, :, :RWKV_HEAD_DIM]
    y0 = q2 - corr[:, :, RWKV_HEAD_DIM:]
    tn = _bdot_tn(wkuv, bh)
    m_mat = tn[:, :RWKV_HEAD_DIM]
    j_t = _bdot_tn(v, kh) - tn[:, RWKV_HEAD_DIM:]

    s_old = s_ref[...]
    s_hi = s_old.astype(BF16)
    s_lo = (s_old - s_hi.astype(F32)).astype(BF16)
    y_ref[...] = _bdot_nt(rw.astype(BF16), s_hi) + y0
    sm = _bdot(jnp.concatenate([s_hi, s_lo], axis=1), m_mat.astype(BF16))
    n = RWKV_HEAD_DIM
    s_ref[...] = s_old * gc_ref[:, 0:1, :] - (sm[:, :n] + sm[:, n:]) + j_t


def _wkv(v, rb, kb, kt, bt, kh, bh, gc):
    s = v.shape[1]
    nc = s // CHUNK
    scaled = (rb, kb, kt, bt, kh, bh)
    fwd = lambda rows: pl.BlockSpec((None, RWKV_HEADS, rows, 64), lambda c: (0, 0, c, 0))
    rev = lambda rows: pl.BlockSpec((None, RWKV_HEADS, rows, 64), lambda c: (1, 0, nc - 1 - c, 0))
    y_shape = jax.ShapeDtypeStruct((RWKV_HEADS, s, 64), F32)
    return pl.pallas_call(
        _wkv_kernel,
        grid=(nc,),
        in_specs=[
            pl.BlockSpec((RWKV_HEADS, CHUNK, 64), lambda c: (0, c, 0)),
            pl.BlockSpec((RWKV_HEADS, CHUNK, 64), lambda c: (0, nc - 1 - c, 0)),
            *[fwd(CHUNK) for _ in scaled], *[rev(CHUNK) for _ in scaled],
            fwd(8), rev(8),
        ],
        out_specs=[
            pl.BlockSpec((RWKV_HEADS, CHUNK, 64), lambda c: (0, c, 0)),
            pl.BlockSpec((RWKV_HEADS, CHUNK, 64), lambda c: (0, nc - 1 - c, 0)),
        ],
        out_shape=[y_shape, y_shape],
        scratch_shapes=[pltpu.VMEM((2, RWKV_HEADS, 64, 64), F32)],
        compiler_params=_cparams(("arbitrary",)),
        name="wkv7_chunked",
    )(v, v, *scaled, *scaled, gc, gc)


def _rwkv_post_kernel(yf_ref, yr_ref, g_ref, bonus_ref, lg_ref, lb_ref, o_ref):
    y = yf_ref[...] + yr_ref[...]
    mean = jnp.mean(y, axis=-1, keepdims=True)
    yc = y - mean
    var = jnp.mean(yc * yc, axis=-1, keepdims=True)
    yn = yc * lax.rsqrt(var + LNX_EPS)
    yt = jnp.concatenate([yn[h] for h in range(RWKV_HEADS)], axis=-1)
    out = (yt * lg_ref[...] + lb_ref[...] + bonus_ref[...].astype(F32)) * g_ref[...].astype(F32)
    o_ref[...] = out.astype(o_ref.dtype)


def _rwkv_post(y_fwd, y_rev, g, bonus, lnx_g, lnx_b, *, tm):
    s = g.shape[0]
    return pl.pallas_call(
        _rwkv_post_kernel,
        grid=(s // tm,),
        in_specs=[
            pl.BlockSpec((RWKV_HEADS, tm, 64), lambda i: (0, i, 0)),
            pl.BlockSpec((RWKV_HEADS, tm, 64), lambda i: (0, i, 0)),
            pl.BlockSpec((tm, RWKV_DIM), lambda i: (i, 0)),
            pl.BlockSpec((tm, RWKV_DIM), lambda i: (i, 0)),
            pl.BlockSpec((1, RWKV_DIM), lambda i: (0, 0)),
            pl.BlockSpec((1, RWKV_DIM), lambda i: (0, 0)),
        ],
        out_specs=pl.BlockSpec((tm, RWKV_DIM), lambda i: (i, 0)),
        out_shape=jax.ShapeDtypeStruct((s, RWKV_DIM), BF16),
        compiler_params=_cparams(("parallel",)),
        name="rwkv_post",
    )(y_fwd, y_rev, g, bonus, lnx_g.reshape(1, RWKV_DIM), lnx_b.reshape(1, RWKV_DIM))


def _mem_attn_kernel(q_ref, kv_ref, gq_ref, gk_ref, o_ref):
    for h in range(MEM_HEADS):
        sl = slice(MEM_HEAD_DIM * h, MEM_HEAD_DIM * (h + 1))
        q = q_ref[:, sl].astype(F32)
        q = q * lax.rsqrt(jnp.mean(q * q, axis=-1, keepdims=True) + NORM_EPS)
        q = q * gq_ref[...] * (MEM_HEAD_DIM ** -0.5)
        km = kv_ref[:, sl].astype(F32)
        km = km * lax.rsqrt(jnp.mean(km * km, axis=-1, keepdims=True) + NORM_EPS)
        km = km * gk_ref[...]
        vm = kv_ref[:, MEM_DIM + MEM_HEAD_DIM * h:MEM_DIM + MEM_HEAD_DIM * (h + 1)]
        s = lax.dot_general(q.astype(BF16), km.astype(BF16), (((1,), (1,)), ((), ())),
                            preferred_element_type=F32)
        s = s - jnp.max(s, axis=-1, keepdims=True)
        e = jnp.exp(s)
        pr = e / jnp.sum(e, axis=-1, keepdims=True)
        o_ref[:, sl] = _dot(pr.astype(BF16), vm).astype(o_ref.dtype)


def _mem_attn(p, kv, gq, gk, *, tm):
    s = p.shape[0]
    assert COL_MEM % MEM_DIM == 0
    return pl.pallas_call(
        _mem_attn_kernel,
        grid=(s // tm,),
        in_specs=[
            pl.BlockSpec((tm, MEM_DIM), lambda i: (i, COL_MEM // MEM_DIM)),
            pl.BlockSpec((N_MEM, 2 * MEM_DIM), lambda i: (0, 0)),
            pl.BlockSpec((1, MEM_HEAD_DIM), lambda i: (0, 0)),
            pl.BlockSpec((1, MEM_HEAD_DIM), lambda i: (0, 0)),
        ],
        out_specs=pl.BlockSpec((tm, MEM_DIM), lambda i: (i, 0)),
        out_shape=jax.ShapeDtypeStruct((s, MEM_DIM), BF16),
        compiler_params=_cparams(("parallel",)),
        name="mem_attn",
    )(p, kv, gq, gk)


def _merge_kernel(o0_ref, o1_ref, o2_ref, g0_ref, g1_ref, g2_ref, w_ref, m_ref):
    def gate(g_ref):
        return 0.5 * jnp.tanh(0.5 * g_ref[...].astype(F32)) + 0.5

    acc = gate(g0_ref) * _dot(o0_ref[...], w_ref[0])
    acc = acc + gate(g1_ref) * _dot(o1_ref[...], w_ref[1])
    acc = acc + gate(g2_ref) * _dot(o2_ref[...], w_ref[2])
    m_ref[...] = acc.astype(m_ref.dtype)


def _merge(o_diff, o_rwkv, o_mem, p, w_branch, layer, *, tm, tn):
    s = p.shape[0]
    assert COL_GATE % tn == 0 and D_MODEL % tn == 0
    gate_blk = COL_GATE // tn
    per = D_MODEL // tn
    o_spec = lambda: pl.BlockSpec((tm, 1024), lambda i, j: (i, 0))
    g_spec = lambda b: pl.BlockSpec((tm, tn), lambda i, j: (i, gate_blk + b * per + j))
    return pl.pallas_call(
        _merge_kernel,
        grid=(s // tm, per),
        in_specs=[
            o_spec(), o_spec(), o_spec(),
            g_spec(0), g_spec(1), g_spec(2),
            pl.BlockSpec((None, N_BRANCH, 1024, tn), lambda i, j: (layer, 0, 0, j)),
        ],
        out_specs=pl.BlockSpec((tm, tn), lambda i, j: (i, j)),
        out_shape=jax.ShapeDtypeStruct((s, D_MODEL), BF16),
        compiler_params=_cparams(("parallel", "arbitrary")),
        name="merge",
    )(o_diff, o_rwkv, o_mem, p, p, p, w_branch)


GLU_TILE = 256


def _ffn_up_glu_kernel(x_ref, xp_ref, xn_ref, g_ref, wg_ref, wv_ref, cw_ref, cb_ref, o_ref,
                       h_ref, halo_ref):
    i = pl.program_id(0)
    nb = pl.num_programs(0)

    def norm(x):
        ms = jnp.mean(x * x, axis=-1, keepdims=True)
        return x * lax.rsqrt(ms + NORM_EPS) * g_ref[...]

    @pl.when(pl.program_id(1) == 0)
    def _():
        h_ref[...] = norm(x_ref[...]).astype(BF16)
        row = lax.broadcasted_iota(jnp.int32, xp_ref.shape, 0)
        before = jnp.where(i > 0, pltpu.roll(norm(xp_ref[...]), 1, 0), 0.0)
        after = jnp.where(i < nb - 1, pltpu.roll(norm(xn_ref[...]), 1, 0), 0.0)
        halo_ref[...] = jnp.where(row == 0, before, jnp.where(row == 1, after, 0.0)).astype(BF16)

    h = h_ref[...]
    halo = halo_ref[...]
    tn = o_ref.shape[1]
    for c0 in range(0, tn, GLU_TILE):
        sl = slice(c0, c0 + GLU_TILE)
        wg = wg_ref[:, sl]
        ug = _dot(h, wg)
        uv = _dot(h, wv_ref[:, sl])
        edge = _dot(halo, wg)
        gp, gn = _shift_rows(ug, edge[0:1], edge[1:2])
        cw = cw_ref[:, sl]
        conv = cw[0:1] * gp + cw[1:2] * ug + cw[2:3] * gn + cb_ref[:, sl]
        half = 0.5 * conv
        o_ref[:, sl] = ((half + half * jnp.tanh(half)) * uv).astype(o_ref.dtype)


def _ffn_up_glu(x, g, w_up, layer, conv_w, conv_b, *, tm, tn):
    s, k = x.shape
    assert tn % GLU_TILE == 0
    nj = D_FF // tn
    hb = tm // 16
    last16 = s // 16 - 1
    return pl.pallas_call(
        _ffn_up_glu_kernel,
        grid=(s // tm, nj),
        in_specs=[
            pl.BlockSpec((tm, k), lambda i, j: (i, 0)),
            pl.BlockSpec((16, k), lambda i, j: (jnp.maximum(i * hb - 1, 0), 0)),
            pl.BlockSpec((16, k), lambda i, j: (jnp.minimum((i + 1) * hb, last16), 0)),
            pl.BlockSpec((1, k), lambda i, j: (0, 0)),
            pl.BlockSpec((None, k, tn), lambda i, j: (layer, 0, j)),
            pl.BlockSpec((None, k, tn), lambda i, j: (layer, 0, nj + j)),
            pl.BlockSpec((3, tn), lambda i, j: (0, j)),
            pl.BlockSpec((1, tn), lambda i, j: (0, j)),
        ],
        out_specs=pl.BlockSpec((tm, tn), lambda i, j: (i, j)),
        out_shape=jax.ShapeDtypeStruct((s, D_FF), BF16),
        scratch_shapes=[pltpu.VMEM((tm, k), BF16), pltpu.VMEM((16, k), BF16)],
        compiler_params=_cparams(("parallel", "arbitrary")),
        name="ffn_up_glu",
    )(x, x, x, g.reshape(1, k), w_up, w_up, conv_w, conv_b.reshape(1, D_FF))


def _block_ones(n, group):
    idx = np.arange(n) // group
    return jnp.asarray(idx[:, None] == idx[None, :], BF16)


def _chunk_tri(n):
    idx = np.arange(n)
    same = (idx[:, None] // CHUNK) == (idx[None, :] // CHUNK)
    return jnp.asarray(same & (idx[:, None] >= idx[None, :]), BF16)


def _pad_rows(w, rows_before, total):
    n = w.shape[-1]
    out = jnp.zeros((total, n), w.dtype)
    return lax.dynamic_update_slice(out, w, (rows_before, 0))


def _pad_w_in(w):
    lora0 = COL_RKV + 3 * RWKV_DIM
    mem0 = lora0 + LORA_COLS
    gate0 = mem0 + MEM_DIM
    pad = jnp.zeros((w.shape[0], LORA_PAD - LORA_COLS), w.dtype)
    parts = [w[:, :lora0], w[:, gate0:], w[:, mem0:gate0], w[:, lora0:mem0], pad]
    return jnp.concatenate(parts, axis=1).astype(BF16)


def kernel(x, mem, attn_norm_g, w_in, diff_qk_g, diff_lambda, diff_subln_g, rwkv_mu, rwkv_w0,
           rwkv_w2, rwkv_a0, rwkv_a2, rwkv_g2, rwkv_k_k, rwkv_k_a, rwkv_r_k, rwkv_lnx_g,
           rwkv_lnx_b, mem_norm_g, w_mem_kv, mem_qk_g, w_branch, w_out, ffn_norm_g, w_ffn_up,
           ffn_conv_w, ffn_conv_b, w_ffn_down):
    b, s, d = x.shape
    assert b == 1 and d == D_MODEL and s % ATT_BQ == 0
    xs = x.reshape(s, d)
    mem2 = mem.reshape(N_MEM, d)
    prep_tm = 256
    bd64 = _block_ones(1024, 64)
    tri = _chunk_tri(prep_tm)
    n_rwkv_main = 3 * RWKV_DIM
    big_tm = 1024 if s % 1024 == 0 else 512
    w_mem_kv_b = w_mem_kv.astype(BF16)
    w_branch_b = w_branch.astype(BF16)
    w_out_b = w_out.astype(BF16)
    w_ffn_up_b = w_ffn_up.astype(BF16)
    w_ffn_down_b = w_ffn_down.astype(BF16)

    for l in range(DEPTH):
        lam_init = 0.8 - 0.6 * math.exp(-0.3 * l)
        p = _rms_mm(xs, attn_norm_g[l], _pad_w_in(w_in[l])[None], 0, tm=big_tm, tn=1536,
                    name="rms_w_in")

        gq = jnp.tile(diff_qk_g[l, 0].reshape(1, 128), (1, DIFF_HEADS))
        gk = jnp.tile(diff_qk_g[l, 1].reshape(1, 128), (1, DIFF_HEADS))
        qc, kc, score_bound = _attn_consts(diff_qk_g[l])
        qa, ka, vt = _diff_prep(p, gq, gk, qc, kc, bd64, tm=256)
        o_diff = lax.cond(
            score_bound <= SCORE_BOUND_MAX,
            functools.partial(_diff_attn, lam_init=lam_init, online=False),
            functools.partial(_diff_attn, lam_init=lam_init, online=True),
            qa, ka, vt, diff_lambda[l], diff_subln_g[l])

        mu = rwkv_mu[l]
        mu_rkv = mu[:, :n_rwkv_main]
        mu_lora = jnp.pad(mu[:, n_rwkv_main:], ((0, 0), (0, LORA_PAD - LORA_COLS)))
        w2p = jnp.stack([_pad_rows(rwkv_w2[l, dd], 64 * dd, 128) for dd in range(2)]).astype(BF16)
        a2p = jnp.stack([_pad_rows(rwkv_a2[l, dd], 64 * dd, 128) for dd in range(2)]).astype(BF16)
        g2p = _pad_rows(rwkv_g2[l], 0, 256).astype(BF16)
        (v_h, g_tok, bonus, rb, kb, kt, bt, kh, bh, gc) = _rwkv_prep(
            p, mu_rkv, mu_lora, rwkv_w0[l], w2p, rwkv_a0[l], a2p, g2p,
            rwkv_k_k[l].reshape(1, RWKV_DIM), rwkv_k_a[l].reshape(1, RWKV_DIM),
            rwkv_r_k[l].reshape(1, RWKV_DIM), bd64, tri, tm=prep_tm)
        y_fwd, y_rev = _wkv(v_h, rb, kb, kt, bt, kh, bh, gc)
        o_rwkv = _rwkv_post(y_fwd, y_rev, g_tok, bonus, rwkv_lnx_g[l], rwkv_lnx_b[l], tm=256)

        kv = _rms_mm(mem2, mem_norm_g[l], w_mem_kv_b, l, tm=N_MEM, tn=1024, name="rms_mem_kv")
        o_mem = _mem_attn(p, kv, mem_qk_g[l, 0].reshape(1, MEM_HEAD_DIM),
                          mem_qk_g[l, 1].reshape(1, MEM_HEAD_DIM), tm=512)

        merged = _merge(o_diff, o_rwkv, o_mem, p, w_branch_b, l, tm=256, tn=D_MODEL)
        xs = _mm_res(merged, w_out_b, l, xs, tm=512, tn=D_MODEL, name="w_out_res")

        act = _ffn_up_glu(xs, ffn_norm_g[l], w_ffn_up_b, l, ffn_conv_w[l], ffn_conv_b[l],
                          tm=big_tm, tn=512)
        xs = _mm_res(act, w_ffn_down_b, l, xs, tm=big_tm, tn=512, name="ffn_down_res")

    return xs.reshape(b, s, d)
```

```python
import functools
import math

import jax
import jax.numpy as jnp
import numpy as np
from jax import lax
from jax.experimental import pallas as pl
from jax.experimental.pallas import tpu as pltpu

F32 = jnp.float32
BF16 = jnp.bfloat16

D_MODEL = 2048
DEPTH = 2
DIFF_HEADS = 8
DIFF_QK_DIM = 64
DIFF_V_DIM = 128
DIFF_DIM = 1024
RWKV_HEADS = 16
RWKV_HEAD_DIM = 64
RWKV_DIM = 1024
DECAY_LORA = 64
AAA_LORA = 64
GATE_LORA = 160
LORA_COLS = 2 * DECAY_LORA + 2 * AAA_LORA + GATE_LORA
LORA_PAD = 512
N_MEM = 256
MEM_HEADS = 4
MEM_HEAD_DIM = 256
MEM_DIM = 1024
N_BRANCH = 3
D_FF = 5632
NORM_EPS = 1e-6
LNX_EPS = 64e-5

COL_DIFF = 0
COL_RKV = 3 * DIFF_DIM
COL_GATE = COL_RKV + 3 * RWKV_DIM
COL_MEM = COL_GATE + N_BRANCH * D_MODEL
COL_LORA = COL_MEM + MEM_DIM
N_IN_PAD = COL_LORA + LORA_PAD

CHUNK = 64
LOG2E = 1.4426950408889634
N_POS_COLS = 12
SCORE_BOUND_MAX = 30.0
ATT_BQ = 1024
ATT_BK = 512
V_AUG = 144
FAR_UNROLL = 6
VMEM_LIMIT = 56 * 1024 * 1024


def _cparams(sem):
    return pltpu.CompilerParams(dimension_semantics=sem, vmem_limit_bytes=VMEM_LIMIT)


def _split2(x):
    hi = x.astype(BF16)
    return hi, (x - hi.astype(F32)).astype(BF16)


def _dot(a, b):
    return jnp.dot(a, b, preferred_element_type=F32)


def _dot_exactish(a_bf16_exact, x_f32):
    hi, lo = _split2(x_f32)
    return _dot(a_bf16_exact, hi) + _dot(a_bf16_exact, lo)


def _rms_mm_kernel(x_ref, g_ref, w_ref, o_ref, h_ref):
    @pl.when(pl.program_id(1) == 0)
    def _():
        x = x_ref[...]
        ms = jnp.mean(x * x, axis=-1, keepdims=True)
        h_ref[...] = (x * lax.rsqrt(ms + NORM_EPS) * g_ref[...]).astype(BF16)

    o_ref[...] = _dot(h_ref[...], w_ref[...]).astype(o_ref.dtype)


def _rms_mm(x, g, w, layer, *, tm, tn, name):
    m, k = x.shape
    n = w.shape[2]
    return pl.pallas_call(
        _rms_mm_kernel,
        grid=(m // tm, n // tn),
        in_specs=[
            pl.BlockSpec((tm, k), lambda i, j: (i, 0)),
            pl.BlockSpec((1, k), lambda i, j: (0, 0)),
            pl.BlockSpec((None, k, tn), lambda i, j: (layer, 0, j)),
        ],
        out_specs=pl.BlockSpec((tm, tn), lambda i, j: (i, j)),
        out_shape=jax.ShapeDtypeStruct((m, n), BF16),
        scratch_shapes=[pltpu.VMEM((tm, k), BF16)],
        compiler_params=_cparams(("parallel", "arbitrary")),
        name=name,
    )(x, g.reshape(1, k), w)


def _mm_res_kernel(a_ref, w_ref, r_ref, o_ref):
    o_ref[...] = r_ref[...] + _dot(a_ref[...], w_ref[...])


def _mm_res(a, w, layer, res, *, tm, tn, name):
    m, k = a.shape
    n = w.shape[2]
    return pl.pallas_call(
        _mm_res_kernel,
        grid=(m // tm, n // tn),
        in_specs=[
            pl.BlockSpec((tm, k), lambda i, j: (i, 0)),
            pl.BlockSpec((None, k, tn), lambda i, j: (layer, 0, j)),
            pl.BlockSpec((tm, tn), lambda i, j: (i, j)),
        ],
        out_specs=pl.BlockSpec((tm, tn), lambda i, j: (i, j)),
        out_shape=jax.ShapeDtypeStruct((m, n), F32),
        compiler_params=_cparams(("parallel", "arbitrary")),
        name=name,
    )(a, w, res)


def _group_mean_sq(x, bd_ref, group):
    return _dot((x * x).astype(BF16), bd_ref[...]) * (1.0 / group)


def _aug_base(mp):
    return 64 if mp == 0 else 0


def _diff_prep_kernel(q_ref, k_ref, v_ref, gq_ref, gk_ref, qc_ref, kc_ref, bd_ref,
                      qa_ref, ka_ref, vt_ref):
    tm = q_ref.shape[0]
    row0 = pl.program_id(0) * tm
    lane = lax.broadcasted_iota(jnp.int32, (tm, 128), 1)
    pos = row0 + lax.broadcasted_iota(jnp.int32, (tm, 128), 0)
    pos_lo = (pos & 127).astype(F32)
    pos_hi = (pos >> 7).astype(F32)

    q = q_ref[...].astype(F32)
    qn = q * lax.rsqrt(_group_mean_sq(q, bd_ref, DIFF_QK_DIM) + NORM_EPS)
    qn = qn * gq_ref[...] * (DIFF_QK_DIM ** -0.5 * LOG2E)
    k = k_ref[...].astype(F32)
    kn = k * lax.rsqrt(_group_mean_sq(k, bd_ref, DIFF_QK_DIM) + NORM_EPS)
    kn = kn * gk_ref[...]

    for h in range(DIFF_HEADS):
        qh = qn[:, 128 * h:128 * (h + 1)]
        kh = kn[:, 128 * h:128 * (h + 1)]
        for mp in range(2):
            a0 = _aug_base(mp)
            g = 2 * h + mp
            is_data = (lane < 64) if mp == 0 else (lane >= 64)
            aug_q = jnp.where(lane < a0 + 3, pos_lo,
                              jnp.where(lane < a0 + 6, pos_hi, qc_ref[g:g + 1, :]))
            aug_k = jnp.where((lane >= a0 + 6) & (lane < a0 + 9), pos_lo,
                              jnp.where((lane >= a0 + 9) & (lane < a0 + 12), pos_hi,
                                        kc_ref[g:g + 1, :]))
            qa_ref[h, mp] = jnp.where(is_data, qh, aug_q).astype(BF16)
            ka_ref[h, mp] = jnp.where(is_data, kh, aug_k).astype(BF16)

    vt = v_ref[...].astype(F32).T
    sub = lax.broadcasted_iota(jnp.int32, (V_AUG - DIFF_V_DIM, tm), 0)
    ones_rows = jnp.where(sub == 0, 1.0, 0.0).astype(BF16)
    for h in range(DIFF_HEADS):
        vt_ref[h, 0:DIFF_V_DIM, :] = vt[128 * h:128 * (h + 1), :].astype(BF16)
        vt_ref[h, DIFF_V_DIM:V_AUG, :] = ones_rows


def _bf16_split3_const(x):
    parts = []
    for _ in range(3):
        part = float(np.asarray(x, np.float32).astype(BF16).astype(np.float32))
        parts.append(part)
        x = x - part
    return parts


def _attn_consts(qk_g):
    l_parts = _bf16_split3_const(LOG2E)
    m_nat = 8.0 * jnp.max(jnp.abs(qk_g[0]), axis=-1) * jnp.max(jnp.abs(qk_g[1]), axis=-1)
    qc = np.zeros((2 * DIFF_HEADS, 128), np.float32)
    kc = np.zeros((2 * DIFF_HEADS, 128), np.float32)
    bound_lane = np.zeros((2, 2 * DIFF_HEADS, 128), np.float32)
    for h in range(DIFF_HEADS):
        slope = 2.0 ** (-(h + 1))
        for mp in range(2):
            a0 = _aug_base(mp)
            g = 2 * h + mp
            for t, lp in enumerate(l_parts):
                qc[g, a0 + 6 + t] = slope * lp
                qc[g, a0 + 9 + t] = 128.0 * slope * lp
                kc[g, a0 + t] = -slope * lp
                kc[g, a0 + 3 + t] = -128.0 * slope * lp
            bound_lane[mp, g, a0 + N_POS_COLS] = 1.0
            kc[g, a0 + N_POS_COLS] = 1.0
    m2 = -m_nat * LOG2E
    qc = qc + m2[0] * bound_lane[0] + m2[1] * bound_lane[1]
    return qc, jnp.asarray(kc), jnp.max(m_nat)


def _diff_prep(p, gq, gk, qc, kc, bd, *, tm):
    s = p.shape[0]
    return pl.pallas_call(
        _diff_prep_kernel,
        grid=(s // tm,),
        in_specs=[
            pl.BlockSpec((tm, DIFF_DIM), lambda i: (i, 0)),
            pl.BlockSpec((tm, DIFF_DIM), lambda i: (i, 1)),
            pl.BlockSpec((tm, DIFF_DIM), lambda i: (i, 2)),
            pl.BlockSpec((1, DIFF_DIM), lambda i: (0, 0)),
            pl.BlockSpec((1, DIFF_DIM), lambda i: (0, 0)),
            pl.BlockSpec((2 * DIFF_HEADS, 128), lambda i: (0, 0)),
            pl.BlockSpec((2 * DIFF_HEADS, 128), lambda i: (0, 0)),
            pl.BlockSpec((DIFF_DIM, DIFF_DIM), lambda i: (0, 0)),
        ],
        out_specs=[
            pl.BlockSpec((DIFF_HEADS, 2, tm, 128), lambda i: (0, 0, i, 0)),
            pl.BlockSpec((DIFF_HEADS, 2, tm, 128), lambda i: (0, 0, i, 0)),
            pl.BlockSpec((DIFF_HEADS, V_AUG, tm), lambda i: (0, 0, i)),
        ],
        out_shape=[
            jax.ShapeDtypeStruct((DIFF_HEADS, 2, s, 128), BF16),
            jax.ShapeDtypeStruct((DIFF_HEADS, 2, s, 128), BF16),
            jax.ShapeDtypeStruct((DIFF_HEADS, V_AUG, s), BF16),
        ],
        compiler_params=_cparams(("parallel",)),
        name="diff_prep",
    )(p, p, p, gq, gk, qc, kc, bd)


def _diff_attn_kernel(lam_ref, sg_ref, dnear_ref, qa_ref, ka_ref, vt_ref, o_ref,
                      acc_ref, qv_ref, *mode_refs, lam_init, online):
    h = pl.program_id(0)
    i = pl.program_id(1)
    bq = qa_ref.shape[1]
    s_len = ka_ref.shape[1]
    bk = ATT_BK
    n_chunks = s_len // bk
    per_q = bq // bk
    n_far = n_chunks - per_q
    j_lo = i * per_q
    q0 = i * bq
    slope2 = jnp.exp2(-jnp.full((1, 1), h + 1, jnp.int32).astype(F32)) * LOG2E
    if online:
        m_ref, sa_ref, sb_ref, mxa_ref, mxb_ref = mode_refs
        m_ref[...] = jnp.full_like(m_ref, -1e30)
        bufs = ((sa_ref, mxa_ref), (sb_ref, mxb_ref))
    else:
        bufs = mode_refs
    acc_ref[...] = jnp.zeros_like(acc_ref)

    lane = lax.broadcasted_iota(jnp.int32, (1, 128), 1)
    for mp in range(2):
        q = qa_ref[mp]
        pos_lane = (lane >= _aug_base(mp)) & (lane < _aug_base(mp) + N_POS_COLS)
        qv_ref[mp, 0] = q
        qv_ref[mp, 1] = jnp.where(pos_lane, -q, q)
        qv_ref[mp, 2] = jnp.where(pos_lane, jnp.zeros_like(q), q)

    def stage_one(k0, variant, bias, buf):
        for mp in range(2):
            kc = ka_ref[mp, pl.ds(k0, bk), :]
            s = lax.dot_general(kc, qv_ref[mp, variant], (((1,), (1,)), ((), ())),
                                preferred_element_type=F32)
            if bias is not None:
                s = s + bias
            if online:
                buf[0][mp] = s
                buf[1][mp] = jnp.max(s, axis=0, keepdims=True)
            else:
                buf[mp] = jnp.exp2(s).astype(BF16)

    def stage_two(k0, buf):
        vt_c = vt_ref[:, pl.ds(k0, bk)]
        for mp in range(2):
            if online:
                m_old = m_ref[mp]
                m_new = jnp.maximum(m_old, buf[1][mp])
                p = jnp.exp2(buf[0][mp] - m_new).astype(BF16)
                acc_ref[mp] = acc_ref[mp] * jnp.exp2(m_old - m_new) + _dot(vt_c, p)
                m_ref[mp] = m_new
            else:
                acc_ref[mp] += _dot(vt_c, buf[mp])

    def near_start(t):
        return pl.multiple_of(q0 + t * bk, bk)

    def far_start(t):
        t = jnp.minimum(t, n_far - 1)
        j = jnp.where(t < j_lo, t, t + per_q)
        return pl.multiple_of(j * bk, bk), (t >= j_lo).astype(jnp.int32)

    def near_one(t):
        stage_one(near_start(t), 2, slope2 * dnear_ref[t], bufs[t % 2])

    def far_one(t, parity):
        k0, after = far_start(t)
        stage_one(k0, after, None, bufs[parity])

    def far_two(t, parity):
        stage_two(far_start(t)[0], bufs[parity])

    near_one(0)
    for t in range(1, per_q):
        near_one(t)
        stage_two(near_start(t - 1), bufs[(t - 1) % 2])
    far_one(0, per_q % 2)
    stage_two(near_start(per_q - 1), bufs[(per_q - 1) % 2])

    def far_body(u, c):
        for r in range(FAR_UNROLL):
            far_one(FAR_UNROLL * u + r + 1, (per_q + r + 1) % 2)
            far_two(FAR_UNROLL * u + r, (per_q + r) % 2)
        return c

    lax.fori_loop(0, n_far // FAR_UNROLL, far_body, 0)
    for t in range(n_far - n_far % FAR_UNROLL, n_far):
        if t + 1 < n_far:
            far_one(t + 1, (per_q + t + 1) % 2)
        far_two(t, (per_q + t) % 2)

    lp = lam_ref[...]
    lam = (jnp.exp(jnp.sum(lp[0:1] * lp[1:2], axis=-1, keepdims=True))
           - jnp.exp(jnp.sum(lp[2:3] * lp[3:4], axis=-1, keepdims=True)) + lam_init)
    a0 = acc_ref[0]
    a1 = acc_ref[1]
    o = (a0[0:DIFF_V_DIM] / a0[DIFF_V_DIM:DIFF_V_DIM + 1]
         - lam * (a1[0:DIFF_V_DIM] / a1[DIFF_V_DIM:DIFF_V_DIM + 1]))
    ms = jnp.mean(o * o, axis=0, keepdims=True)
    o = o * lax.rsqrt(ms + NORM_EPS) * sg_ref[...] * (1.0 - lam_init)
    o_ref[...] = o.T.astype(o_ref.dtype)


def _diff_attn(qa, ka, vt, lam_p, subln_g, *, lam_init, online):
    s = qa.shape[2]
    bq = ATT_BQ
    assert bq % ATT_BK == 0 and s > bq
    kern = functools.partial(_diff_attn_kernel, lam_init=lam_init, online=online)
    per_q = bq // ATT_BK
    key = np.arange(per_q * ATT_BK).reshape(per_q, ATT_BK, 1)
    dnear = jnp.asarray(-np.abs(np.arange(bq).reshape(1, 1, bq) - key), F32)
    scratch = [
        pltpu.VMEM((2, V_AUG, bq), F32),
        pltpu.VMEM((2, 3, bq, 128), BF16),
    ]
    if online:
        scratch += [
            pltpu.VMEM((2, 1, bq), F32),
            pltpu.VMEM((2, ATT_BK, bq), F32),
            pltpu.VMEM((2, ATT_BK, bq), F32),
            pltpu.VMEM((2, 1, bq), F32),
            pltpu.VMEM((2, 1, bq), F32),
        ]
    else:
        scratch += [pltpu.VMEM((2, ATT_BK, bq), BF16), pltpu.VMEM((2, ATT_BK, bq), BF16)]
    return pl.pallas_call(
        kern,
        grid=(DIFF_HEADS, s // bq),
        in_specs=[
            pl.BlockSpec((4, DIFF_QK_DIM), lambda h, i: (0, 0)),
            pl.BlockSpec((DIFF_V_DIM, 1), lambda h, i: (0, 0)),
            pl.BlockSpec((per_q, ATT_BK, bq), lambda h, i: (0, 0, 0)),
            pl.BlockSpec((None, 2, bq, 128), lambda h, i: (h, 0, i, 0)),
            pl.BlockSpec((None, 2, s, 128), lambda h, i: (h, 0, 0, 0)),
            pl.BlockSpec((None, V_AUG, s), lambda h, i: (h, 0, 0)),
        ],
        out_specs=pl.BlockSpec((bq, DIFF_V_DIM), lambda h, i: (i, h)),
        out_shape=jax.ShapeDtypeStruct((s, DIFF_DIM), BF16),
        scratch_shapes=scratch,
        compiler_params=_cparams(("parallel", "arbitrary")),
        name="diff_attn_online" if online else "diff_attn",
    )(lam_p, subln_g.reshape(DIFF_V_DIM, 1), dnear, qa, ka, vt)


def _shift_rows(x, prev_row, next_row):
    n = x.shape[0]
    row = lax.broadcasted_iota(jnp.int32, x.shape, 0)
    xp = jnp.where(row == 0, prev_row, pltpu.roll(x, 1, 0))
    xn = jnp.where(row == n - 1, next_row, pltpu.roll(x, n - 1, 0))
    return xp, xn


def _halo_rows(prev_ref, next_ref, i, n_blocks):
    hp = prev_ref.shape[0]
    prev_row = prev_ref[hp - 1:hp, :].astype(F32)
    next_row = next_ref[0:1, :].astype(F32)
    prev_row = jnp.where(i > 0, prev_row, 0.0)
    next_row = jnp.where(i < n_blocks - 1, next_row, 0.0)
    return prev_row, next_row


def _rwkv_prep_kernel(
        x_ref, xp_ref, xn_ref, l_ref, lp_ref, ln_ref,
        mu_ref, mul_ref, w0_ref, w2_ref, a0_ref, a2_ref, g2_ref, kk_ref, ka_ref, rk_ref,
        bd_ref, tri_ref,
        v_out, g_out, bonus_out, rb_out, kb_out, kt_out, bt_out, kh_out, bh_out, gc_out):
    i = pl.program_id(0)
    nb = pl.num_programs(0)
    tm = x_ref.shape[0]

    x = x_ref[...].astype(F32)
    prev_row, next_row = _halo_rows(xp_ref, xn_ref, i, nb)
    xp, xn = _shift_rows(x, prev_row, next_row)
    mu = mu_ref[...]
    x = x + mu[0:1] * (xp - x) + mu[1:2] * (xn - x)
    lo = l_ref[...].astype(F32)
    prev_row, next_row = _halo_rows(lp_ref, ln_ref, i, nb)
    lop, lon = _shift_rows(lo, prev_row, next_row)
    mul = mul_ref[...]
    lo = lo + mul[0:1] * (lop - lo) + mul[1:2] * (lon - lo)

    r = x[:, 0:RWKV_DIM]
    k = x[:, RWKV_DIM:2 * RWKV_DIM]
    v = x[:, 2 * RWKV_DIM:3 * RWKV_DIM]
    tw = jnp.tanh(lo[:, 0:128]).astype(BF16)
    la = lo[:, 128:256].astype(BF16)
    lg = jax.nn.sigmoid(lo[:, 256:512]).astype(BF16)

    g = _dot(lg, g2_ref[...])
    kk = k * kk_ref[...]
    ss = _group_mean_sq(kk, bd_ref, 1.0)
    kk = kk * lax.rsqrt(jnp.maximum(ss, 1e-24))
    bonus = _dot((r * k * rk_ref[...]).astype(BF16), bd_ref[...]) * v

    g_out[...] = g.astype(g_out.dtype)
    bonus_out[...] = bonus.astype(bonus_out.dtype)
    for h in range(RWKV_HEADS):
        v_out[h] = v[:, 64 * h:64 * (h + 1)].astype(BF16)

    tri = tri_ref[...]
    for d in range(2):
        wl = w0_ref[d:d + 1, :] + _dot(tw, w2_ref[d])
        logdec = -math.exp(-0.5) * jax.nn.sigmoid(wl)
        a = jax.nn.sigmoid(a0_ref[d:d + 1, :] + _dot(la, a2_ref[d]))
        k_d = k * (1.0 + (a - 1.0) * ka_ref[...])
        b_d = kk * a
        pre = _dot_exactish(tri, logdec)
        chunk_tot = jnp.broadcast_to(
            pre.reshape(tm // CHUNK, CHUNK, RWKV_DIM)[:, CHUNK - 1:CHUNK, :],
            (tm // CHUNK, CHUNK, RWKV_DIM)).reshape(tm, RWKV_DIM)
        suf = chunk_tot - pre
        if d == 0:
            lc, ex, rem = pre, pre - logdec, suf
        else:
            lc, ex, rem = suf + logdec, suf, pre - logdec
        e_neg = jnp.exp(-lc)
        e_rem = jnp.exp(rem)
        outs = (
            (rb_out, r * jnp.exp(lc)),
            (kb_out, kk * jnp.exp(ex)),
            (kt_out, k_d * e_neg),
            (bt_out, b_d * e_neg),
            (kh_out, k_d * e_rem),
            (bh_out, b_d * e_rem),
        )
        for ref, val in outs:
            val = val.astype(BF16)
            for h in range(RWKV_HEADS):
                ref[d, h] = val[:, 64 * h:64 * (h + 1)]
        tot = jnp.exp(chunk_tot).reshape(tm // 8, 8, RWKV_DIM)[:, 0, :]
        for h in range(RWKV_HEADS):
            gc_out[d, h] = tot[:, 64 * h:64 * (h + 1)]


def _rwkv_prep(p, mu_rkv, mu_lora, w0, w2p, a0, a2p, g2p, k_k, k_a, r_k, bd, tri, *, tm):
    s = p.shape[0]
    nb = s // tm
    hb = tm // 16
    last16 = s // 16 - 1
    rkv_w = 3 * RWKV_DIM
    assert COL_RKV % rkv_w == 0 and COL_LORA % LORA_PAD == 0
    c_rkv = COL_RKV // rkv_w
    c_lora = COL_LORA // LORA_PAD

    def prev_map(c):
        return lambda i: (jnp.maximum(i * hb - 1, 0), c)

    def next_map(c):
        return lambda i: (jnp.minimum((i + 1) * hb, last16), c)

    full = lambda *shape: pl.BlockSpec(shape, lambda i: (0,) * len(shape))
    hm = lambda: pl.BlockSpec((2, RWKV_HEADS, tm, 64), lambda i: (0, 0, i, 0))
    hm_shape = jax.ShapeDtypeStruct((2, RWKV_HEADS, s, 64), BF16)
    return pl.pallas_call(
        _rwkv_prep_kernel,
        grid=(nb,),
        in_specs=[
            pl.BlockSpec((tm, rkv_w), lambda i: (i, c_rkv)),
            pl.BlockSpec((16, rkv_w), prev_map(c_rkv)),
            pl.BlockSpec((16, rkv_w), next_map(c_rkv)),
            pl.BlockSpec((tm, LORA_PAD), lambda i: (i, c_lora)),
            pl.BlockSpec((16, LORA_PAD), prev_map(c_lora)),
            pl.BlockSpec((16, LORA_PAD), next_map(c_lora)),
            full(2, rkv_w), full(2, LORA_PAD),
            full(2, RWKV_DIM), full(2, 128, RWKV_DIM),
            full(2, RWKV_DIM), full(2, 128, RWKV_DIM),
            full(256, RWKV_DIM),
            full(1, RWKV_DIM), full(1, RWKV_DIM), full(1, RWKV_DIM),
            full(RWKV_DIM, RWKV_DIM), full(tm, tm),
        ],
        out_specs=[
            pl.BlockSpec((RWKV_HEADS, tm, 64), lambda i: (0, i, 0)),
            pl.BlockSpec((tm, RWKV_DIM), lambda i: (i, 0)),
            pl.BlockSpec((tm, RWKV_DIM), lambda i: (i, 0)),
            hm(), hm(), hm(), hm(), hm(), hm(),
            pl.BlockSpec((2, RWKV_HEADS, tm // 8, 64), lambda i: (0, 0, i, 0)),
        ],
        out_shape=[
            jax.ShapeDtypeStruct((RWKV_HEADS, s, 64), BF16),
            jax.ShapeDtypeStruct((s, RWKV_DIM), BF16),
            jax.ShapeDtypeStruct((s, RWKV_DIM), BF16),
            hm_shape, hm_shape, hm_shape, hm_shape, hm_shape, hm_shape,
            jax.ShapeDtypeStruct((2, RWKV_HEADS, s // 8, 64), F32),
        ],
        compiler_params=_cparams(("parallel",)),
        name="rwkv_prep",
    )(p, p, p, p, p, p, mu_rkv, mu_lora, w0, w2p, a0, a2p, g2p, k_k, k_a, r_k, bd, tri)


def _bdot(a, b):
    return lax.dot_general(a, b, (((2,), (1,)), ((0,), (0,))), preferred_element_type=F32)


def _bdot_nt(a, b):
    return lax.dot_general(a, b, (((2,), (2,)), ((0,), (0,))), preferred_element_type=F32)


def _bdot_tn(a, b):
    return lax.dot_general(a, b, (((1,), (1,)), ((0,), (0,))), preferred_element_type=F32)


def _bdot_inv(a, b):
    return _bdot(a.astype(BF16), b.astype(BF16))


WKV_CHUNKS_PER_STEP = 2


def _wkv_kernel(*refs):
    (vf_ref, vr_ref), ins, (gcf_ref, gcr_ref, yf_ref, yr_ref, s_ref) = refs[:2], refs[2:14], refs[14:]

    @pl.when(pl.program_id(0) == 0)
    def _():
        s_ref[...] = jnp.zeros_like(s_ref)

    def rows(ref, part, n):
        return ref.at[:, n * part:n * (part + 1), :]

    for part in range(WKV_CHUNKS_PER_STEP):
        back = WKV_CHUNKS_PER_STEP - 1 - part
        _wkv_chunk(False, rows(vf_ref, part, CHUNK), [rows(r, part, CHUNK) for r in ins[0:6]],
                   rows(gcf_ref, part, 8), rows(yf_ref, part, CHUNK), s_ref.at[0])
        _wkv_chunk(True, rows(vr_ref, back, CHUNK), [rows(r, back, CHUNK) for r in ins[6:12]],
                   rows(gcr_ref, back, 8), rows(yr_ref, back, CHUNK), s_ref.at[1])


def _wkv_chunk(reverse, v_ref, scaled_refs, gc_ref, y_ref, s_ref):
    nh = RWKV_HEADS
    row = lax.broadcasted_iota(jnp.int32, (nh, CHUNK, CHUNK), 1)
    col = lax.broadcasted_iota(jnp.int32, (nh, CHUNK, CHUNK), 2)
    strict = (row < col) if reverse else (row > col)
    incl = (row <= col) if reverse else (row >= col)

    v = v_ref[...]
    rb, kb, kt, bt, kh, bh = [r[...] for r in scaled_refs]

    c = CHUNK
    kr = jnp.concatenate([kb, rb], axis=1)
    sk = _bdot_nt(kr, kt)
    sb = _bdot_nt(kr, bt)
    a_kk = jnp.where(strict, sk[:, :c], 0.0)
    a_rk = jnp.where(incl, sk[:, c:], 0.0)
    l_mat = jnp.where(strict, sb[:, :c], 0.0)
    a_rb = jnp.where(incl, sb[:, c:], 0.0)

    eye = jnp.where(row == col, 1.0, 0.0)
    t_inv = eye - l_mat
    pw = _bdot_inv(l_mat, l_mat)
    for step in range(5):
        if step < 4:
            both = _bdot_inv(jnp.concatenate([t_inv, pw], axis=1), pw)
            t_inv = t_inv + both[:, :c]
            pw = both[:, c:]
        else:
            t_inv = t_inv + _bdot_inv(t_inv, pw)

    q12 = _bdot(jnp.concatenate([a_kk, a_rk], axis=1).astype(BF16), v)
    q1, q2 = q12[:, :c], q12[:, c:]
    t_b = t_inv.astype(BF16)
    wkuv = _bdot(t_b, jnp.concatenate([kb, q1.astype(BF16)], axis=2)).astype(BF16)
    corr = _bdot(a_rb.astype(BF16), wkuv)
    rw = rb.astype(F32) - corr[:, :, :RWKV_HEAD_DIM]
    y0 = q2 - corr[:, :, RWKV_HEAD_DIM:]
    tn = _bdot_tn(wkuv, bh)
    m_mat = tn[:, :RWKV_HEAD_DIM]
    j_t = _bdot_tn(v, kh) - tn[:, RWKV_HEAD_DIM:]

    s_old = s_ref[...]
    s_hi = s_old.astype(BF16)
    s_lo = (s_old - s_hi.astype(F32)).astype(BF16)
    y_ref[...] = _bdot_nt(rw.astype(BF16), s_hi) + y0
    sm = _bdot(jnp.concatenate([s_hi, s_lo], axis=1), m_mat.astype(BF16))
    n = RWKV_HEAD_DIM
    s_ref[...] = s_old * gc_ref[:, 0:1, :] - (sm[:, :n] + sm[:, n:]) + j_t


def _wkv(v, rb, kb, kt, bt, kh, bh, gc):
    s = v.shape[1]
    per = WKV_CHUNKS_PER_STEP
    assert s % (per * CHUNK) == 0
    nc = s // (per * CHUNK)
    scaled = (rb, kb, kt, bt, kh, bh)
    fwd = lambda rows: pl.BlockSpec((None, RWKV_HEADS, rows, 64), lambda c: (0, 0, c, 0))
    rev = lambda rows: pl.BlockSpec((None, RWKV_HEADS, rows, 64), lambda c: (1, 0, nc - 1 - c, 0))
    y_shape = jax.ShapeDtypeStruct((RWKV_HEADS, s, 64), F32)
    return pl.pallas_call(
        _wkv_kernel,
        grid=(nc,),
        in_specs=[
            pl.BlockSpec((RWKV_HEADS, per * CHUNK, 64), lambda c: (0, c, 0)),
            pl.BlockSpec((RWKV_HEADS, per * CHUNK, 64), lambda c: (0, nc - 1 - c, 0)),
            *[fwd(per * CHUNK) for _ in scaled], *[rev(per * CHUNK) for _ in scaled],
            fwd(per * 8), rev(per * 8),
        ],
        out_specs=[
            pl.BlockSpec((RWKV_HEADS, per * CHUNK, 64), lambda c: (0, c, 0)),
            pl.BlockSpec((RWKV_HEADS, per * CHUNK, 64), lambda c: (0, nc - 1 - c, 0)),
        ],
        out_shape=[y_shape, y_shape],
        scratch_shapes=[pltpu.VMEM((2, RWKV_HEADS, 64, 64), F32)],
        compiler_params=_cparams(("arbitrary",)),
        name="wkv7_chunked",
    )(v, v, *scaled, *scaled, gc, gc)


def _rwkv_post_kernel(yf_ref, yr_ref, g_ref, bonus_ref, lg_ref, lb_ref, o_ref):
    y = yf_ref[...] + yr_ref[...]
    mean = jnp.mean(y, axis=-1, keepdims=True)
    yc = y - mean
    var = jnp.mean(yc * yc, axis=-1, keepdims=True)
    yn = yc * lax.rsqrt(var + LNX_EPS)
    yt = jnp.concatenate([yn[h] for h in range(RWKV_HEADS)], axis=-1)
    out = (yt * lg_ref[...] + lb_ref[...] + bonus_ref[...].astype(F32)) * g_ref[...].astype(F32)
    o_ref[...] = out.astype(o_ref.dtype)


def _rwkv_post(y_fwd, y_rev, g, bonus, lnx_g, lnx_b, *, tm):
    s = g.shape[0]
    return pl.pallas_call(
        _rwkv_post_kernel,
        grid=(s // tm,),
        in_specs=[
            pl.BlockSpec((RWKV_HEADS, tm, 64), lambda i: (0, i, 0)),
            pl.BlockSpec((RWKV_HEADS, tm, 64), lambda i: (0, i, 0)),
            pl.BlockSpec((tm, RWKV_DIM), lambda i: (i, 0)),
            pl.BlockSpec((tm, RWKV_DIM), lambda i: (i, 0)),
            pl.BlockSpec((1, RWKV_DIM), lambda i: (0, 0)),
            pl.BlockSpec((1, RWKV_DIM), lambda i: (0, 0)),
        ],
        out_specs=pl.BlockSpec((tm, RWKV_DIM), lambda i: (i, 0)),
        out_shape=jax.ShapeDtypeStruct((s, RWKV_DIM), BF16),
        compiler_params=_cparams(("parallel",)),
        name="rwkv_post",
    )(y_fwd, y_rev, g, bonus, lnx_g.reshape(1, RWKV_DIM), lnx_b.reshape(1, RWKV_DIM))


def _mem_attn_kernel(q_ref, kv_ref, gq_ref, gk_ref, o_ref):
    for h in range(MEM_HEADS):
        sl = slice(MEM_HEAD_DIM * h, MEM_HEAD_DIM * (h + 1))
        q = q_ref[:, sl].astype(F32)
        q = q * lax.rsqrt(jnp.mean(q * q, axis=-1, keepdims=True) + NORM_EPS)
        q = q * gq_ref[...] * (MEM_HEAD_DIM ** -0.5)
        km = kv_ref[:, sl].astype(F32)
        km = km * lax.rsqrt(jnp.mean(km * km, axis=-1, keepdims=True) + NORM_EPS)
        km = km * gk_ref[...]
        vm = kv_ref[:, MEM_DIM + MEM_HEAD_DIM * h:MEM_DIM + MEM_HEAD_DIM * (h + 1)]
        s = lax.dot_general(q.astype(BF16), km.astype(BF16), (((1,), (1,)), ((), ())),
                            preferred_element_type=F32)
        s = s - jnp.max(s, axis=-1, keepdims=True)
        e = jnp.exp(s)
        pr = e / jnp.sum(e, axis=-1, keepdims=True)
        o_ref[:, sl] = _dot(pr.astype(BF16), vm).astype(o_ref.dtype)


def _mem_attn(p, kv, gq, gk, *, tm):
    s = p.shape[0]
    assert COL_MEM % MEM_DIM == 0
    return pl.pallas_call(
        _mem_attn_kernel,
        grid=(s // tm,),
        in_specs=[
            pl.BlockSpec((tm, MEM_DIM), lambda i: (i, COL_MEM // MEM_DIM)),
            pl.BlockSpec((N_MEM, 2 * MEM_DIM), lambda i: (0, 0)),
            pl.BlockSpec((1, MEM_HEAD_DIM), lambda i: (0, 0)),
            pl.BlockSpec((1, MEM_HEAD_DIM), lambda i: (0, 0)),
        ],
        out_specs=pl.BlockSpec((tm, MEM_DIM), lambda i: (i, 0)),
        out_shape=jax.ShapeDtypeStruct((s, MEM_DIM), BF16),
        compiler_params=_cparams(("parallel",)),
        name="mem_attn",
    )(p, kv, gq, gk)


def _merge_kernel(o0_ref, o1_ref, o2_ref, g0_ref, g1_ref, g2_ref, w_ref, m_ref):
    def gate(g_ref):
        return 0.5 * jnp.tanh(0.5 * g_ref[...].astype(F32)) + 0.5

    acc = gate(g0_ref) * _dot(o0_ref[...], w_ref[0])
    acc = acc + gate(g1_ref) * _dot(o1_ref[...], w_ref[1])
    acc = acc + gate(g2_ref) * _dot(o2_ref[...], w_ref[2])
    m_ref[...] = acc.astype(m_ref.dtype)


def _merge(o_diff, o_rwkv, o_mem, p, w_branch, layer, *, tm, tn):
    s = p.shape[0]
    assert COL_GATE % tn == 0 and D_MODEL % tn == 0
    gate_blk = COL_GATE // tn
    per = D_MODEL // tn
    o_spec = lambda: pl.BlockSpec((tm, 1024), lambda i, j: (i, 0))
    g_spec = lambda b: pl.BlockSpec((tm, tn), lambda i, j: (i, gate_blk + b * per + j))
    return pl.pallas_call(
        _merge_kernel,
        grid=(s // tm, per),
        in_specs=[
            o_spec(), o_spec(), o_spec(),
            g_spec(0), g_spec(1), g_spec(2),
            pl.BlockSpec((None, N_BRANCH, 1024, tn), lambda i, j: (layer, 0, 0, j)),
        ],
        out_specs=pl.BlockSpec((tm, tn), lambda i, j: (i, j)),
        out_shape=jax.ShapeDtypeStruct((s, D_MODEL), BF16),
        compiler_params=_cparams(("parallel", "arbitrary")),
        name="merge",
    )(o_diff, o_rwkv, o_mem, p, p, p, w_branch)


GLU_TILE = 256


def _ffn_up_glu_kernel(x_ref, xp_ref, xn_ref, g_ref, wg_ref, wv_ref, cw_ref, cb_ref, o_ref,
                       h_ref, halo_ref):
    i = pl.program_id(0)
    nb = pl.num_programs(0)

    def norm(x):
        ms = jnp.mean(x * x, axis=-1, keepdims=True)
        return x * lax.rsqrt(ms + NORM_EPS) * g_ref[...]

    @pl.when(pl.program_id(1) == 0)
    def _():
        h_ref[...] = norm(x_ref[...]).astype(BF16)
        row = lax.broadcasted_iota(jnp.int32, xp_ref.shape, 0)
        before = jnp.where(i > 0, pltpu.roll(norm(xp_ref[...]), 1, 0), 0.0)
        after = jnp.where(i < nb - 1, pltpu.roll(norm(xn_ref[...]), 1, 0), 0.0)
        halo_ref[...] = jnp.where(row == 0, before, jnp.where(row == 1, after, 0.0)).astype(BF16)

    h = h_ref[...]
    halo = halo_ref[...]
    tn = o_ref.shape[1]
    for c0 in range(0, tn, GLU_TILE):
        sl = slice(c0, c0 + GLU_TILE)
        wg = wg_ref[:, sl]
        ug = _dot(h, wg)
        uv = _dot(h, wv_ref[:, sl])
        edge = _dot(halo, wg)
        gp, gn = _shift_rows(ug, edge[0:1], edge[1:2])
        cw = cw_ref[:, sl]
        conv = cw[0:1] * gp + cw[1:2] * ug + cw[2:3] * gn + cb_ref[:, sl]
        half = 0.5 * conv
        o_ref[:, sl] = ((half + half * jnp.tanh(half)) * uv).astype(o_ref.dtype)


def _ffn_up_glu(x, g, w_up, layer, conv_w, conv_b, *, tm, tn):
    s, k = x.shape
    assert tn % GLU_TILE == 0
    nj = D_FF // tn
    hb = tm // 16
    last16 = s // 16 - 1
    return pl.pallas_call(
        _ffn_up_glu_kernel,
        grid=(s // tm, nj),
        in_specs=[
            pl.BlockSpec((tm, k), lambda i, j: (i, 0)),
            pl.BlockSpec((16, k), lambda i, j: (jnp.maximum(i * hb - 1, 0), 0)),
            pl.BlockSpec((16, k), lambda i, j: (jnp.minimum((i + 1) * hb, last16), 0)),
            pl.BlockSpec((1, k), lambda i, j: (0, 0)),
            pl.BlockSpec((None, k, tn), lambda i, j: (layer, 0, j)),
            pl.BlockSpec((None, k, tn), lambda i, j: (layer, 0, nj + j)),
            pl.BlockSpec((3, tn), lambda i, j: (0, j)),
            pl.BlockSpec((1, tn), lambda i, j: (0, j)),
        ],
        out_specs=pl.BlockSpec((tm, tn), lambda i, j: (i, j)),
        out_shape=jax.ShapeDtypeStruct((s, D_FF), BF16),
        scratch_shapes=[pltpu.VMEM((tm, k), BF16), pltpu.VMEM((16, k), BF16)],
        compiler_params=_cparams(("parallel", "arbitrary")),
        name="ffn_up_glu",
    )(x, x, x, g.reshape(1, k), w_up, w_up, conv_w, conv_b.reshape(1, D_FF))


def _block_ones(n, group):
    idx = np.arange(n) // group
    return jnp.asarray(idx[:, None] == idx[None, :], BF16)


def _chunk_tri(n):
    idx = np.arange(n)
    same = (idx[:, None] // CHUNK) == (idx[None, :] // CHUNK)
    return jnp.asarray(same & (idx[:, None] >= idx[None, :]), BF16)


def _pad_rows(w, rows_before, total):
    n = w.shape[-1]
    out = jnp.zeros((total, n), w.dtype)
    return lax.dynamic_update_slice(out, w, (rows_before, 0))


def _pad_w_in(w):
    lora0 = COL_RKV + 3 * RWKV_DIM
    mem0 = lora0 + LORA_COLS
    gate0 = mem0 + MEM_DIM
    pad = jnp.zeros((w.shape[0], LORA_PAD - LORA_COLS), w.dtype)
    parts = [w[:, :lora0], w[:, gate0:], w[:, mem0:gate0], w[:, lora0:mem0], pad]
    return jnp.concatenate(parts, axis=1).astype(BF16)


def kernel(x, mem, attn_norm_g, w_in, diff_qk_g, diff_lambda, diff_subln_g, rwkv_mu, rwkv_w0,
           rwkv_w2, rwkv_a0, rwkv_a2, rwkv_g2, rwkv_k_k, rwkv_k_a, rwkv_r_k, rwkv_lnx_g,
           rwkv_lnx_b, mem_norm_g, w_mem_kv, mem_qk_g, w_branch, w_out, ffn_norm_g, w_ffn_up,
           ffn_conv_w, ffn_conv_b, w_ffn_down):
    b, s, d = x.shape
    assert b == 1 and d == D_MODEL and s % ATT_BQ == 0
    xs = x.reshape(s, d)
    mem2 = mem.reshape(N_MEM, d)
    prep_tm = 256
    bd64 = _block_ones(1024, 64)
    tri = _chunk_tri(prep_tm)
    n_rwkv_main = 3 * RWKV_DIM
    big_tm = 1024 if s % 1024 == 0 else 512
    w_mem_kv_b = w_mem_kv.astype(BF16)
    w_branch_b = w_branch.astype(BF16)
    w_out_b = w_out.astype(BF16)
    w_ffn_up_b = w_ffn_up.astype(BF16)
    w_ffn_down_b = w_ffn_down.astype(BF16)

    for l in range(DEPTH):
        lam_init = 0.8 - 0.6 * math.exp(-0.3 * l)
        p = _rms_mm(xs, attn_norm_g[l], _pad_w_in(w_in[l])[None], 0, tm=big_tm, tn=1536,
                    name="rms_w_in")

        gq = jnp.tile(diff_qk_g[l, 0].reshape(1, 128), (1, DIFF_HEADS))
        gk = jnp.tile(diff_qk_g[l, 1].reshape(1, 128), (1, DIFF_HEADS))
        qc, kc, score_bound = _attn_consts(diff_qk_g[l])
        qa, ka, vt = _diff_prep(p, gq, gk, qc, kc, bd64, tm=256)
        o_diff = lax.cond(
            score_bound <= SCORE_BOUND_MAX,
            functools.partial(_diff_attn, lam_init=lam_init, online=False),
            functools.partial(_diff_attn, lam_init=lam_init, online=True),
            qa, ka, vt, diff_lambda[l], diff_subln_g[l])

        mu = rwkv_mu[l]
        mu_rkv = mu[:, :n_rwkv_main]
        mu_lora = jnp.pad(mu[:, n_rwkv_main:], ((0, 0), (0, LORA_PAD - LORA_COLS)))
        w2p = jnp.stack([_pad_rows(rwkv_w2[l, dd], 64 * dd, 128) for dd in range(2)]).astype(BF16)
        a2p = jnp.stack([_pad_rows(rwkv_a2[l, dd], 64 * dd, 128) for dd in range(2)]).astype(BF16)
        g2p = _pad_rows(rwkv_g2[l], 0, 256).astype(BF16)
        (v_h, g_tok, bonus, rb, kb, kt, bt, kh, bh, gc) = _rwkv_prep(
            p, mu_rkv, mu_lora, rwkv_w0[l], w2p, rwkv_a0[l], a2p, g2p,
            rwkv_k_k[l].reshape(1, RWKV_DIM), rwkv_k_a[l].reshape(1, RWKV_DIM),
            rwkv_r_k[l].reshape(1, RWKV_DIM), bd64, tri, tm=prep_tm)
        y_fwd, y_rev = _wkv(v_h, rb, kb, kt, bt, kh, bh, gc)
        o_rwkv = _rwkv_post(y_fwd, y_rev, g_tok, bonus, rwkv_lnx_g[l], rwkv_lnx_b[l], tm=256)

        kv = _rms_mm(mem2, mem_norm_g[l], w_mem_kv_b, l, tm=N_MEM, tn=1024, name="rms_mem_kv")
        o_mem = _mem_attn(p, kv, mem_qk_g[l, 0].reshape(1, MEM_HEAD_DIM),
                          mem_qk_g[l, 1].reshape(1, MEM_HEAD_DIM), tm=512)

        merged = _merge(o_diff, o_rwkv, o_mem, p, w_branch_b, l, tm=256, tn=D_MODEL)
        xs = _mm_res(merged, w_out_b, l, xs, tm=512, tn=D_MODEL, name="w_out_res")

        act = _ffn_up_glu(xs, ffn_norm_g[l], w_ffn_up_b, l, ffn_conv_w[l], ffn_conv_b[l],
                          tm=big_tm, tn=512)
        xs = _mm_res(act, w_ffn_down_b, l, xs, tm=big_tm, tn=512, name="ffn_down_res")

    return xs.reshape(b, s, d)
```

```python
import functools
import math

import jax
import jax.numpy as jnp
import numpy as np
from jax import lax
from jax.experimental import pallas as pl
from jax.experimental.pallas import tpu as pltpu

F32 = jnp.float32
BF16 = jnp.bfloat16

D_MODEL = 2048
DEPTH = 2
DIFF_HEADS = 8
DIFF_QK_DIM = 64
DIFF_V_DIM = 128
DIFF_DIM = 1024
RWKV_HEADS = 16
RWKV_HEAD_DIM = 64
RWKV_DIM = 1024
DECAY_LORA = 64
AAA_LORA = 64
GATE_LORA = 160
LORA_COLS = 2 * DECAY_LORA + 2 * AAA_LORA + GATE_LORA
LORA_PAD = 512
N_MEM = 256
MEM_HEADS = 4
MEM_HEAD_DIM = 256
MEM_DIM = 1024
N_BRANCH = 3
D_FF = 5632
NORM_EPS = 1e-6
LNX_EPS = 64e-5

COL_DIFF = 0
COL_RKV = 3 * DIFF_DIM
COL_GATE = COL_RKV + 3 * RWKV_DIM
COL_MEM = COL_GATE + N_BRANCH * D_MODEL
COL_LORA = COL_MEM + MEM_DIM
N_IN_PAD = COL_LORA + LORA_PAD

CHUNK = 64
LOG2E = 1.4426950408889634
N_POS_COLS = 12
SCORE_BOUND_MAX = 30.0
ATT_BQ = 1024
ATT_BK = 512
V_AUG = 144
FAR_UNROLL = 6
VMEM_LIMIT = 56 * 1024 * 1024


def _cparams(sem):
    return pltpu.CompilerParams(dimension_semantics=sem, vmem_limit_bytes=VMEM_LIMIT)


def _split2(x):
    hi = x.astype(BF16)
    return hi, (x - hi.astype(F32)).astype(BF16)


def _dot(a, b):
    return jnp.dot(a, b, preferred_element_type=F32)


def _dot_exactish(a_bf16_exact, x_f32):
    hi, lo = _split2(x_f32)
    return _dot(a_bf16_exact, hi) + _dot(a_bf16_exact, lo)


def _rms_mm_kernel(x_ref, g_ref, w_ref, o_ref, h_ref):
    @pl.when(pl.program_id(1) == 0)
    def _():
        x = x_ref[...]
        ms = jnp.mean(x * x, axis=-1, keepdims=True)
        h_ref[...] = (x * lax.rsqrt(ms + NORM_EPS) * g_ref[...]).astype(BF16)

    o_ref[...] = _dot(h_ref[...], w_ref[...]).astype(o_ref.dtype)


def _rms_mm(x, g, w, layer, *, tm, tn, name):
    m, k = x.shape
    n = w.shape[2]
    return pl.pallas_call(
        _rms_mm_kernel,
        grid=(m // tm, n // tn),
        in_specs=[
            pl.BlockSpec((tm, k), lambda i, j: (i, 0)),
            pl.BlockSpec((1, k), lambda i, j: (0, 0)),
            pl.BlockSpec((None, k, tn), lambda i, j: (layer, 0, j)),
        ],
        out_specs=pl.BlockSpec((tm, tn), lambda i, j: (i, j)),
        out_shape=jax.ShapeDtypeStruct((m, n), BF16),
        scratch_shapes=[pltpu.VMEM((tm, k), BF16)],
        compiler_params=_cparams(("parallel", "arbitrary")),
        name=name,
    )(x, g.reshape(1, k), w)


def _mm_res_kernel(a_ref, w_ref, r_ref, o_ref):
    o_ref[...] = r_ref[...] + _dot(a_ref[...], w_ref[...])


def _mm_res(a, w, layer, res, *, tm, tn, name):
    m, k = a.shape
    n = w.shape[2]
    return pl.pallas_call(
        _mm_res_kernel,
        grid=(m // tm, n // tn),
        in_specs=[
            pl.BlockSpec((tm, k), lambda i, j: (i, 0)),
            pl.BlockSpec((None, k, tn), lambda i, j: (layer, 0, j)),
            pl.BlockSpec((tm, tn), lambda i, j: (i, j)),
        ],
        out_specs=pl.BlockSpec((tm, tn), lambda i, j: (i, j)),
        out_shape=jax.ShapeDtypeStruct((m, n), F32),
        compiler_params=_cparams(("parallel", "arbitrary")),
        name=name,
    )(a, w, res)


GROUP_SLAB = 256


def _group_sum(xb, bd_ref):
    bd = bd_ref[...]
    n = xb.shape[1]
    slabs = [_dot(xb[:, q:q + GROUP_SLAB], bd) for q in range(0, n, GROUP_SLAB)]
    return jnp.concatenate(slabs, axis=1)


def _group_mean_sq(x, bd_ref, group):
    return _group_sum((x * x).astype(BF16), bd_ref) * (1.0 / group)


def _aug_base(mp):
    return 64 if mp == 0 else 0


def _diff_prep_kernel(q_ref, k_ref, v_ref, gq_ref, gk_ref, qc_ref, kc_ref, bd_ref,
                      qa_ref, ka_ref, vt_ref):
    tm = q_ref.shape[0]
    row0 = pl.program_id(0) * tm
    lane = lax.broadcasted_iota(jnp.int32, (tm, 128), 1)
    pos = row0 + lax.broadcasted_iota(jnp.int32, (tm, 128), 0)
    pos_lo = (pos & 127).astype(F32)
    pos_hi = (pos >> 7).astype(F32)

    q = q_ref[...].astype(F32)
    qn = q * lax.rsqrt(_group_mean_sq(q, bd_ref, DIFF_QK_DIM) + NORM_EPS)
    qn = qn * gq_ref[...] * (DIFF_QK_DIM ** -0.5 * LOG2E)
    k = k_ref[...].astype(F32)
    kn = k * lax.rsqrt(_group_mean_sq(k, bd_ref, DIFF_QK_DIM) + NORM_EPS)
    kn = kn * gk_ref[...]

    for h in range(DIFF_HEADS):
        qh = qn[:, 128 * h:128 * (h + 1)]
        kh = kn[:, 128 * h:128 * (h + 1)]
        for mp in range(2):
            a0 = _aug_base(mp)
            g = 2 * h + mp
            is_data = (lane < 64) if mp == 0 else (lane >= 64)
            aug_q = jnp.where(lane < a0 + 3, pos_lo,
                              jnp.where(lane < a0 + 6, pos_hi, qc_ref[g:g + 1, :]))
            aug_k = jnp.where((lane >= a0 + 6) & (lane < a0 + 9), pos_lo,
                              jnp.where((lane >= a0 + 9) & (lane < a0 + 12), pos_hi,
                                        kc_ref[g:g + 1, :]))
            qa_ref[h, mp] = jnp.where(is_data, qh, aug_q).astype(BF16)
            ka_ref[h, mp] = jnp.where(is_data, kh, aug_k).astype(BF16)

    vt = v_ref[...].astype(F32).T
    sub = lax.broadcasted_iota(jnp.int32, (V_AUG - DIFF_V_DIM, tm), 0)
    ones_rows = jnp.where(sub == 0, 1.0, 0.0).astype(BF16)
    for h in range(DIFF_HEADS):
        vt_ref[h, 0:DIFF_V_DIM, :] = vt[128 * h:128 * (h + 1), :].astype(BF16)
        vt_ref[h, DIFF_V_DIM:V_AUG, :] = ones_rows


def _bf16_split3_const(x):
    parts = []
    for _ in range(3):
        part = float(np.asarray(x, np.float32).astype(BF16).astype(np.float32))
        parts.append(part)
        x = x - part
    return parts


def _attn_consts(qk_g):
    l_parts = _bf16_split3_const(LOG2E)
    m_nat = 8.0 * jnp.max(jnp.abs(qk_g[0]), axis=-1) * jnp.max(jnp.abs(qk_g[1]), axis=-1)
    qc = np.zeros((2 * DIFF_HEADS, 128), np.float32)
    kc = np.zeros((2 * DIFF_HEADS, 128), np.float32)
    bound_lane = np.zeros((2, 2 * DIFF_HEADS, 128), np.float32)
    for h in range(DIFF_HEADS):
        slope = 2.0 ** (-(h + 1))
        for mp in range(2):
            a0 = _aug_base(mp)
            g = 2 * h + mp
            for t, lp in enumerate(l_parts):
                qc[g, a0 + 6 + t] = slope * lp
                qc[g, a0 + 9 + t] = 128.0 * slope * lp
                kc[g, a0 + t] = -slope * lp
                kc[g, a0 + 3 + t] = -128.0 * slope * lp
            bound_lane[mp, g, a0 + N_POS_COLS] = 1.0
            kc[g, a0 + N_POS_COLS] = 1.0
    m2 = -m_nat * LOG2E
    qc = qc + m2[0] * bound_lane[0] + m2[1] * bound_lane[1]
    return qc, jnp.asarray(kc), jnp.max(m_nat)


def _diff_prep(p, gq, gk, qc, kc, bd, *, tm):
    s = p.shape[0]
    return pl.pallas_call(
        _diff_prep_kernel,
        grid=(s // tm,),
        in_specs=[
            pl.BlockSpec((tm, DIFF_DIM), lambda i: (i, 0)),
            pl.BlockSpec((tm, DIFF_DIM), lambda i: (i, 1)),
            pl.BlockSpec((tm, DIFF_DIM), lambda i: (i, 2)),
            pl.BlockSpec((1, DIFF_DIM), lambda i: (0, 0)),
            pl.BlockSpec((1, DIFF_DIM), lambda i: (0, 0)),
            pl.BlockSpec((2 * DIFF_HEADS, 128), lambda i: (0, 0)),
            pl.BlockSpec((2 * DIFF_HEADS, 128), lambda i: (0, 0)),
            pl.BlockSpec((GROUP_SLAB, GROUP_SLAB), lambda i: (0, 0)),
        ],
        out_specs=[
            pl.BlockSpec((DIFF_HEADS, 2, tm, 128), lambda i: (0, 0, i, 0)),
            pl.BlockSpec((DIFF_HEADS, 2, tm, 128), lambda i: (0, 0, i, 0)),
            pl.BlockSpec((DIFF_HEADS, V_AUG, tm), lambda i: (0, 0, i)),
        ],
        out_shape=[
            jax.ShapeDtypeStruct((DIFF_HEADS, 2, s, 128), BF16),
            jax.ShapeDtypeStruct((DIFF_HEADS, 2, s, 128), BF16),
            jax.ShapeDtypeStruct((DIFF_HEADS, V_AUG, s), BF16),
        ],
        compiler_params=_cparams(("parallel",)),
        name="diff_prep",
    )(p, p, p, gq, gk, qc, kc, bd)


def _diff_attn_kernel(lam_ref, sg_ref, dnear_ref, qa_ref, ka_ref, vt_ref, o_ref,
                      acc_ref, qv_ref, *mode_refs, lam_init, online):
    h = pl.program_id(0)
    i = pl.program_id(1)
    bq = qa_ref.shape[1]
    s_len = ka_ref.shape[1]
    bk = ATT_BK
    n_chunks = s_len // bk
    per_q = bq // bk
    n_far = n_chunks - per_q
    j_lo = i * per_q
    q0 = i * bq
    slope2 = jnp.exp2(-jnp.full((1, 1), h + 1, jnp.int32).astype(F32)) * LOG2E
    if online:
        m_ref, sa_ref, sb_ref, mxa_ref, mxb_ref = mode_refs
        m_ref[...] = jnp.full_like(m_ref, -1e30)
        bufs = ((sa_ref, mxa_ref), (sb_ref, mxb_ref))
    else:
        bufs = mode_refs
    acc_ref[...] = jnp.zeros_like(acc_ref)

    lane = lax.broadcasted_iota(jnp.int32, (1, 128), 1)
    for mp in range(2):
        q = qa_ref[mp]
        pos_lane = (lane >= _aug_base(mp)) & (lane < _aug_base(mp) + N_POS_COLS)
        qv_ref[mp, 0] = q
        qv_ref[mp, 1] = jnp.where(pos_lane, -q, q)
        qv_ref[mp, 2] = jnp.where(pos_lane, jnp.zeros_like(q), q)

    def stage_one(k0, variant, bias, buf):
        for mp in range(2):
            kc = ka_ref[mp, pl.ds(k0, bk), :]
            s = lax.dot_general(kc, qv_ref[mp, variant], (((1,), (1,)), ((), ())),
                                preferred_element_type=F32)
            if bias is not None:
                s = s + bias
            if online:
                buf[0][mp] = s
                buf[1][mp] = jnp.max(s, axis=0, keepdims=True)
            else:
                buf[mp] = jnp.exp2(s).astype(BF16)

    def stage_two(k0, buf):
        vt_c = vt_ref[:, pl.ds(k0, bk)]
        for mp in range(2):
            if online:
                m_old = m_ref[mp]
                m_new = jnp.maximum(m_old, buf[1][mp])
                p = jnp.exp2(buf[0][mp] - m_new).astype(BF16)
                acc_ref[mp] = acc_ref[mp] * jnp.exp2(m_old - m_new) + _dot(vt_c, p)
                m_ref[mp] = m_new
            else:
                acc_ref[mp] += _dot(vt_c, buf[mp])

    def near_start(t):
        return pl.multiple_of(q0 + t * bk, bk)

    def far_start(t):
        t = jnp.minimum(t, n_far - 1)
        j = jnp.where(t < j_lo, t, t + per_q)
        return pl.multiple_of(j * bk, bk), (t >= j_lo).astype(jnp.int32)

    def near_one(t):
        stage_one(near_start(t), 2, slope2 * dnear_ref[t], bufs[t % 2])

    def far_one(t, parity):
        k0, after = far_start(t)
        stage_one(k0, after, None, bufs[parity])

    def far_two(t, parity):
        stage_two(far_start(t)[0], bufs[parity])

    near_one(0)
    for t in range(1, per_q):
        near_one(t)
        stage_two(near_start(t - 1), bufs[(t - 1) % 2])
    far_one(0, per_q % 2)
    stage_two(near_start(per_q - 1), bufs[(per_q - 1) % 2])

    def far_body(u, c):
        for r in range(FAR_UNROLL):
            far_one(FAR_UNROLL * u + r + 1, (per_q + r + 1) % 2)
            far_two(FAR_UNROLL * u + r, (per_q + r) % 2)
        return c

    lax.fori_loop(0, n_far // FAR_UNROLL, far_body, 0)
    for t in range(n_far - n_far % FAR_UNROLL, n_far):
        if t + 1 < n_far:
            far_one(t + 1, (per_q + t + 1) % 2)
        far_two(t, (per_q + t) % 2)

    lp = lam_ref[...]
    lam = (jnp.exp(jnp.sum(lp[0:1] * lp[1:2], axis=-1, keepdims=True))
           - jnp.exp(jnp.sum(lp[2:3] * lp[3:4], axis=-1, keepdims=True)) + lam_init)
    a0 = acc_ref[0]
    a1 = acc_ref[1]
    o = (a0[0:DIFF_V_DIM] / a0[DIFF_V_DIM:DIFF_V_DIM + 1]
         - lam * (a1[0:DIFF_V_DIM] / a1[DIFF_V_DIM:DIFF_V_DIM + 1]))
    ms = jnp.mean(o * o, axis=0, keepdims=True)
    o = o * lax.rsqrt(ms + NORM_EPS) * sg_ref[...] * (1.0 - lam_init)
    o_ref[...] = o.T.astype(o_ref.dtype)


def _diff_attn(qa, ka, vt, lam_p, subln_g, *, lam_init, online):
    s = qa.shape[2]
    bq = ATT_BQ
    assert bq % ATT_BK == 0 and s > bq
    kern = functools.partial(_diff_attn_kernel, lam_init=lam_init, online=online)
    per_q = bq // ATT_BK
    key = np.arange(per_q * ATT_BK).reshape(per_q, ATT_BK, 1)
    dnear = jnp.asarray(-np.abs(np.arange(bq).reshape(1, 1, bq) - key), F32)
    scratch = [
        pltpu.VMEM((2, V_AUG, bq), F32),
        pltpu.VMEM((2, 3, bq, 128), BF16),
    ]
    if online:
        scratch += [
            pltpu.VMEM((2, 1, bq), F32),
            pltpu.VMEM((2, ATT_BK, bq), F32),
            pltpu.VMEM((2, ATT_BK, bq), F32),
            pltpu.VMEM((2, 1, bq), F32),
            pltpu.VMEM((2, 1, bq), F32),
        ]
    else:
        scratch += [pltpu.VMEM((2, ATT_BK, bq), BF16), pltpu.VMEM((2, ATT_BK, bq), BF16)]
    return pl.pallas_call(
        kern,
        grid=(DIFF_HEADS, s // bq),
        in_specs=[
            pl.BlockSpec((4, DIFF_QK_DIM), lambda h, i: (0, 0)),
            pl.BlockSpec((DIFF_V_DIM, 1), lambda h, i: (0, 0)),
            pl.BlockSpec((per_q, ATT_BK, bq), lambda h, i: (0, 0, 0)),
            pl.BlockSpec((None, 2, bq, 128), lambda h, i: (h, 0, i, 0)),
            pl.BlockSpec((None, 2, s, 128), lambda h, i: (h, 0, 0, 0)),
            pl.BlockSpec((None, V_AUG, s), lambda h, i: (h, 0, 0)),
        ],
        out_specs=pl.BlockSpec((bq, DIFF_V_DIM), lambda h, i: (i, h)),
        out_shape=jax.ShapeDtypeStruct((s, DIFF_DIM), BF16),
        scratch_shapes=scratch,
        compiler_params=_cparams(("parallel", "arbitrary")),
        name="diff_attn_online" if online else "diff_attn",
    )(lam_p, subln_g.reshape(DIFF_V_DIM, 1), dnear, qa, ka, vt)


def _shift_rows(x, prev_row, next_row):
    n = x.shape[0]
    row = lax.broadcasted_iota(jnp.int32, x.shape, 0)
    xp = jnp.where(row == 0, prev_row, pltpu.roll(x, 1, 0))
    xn = jnp.where(row == n - 1, next_row, pltpu.roll(x, n - 1, 0))
    return xp, xn


def _halo_rows(prev_ref, next_ref, i, n_blocks):
    hp = prev_ref.shape[0]
    prev_row = prev_ref[hp - 1:hp, :].astype(F32)
    next_row = next_ref[0:1, :].astype(F32)
    prev_row = jnp.where(i > 0, prev_row, 0.0)
    next_row = jnp.where(i < n_blocks - 1, next_row, 0.0)
    return prev_row, next_row


def _rwkv_prep_kernel(
        x_ref, xp_ref, xn_ref, l_ref, lp_ref, ln_ref,
        mu_ref, mul_ref, w0_ref, w2_ref, a0_ref, a2_ref, g2_ref, kk_ref, ka_ref, rk_ref,
        bd_ref, tri_ref,
        v_out, g_out, bonus_out, rb_out, kb_out, kt_out, bt_out, kh_out, bh_out, gc_out):
    i = pl.program_id(0)
    nb = pl.num_programs(0)
    tm = x_ref.shape[0]

    x = x_ref[...].astype(F32)
    prev_row, next_row = _halo_rows(xp_ref, xn_ref, i, nb)
    xp, xn = _shift_rows(x, prev_row, next_row)
    mu = mu_ref[...]
    x = x + mu[0:1] * (xp - x) + mu[1:2] * (xn - x)
    lo = l_ref[...].astype(F32)
    prev_row, next_row = _halo_rows(lp_ref, ln_ref, i, nb)
    lop, lon = _shift_rows(lo, prev_row, next_row)
    mul = mul_ref[...]
    lo = lo + mul[0:1] * (lop - lo) + mul[1:2] * (lon - lo)

    r = x[:, 0:RWKV_DIM]
    k = x[:, RWKV_DIM:2 * RWKV_DIM]
    v = x[:, 2 * RWKV_DIM:3 * RWKV_DIM]
    tw = jnp.tanh(lo[:, 0:128]).astype(BF16)
    la = lo[:, 128:256].astype(BF16)
    lg = jax.nn.sigmoid(lo[:, 256:512]).astype(BF16)

    g = _dot(lg, g2_ref[...])
    kk = k * kk_ref[...]
    ss = _group_mean_sq(kk, bd_ref, 1.0)
    kk = kk * lax.rsqrt(jnp.maximum(ss, 1e-24))
    bonus = _group_sum((r * k * rk_ref[...]).astype(BF16), bd_ref) * v

    g_out[...] = g.astype(g_out.dtype)
    bonus_out[...] = bonus.astype(bonus_out.dtype)
    v_out[...] = v.astype(BF16)

    tri = tri_ref[...]
    for d in range(2):
        wl = w0_ref[d:d + 1, :] + _dot(tw, w2_ref[d])
        logdec = -math.exp(-0.5) * jax.nn.sigmoid(wl)
        a = jax.nn.sigmoid(a0_ref[d:d + 1, :] + _dot(la, a2_ref[d]))
        k_d = k * (1.0 + (a - 1.0) * ka_ref[...])
        b_d = kk * a
        pre = _dot_exactish(tri, logdec)
        chunk_tot = jnp.broadcast_to(
            pre.reshape(tm // CHUNK, CHUNK, RWKV_DIM)[:, CHUNK - 1:CHUNK, :],
            (tm // CHUNK, CHUNK, RWKV_DIM)).reshape(tm, RWKV_DIM)
        suf = chunk_tot - pre
        if d == 0:
            lc, ex, rem = pre, pre - logdec, suf
        else:
            lc, ex, rem = suf + logdec, suf, pre - logdec
        e_neg = jnp.exp(-lc)
        e_rem = jnp.exp(rem)
        outs = (
            (rb_out, r * jnp.exp(lc)),
            (kb_out, kk * jnp.exp(ex)),
            (kt_out, k_d * e_neg),
            (bt_out, b_d * e_neg),
            (kh_out, k_d * e_rem),
            (bh_out, b_d * e_rem),
        )
        for ref, val in outs:
            ref[d] = val.astype(BF16)
        gc_out[d] = jnp.exp(chunk_tot).reshape(tm // 8, 8, RWKV_DIM)[:, 0, :]


def _rwkv_prep(p, mu_rkv, mu_lora, w0, w2p, a0, a2p, g2p, k_k, k_a, r_k, bd, tri, *, tm):
    s = p.shape[0]
    nb = s // tm
    hb = tm // 16
    last16 = s // 16 - 1
    rkv_w = 3 * RWKV_DIM
    assert COL_RKV % rkv_w == 0 and COL_LORA % LORA_PAD == 0
    c_rkv = COL_RKV // rkv_w
    c_lora = COL_LORA // LORA_PAD

    def prev_map(c):
        return lambda i: (jnp.maximum(i * hb - 1, 0), c)

    def next_map(c):
        return lambda i: (jnp.minimum((i + 1) * hb, last16), c)

    full = lambda *shape: pl.BlockSpec(shape, lambda i: (0,) * len(shape))
    hm = lambda: pl.BlockSpec((2, tm, RWKV_DIM), lambda i: (0, i, 0))
    hm_shape = jax.ShapeDtypeStruct((2, s, RWKV_DIM), BF16)
    return pl.pallas_call(
        _rwkv_prep_kernel,
        grid=(nb,),
        in_specs=[
            pl.BlockSpec((tm, rkv_w), lambda i: (i, c_rkv)),
            pl.BlockSpec((16, rkv_w), prev_map(c_rkv)),
            pl.BlockSpec((16, rkv_w), next_map(c_rkv)),
            pl.BlockSpec((tm, LORA_PAD), lambda i: (i, c_lora)),
            pl.BlockSpec((16, LORA_PAD), prev_map(c_lora)),
            pl.BlockSpec((16, LORA_PAD), next_map(c_lora)),
            full(2, rkv_w), full(2, LORA_PAD),
            full(2, RWKV_DIM), full(2, 128, RWKV_DIM),
            full(2, RWKV_DIM), full(2, 128, RWKV_DIM),
            full(256, RWKV_DIM),
            full(1, RWKV_DIM), full(1, RWKV_DIM), full(1, RWKV_DIM),
            full(GROUP_SLAB, GROUP_SLAB), full(tm, tm),
        ],
        out_specs=[
            pl.BlockSpec((tm, RWKV_DIM), lambda i: (i, 0)),
            pl.BlockSpec((tm, RWKV_DIM), lambda i: (i, 0)),
            pl.BlockSpec((tm, RWKV_DIM), lambda i: (i, 0)),
            hm(), hm(), hm(), hm(), hm(), hm(),
            pl.BlockSpec((2, tm // 8, RWKV_DIM), lambda i: (0, i, 0)),
        ],
        out_shape=[
            jax.ShapeDtypeStruct((s, RWKV_DIM), BF16),
            jax.ShapeDtypeStruct((s, RWKV_DIM), BF16),
            jax.ShapeDtypeStruct((s, RWKV_DIM), BF16),
            hm_shape, hm_shape, hm_shape, hm_shape, hm_shape, hm_shape,
            jax.ShapeDtypeStruct((2, s // 8, RWKV_DIM), F32),
        ],
        compiler_params=_cparams(("parallel",)),
        name="rwkv_prep",
    )(p, p, p, p, p, p, mu_rkv, mu_lora, w0, w2p, a0, a2p, g2p, k_k, k_a, r_k, bd, tri)


def _bdot(a, b):
    return lax.dot_general(a, b, (((2,), (1,)), ((0,), (0,))), preferred_element_type=F32)


def _bdot_nt(a, b):
    return lax.dot_general(a, b, (((2,), (2,)), ((0,), (0,))), preferred_element_type=F32)


def _bdot_tn(a, b):
    return lax.dot_general(a, b, (((1,), (1,)), ((0,), (0,))), preferred_element_type=F32)


def _bdot_inv(a, b):
    return _bdot(a.astype(BF16), b.astype(BF16))


WKV_CHUNKS_PER_STEP = 2


def _wkv_kernel(*refs):
    (vf_ref, vr_ref), ins, (gcf_ref, gcr_ref, yf_ref, yr_ref, s_ref) = refs[:2], refs[2:14], refs[14:]

    @pl.when(pl.program_id(0) == 0)
    def _():
        s_ref[...] = jnp.zeros_like(s_ref)

    def rows(ref, part, n):
        return ref.at[n * part:n * (part + 1), :]

    for part in range(WKV_CHUNKS_PER_STEP):
        back = WKV_CHUNKS_PER_STEP - 1 - part
        _wkv_chunk(False, rows(vf_ref, part, CHUNK), [rows(r, part, CHUNK) for r in ins[0:6]],
                   rows(gcf_ref, part, 8), rows(yf_ref, part, CHUNK), s_ref.at[0])
        _wkv_chunk(True, rows(vr_ref, back, CHUNK), [rows(r, back, CHUNK) for r in ins[6:12]],
                   rows(gcr_ref, back, 8), rows(yr_ref, back, CHUNK), s_ref.at[1])


N_PAIRS = RWKV_HEADS // 2


def _pairs(x):
    return jnp.stack([x[:, 128 * p:128 * (p + 1)] for p in range(N_PAIRS)])


def _pair_diag(y):
    lane = lax.broadcasted_iota(jnp.int32, y.shape, 2)
    zero = jnp.zeros_like(y)
    return jnp.concatenate([jnp.where(lane < 64, y, zero), jnp.where(lane >= 64, y, zero)], axis=1)


def _pair_tn(a, b):
    full = _bdot_tn(a, b)
    lane = lax.broadcasted_iota(jnp.int32, (N_PAIRS, 64, 128), 2)
    return jnp.where(lane < 64, full[:, :64], full[:, 64:])


def _wkv_chunk(reverse, v_ref, scaled_refs, gc_ref, y_ref, s_ref):
    c = CHUNK
    row = lax.broadcasted_iota(jnp.int32, (N_PAIRS, c, 128), 1)
    col = lax.broadcasted_iota(jnp.int32, (N_PAIRS, c, 128), 2) % 64
    strict = (row < col) if reverse else (row > col)
    incl = (row <= col) if reverse else (row >= col)

    v = _pairs(v_ref[...])
    rb, kb, kt, bt, kh, bh = [_pairs(r[...]) for r in scaled_refs]

    kr = jnp.concatenate([kb, rb], axis=1)
    sk = _bdot_nt(kr, _pair_diag(kt))
    sb = _bdot_nt(kr, _pair_diag(bt))
    a_kk = jnp.where(strict, sk[:, :c], 0.0)
    a_rk = jnp.where(incl, sk[:, c:], 0.0)
    l_mat = jnp.where(strict, sb[:, :c], 0.0)
    a_rb = jnp.where(incl, sb[:, c:], 0.0)

    eye = jnp.where(row == col, 1.0, 0.0)
    t_inv = eye - l_mat
    l_b = l_mat.astype(BF16)
    pw = _bdot(l_b, _pair_diag(l_b))
    for step in range(5):
        pw_b = _pair_diag(pw.astype(BF16))
        if step < 4:
            both = _bdot(jnp.concatenate([t_inv, pw], axis=1).astype(BF16), pw_b)
            t_inv = t_inv + both[:, :c]
            pw = both[:, c:]
        else:
            t_inv = t_inv + _bdot(t_inv.astype(BF16), pw_b)

    q12 = _bdot(jnp.concatenate([a_kk, a_rk], axis=1).astype(BF16), _pair_diag(v))
    q1, q2 = q12[:, :c], q12[:, c:]
    t_b = t_inv.astype(BF16)
    kq = jnp.concatenate([_pair_diag(kb), _pair_diag(q1.astype(BF16))], axis=2)
    wkuv = _bdot(t_b, kq).astype(BF16)
    wk, uv = wkuv[:, :, :128], wkuv[:, :, 128:]
    corr = _bdot(a_rb.astype(BF16), jnp.concatenate([_pair_diag(wk), _pair_diag(uv)], axis=2))
    rw = rb.astype(F32) - corr[:, :, :128]
    y0 = q2 - corr[:, :, 128:]
    m_mat = _pair_tn(wk, bh)
    j_t = _pair_tn(v, kh) - _pair_tn(uv, bh)

    s_old = s_ref[...]
    s_hi = s_old.astype(BF16)
    s_lo = (s_old - s_hi.astype(F32)).astype(BF16)
    y = _bdot_nt(rw.astype(BF16), _pair_diag(s_hi)) + y0
    sm = _bdot(jnp.concatenate([s_hi, s_lo], axis=1), _pair_diag(m_mat.astype(BF16)))
    s_ref[...] = s_old * _pairs(gc_ref[0:1, :]) - (sm[:, :64] + sm[:, 64:]) + j_t
    for p in range(N_PAIRS):
        y_ref[:, 128 * p:128 * (p + 1)] = y[p]


def _wkv(v, rb, kb, kt, bt, kh, bh, gc):
    s = v.shape[0]
    per = WKV_CHUNKS_PER_STEP
    assert s % (per * CHUNK) == 0
    nc = s // (per * CHUNK)
    scaled = (rb, kb, kt, bt, kh, bh)
    fwd = lambda rows: pl.BlockSpec((None, rows, RWKV_DIM), lambda c: (0, c, 0))
    rev = lambda rows: pl.BlockSpec((None, rows, RWKV_DIM), lambda c: (1, nc - 1 - c, 0))
    y_shape = jax.ShapeDtypeStruct((s, RWKV_DIM), F32)
    return pl.pallas_call(
        _wkv_kernel,
        grid=(nc,),
        in_specs=[
            pl.BlockSpec((per * CHUNK, RWKV_DIM), lambda c: (c, 0)),
            pl.BlockSpec((per * CHUNK, RWKV_DIM), lambda c: (nc - 1 - c, 0)),
            *[fwd(per * CHUNK) for _ in scaled], *[rev(per * CHUNK) for _ in scaled],
            fwd(per * 8), rev(per * 8),
        ],
        out_specs=[
            pl.BlockSpec((per * CHUNK, RWKV_DIM), lambda c: (c, 0)),
            pl.BlockSpec((per * CHUNK, RWKV_DIM), lambda c: (nc - 1 - c, 0)),
        ],
        out_shape=[y_shape, y_shape],
        scratch_shapes=[pltpu.VMEM((2, N_PAIRS, 64, 128), F32)],
        compiler_params=_cparams(("arbitrary",)),
        name="wkv7_chunked",
    )(v, v, *scaled, *scaled, gc, gc)


def _rwkv_post_kernel(yf_ref, yr_ref, g_ref, bonus_ref, lg_ref, lb_ref, bd_ref, o_ref):
    y = yf_ref[...] + yr_ref[...]
    hi, lo = _split2(y)
    mean = (_group_sum(hi, bd_ref) + _group_sum(lo, bd_ref)) * (1.0 / RWKV_HEAD_DIM)
    yc = y - mean
    var = _group_mean_sq(yc, bd_ref, RWKV_HEAD_DIM)
    yn = yc * lax.rsqrt(var + LNX_EPS)
    out = (yn * lg_ref[...] + lb_ref[...] + bonus_ref[...].astype(F32)) * g_ref[...].astype(F32)
    o_ref[...] = out.astype(o_ref.dtype)


def _rwkv_post(y_fwd, y_rev, g, bonus, lnx_g, lnx_b, bd, *, tm):
    s = g.shape[0]
    return pl.pallas_call(
        _rwkv_post_kernel,
        grid=(s // tm,),
        in_specs=[
            pl.BlockSpec((tm, RWKV_DIM), lambda i: (i, 0)),
            pl.BlockSpec((tm, RWKV_DIM), lambda i: (i, 0)),
            pl.BlockSpec((tm, RWKV_DIM), lambda i: (i, 0)),
            pl.BlockSpec((tm, RWKV_DIM), lambda i: (i, 0)),
            pl.BlockSpec((1, RWKV_DIM), lambda i: (0, 0)),
            pl.BlockSpec((1, RWKV_DIM), lambda i: (0, 0)),
            pl.BlockSpec((GROUP_SLAB, GROUP_SLAB), lambda i: (0, 0)),
        ],
        out_specs=pl.BlockSpec((tm, RWKV_DIM), lambda i: (i, 0)),
        out_shape=jax.ShapeDtypeStruct((s, RWKV_DIM), BF16),
        compiler_params=_cparams(("parallel",)),
        name="rwkv_post",
    )(y_fwd, y_rev, g, bonus, lnx_g.reshape(1, RWKV_DIM), lnx_b.reshape(1, RWKV_DIM), bd)


def _mem_attn_kernel(q_ref, kv_ref, gq_ref, gk_ref, o_ref):
    for h in range(MEM_HEADS):
        sl = slice(MEM_HEAD_DIM * h, MEM_HEAD_DIM * (h + 1))
        q = q_ref[:, sl].astype(F32)
        q = q * lax.rsqrt(jnp.mean(q * q, axis=-1, keepdims=True) + NORM_EPS)
        q = q * gq_ref[...] * (MEM_HEAD_DIM ** -0.5)
        km = kv_ref[:, sl].astype(F32)
        km = km * lax.rsqrt(jnp.mean(km * km, axis=-1, keepdims=True) + NORM_EPS)
        km = km * gk_ref[...]
        vm = kv_ref[:, MEM_DIM + MEM_HEAD_DIM * h:MEM_DIM + MEM_HEAD_DIM * (h + 1)]
        s = lax.dot_general(q.astype(BF16), km.astype(BF16), (((1,), (1,)), ((), ())),
                            preferred_element_type=F32)
        s = s - jnp.max(s, axis=-1, keepdims=True)
        e = jnp.exp(s)
        pr = e / jnp.sum(e, axis=-1, keepdims=True)
        o_ref[:, sl] = _dot(pr.astype(BF16), vm).astype(o_ref.dtype)


def _mem_attn(p, kv, gq, gk, *, tm):
    s = p.shape[0]
    assert COL_MEM % MEM_DIM == 0
    return pl.pallas_call(
        _mem_attn_kernel,
        grid=(s // tm,),
        in_specs=[
            pl.BlockSpec((tm, MEM_DIM), lambda i: (i, COL_MEM // MEM_DIM)),
            pl.BlockSpec((N_MEM, 2 * MEM_DIM), lambda i: (0, 0)),
            pl.BlockSpec((1, MEM_HEAD_DIM), lambda i: (0, 0)),
            pl.BlockSpec((1, MEM_HEAD_DIM), lambda i: (0, 0)),
        ],
        out_specs=pl.BlockSpec((tm, MEM_DIM), lambda i: (i, 0)),
        out_shape=jax.ShapeDtypeStruct((s, MEM_DIM), BF16),
        compiler_params=_cparams(("parallel",)),
        name="mem_attn",
    )(p, kv, gq, gk)


def _merge_kernel(o0_ref, o1_ref, o2_ref, g0_ref, g1_ref, g2_ref, w_ref, m_ref):
    def gate(g_ref):
        return 0.5 * jnp.tanh(0.5 * g_ref[...].astype(F32)) + 0.5

    acc = gate(g0_ref) * _dot(o0_ref[...], w_ref[0])
    acc = acc + gate(g1_ref) * _dot(o1_ref[...], w_ref[1])
    acc = acc + gate(g2_ref) * _dot(o2_ref[...], w_ref[2])
    m_ref[...] = acc.astype(m_ref.dtype)


def _merge(o_diff, o_rwkv, o_mem, p, w_branch, layer, *, tm, tn):
    s = p.shape[0]
    assert COL_GATE % tn == 0 and D_MODEL % tn == 0
    gate_blk = COL_GATE // tn
    per = D_MODEL // tn
    o_spec = lambda: pl.BlockSpec((tm, 1024), lambda i, j: (i, 0))
    g_spec = lambda b: pl.BlockSpec((tm, tn), lambda i, j: (i, gate_blk + b * per + j))
    return pl.pallas_call(
        _merge_kernel,
        grid=(s // tm, per),
        in_specs=[
            o_spec(), o_spec(), o_spec(),
            g_spec(0), g_spec(1), g_spec(2),
            pl.BlockSpec((None, N_BRANCH, 1024, tn), lambda i, j: (layer, 0, 0, j)),
        ],
        out_specs=pl.BlockSpec((tm, tn), lambda i, j: (i, j)),
        out_shape=jax.ShapeDtypeStruct((s, D_MODEL), BF16),
        compiler_params=_cparams(("parallel", "arbitrary")),
        name="merge",
    )(o_diff, o_rwkv, o_mem, p, p, p, w_branch)


GLU_TILE = 256


def _ffn_up_glu_kernel(x_ref, xp_ref, xn_ref, g_ref, wg_ref, wv_ref, cw_ref, cb_ref, o_ref,
                       h_ref, halo_ref):
    i = pl.program_id(0)
    nb = pl.num_programs(0)

    def norm(x):
        ms = jnp.mean(x * x, axis=-1, keepdims=True)
        return x * lax.rsqrt(ms + NORM_EPS) * g_ref[...]

    @pl.when(pl.program_id(1) == 0)
    def _():
        h_ref[...] = norm(x_ref[...]).astype(BF16)
        row = lax.broadcasted_iota(jnp.int32, xp_ref.shape, 0)
        before = jnp.where(i > 0, pltpu.roll(norm(xp_ref[...]), 1, 0), 0.0)
        after = jnp.where(i < nb - 1, pltpu.roll(norm(xn_ref[...]), 1, 0), 0.0)
        halo_ref[...] = jnp.where(row == 0, before, jnp.where(row == 1, after, 0.0)).astype(BF16)

    h = h_ref[...]
    halo = halo_ref[...]
    tn = o_ref.shape[1]
    for c0 in range(0, tn, GLU_TILE):
        sl = slice(c0, c0 + GLU_TILE)
        wg = wg_ref[:, sl]
        ug = _dot(h, wg)
        uv = _dot(h, wv_ref[:, sl])
        edge = _dot(halo, wg)
        gp, gn = _shift_rows(ug, edge[0:1], edge[1:2])
        cw = cw_ref[:, sl]
        conv = cw[0:1] * gp + cw[1:2] * ug + cw[2:3] * gn + cb_ref[:, sl]
        half = 0.5 * conv
        o_ref[:, sl] = ((half + half * jnp.tanh(half)) * uv).astype(o_ref.dtype)


def _ffn_up_glu(x, g, w_up, layer, conv_w, conv_b, *, tm, tn):
    s, k = x.shape
    assert tn % GLU_TILE == 0
    nj = D_FF // tn
    hb = tm // 16
    last16 = s // 16 - 1
    return pl.pallas_call(
        _ffn_up_glu_kernel,
        grid=(s // tm, nj),
        in_specs=[
            pl.BlockSpec((tm, k), lambda i, j: (i, 0)),
            pl.BlockSpec((16, k), lambda i, j: (jnp.maximum(i * hb - 1, 0), 0)),
            pl.BlockSpec((16, k), lambda i, j: (jnp.minimum((i + 1) * hb, last16), 0)),
            pl.BlockSpec((1, k), lambda i, j: (0, 0)),
            pl.BlockSpec((None, k, tn), lambda i, j: (layer, 0, j)),
            pl.BlockSpec((None, k, tn), lambda i, j: (layer, 0, nj + j)),
            pl.BlockSpec((3, tn), lambda i, j: (0, j)),
            pl.BlockSpec((1, tn), lambda i, j: (0, j)),
        ],
        out_specs=pl.BlockSpec((tm, tn), lambda i, j: (i, j)),
        out_shape=jax.ShapeDtypeStruct((s, D_FF), BF16),
        scratch_shapes=[pltpu.VMEM((tm, k), BF16), pltpu.VMEM((16, k), BF16)],
        compiler_params=_cparams(("parallel", "arbitrary")),
        name="ffn_up_glu",
    )(x, x, x, g.reshape(1, k), w_up, w_up, conv_w, conv_b.reshape(1, D_FF))


def _block_ones(n, group):
    idx = np.arange(n) // group
    return jnp.asarray(idx[:, None] == idx[None, :], BF16)


def _chunk_tri(n):
    idx = np.arange(n)
    same = (idx[:, None] // CHUNK) == (idx[None, :] // CHUNK)
    return jnp.asarray(same & (idx[:, None] >= idx[None, :]), BF16)


def _pad_rows(w, rows_before, total):
    n = w.shape[-1]
    out = jnp.zeros((total, n), w.dtype)
    return lax.dynamic_update_slice(out, w, (rows_before, 0))


def _pad_w_in(w):
    lora0 = COL_RKV + 3 * RWKV_DIM
    mem0 = lora0 + LORA_COLS
    gate0 = mem0 + MEM_DIM
    pad = jnp.zeros((w.shape[0], LORA_PAD - LORA_COLS), w.dtype)
    parts = [w[:, :lora0], w[:, gate0:], w[:, mem0:gate0], w[:, lora0:mem0], pad]
    return jnp.concatenate(parts, axis=1).astype(BF16)


def kernel(x, mem, attn_norm_g, w_in, diff_qk_g, diff_lambda, diff_subln_g, rwkv_mu, rwkv_w0,
           rwkv_w2, rwkv_a0, rwkv_a2, rwkv_g2, rwkv_k_k, rwkv_k_a, rwkv_r_k, rwkv_lnx_g,
           rwkv_lnx_b, mem_norm_g, w_mem_kv, mem_qk_g, w_branch, w_out, ffn_norm_g, w_ffn_up,
           ffn_conv_w, ffn_conv_b, w_ffn_down):
    b, s, d = x.shape
    assert b == 1 and d == D_MODEL and s % ATT_BQ == 0
    xs = x.reshape(s, d)
    mem2 = mem.reshape(N_MEM, d)
    prep_tm = 256
    bd64 = _block_ones(GROUP_SLAB, 64)
    tri = _chunk_tri(prep_tm)
    n_rwkv_main = 3 * RWKV_DIM
    big_tm = 1024 if s % 1024 == 0 else 512
    w_mem_kv_b = w_mem_kv.astype(BF16)
    w_branch_b = w_branch.astype(BF16)
    w_out_b = w_out.astype(BF16)
    w_ffn_up_b = w_ffn_up.astype(BF16)
    w_ffn_down_b = w_ffn_down.astype(BF16)

    for l in range(DEPTH):
        lam_init = 0.8 - 0.6 * math.exp(-0.3 * l)
        p = _rms_mm(xs, attn_norm_g[l], _pad_w_in(w_in[l])[None], 0, tm=big_tm, tn=1536,
                    name="rms_w_in")

        gq = jnp.tile(diff_qk_g[l, 0].reshape(1, 128), (1, DIFF_HEADS))
        gk = jnp.tile(diff_qk_g[l, 1].reshape(1, 128), (1, DIFF_HEADS))
        qc, kc, score_bound = _attn_consts(diff_qk_g[l])
        qa, ka, vt = _diff_prep(p, gq, gk, qc, kc, bd64, tm=256)
        o_diff = lax.cond(
            score_bound <= SCORE_BOUND_MAX,
            functools.partial(_diff_attn, lam_init=lam_init, online=False),
            functools.partial(_diff_attn, lam_init=lam_init, online=True),
            qa, ka, vt, diff_lambda[l], diff_subln_g[l])

        mu = rwkv_mu[l]
        mu_rkv = mu[:, :n_rwkv_main]
        mu_lora = jnp.pad(mu[:, n_rwkv_main:], ((0, 0), (0, LORA_PAD - LORA_COLS)))
        w2p = jnp.stack([_pad_rows(rwkv_w2[l, dd], 64 * dd, 128) for dd in range(2)]).astype(BF16)
        a2p = jnp.stack([_pad_rows(rwkv_a2[l, dd], 64 * dd, 128) for dd in range(2)]).astype(BF16)
        g2p = _pad_rows(rwkv_g2[l], 0, 256).astype(BF16)
        (v_h, g_tok, bonus, rb, kb, kt, bt, kh, bh, gc) = _rwkv_prep(
            p, mu_rkv, mu_lora, rwkv_w0[l], w2p, rwkv_a0[l], a2p, g2p,
            rwkv_k_k[l].reshape(1, RWKV_DIM), rwkv_k_a[l].reshape(1, RWKV_DIM),
            rwkv_r_k[l].reshape(1, RWKV_DIM), bd64, tri, tm=prep_tm)
        y_fwd, y_rev = _wkv(v_h, rb, kb, kt, bt, kh, bh, gc)
        o_rwkv = _rwkv_post(y_fwd, y_rev, g_tok, bonus, rwkv_lnx_g[l], rwkv_lnx_b[l], bd64,
                            tm=256)

        kv = _rms_mm(mem2, mem_norm_g[l], w_mem_kv_b, l, tm=N_MEM, tn=1024, name="rms_mem_kv")
        o_mem = _mem_attn(p, kv, mem_qk_g[l, 0].reshape(1, MEM_HEAD_DIM),
                          mem_qk_g[l, 1].reshape(1, MEM_HEAD_DIM), tm=512)

        merged = _merge(o_diff, o_rwkv, o_mem, p, w_branch_b, l, tm=256, tn=D_MODEL)
        xs = _mm_res(merged, w_out_b, l, xs, tm=512, tn=D_MODEL, name="w_out_res")

        act = _ffn_up_glu(xs, ffn_norm_g[l], w_ffn_up_b, l, ffn_conv_w[l], ffn_conv_b[l],
                          tm=big_tm, tn=512)
        xs = _mm_res(act, w_ffn_down_b, l, xs, tm=big_tm, tn=512, name="ffn_down_res")

    return xs.reshape(b, s, d)
```

```python
import functools
import math

import jax
import jax.numpy as jnp
import numpy as np
from jax import lax
from jax.experimental import pallas as pl
from jax.experimental.pallas import tpu as pltpu

F32 = jnp.float32
BF16 = jnp.bfloat16

D_MODEL = 2048
DEPTH = 2
DIFF_HEADS = 8
DIFF_QK_DIM = 64
DIFF_V_DIM = 128
DIFF_DIM = 1024
RWKV_HEADS = 16
RWKV_HEAD_DIM = 64
RWKV_DIM = 1024
DECAY_LORA = 64
AAA_LORA = 64
GATE_LORA = 160
LORA_COLS = 2 * DECAY_LORA + 2 * AAA_LORA + GATE_LORA
LORA_PAD = 512
N_MEM = 256
MEM_HEADS = 4
MEM_HEAD_DIM = 256
MEM_DIM = 1024
N_BRANCH = 3
D_FF = 5632
NORM_EPS = 1e-6
LNX_EPS = 64e-5

COL_DIFF = 0
COL_RKV = 3 * DIFF_DIM
COL_GATE = COL_RKV + 3 * RWKV_DIM
COL_MEM = COL_GATE + N_BRANCH * D_MODEL
COL_LORA = COL_MEM + MEM_DIM
N_IN_PAD = COL_LORA + LORA_PAD

CHUNK = 64
LOG2E = 1.4426950408889634
N_POS_COLS = 12
SCORE_BOUND_MAX = 30.0
ATT_BQ = 1024
ATT_BK = 512
V_AUG = 144
FAR_UNROLL = 6
VMEM_LIMIT = 56 * 1024 * 1024


def _cparams(sem):
    return pltpu.CompilerParams(dimension_semantics=sem, vmem_limit_bytes=VMEM_LIMIT)


def _split2(x):
    hi = x.astype(BF16)
    return hi, (x - hi.astype(F32)).astype(BF16)


def _dot(a, b):
    return jnp.dot(a, b, preferred_element_type=F32)


def _dot_exactish(a_bf16_exact, x_f32):
    hi, lo = _split2(x_f32)
    return _dot(a_bf16_exact, hi) + _dot(a_bf16_exact, lo)


def _rms_mm_kernel(x_ref, g_ref, w_ref, o_ref, h_ref):
    @pl.when(pl.program_id(1) == 0)
    def _():
        x = x_ref[...]
        ms = jnp.mean(x * x, axis=-1, keepdims=True)
        h_ref[...] = (x * lax.rsqrt(ms + NORM_EPS) * g_ref[...]).astype(BF16)

    o_ref[...] = _dot(h_ref[...], w_ref[...]).astype(o_ref.dtype)


def _rms_mm(x, g, w, layer, *, tm, tn, name):
    m, k = x.shape
    n = w.shape[2]
    return pl.pallas_call(
        _rms_mm_kernel,
        grid=(m // tm, n // tn),
        in_specs=[
            pl.BlockSpec((tm, k), lambda i, j: (i, 0)),
            pl.BlockSpec((1, k), lambda i, j: (0, 0)),
            pl.BlockSpec((None, k, tn), lambda i, j: (layer, 0, j)),
        ],
        out_specs=pl.BlockSpec((tm, tn), lambda i, j: (i, j)),
        out_shape=jax.ShapeDtypeStruct((m, n), BF16),
        scratch_shapes=[pltpu.VMEM((tm, k), BF16)],
        compiler_params=_cparams(("parallel", "arbitrary")),
        name=name,
    )(x, g.reshape(1, k), w)


def _mm_res_kernel(a_ref, w_ref, r_ref, o_ref):
    o_ref[...] = r_ref[...] + _dot(a_ref[...], w_ref[...])


def _mm_res(a, w, layer, res, *, tm, tn, name):
    m, k = a.shape
    n = w.shape[2]
    return pl.pallas_call(
        _mm_res_kernel,
        grid=(m // tm, n // tn),
        in_specs=[
            pl.BlockSpec((tm, k), lambda i, j: (i, 0)),
            pl.BlockSpec((None, k, tn), lambda i, j: (layer, 0, j)),
            pl.BlockSpec((tm, tn), lambda i, j: (i, j)),
        ],
        out_specs=pl.BlockSpec((tm, tn), lambda i, j: (i, j)),
        out_shape=jax.ShapeDtypeStruct((m, n), F32),
        compiler_params=_cparams(("parallel", "arbitrary")),
        name=name,
    )(a, w, res)


GROUP_SLAB = 256


def _group_sum(xb, bd_ref):
    bd = bd_ref[...]
    n = xb.shape[1]
    slabs = [_dot(xb[:, q:q + GROUP_SLAB], bd) for q in range(0, n, GROUP_SLAB)]
    return jnp.concatenate(slabs, axis=1)


def _group_mean_sq(x, bd_ref, group):
    return _group_sum((x * x).astype(BF16), bd_ref) * (1.0 / group)


def _aug_base(mp):
    return 64 if mp == 0 else 0


def _diff_prep_kernel(q_ref, k_ref, v_ref, gq_ref, gk_ref, qc_ref, kc_ref, bd_ref,
                      qa_ref, ka_ref, vt_ref):
    tm = q_ref.shape[0]
    row0 = pl.program_id(0) * tm
    lane = lax.broadcasted_iota(jnp.int32, (tm, 128), 1)
    pos = row0 + lax.broadcasted_iota(jnp.int32, (tm, 128), 0)
    pos_lo = (pos & 127).astype(F32)
    pos_hi = (pos >> 7).astype(F32)

    q = q_ref[...].astype(F32)
    qn = q * lax.rsqrt(_group_mean_sq(q, bd_ref, DIFF_QK_DIM) + NORM_EPS)
    qn = qn * gq_ref[...] * (DIFF_QK_DIM ** -0.5 * LOG2E)
    k = k_ref[...].astype(F32)
    kn = k * lax.rsqrt(_group_mean_sq(k, bd_ref, DIFF_QK_DIM) + NORM_EPS)
    kn = kn * gk_ref[...]

    for h in range(DIFF_HEADS):
        qh = qn[:, 128 * h:128 * (h + 1)]
        kh = kn[:, 128 * h:128 * (h + 1)]
        for mp in range(2):
            a0 = _aug_base(mp)
            g = 2 * h + mp
            is_data = (lane < 64) if mp == 0 else (lane >= 64)
            aug_q = jnp.where(lane < a0 + 3, pos_lo,
                              jnp.where(lane < a0 + 6, pos_hi, qc_ref[g:g + 1, :]))
            aug_k = jnp.where((lane >= a0 + 6) & (lane < a0 + 9), pos_lo,
                              jnp.where((lane >= a0 + 9) & (lane < a0 + 12), pos_hi,
                                        kc_ref[g:g + 1, :]))
            qa_ref[h, mp] = jnp.where(is_data, qh, aug_q).astype(BF16)
            ka_ref[h, mp] = jnp.where(is_data, kh, aug_k).astype(BF16)

    vt = v_ref[...].astype(F32).T
    sub = lax.broadcasted_iota(jnp.int32, (V_AUG - DIFF_V_DIM, tm), 0)
    ones_rows = jnp.where(sub == 0, 1.0, 0.0).astype(BF16)
    for h in range(DIFF_HEADS):
        vt_ref[h, 0:DIFF_V_DIM, :] = vt[128 * h:128 * (h + 1), :].astype(BF16)
        vt_ref[h, DIFF_V_DIM:V_AUG, :] = ones_rows


def _bf16_split3_const(x):
    parts = []
    for _ in range(3):
        part = float(np.asarray(x, np.float32).astype(BF16).astype(np.float32))
        parts.append(part)
        x = x - part
    return parts


def _attn_consts(qk_g):
    l_parts = _bf16_split3_const(LOG2E)
    m_nat = 8.0 * jnp.max(jnp.abs(qk_g[0]), axis=-1) * jnp.max(jnp.abs(qk_g[1]), axis=-1)
    qc = np.zeros((2 * DIFF_HEADS, 128), np.float32)
    kc = np.zeros((2 * DIFF_HEADS, 128), np.float32)
    bound_lane = np.zeros((2, 2 * DIFF_HEADS, 128), np.float32)
    for h in range(DIFF_HEADS):
        slope = 2.0 ** (-(h + 1))
        for mp in range(2):
            a0 = _aug_base(mp)
            g = 2 * h + mp
            for t, lp in enumerate(l_parts):
                qc[g, a0 + 6 + t] = slope * lp
                qc[g, a0 + 9 + t] = 128.0 * slope * lp
                kc[g, a0 + t] = -slope * lp
                kc[g, a0 + 3 + t] = -128.0 * slope * lp
            bound_lane[mp, g, a0 + N_POS_COLS] = 1.0
            kc[g, a0 + N_POS_COLS] = 1.0
    m2 = -m_nat * LOG2E
    qc = qc + m2[0] * bound_lane[0] + m2[1] * bound_lane[1]
    return qc, jnp.asarray(kc), jnp.max(m_nat)


def _diff_prep(p, gq, gk, qc, kc, bd, *, tm):
    s = p.shape[0]
    return pl.pallas_call(
        _diff_prep_kernel,
        grid=(s // tm,),
        in_specs=[
            pl.BlockSpec((tm, DIFF_DIM), lambda i: (i, 0)),
            pl.BlockSpec((tm, DIFF_DIM), lambda i: (i, 1)),
            pl.BlockSpec((tm, DIFF_DIM), lambda i: (i, 2)),
            pl.BlockSpec((1, DIFF_DIM), lambda i: (0, 0)),
            pl.BlockSpec((1, DIFF_DIM), lambda i: (0, 0)),
            pl.BlockSpec((2 * DIFF_HEADS, 128), lambda i: (0, 0)),
            pl.BlockSpec((2 * DIFF_HEADS, 128), lambda i: (0, 0)),
            pl.BlockSpec((GROUP_SLAB, GROUP_SLAB), lambda i: (0, 0)),
        ],
        out_specs=[
            pl.BlockSpec((DIFF_HEADS, 2, tm, 128), lambda i: (0, 0, i, 0)),
            pl.BlockSpec((DIFF_HEADS, 2, tm, 128), lambda i: (0, 0, i, 0)),
            pl.BlockSpec((DIFF_HEADS, V_AUG, tm), lambda i: (0, 0, i)),
        ],
        out_shape=[
            jax.ShapeDtypeStruct((DIFF_HEADS, 2, s, 128), BF16),
            jax.ShapeDtypeStruct((DIFF_HEADS, 2, s, 128), BF16),
            jax.ShapeDtypeStruct((DIFF_HEADS, V_AUG, s), BF16),
        ],
        compiler_params=_cparams(("parallel",)),
        name="diff_prep",
    )(p, p, p, gq, gk, qc, kc, bd)


def _diff_attn_kernel(lam_ref, sg_ref, dnear_ref, qa_ref, ka_ref, vt_ref, o_ref,
                      acc_ref, qv_ref, *mode_refs, lam_init, online):
    h = pl.program_id(0)
    i = pl.program_id(1)
    bq = qa_ref.shape[1]
    s_len = ka_ref.shape[1]
    bk = ATT_BK
    n_chunks = s_len // bk
    per_q = bq // bk
    n_far = n_chunks - per_q
    j_lo = i * per_q
    q0 = i * bq
    slope2 = jnp.exp2(-jnp.full((1, 1), h + 1, jnp.int32).astype(F32)) * LOG2E
    if online:
        m_ref, sa_ref, sb_ref, mxa_ref, mxb_ref = mode_refs
        m_ref[...] = jnp.full_like(m_ref, -1e30)
        bufs = ((sa_ref, mxa_ref), (sb_ref, mxb_ref))
    else:
        bufs = mode_refs
    acc_ref[...] = jnp.zeros_like(acc_ref)

    lane = lax.broadcasted_iota(jnp.int32, (1, 128), 1)
    for mp in range(2):
        q = qa_ref[mp]
        pos_lane = (lane >= _aug_base(mp)) & (lane < _aug_base(mp) + N_POS_COLS)
        qv_ref[mp, 0] = q
        qv_ref[mp, 1] = jnp.where(pos_lane, -q, q)
        qv_ref[mp, 2] = jnp.where(pos_lane, jnp.zeros_like(q), q)

    def stage_one(k0, variant, bias, buf):
        for mp in range(2):
            kc = ka_ref[mp, pl.ds(k0, bk), :]
            s = lax.dot_general(kc, qv_ref[mp, variant], (((1,), (1,)), ((), ())),
                                preferred_element_type=F32)
            if bias is not None:
                s = s + bias
            if online:
                buf[0][mp] = s
                buf[1][mp] = jnp.max(s, axis=0, keepdims=True)
            else:
                buf[mp] = jnp.exp2(s).astype(BF16)

    def stage_two(k0, buf):
        vt_c = vt_ref[:, pl.ds(k0, bk)]
        for mp in range(2):
            if online:
                m_old = m_ref[mp]
                m_new = jnp.maximum(m_old, buf[1][mp])
                p = jnp.exp2(buf[0][mp] - m_new).astype(BF16)
                acc_ref[mp] = acc_ref[mp] * jnp.exp2(m_old - m_new) + _dot(vt_c, p)
                m_ref[mp] = m_new
            else:
                acc_ref[mp] += _dot(vt_c, buf[mp])

    def near_start(t):
        return pl.multiple_of(q0 + t * bk, bk)

    def far_start(t):
        t = jnp.minimum(t, n_far - 1)
        j = jnp.where(t < j_lo, t, t + per_q)
        return pl.multiple_of(j * bk, bk), (t >= j_lo).astype(jnp.int32)

    def near_one(t):
        stage_one(near_start(t), 2, slope2 * dnear_ref[t], bufs[t % 2])

    def far_one(t, parity):
        k0, after = far_start(t)
        stage_one(k0, after, None, bufs[parity])

    def far_two(t, parity):
        stage_two(far_start(t)[0], bufs[parity])

    near_one(0)
    for t in range(1, per_q):
        near_one(t)
        stage_two(near_start(t - 1), bufs[(t - 1) % 2])
    far_one(0, per_q % 2)
    stage_two(near_start(per_q - 1), bufs[(per_q - 1) % 2])

    def far_body(u, c):
        for r in range(FAR_UNROLL):
            far_one(FAR_UNROLL * u + r + 1, (per_q + r + 1) % 2)
            far_two(FAR_UNROLL * u + r, (per_q + r) % 2)
        return c

    lax.fori_loop(0, n_far // FAR_UNROLL, far_body, 0)
    for t in range(n_far - n_far % FAR_UNROLL, n_far):
        if t + 1 < n_far:
            far_one(t + 1, (per_q + t + 1) % 2)
        far_two(t, (per_q + t) % 2)

    lp = lam_ref[...]
    lam = (jnp.exp(jnp.sum(lp[0:1] * lp[1:2], axis=-1, keepdims=True))
           - jnp.exp(jnp.sum(lp[2:3] * lp[3:4], axis=-1, keepdims=True)) + lam_init)
    a0 = acc_ref[0]
    a1 = acc_ref[1]
    o = (a0[0:DIFF_V_DIM] / a0[DIFF_V_DIM:DIFF_V_DIM + 1]
         - lam * (a1[0:DIFF_V_DIM] / a1[DIFF_V_DIM:DIFF_V_DIM + 1]))
    ms = jnp.mean(o * o, axis=0, keepdims=True)
    o = o * lax.rsqrt(ms + NORM_EPS) * sg_ref[...] * (1.0 - lam_init)
    o_ref[...] = o.T.astype(o_ref.dtype)


def _diff_attn(qa, ka, vt, lam_p, subln_g, *, lam_init, online):
    s = qa.shape[2]
    bq = ATT_BQ
    assert bq % ATT_BK == 0 and s > bq
    kern = functools.partial(_diff_attn_kernel, lam_init=lam_init, online=online)
    per_q = bq // ATT_BK
    key = np.arange(per_q * ATT_BK).reshape(per_q, ATT_BK, 1)
    dnear = jnp.asarray(-np.abs(np.arange(bq).reshape(1, 1, bq) - key), F32)
    scratch = [
        pltpu.VMEM((2, V_AUG, bq), F32),
        pltpu.VMEM((2, 3, bq, 128), BF16),
    ]
    if online:
        scratch += [
            pltpu.VMEM((2, 1, bq), F32),
            pltpu.VMEM((2, ATT_BK, bq), F32),
            pltpu.VMEM((2, ATT_BK, bq), F32),
            pltpu.VMEM((2, 1, bq), F32),
            pltpu.VMEM((2, 1, bq), F32),
        ]
    else:
        scratch += [pltpu.VMEM((2, ATT_BK, bq), BF16), pltpu.VMEM((2, ATT_BK, bq), BF16)]
    return pl.pallas_call(
        kern,
        grid=(DIFF_HEADS, s // bq),
        in_specs=[
            pl.BlockSpec((4, DIFF_QK_DIM), lambda h, i: (0, 0)),
            pl.BlockSpec((DIFF_V_DIM, 1), lambda h, i: (0, 0)),
            pl.BlockSpec((per_q, ATT_BK, bq), lambda h, i: (0, 0, 0)),
            pl.BlockSpec((None, 2, bq, 128), lambda h, i: (h, 0, i, 0)),
            pl.BlockSpec((None, 2, s, 128), lambda h, i: (h, 0, 0, 0)),
            pl.BlockSpec((None, V_AUG, s), lambda h, i: (h, 0, 0)),
        ],
        out_specs=pl.BlockSpec((bq, DIFF_V_DIM), lambda h, i: (i, h)),
        out_shape=jax.ShapeDtypeStruct((s, DIFF_DIM), BF16),
        scratch_shapes=scratch,
        compiler_params=_cparams(("parallel", "arbitrary")),
        name="diff_attn_online" if online else "diff_attn",
    )(lam_p, subln_g.reshape(DIFF_V_DIM, 1), dnear, qa, ka, vt)


def _shift_rows(x, prev_row, next_row):
    n = x.shape[0]
    row = lax.broadcasted_iota(jnp.int32, x.shape, 0)
    xp = jnp.where(row == 0, prev_row, pltpu.roll(x, 1, 0))
    xn = jnp.where(row == n - 1, next_row, pltpu.roll(x, n - 1, 0))
    return xp, xn


def _halo_rows(prev_ref, next_ref, i, n_blocks):
    hp = prev_ref.shape[0]
    prev_row = prev_ref[hp - 1:hp, :].astype(F32)
    next_row = next_ref[0:1, :].astype(F32)
    prev_row = jnp.where(i > 0, prev_row, 0.0)
    next_row = jnp.where(i < n_blocks - 1, next_row, 0.0)
    return prev_row, next_row


def _rwkv_prep_kernel(
        x_ref, xp_ref, xn_ref, l_ref, lp_ref, ln_ref,
        mu_ref, mul_ref, w0_ref, w2_ref, a0_ref, a2_ref, g2_ref, kk_ref, ka_ref, rk_ref,
        bd_ref, tri_ref,
        v_out, g_out, bonus_out, rb_out, kb_out, kt_out, bt_out, kh_out, bh_out, gc_out):
    i = pl.program_id(0)
    nb = pl.num_programs(0)
    tm = x_ref.shape[0]

    x = x_ref[...].astype(F32)
    prev_row, next_row = _halo_rows(xp_ref, xn_ref, i, nb)
    xp, xn = _shift_rows(x, prev_row, next_row)
    mu = mu_ref[...]
    x = x + mu[0:1] * (xp - x) + mu[1:2] * (xn - x)
    lo = l_ref[...].astype(F32)
    prev_row, next_row = _halo_rows(lp_ref, ln_ref, i, nb)
    lop, lon = _shift_rows(lo, prev_row, next_row)
    mul = mul_ref[...]
    lo = lo + mul[0:1] * (lop - lo) + mul[1:2] * (lon - lo)

    r = x[:, 0:RWKV_DIM]
    k = x[:, RWKV_DIM:2 * RWKV_DIM]
    v = x[:, 2 * RWKV_DIM:3 * RWKV_DIM]
    tw = jnp.tanh(lo[:, 0:128]).astype(BF16)
    la = lo[:, 128:256].astype(BF16)
    lg = jax.nn.sigmoid(lo[:, 256:512]).astype(BF16)

    g = _dot(lg, g2_ref[...])
    kk = k * kk_ref[...]
    ss = _group_mean_sq(kk, bd_ref, 1.0)
    kk = kk * lax.rsqrt(jnp.maximum(ss, 1e-24))
    bonus = _group_sum((r * k * rk_ref[...]).astype(BF16), bd_ref) * v

    g_out[...] = g.astype(g_out.dtype)
    bonus_out[...] = bonus.astype(bonus_out.dtype)
    v_out[...] = v.astype(BF16)

    tri = tri_ref[...]
    for d in range(2):
        wl = w0_ref[d:d + 1, :] + _dot(tw, w2_ref[d])
        logdec = -math.exp(-0.5) * jax.nn.sigmoid(wl)
        a = jax.nn.sigmoid(a0_ref[d:d + 1, :] + _dot(la, a2_ref[d]))
        k_d = k * (1.0 + (a - 1.0) * ka_ref[...])
        b_d = kk * a
        pre = _dot_exactish(tri, logdec)
        chunk_tot = jnp.broadcast_to(
            pre.reshape(tm // CHUNK, CHUNK, RWKV_DIM)[:, CHUNK - 1:CHUNK, :],
            (tm // CHUNK, CHUNK, RWKV_DIM)).reshape(tm, RWKV_DIM)
        suf = chunk_tot - pre
        if d == 0:
            lc, ex, rem = pre, pre - logdec, suf
        else:
            lc, ex, rem = suf + logdec, suf, pre - logdec
        e_neg = jnp.exp(-lc)
        e_rem = jnp.exp(rem)
        outs = (
            (rb_out, r * jnp.exp(lc)),
            (kb_out, kk * jnp.exp(ex)),
            (kt_out, k_d * e_neg),
            (bt_out, b_d * e_neg),
            (kh_out, k_d * e_rem),
            (bh_out, b_d * e_rem),
        )
        for ref, val in outs:
            ref[d] = val.astype(BF16)
        gc_out[d] = jnp.exp(chunk_tot).reshape(tm // 8, 8, RWKV_DIM)[:, 0, :]


def _rwkv_prep(p, mu_rkv, mu_lora, w0, w2p, a0, a2p, g2p, k_k, k_a, r_k, bd, tri, *, tm):
    s = p.shape[0]
    nb = s // tm
    hb = tm // 16
    last16 = s // 16 - 1
    rkv_w = 3 * RWKV_DIM
    assert COL_RKV % rkv_w == 0 and COL_LORA % LORA_PAD == 0
    c_rkv = COL_RKV // rkv_w
    c_lora = COL_LORA // LORA_PAD

    def prev_map(c):
        return lambda i: (jnp.maximum(i * hb - 1, 0), c)

    def next_map(c):
        return lambda i: (jnp.minimum((i + 1) * hb, last16), c)

    full = lambda *shape: pl.BlockSpec(shape, lambda i: (0,) * len(shape))
    hm = lambda: pl.BlockSpec((2, tm, RWKV_DIM), lambda i: (0, i, 0))
    hm_shape = jax.ShapeDtypeStruct((2, s, RWKV_DIM), BF16)
    return pl.pallas_call(
        _rwkv_prep_kernel,
        grid=(nb,),
        in_specs=[
            pl.BlockSpec((tm, rkv_w), lambda i: (i, c_rkv)),
            pl.BlockSpec((16, rkv_w), prev_map(c_rkv)),
            pl.BlockSpec((16, rkv_w), next_map(c_rkv)),
            pl.BlockSpec((tm, LORA_PAD), lambda i: (i, c_lora)),
            pl.BlockSpec((16, LORA_PAD), prev_map(c_lora)),
            pl.BlockSpec((16, LORA_PAD), next_map(c_lora)),
            full(2, rkv_w), full(2, LORA_PAD),
            full(2, RWKV_DIM), full(2, 128, RWKV_DIM),
            full(2, RWKV_DIM), full(2, 128, RWKV_DIM),
            full(256, RWKV_DIM),
            full(1, RWKV_DIM), full(1, RWKV_DIM), full(1, RWKV_DIM),
            full(GROUP_SLAB, GROUP_SLAB), full(tm, tm),
        ],
        out_specs=[
            pl.BlockSpec((tm, RWKV_DIM), lambda i: (i, 0)),
            pl.BlockSpec((tm, RWKV_DIM), lambda i: (i, 0)),
            pl.BlockSpec((tm, RWKV_DIM), lambda i: (i, 0)),
            hm(), hm(), hm(), hm(), hm(), hm(),
            pl.BlockSpec((2, tm // 8, RWKV_DIM), lambda i: (0, i, 0)),
        ],
        out_shape=[
            jax.ShapeDtypeStruct((s, RWKV_DIM), BF16),
            jax.ShapeDtypeStruct((s, RWKV_DIM), BF16),
            jax.ShapeDtypeStruct((s, RWKV_DIM), BF16),
            hm_shape, hm_shape, hm_shape, hm_shape, hm_shape, hm_shape,
            jax.ShapeDtypeStruct((2, s // 8, RWKV_DIM), F32),
        ],
        compiler_params=_cparams(("parallel",)),
        name="rwkv_prep",
    )(p, p, p, p, p, p, mu_rkv, mu_lora, w0, w2p, a0, a2p, g2p, k_k, k_a, r_k, bd, tri)


def _bdot(a, b):
    return lax.dot_general(a, b, (((2,), (1,)), ((0,), (0,))), preferred_element_type=F32)


def _bdot_nt(a, b):
    return lax.dot_general(a, b, (((2,), (2,)), ((0,), (0,))), preferred_element_type=F32)


def _bdot_tn(a, b):
    return lax.dot_general(a, b, (((1,), (1,)), ((0,), (0,))), preferred_element_type=F32)


WKV_CHUNKS_PER_STEP = 2


def _wkv_kernel(*refs):
    (vf_ref, vr_ref), ins, (gcf_ref, gcr_ref, yf_ref, yr_ref, s_ref) = refs[:2], refs[2:14], refs[14:]

    @pl.when(pl.program_id(0) == 0)
    def _():
        s_ref[...] = jnp.zeros_like(s_ref)

    def rows(ref, part, n):
        return ref.at[n * part:n * (part + 1), :]

    for part in range(WKV_CHUNKS_PER_STEP):
        back = WKV_CHUNKS_PER_STEP - 1 - part
        _wkv_chunk(False, rows(vf_ref, part, CHUNK), [rows(r, part, CHUNK) for r in ins[0:6]],
                   rows(gcf_ref, part, 8), rows(yf_ref, part, CHUNK), s_ref.at[0])
        _wkv_chunk(True, rows(vr_ref, back, CHUNK), [rows(r, back, CHUNK) for r in ins[6:12]],
                   rows(gcr_ref, back, 8), rows(yr_ref, back, CHUNK), s_ref.at[1])


WKV_GROUP = 2
GROUP_LANES = WKV_GROUP * RWKV_HEAD_DIM
N_GROUPS = RWKV_HEADS // WKV_GROUP


def _groups(x):
    return jnp.stack([x[:, GROUP_LANES * p:GROUP_LANES * (p + 1)] for p in range(N_GROUPS)])


def _head_of_lane(shape):
    return lax.broadcasted_iota(jnp.int32, shape, 2) // RWKV_HEAD_DIM


def _group_diag(y):
    head = _head_of_lane(y.shape)
    zero = jnp.zeros_like(y)
    return jnp.concatenate([jnp.where(head == h, y, zero) for h in range(WKV_GROUP)], axis=1)


def _group_tn(a, b):
    full = _bdot_tn(a, b)
    n = RWKV_HEAD_DIM
    head = _head_of_lane((N_GROUPS, n, GROUP_LANES))
    out = full[:, :n]
    for h in range(1, WKV_GROUP):
        out = jnp.where(head == h, full[:, n * h:n * (h + 1)], out)
    return out


def _wkv_chunk(reverse, v_ref, scaled_refs, gc_ref, y_ref, s_ref):
    c = CHUNK
    lanes = GROUP_LANES
    row = lax.broadcasted_iota(jnp.int32, (N_GROUPS, c, lanes), 1)
    col = lax.broadcasted_iota(jnp.int32, (N_GROUPS, c, lanes), 2) % RWKV_HEAD_DIM
    strict = (row < col) if reverse else (row > col)
    incl = (row <= col) if reverse else (row >= col)

    v = _groups(v_ref[...])
    rb, kb, kt, bt, kh, bh = [_groups(r[...]) for r in scaled_refs]

    kr = jnp.concatenate([kb, rb], axis=1)
    sk = _bdot_nt(kr, _group_diag(kt))
    sb = _bdot_nt(kr, _group_diag(bt))
    a_kk = jnp.where(strict, sk[:, :c], 0.0)
    a_rk = jnp.where(incl, sk[:, c:], 0.0)
    l_mat = jnp.where(strict, sb[:, :c], 0.0)
    a_rb = jnp.where(incl, sb[:, c:], 0.0)

    eye = jnp.where(row == col, 1.0, 0.0)
    t_inv = eye - l_mat
    l_b = l_mat.astype(BF16)
    pw = _bdot(l_b, _group_diag(l_b))
    for step in range(5):
        pw_b = _group_diag(pw.astype(BF16))
        if step < 4:
            both = _bdot(jnp.concatenate([t_inv, pw], axis=1).astype(BF16), pw_b)
            t_inv = t_inv + both[:, :c]
            pw = both[:, c:]
        else:
            t_inv = t_inv + _bdot(t_inv.astype(BF16), pw_b)

    q12 = _bdot(jnp.concatenate([a_kk, a_rk], axis=1).astype(BF16), _group_diag(v))
    q1, q2 = q12[:, :c], q12[:, c:]
    t_b = t_inv.astype(BF16)
    kq = jnp.concatenate([_group_diag(kb), _group_diag(q1.astype(BF16))], axis=2)
    wkuv = _bdot(t_b, kq).astype(BF16)
    wk, uv = wkuv[:, :, :lanes], wkuv[:, :, lanes:]
    corr = _bdot(a_rb.astype(BF16), jnp.concatenate([_group_diag(wk), _group_diag(uv)], axis=2))
    rw = rb.astype(F32) - corr[:, :, :lanes]
    y0 = q2 - corr[:, :, lanes:]
    m_mat = _group_tn(wk, bh)
    j_t = _group_tn(v, kh) - _group_tn(uv, bh)

    n = RWKV_HEAD_DIM
    s_old = s_ref[...]
    s_hi = s_old.astype(BF16)
    s_lo = (s_old - s_hi.astype(F32)).astype(BF16)
    y = _bdot_nt(rw.astype(BF16), _group_diag(s_hi)) + y0
    sm = _bdot(jnp.concatenate([s_hi, s_lo], axis=1), _group_diag(m_mat.astype(BF16)))
    s_ref[...] = s_old * _groups(gc_ref[0:1, :]) - (sm[:, :n] + sm[:, n:]) + j_t
    for p in range(N_GROUPS):
        y_ref[:, lanes * p:lanes * (p + 1)] = y[p]


def _wkv(v, rb, kb, kt, bt, kh, bh, gc):
    s = v.shape[0]
    per = WKV_CHUNKS_PER_STEP
    assert s % (per * CHUNK) == 0
    nc = s // (per * CHUNK)
    scaled = (rb, kb, kt, bt, kh, bh)
    fwd = lambda rows: pl.BlockSpec((None, rows, RWKV_DIM), lambda c: (0, c, 0))
    rev = lambda rows: pl.BlockSpec((None, rows, RWKV_DIM), lambda c: (1, nc - 1 - c, 0))
    y_shape = jax.ShapeDtypeStruct((s, RWKV_DIM), F32)
    return pl.pallas_call(
        _wkv_kernel,
        grid=(nc,),
        in_specs=[
            pl.BlockSpec((per * CHUNK, RWKV_DIM), lambda c: (c, 0)),
            pl.BlockSpec((per * CHUNK, RWKV_DIM), lambda c: (nc - 1 - c, 0)),
            *[fwd(per * CHUNK) for _ in scaled], *[rev(per * CHUNK) for _ in scaled],
            fwd(per * 8), rev(per * 8),
        ],
        out_specs=[
            pl.BlockSpec((per * CHUNK, RWKV_DIM), lambda c: (c, 0)),
            pl.BlockSpec((per * CHUNK, RWKV_DIM), lambda c: (nc - 1 - c, 0)),
        ],
        out_shape=[y_shape, y_shape],
        scratch_shapes=[pltpu.VMEM((2, N_GROUPS, RWKV_HEAD_DIM, GROUP_LANES), F32)],
        compiler_params=_cparams(("arbitrary",)),
        name="wkv7_chunked",
    )(v, v, *scaled, *scaled, gc, gc)


def _rwkv_post_kernel(yf_ref, yr_ref, g_ref, bonus_ref, lg_ref, lb_ref, bd_ref, o_ref):
    y = yf_ref[...] + yr_ref[...]
    hi, lo = _split2(y)
    mean = (_group_sum(hi, bd_ref) + _group_sum(lo, bd_ref)) * (1.0 / RWKV_HEAD_DIM)
    yc = y - mean
    var = _group_mean_sq(yc, bd_ref, RWKV_HEAD_DIM)
    yn = yc * lax.rsqrt(var + LNX_EPS)
    out = (yn * lg_ref[...] + lb_ref[...] + bonus_ref[...].astype(F32)) * g_ref[...].astype(F32)
    o_ref[...] = out.astype(o_ref.dtype)


def _rwkv_post(y_fwd, y_rev, g, bonus, lnx_g, lnx_b, bd, *, tm):
    s = g.shape[0]
    return pl.pallas_call(
        _rwkv_post_kernel,
        grid=(s // tm,),
        in_specs=[
            pl.BlockSpec((tm, RWKV_DIM), lambda i: (i, 0)),
            pl.BlockSpec((tm, RWKV_DIM), lambda i: (i, 0)),
            pl.BlockSpec((tm, RWKV_DIM), lambda i: (i, 0)),
            pl.BlockSpec((tm, RWKV_DIM), lambda i: (i, 0)),
            pl.BlockSpec((1, RWKV_DIM), lambda i: (0, 0)),
            pl.BlockSpec((1, RWKV_DIM), lambda i: (0, 0)),
            pl.BlockSpec((GROUP_SLAB, GROUP_SLAB), lambda i: (0, 0)),
        ],
        out_specs=pl.BlockSpec((tm, RWKV_DIM), lambda i: (i, 0)),
        out_shape=jax.ShapeDtypeStruct((s, RWKV_DIM), BF16),
        compiler_params=_cparams(("parallel",)),
        name="rwkv_post",
    )(y_fwd, y_rev, g, bonus, lnx_g.reshape(1, RWKV_DIM), lnx_b.reshape(1, RWKV_DIM), bd)


def _mem_attn_kernel(q_ref, kv_ref, gq_ref, gk_ref, o_ref):
    for h in range(MEM_HEADS):
        sl = slice(MEM_HEAD_DIM * h, MEM_HEAD_DIM * (h + 1))
        q = q_ref[:, sl].astype(F32)
        q = q * lax.rsqrt(jnp.mean(q * q, axis=-1, keepdims=True) + NORM_EPS)
        q = q * gq_ref[...] * (MEM_HEAD_DIM ** -0.5)
        km = kv_ref[:, sl].astype(F32)
        km = km * lax.rsqrt(jnp.mean(km * km, axis=-1, keepdims=True) + NORM_EPS)
        km = km * gk_ref[...]
        vm = kv_ref[:, MEM_DIM + MEM_HEAD_DIM * h:MEM_DIM + MEM_HEAD_DIM * (h + 1)]
        s = lax.dot_general(q.astype(BF16), km.astype(BF16), (((1,), (1,)), ((), ())),
                            preferred_element_type=F32)
        s = s - jnp.max(s, axis=-1, keepdims=True)
        e = jnp.exp(s)
        pr = e / jnp.sum(e, axis=-1, keepdims=True)
        o_ref[:, sl] = _dot(pr.astype(BF16), vm).astype(o_ref.dtype)


def _mem_attn(p, kv, gq, gk, *, tm):
    s = p.shape[0]
    assert COL_MEM % MEM_DIM == 0
    return pl.pallas_call(
        _mem_attn_kernel,
        grid=(s // tm,),
        in_specs=[
            pl.BlockSpec((tm, MEM_DIM), lambda i: (i, COL_MEM // MEM_DIM)),
            pl.BlockSpec((N_MEM, 2 * MEM_DIM), lambda i: (0, 0)),
            pl.BlockSpec((1, MEM_HEAD_DIM), lambda i: (0, 0)),
            pl.BlockSpec((1, MEM_HEAD_DIM), lambda i: (0, 0)),
        ],
        out_specs=pl.BlockSpec((tm, MEM_DIM), lambda i: (i, 0)),
        out_shape=jax.ShapeDtypeStruct((s, MEM_DIM), BF16),
        compiler_params=_cparams(("parallel",)),
        name="mem_attn",
    )(p, kv, gq, gk)


def _merge_kernel(o0_ref, o1_ref, o2_ref, g0_ref, g1_ref, g2_ref, w_ref, m_ref):
    def gate(g_ref):
        return 0.5 * jnp.tanh(0.5 * g_ref[...].astype(F32)) + 0.5

    acc = gate(g0_ref) * _dot(o0_ref[...], w_ref[0])
    acc = acc + gate(g1_ref) * _dot(o1_ref[...], w_ref[1])
    acc = acc + gate(g2_ref) * _dot(o2_ref[...], w_ref[2])
    m_ref[...] = acc.astype(m_ref.dtype)


def _merge(o_diff, o_rwkv, o_mem, p, w_branch, layer, *, tm, tn):
    s = p.shape[0]
    assert COL_GATE % tn == 0 and D_MODEL % tn == 0
    gate_blk = COL_GATE // tn
    per = D_MODEL // tn
    o_spec = lambda: pl.BlockSpec((tm, 1024), lambda i, j: (i, 0))
    g_spec = lambda b: pl.BlockSpec((tm, tn), lambda i, j: (i, gate_blk + b * per + j))
    return pl.pallas_call(
        _merge_kernel,
        grid=(s // tm, per),
        in_specs=[
            o_spec(), o_spec(), o_spec(),
            g_spec(0), g_spec(1), g_spec(2),
            pl.BlockSpec((None, N_BRANCH, 1024, tn), lambda i, j: (layer, 0, 0, j)),
        ],
        out_specs=pl.BlockSpec((tm, tn), lambda i, j: (i, j)),
        out_shape=jax.ShapeDtypeStruct((s, D_MODEL), BF16),
        compiler_params=_cparams(("parallel", "arbitrary")),
        name="merge",
    )(o_diff, o_rwkv, o_mem, p, p, p, w_branch)


GLU_TILE = 256


def _ffn_up_glu_kernel(x_ref, xp_ref, xn_ref, g_ref, wg_ref, wv_ref, cw_ref, cb_ref, o_ref,
                       h_ref, halo_ref):
    i = pl.program_id(0)
    nb = pl.num_programs(0)

    def norm(x):
        ms = jnp.mean(x * x, axis=-1, keepdims=True)
        return x * lax.rsqrt(ms + NORM_EPS) * g_ref[...]

    @pl.when(pl.program_id(1) == 0)
    def _():
        h_ref[...] = norm(x_ref[...]).astype(BF16)
        row = lax.broadcasted_iota(jnp.int32, xp_ref.shape, 0)
        before = jnp.where(i > 0, pltpu.roll(norm(xp_ref[...]), 1, 0), 0.0)
        after = jnp.where(i < nb - 1, pltpu.roll(norm(xn_ref[...]), 1, 0), 0.0)
        halo_ref[...] = jnp.where(row == 0, before, jnp.where(row == 1, after, 0.0)).astype(BF16)

    h = h_ref[...]
    halo = halo_ref[...]
    tn = o_ref.shape[1]
    for c0 in range(0, tn, GLU_TILE):
        sl = slice(c0, c0 + GLU_TILE)
        wg = wg_ref[:, sl]
        ug = _dot(h, wg)
        uv = _dot(h, wv_ref[:, sl])
        edge = _dot(halo, wg)
        gp, gn = _shift_rows(ug, edge[0:1], edge[1:2])
        cw = cw_ref[:, sl]
        conv = cw[0:1] * gp + cw[1:2] * ug + cw[2:3] * gn + cb_ref[:, sl]
        half = 0.5 * conv
        o_ref[:, sl] = ((half + half * jnp.tanh(half)) * uv).astype(o_ref.dtype)


def _ffn_up_glu(x, g, w_up, layer, conv_w, conv_b, *, tm, tn):
    s, k = x.shape
    assert tn % GLU_TILE == 0
    nj = D_FF // tn
    hb = tm // 16
    last16 = s // 16 - 1
    return pl.pallas_call(
        _ffn_up_glu_kernel,
        grid=(s // tm, nj),
        in_specs=[
            pl.BlockSpec((tm, k), lambda i, j: (i, 0)),
            pl.BlockSpec((16, k), lambda i, j: (jnp.maximum(i * hb - 1, 0), 0)),
            pl.BlockSpec((16, k), lambda i, j: (jnp.minimum((i + 1) * hb, last16), 0)),
            pl.BlockSpec((1, k), lambda i, j: (0, 0)),
            pl.BlockSpec((None, k, tn), lambda i, j: (layer, 0, j)),
            pl.BlockSpec((None, k, tn), lambda i, j: (layer, 0, nj + j)),
            pl.BlockSpec((3, tn), lambda i, j: (0, j)),
            pl.BlockSpec((1, tn), lambda i, j: (0, j)),
        ],
        out_specs=pl.BlockSpec((tm, tn), lambda i, j: (i, j)),
        out_shape=jax.ShapeDtypeStruct((s, D_FF), BF16),
        scratch_shapes=[pltpu.VMEM((tm, k), BF16), pltpu.VMEM((16, k), BF16)],
        compiler_params=_cparams(("parallel", "arbitrary")),
        name="ffn_up_glu",
    )(x, x, x, g.reshape(1, k), w_up, w_up, conv_w, conv_b.reshape(1, D_FF))


def _block_ones(n, group):
    idx = np.arange(n) // group
    return jnp.asarray(idx[:, None] == idx[None, :], BF16)


def _chunk_tri(n):
    idx = np.arange(n)
    same = (idx[:, None] // CHUNK) == (idx[None, :] // CHUNK)
    return jnp.asarray(same & (idx[:, None] >= idx[None, :]), BF16)


def _pad_rows(w, rows_before, total):
    n = w.shape[-1]
    out = jnp.zeros((total, n), w.dtype)
    return lax.dynamic_update_slice(out, w, (rows_before, 0))


def _pad_w_in(w):
    lora0 = COL_RKV + 3 * RWKV_DIM
    mem0 = lora0 + LORA_COLS
    gate0 = mem0 + MEM_DIM
    pad = jnp.zeros((w.shape[0], LORA_PAD - LORA_COLS), w.dtype)
    parts = [w[:, :lora0], w[:, gate0:], w[:, mem0:gate0], w[:, lora0:mem0], pad]
    return jnp.concatenate(parts, axis=1).astype(BF16)


def _tile_plan(s):
    rows = 1024 if s % 1024 == 0 else 512
    return dict(
        w_in=dict(tm=rows, tn=1536),
        mem_kv=dict(tm=N_MEM, tn=1024),
        diff_prep=256, rwkv_prep=256, rwkv_post=256, mem_attn=512,
        merge=dict(tm=256, tn=D_MODEL),
        w_out=dict(tm=512, tn=D_MODEL),
        ffn_up=dict(tm=rows, tn=512),
        ffn_down=dict(tm=rows, tn=512),
    )


def kernel(x, mem, attn_norm_g, w_in, diff_qk_g, diff_lambda, diff_subln_g, rwkv_mu, rwkv_w0,
           rwkv_w2, rwkv_a0, rwkv_a2, rwkv_g2, rwkv_k_k, rwkv_k_a, rwkv_r_k, rwkv_lnx_g,
           rwkv_lnx_b, mem_norm_g, w_mem_kv, mem_qk_g, w_branch, w_out, ffn_norm_g, w_ffn_up,
           ffn_conv_w, ffn_conv_b, w_ffn_down):
    b, s, d = x.shape
    assert b == 1 and d == D_MODEL and s % ATT_BQ == 0
    xs = x.reshape(s, d)
    mem2 = mem.reshape(N_MEM, d)
    tiles = _tile_plan(s)
    bd64 = _block_ones(GROUP_SLAB, 64)
    tri = _chunk_tri(tiles["rwkv_prep"])
    n_rwkv_main = 3 * RWKV_DIM
    w_mem_kv_b = w_mem_kv.astype(BF16)
    w_branch_b = w_branch.astype(BF16)
    w_out_b = w_out.astype(BF16)
    w_ffn_up_b = w_ffn_up.astype(BF16)
    w_ffn_down_b = w_ffn_down.astype(BF16)

    for l in range(DEPTH):
        lam_init = 0.8 - 0.6 * math.exp(-0.3 * l)
        p = _rms_mm(xs, attn_norm_g[l], _pad_w_in(w_in[l])[None], 0, name="rms_w_in",
                    **tiles["w_in"])

        gq = jnp.tile(diff_qk_g[l, 0].reshape(1, 128), (1, DIFF_HEADS))
        gk = jnp.tile(diff_qk_g[l, 1].reshape(1, 128), (1, DIFF_HEADS))
        qc, kc, score_bound = _attn_consts(diff_qk_g[l])
        qa, ka, vt = _diff_prep(p, gq, gk, qc, kc, bd64, tm=tiles["diff_prep"])
        o_diff = lax.cond(
            score_bound <= SCORE_BOUND_MAX,
            functools.partial(_diff_attn, lam_init=lam_init, online=False),
            functools.partial(_diff_attn, lam_init=lam_init, online=True),
            qa, ka, vt, diff_lambda[l], diff_subln_g[l])

        mu = rwkv_mu[l]
        mu_rkv = mu[:, :n_rwkv_main]
        mu_lora = jnp.pad(mu[:, n_rwkv_main:], ((0, 0), (0, LORA_PAD - LORA_COLS)))
        w2p = jnp.stack([_pad_rows(rwkv_w2[l, dd], 64 * dd, 128) for dd in range(2)]).astype(BF16)
        a2p = jnp.stack([_pad_rows(rwkv_a2[l, dd], 64 * dd, 128) for dd in range(2)]).astype(BF16)
        g2p = _pad_rows(rwkv_g2[l], 0, 256).astype(BF16)
        (v_h, g_tok, bonus, rb, kb, kt, bt, kh, bh, gc) = _rwkv_prep(
            p, mu_rkv, mu_lora, rwkv_w0[l], w2p, rwkv_a0[l], a2p, g2p,
            rwkv_k_k[l].reshape(1, RWKV_DIM), rwkv_k_a[l].reshape(1, RWKV_DIM),
            rwkv_r_k[l].reshape(1, RWKV_DIM), bd64, tri, tm=tiles["rwkv_prep"])
        y_fwd, y_rev = _wkv(v_h, rb, kb, kt, bt, kh, bh, gc)
        o_rwkv = _rwkv_post(y_fwd, y_rev, g_tok, bonus, rwkv_lnx_g[l], rwkv_lnx_b[l], bd64,
                            tm=tiles["rwkv_post"])

        kv = _rms_mm(mem2, mem_norm_g[l], w_mem_kv_b, l, name="rms_mem_kv", **tiles["mem_kv"])
        o_mem = _mem_attn(p, kv, mem_qk_g[l, 0].reshape(1, MEM_HEAD_DIM),
                          mem_qk_g[l, 1].reshape(1, MEM_HEAD_DIM), tm=tiles["mem_attn"])

        merged = _merge(o_diff, o_rwkv, o_mem, p, w_branch_b, l, **tiles["merge"])
        xs = _mm_res(merged, w_out_b, l, xs, name="w_out_res", **tiles["w_out"])

        act = _ffn_up_glu(xs, ffn_norm_g[l], w_ffn_up_b, l, ffn_conv_w[l], ffn_conv_b[l],
                          **tiles["ffn_up"])
        xs = _mm_res(act, w_ffn_down_b, l, xs, name="ffn_down_res", **tiles["ffn_down"])

    return xs.reshape(b, s, d)
```

```python
import functools
import math

import jax
import jax.numpy as jnp
import numpy as np
from jax import lax
from jax.experimental import pallas as pl
from jax.experimental.pallas import tpu as pltpu

F32 = jnp.float32
BF16 = jnp.bfloat16

D_MODEL = 2048
DEPTH = 2
DIFF_HEADS = 8
DIFF_QK_DIM = 64
DIFF_V_DIM = 128
DIFF_DIM = 1024
RWKV_HEADS = 16
RWKV_HEAD_DIM = 64
RWKV_DIM = 1024
DECAY_LORA = 64
AAA_LORA = 64
GATE_LORA = 160
LORA_COLS = 2 * DECAY_LORA + 2 * AAA_LORA + GATE_LORA
LORA_PAD = 512
N_MEM = 256
MEM_HEADS = 4
MEM_HEAD_DIM = 256
MEM_DIM = 1024
N_BRANCH = 3
D_FF = 5632
NORM_EPS = 1e-6
LNX_EPS = 64e-5

COL_DIFF = 0
COL_RKV = 3 * DIFF_DIM
COL_GATE = COL_RKV + 3 * RWKV_DIM
COL_MEM = COL_GATE + N_BRANCH * D_MODEL
COL_LORA = COL_MEM + MEM_DIM
N_IN_PAD = COL_LORA + LORA_PAD

CHUNK = 64
LOG2E = 1.4426950408889634
N_POS_COLS = 12
SCORE_BOUND_MAX = 30.0
ATT_BQ = 1024
ATT_BK = 512
V_AUG = 144
FAR_UNROLL = 10
VMEM_LIMIT = 56 * 1024 * 1024


def _cparams(sem):
    return pltpu.CompilerParams(dimension_semantics=sem, vmem_limit_bytes=VMEM_LIMIT)


def _split2(x):
    hi = x.astype(BF16)
    return hi, (x - hi.astype(F32)).astype(BF16)


def _dot(a, b):
    return jnp.dot(a, b, preferred_element_type=F32)


def _dot_exactish(a_bf16_exact, x_f32):
    hi, lo = _split2(x_f32)
    return _dot(a_bf16_exact, hi) + _dot(a_bf16_exact, lo)


def _rms_mm_kernel(x_ref, g_ref, w_ref, o_ref, h_ref):
    @pl.when(pl.program_id(1) == 0)
    def _():
        x = x_ref[...]
        ms = jnp.mean(x * x, axis=-1, keepdims=True)
        h_ref[...] = (x * lax.rsqrt(ms + NORM_EPS) * g_ref[...]).astype(BF16)

    o_ref[...] = _dot(h_ref[...], w_ref[...]).astype(o_ref.dtype)


def _rms_mm(x, g, w, layer, *, tm, tn, name):
    m, k = x.shape
    n = w.shape[2]
    return pl.pallas_call(
        _rms_mm_kernel,
        grid=(m // tm, n // tn),
        in_specs=[
            pl.BlockSpec((tm, k), lambda i, j: (i, 0)),
            pl.BlockSpec((1, k), lambda i, j: (0, 0)),
            pl.BlockSpec((None, k, tn), lambda i, j: (layer, 0, j)),
        ],
        out_specs=pl.BlockSpec((tm, tn), lambda i, j: (i, j)),
        out_shape=jax.ShapeDtypeStruct((m, n), BF16),
        scratch_shapes=[pltpu.VMEM((tm, k), BF16)],
        compiler_params=_cparams(("parallel", "arbitrary")),
        name=name,
    )(x, g.reshape(1, k), w)


def _mm_res_kernel(a_ref, w_ref, r_ref, o_ref):
    o_ref[...] = r_ref[...] + _dot(a_ref[...], w_ref[...])


def _mm_res(a, w, layer, res, *, tm, tn, name):
    m, k = a.shape
    n = w.shape[2]
    return pl.pallas_call(
        _mm_res_kernel,
        grid=(m // tm, n // tn),
        in_specs=[
            pl.BlockSpec((tm, k), lambda i, j: (i, 0)),
            pl.BlockSpec((None, k, tn), lambda i, j: (layer, 0, j)),
            pl.BlockSpec((tm, tn), lambda i, j: (i, j)),
        ],
        out_specs=pl.BlockSpec((tm, tn), lambda i, j: (i, j)),
        out_shape=jax.ShapeDtypeStruct((m, n), F32),
        compiler_params=_cparams(("parallel", "arbitrary")),
        name=name,
    )(a, w, res)


GROUP_SLAB = 256


def _group_sum(xb, bd_ref):
    bd = bd_ref[...]
    n = xb.shape[1]
    slabs = [_dot(xb[:, q:q + GROUP_SLAB], bd) for q in range(0, n, GROUP_SLAB)]
    return jnp.concatenate(slabs, axis=1)


def _group_mean_sq(x, bd_ref, group):
    return _group_sum((x * x).astype(BF16), bd_ref) * (1.0 / group)


def _aug_base(mp):
    return 64 if mp == 0 else 0


def _diff_prep_kernel(q_ref, k_ref, v_ref, gq_ref, gk_ref, qc_ref, kc_ref, bd_ref,
                      qa_ref, ka_ref, vt_ref):
    tm = q_ref.shape[0]
    row0 = pl.program_id(0) * tm
    lane = lax.broadcasted_iota(jnp.int32, (tm, 128), 1)
    pos = row0 + lax.broadcasted_iota(jnp.int32, (tm, 128), 0)
    pos_lo = (pos & 127).astype(F32)
    pos_hi = (pos >> 7).astype(F32)

    q = q_ref[...].astype(F32)
    qn = q * lax.rsqrt(_group_mean_sq(q, bd_ref, DIFF_QK_DIM) + NORM_EPS)
    qn = qn * gq_ref[...] * (DIFF_QK_DIM ** -0.5 * LOG2E)
    k = k_ref[...].astype(F32)
    kn = k * lax.rsqrt(_group_mean_sq(k, bd_ref, DIFF_QK_DIM) + NORM_EPS)
    kn = kn * gk_ref[...]

    for h in range(DIFF_HEADS):
        qh = qn[:, 128 * h:128 * (h + 1)]
        kh = kn[:, 128 * h:128 * (h + 1)]
        for mp in range(2):
            a0 = _aug_base(mp)
            g = 2 * h + mp
            is_data = (lane < 64) if mp == 0 else (lane >= 64)
            aug_q = jnp.where(lane < a0 + 3, pos_lo,
                              jnp.where(lane < a0 + 6, pos_hi, qc_ref[g:g + 1, :]))
            aug_k = jnp.where((lane >= a0 + 6) & (lane < a0 + 9), pos_lo,
                              jnp.where((lane >= a0 + 9) & (lane < a0 + 12), pos_hi,
                                        kc_ref[g:g + 1, :]))
            qa_ref[h, mp] = jnp.where(is_data, qh, aug_q).astype(BF16)
            ka_ref[h, mp] = jnp.where(is_data, kh, aug_k).astype(BF16)

    vt = v_ref[...].astype(F32).T
    sub = lax.broadcasted_iota(jnp.int32, (V_AUG - DIFF_V_DIM, tm), 0)
    ones_rows = jnp.where(sub == 0, 1.0, 0.0).astype(BF16)
    for h in range(DIFF_HEADS):
        vt_ref[h, 0:DIFF_V_DIM, :] = vt[128 * h:128 * (h + 1), :].astype(BF16)
        vt_ref[h, DIFF_V_DIM:V_AUG, :] = ones_rows


def _bf16_split3_const(x):
    parts = []
    for _ in range(3):
        part = float(np.asarray(x, np.float32).astype(BF16).astype(np.float32))
        parts.append(part)
        x = x - part
    return parts


def _attn_consts(qk_g):
    l_parts = _bf16_split3_const(LOG2E)
    m_nat = 8.0 * jnp.max(jnp.abs(qk_g[0]), axis=-1) * jnp.max(jnp.abs(qk_g[1]), axis=-1)
    qc = np.zeros((2 * DIFF_HEADS, 128), np.float32)
    kc = np.zeros((2 * DIFF_HEADS, 128), np.float32)
    bound_lane = np.zeros((2, 2 * DIFF_HEADS, 128), np.float32)
    for h in range(DIFF_HEADS):
        slope = 2.0 ** (-(h + 1))
        for mp in range(2):
            a0 = _aug_base(mp)
            g = 2 * h + mp
            for t, lp in enumerate(l_parts):
                qc[g, a0 + 6 + t] = slope * lp
                qc[g, a0 + 9 + t] = 128.0 * slope * lp
                kc[g, a0 + t] = -slope * lp
                kc[g, a0 + 3 + t] = -128.0 * slope * lp
            bound_lane[mp, g, a0 + N_POS_COLS] = 1.0
            kc[g, a0 + N_POS_COLS] = 1.0
    m2 = -m_nat * LOG2E
    qc = qc + m2[0] * bound_lane[0] + m2[1] * bound_lane[1]
    return qc, jnp.asarray(kc), jnp.max(m_nat)


def _diff_prep(p, gq, gk, qc, kc, bd, *, tm):
    s = p.shape[0]
    return pl.pallas_call(
        _diff_prep_kernel,
        grid=(s // tm,),
        in_specs=[
            pl.BlockSpec((tm, DIFF_DIM), lambda i: (i, 0)),
            pl.BlockSpec((tm, DIFF_DIM), lambda i: (i, 1)),
            pl.BlockSpec((tm, DIFF_DIM), lambda i: (i, 2)),
            pl.BlockSpec((1, DIFF_DIM), lambda i: (0, 0)),
            pl.BlockSpec((1, DIFF_DIM), lambda i: (0, 0)),
            pl.BlockSpec((2 * DIFF_HEADS, 128), lambda i: (0, 0)),
            pl.BlockSpec((2 * DIFF_HEADS, 128), lambda i: (0, 0)),
            pl.BlockSpec((GROUP_SLAB, GROUP_SLAB), lambda i: (0, 0)),
        ],
        out_specs=[
            pl.BlockSpec((DIFF_HEADS, 2, tm, 128), lambda i: (0, 0, i, 0)),
            pl.BlockSpec((DIFF_HEADS, 2, tm, 128), lambda i: (0, 0, i, 0)),
            pl.BlockSpec((DIFF_HEADS, V_AUG, tm), lambda i: (0, 0, i)),
        ],
        out_shape=[
            jax.ShapeDtypeStruct((DIFF_HEADS, 2, s, 128), BF16),
            jax.ShapeDtypeStruct((DIFF_HEADS, 2, s, 128), BF16),
            jax.ShapeDtypeStruct((DIFF_HEADS, V_AUG, s), BF16),
        ],
        compiler_params=_cparams(("parallel",)),
        name="diff_prep",
    )(p, p, p, gq, gk, qc, kc, bd)


def _diff_attn_kernel(lam_ref, sg_ref, dnear_ref, qa_ref, ka_ref, vt_ref, o_ref,
                      acc_ref, qv_ref, *mode_refs, lam_init, online):
    h = pl.program_id(0)
    i = pl.program_id(1)
    bq = qa_ref.shape[1]
    s_len = ka_ref.shape[1]
    bk = ATT_BK
    n_chunks = s_len // bk
    per_q = bq // bk
    n_far = n_chunks - per_q
    j_lo = i * per_q
    q0 = i * bq
    slope2 = jnp.exp2(-jnp.full((1, 1), h + 1, jnp.int32).astype(F32)) * LOG2E
    if online:
        m_ref, sa_ref, sb_ref, mxa_ref, mxb_ref = mode_refs
        m_ref[...] = jnp.full_like(m_ref, -1e30)
        bufs = ((sa_ref, mxa_ref), (sb_ref, mxb_ref))
    else:
        bufs = mode_refs
    acc_ref[...] = jnp.zeros_like(acc_ref)

    lane = lax.broadcasted_iota(jnp.int32, (1, 128), 1)
    for mp in range(2):
        q = qa_ref[mp]
        pos_lane = (lane >= _aug_base(mp)) & (lane < _aug_base(mp) + N_POS_COLS)
        qv_ref[mp, 0] = q
        qv_ref[mp, 1] = jnp.where(pos_lane, -q, q)
        qv_ref[mp, 2] = jnp.where(pos_lane, jnp.zeros_like(q), q)

    def stage_one(k0, variant, bias, buf):
        for mp in range(2):
            kc = ka_ref[mp, pl.ds(k0, bk), :]
            s = lax.dot_general(kc, qv_ref[mp, variant], (((1,), (1,)), ((), ())),
                                preferred_element_type=F32)
            if bias is not None:
                s = s + bias
            if online:
                buf[0][mp] = s
                buf[1][mp] = jnp.max(s, axis=0, keepdims=True)
            else:
                buf[mp] = jnp.exp2(s).astype(BF16)

    def stage_two(k0, buf):
        vt_c = vt_ref[:, pl.ds(k0, bk)]
        for mp in range(2):
            if online:
                m_old = m_ref[mp]
                m_new = jnp.maximum(m_old, buf[1][mp])
                p = jnp.exp2(buf[0][mp] - m_new).astype(BF16)
                acc_ref[mp] = acc_ref[mp] * jnp.exp2(m_old - m_new) + _dot(vt_c, p)
                m_ref[mp] = m_new
            else:
                acc_ref[mp] += _dot(vt_c, buf[mp])

    def near_start(t):
        return pl.multiple_of(q0 + t * bk, bk)

    def far_start(t):
        t = jnp.minimum(t, n_far - 1)
        j = jnp.where(t < j_lo, t, t + per_q)
        return pl.multiple_of(j * bk, bk), (t >= j_lo).astype(jnp.int32)

    def near_one(t):
        stage_one(near_start(t), 2, slope2 * dnear_ref[t], bufs[t % 2])

    def far_one(t, parity):
        k0, after = far_start(t)
        stage_one(k0, after, None, bufs[parity])

    def far_two(t, parity):
        stage_two(far_start(t)[0], bufs[parity])

    near_one(0)
    for t in range(1, per_q):
        near_one(t)
        stage_two(near_start(t - 1), bufs[(t - 1) % 2])
    far_one(0, per_q % 2)
    stage_two(near_start(per_q - 1), bufs[(per_q - 1) % 2])

    def far_body(u, c):
        for r in range(FAR_UNROLL):
            far_one(FAR_UNROLL * u + r + 1, (per_q + r + 1) % 2)
            far_two(FAR_UNROLL * u + r, (per_q + r) % 2)
        return c

    lax.fori_loop(0, n_far // FAR_UNROLL, far_body, 0)
    for t in range(n_far - n_far % FAR_UNROLL, n_far):
        if t + 1 < n_far:
            far_one(t + 1, (per_q + t + 1) % 2)
        far_two(t, (per_q + t) % 2)

    lp = lam_ref[...]
    lam = (jnp.exp(jnp.sum(lp[0:1] * lp[1:2], axis=-1, keepdims=True))
           - jnp.exp(jnp.sum(lp[2:3] * lp[3:4], axis=-1, keepdims=True)) + lam_init)
    a0 = acc_ref[0]
    a1 = acc_ref[1]
    o = (a0[0:DIFF_V_DIM] / a0[DIFF_V_DIM:DIFF_V_DIM + 1]
         - lam * (a1[0:DIFF_V_DIM] / a1[DIFF_V_DIM:DIFF_V_DIM + 1]))
    ms = jnp.mean(o * o, axis=0, keepdims=True)
    o = o * lax.rsqrt(ms + NORM_EPS) * sg_ref[...] * (1.0 - lam_init)
    o_ref[...] = o.T.astype(o_ref.dtype)


def _diff_attn(qa, ka, vt, lam_p, subln_g, *, lam_init, online):
    s = qa.shape[2]
    bq = ATT_BQ
    assert bq % ATT_BK == 0 and s > bq
    kern = functools.partial(_diff_attn_kernel, lam_init=lam_init, online=online)
    per_q = bq // ATT_BK
    key = np.arange(per_q * ATT_BK).reshape(per_q, ATT_BK, 1)
    dnear = jnp.asarray(-np.abs(np.arange(bq).reshape(1, 1, bq) - key), F32)
    scratch = [
        pltpu.VMEM((2, V_AUG, bq), F32),
        pltpu.VMEM((2, 3, bq, 128), BF16),
    ]
    if online:
        scratch += [
            pltpu.VMEM((2, 1, bq), F32),
            pltpu.VMEM((2, ATT_BK, bq), F32),
            pltpu.VMEM((2, ATT_BK, bq), F32),
            pltpu.VMEM((2, 1, bq), F32),
            pltpu.VMEM((2, 1, bq), F32),
        ]
    else:
        scratch += [pltpu.VMEM((2, ATT_BK, bq), BF16), pltpu.VMEM((2, ATT_BK, bq), BF16)]
    return pl.pallas_call(
        kern,
        grid=(DIFF_HEADS, s // bq),
        in_specs=[
            pl.BlockSpec((4, DIFF_QK_DIM), lambda h, i: (0, 0)),
            pl.BlockSpec((DIFF_V_DIM, 1), lambda h, i: (0, 0)),
            pl.BlockSpec((per_q, ATT_BK, bq), lambda h, i: (0, 0, 0)),
            pl.BlockSpec((None, 2, bq, 128), lambda h, i: (h, 0, i, 0)),
            pl.BlockSpec((None, 2, s, 128), lambda h, i: (h, 0, 0, 0)),
            pl.BlockSpec((None, V_AUG, s), lambda h, i: (h, 0, 0)),
        ],
        out_specs=pl.BlockSpec((bq, DIFF_V_DIM), lambda h, i: (i, h)),
        out_shape=jax.ShapeDtypeStruct((s, DIFF_DIM), BF16),
        scratch_shapes=scratch,
        compiler_params=_cparams(("parallel", "arbitrary")),
        name="diff_attn_online" if online else "diff_attn",
    )(lam_p, subln_g.reshape(DIFF_V_DIM, 1), dnear, qa, ka, vt)


def _shift_rows(x, prev_row, next_row):
    n = x.shape[0]
    row = lax.broadcasted_iota(jnp.int32, x.shape, 0)
    xp = jnp.where(row == 0, prev_row, pltpu.roll(x, 1, 0))
    xn = jnp.where(row == n - 1, next_row, pltpu.roll(x, n - 1, 0))
    return xp, xn


def _halo_rows(prev_ref, next_ref, i, n_blocks):
    hp = prev_ref.shape[0]
    prev_row = prev_ref[hp - 1:hp, :].astype(F32)
    next_row = next_ref[0:1, :].astype(F32)
    prev_row = jnp.where(i > 0, prev_row, 0.0)
    next_row = jnp.where(i < n_blocks - 1, next_row, 0.0)
    return prev_row, next_row


def _rwkv_prep_kernel(
        x_ref, xp_ref, xn_ref, l_ref, lp_ref, ln_ref,
        mu_ref, mul_ref, w0_ref, w2_ref, a0_ref, a2_ref, g2_ref, kk_ref, ka_ref, rk_ref,
        bd_ref, tri_ref,
        v_out, g_out, bonus_out, rb_out, kb_out, kt_out, bt_out, kh_out, bh_out, gc_out):
    i = pl.program_id(0)
    nb = pl.num_programs(0)
    tm = x_ref.shape[0]

    x = x_ref[...].astype(F32)
    prev_row, next_row = _halo_rows(xp_ref, xn_ref, i, nb)
    xp, xn = _shift_rows(x, prev_row, next_row)
    mu = mu_ref[...]
    x = x + mu[0:1] * (xp - x) + mu[1:2] * (xn - x)
    lo = l_ref[...].astype(F32)
    prev_row, next_row = _halo_rows(lp_ref, ln_ref, i, nb)
    lop, lon = _shift_rows(lo, prev_row, next_row)
    mul = mul_ref[...]
    lo = lo + mul[0:1] * (lop - lo) + mul[1:2] * (lon - lo)

    r = x[:, 0:RWKV_DIM]
    k = x[:, RWKV_DIM:2 * RWKV_DIM]
    v = x[:, 2 * RWKV_DIM:3 * RWKV_DIM]
    tw = jnp.tanh(lo[:, 0:128]).astype(BF16)
    la = lo[:, 128:256].astype(BF16)
    lg = jax.nn.sigmoid(lo[:, 256:512]).astype(BF16)

    g = _dot(lg, g2_ref[...])
    kk = k * kk_ref[...]
    ss = _group_mean_sq(kk, bd_ref, 1.0)
    kk = kk * lax.rsqrt(jnp.maximum(ss, 1e-24))
    bonus = _group_sum((r * k * rk_ref[...]).astype(BF16), bd_ref) * v

    g_out[...] = g.astype(g_out.dtype)
    bonus_out[...] = bonus.astype(bonus_out.dtype)
    v_out[...] = v.astype(BF16)

    tri = tri_ref[...]
    for d in range(2):
        wl = w0_ref[d:d + 1, :] + _dot(tw, w2_ref[d])
        logdec = -math.exp(-0.5) * jax.nn.sigmoid(wl)
        a = jax.nn.sigmoid(a0_ref[d:d + 1, :] + _dot(la, a2_ref[d]))
        k_d = k * (1.0 + (a - 1.0) * ka_ref[...])
        b_d = kk * a
        pre = _dot_exactish(tri, logdec)
        chunk_tot = jnp.broadcast_to(
            pre.reshape(tm // CHUNK, CHUNK, RWKV_DIM)[:, CHUNK - 1:CHUNK, :],
            (tm // CHUNK, CHUNK, RWKV_DIM)).reshape(tm, RWKV_DIM)
        suf = chunk_tot - pre
        if d == 0:
            lc, ex, rem = pre, pre - logdec, suf
        else:
            lc, ex, rem = suf + logdec, suf, pre - logdec
        e_neg = jnp.exp(-lc)
        e_rem = jnp.exp(rem)
        outs = (
            (rb_out, r * jnp.exp(lc)),
            (kb_out, kk * jnp.exp(ex)),
            (kt_out, k_d * e_neg),
            (bt_out, b_d * e_neg),
            (kh_out, k_d * e_rem),
            (bh_out, b_d * e_rem),
        )
        for ref, val in outs:
            ref[d] = val.astype(BF16)
        gc_out[d] = jnp.exp(chunk_tot).reshape(tm // 8, 8, RWKV_DIM)[:, 0, :]


def _rwkv_prep(p, mu_rkv, mu_lora, w0, w2p, a0, a2p, g2p, k_k, k_a, r_k, bd, tri, *, tm):
    s = p.shape[0]
    nb = s // tm
    hb = tm // 16
    last16 = s // 16 - 1
    rkv_w = 3 * RWKV_DIM
    assert COL_RKV % rkv_w == 0 and COL_LORA % LORA_PAD == 0
    c_rkv = COL_RKV // rkv_w
    c_lora = COL_LORA // LORA_PAD

    def prev_map(c):
        return lambda i: (jnp.maximum(i * hb - 1, 0), c)

    def next_map(c):
        return lambda i: (jnp.minimum((i + 1) * hb, last16), c)

    full = lambda *shape: pl.BlockSpec(shape, lambda i: (0,) * len(shape))
    hm = lambda: pl.BlockSpec((2, tm, RWKV_DIM), lambda i: (0, i, 0))
    hm_shape = jax.ShapeDtypeStruct((2, s, RWKV_DIM), BF16)
    return pl.pallas_call(
        _rwkv_prep_kernel,
        grid=(nb,),
        in_specs=[
            pl.BlockSpec((tm, rkv_w), lambda i: (i, c_rkv)),
            pl.BlockSpec((16, rkv_w), prev_map(c_rkv)),
            pl.BlockSpec((16, rkv_w), next_map(c_rkv)),
            pl.BlockSpec((tm, LORA_PAD), lambda i: (i, c_lora)),
            pl.BlockSpec((16, LORA_PAD), prev_map(c_lora)),
            pl.BlockSpec((16, LORA_PAD), next_map(c_lora)),
            full(2, rkv_w), full(2, LORA_PAD),
            full(2, RWKV_DIM), full(2, 128, RWKV_DIM),
            full(2, RWKV_DIM), full(2, 128, RWKV_DIM),
            full(256, RWKV_DIM),
            full(1, RWKV_DIM), full(1, RWKV_DIM), full(1, RWKV_DIM),
            full(GROUP_SLAB, GROUP_SLAB), full(tm, tm),
        ],
        out_specs=[
            pl.BlockSpec((tm, RWKV_DIM), lambda i: (i, 0)),
            pl.BlockSpec((tm, RWKV_DIM), lambda i: (i, 0)),
            pl.BlockSpec((tm, RWKV_DIM), lambda i: (i, 0)),
            hm(), hm(), hm(), hm(), hm(), hm(),
            pl.BlockSpec((2, tm // 8, RWKV_DIM), lambda i: (0, i, 0)),
        ],
        out_shape=[
            jax.ShapeDtypeStruct((s, RWKV_DIM), BF16),
            jax.ShapeDtypeStruct((s, RWKV_DIM), BF16),
            jax.ShapeDtypeStruct((s, RWKV_DIM), BF16),
            hm_shape, hm_shape, hm_shape, hm_shape, hm_shape, hm_shape,
            jax.ShapeDtypeStruct((2, s // 8, RWKV_DIM), F32),
        ],
        compiler_params=_cparams(("parallel",)),
        name="rwkv_prep",
    )(p, p, p, p, p, p, mu_rkv, mu_lora, w0, w2p, a0, a2p, g2p, k_k, k_a, r_k, bd, tri)


def _bdot(a, b):
    return lax.dot_general(a, b, (((2,), (1,)), ((0,), (0,))), preferred_element_type=F32)


def _bdot_nt(a, b):
    return lax.dot_general(a, b, (((2,), (2,)), ((0,), (0,))), preferred_element_type=F32)


def _bdot_tn(a, b):
    return lax.dot_general(a, b, (((1,), (1,)), ((0,), (0,))), preferred_element_type=F32)


WKV_CHUNKS_PER_STEP = 2


def _wkv_kernel(*refs):
    (vf_ref, vr_ref), ins, (gcf_ref, gcr_ref, yf_ref, yr_ref, s_ref) = refs[:2], refs[2:14], refs[14:]

    @pl.when(pl.program_id(0) == 0)
    def _():
        s_ref[...] = jnp.zeros_like(s_ref)

    def rows(ref, part, n):
        return ref.at[n * part:n * (part + 1), :]

    for part in range(WKV_CHUNKS_PER_STEP):
        back = WKV_CHUNKS_PER_STEP - 1 - part
        _wkv_chunk(False, rows(vf_ref, part, CHUNK), [rows(r, part, CHUNK) for r in ins[0:6]],
                   rows(gcf_ref, part, 8), rows(yf_ref, part, CHUNK), s_ref.at[0])
        _wkv_chunk(True, rows(vr_ref, back, CHUNK), [rows(r, back, CHUNK) for r in ins[6:12]],
                   rows(gcr_ref, back, 8), rows(yr_ref, back, CHUNK), s_ref.at[1])


WKV_GROUP = 2
GROUP_LANES = WKV_GROUP * RWKV_HEAD_DIM
N_GROUPS = RWKV_HEADS // WKV_GROUP


def _groups(x):
    return jnp.stack([x[:, GROUP_LANES * p:GROUP_LANES * (p + 1)] for p in range(N_GROUPS)])


def _head_of_lane(shape):
    return lax.broadcasted_iota(jnp.int32, shape, 2) // RWKV_HEAD_DIM


def _group_diag(y):
    head = _head_of_lane(y.shape)
    zero = jnp.zeros_like(y)
    return jnp.concatenate([jnp.where(head == h, y, zero) for h in range(WKV_GROUP)], axis=1)


def _group_tn(a, b):
    full = _bdot_tn(a, b)
    n = RWKV_HEAD_DIM
    head = _head_of_lane((N_GROUPS, n, GROUP_LANES))
    out = full[:, :n]
    for h in range(1, WKV_GROUP):
        out = jnp.where(head == h, full[:, n * h:n * (h + 1)], out)
    return out


def _wkv_chunk(reverse, v_ref, scaled_refs, gc_ref, y_ref, s_ref):
    c = CHUNK
    lanes = GROUP_LANES
    row = lax.broadcasted_iota(jnp.int32, (N_GROUPS, c, lanes), 1)
    col = lax.broadcasted_iota(jnp.int32, (N_GROUPS, c, lanes), 2) % RWKV_HEAD_DIM
    strict = (row < col) if reverse else (row > col)
    incl = (row <= col) if reverse else (row >= col)

    v = _groups(v_ref[...])
    rb, kb, kt, bt, kh, bh = [_groups(r[...]) for r in scaled_refs]

    kr = jnp.concatenate([kb, rb], axis=1)
    sk = _bdot_nt(kr, _group_diag(kt))
    sb = _bdot_nt(kr, _group_diag(bt))
    a_kk = jnp.where(strict, sk[:, :c], 0.0)
    a_rk = jnp.where(incl, sk[:, c:], 0.0)
    l_mat = jnp.where(strict, sb[:, :c], 0.0)
    a_rb = jnp.where(incl, sb[:, c:], 0.0)

    eye = jnp.where(row == col, 1.0, 0.0)
    t_inv = eye - l_mat
    l_b = l_mat.astype(BF16)
    pw = _bdot(l_b, _group_diag(l_b))
    for step in range(5):
        pw_b = _group_diag(pw.astype(BF16))
        if step < 4:
            both = _bdot(jnp.concatenate([t_inv, pw], axis=1).astype(BF16), pw_b)
            t_inv = t_inv + both[:, :c]
            pw = both[:, c:]
        else:
            t_inv = t_inv + _bdot(t_inv.astype(BF16), pw_b)

    q12 = _bdot(jnp.concatenate([a_kk, a_rk], axis=1).astype(BF16), _group_diag(v))
    q1, q2 = q12[:, :c], q12[:, c:]
    t_b = t_inv.astype(BF16)
    kq = jnp.concatenate([_group_diag(kb), _group_diag(q1.astype(BF16))], axis=2)
    wkuv = _bdot(t_b, kq).astype(BF16)
    wk, uv = wkuv[:, :, :lanes], wkuv[:, :, lanes:]
    corr = _bdot(a_rb.astype(BF16), jnp.concatenate([_group_diag(wk), _group_diag(uv)], axis=2))
    rw = rb.astype(F32) - corr[:, :, :lanes]
    y0 = q2 - corr[:, :, lanes:]
    m_mat = _group_tn(wk, bh)
    j_t = _group_tn(v, kh) - _group_tn(uv, bh)

    n = RWKV_HEAD_DIM
    s_old = s_ref[...]
    s_hi = s_old.astype(BF16)
    s_lo = (s_old - s_hi.astype(F32)).astype(BF16)
    y = _bdot_nt(rw.astype(BF16), _group_diag(s_hi)) + y0
    sm = _bdot(jnp.concatenate([s_hi, s_lo], axis=1), _group_diag(m_mat.astype(BF16)))
    s_ref[...] = s_old * _groups(gc_ref[0:1, :]) - (sm[:, :n] + sm[:, n:]) + j_t
    for p in range(N_GROUPS):
        y_ref[:, lanes * p:lanes * (p + 1)] = y[p]


def _wkv(v, rb, kb, kt, bt, kh, bh, gc):
    s = v.shape[0]
    per = WKV_CHUNKS_PER_STEP
    assert s % (per * CHUNK) == 0
    nc = s // (per * CHUNK)
    scaled = (rb, kb, kt, bt, kh, bh)
    fwd = lambda rows: pl.BlockSpec((None, rows, RWKV_DIM), lambda c: (0, c, 0))
    rev = lambda rows: pl.BlockSpec((None, rows, RWKV_DIM), lambda c: (1, nc - 1 - c, 0))
    y_shape = jax.ShapeDtypeStruct((s, RWKV_DIM), F32)
    return pl.pallas_call(
        _wkv_kernel,
        grid=(nc,),
        in_specs=[
            pl.BlockSpec((per * CHUNK, RWKV_DIM), lambda c: (c, 0)),
            pl.BlockSpec((per * CHUNK, RWKV_DIM), lambda c: (nc - 1 - c, 0)),
            *[fwd(per * CHUNK) for _ in scaled], *[rev(per * CHUNK) for _ in scaled],
            fwd(per * 8), rev(per * 8),
        ],
        out_specs=[
            pl.BlockSpec((per * CHUNK, RWKV_DIM), lambda c: (c, 0)),
            pl.BlockSpec((per * CHUNK, RWKV_DIM), lambda c: (nc - 1 - c, 0)),
        ],
        out_shape=[y_shape, y_shape],
        scratch_shapes=[pltpu.VMEM((2, N_GROUPS, RWKV_HEAD_DIM, GROUP_LANES), F32)],
        compiler_params=_cparams(("arbitrary",)),
        name="wkv7_chunked",
    )(v, v, *scaled, *scaled, gc, gc)


def _rwkv_post_kernel(yf_ref, yr_ref, g_ref, bonus_ref, lg_ref, lb_ref, bd_ref, o_ref):
    y = yf_ref[...] + yr_ref[...]
    hi, lo = _split2(y)
    mean = (_group_sum(hi, bd_ref) + _group_sum(lo, bd_ref)) * (1.0 / RWKV_HEAD_DIM)
    yc = y - mean
    var = _group_mean_sq(yc, bd_ref, RWKV_HEAD_DIM)
    yn = yc * lax.rsqrt(var + LNX_EPS)
    out = (yn * lg_ref[...] + lb_ref[...] + bonus_ref[...].astype(F32)) * g_ref[...].astype(F32)
    o_ref[...] = out.astype(o_ref.dtype)


def _rwkv_post(y_fwd, y_rev, g, bonus, lnx_g, lnx_b, bd, *, tm):
    s = g.shape[0]
    return pl.pallas_call(
        _rwkv_post_kernel,
        grid=(s // tm,),
        in_specs=[
            pl.BlockSpec((tm, RWKV_DIM), lambda i: (i, 0)),
            pl.BlockSpec((tm, RWKV_DIM), lambda i: (i, 0)),
            pl.BlockSpec((tm, RWKV_DIM), lambda i: (i, 0)),
            pl.BlockSpec((tm, RWKV_DIM), lambda i: (i, 0)),
            pl.BlockSpec((1, RWKV_DIM), lambda i: (0, 0)),
            pl.BlockSpec((1, RWKV_DIM), lambda i: (0, 0)),
            pl.BlockSpec((GROUP_SLAB, GROUP_SLAB), lambda i: (0, 0)),
        ],
        out_specs=pl.BlockSpec((tm, RWKV_DIM), lambda i: (i, 0)),
        out_shape=jax.ShapeDtypeStruct((s, RWKV_DIM), BF16),
        compiler_params=_cparams(("parallel",)),
        name="rwkv_post",
    )(y_fwd, y_rev, g, bonus, lnx_g.reshape(1, RWKV_DIM), lnx_b.reshape(1, RWKV_DIM), bd)


def _mem_attn_kernel(q_ref, kv_ref, gq_ref, gk_ref, o_ref):
    for h in range(MEM_HEADS):
        sl = slice(MEM_HEAD_DIM * h, MEM_HEAD_DIM * (h + 1))
        q = q_ref[:, sl].astype(F32)
        q = q * lax.rsqrt(jnp.mean(q * q, axis=-1, keepdims=True) + NORM_EPS)
        q = q * gq_ref[...] * (MEM_HEAD_DIM ** -0.5)
        km = kv_ref[:, sl].astype(F32)
        km = km * lax.rsqrt(jnp.mean(km * km, axis=-1, keepdims=True) + NORM_EPS)
        km = km * gk_ref[...]
        vm = kv_ref[:, MEM_DIM + MEM_HEAD_DIM * h:MEM_DIM + MEM_HEAD_DIM * (h + 1)]
        s = lax.dot_general(q.astype(BF16), km.astype(BF16), (((1,), (1,)), ((), ())),
                            preferred_element_type=F32)
        s = s - jnp.max(s, axis=-1, keepdims=True)
        e = jnp.exp(s)
        pr = e / jnp.sum(e, axis=-1, keepdims=True)
        o_ref[:, sl] = _dot(pr.astype(BF16), vm).astype(o_ref.dtype)


def _mem_attn(p, kv, gq, gk, *, tm):
    s = p.shape[0]
    assert COL_MEM % MEM_DIM == 0
    return pl.pallas_call(
        _mem_attn_kernel,
        grid=(s // tm,),
        in_specs=[
            pl.BlockSpec((tm, MEM_DIM), lambda i: (i, COL_MEM // MEM_DIM)),
            pl.BlockSpec((N_MEM, 2 * MEM_DIM), lambda i: (0, 0)),
            pl.BlockSpec((1, MEM_HEAD_DIM), lambda i: (0, 0)),
            pl.BlockSpec((1, MEM_HEAD_DIM), lambda i: (0, 0)),
        ],
        out_specs=pl.BlockSpec((tm, MEM_DIM), lambda i: (i, 0)),
        out_shape=jax.ShapeDtypeStruct((s, MEM_DIM), BF16),
        compiler_params=_cparams(("parallel",)),
        name="mem_attn",
    )(p, kv, gq, gk)


def _merge_kernel(o0_ref, o1_ref, o2_ref, g0_ref, g1_ref, g2_ref, w_ref, m_ref):
    def gate(g_ref):
        return 0.5 * jnp.tanh(0.5 * g_ref[...].astype(F32)) + 0.5

    acc = gate(g0_ref) * _dot(o0_ref[...], w_ref[0])
    acc = acc + gate(g1_ref) * _dot(o1_ref[...], w_ref[1])
    acc = acc + gate(g2_ref) * _dot(o2_ref[...], w_ref[2])
    m_ref[...] = acc.astype(m_ref.dtype)


def _merge(o_diff, o_rwkv, o_mem, p, w_branch, layer, *, tm, tn):
    s = p.shape[0]
    assert COL_GATE % tn == 0 and D_MODEL % tn == 0
    gate_blk = COL_GATE // tn
    per = D_MODEL // tn
    o_spec = lambda: pl.BlockSpec((tm, 1024), lambda i, j: (i, 0))
    g_spec = lambda b: pl.BlockSpec((tm, tn), lambda i, j: (i, gate_blk + b * per + j))
    return pl.pallas_call(
        _merge_kernel,
        grid=(s // tm, per),
        in_specs=[
            o_spec(), o_spec(), o_spec(),
            g_spec(0), g_spec(1), g_spec(2),
            pl.BlockSpec((None, N_BRANCH, 1024, tn), lambda i, j: (layer, 0, 0, j)),
        ],
        out_specs=pl.BlockSpec((tm, tn), lambda i, j: (i, j)),
        out_shape=jax.ShapeDtypeStruct((s, D_MODEL), BF16),
        compiler_params=_cparams(("parallel", "arbitrary")),
        name="merge",
    )(o_diff, o_rwkv, o_mem, p, p, p, w_branch)


GLU_TILE = 256


def _ffn_up_glu_kernel(x_ref, xp_ref, xn_ref, g_ref, wg_ref, wv_ref, cw_ref, cb_ref, o_ref,
                       h_ref, halo_ref):
    i = pl.program_id(0)
    nb = pl.num_programs(0)

    def norm(x):
        ms = jnp.mean(x * x, axis=-1, keepdims=True)
        return x * lax.rsqrt(ms + NORM_EPS) * g_ref[...]

    @pl.when(pl.program_id(1) == 0)
    def _():
        h_ref[...] = norm(x_ref[...]).astype(BF16)
        row = lax.broadcasted_iota(jnp.int32, xp_ref.shape, 0)
        before = jnp.where(i > 0, pltpu.roll(norm(xp_ref[...]), 1, 0), 0.0)
        after = jnp.where(i < nb - 1, pltpu.roll(norm(xn_ref[...]), 1, 0), 0.0)
        halo_ref[...] = jnp.where(row == 0, before, jnp.where(row == 1, after, 0.0)).astype(BF16)

    h = h_ref[...]
    halo = halo_ref[...]
    tn = o_ref.shape[1]
    for c0 in range(0, tn, GLU_TILE):
        sl = slice(c0, c0 + GLU_TILE)
        wg = wg_ref[:, sl]
        ug = _dot(h, wg)
        uv = _dot(h, wv_ref[:, sl])
        edge = _dot(halo, wg)
        gp, gn = _shift_rows(ug, edge[0:1], edge[1:2])
        cw = cw_ref[:, sl]
        conv = cw[0:1] * gp + cw[1:2] * ug + cw[2:3] * gn + cb_ref[:, sl]
        half = 0.5 * conv
        o_ref[:, sl] = ((half + half * jnp.tanh(half)) * uv).astype(o_ref.dtype)


def _ffn_up_glu(x, g, w_up, layer, conv_w, conv_b, *, tm, tn):
    s, k = x.shape
    assert tn % GLU_TILE == 0
    nj = D_FF // tn
    hb = tm // 16
    last16 = s // 16 - 1
    return pl.pallas_call(
        _ffn_up_glu_kernel,
        grid=(s // tm, nj),
        in_specs=[
            pl.BlockSpec((tm, k), lambda i, j: (i, 0)),
            pl.BlockSpec((16, k), lambda i, j: (jnp.maximum(i * hb - 1, 0), 0)),
            pl.BlockSpec((16, k), lambda i, j: (jnp.minimum((i + 1) * hb, last16), 0)),
            pl.BlockSpec((1, k), lambda i, j: (0, 0)),
            pl.BlockSpec((None, k, tn), lambda i, j: (layer, 0, j)),
            pl.BlockSpec((None, k, tn), lambda i, j: (layer, 0, nj + j)),
            pl.BlockSpec((3, tn), lambda i, j: (0, j)),
            pl.BlockSpec((1, tn), lambda i, j: (0, j)),
        ],
        out_specs=pl.BlockSpec((tm, tn), lambda i, j: (i, j)),
        out_shape=jax.ShapeDtypeStruct((s, D_FF), BF16),
        scratch_shapes=[pltpu.VMEM((tm, k), BF16), pltpu.VMEM((16, k), BF16)],
        compiler_params=_cparams(("parallel", "arbitrary")),
        name="ffn_up_glu",
    )(x, x, x, g.reshape(1, k), w_up, w_up, conv_w, conv_b.reshape(1, D_FF))


def _block_ones(n, group):
    idx = np.arange(n) // group
    return jnp.asarray(idx[:, None] == idx[None, :], BF16)


def _chunk_tri(n):
    idx = np.arange(n)
    same = (idx[:, None] // CHUNK) == (idx[None, :] // CHUNK)
    return jnp.asarray(same & (idx[:, None] >= idx[None, :]), BF16)


def _pad_rows(w, rows_before, total):
    n = w.shape[-1]
    out = jnp.zeros((total, n), w.dtype)
    return lax.dynamic_update_slice(out, w, (rows_before, 0))


def _pad_w_in(w):
    lora0 = COL_RKV + 3 * RWKV_DIM
    mem0 = lora0 + LORA_COLS
    gate0 = mem0 + MEM_DIM
    pad = jnp.zeros((w.shape[0], LORA_PAD - LORA_COLS), w.dtype)
    parts = [w[:, :lora0], w[:, gate0:], w[:, mem0:gate0], w[:, lora0:mem0], pad]
    return jnp.concatenate(parts, axis=1).astype(BF16)


def _tile_plan(s):
    rows = 1024 if s % 1024 == 0 else 512
    return dict(
        w_in=dict(tm=rows, tn=1536),
        mem_kv=dict(tm=N_MEM, tn=1024),
        diff_prep=256, rwkv_prep=256, rwkv_post=256, mem_attn=512,
        merge=dict(tm=256, tn=D_MODEL),
        w_out=dict(tm=512, tn=D_MODEL),
        ffn_up=dict(tm=rows, tn=512),
        ffn_down=dict(tm=rows, tn=512),
    )


def kernel(x, mem, attn_norm_g, w_in, diff_qk_g, diff_lambda, diff_subln_g, rwkv_mu, rwkv_w0,
           rwkv_w2, rwkv_a0, rwkv_a2, rwkv_g2, rwkv_k_k, rwkv_k_a, rwkv_r_k, rwkv_lnx_g,
           rwkv_lnx_b, mem_norm_g, w_mem_kv, mem_qk_g, w_branch, w_out, ffn_norm_g, w_ffn_up,
           ffn_conv_w, ffn_conv_b, w_ffn_down):
    b, s, d = x.shape
    assert b == 1 and d == D_MODEL and s % ATT_BQ == 0
    xs = x.reshape(s, d)
    mem2 = mem.reshape(N_MEM, d)
    tiles = _tile_plan(s)
    bd64 = _block_ones(GROUP_SLAB, 64)
    tri = _chunk_tri(tiles["rwkv_prep"])
    n_rwkv_main = 3 * RWKV_DIM
    w_mem_kv_b = w_mem_kv.astype(BF16)
    w_branch_b = w_branch.astype(BF16)
    w_out_b = w_out.astype(BF16)
    w_ffn_up_b = w_ffn_up.astype(BF16)
    w_ffn_down_b = w_ffn_down.astype(BF16)

    for l in range(DEPTH):
        lam_init = 0.8 - 0.6 * math.exp(-0.3 * l)
        p = _rms_mm(xs, attn_norm_g[l], _pad_w_in(w_in[l])[None], 0, name="rms_w_in",
                    **tiles["w_in"])

        gq = jnp.tile(diff_qk_g[l, 0].reshape(1, 128), (1, DIFF_HEADS))
        gk = jnp.tile(diff_qk_g[l, 1].reshape(1, 128), (1, DIFF_HEADS))
        qc, kc, score_bound = _attn_consts(diff_qk_g[l])
        qa, ka, vt = _diff_prep(p, gq, gk, qc, kc, bd64, tm=tiles["diff_prep"])
        o_diff = lax.cond(
            score_bound <= SCORE_BOUND_MAX,
            functools.partial(_diff_attn, lam_init=lam_init, online=False),
            functools.partial(_diff_attn, lam_init=lam_init, online=True),
            qa, ka, vt, diff_lambda[l], diff_subln_g[l])

        mu = rwkv_mu[l]
        mu_rkv = mu[:, :n_rwkv_main]
        mu_lora = jnp.pad(mu[:, n_rwkv_main:], ((0, 0), (0, LORA_PAD - LORA_COLS)))
        w2p = jnp.stack([_pad_rows(rwkv_w2[l, dd], 64 * dd, 128) for dd in range(2)]).astype(BF16)
        a2p = jnp.stack([_pad_rows(rwkv_a2[l, dd], 64 * dd, 128) for dd in range(2)]).astype(BF16)
        g2p = _pad_rows(rwkv_g2[l], 0, 256).astype(BF16)
        (v_h, g_tok, bonus, rb, kb, kt, bt, kh, bh, gc) = _rwkv_prep(
            p, mu_rkv, mu_lora, rwkv_w0[l], w2p, rwkv_a0[l], a2p, g2p,
            rwkv_k_k[l].reshape(1, RWKV_DIM), rwkv_k_a[l].reshape(1, RWKV_DIM),
            rwkv_r_k[l].reshape(1, RWKV_DIM), bd64, tri, tm=tiles["rwkv_prep"])
        y_fwd, y_rev = _wkv(v_h, rb, kb, kt, bt, kh, bh, gc)
        o_rwkv = _rwkv_post(y_fwd, y_rev, g_tok, bonus, rwkv_lnx_g[l], rwkv_lnx_b[l], bd64,
                            tm=tiles["rwkv_post"])

        kv = _rms_mm(mem2, mem_norm_g[l], w_mem_kv_b, l, name="rms_mem_kv", **tiles["mem_kv"])
        o_mem = _mem_attn(p, kv, mem_qk_g[l, 0].reshape(1, MEM_HEAD_DIM),
                          mem_qk_g[l, 1].reshape(1, MEM_HEAD_DIM), tm=tiles["mem_attn"])

        merged = _merge(o_diff, o_rwkv, o_mem, p, w_branch_b, l, **tiles["merge"])
        xs = _mm_res(merged, w_out_b, l, xs, name="w_out_res", **tiles["w_out"])

        act = _ffn_up_glu(xs, ffn_norm_g[l], w_ffn_up_b, l, ffn_conv_w[l], ffn_conv_b[l],
                          **tiles["ffn_up"])
        xs = _mm_res(act, w_ffn_down_b, l, xs, name="ffn_down_res", **tiles["ffn_down"])

    return xs.reshape(b, s, d)
```

```python
import functools
import math

import jax
import jax.numpy as jnp
import numpy as np
from jax import lax
from jax.experimental import pallas as pl
from jax.experimental.pallas import tpu as pltpu

F32 = jnp.float32
BF16 = jnp.bfloat16

D_MODEL = 2048
DEPTH = 2
DIFF_HEADS = 8
DIFF_QK_DIM = 64
DIFF_V_DIM = 128
DIFF_DIM = 1024
RWKV_HEADS = 16
RWKV_HEAD_DIM = 64
RWKV_DIM = 1024
DECAY_LORA = 64
AAA_LORA = 64
GATE_LORA = 160
LORA_COLS = 2 * DECAY_LORA + 2 * AAA_LORA + GATE_LORA
LORA_PAD = 512
N_MEM = 256
MEM_HEADS = 4
MEM_HEAD_DIM = 256
MEM_DIM = 1024
N_BRANCH = 3
D_FF = 5632
NORM_EPS = 1e-6
LNX_EPS = 64e-5

COL_DIFF = 0
COL_RKV = 3 * DIFF_DIM
COL_GATE = COL_RKV + 3 * RWKV_DIM
COL_MEM = COL_GATE + N_BRANCH * D_MODEL
COL_LORA = COL_MEM + MEM_DIM
N_IN_PAD = COL_LORA + LORA_PAD

CHUNK = 64
LOG2E = 1.4426950408889634
N_POS_COLS = 12
SCORE_BOUND_MAX = 30.0
ATT_BQ = 1024
ATT_BK = 512
V_AUG = 144
FAR_UNROLL = 10
VMEM_LIMIT = 56 * 1024 * 1024


def _cparams(sem):
    return pltpu.CompilerParams(dimension_semantics=sem, vmem_limit_bytes=VMEM_LIMIT)


def _split2(x):
    hi = x.astype(BF16)
    return hi, (x - hi.astype(F32)).astype(BF16)


def _dot(a, b):
    return jnp.dot(a, b, preferred_element_type=F32)


def _dot_exactish(a_bf16_exact, x_f32):
    hi, lo = _split2(x_f32)
    return _dot(a_bf16_exact, hi) + _dot(a_bf16_exact, lo)


def _rms_mm_kernel(x_ref, g_ref, w_ref, o_ref, h_ref):
    @pl.when(pl.program_id(1) == 0)
    def _():
        x = x_ref[...]
        ms = jnp.mean(x * x, axis=-1, keepdims=True)
        h_ref[...] = (x * lax.rsqrt(ms + NORM_EPS) * g_ref[...]).astype(BF16)

    o_ref[...] = _dot(h_ref[...], w_ref[...]).astype(o_ref.dtype)


def _rms_mm(x, g, w, layer, *, tm, tn, name):
    m, k = x.shape
    n = w.shape[2]
    return pl.pallas_call(
        _rms_mm_kernel,
        grid=(m // tm, n // tn),
        in_specs=[
            pl.BlockSpec((tm, k), lambda i, j: (i, 0)),
            pl.BlockSpec((1, k), lambda i, j: (0, 0)),
            pl.BlockSpec((None, k, tn), lambda i, j: (layer, 0, j)),
        ],
        out_specs=pl.BlockSpec((tm, tn), lambda i, j: (i, j)),
        out_shape=jax.ShapeDtypeStruct((m, n), BF16),
        scratch_shapes=[pltpu.VMEM((tm, k), BF16)],
        compiler_params=_cparams(("parallel", "arbitrary")),
        name=name,
    )(x, g.reshape(1, k), w)


def _mm_res_kernel(a_ref, w_ref, r_ref, o_ref):
    o_ref[...] = r_ref[...] + _dot(a_ref[...], w_ref[...])


def _mm_res(a, w, layer, res, *, tm, tn, name):
    m, k = a.shape
    n = w.shape[2]
    return pl.pallas_call(
        _mm_res_kernel,
        grid=(m // tm, n // tn),
        in_specs=[
            pl.BlockSpec((tm, k), lambda i, j: (i, 0)),
            pl.BlockSpec((None, k, tn), lambda i, j: (layer, 0, j)),
            pl.BlockSpec((tm, tn), lambda i, j: (i, j)),
        ],
        out_specs=pl.BlockSpec((tm, tn), lambda i, j: (i, j)),
        out_shape=jax.ShapeDtypeStruct((m, n), F32),
        compiler_params=_cparams(("parallel", "arbitrary")),
        name=name,
    )(a, w, res)


GROUP_SLAB = 256


def _group_sum(xb, bd_ref):
    bd = bd_ref[...]
    n = xb.shape[1]
    slabs = [_dot(xb[:, q:q + GROUP_SLAB], bd) for q in range(0, n, GROUP_SLAB)]
    return jnp.concatenate(slabs, axis=1)


def _group_mean_sq(x, bd_ref, group):
    return _group_sum((x * x).astype(BF16), bd_ref) * (1.0 / group)


def _aug_base(mp):
    return 64 if mp == 0 else 0


def _diff_prep_kernel(q_ref, k_ref, v_ref, gq_ref, gk_ref, qc_ref, kc_ref, bd_ref,
                      qa_ref, ka_ref, vt_ref):
    tm = q_ref.shape[0]
    row0 = pl.program_id(0) * tm
    lane = lax.broadcasted_iota(jnp.int32, (tm, 128), 1)
    pos = row0 + lax.broadcasted_iota(jnp.int32, (tm, 128), 0)
    pos_lo = (pos & 127).astype(F32)
    pos_hi = (pos >> 7).astype(F32)

    q = q_ref[...].astype(F32)
    qn = q * lax.rsqrt(_group_mean_sq(q, bd_ref, DIFF_QK_DIM) + NORM_EPS)
    qn = qn * gq_ref[...] * (DIFF_QK_DIM ** -0.5 * LOG2E)
    k = k_ref[...].astype(F32)
    kn = k * lax.rsqrt(_group_mean_sq(k, bd_ref, DIFF_QK_DIM) + NORM_EPS)
    kn = kn * gk_ref[...]

    for h in range(DIFF_HEADS):
        qh = qn[:, 128 * h:128 * (h + 1)]
        kh = kn[:, 128 * h:128 * (h + 1)]
        for mp in range(2):
            a0 = _aug_base(mp)
            g = 2 * h + mp
            is_data = (lane < 64) if mp == 0 else (lane >= 64)
            aug_q = jnp.where(lane < a0 + 3, pos_lo,
                              jnp.where(lane < a0 + 6, pos_hi, qc_ref[g:g + 1, :]))
            aug_k = jnp.where((lane >= a0 + 6) & (lane < a0 + 9), pos_lo,
                              jnp.where((lane >= a0 + 9) & (lane < a0 + 12), pos_hi,
                                        kc_ref[g:g + 1, :]))
            qa_ref[h, mp] = jnp.where(is_data, qh, aug_q).astype(BF16)
            ka_ref[h, mp] = jnp.where(is_data, kh, aug_k).astype(BF16)

    vt = v_ref[...].astype(F32).T
    sub = lax.broadcasted_iota(jnp.int32, (V_AUG - DIFF_V_DIM, tm), 0)
    ones_rows = jnp.where(sub == 0, 1.0, 0.0).astype(BF16)
    for h in range(DIFF_HEADS):
        vt_ref[h, 0:DIFF_V_DIM, :] = vt[128 * h:128 * (h + 1), :].astype(BF16)
        vt_ref[h, DIFF_V_DIM:V_AUG, :] = ones_rows


def _bf16_split3_const(x):
    parts = []
    for _ in range(3):
        part = float(np.asarray(x, np.float32).astype(BF16).astype(np.float32))
        parts.append(part)
        x = x - part
    return parts


def _attn_consts(qk_g):
    l_parts = _bf16_split3_const(LOG2E)
    m_nat = 8.0 * jnp.max(jnp.abs(qk_g[0]), axis=-1) * jnp.max(jnp.abs(qk_g[1]), axis=-1)
    qc = np.zeros((2 * DIFF_HEADS, 128), np.float32)
    kc = np.zeros((2 * DIFF_HEADS, 128), np.float32)
    bound_lane = np.zeros((2, 2 * DIFF_HEADS, 128), np.float32)
    for h in range(DIFF_HEADS):
        slope = 2.0 ** (-(h + 1))
        for mp in range(2):
            a0 = _aug_base(mp)
            g = 2 * h + mp
            for t, lp in enumerate(l_parts):
                qc[g, a0 + 6 + t] = slope * lp
                qc[g, a0 + 9 + t] = 128.0 * slope * lp
                kc[g, a0 + t] = -slope * lp
                kc[g, a0 + 3 + t] = -128.0 * slope * lp
            bound_lane[mp, g, a0 + N_POS_COLS] = 1.0
            kc[g, a0 + N_POS_COLS] = 1.0
    m2 = -m_nat * LOG2E
    qc = qc + m2[0] * bound_lane[0] + m2[1] * bound_lane[1]
    return qc, jnp.asarray(kc), jnp.max(m_nat)


def _diff_prep(p, gq, gk, qc, kc, bd, *, tm):
    s = p.shape[0]
    return pl.pallas_call(
        _diff_prep_kernel,
        grid=(s // tm,),
        in_specs=[
            pl.BlockSpec((tm, DIFF_DIM), lambda i: (i, 0)),
            pl.BlockSpec((tm, DIFF_DIM), lambda i: (i, 1)),
            pl.BlockSpec((tm, DIFF_DIM), lambda i: (i, 2)),
            pl.BlockSpec((1, DIFF_DIM), lambda i: (0, 0)),
            pl.BlockSpec((1, DIFF_DIM), lambda i: (0, 0)),
            pl.BlockSpec((2 * DIFF_HEADS, 128), lambda i: (0, 0)),
            pl.BlockSpec((2 * DIFF_HEADS, 128), lambda i: (0, 0)),
            pl.BlockSpec((GROUP_SLAB, GROUP_SLAB), lambda i: (0, 0)),
        ],
        out_specs=[
            pl.BlockSpec((DIFF_HEADS, 2, tm, 128), lambda i: (0, 0, i, 0)),
            pl.BlockSpec((DIFF_HEADS, 2, tm, 128), lambda i: (0, 0, i, 0)),
            pl.BlockSpec((DIFF_HEADS, V_AUG, tm), lambda i: (0, 0, i)),
        ],
        out_shape=[
            jax.ShapeDtypeStruct((DIFF_HEADS, 2, s, 128), BF16),
            jax.ShapeDtypeStruct((DIFF_HEADS, 2, s, 128), BF16),
            jax.ShapeDtypeStruct((DIFF_HEADS, V_AUG, s), BF16),
        ],
        compiler_params=_cparams(("parallel",)),
        name="diff_prep",
    )(p, p, p, gq, gk, qc, kc, bd)


def _diff_attn_kernel(lam_ref, sg_ref, dnear_ref, qa_ref, ka_ref, vt_ref, o_ref,
                      acc_ref, qv_ref, *mode_refs, lam_init, online):
    h = pl.program_id(0)
    i = pl.program_id(1)
    bq = qa_ref.shape[1]
    s_len = ka_ref.shape[1]
    bk = ATT_BK
    n_chunks = s_len // bk
    per_q = bq // bk
    n_far = n_chunks - per_q
    j_lo = i * per_q
    q0 = i * bq
    slope2 = jnp.exp2(-jnp.full((1, 1), h + 1, jnp.int32).astype(F32)) * LOG2E
    if online:
        m_ref, sa_ref, sb_ref, mxa_ref, mxb_ref = mode_refs
        m_ref[...] = jnp.full_like(m_ref, -1e30)
        bufs = ((sa_ref, mxa_ref), (sb_ref, mxb_ref))
    else:
        bufs = mode_refs
    acc_ref[...] = jnp.zeros_like(acc_ref)

    lane = lax.broadcasted_iota(jnp.int32, (1, 128), 1)
    for mp in range(2):
        q = qa_ref[mp]
        pos_lane = (lane >= _aug_base(mp)) & (lane < _aug_base(mp) + N_POS_COLS)
        qv_ref[mp, 0] = q
        qv_ref[mp, 1] = jnp.where(pos_lane, -q, q)
        qv_ref[mp, 2] = jnp.where(pos_lane, jnp.zeros_like(q), q)

    def stage_one(k0, variant, bias, buf):
        for mp in range(2):
            kc = ka_ref[mp, pl.ds(k0, bk), :]
            s = lax.dot_general(kc, qv_ref[mp, variant], (((1,), (1,)), ((), ())),
                                preferred_element_type=F32)
            if bias is not None:
                s = s + bias
            if online:
                buf[0][mp] = s
                buf[1][mp] = jnp.max(s, axis=0, keepdims=True)
            else:
                buf[mp] = jnp.exp2(s).astype(BF16)

    def stage_two(k0, buf):
        vt_c = vt_ref[:, pl.ds(k0, bk)]
        for mp in range(2):
            if online:
                m_old = m_ref[mp]
                m_new = jnp.maximum(m_old, buf[1][mp])
                p = jnp.exp2(buf[0][mp] - m_new).astype(BF16)
                acc_ref[mp] = acc_ref[mp] * jnp.exp2(m_old - m_new) + _dot(vt_c, p)
                m_ref[mp] = m_new
            else:
                acc_ref[mp] += _dot(vt_c, buf[mp])

    def near_start(t):
        return pl.multiple_of(q0 + t * bk, bk)

    def far_start(t):
        t = jnp.minimum(t, n_far - 1)
        j = jnp.where(t < j_lo, t, t + per_q)
        return pl.multiple_of(j * bk, bk), (t >= j_lo).astype(jnp.int32)

    def near_one(t):
        stage_one(near_start(t), 2, slope2 * dnear_ref[t], bufs[t % 2])

    def far_one(t, parity):
        k0, after = far_start(t)
        stage_one(k0, after, None, bufs[parity])

    def far_two(t, parity):
        stage_two(far_start(t)[0], bufs[parity])

    near_one(0)
    for t in range(1, per_q):
        near_one(t)
        stage_two(near_start(t - 1), bufs[(t - 1) % 2])
    far_one(0, per_q % 2)
    stage_two(near_start(per_q - 1), bufs[(per_q - 1) % 2])

    def far_body(u, c):
        for r in range(FAR_UNROLL):
            far_one(FAR_UNROLL * u + r + 1, (per_q + r + 1) % 2)
            far_two(FAR_UNROLL * u + r, (per_q + r) % 2)
        return c

    lax.fori_loop(0, n_far // FAR_UNROLL, far_body, 0)
    for t in range(n_far - n_far % FAR_UNROLL, n_far):
        if t + 1 < n_far:
            far_one(t + 1, (per_q + t + 1) % 2)
        far_two(t, (per_q + t) % 2)

    lp = lam_ref[...]
    lam = (jnp.exp(jnp.sum(lp[0:1] * lp[1:2], axis=-1, keepdims=True))
           - jnp.exp(jnp.sum(lp[2:3] * lp[3:4], axis=-1, keepdims=True)) + lam_init)
    a0 = acc_ref[0]
    a1 = acc_ref[1]
    o = (a0[0:DIFF_V_DIM] / a0[DIFF_V_DIM:DIFF_V_DIM + 1]
         - lam * (a1[0:DIFF_V_DIM] / a1[DIFF_V_DIM:DIFF_V_DIM + 1]))
    ms = jnp.mean(o * o, axis=0, keepdims=True)
    o = o * lax.rsqrt(ms + NORM_EPS) * sg_ref[...] * (1.0 - lam_init)
    o_ref[...] = o.T.astype(o_ref.dtype)


def _diff_attn(qa, ka, vt, lam_p, subln_g, *, lam_init, online):
    s = qa.shape[2]
    bq = ATT_BQ
    assert bq % ATT_BK == 0 and s > bq
    kern = functools.partial(_diff_attn_kernel, lam_init=lam_init, online=online)
    per_q = bq // ATT_BK
    key = np.arange(per_q * ATT_BK).reshape(per_q, ATT_BK, 1)
    dnear = jnp.asarray(-np.abs(np.arange(bq).reshape(1, 1, bq) - key), F32)
    scratch = [
        pltpu.VMEM((2, V_AUG, bq), F32),
        pltpu.VMEM((2, 3, bq, 128), BF16),
    ]
    if online:
        scratch += [
            pltpu.VMEM((2, 1, bq), F32),
            pltpu.VMEM((2, ATT_BK, bq), F32),
            pltpu.VMEM((2, ATT_BK, bq), F32),
            pltpu.VMEM((2, 1, bq), F32),
            pltpu.VMEM((2, 1, bq), F32),
        ]
    else:
        scratch += [pltpu.VMEM((2, ATT_BK, bq), BF16), pltpu.VMEM((2, ATT_BK, bq), BF16)]
    return pl.pallas_call(
        kern,
        grid=(DIFF_HEADS, s // bq),
        in_specs=[
            pl.BlockSpec((4, DIFF_QK_DIM), lambda h, i: (0, 0)),
            pl.BlockSpec((DIFF_V_DIM, 1), lambda h, i: (0, 0)),
            pl.BlockSpec((per_q, ATT_BK, bq), lambda h, i: (0, 0, 0)),
            pl.BlockSpec((None, 2, bq, 128), lambda h, i: (h, 0, i, 0)),
            pl.BlockSpec((None, 2, s, 128), lambda h, i: (h, 0, 0, 0)),
            pl.BlockSpec((None, V_AUG, s), lambda h, i: (h, 0, 0)),
        ],
        out_specs=pl.BlockSpec((bq, DIFF_V_DIM), lambda h, i: (i, h)),
        out_shape=jax.ShapeDtypeStruct((s, DIFF_DIM), BF16),
        scratch_shapes=scratch,
        compiler_params=_cparams(("parallel", "arbitrary")),
        name="diff_attn_online" if online else "diff_attn",
    )(lam_p, subln_g.reshape(DIFF_V_DIM, 1), dnear, qa, ka, vt)


def _shift_rows(x, prev_row, next_row):
    n = x.shape[0]
    row = lax.broadcasted_iota(jnp.int32, x.shape, 0)
    xp = jnp.where(row == 0, prev_row, pltpu.roll(x, 1, 0))
    xn = jnp.where(row == n - 1, next_row, pltpu.roll(x, n - 1, 0))
    return xp, xn


def _halo_rows(prev_ref, next_ref, i, n_blocks):
    hp = prev_ref.shape[0]
    prev_row = prev_ref[hp - 1:hp, :].astype(F32)
    next_row = next_ref[0:1, :].astype(F32)
    prev_row = jnp.where(i > 0, prev_row, 0.0)
    next_row = jnp.where(i < n_blocks - 1, next_row, 0.0)
    return prev_row, next_row


def _rwkv_prep_kernel(
        x_ref, xp_ref, xn_ref, l_ref, lp_ref, ln_ref,
        mu_ref, mul_ref, w0_ref, w2_ref, a0_ref, a2_ref, g2_ref, kk_ref, ka_ref, rk_ref,
        bd_ref, tri_ref,
        v_out, g_out, bonus_out, rb_out, kb_out, kt_out, bt_out, kh_out, bh_out, gc_out):
    i = pl.program_id(0)
    nb = pl.num_programs(0)
    tm = x_ref.shape[0]

    x = x_ref[...].astype(F32)
    prev_row, next_row = _halo_rows(xp_ref, xn_ref, i, nb)
    xp, xn = _shift_rows(x, prev_row, next_row)
    mu = mu_ref[...]
    x = x + mu[0:1] * (xp - x) + mu[1:2] * (xn - x)
    lo = l_ref[...].astype(F32)
    prev_row, next_row = _halo_rows(lp_ref, ln_ref, i, nb)
    lop, lon = _shift_rows(lo, prev_row, next_row)
    mul = mul_ref[...]
    lo = lo + mul[0:1] * (lop - lo) + mul[1:2] * (lon - lo)

    r = x[:, 0:RWKV_DIM]
    k = x[:, RWKV_DIM:2 * RWKV_DIM]
    v = x[:, 2 * RWKV_DIM:3 * RWKV_DIM]
    tw = jnp.tanh(lo[:, 0:128]).astype(BF16)
    la = lo[:, 128:256].astype(BF16)
    lg = jax.nn.sigmoid(lo[:, 256:512]).astype(BF16)

    g = _dot(lg, g2_ref[...])
    kk = k * kk_ref[...]
    ss = _group_mean_sq(kk, bd_ref, 1.0)
    kk = kk * lax.rsqrt(jnp.maximum(ss, 1e-24))
    bonus = _group_sum((r * k * rk_ref[...]).astype(BF16), bd_ref) * v

    g_out[...] = g.astype(g_out.dtype)
    bonus_out[...] = bonus.astype(bonus_out.dtype)
    v_out[...] = v.astype(BF16)

    tri = tri_ref[...]
    for d in range(2):
        wl = w0_ref[d:d + 1, :] + _dot(tw, w2_ref[d])
        logdec = -math.exp(-0.5) * jax.nn.sigmoid(wl)
        a = jax.nn.sigmoid(a0_ref[d:d + 1, :] + _dot(la, a2_ref[d]))
        k_d = k * (1.0 + (a - 1.0) * ka_ref[...])
        b_d = kk * a
        pre = _dot_exactish(tri, logdec)
        chunk_tot = jnp.broadcast_to(
            pre.reshape(tm // CHUNK, CHUNK, RWKV_DIM)[:, CHUNK - 1:CHUNK, :],
            (tm // CHUNK, CHUNK, RWKV_DIM)).reshape(tm, RWKV_DIM)
        suf = chunk_tot - pre
        if d == 0:
            lc, ex, rem = pre, pre - logdec, suf
        else:
            lc, ex, rem = suf + logdec, suf, pre - logdec
        e_neg = jnp.exp(-lc)
        e_rem = jnp.exp(rem)
        outs = (
            (rb_out, r * jnp.exp(lc)),
            (kb_out, kk * jnp.exp(ex)),
            (kt_out, k_d * e_neg),
            (bt_out, b_d * e_neg),
            (kh_out, k_d * e_rem),
            (bh_out, b_d * e_rem),
        )
        for ref, val in outs:
            ref[d] = val.astype(BF16)
        gc_out[d] = jnp.exp(chunk_tot).reshape(tm // 8, 8, RWKV_DIM)[:, 0, :]


def _rwkv_prep(p, mu_rkv, mu_lora, w0, w2p, a0, a2p, g2p, k_k, k_a, r_k, bd, tri, *, tm):
    s = p.shape[0]
    nb = s // tm
    hb = tm // 16
    last16 = s // 16 - 1
    rkv_w = 3 * RWKV_DIM
    assert COL_RKV % rkv_w == 0 and COL_LORA % LORA_PAD == 0
    c_rkv = COL_RKV // rkv_w
    c_lora = COL_LORA // LORA_PAD

    def prev_map(c):
        return lambda i: (jnp.maximum(i * hb - 1, 0), c)

    def next_map(c):
        return lambda i: (jnp.minimum((i + 1) * hb, last16), c)

    full = lambda *shape: pl.BlockSpec(shape, lambda i: (0,) * len(shape))
    hm = lambda: pl.BlockSpec((2, tm, RWKV_DIM), lambda i: (0, i, 0))
    hm_shape = jax.ShapeDtypeStruct((2, s, RWKV_DIM), BF16)
    return pl.pallas_call(
        _rwkv_prep_kernel,
        grid=(nb,),
        in_specs=[
            pl.BlockSpec((tm, rkv_w), lambda i: (i, c_rkv)),
            pl.BlockSpec((16, rkv_w), prev_map(c_rkv)),
            pl.BlockSpec((16, rkv_w), next_map(c_rkv)),
            pl.BlockSpec((tm, LORA_PAD), lambda i: (i, c_lora)),
            pl.BlockSpec((16, LORA_PAD), prev_map(c_lora)),
            pl.BlockSpec((16, LORA_PAD), next_map(c_lora)),
            full(2, rkv_w), full(2, LORA_PAD),
            full(2, RWKV_DIM), full(2, 128, RWKV_DIM),
            full(2, RWKV_DIM), full(2, 128, RWKV_DIM),
            full(256, RWKV_DIM),
            full(1, RWKV_DIM), full(1, RWKV_DIM), full(1, RWKV_DIM),
            full(GROUP_SLAB, GROUP_SLAB), full(tm, tm),
        ],
        out_specs=[
            pl.BlockSpec((tm, RWKV_DIM), lambda i: (i, 0)),
            pl.BlockSpec((tm, RWKV_DIM), lambda i: (i, 0)),
            pl.BlockSpec((tm, RWKV_DIM), lambda i: (i, 0)),
            hm(), hm(), hm(), hm(), hm(), hm(),
            pl.BlockSpec((2, tm // 8, RWKV_DIM), lambda i: (0, i, 0)),
        ],
        out_shape=[
            jax.ShapeDtypeStruct((s, RWKV_DIM), BF16),
            jax.ShapeDtypeStruct((s, RWKV_DIM), BF16),
            jax.ShapeDtypeStruct((s, RWKV_DIM), BF16),
            hm_shape, hm_shape, hm_shape, hm_shape, hm_shape, hm_shape,
            jax.ShapeDtypeStruct((2, s // 8, RWKV_DIM), F32),
        ],
        compiler_params=_cparams(("parallel",)),
        name="rwkv_prep",
    )(p, p, p, p, p, p, mu_rkv, mu_lora, w0, w2p, a0, a2p, g2p, k_k, k_a, r_k, bd, tri)


def _bdot(a, b):
    return lax.dot_general(a, b, (((2,), (1,)), ((0,), (0,))), preferred_element_type=F32)


def _bdot_nt(a, b):
    return lax.dot_general(a, b, (((2,), (2,)), ((0,), (0,))), preferred_element_type=F32)


def _bdot_tn(a, b):
    return lax.dot_general(a, b, (((1,), (1,)), ((0,), (0,))), preferred_element_type=F32)


WKV_CHUNKS_PER_STEP = 4


def _wkv_kernel(*refs):
    (vf_ref, vr_ref), ins, (gcf_ref, gcr_ref, yf_ref, yr_ref, s_ref) = refs[:2], refs[2:14], refs[14:]

    @pl.when(pl.program_id(0) == 0)
    def _():
        s_ref[...] = jnp.zeros_like(s_ref)

    def rows(ref, part, n):
        return ref.at[n * part:n * (part + 1), :]

    for part in range(WKV_CHUNKS_PER_STEP):
        back = WKV_CHUNKS_PER_STEP - 1 - part
        _wkv_chunk(False, rows(vf_ref, part, CHUNK), [rows(r, part, CHUNK) for r in ins[0:6]],
                   rows(gcf_ref, part, 8), rows(yf_ref, part, CHUNK), s_ref.at[0])
        _wkv_chunk(True, rows(vr_ref, back, CHUNK), [rows(r, back, CHUNK) for r in ins[6:12]],
                   rows(gcr_ref, back, 8), rows(yr_ref, back, CHUNK), s_ref.at[1])


WKV_GROUP = 2
GROUP_LANES = WKV_GROUP * RWKV_HEAD_DIM
N_GROUPS = RWKV_HEADS // WKV_GROUP


def _groups(x):
    return jnp.stack([x[:, GROUP_LANES * p:GROUP_LANES * (p + 1)] for p in range(N_GROUPS)])


def _head_of_lane(shape):
    return lax.broadcasted_iota(jnp.int32, shape, 2) // RWKV_HEAD_DIM


def _group_diag(y):
    head = _head_of_lane(y.shape)
    zero = jnp.zeros_like(y)
    return jnp.concatenate([jnp.where(head == h, y, zero) for h in range(WKV_GROUP)], axis=1)


def _group_tn(a, b):
    full = _bdot_tn(a, b)
    n = RWKV_HEAD_DIM
    head = _head_of_lane((N_GROUPS, n, GROUP_LANES))
    out = full[:, :n]
    for h in range(1, WKV_GROUP):
        out = jnp.where(head == h, full[:, n * h:n * (h + 1)], out)
    return out


def _wkv_chunk(reverse, v_ref, scaled_refs, gc_ref, y_ref, s_ref):
    c = CHUNK
    lanes = GROUP_LANES
    row = lax.broadcasted_iota(jnp.int32, (N_GROUPS, c, lanes), 1)
    col = lax.broadcasted_iota(jnp.int32, (N_GROUPS, c, lanes), 2) % RWKV_HEAD_DIM
    strict = (row < col) if reverse else (row > col)
    incl = (row <= col) if reverse else (row >= col)

    v = _groups(v_ref[...])
    rb, kb, kt, bt, kh, bh = [_groups(r[...]) for r in scaled_refs]

    kr = jnp.concatenate([kb, rb], axis=1)
    sk = _bdot_nt(kr, _group_diag(kt))
    sb = _bdot_nt(kr, _group_diag(bt))
    a_kk = jnp.where(strict, sk[:, :c], 0.0)
    a_rk = jnp.where(incl, sk[:, c:], 0.0)
    l_mat = jnp.where(strict, sb[:, :c], 0.0)
    a_rb = jnp.where(incl, sb[:, c:], 0.0)

    eye = jnp.where(row == col, 1.0, 0.0)
    t_inv = eye - l_mat
    l_b = l_mat.astype(BF16)
    pw = _bdot(l_b, _group_diag(l_b))
    for step in range(5):
        pw_b = _group_diag(pw.astype(BF16))
        if step < 4:
            both = _bdot(jnp.concatenate([t_inv, pw], axis=1).astype(BF16), pw_b)
            t_inv = t_inv + both[:, :c]
            pw = both[:, c:]
        else:
            t_inv = t_inv + _bdot(t_inv.astype(BF16), pw_b)

    q12 = _bdot(jnp.concatenate([a_kk, a_rk], axis=1).astype(BF16), _group_diag(v))
    q1, q2 = q12[:, :c], q12[:, c:]
    t_b = t_inv.astype(BF16)
    kq = jnp.concatenate([_group_diag(kb), _group_diag(q1.astype(BF16))], axis=2)
    wkuv = _bdot(t_b, kq).astype(BF16)
    wk, uv = wkuv[:, :, :lanes], wkuv[:, :, lanes:]
    corr = _bdot(a_rb.astype(BF16), jnp.concatenate([_group_diag(wk), _group_diag(uv)], axis=2))
    rw = rb.astype(F32) - corr[:, :, :lanes]
    y0 = q2 - corr[:, :, lanes:]
    m_mat = _group_tn(wk, bh)
    j_t = _group_tn(v, kh) - _group_tn(uv, bh)

    n = RWKV_HEAD_DIM
    s_old = s_ref[...]
    s_hi = s_old.astype(BF16)
    s_lo = (s_old - s_hi.astype(F32)).astype(BF16)
    y = _bdot_nt(rw.astype(BF16), _group_diag(s_hi)) + y0
    sm = _bdot(jnp.concatenate([s_hi, s_lo], axis=1), _group_diag(m_mat.astype(BF16)))
    s_ref[...] = s_old * _groups(gc_ref[0:1, :]) - (sm[:, :n] + sm[:, n:]) + j_t
    for p in range(N_GROUPS):
        y_ref[:, lanes * p:lanes * (p + 1)] = y[p]


def _wkv(v, rb, kb, kt, bt, kh, bh, gc):
    s = v.shape[0]
    per = WKV_CHUNKS_PER_STEP
    assert s % (per * CHUNK) == 0
    nc = s // (per * CHUNK)
    scaled = (rb, kb, kt, bt, kh, bh)
    fwd = lambda rows: pl.BlockSpec((None, rows, RWKV_DIM), lambda c: (0, c, 0))
    rev = lambda rows: pl.BlockSpec((None, rows, RWKV_DIM), lambda c: (1, nc - 1 - c, 0))
    y_shape = jax.ShapeDtypeStruct((s, RWKV_DIM), F32)
    return pl.pallas_call(
        _wkv_kernel,
        grid=(nc,),
        in_specs=[
            pl.BlockSpec((per * CHUNK, RWKV_DIM), lambda c: (c, 0)),
            pl.BlockSpec((per * CHUNK, RWKV_DIM), lambda c: (nc - 1 - c, 0)),
            *[fwd(per * CHUNK) for _ in scaled], *[rev(per * CHUNK) for _ in scaled],
            fwd(per * 8), rev(per * 8),
        ],
        out_specs=[
            pl.BlockSpec((per * CHUNK, RWKV_DIM), lambda c: (c, 0)),
            pl.BlockSpec((per * CHUNK, RWKV_DIM), lambda c: (nc - 1 - c, 0)),
        ],
        out_shape=[y_shape, y_shape],
        scratch_shapes=[pltpu.VMEM((2, N_GROUPS, RWKV_HEAD_DIM, GROUP_LANES), F32)],
        compiler_params=_cparams(("arbitrary",)),
        name="wkv7_chunked",
    )(v, v, *scaled, *scaled, gc, gc)


def _rwkv_post_kernel(yf_ref, yr_ref, g_ref, bonus_ref, lg_ref, lb_ref, bd_ref, o_ref):
    y = yf_ref[...] + yr_ref[...]
    hi, lo = _split2(y)
    mean = (_group_sum(hi, bd_ref) + _group_sum(lo, bd_ref)) * (1.0 / RWKV_HEAD_DIM)
    yc = y - mean
    var = _group_mean_sq(yc, bd_ref, RWKV_HEAD_DIM)
    yn = yc * lax.rsqrt(var + LNX_EPS)
    out = (yn * lg_ref[...] + lb_ref[...] + bonus_ref[...].astype(F32)) * g_ref[...].astype(F32)
    o_ref[...] = out.astype(o_ref.dtype)


def _rwkv_post(y_fwd, y_rev, g, bonus, lnx_g, lnx_b, bd, *, tm):
    s = g.shape[0]
    return pl.pallas_call(
        _rwkv_post_kernel,
        grid=(s // tm,),
        in_specs=[
            pl.BlockSpec((tm, RWKV_DIM), lambda i: (i, 0)),
            pl.BlockSpec((tm, RWKV_DIM), lambda i: (i, 0)),
            pl.BlockSpec((tm, RWKV_DIM), lambda i: (i, 0)),
            pl.BlockSpec((tm, RWKV_DIM), lambda i: (i, 0)),
            pl.BlockSpec((1, RWKV_DIM), lambda i: (0, 0)),
            pl.BlockSpec((1, RWKV_DIM), lambda i: (0, 0)),
            pl.BlockSpec((GROUP_SLAB, GROUP_SLAB), lambda i: (0, 0)),
        ],
        out_specs=pl.BlockSpec((tm, RWKV_DIM), lambda i: (i, 0)),
        out_shape=jax.ShapeDtypeStruct((s, RWKV_DIM), BF16),
        compiler_params=_cparams(("parallel",)),
        name="rwkv_post",
    )(y_fwd, y_rev, g, bonus, lnx_g.reshape(1, RWKV_DIM), lnx_b.reshape(1, RWKV_DIM), bd)


def _mem_attn_kernel(q_ref, kv_ref, gq_ref, gk_ref, o_ref):
    for h in range(MEM_HEADS):
        sl = slice(MEM_HEAD_DIM * h, MEM_HEAD_DIM * (h + 1))
        q = q_ref[:, sl].astype(F32)
        q = q * lax.rsqrt(jnp.mean(q * q, axis=-1, keepdims=True) + NORM_EPS)
        q = q * gq_ref[...] * (MEM_HEAD_DIM ** -0.5)
        km = kv_ref[:, sl].astype(F32)
        km = km * lax.rsqrt(jnp.mean(km * km, axis=-1, keepdims=True) + NORM_EPS)
        km = km * gk_ref[...]
        vm = kv_ref[:, MEM_DIM + MEM_HEAD_DIM * h:MEM_DIM + MEM_HEAD_DIM * (h + 1)]
        s = lax.dot_general(q.astype(BF16), km.astype(BF16), (((1,), (1,)), ((), ())),
                            preferred_element_type=F32)
        s = s - jnp.max(s, axis=-1, keepdims=True)
        e = jnp.exp(s)
        pr = e / jnp.sum(e, axis=-1, keepdims=True)
        o_ref[:, sl] = _dot(pr.astype(BF16), vm).astype(o_ref.dtype)


def _mem_attn(p, kv, gq, gk, *, tm):
    s = p.shape[0]
    assert COL_MEM % MEM_DIM == 0
    return pl.pallas_call(
        _mem_attn_kernel,
        grid=(s // tm,),
        in_specs=[
            pl.BlockSpec((tm, MEM_DIM), lambda i: (i, COL_MEM // MEM_DIM)),
            pl.BlockSpec((N_MEM, 2 * MEM_DIM), lambda i: (0, 0)),
            pl.BlockSpec((1, MEM_HEAD_DIM), lambda i: (0, 0)),
            pl.BlockSpec((1, MEM_HEAD_DIM), lambda i: (0, 0)),
        ],
        out_specs=pl.BlockSpec((tm, MEM_DIM), lambda i: (i, 0)),
        out_shape=jax.ShapeDtypeStruct((s, MEM_DIM), BF16),
        compiler_params=_cparams(("parallel",)),
        name="mem_attn",
    )(p, kv, gq, gk)


def _merge_kernel(o0_ref, o1_ref, o2_ref, g0_ref, g1_ref, g2_ref, w_ref, m_ref):
    def gate(g_ref):
        return 0.5 * jnp.tanh(0.5 * g_ref[...].astype(F32)) + 0.5

    acc = gate(g0_ref) * _dot(o0_ref[...], w_ref[0])
    acc = acc + gate(g1_ref) * _dot(o1_ref[...], w_ref[1])
    acc = acc + gate(g2_ref) * _dot(o2_ref[...], w_ref[2])
    m_ref[...] = acc.astype(m_ref.dtype)


def _merge(o_diff, o_rwkv, o_mem, p, w_branch, layer, *, tm, tn):
    s = p.shape[0]
    assert COL_GATE % tn == 0 and D_MODEL % tn == 0
    gate_blk = COL_GATE // tn
    per = D_MODEL // tn
    o_spec = lambda: pl.BlockSpec((tm, 1024), lambda i, j: (i, 0))
    g_spec = lambda b: pl.BlockSpec((tm, tn), lambda i, j: (i, gate_blk + b * per + j))
    return pl.pallas_call(
        _merge_kernel,
        grid=(s // tm, per),
        in_specs=[
            o_spec(), o_spec(), o_spec(),
            g_spec(0), g_spec(1), g_spec(2),
            pl.BlockSpec((None, N_BRANCH, 1024, tn), lambda i, j: (layer, 0, 0, j)),
        ],
        out_specs=pl.BlockSpec((tm, tn), lambda i, j: (i, j)),
        out_shape=jax.ShapeDtypeStruct((s, D_MODEL), BF16),
        compiler_params=_cparams(("parallel", "arbitrary")),
        name="merge",
    )(o_diff, o_rwkv, o_mem, p, p, p, w_branch)


GLU_TILE = 256


def _ffn_up_glu_kernel(x_ref, xp_ref, xn_ref, g_ref, wg_ref, wv_ref, cw_ref, cb_ref, o_ref,
                       h_ref, halo_ref):
    i = pl.program_id(0)
    nb = pl.num_programs(0)

    def norm(x):
        ms = jnp.mean(x * x, axis=-1, keepdims=True)
        return x * lax.rsqrt(ms + NORM_EPS) * g_ref[...]

    @pl.when(pl.program_id(1) == 0)
    def _():
        h_ref[...] = norm(x_ref[...]).astype(BF16)
        row = lax.broadcasted_iota(jnp.int32, xp_ref.shape, 0)
        before = jnp.where(i > 0, pltpu.roll(norm(xp_ref[...]), 1, 0), 0.0)
        after = jnp.where(i < nb - 1, pltpu.roll(norm(xn_ref[...]), 1, 0), 0.0)
        halo_ref[...] = jnp.where(row == 0, before, jnp.where(row == 1, after, 0.0)).astype(BF16)

    h = h_ref[...]
    halo = halo_ref[...]
    tn = o_ref.shape[1]
    for c0 in range(0, tn, GLU_TILE):
        sl = slice(c0, c0 + GLU_TILE)
        wg = wg_ref[:, sl]
        ug = _dot(h, wg)
        uv = _dot(h, wv_ref[:, sl])
        edge = _dot(halo, wg)
        gp, gn = _shift_rows(ug, edge[0:1], edge[1:2])
        cw = cw_ref[:, sl]
        conv = cw[0:1] * gp + cw[1:2] * ug + cw[2:3] * gn + cb_ref[:, sl]
        half = 0.5 * conv
        o_ref[:, sl] = ((half + half * jnp.tanh(half)) * uv).astype(o_ref.dtype)


def _ffn_up_glu(x, g, w_up, layer, conv_w, conv_b, *, tm, tn):
    s, k = x.shape
    assert tn % GLU_TILE == 0
    nj = D_FF // tn
    hb = tm // 16
    last16 = s // 16 - 1
    return pl.pallas_call(
        _ffn_up_glu_kernel,
        grid=(s // tm, nj),
        in_specs=[
            pl.BlockSpec((tm, k), lambda i, j: (i, 0)),
            pl.BlockSpec((16, k), lambda i, j: (jnp.maximum(i * hb - 1, 0), 0)),
            pl.BlockSpec((16, k), lambda i, j: (jnp.minimum((i + 1) * hb, last16), 0)),
            pl.BlockSpec((1, k), lambda i, j: (0, 0)),
            pl.BlockSpec((None, k, tn), lambda i, j: (layer, 0, j)),
            pl.BlockSpec((None, k, tn), lambda i, j: (layer, 0, nj + j)),
            pl.BlockSpec((3, tn), lambda i, j: (0, j)),
            pl.BlockSpec((1, tn), lambda i, j: (0, j)),
        ],
        out_specs=pl.BlockSpec((tm, tn), lambda i, j: (i, j)),
        out_shape=jax.ShapeDtypeStruct((s, D_FF), BF16),
        scratch_shapes=[pltpu.VMEM((tm, k), BF16), pltpu.VMEM((16, k), BF16)],
        compiler_params=_cparams(("parallel", "arbitrary")),
        name="ffn_up_glu",
    )(x, x, x, g.reshape(1, k), w_up, w_up, conv_w, conv_b.reshape(1, D_FF))


def _block_ones(n, group):
    idx = np.arange(n) // group
    return jnp.asarray(idx[:, None] == idx[None, :], BF16)


def _chunk_tri(n):
    idx = np.arange(n)
    same = (idx[:, None] // CHUNK) == (idx[None, :] // CHUNK)
    return jnp.asarray(same & (idx[:, None] >= idx[None, :]), BF16)


def _pad_rows(w, rows_before, total):
    n = w.shape[-1]
    out = jnp.zeros((total, n), w.dtype)
    return lax.dynamic_update_slice(out, w, (rows_before, 0))


def _pad_w_in(w):
    lora0 = COL_RKV + 3 * RWKV_DIM
    mem0 = lora0 + LORA_COLS
    gate0 = mem0 + MEM_DIM
    pad = jnp.zeros((w.shape[0], LORA_PAD - LORA_COLS), w.dtype)
    parts = [w[:, :lora0], w[:, gate0:], w[:, mem0:gate0], w[:, lora0:mem0], pad]
    return jnp.concatenate(parts, axis=1).astype(BF16)


def _tile_plan(s):
    rows = 1024 if s % 1024 == 0 else 512
    return dict(
        w_in=dict(tm=rows, tn=1536),
        mem_kv=dict(tm=N_MEM, tn=1024),
        diff_prep=256, rwkv_prep=256, rwkv_post=256, mem_attn=512,
        merge=dict(tm=512, tn=D_MODEL),
        w_out=dict(tm=512, tn=D_MODEL),
        ffn_up=dict(tm=rows, tn=512),
        ffn_down=dict(tm=rows, tn=512),
    )


def kernel(x, mem, attn_norm_g, w_in, diff_qk_g, diff_lambda, diff_subln_g, rwkv_mu, rwkv_w0,
           rwkv_w2, rwkv_a0, rwkv_a2, rwkv_g2, rwkv_k_k, rwkv_k_a, rwkv_r_k, rwkv_lnx_g,
           rwkv_lnx_b, mem_norm_g, w_mem_kv, mem_qk_g, w_branch, w_out, ffn_norm_g, w_ffn_up,
           ffn_conv_w, ffn_conv_b, w_ffn_down):
    b, s, d = x.shape
    assert b == 1 and d == D_MODEL and s % ATT_BQ == 0
    xs = x.reshape(s, d)
    mem2 = mem.reshape(N_MEM, d)
    tiles = _tile_plan(s)
    bd64 = _block_ones(GROUP_SLAB, 64)
    tri = _chunk_tri(tiles["rwkv_prep"])
    n_rwkv_main = 3 * RWKV_DIM
    w_mem_kv_b = w_mem_kv.astype(BF16)
    w_branch_b = w_branch.astype(BF16)
    w_out_b = w_out.astype(BF16)
    w_ffn_up_b = w_ffn_up.astype(BF16)
    w_ffn_down_b = w_ffn_down.astype(BF16)

    for l in range(DEPTH):
        lam_init = 0.8 - 0.6 * math.exp(-0.3 * l)
        p = _rms_mm(xs, attn_norm_g[l], _pad_w_in(w_in[l])[None], 0, name="rms_w_in",
                    **tiles["w_in"])

        gq = jnp.tile(diff_qk_g[l, 0].reshape(1, 128), (1, DIFF_HEADS))
        gk = jnp.tile(diff_qk_g[l, 1].reshape(1, 128), (1, DIFF_HEADS))
        qc, kc, score_bound = _attn_consts(diff_qk_g[l])
        qa, ka, vt = _diff_prep(p, gq, gk, qc, kc, bd64, tm=tiles["diff_prep"])
        o_diff = lax.cond(
            score_bound <= SCORE_BOUND_MAX,
            functools.partial(_diff_attn, lam_init=lam_init, online=False),
            functools.partial(_diff_attn, lam_init=lam_init, online=True),
            qa, ka, vt, diff_lambda[l], diff_subln_g[l])

        mu = rwkv_mu[l]
        mu_rkv = mu[:, :n_rwkv_main]
        mu_lora = jnp.pad(mu[:, n_rwkv_main:], ((0, 0), (0, LORA_PAD - LORA_COLS)))
        w2p = jnp.stack([_pad_rows(rwkv_w2[l, dd], 64 * dd, 128) for dd in range(2)]).astype(BF16)
        a2p = jnp.stack([_pad_rows(rwkv_a2[l, dd], 64 * dd, 128) for dd in range(2)]).astype(BF16)
        g2p = _pad_rows(rwkv_g2[l], 0, 256).astype(BF16)
        (v_h, g_tok, bonus, rb, kb, kt, bt, kh, bh, gc) = _rwkv_prep(
            p, mu_rkv, mu_lora, rwkv_w0[l], w2p, rwkv_a0[l], a2p, g2p,
            rwkv_k_k[l].reshape(1, RWKV_DIM), rwkv_k_a[l].reshape(1, RWKV_DIM),
            rwkv_r_k[l].reshape(1, RWKV_DIM), bd64, tri, tm=tiles["rwkv_prep"])
        y_fwd, y_rev = _wkv(v_h, rb, kb, kt, bt, kh, bh, gc)
        o_rwkv = _rwkv_post(y_fwd, y_rev, g_tok, bonus, rwkv_lnx_g[l], rwkv_lnx_b[l], bd64,
                            tm=tiles["rwkv_post"])

        kv = _rms_mm(mem2, mem_norm_g[l], w_mem_kv_b, l, name="rms_mem_kv", **tiles["mem_kv"])
        o_mem = _mem_attn(p, kv, mem_qk_g[l, 0].reshape(1, MEM_HEAD_DIM),
                          mem_qk_g[l, 1].reshape(1, MEM_HEAD_DIM), tm=tiles["mem_attn"])

        merged = _merge(o_diff, o_rwkv, o_mem, p, w_branch_b, l, **tiles["merge"])
        xs = _mm_res(merged, w_out_b, l, xs, name="w_out_res", **tiles["w_out"])

        act = _ffn_up_glu(xs, ffn_norm_g[l], w_ffn_up_b, l, ffn_conv_w[l], ffn_conv_b[l],
                          **tiles["ffn_up"])
        xs = _mm_res(act, w_ffn_down_b, l, xs, name="ffn_down_res", **tiles["ffn_down"])

    return xs.reshape(b, s, d)
```

```python
import functools
import math

import jax
import jax.numpy as jnp
import numpy as np
from jax import lax
from jax.experimental import pallas as pl
from jax.experimental.pallas import tpu as pltpu

F32 = jnp.float32
BF16 = jnp.bfloat16

D_MODEL = 2048
DEPTH = 2
DIFF_HEADS = 8
DIFF_QK_DIM = 64
DIFF_V_DIM = 128
DIFF_DIM = 1024
RWKV_HEADS = 16
RWKV_HEAD_DIM = 64
RWKV_DIM = 1024
DECAY_LORA = 64
AAA_LORA = 64
GATE_LORA = 160
LORA_COLS = 2 * DECAY_LORA + 2 * AAA_LORA + GATE_LORA
LORA_PAD = 512
N_MEM = 256
MEM_HEADS = 4
MEM_HEAD_DIM = 256
MEM_DIM = 1024
N_BRANCH = 3
D_FF = 5632
NORM_EPS = 1e-6
LNX_EPS = 64e-5

COL_DIFF = 0
COL_RKV = 3 * DIFF_DIM
COL_GATE = COL_RKV + 3 * RWKV_DIM
COL_MEM = COL_GATE + N_BRANCH * D_MODEL
COL_LORA = COL_MEM + MEM_DIM
N_IN_PAD = COL_LORA + LORA_PAD

CHUNK = 64
LOG2E = 1.4426950408889634
N_POS_COLS = 12
SCORE_BOUND_MAX = 30.0
ATT_BQ = 1024
ATT_BK = 512
V_AUG = 144
FAR_UNROLL = 10
VMEM_LIMIT = 56 * 1024 * 1024


def _cparams(sem):
    return pltpu.CompilerParams(dimension_semantics=sem, vmem_limit_bytes=VMEM_LIMIT)


def _split2(x):
    hi = x.astype(BF16)
    return hi, (x - hi.astype(F32)).astype(BF16)


def _dot(a, b):
    return jnp.dot(a, b, preferred_element_type=F32)


def _dot_exactish(a_bf16_exact, x_f32):
    hi, lo = _split2(x_f32)
    return _dot(a_bf16_exact, hi) + _dot(a_bf16_exact, lo)


def _rms_mm_kernel(x_ref, g_ref, w_ref, o_ref, h_ref):
    @pl.when(pl.program_id(1) == 0)
    def _():
        x = x_ref[...]
        ms = jnp.mean(x * x, axis=-1, keepdims=True)
        h_ref[...] = (x * lax.rsqrt(ms + NORM_EPS) * g_ref[...]).astype(BF16)

    o_ref[...] = _dot(h_ref[...], w_ref[...]).astype(o_ref.dtype)


def _rms_mm(x, g, w, layer, *, tm, tn, name):
    m, k = x.shape
    n = w.shape[2]
    return pl.pallas_call(
        _rms_mm_kernel,
        grid=(m // tm, n // tn),
        in_specs=[
            pl.BlockSpec((tm, k), lambda i, j: (i, 0)),
            pl.BlockSpec((1, k), lambda i, j: (0, 0)),
            pl.BlockSpec((None, k, tn), lambda i, j: (layer, 0, j)),
        ],
        out_specs=pl.BlockSpec((tm, tn), lambda i, j: (i, j)),
        out_shape=jax.ShapeDtypeStruct((m, n), BF16),
        scratch_shapes=[pltpu.VMEM((tm, k), BF16)],
        compiler_params=_cparams(("parallel", "arbitrary")),
        name=name,
    )(x, g.reshape(1, k), w)


def _mm_res_kernel(a_ref, w_ref, r_ref, o_ref):
    o_ref[...] = r_ref[...] + _dot(a_ref[...], w_ref[...])


def _mm_res(a, w, layer, res, *, tm, tn, name):
    m, k = a.shape
    n = w.shape[2]
    return pl.pallas_call(
        _mm_res_kernel,
        grid=(m // tm, n // tn),
        in_specs=[
            pl.BlockSpec((tm, k), lambda i, j: (i, 0)),
            pl.BlockSpec((None, k, tn), lambda i, j: (layer, 0, j)),
            pl.BlockSpec((tm, tn), lambda i, j: (i, j)),
        ],
        out_specs=pl.BlockSpec((tm, tn), lambda i, j: (i, j)),
        out_shape=jax.ShapeDtypeStruct((m, n), F32),
        compiler_params=_cparams(("parallel", "arbitrary")),
        name=name,
    )(a, w, res)


GROUP_SLAB = 256


def _group_sum(xb, bd_ref):
    bd = bd_ref[...]
    n = xb.shape[1]
    slabs = [_dot(xb[:, q:q + GROUP_SLAB], bd) for q in range(0, n, GROUP_SLAB)]
    return jnp.concatenate(slabs, axis=1)


def _group_mean_sq(x, bd_ref, group):
    return _group_sum((x * x).astype(BF16), bd_ref) * (1.0 / group)


def _aug_base(mp):
    return 64 if mp == 0 else 0


def _diff_prep_kernel(q_ref, k_ref, v_ref, gq_ref, gk_ref, qc_ref, kc_ref, bd_ref,
                      qa_ref, ka_ref, vt_ref):
    tm = q_ref.shape[0]
    row0 = pl.program_id(0) * tm
    lane = lax.broadcasted_iota(jnp.int32, (tm, 128), 1)
    pos = row0 + lax.broadcasted_iota(jnp.int32, (tm, 128), 0)
    pos_lo = (pos & 127).astype(F32)
    pos_hi = (pos >> 7).astype(F32)

    q = q_ref[...].astype(F32)
    qn = q * lax.rsqrt(_group_mean_sq(q, bd_ref, DIFF_QK_DIM) + NORM_EPS)
    qn = qn * gq_ref[...] * (DIFF_QK_DIM ** -0.5 * LOG2E)
    k = k_ref[...].astype(F32)
    kn = k * lax.rsqrt(_group_mean_sq(k, bd_ref, DIFF_QK_DIM) + NORM_EPS)
    kn = kn * gk_ref[...]

    for h in range(DIFF_HEADS):
        qh = qn[:, 128 * h:128 * (h + 1)]
        kh = kn[:, 128 * h:128 * (h + 1)]
        for mp in range(2):
            a0 = _aug_base(mp)
            g = 2 * h + mp
            is_data = (lane < 64) if mp == 0 else (lane >= 64)
            aug_q = jnp.where(lane < a0 + 3, pos_lo,
                              jnp.where(lane < a0 + 6, pos_hi, qc_ref[g:g + 1, :]))
            aug_k = jnp.where((lane >= a0 + 6) & (lane < a0 + 9), pos_lo,
                              jnp.where((lane >= a0 + 9) & (lane < a0 + 12), pos_hi,
                                        kc_ref[g:g + 1, :]))
            qa_ref[h, mp] = jnp.where(is_data, qh, aug_q).astype(BF16)
            ka_ref[h, mp] = jnp.where(is_data, kh, aug_k).astype(BF16)

    vt = v_ref[...].astype(F32).T
    sub = lax.broadcasted_iota(jnp.int32, (V_AUG - DIFF_V_DIM, tm), 0)
    ones_rows = jnp.where(sub == 0, 1.0, 0.0).astype(BF16)
    for h in range(DIFF_HEADS):
        vt_ref[h, 0:DIFF_V_DIM, :] = vt[128 * h:128 * (h + 1), :].astype(BF16)
        vt_ref[h, DIFF_V_DIM:V_AUG, :] = ones_rows


def _bf16_split3_const(x):
    parts = []
    for _ in range(3):
        part = float(np.asarray(x, np.float32).astype(BF16).astype(np.float32))
        parts.append(part)
        x = x - part
    return parts


def _attn_consts(qk_g):
    l_parts = _bf16_split3_const(LOG2E)
    m_nat = 8.0 * jnp.max(jnp.abs(qk_g[0]), axis=-1) * jnp.max(jnp.abs(qk_g[1]), axis=-1)
    qc = np.zeros((2 * DIFF_HEADS, 128), np.float32)
    kc = np.zeros((2 * DIFF_HEADS, 128), np.float32)
    bound_lane = np.zeros((2, 2 * DIFF_HEADS, 128), np.float32)
    for h in range(DIFF_HEADS):
        slope = 2.0 ** (-(h + 1))
        for mp in range(2):
            a0 = _aug_base(mp)
            g = 2 * h + mp
            for t, lp in enumerate(l_parts):
                qc[g, a0 + 6 + t] = slope * lp
                qc[g, a0 + 9 + t] = 128.0 * slope * lp
                kc[g, a0 + t] = -slope * lp
                kc[g, a0 + 3 + t] = -128.0 * slope * lp
            bound_lane[mp, g, a0 + N_POS_COLS] = 1.0
            kc[g, a0 + N_POS_COLS] = 1.0
    m2 = -m_nat * LOG2E
    qc = qc + m2[0] * bound_lane[0] + m2[1] * bound_lane[1]
    return qc, jnp.asarray(kc), jnp.max(m_nat)


def _diff_prep(p, gq, gk, qc, kc, bd, *, tm):
    s = p.shape[0]
    return pl.pallas_call(
        _diff_prep_kernel,
        grid=(s // tm,),
        in_specs=[
            pl.BlockSpec((tm, DIFF_DIM), lambda i: (i, 0)),
            pl.BlockSpec((tm, DIFF_DIM), lambda i: (i, 1)),
            pl.BlockSpec((tm, DIFF_DIM), lambda i: (i, 2)),
            pl.BlockSpec((1, DIFF_DIM), lambda i: (0, 0)),
            pl.BlockSpec((1, DIFF_DIM), lambda i: (0, 0)),
            pl.BlockSpec((2 * DIFF_HEADS, 128), lambda i: (0, 0)),
            pl.BlockSpec((2 * DIFF_HEADS, 128), lambda i: (0, 0)),
            pl.BlockSpec((GROUP_SLAB, GROUP_SLAB), lambda i: (0, 0)),
        ],
        out_specs=[
            pl.BlockSpec((DIFF_HEADS, 2, tm, 128), lambda i: (0, 0, i, 0)),
            pl.BlockSpec((DIFF_HEADS, 2, tm, 128), lambda i: (0, 0, i, 0)),
            pl.BlockSpec((DIFF_HEADS, V_AUG, tm), lambda i: (0, 0, i)),
        ],
        out_shape=[
            jax.ShapeDtypeStruct((DIFF_HEADS, 2, s, 128), BF16),
            jax.ShapeDtypeStruct((DIFF_HEADS, 2, s, 128), BF16),
            jax.ShapeDtypeStruct((DIFF_HEADS, V_AUG, s), BF16),
        ],
        compiler_params=_cparams(("parallel",)),
        name="diff_prep",
    )(p, p, p, gq, gk, qc, kc, bd)


def _diff_attn_kernel(lam_ref, sg_ref, dnear_ref, qa_ref, ka_ref, vt_ref, o_ref,
                      acc_ref, qv_ref, *mode_refs, lam_init, online):
    h = pl.program_id(0)
    i = pl.program_id(1)
    bq = qa_ref.shape[1]
    s_len = ka_ref.shape[1]
    bk = ATT_BK
    n_chunks = s_len // bk
    per_q = bq // bk
    n_far = n_chunks - per_q
    j_lo = i * per_q
    q0 = i * bq
    slope2 = jnp.exp2(-jnp.full((1, 1), h + 1, jnp.int32).astype(F32)) * LOG2E
    if online:
        m_ref, sa_ref, sb_ref, mxa_ref, mxb_ref = mode_refs
        m_ref[...] = jnp.full_like(m_ref, -1e30)
        bufs = ((sa_ref, mxa_ref), (sb_ref, mxb_ref))
    else:
        bufs = mode_refs
    acc_ref[...] = jnp.zeros_like(acc_ref)

    lane = lax.broadcasted_iota(jnp.int32, (1, 128), 1)
    for mp in range(2):
        q = qa_ref[mp]
        pos_lane = (lane >= _aug_base(mp)) & (lane < _aug_base(mp) + N_POS_COLS)
        qv_ref[mp, 0] = q
        qv_ref[mp, 1] = jnp.where(pos_lane, -q, q)
        qv_ref[mp, 2] = jnp.where(pos_lane, jnp.zeros_like(q), q)

    def stage_one(k0, variant, bias, buf):
        for mp in range(2):
            kc = ka_ref[mp, pl.ds(k0, bk), :]
            s = lax.dot_general(kc, qv_ref[mp, variant], (((1,), (1,)), ((), ())),
                                preferred_element_type=F32)
            if bias is not None:
                s = s + bias
            if online:
                buf[0][mp] = s
                buf[1][mp] = jnp.max(s, axis=0, keepdims=True)
            else:
                buf[mp] = jnp.exp2(s).astype(BF16)

    def stage_two(k0, buf):
        vt_c = vt_ref[:, pl.ds(k0, bk)]
        for mp in range(2):
            if online:
                m_old = m_ref[mp]
                m_new = jnp.maximum(m_old, buf[1][mp])
                p = jnp.exp2(buf[0][mp] - m_new).astype(BF16)
                acc_ref[mp] = acc_ref[mp] * jnp.exp2(m_old - m_new) + _dot(vt_c, p)
                m_ref[mp] = m_new
            else:
                acc_ref[mp] += _dot(vt_c, buf[mp])

    def near_start(t):
        return pl.multiple_of(q0 + t * bk, bk)

    def far_start(t):
        t = jnp.minimum(t, n_far - 1)
        j = jnp.where(t < j_lo, t, t + per_q)
        return pl.multiple_of(j * bk, bk), (t >= j_lo).astype(jnp.int32)

    def near_one(t):
        stage_one(near_start(t), 2, slope2 * dnear_ref[t], bufs[t % 2])

    def far_one(t, parity):
        k0, after = far_start(t)
        stage_one(k0, after, None, bufs[parity])

    def far_two(t, parity):
        stage_two(far_start(t)[0], bufs[parity])

    near_one(0)
    for t in range(1, per_q):
        near_one(t)
        stage_two(near_start(t - 1), bufs[(t - 1) % 2])
    far_one(0, per_q % 2)
    stage_two(near_start(per_q - 1), bufs[(per_q - 1) % 2])

    def far_body(u, c):
        for r in range(FAR_UNROLL):
            far_one(FAR_UNROLL * u + r + 1, (per_q + r + 1) % 2)
            far_two(FAR_UNROLL * u + r, (per_q + r) % 2)
        return c

    lax.fori_loop(0, n_far // FAR_UNROLL, far_body, 0)
    for t in range(n_far - n_far % FAR_UNROLL, n_far):
        if t + 1 < n_far:
            far_one(t + 1, (per_q + t + 1) % 2)
        far_two(t, (per_q + t) % 2)

    lp = lam_ref[...]
    lam = (jnp.exp(jnp.sum(lp[0:1] * lp[1:2], axis=-1, keepdims=True))
           - jnp.exp(jnp.sum(lp[2:3] * lp[3:4], axis=-1, keepdims=True)) + lam_init)
    a0 = acc_ref[0]
    a1 = acc_ref[1]
    o = (a0[0:DIFF_V_DIM] / a0[DIFF_V_DIM:DIFF_V_DIM + 1]
         - lam * (a1[0:DIFF_V_DIM] / a1[DIFF_V_DIM:DIFF_V_DIM + 1]))
    ms = jnp.mean(o * o, axis=0, keepdims=True)
    o = o * lax.rsqrt(ms + NORM_EPS) * sg_ref[...] * (1.0 - lam_init)
    o_ref[...] = o.T.astype(o_ref.dtype)


def _diff_attn(qa, ka, vt, lam_p, subln_g, *, lam_init, online):
    s = qa.shape[2]
    bq = ATT_BQ
    assert bq % ATT_BK == 0 and s > bq
    kern = functools.partial(_diff_attn_kernel, lam_init=lam_init, online=online)
    per_q = bq // ATT_BK
    key = np.arange(per_q * ATT_BK).reshape(per_q, ATT_BK, 1)
    dnear = jnp.asarray(-np.abs(np.arange(bq).reshape(1, 1, bq) - key), F32)
    scratch = [
        pltpu.VMEM((2, V_AUG, bq), F32),
        pltpu.VMEM((2, 3, bq, 128), BF16),
    ]
    if online:
        scratch += [
            pltpu.VMEM((2, 1, bq), F32),
            pltpu.VMEM((2, ATT_BK, bq), F32),
            pltpu.VMEM((2, ATT_BK, bq), F32),
            pltpu.VMEM((2, 1, bq), F32),
            pltpu.VMEM((2, 1, bq), F32),
        ]
    else:
        scratch += [pltpu.VMEM((2, ATT_BK, bq), BF16), pltpu.VMEM((2, ATT_BK, bq), BF16)]
    return pl.pallas_call(
        kern,
        grid=(DIFF_HEADS, s // bq),
        in_specs=[
            pl.BlockSpec((4, DIFF_QK_DIM), lambda h, i: (0, 0)),
            pl.BlockSpec((DIFF_V_DIM, 1), lambda h, i: (0, 0)),
            pl.BlockSpec((per_q, ATT_BK, bq), lambda h, i: (0, 0, 0)),
            pl.BlockSpec((None, 2, bq, 128), lambda h, i: (h, 0, i, 0)),
            pl.BlockSpec((None, 2, s, 128), lambda h, i: (h, 0, 0, 0)),
            pl.BlockSpec((None, V_AUG, s), lambda h, i: (h, 0, 0)),
        ],
        out_specs=pl.BlockSpec((bq, DIFF_V_DIM), lambda h, i: (i, h)),
        out_shape=jax.ShapeDtypeStruct((s, DIFF_DIM), BF16),
        scratch_shapes=scratch,
        compiler_params=_cparams(("parallel", "arbitrary")),
        name="diff_attn_online" if online else "diff_attn",
    )(lam_p, subln_g.reshape(DIFF_V_DIM, 1), dnear, qa, ka, vt)


def _shift_rows(x, prev_row, next_row):
    n = x.shape[0]
    row = lax.broadcasted_iota(jnp.int32, x.shape, 0)
    xp = jnp.where(row == 0, prev_row, pltpu.roll(x, 1, 0))
    xn = jnp.where(row == n - 1, next_row, pltpu.roll(x, n - 1, 0))
    return xp, xn


def _halo_rows(prev_ref, next_ref, i, n_blocks):
    hp = prev_ref.shape[0]
    prev_row = prev_ref[hp - 1:hp, :].astype(F32)
    next_row = next_ref[0:1, :].astype(F32)
    prev_row = jnp.where(i > 0, prev_row, 0.0)
    next_row = jnp.where(i < n_blocks - 1, next_row, 0.0)
    return prev_row, next_row


def _rwkv_prep_kernel(
        x_ref, xp_ref, xn_ref, l_ref, lp_ref, ln_ref,
        mu_ref, mul_ref, w0_ref, w2_ref, a0_ref, a2_ref, g2_ref, kk_ref, ka_ref, rk_ref,
        bd_ref, tri_ref,
        v_out, g_out, bonus_out, rb_out, kb_out, kt_out, bt_out, kh_out, bh_out, gc_out):
    i = pl.program_id(0)
    nb = pl.num_programs(0)
    tm = x_ref.shape[0]

    x = x_ref[...].astype(F32)
    prev_row, next_row = _halo_rows(xp_ref, xn_ref, i, nb)
    xp, xn = _shift_rows(x, prev_row, next_row)
    mu = mu_ref[...]
    x = x + mu[0:1] * (xp - x) + mu[1:2] * (xn - x)
    lo = l_ref[...].astype(F32)
    prev_row, next_row = _halo_rows(lp_ref, ln_ref, i, nb)
    lop, lon = _shift_rows(lo, prev_row, next_row)
    mul = mul_ref[...]
    lo = lo + mul[0:1] * (lop - lo) + mul[1:2] * (lon - lo)

    r = x[:, 0:RWKV_DIM]
    k = x[:, RWKV_DIM:2 * RWKV_DIM]
    v = x[:, 2 * RWKV_DIM:3 * RWKV_DIM]
    tw = jnp.tanh(lo[:, 0:128]).astype(BF16)
    la = lo[:, 128:256].astype(BF16)
    lg = jax.nn.sigmoid(lo[:, 256:512]).astype(BF16)

    g = _dot(lg, g2_ref[...])
    kk = k * kk_ref[...]
    ss = _group_mean_sq(kk, bd_ref, 1.0)
    kk = kk * lax.rsqrt(jnp.maximum(ss, 1e-24))
    bonus = _group_sum((r * k * rk_ref[...]).astype(BF16), bd_ref) * v

    g_out[...] = g.astype(g_out.dtype)
    bonus_out[...] = bonus.astype(bonus_out.dtype)
    v_out[...] = v.astype(BF16)

    tri = tri_ref[...]
    for d in range(2):
        wl = w0_ref[d:d + 1, :] + _dot(tw, w2_ref[d])
        logdec = -math.exp(-0.5) * jax.nn.sigmoid(wl)
        a = jax.nn.sigmoid(a0_ref[d:d + 1, :] + _dot(la, a2_ref[d]))
        k_d = k * (1.0 + (a - 1.0) * ka_ref[...])
        b_d = kk * a
        pre = _dot_exactish(tri, logdec)
        chunk_tot = jnp.broadcast_to(
            pre.reshape(tm // CHUNK, CHUNK, RWKV_DIM)[:, CHUNK - 1:CHUNK, :],
            (tm // CHUNK, CHUNK, RWKV_DIM)).reshape(tm, RWKV_DIM)
        suf = chunk_tot - pre
        if d == 0:
            lc, ex, rem = pre, pre - logdec, suf
        else:
            lc, ex, rem = suf + logdec, suf, pre - logdec
        e_neg = jnp.exp(-lc)
        e_rem = jnp.exp(rem)
        outs = (
            (rb_out, r * jnp.exp(lc)),
            (kb_out, kk * jnp.exp(ex)),
            (kt_out, k_d * e_neg),
            (bt_out, b_d * e_neg),
            (kh_out, k_d * e_rem),
            (bh_out, b_d * e_rem),
        )
        for ref, val in outs:
            ref[d] = val.astype(BF16)
        gc_out[d] = jnp.exp(chunk_tot).reshape(tm // 8, 8, RWKV_DIM)[:, 0, :]


def _rwkv_prep(p, mu_rkv, mu_lora, w0, w2p, a0, a2p, g2p, k_k, k_a, r_k, bd, tri, *, tm):
    s = p.shape[0]
    nb = s // tm
    hb = tm // 16
    last16 = s // 16 - 1
    rkv_w = 3 * RWKV_DIM
    assert COL_RKV % rkv_w == 0 and COL_LORA % LORA_PAD == 0
    c_rkv = COL_RKV // rkv_w
    c_lora = COL_LORA // LORA_PAD

    def prev_map(c):
        return lambda i: (jnp.maximum(i * hb - 1, 0), c)

    def next_map(c):
        return lambda i: (jnp.minimum((i + 1) * hb, last16), c)

    full = lambda *shape: pl.BlockSpec(shape, lambda i: (0,) * len(shape))
    hm = lambda: pl.BlockSpec((2, tm, RWKV_DIM), lambda i: (0, i, 0))
    hm_shape = jax.ShapeDtypeStruct((2, s, RWKV_DIM), BF16)
    return pl.pallas_call(
        _rwkv_prep_kernel,
        grid=(nb,),
        in_specs=[
            pl.BlockSpec((tm, rkv_w), lambda i: (i, c_rkv)),
            pl.BlockSpec((16, rkv_w), prev_map(c_rkv)),
            pl.BlockSpec((16, rkv_w), next_map(c_rkv)),
            pl.BlockSpec((tm, LORA_PAD), lambda i: (i, c_lora)),
            pl.BlockSpec((16, LORA_PAD), prev_map(c_lora)),
            pl.BlockSpec((16, LORA_PAD), next_map(c_lora)),
            full(2, rkv_w), full(2, LORA_PAD),
            full(2, RWKV_DIM), full(2, 128, RWKV_DIM),
            full(2, RWKV_DIM), full(2, 128, RWKV_DIM),
            full(256, RWKV_DIM),
            full(1, RWKV_DIM), full(1, RWKV_DIM), full(1, RWKV_DIM),
            full(GROUP_SLAB, GROUP_SLAB), full(tm, tm),
        ],
        out_specs=[
            pl.BlockSpec((tm, RWKV_DIM), lambda i: (i, 0)),
            pl.BlockSpec((tm, RWKV_DIM), lambda i: (i, 0)),
            pl.BlockSpec((tm, RWKV_DIM), lambda i: (i, 0)),
            hm(), hm(), hm(), hm(), hm(), hm(),
            pl.BlockSpec((2, tm // 8, RWKV_DIM), lambda i: (0, i, 0)),
        ],
        out_shape=[
            jax.ShapeDtypeStruct((s, RWKV_DIM), BF16),
            jax.ShapeDtypeStruct((s, RWKV_DIM), BF16),
            jax.ShapeDtypeStruct((s, RWKV_DIM), BF16),
            hm_shape, hm_shape, hm_shape, hm_shape, hm_shape, hm_shape,
            jax.ShapeDtypeStruct((2, s // 8, RWKV_DIM), F32),
        ],
        compiler_params=_cparams(("parallel",)),
        name="rwkv_prep",
    )(p, p, p, p, p, p, mu_rkv, mu_lora, w0, w2p, a0, a2p, g2p, k_k, k_a, r_k, bd, tri)


def _bdot(a, b):
    return lax.dot_general(a, b, (((2,), (1,)), ((0,), (0,))), preferred_element_type=F32)


def _bdot_nt(a, b):
    return lax.dot_general(a, b, (((2,), (2,)), ((0,), (0,))), preferred_element_type=F32)


def _bdot_tn(a, b):
    return lax.dot_general(a, b, (((1,), (1,)), ((0,), (0,))), preferred_element_type=F32)


WKV_CHUNKS_PER_STEP = 8


def _wkv_kernel(*refs):
    (vf_ref, vr_ref), ins, (gcf_ref, gcr_ref, yf_ref, yr_ref, s_ref) = refs[:2], refs[2:14], refs[14:]

    @pl.when(pl.program_id(0) == 0)
    def _():
        s_ref[...] = jnp.zeros_like(s_ref)

    def rows(ref, part, n):
        return ref.at[n * part:n * (part + 1), :]

    for part in range(WKV_CHUNKS_PER_STEP):
        back = WKV_CHUNKS_PER_STEP - 1 - part
        _wkv_chunk(False, rows(vf_ref, part, CHUNK), [rows(r, part, CHUNK) for r in ins[0:6]],
                   rows(gcf_ref, part, 8), rows(yf_ref, part, CHUNK), s_ref.at[0])
        _wkv_chunk(True, rows(vr_ref, back, CHUNK), [rows(r, back, CHUNK) for r in ins[6:12]],
                   rows(gcr_ref, back, 8), rows(yr_ref, back, CHUNK), s_ref.at[1])


WKV_GROUP = 2
GROUP_LANES = WKV_GROUP * RWKV_HEAD_DIM
N_GROUPS = RWKV_HEADS // WKV_GROUP


def _groups(x):
    return jnp.stack([x[:, GROUP_LANES * p:GROUP_LANES * (p + 1)] for p in range(N_GROUPS)])


def _head_of_lane(shape):
    return lax.broadcasted_iota(jnp.int32, shape, 2) // RWKV_HEAD_DIM


def _group_diag(y):
    head = _head_of_lane(y.shape)
    zero = jnp.zeros_like(y)
    return jnp.concatenate([jnp.where(head == h, y, zero) for h in range(WKV_GROUP)], axis=1)


def _group_tn(a, b):
    full = _bdot_tn(a, b)
    n = RWKV_HEAD_DIM
    head = _head_of_lane((N_GROUPS, n, GROUP_LANES))
    out = full[:, :n]
    for h in range(1, WKV_GROUP):
        out = jnp.where(head == h, full[:, n * h:n * (h + 1)], out)
    return out


def _wkv_chunk(reverse, v_ref, scaled_refs, gc_ref, y_ref, s_ref):
    c = CHUNK
    lanes = GROUP_LANES
    row = lax.broadcasted_iota(jnp.int32, (N_GROUPS, c, lanes), 1)
    col = lax.broadcasted_iota(jnp.int32, (N_GROUPS, c, lanes), 2) % RWKV_HEAD_DIM
    strict = (row < col) if reverse else (row > col)
    incl = (row <= col) if reverse else (row >= col)

    v = _groups(v_ref[...])
    rb, kb, kt, bt, kh, bh = [_groups(r[...]) for r in scaled_refs]

    kr = jnp.concatenate([kb, rb], axis=1)
    sk = _bdot_nt(kr, _group_diag(kt))
    sb = _bdot_nt(kr, _group_diag(bt))
    a_kk = jnp.where(strict, sk[:, :c], 0.0)
    a_rk = jnp.where(incl, sk[:, c:], 0.0)
    l_mat = jnp.where(strict, sb[:, :c], 0.0)
    a_rb = jnp.where(incl, sb[:, c:], 0.0)

    eye = jnp.where(row == col, 1.0, 0.0)
    t_inv = eye - l_mat
    l_b = l_mat.astype(BF16)
    pw = _bdot(l_b, _group_diag(l_b))
    for step in range(5):
        pw_b = _group_diag(pw.astype(BF16))
        if step < 4:
            both = _bdot(jnp.concatenate([t_inv, pw], axis=1).astype(BF16), pw_b)
            t_inv = t_inv + both[:, :c]
            pw = both[:, c:]
        else:
            t_inv = t_inv + _bdot(t_inv.astype(BF16), pw_b)

    q12 = _bdot(jnp.concatenate([a_kk, a_rk], axis=1).astype(BF16), _group_diag(v))
    q1, q2 = q12[:, :c], q12[:, c:]
    t_b = t_inv.astype(BF16)
    kq = jnp.concatenate([_group_diag(kb), _group_diag(q1.astype(BF16))], axis=2)
    wkuv = _bdot(t_b, kq).astype(BF16)
    wk, uv = wkuv[:, :, :lanes], wkuv[:, :, lanes:]
    corr = _bdot(a_rb.astype(BF16), jnp.concatenate([_group_diag(wk), _group_diag(uv)], axis=2))
    rw = rb.astype(F32) - corr[:, :, :lanes]
    y0 = q2 - corr[:, :, lanes:]
    m_mat = _group_tn(wk, bh)
    j_t = _group_tn(v, kh) - _group_tn(uv, bh)

    n = RWKV_HEAD_DIM
    s_old = s_ref[...]
    s_hi = s_old.astype(BF16)
    s_lo = (s_old - s_hi.astype(F32)).astype(BF16)
    y = _bdot_nt(rw.astype(BF16), _group_diag(s_hi)) + y0
    sm = _bdot(jnp.concatenate([s_hi, s_lo], axis=1), _group_diag(m_mat.astype(BF16)))
    s_ref[...] = s_old * _groups(gc_ref[0:1, :]) - (sm[:, :n] + sm[:, n:]) + j_t
    for p in range(N_GROUPS):
        y_ref[:, lanes * p:lanes * (p + 1)] = y[p]


def _wkv(v, rb, kb, kt, bt, kh, bh, gc):
    s = v.shape[0]
    per = WKV_CHUNKS_PER_STEP
    assert s % (per * CHUNK) == 0
    nc = s // (per * CHUNK)
    scaled = (rb, kb, kt, bt, kh, bh)
    fwd = lambda rows: pl.BlockSpec((None, rows, RWKV_DIM), lambda c: (0, c, 0))
    rev = lambda rows: pl.BlockSpec((None, rows, RWKV_DIM), lambda c: (1, nc - 1 - c, 0))
    y_shape = jax.ShapeDtypeStruct((s, RWKV_DIM), F32)
    return pl.pallas_call(
        _wkv_kernel,
        grid=(nc,),
        in_specs=[
            pl.BlockSpec((per * CHUNK, RWKV_DIM), lambda c: (c, 0)),
            pl.BlockSpec((per * CHUNK, RWKV_DIM), lambda c: (nc - 1 - c, 0)),
            *[fwd(per * CHUNK) for _ in scaled], *[rev(per * CHUNK) for _ in scaled],
            fwd(per * 8), rev(per * 8),
        ],
        out_specs=[
            pl.BlockSpec((per * CHUNK, RWKV_DIM), lambda c: (c, 0)),
            pl.BlockSpec((per * CHUNK, RWKV_DIM), lambda c: (nc - 1 - c, 0)),
        ],
        out_shape=[y_shape, y_shape],
        scratch_shapes=[pltpu.VMEM((2, N_GROUPS, RWKV_HEAD_DIM, GROUP_LANES), F32)],
        compiler_params=_cparams(("arbitrary",)),
        name="wkv7_chunked",
    )(v, v, *scaled, *scaled, gc, gc)


def _rwkv_post_kernel(yf_ref, yr_ref, g_ref, bonus_ref, lg_ref, lb_ref, bd_ref, o_ref):
    y = yf_ref[...] + yr_ref[...]
    hi, lo = _split2(y)
    mean = (_group_sum(hi, bd_ref) + _group_sum(lo, bd_ref)) * (1.0 / RWKV_HEAD_DIM)
    yc = y - mean
    var = _group_mean_sq(yc, bd_ref, RWKV_HEAD_DIM)
    yn = yc * lax.rsqrt(var + LNX_EPS)
    out = (yn * lg_ref[...] + lb_ref[...] + bonus_ref[...].astype(F32)) * g_ref[...].astype(F32)
    o_ref[...] = out.astype(o_ref.dtype)


def _rwkv_post(y_fwd, y_rev, g, bonus, lnx_g, lnx_b, bd, *, tm):
    s = g.shape[0]
    return pl.pallas_call(
        _rwkv_post_kernel,
        grid=(s // tm,),
        in_specs=[
            pl.BlockSpec((tm, RWKV_DIM), lambda i: (i, 0)),
            pl.BlockSpec((tm, RWKV_DIM), lambda i: (i, 0)),
            pl.BlockSpec((tm, RWKV_DIM), lambda i: (i, 0)),
            pl.BlockSpec((tm, RWKV_DIM), lambda i: (i, 0)),
            pl.BlockSpec((1, RWKV_DIM), lambda i: (0, 0)),
            pl.BlockSpec((1, RWKV_DIM), lambda i: (0, 0)),
            pl.BlockSpec((GROUP_SLAB, GROUP_SLAB), lambda i: (0, 0)),
        ],
        out_specs=pl.BlockSpec((tm, RWKV_DIM), lambda i: (i, 0)),
        out_shape=jax.ShapeDtypeStruct((s, RWKV_DIM), BF16),
        compiler_params=_cparams(("parallel",)),
        name="rwkv_post",
    )(y_fwd, y_rev, g, bonus, lnx_g.reshape(1, RWKV_DIM), lnx_b.reshape(1, RWKV_DIM), bd)


def _mem_attn_kernel(q_ref, kv_ref, gq_ref, gk_ref, o_ref):
    for h in range(MEM_HEADS):
        sl = slice(MEM_HEAD_DIM * h, MEM_HEAD_DIM * (h + 1))
        q = q_ref[:, sl].astype(F32)
        q = q * lax.rsqrt(jnp.mean(q * q, axis=-1, keepdims=True) + NORM_EPS)
        q = q * gq_ref[...] * (MEM_HEAD_DIM ** -0.5)
        km = kv_ref[:, sl].astype(F32)
        km = km * lax.rsqrt(jnp.mean(km * km, axis=-1, keepdims=True) + NORM_EPS)
        km = km * gk_ref[...]
        vm = kv_ref[:, MEM_DIM + MEM_HEAD_DIM * h:MEM_DIM + MEM_HEAD_DIM * (h + 1)]
        s = lax.dot_general(q.astype(BF16), km.astype(BF16), (((1,), (1,)), ((), ())),
                            preferred_element_type=F32)
        s = s - jnp.max(s, axis=-1, keepdims=True)
        e = jnp.exp(s)
        pr = e / jnp.sum(e, axis=-1, keepdims=True)
        o_ref[:, sl] = _dot(pr.astype(BF16), vm).astype(o_ref.dtype)


def _mem_attn(p, kv, gq, gk, *, tm):
    s = p.shape[0]
    assert COL_MEM % MEM_DIM == 0
    return pl.pallas_call(
        _mem_attn_kernel,
        grid=(s // tm,),
        in_specs=[
            pl.BlockSpec((tm, MEM_DIM), lambda i: (i, COL_MEM // MEM_DIM)),
            pl.BlockSpec((N_MEM, 2 * MEM_DIM), lambda i: (0, 0)),
            pl.BlockSpec((1, MEM_HEAD_DIM), lambda i: (0, 0)),
            pl.BlockSpec((1, MEM_HEAD_DIM), lambda i: (0, 0)),
        ],
        out_specs=pl.BlockSpec((tm, MEM_DIM), lambda i: (i, 0)),
        out_shape=jax.ShapeDtypeStruct((s, MEM_DIM), BF16),
        compiler_params=_cparams(("parallel",)),
        name="mem_attn",
    )(p, kv, gq, gk)


def _merge_kernel(o0_ref, o1_ref, o2_ref, g0_ref, g1_ref, g2_ref, w_ref, m_ref):
    def gate(g_ref):
        return 0.5 * jnp.tanh(0.5 * g_ref[...].astype(F32)) + 0.5

    acc = gate(g0_ref) * _dot(o0_ref[...], w_ref[0])
    acc = acc + gate(g1_ref) * _dot(o1_ref[...], w_ref[1])
    acc = acc + gate(g2_ref) * _dot(o2_ref[...], w_ref[2])
    m_ref[...] = acc.astype(m_ref.dtype)


def _merge(o_diff, o_rwkv, o_mem, p, w_branch, layer, *, tm, tn):
    s = p.shape[0]
    assert COL_GATE % tn == 0 and D_MODEL % tn == 0
    gate_blk = COL_GATE // tn
    per = D_MODEL // tn
    o_spec = lambda: pl.BlockSpec((tm, 1024), lambda i, j: (i, 0))
    g_spec = lambda b: pl.BlockSpec((tm, tn), lambda i, j: (i, gate_blk + b * per + j))
    return pl.pallas_call(
        _merge_kernel,
        grid=(s // tm, per),
        in_specs=[
            o_spec(), o_spec(), o_spec(),
            g_spec(0), g_spec(1), g_spec(2),
            pl.BlockSpec((None, N_BRANCH, 1024, tn), lambda i, j: (layer, 0, 0, j)),
        ],
        out_specs=pl.BlockSpec((tm, tn), lambda i, j: (i, j)),
        out_shape=jax.ShapeDtypeStruct((s, D_MODEL), BF16),
        compiler_params=_cparams(("parallel", "arbitrary")),
        name="merge",
    )(o_diff, o_rwkv, o_mem, p, p, p, w_branch)


GLU_TILE = 256


def _ffn_up_glu_kernel(x_ref, xp_ref, xn_ref, g_ref, wg_ref, wv_ref, cw_ref, cb_ref, o_ref,
                       h_ref, halo_ref):
    i = pl.program_id(0)
    nb = pl.num_programs(0)

    def norm(x):
        ms = jnp.mean(x * x, axis=-1, keepdims=True)
        return x * lax.rsqrt(ms + NORM_EPS) * g_ref[...]

    @pl.when(pl.program_id(1) == 0)
    def _():
        h_ref[...] = norm(x_ref[...]).astype(BF16)
        row = lax.broadcasted_iota(jnp.int32, xp_ref.shape, 0)
        before = jnp.where(i > 0, pltpu.roll(norm(xp_ref[...]), 1, 0), 0.0)
        after = jnp.where(i < nb - 1, pltpu.roll(norm(xn_ref[...]), 1, 0), 0.0)
        halo_ref[...] = jnp.where(row == 0, before, jnp.where(row == 1, after, 0.0)).astype(BF16)

    h = h_ref[...]
    halo = halo_ref[...]
    tn = o_ref.shape[1]
    for c0 in range(0, tn, GLU_TILE):
        sl = slice(c0, c0 + GLU_TILE)
        wg = wg_ref[:, sl]
        ug = _dot(h, wg)
        uv = _dot(h, wv_ref[:, sl])
        edge = _dot(halo, wg)
        gp, gn = _shift_rows(ug, edge[0:1], edge[1:2])
        cw = cw_ref[:, sl]
        conv = cw[0:1] * gp + cw[1:2] * ug + cw[2:3] * gn + cb_ref[:, sl]
        half = 0.5 * conv
        o_ref[:, sl] = ((half + half * jnp.tanh(half)) * uv).astype(o_ref.dtype)


def _ffn_up_glu(x, g, w_up, layer, conv_w, conv_b, *, tm, tn):
    s, k = x.shape
    assert tn % GLU_TILE == 0
    nj = D_FF // tn
    hb = tm // 16
    last16 = s // 16 - 1
    return pl.pallas_call(
        _ffn_up_glu_kernel,
        grid=(s // tm, nj),
        in_specs=[
            pl.BlockSpec((tm, k), lambda i, j: (i, 0)),
            pl.BlockSpec((16, k), lambda i, j: (jnp.maximum(i * hb - 1, 0), 0)),
            pl.BlockSpec((16, k), lambda i, j: (jnp.minimum((i + 1) * hb, last16), 0)),
            pl.BlockSpec((1, k), lambda i, j: (0, 0)),
            pl.BlockSpec((None, k, tn), lambda i, j: (layer, 0, j)),
            pl.BlockSpec((None, k, tn), lambda i, j: (layer, 0, nj + j)),
            pl.BlockSpec((3, tn), lambda i, j: (0, j)),
            pl.BlockSpec((1, tn), lambda i, j: (0, j)),
        ],
        out_specs=pl.BlockSpec((tm, tn), lambda i, j: (i, j)),
        out_shape=jax.ShapeDtypeStruct((s, D_FF), BF16),
        scratch_shapes=[pltpu.VMEM((tm, k), BF16), pltpu.VMEM((16, k), BF16)],
        compiler_params=_cparams(("parallel", "arbitrary")),
        name="ffn_up_glu",
    )(x, x, x, g.reshape(1, k), w_up, w_up, conv_w, conv_b.reshape(1, D_FF))


def _block_ones(n, group):
    idx = np.arange(n) // group
    return jnp.asarray(idx[:, None] == idx[None, :], BF16)


def _chunk_tri(n):
    idx = np.arange(n)
    same = (idx[:, None] // CHUNK) == (idx[None, :] // CHUNK)
    return jnp.asarray(same & (idx[:, None] >= idx[None, :]), BF16)


def _pad_rows(w, rows_before, total):
    n = w.shape[-1]
    out = jnp.zeros((total, n), w.dtype)
    return lax.dynamic_update_slice(out, w, (rows_before, 0))


def _pad_w_in(w):
    lora0 = COL_RKV + 3 * RWKV_DIM
    mem0 = lora0 + LORA_COLS
    gate0 = mem0 + MEM_DIM
    pad = jnp.zeros((w.shape[0], LORA_PAD - LORA_COLS), w.dtype)
    parts = [w[:, :lora0], w[:, gate0:], w[:, mem0:gate0], w[:, lora0:mem0], pad]
    return jnp.concatenate(parts, axis=1).astype(BF16)


def _tile_plan(s):
    rows = 1024 if s % 1024 == 0 else 512
    return dict(
        w_in=dict(tm=rows, tn=2304),
        mem_kv=dict(tm=N_MEM, tn=1024),
        diff_prep=256, rwkv_prep=256, rwkv_post=256, mem_attn=512,
        merge=dict(tm=512, tn=D_MODEL),
        w_out=dict(tm=512, tn=D_MODEL),
        ffn_up=dict(tm=rows, tn=512),
        ffn_down=dict(tm=rows, tn=512),
    )


def kernel(x, mem, attn_norm_g, w_in, diff_qk_g, diff_lambda, diff_subln_g, rwkv_mu, rwkv_w0,
           rwkv_w2, rwkv_a0, rwkv_a2, rwkv_g2, rwkv_k_k, rwkv_k_a, rwkv_r_k, rwkv_lnx_g,
           rwkv_lnx_b, mem_norm_g, w_mem_kv, mem_qk_g, w_branch, w_out, ffn_norm_g, w_ffn_up,
           ffn_conv_w, ffn_conv_b, w_ffn_down):
    b, s, d = x.shape
    assert b == 1 and d == D_MODEL and s % ATT_BQ == 0
    xs = x.reshape(s, d)
    mem2 = mem.reshape(N_MEM, d)
    tiles = _tile_plan(s)
    bd64 = _block_ones(GROUP_SLAB, 64)
    tri = _chunk_tri(tiles["rwkv_prep"])
    n_rwkv_main = 3 * RWKV_DIM
    w_mem_kv_b = w_mem_kv.astype(BF16)
    w_branch_b = w_branch.astype(BF16)
    w_out_b = w_out.astype(BF16)
    w_ffn_up_b = w_ffn_up.astype(BF16)
    w_ffn_down_b = w_ffn_down.astype(BF16)

    for l in range(DEPTH):
        lam_init = 0.8 - 0.6 * math.exp(-0.3 * l)
        p = _rms_mm(xs, attn_norm_g[l], _pad_w_in(w_in[l])[None], 0, name="rms_w_in",
                    **tiles["w_in"])

        gq = jnp.tile(diff_qk_g[l, 0].reshape(1, 128), (1, DIFF_HEADS))
        gk = jnp.tile(diff_qk_g[l, 1].reshape(1, 128), (1, DIFF_HEADS))
        qc, kc, score_bound = _attn_consts(diff_qk_g[l])
        qa, ka, vt = _diff_prep(p, gq, gk, qc, kc, bd64, tm=tiles["diff_prep"])
        o_diff = lax.cond(
            score_bound <= SCORE_BOUND_MAX,
            functools.partial(_diff_attn, lam_init=lam_init, online=False),
            functools.partial(_diff_attn, lam_init=lam_init, online=True),
            qa, ka, vt, diff_lambda[l], diff_subln_g[l])

        mu = rwkv_mu[l]
        mu_rkv = mu[:, :n_rwkv_main]
        mu_lora = jnp.pad(mu[:, n_rwkv_main:], ((0, 0), (0, LORA_PAD - LORA_COLS)))
        w2p = jnp.stack([_pad_rows(rwkv_w2[l, dd], 64 * dd, 128) for dd in range(2)]).astype(BF16)
        a2p = jnp.stack([_pad_rows(rwkv_a2[l, dd], 64 * dd, 128) for dd in range(2)]).astype(BF16)
        g2p = _pad_rows(rwkv_g2[l], 0, 256).astype(BF16)
        (v_h, g_tok, bonus, rb, kb, kt, bt, kh, bh, gc) = _rwkv_prep(
            p, mu_rkv, mu_lora, rwkv_w0[l], w2p, rwkv_a0[l], a2p, g2p,
            rwkv_k_k[l].reshape(1, RWKV_DIM), rwkv_k_a[l].reshape(1, RWKV_DIM),
            rwkv_r_k[l].reshape(1, RWKV_DIM), bd64, tri, tm=tiles["rwkv_prep"])
        y_fwd, y_rev = _wkv(v_h, rb, kb, kt, bt, kh, bh, gc)
        o_rwkv = _rwkv_post(y_fwd, y_rev, g_tok, bonus, rwkv_lnx_g[l], rwkv_lnx_b[l], bd64,
                            tm=tiles["rwkv_post"])

        kv = _rms_mm(mem2, mem_norm_g[l], w_mem_kv_b, l, name="rms_mem_kv", **tiles["mem_kv"])
        o_mem = _mem_attn(p, kv, mem_qk_g[l, 0].reshape(1, MEM_HEAD_DIM),
                          mem_qk_g[l, 1].reshape(1, MEM_HEAD_DIM), tm=tiles["mem_attn"])

        merged = _merge(o_diff, o_rwkv, o_mem, p, w_branch_b, l, **tiles["merge"])
        xs = _mm_res(merged, w_out_b, l, xs, name="w_out_res", **tiles["w_out"])

        act = _ffn_up_glu(xs, ffn_norm_g[l], w_ffn_up_b, l, ffn_conv_w[l], ffn_conv_b[l],
                          **tiles["ffn_up"])
        xs = _mm_res(act, w_ffn_down_b, l, xs, name="ffn_down_res", **tiles["ffn_down"])

    return xs.reshape(b, s, d)
```

```python
import functools
import math

import jax
import jax.numpy as jnp
import numpy as np
from jax import lax
from jax.experimental import pallas as pl
from jax.experimental.pallas import tpu as pltpu

F32 = jnp.float32
BF16 = jnp.bfloat16

D_MODEL = 2048
DEPTH = 2
DIFF_HEADS = 8
DIFF_QK_DIM = 64
DIFF_V_DIM = 128
DIFF_DIM = 1024
RWKV_HEADS = 16
RWKV_HEAD_DIM = 64
RWKV_DIM = 1024
DECAY_LORA = 64
AAA_LORA = 64
GATE_LORA = 160
LORA_COLS = 2 * DECAY_LORA + 2 * AAA_LORA + GATE_LORA
LORA_PAD = 512
N_MEM = 256
MEM_HEADS = 4
MEM_HEAD_DIM = 256
MEM_DIM = 1024
N_BRANCH = 3
D_FF = 5632
NORM_EPS = 1e-6
LNX_EPS = 64e-5

COL_DIFF = 0
COL_RKV = 3 * DIFF_DIM
COL_GATE = COL_RKV + 3 * RWKV_DIM
COL_MEM = COL_GATE + N_BRANCH * D_MODEL
COL_LORA = COL_MEM + MEM_DIM
N_IN_PAD = COL_LORA + LORA_PAD

CHUNK = 64
LOG2E = 1.4426950408889634
N_POS_COLS = 12
SCORE_BOUND_MAX = 30.0
ATT_BQ = 1024
ATT_BK = 512
V_AUG = 144
FAR_UNROLL = 10
VMEM_LIMIT = 56 * 1024 * 1024


def _cparams(sem):
    return pltpu.CompilerParams(dimension_semantics=sem, vmem_limit_bytes=VMEM_LIMIT)


def _split2(x):
    hi = x.astype(BF16)
    return hi, (x - hi.astype(F32)).astype(BF16)


def _dot(a, b):
    return jnp.dot(a, b, preferred_element_type=F32)


def _dot_exactish(a_bf16_exact, x_f32):
    hi, lo = _split2(x_f32)
    return _dot(a_bf16_exact, hi) + _dot(a_bf16_exact, lo)


def _rms_mm_kernel(x_ref, g_ref, w_ref, o_ref, h_ref):
    @pl.when(pl.program_id(1) == 0)
    def _():
        x = x_ref[...]
        ms = jnp.mean(x * x, axis=-1, keepdims=True)
        h_ref[...] = (x * lax.rsqrt(ms + NORM_EPS) * g_ref[...]).astype(BF16)

    o_ref[...] = _dot(h_ref[...], w_ref[...]).astype(o_ref.dtype)


def _rms_mm(x, g, w, layer, *, tm, tn, name):
    m, k = x.shape
    n = w.shape[2]
    return pl.pallas_call(
        _rms_mm_kernel,
        grid=(m // tm, n // tn),
        in_specs=[
            pl.BlockSpec((tm, k), lambda i, j: (i, 0)),
            pl.BlockSpec((1, k), lambda i, j: (0, 0)),
            pl.BlockSpec((None, k, tn), lambda i, j: (layer, 0, j)),
        ],
        out_specs=pl.BlockSpec((tm, tn), lambda i, j: (i, j)),
        out_shape=jax.ShapeDtypeStruct((m, n), BF16),
        scratch_shapes=[pltpu.VMEM((tm, k), BF16)],
        compiler_params=_cparams(("parallel", "arbitrary")),
        name=name,
    )(x, g.reshape(1, k), w)


def _mm_res_kernel(a_ref, w_ref, r_ref, o_ref):
    o_ref[...] = r_ref[...] + _dot(a_ref[...], w_ref[...])


def _mm_res(a, w, layer, res, *, tm, tn, name):
    m, k = a.shape
    n = w.shape[2]
    return pl.pallas_call(
        _mm_res_kernel,
        grid=(m // tm, n // tn),
        in_specs=[
            pl.BlockSpec((tm, k), lambda i, j: (i, 0)),
            pl.BlockSpec((None, k, tn), lambda i, j: (layer, 0, j)),
            pl.BlockSpec((tm, tn), lambda i, j: (i, j)),
        ],
        out_specs=pl.BlockSpec((tm, tn), lambda i, j: (i, j)),
        out_shape=jax.ShapeDtypeStruct((m, n), F32),
        compiler_params=_cparams(("parallel", "arbitrary")),
        name=name,
    )(a, w, res)


GROUP_SLAB = 256


def _group_sum(xb, bd_ref):
    bd = bd_ref[...]
    n = xb.shape[1]
    slabs = [_dot(xb[:, q:q + GROUP_SLAB], bd) for q in range(0, n, GROUP_SLAB)]
    return jnp.concatenate(slabs, axis=1)


def _group_mean_sq(x, bd_ref, group):
    return _group_sum((x * x).astype(BF16), bd_ref) * (1.0 / group)


def _aug_base(mp):
    return 64 if mp == 0 else 0


def _diff_prep_kernel(q_ref, k_ref, v_ref, gq_ref, gk_ref, qc_ref, kc_ref, bd_ref,
                      qa_ref, ka_ref, vt_ref):
    tm = q_ref.shape[0]
    row0 = pl.program_id(0) * tm
    lane = lax.broadcasted_iota(jnp.int32, (tm, 128), 1)
    pos = row0 + lax.broadcasted_iota(jnp.int32, (tm, 128), 0)
    pos_lo = (pos & 127).astype(F32)
    pos_hi = (pos >> 7).astype(F32)

    q = q_ref[...].astype(F32)
    qn = q * lax.rsqrt(_group_mean_sq(q, bd_ref, DIFF_QK_DIM) + NORM_EPS)
    qn = qn * gq_ref[...] * (DIFF_QK_DIM ** -0.5 * LOG2E)
    k = k_ref[...].astype(F32)
    kn = k * lax.rsqrt(_group_mean_sq(k, bd_ref, DIFF_QK_DIM) + NORM_EPS)
    kn = kn * gk_ref[...]

    for h in range(DIFF_HEADS):
        qh = qn[:, 128 * h:128 * (h + 1)]
        kh = kn[:, 128 * h:128 * (h + 1)]
        for mp in range(2):
            a0 = _aug_base(mp)
            g = 2 * h + mp
            is_data = (lane < 64) if mp == 0 else (lane >= 64)
            aug_q = jnp.where(lane < a0 + 3, pos_lo,
                              jnp.where(lane < a0 + 6, pos_hi, qc_ref[g:g + 1, :]))
            aug_k = jnp.where((lane >= a0 + 6) & (lane < a0 + 9), pos_lo,
                              jnp.where((lane >= a0 + 9) & (lane < a0 + 12), pos_hi,
                                        kc_ref[g:g + 1, :]))
            qa_ref[h, mp] = jnp.where(is_data, qh, aug_q).astype(BF16)
            ka_ref[h, mp] = jnp.where(is_data, kh, aug_k).astype(BF16)

    vt = v_ref[...].astype(F32).T
    sub = lax.broadcasted_iota(jnp.int32, (V_AUG - DIFF_V_DIM, tm), 0)
    ones_rows = jnp.where(sub == 0, 1.0, 0.0).astype(BF16)
    for h in range(DIFF_HEADS):
        vt_ref[h, 0:DIFF_V_DIM, :] = vt[128 * h:128 * (h + 1), :].astype(BF16)
        vt_ref[h, DIFF_V_DIM:V_AUG, :] = ones_rows


def _bf16_split3_const(x):
    parts = []
    for _ in range(3):
        part = float(np.asarray(x, np.float32).astype(BF16).astype(np.float32))
        parts.append(part)
        x = x - part
    return parts


def _attn_consts(qk_g):
    l_parts = _bf16_split3_const(LOG2E)
    m_nat = 8.0 * jnp.max(jnp.abs(qk_g[0]), axis=-1) * jnp.max(jnp.abs(qk_g[1]), axis=-1)
    qc = np.zeros((2 * DIFF_HEADS, 128), np.float32)
    kc = np.zeros((2 * DIFF_HEADS, 128), np.float32)
    bound_lane = np.zeros((2, 2 * DIFF_HEADS, 128), np.float32)
    for h in range(DIFF_HEADS):
        slope = 2.0 ** (-(h + 1))
        for mp in range(2):
            a0 = _aug_base(mp)
            g = 2 * h + mp
            for t, lp in enumerate(l_parts):
                qc[g, a0 + 6 + t] = slope * lp
                qc[g, a0 + 9 + t] = 128.0 * slope * lp
                kc[g, a0 + t] = -slope * lp
                kc[g, a0 + 3 + t] = -128.0 * slope * lp
            bound_lane[mp, g, a0 + N_POS_COLS] = 1.0
            kc[g, a0 + N_POS_COLS] = 1.0
    m2 = -m_nat * LOG2E
    qc = qc + m2[0] * bound_lane[0] + m2[1] * bound_lane[1]
    return qc, jnp.asarray(kc), jnp.max(m_nat)


def _diff_prep(p, gq, gk, qc, kc, bd, *, tm):
    s = p.shape[0]
    return pl.pallas_call(
        _diff_prep_kernel,
        grid=(s // tm,),
        in_specs=[
            pl.BlockSpec((tm, DIFF_DIM), lambda i: (i, 0)),
            pl.BlockSpec((tm, DIFF_DIM), lambda i: (i, 1)),
            pl.BlockSpec((tm, DIFF_DIM), lambda i: (i, 2)),
            pl.BlockSpec((1, DIFF_DIM), lambda i: (0, 0)),
            pl.BlockSpec((1, DIFF_DIM), lambda i: (0, 0)),
            pl.BlockSpec((2 * DIFF_HEADS, 128), lambda i: (0, 0)),
            pl.BlockSpec((2 * DIFF_HEADS, 128), lambda i: (0, 0)),
            pl.BlockSpec((GROUP_SLAB, GROUP_SLAB), lambda i: (0, 0)),
        ],
        out_specs=[
            pl.BlockSpec((DIFF_HEADS, 2, tm, 128), lambda i: (0, 0, i, 0)),
            pl.BlockSpec((DIFF_HEADS, 2, tm, 128), lambda i: (0, 0, i, 0)),
            pl.BlockSpec((DIFF_HEADS, V_AUG, tm), lambda i: (0, 0, i)),
        ],
        out_shape=[
            jax.ShapeDtypeStruct((DIFF_HEADS, 2, s, 128), BF16),
            jax.ShapeDtypeStruct((DIFF_HEADS, 2, s, 128), BF16),
            jax.ShapeDtypeStruct((DIFF_HEADS, V_AUG, s), BF16),
        ],
        compiler_params=_cparams(("parallel",)),
        name="diff_prep",
    )(p, p, p, gq, gk, qc, kc, bd)


def _diff_attn_kernel(lam_ref, sg_ref, dnear_ref, qa_ref, ka_ref, vt_ref, o_ref,
                      acc_ref, qv_ref, *mode_refs, lam_init, online):
    h = pl.program_id(0)
    i = pl.program_id(1)
    bq = qa_ref.shape[1]
    s_len = ka_ref.shape[1]
    bk = ATT_BK
    n_chunks = s_len // bk
    per_q = bq // bk
    n_far = n_chunks - per_q
    j_lo = i * per_q
    q0 = i * bq
    slope2 = jnp.exp2(-jnp.full((1, 1), h + 1, jnp.int32).astype(F32)) * LOG2E
    if online:
        m_ref, sa_ref, sb_ref, mxa_ref, mxb_ref = mode_refs
        m_ref[...] = jnp.full_like(m_ref, -1e30)
        bufs = ((sa_ref, mxa_ref), (sb_ref, mxb_ref))
    else:
        bufs = mode_refs
    acc_ref[...] = jnp.zeros_like(acc_ref)

    lane = lax.broadcasted_iota(jnp.int32, (1, 128), 1)
    for mp in range(2):
        q = qa_ref[mp]
        pos_lane = (lane >= _aug_base(mp)) & (lane < _aug_base(mp) + N_POS_COLS)
        qv_ref[mp, 0] = q
        qv_ref[mp, 1] = jnp.where(pos_lane, -q, q)
        qv_ref[mp, 2] = jnp.where(pos_lane, jnp.zeros_like(q), q)

    def stage_one(k0, variant, bias, buf):
        for mp in range(2):
            kc = ka_ref[mp, pl.ds(k0, bk), :]
            s = lax.dot_general(kc, qv_ref[mp, variant], (((1,), (1,)), ((), ())),
                                preferred_element_type=F32)
            if bias is not None:
                s = s + bias
            if online:
                buf[0][mp] = s
                buf[1][mp] = jnp.max(s, axis=0, keepdims=True)
            else:
                buf[mp] = jnp.exp2(s).astype(BF16)

    def stage_two(k0, buf):
        vt_c = vt_ref[:, pl.ds(k0, bk)]
        for mp in range(2):
            if online:
                m_old = m_ref[mp]
                m_new = jnp.maximum(m_old, buf[1][mp])
                p = jnp.exp2(buf[0][mp] - m_new).astype(BF16)
                acc_ref[mp] = acc_ref[mp] * jnp.exp2(m_old - m_new) + _dot(vt_c, p)
                m_ref[mp] = m_new
            else:
                acc_ref[mp] += _dot(vt_c, buf[mp])

    def near_start(t):
        return pl.multiple_of(q0 + t * bk, bk)

    def far_start(t):
        t = jnp.minimum(t, n_far - 1)
        j = jnp.where(t < j_lo, t, t + per_q)
        return pl.multiple_of(j * bk, bk), (t >= j_lo).astype(jnp.int32)

    def near_one(t):
        stage_one(near_start(t), 2, slope2 * dnear_ref[t], bufs[t % 2])

    def far_one(t, parity):
        k0, after = far_start(t)
        stage_one(k0, after, None, bufs[parity])

    def far_two(t, parity):
        stage_two(far_start(t)[0], bufs[parity])

    near_one(0)
    for t in range(1, per_q):
        near_one(t)
        stage_two(near_start(t - 1), bufs[(t - 1) % 2])
    far_one(0, per_q % 2)
    stage_two(near_start(per_q - 1), bufs[(per_q - 1) % 2])

    def far_body(u, c):
        for r in range(FAR_UNROLL):
            far_one(FAR_UNROLL * u + r + 1, (per_q + r + 1) % 2)
            far_two(FAR_UNROLL * u + r, (per_q + r) % 2)
        return c

    lax.fori_loop(0, n_far // FAR_UNROLL, far_body, 0)
    for t in range(n_far - n_far % FAR_UNROLL, n_far):
        if t + 1 < n_far:
            far_one(t + 1, (per_q + t + 1) % 2)
        far_two(t, (per_q + t) % 2)

    lp = lam_ref[...]
    lam = (jnp.exp(jnp.sum(lp[0:1] * lp[1:2], axis=-1, keepdims=True))
           - jnp.exp(jnp.sum(lp[2:3] * lp[3:4], axis=-1, keepdims=True)) + lam_init)
    a0 = acc_ref[0]
    a1 = acc_ref[1]
    o = (a0[0:DIFF_V_DIM] / a0[DIFF_V_DIM:DIFF_V_DIM + 1]
         - lam * (a1[0:DIFF_V_DIM] / a1[DIFF_V_DIM:DIFF_V_DIM + 1]))
    ms = jnp.mean(o * o, axis=0, keepdims=True)
    o = o * lax.rsqrt(ms + NORM_EPS) * sg_ref[...] * (1.0 - lam_init)
    o_ref[...] = o.T.astype(o_ref.dtype)


def _diff_attn(qa, ka, vt, lam_p, subln_g, *, lam_init, online):
    s = qa.shape[2]
    bq = ATT_BQ
    assert bq % ATT_BK == 0 and s > bq
    kern = functools.partial(_diff_attn_kernel, lam_init=lam_init, online=online)
    per_q = bq // ATT_BK
    key = np.arange(per_q * ATT_BK).reshape(per_q, ATT_BK, 1)
    dnear = jnp.asarray(-np.abs(np.arange(bq).reshape(1, 1, bq) - key), F32)
    scratch = [
        pltpu.VMEM((2, V_AUG, bq), F32),
        pltpu.VMEM((2, 3, bq, 128), BF16),
    ]
    if online:
        scratch += [
            pltpu.VMEM((2, 1, bq), F32),
            pltpu.VMEM((2, ATT_BK, bq), F32),
            pltpu.VMEM((2, ATT_BK, bq), F32),
            pltpu.VMEM((2, 1, bq), F32),
            pltpu.VMEM((2, 1, bq), F32),
        ]
    else:
        scratch += [pltpu.VMEM((2, ATT_BK, bq), BF16), pltpu.VMEM((2, ATT_BK, bq), BF16)]
    return pl.pallas_call(
        kern,
        grid=(DIFF_HEADS, s // bq),
        in_specs=[
            pl.BlockSpec((4, DIFF_QK_DIM), lambda h, i: (0, 0)),
            pl.BlockSpec((DIFF_V_DIM, 1), lambda h, i: (0, 0)),
            pl.BlockSpec((per_q, ATT_BK, bq), lambda h, i: (0, 0, 0)),
            pl.BlockSpec((None, 2, bq, 128), lambda h, i: (h, 0, i, 0)),
            pl.BlockSpec((None, 2, s, 128), lambda h, i: (h, 0, 0, 0)),
            pl.BlockSpec((None, V_AUG, s), lambda h, i: (h, 0, 0)),
        ],
        out_specs=pl.BlockSpec((bq, DIFF_V_DIM), lambda h, i: (i, h)),
        out_shape=jax.ShapeDtypeStruct((s, DIFF_DIM), BF16),
        scratch_shapes=scratch,
        compiler_params=_cparams(("parallel", "arbitrary")),
        name="diff_attn_online" if online else "diff_attn",
    )(lam_p, subln_g.reshape(DIFF_V_DIM, 1), dnear, qa, ka, vt)


def _shift_rows(x, prev_row, next_row):
    n = x.shape[0]
    row = lax.broadcasted_iota(jnp.int32, x.shape, 0)
    xp = jnp.where(row == 0, prev_row, pltpu.roll(x, 1, 0))
    xn = jnp.where(row == n - 1, next_row, pltpu.roll(x, n - 1, 0))
    return xp, xn


def _halo_rows(prev_ref, next_ref, i, n_blocks):
    hp = prev_ref.shape[0]
    prev_row = prev_ref[hp - 1:hp, :].astype(F32)
    next_row = next_ref[0:1, :].astype(F32)
    prev_row = jnp.where(i > 0, prev_row, 0.0)
    next_row = jnp.where(i < n_blocks - 1, next_row, 0.0)
    return prev_row, next_row


def _rwkv_prep_kernel(
        x_ref, xp_ref, xn_ref, l_ref, lp_ref, ln_ref,
        mu_ref, mul_ref, w0_ref, w2_ref, a0_ref, a2_ref, g2_ref, kk_ref, ka_ref, rk_ref,
        bd_ref, tri_ref,
        v_out, g_out, bonus_out, rb_out, kb_out, kt_out, bt_out, kh_out, bh_out, gc_out):
    i = pl.program_id(0)
    nb = pl.num_programs(0)
    tm = x_ref.shape[0]

    x = x_ref[...].astype(F32)
    prev_row, next_row = _halo_rows(xp_ref, xn_ref, i, nb)
    xp, xn = _shift_rows(x, prev_row, next_row)
    mu = mu_ref[...]
    x = x + mu[0:1] * (xp - x) + mu[1:2] * (xn - x)
    lo = l_ref[...].astype(F32)
    prev_row, next_row = _halo_rows(lp_ref, ln_ref, i, nb)
    lop, lon = _shift_rows(lo, prev_row, next_row)
    mul = mul_ref[...]
    lo = lo + mul[0:1] * (lop - lo) + mul[1:2] * (lon - lo)

    r = x[:, 0:RWKV_DIM]
    k = x[:, RWKV_DIM:2 * RWKV_DIM]
    v = x[:, 2 * RWKV_DIM:3 * RWKV_DIM]
    tw = jnp.tanh(lo[:, 0:128]).astype(BF16)
    la = lo[:, 128:256].astype(BF16)
    lg = jax.nn.sigmoid(lo[:, 256:512]).astype(BF16)

    g = _dot(lg, g2_ref[...])
    kk = k * kk_ref[...]
    ss = _group_mean_sq(kk, bd_ref, 1.0)
    kk = kk * lax.rsqrt(jnp.maximum(ss, 1e-24))
    bonus = _group_sum((r * k * rk_ref[...]).astype(BF16), bd_ref) * v

    g_out[...] = g.astype(g_out.dtype)
    bonus_out[...] = bonus.astype(bonus_out.dtype)
    v_out[...] = v.astype(BF16)

    tri = tri_ref[...]
    for d in range(2):
        wl = w0_ref[d:d + 1, :] + _dot(tw, w2_ref[d])
        logdec = -math.exp(-0.5) * jax.nn.sigmoid(wl)
        a = jax.nn.sigmoid(a0_ref[d:d + 1, :] + _dot(la, a2_ref[d]))
        k_d = k * (1.0 + (a - 1.0) * ka_ref[...])
        b_d = kk * a
        pre = _dot_exactish(tri, logdec)
        chunk_tot = jnp.broadcast_to(
            pre.reshape(tm // CHUNK, CHUNK, RWKV_DIM)[:, CHUNK - 1:CHUNK, :],
            (tm // CHUNK, CHUNK, RWKV_DIM)).reshape(tm, RWKV_DIM)
        suf = chunk_tot - pre
        if d == 0:
            lc, ex, rem = pre, pre - logdec, suf
        else:
            lc, ex, rem = suf + logdec, suf, pre - logdec
        e_neg = jnp.exp(-lc)
        e_rem = jnp.exp(rem)
        outs = (
            (rb_out, r * jnp.exp(lc)),
            (kb_out, kk * jnp.exp(ex)),
            (kt_out, k_d * e_neg),
            (bt_out, b_d * e_neg),
            (kh_out, k_d * e_rem),
            (bh_out, b_d * e_rem),
        )
        for ref, val in outs:
            ref[d] = val.astype(BF16)
        gc_out[d] = jnp.exp(chunk_tot).reshape(tm // 8, 8, RWKV_DIM)[:, 0, :]


def _rwkv_prep(p, mu_rkv, mu_lora, w0, w2p, a0, a2p, g2p, k_k, k_a, r_k, bd, tri, *, tm):
    s = p.shape[0]
    nb = s // tm
    hb = tm // 16
    last16 = s // 16 - 1
    rkv_w = 3 * RWKV_DIM
    assert COL_RKV % rkv_w == 0 and COL_LORA % LORA_PAD == 0
    c_rkv = COL_RKV // rkv_w
    c_lora = COL_LORA // LORA_PAD

    def prev_map(c):
        return lambda i: (jnp.maximum(i * hb - 1, 0), c)

    def next_map(c):
        return lambda i: (jnp.minimum((i + 1) * hb, last16), c)

    full = lambda *shape: pl.BlockSpec(shape, lambda i: (0,) * len(shape))
    hm = lambda: pl.BlockSpec((2, tm, RWKV_DIM), lambda i: (0, i, 0))
    hm_shape = jax.ShapeDtypeStruct((2, s, RWKV_DIM), BF16)
    return pl.pallas_call(
        _rwkv_prep_kernel,
        grid=(nb,),
        in_specs=[
            pl.BlockSpec((tm, rkv_w), lambda i: (i, c_rkv)),
            pl.BlockSpec((16, rkv_w), prev_map(c_rkv)),
            pl.BlockSpec((16, rkv_w), next_map(c_rkv)),
            pl.BlockSpec((tm, LORA_PAD), lambda i: (i, c_lora)),
            pl.BlockSpec((16, LORA_PAD), prev_map(c_lora)),
            pl.BlockSpec((16, LORA_PAD), next_map(c_lora)),
            full(2, rkv_w), full(2, LORA_PAD),
            full(2, RWKV_DIM), full(2, 128, RWKV_DIM),
            full(2, RWKV_DIM), full(2, 128, RWKV_DIM),
            full(256, RWKV_DIM),
            full(1, RWKV_DIM), full(1, RWKV_DIM), full(1, RWKV_DIM),
            full(GROUP_SLAB, GROUP_SLAB), full(tm, tm),
        ],
        out_specs=[
            pl.BlockSpec((tm, RWKV_DIM), lambda i: (i, 0)),
            pl.BlockSpec((tm, RWKV_DIM), lambda i: (i, 0)),
            pl.BlockSpec((tm, RWKV_DIM), lambda i: (i, 0)),
            hm(), hm(), hm(), hm(), hm(), hm(),
            pl.BlockSpec((2, tm // 8, RWKV_DIM), lambda i: (0, i, 0)),
        ],
        out_shape=[
            jax.ShapeDtypeStruct((s, RWKV_DIM), BF16),
            jax.ShapeDtypeStruct((s, RWKV_DIM), BF16),
            jax.ShapeDtypeStruct((s, RWKV_DIM), BF16),
            hm_shape, hm_shape, hm_shape, hm_shape, hm_shape, hm_shape,
            jax.ShapeDtypeStruct((2, s // 8, RWKV_DIM), F32),
        ],
        compiler_params=_cparams(("parallel",)),
        name="rwkv_prep",
    )(p, p, p, p, p, p, mu_rkv, mu_lora, w0, w2p, a0, a2p, g2p, k_k, k_a, r_k, bd, tri)


def _bdot(a, b):
    return lax.dot_general(a, b, (((2,), (1,)), ((0,), (0,))), preferred_element_type=F32)


def _bdot_nt(a, b):
    return lax.dot_general(a, b, (((2,), (2,)), ((0,), (0,))), preferred_element_type=F32)


def _bdot_tn(a, b):
    return lax.dot_general(a, b, (((1,), (1,)), ((0,), (0,))), preferred_element_type=F32)


WKV_CHUNKS_PER_STEP = 8


def _wkv_kernel(*refs):
    (vf_ref, vr_ref), ins, (gcf_ref, gcr_ref, yf_ref, yr_ref, s_ref) = refs[:2], refs[2:14], refs[14:]

    @pl.when(pl.program_id(0) == 0)
    def _():
        s_ref[...] = jnp.zeros_like(s_ref)

    def rows(ref, part, n):
        return ref.at[n * part:n * (part + 1), :]

    for part in range(WKV_CHUNKS_PER_STEP):
        back = WKV_CHUNKS_PER_STEP - 1 - part
        _wkv_chunk(False, rows(vf_ref, part, CHUNK), [rows(r, part, CHUNK) for r in ins[0:6]],
                   rows(gcf_ref, part, 8), rows(yf_ref, part, CHUNK), s_ref.at[0])
        _wkv_chunk(True, rows(vr_ref, back, CHUNK), [rows(r, back, CHUNK) for r in ins[6:12]],
                   rows(gcr_ref, back, 8), rows(yr_ref, back, CHUNK), s_ref.at[1])


WKV_GROUP = 2
GROUP_LANES = WKV_GROUP * RWKV_HEAD_DIM
N_GROUPS = RWKV_HEADS // WKV_GROUP


def _groups(x):
    return jnp.stack([x[:, GROUP_LANES * p:GROUP_LANES * (p + 1)] for p in range(N_GROUPS)])


def _head_of_lane(shape):
    return lax.broadcasted_iota(jnp.int32, shape, 2) // RWKV_HEAD_DIM


def _group_diag(y):
    head = _head_of_lane(y.shape)
    zero = jnp.zeros_like(y)
    return jnp.concatenate([jnp.where(head == h, y, zero) for h in range(WKV_GROUP)], axis=1)


def _group_tn(a, b):
    full = _bdot_tn(a, b)
    n = RWKV_HEAD_DIM
    head = _head_of_lane((N_GROUPS, n, GROUP_LANES))
    out = full[:, :n]
    for h in range(1, WKV_GROUP):
        out = jnp.where(head == h, full[:, n * h:n * (h + 1)], out)
    return out


def _wkv_chunk(reverse, v_ref, scaled_refs, gc_ref, y_ref, s_ref):
    c = CHUNK
    lanes = GROUP_LANES
    row = lax.broadcasted_iota(jnp.int32, (N_GROUPS, c, lanes), 1)
    col = lax.broadcasted_iota(jnp.int32, (N_GROUPS, c, lanes), 2) % RWKV_HEAD_DIM
    strict = (row < col) if reverse else (row > col)
    incl = (row <= col) if reverse else (row >= col)

    v = _groups(v_ref[...])
    rb, kb, kt, bt, kh, bh = [_groups(r[...]) for r in scaled_refs]

    kr = jnp.concatenate([kb, rb], axis=1)
    sk = _bdot_nt(kr, _group_diag(kt))
    sb = _bdot_nt(kr, _group_diag(bt))
    a_kk = jnp.where(strict, sk[:, :c], 0.0)
    a_rk = jnp.where(incl, sk[:, c:], 0.0)
    l_mat = jnp.where(strict, sb[:, :c], 0.0)
    a_rb = jnp.where(incl, sb[:, c:], 0.0)

    eye = jnp.where(row == col, 1.0, 0.0)
    t_inv = eye - l_mat
    l_b = l_mat.astype(BF16)
    pw = _bdot(l_b, _group_diag(l_b))
    for step in range(5):
        pw_b = _group_diag(pw.astype(BF16))
        if step < 4:
            both = _bdot(jnp.concatenate([t_inv, pw], axis=1).astype(BF16), pw_b)
            t_inv = t_inv + both[:, :c]
            pw = both[:, c:]
        else:
            t_inv = t_inv + _bdot(t_inv.astype(BF16), pw_b)

    q12 = _bdot(jnp.concatenate([a_kk, a_rk], axis=1).astype(BF16), _group_diag(v))
    q1, q2 = q12[:, :c], q12[:, c:]
    t_b = t_inv.astype(BF16)
    kq = jnp.concatenate([_group_diag(kb), _group_diag(q1.astype(BF16))], axis=2)
    wkuv = _bdot(t_b, kq).astype(BF16)
    wk, uv = wkuv[:, :, :lanes], wkuv[:, :, lanes:]
    corr = _bdot(a_rb.astype(BF16), jnp.concatenate([_group_diag(wk), _group_diag(uv)], axis=2))
    rw = rb.astype(F32) - corr[:, :, :lanes]
    y0 = q2 - corr[:, :, lanes:]
    m_mat = _group_tn(wk, bh)
    j_t = _group_tn(jnp.concatenate([v, uv], axis=1), jnp.concatenate([kh, -bh], axis=1))

    n = RWKV_HEAD_DIM
    s_old = s_ref[...]
    s_hi = s_old.astype(BF16)
    s_lo = (s_old - s_hi.astype(F32)).astype(BF16)
    y = _bdot_nt(rw.astype(BF16), _group_diag(s_hi)) + y0
    sm = _bdot(jnp.concatenate([s_hi, s_lo], axis=1), _group_diag(m_mat.astype(BF16)))
    s_ref[...] = s_old * _groups(gc_ref[0:1, :]) - (sm[:, :n] + sm[:, n:]) + j_t
    for p in range(N_GROUPS):
        y_ref[:, lanes * p:lanes * (p + 1)] = y[p]


def _wkv(v, rb, kb, kt, bt, kh, bh, gc):
    s = v.shape[0]
    per = WKV_CHUNKS_PER_STEP
    assert s % (per * CHUNK) == 0
    nc = s // (per * CHUNK)
    scaled = (rb, kb, kt, bt, kh, bh)
    fwd = lambda rows: pl.BlockSpec((None, rows, RWKV_DIM), lambda c: (0, c, 0))
    rev = lambda rows: pl.BlockSpec((None, rows, RWKV_DIM), lambda c: (1, nc - 1 - c, 0))
    y_shape = jax.ShapeDtypeStruct((s, RWKV_DIM), F32)
    return pl.pallas_call(
        _wkv_kernel,
        grid=(nc,),
        in_specs=[
            pl.BlockSpec((per * CHUNK, RWKV_DIM), lambda c: (c, 0)),
            pl.BlockSpec((per * CHUNK, RWKV_DIM), lambda c: (nc - 1 - c, 0)),
            *[fwd(per * CHUNK) for _ in scaled], *[rev(per * CHUNK) for _ in scaled],
            fwd(per * 8), rev(per * 8),
        ],
        out_specs=[
            pl.BlockSpec((per * CHUNK, RWKV_DIM), lambda c: (c, 0)),
            pl.BlockSpec((per * CHUNK, RWKV_DIM), lambda c: (nc - 1 - c, 0)),
        ],
        out_shape=[y_shape, y_shape],
        scratch_shapes=[pltpu.VMEM((2, N_GROUPS, RWKV_HEAD_DIM, GROUP_LANES), F32)],
        compiler_params=_cparams(("arbitrary",)),
        name="wkv7_chunked",
    )(v, v, *scaled, *scaled, gc, gc)


def _rwkv_post_kernel(yf_ref, yr_ref, g_ref, bonus_ref, lg_ref, lb_ref, bd_ref, o_ref):
    y = yf_ref[...] + yr_ref[...]
    hi, lo = _split2(y)
    mean = (_group_sum(hi, bd_ref) + _group_sum(lo, bd_ref)) * (1.0 / RWKV_HEAD_DIM)
    yc = y - mean
    var = _group_mean_sq(yc, bd_ref, RWKV_HEAD_DIM)
    yn = yc * lax.rsqrt(var + LNX_EPS)
    out = (yn * lg_ref[...] + lb_ref[...] + bonus_ref[...].astype(F32)) * g_ref[...].astype(F32)
    o_ref[...] = out.astype(o_ref.dtype)


def _rwkv_post(y_fwd, y_rev, g, bonus, lnx_g, lnx_b, bd, *, tm):
    s = g.shape[0]
    return pl.pallas_call(
        _rwkv_post_kernel,
        grid=(s // tm,),
        in_specs=[
            pl.BlockSpec((tm, RWKV_DIM), lambda i: (i, 0)),
            pl.BlockSpec((tm, RWKV_DIM), lambda i: (i, 0)),
            pl.BlockSpec((tm, RWKV_DIM), lambda i: (i, 0)),
            pl.BlockSpec((tm, RWKV_DIM), lambda i: (i, 0)),
            pl.BlockSpec((1, RWKV_DIM), lambda i: (0, 0)),
            pl.BlockSpec((1, RWKV_DIM), lambda i: (0, 0)),
            pl.BlockSpec((GROUP_SLAB, GROUP_SLAB), lambda i: (0, 0)),
        ],
        out_specs=pl.BlockSpec((tm, RWKV_DIM), lambda i: (i, 0)),
        out_shape=jax.ShapeDtypeStruct((s, RWKV_DIM), BF16),
        compiler_params=_cparams(("parallel",)),
        name="rwkv_post",
    )(y_fwd, y_rev, g, bonus, lnx_g.reshape(1, RWKV_DIM), lnx_b.reshape(1, RWKV_DIM), bd)


def _mem_attn_kernel(q_ref, kv_ref, gq_ref, gk_ref, o_ref):
    for h in range(MEM_HEADS):
        sl = slice(MEM_HEAD_DIM * h, MEM_HEAD_DIM * (h + 1))
        q = q_ref[:, sl].astype(F32)
        q = q * lax.rsqrt(jnp.mean(q * q, axis=-1, keepdims=True) + NORM_EPS)
        q = q * gq_ref[...] * (MEM_HEAD_DIM ** -0.5)
        km = kv_ref[:, sl].astype(F32)
        km = km * lax.rsqrt(jnp.mean(km * km, axis=-1, keepdims=True) + NORM_EPS)
        km = km * gk_ref[...]
        vm = kv_ref[:, MEM_DIM + MEM_HEAD_DIM * h:MEM_DIM + MEM_HEAD_DIM * (h + 1)]
        s = lax.dot_general(q.astype(BF16), km.astype(BF16), (((1,), (1,)), ((), ())),
                            preferred_element_type=F32)
        s = s - jnp.max(s, axis=-1, keepdims=True)
        e = jnp.exp(s)
        pr = e / jnp.sum(e, axis=-1, keepdims=True)
        o_ref[:, sl] = _dot(pr.astype(BF16), vm).astype(o_ref.dtype)


def _mem_attn(p, kv, gq, gk, *, tm):
    s = p.shape[0]
    assert COL_MEM % MEM_DIM == 0
    return pl.pallas_call(
        _mem_attn_kernel,
        grid=(s // tm,),
        in_specs=[
            pl.BlockSpec((tm, MEM_DIM), lambda i: (i, COL_MEM // MEM_DIM)),
            pl.BlockSpec((N_MEM, 2 * MEM_DIM), lambda i: (0, 0)),
            pl.BlockSpec((1, MEM_HEAD_DIM), lambda i: (0, 0)),
            pl.BlockSpec((1, MEM_HEAD_DIM), lambda i: (0, 0)),
        ],
        out_specs=pl.BlockSpec((tm, MEM_DIM), lambda i: (i, 0)),
        out_shape=jax.ShapeDtypeStruct((s, MEM_DIM), BF16),
        compiler_params=_cparams(("parallel",)),
        name="mem_attn",
    )(p, kv, gq, gk)


def _merge_kernel(o0_ref, o1_ref, o2_ref, g0_ref, g1_ref, g2_ref, w_ref, m_ref):
    def gate(g_ref):
        return 0.5 * jnp.tanh(0.5 * g_ref[...].astype(F32)) + 0.5

    acc = gate(g0_ref) * _dot(o0_ref[...], w_ref[0])
    acc = acc + gate(g1_ref) * _dot(o1_ref[...], w_ref[1])
    acc = acc + gate(g2_ref) * _dot(o2_ref[...], w_ref[2])
    m_ref[...] = acc.astype(m_ref.dtype)


def _merge(o_diff, o_rwkv, o_mem, p, w_branch, layer, *, tm, tn):
    s = p.shape[0]
    assert COL_GATE % tn == 0 and D_MODEL % tn == 0
    gate_blk = COL_GATE // tn
    per = D_MODEL // tn
    o_spec = lambda: pl.BlockSpec((tm, 1024), lambda i, j: (i, 0))
    g_spec = lambda b: pl.BlockSpec((tm, tn), lambda i, j: (i, gate_blk + b * per + j))
    return pl.pallas_call(
        _merge_kernel,
        grid=(s // tm, per),
        in_specs=[
            o_spec(), o_spec(), o_spec(),
            g_spec(0), g_spec(1), g_spec(2),
            pl.BlockSpec((None, N_BRANCH, 1024, tn), lambda i, j: (layer, 0, 0, j)),
        ],
        out_specs=pl.BlockSpec((tm, tn), lambda i, j: (i, j)),
        out_shape=jax.ShapeDtypeStruct((s, D_MODEL), BF16),
        compiler_params=_cparams(("parallel", "arbitrary")),
        name="merge",
    )(o_diff, o_rwkv, o_mem, p, p, p, w_branch)


GLU_TILE = 256


def _ffn_up_glu_kernel(x_ref, xp_ref, xn_ref, g_ref, wg_ref, wv_ref, cw_ref, cb_ref, o_ref,
                       h_ref, halo_ref):
    i = pl.program_id(0)
    nb = pl.num_programs(0)

    def norm(x):
        ms = jnp.mean(x * x, axis=-1, keepdims=True)
        return x * lax.rsqrt(ms + NORM_EPS) * g_ref[...]

    @pl.when(pl.program_id(1) == 0)
    def _():
        h_ref[...] = norm(x_ref[...]).astype(BF16)
        row = lax.broadcasted_iota(jnp.int32, xp_ref.shape, 0)
        before = jnp.where(i > 0, pltpu.roll(norm(xp_ref[...]), 1, 0), 0.0)
        after = jnp.where(i < nb - 1, pltpu.roll(norm(xn_ref[...]), 1, 0), 0.0)
        halo_ref[...] = jnp.where(row == 0, before, jnp.where(row == 1, after, 0.0)).astype(BF16)

    h = h_ref[...]
    halo = halo_ref[...]
    tn = o_ref.shape[1]
    for c0 in range(0, tn, GLU_TILE):
        sl = slice(c0, c0 + GLU_TILE)
        wg = wg_ref[:, sl]
        ug = _dot(h, wg)
        uv = _dot(h, wv_ref[:, sl])
        edge = _dot(halo, wg)
        gp, gn = _shift_rows(ug, edge[0:1], edge[1:2])
        cw = cw_ref[:, sl]
        conv = cw[0:1] * gp + cw[1:2] * ug + cw[2:3] * gn + cb_ref[:, sl]
        half = 0.5 * conv
        o_ref[:, sl] = ((half + half * jnp.tanh(half)) * uv).astype(o_ref.dtype)


def _ffn_up_glu(x, g, w_up, layer, conv_w, conv_b, *, tm, tn):
    s, k = x.shape
    assert tn % GLU_TILE == 0
    nj = D_FF // tn
    hb = tm // 16
    last16 = s // 16 - 1
    return pl.pallas_call(
        _ffn_up_glu_kernel,
        grid=(s // tm, nj),
        in_specs=[
            pl.BlockSpec((tm, k), lambda i, j: (i, 0)),
            pl.BlockSpec((16, k), lambda i, j: (jnp.maximum(i * hb - 1, 0), 0)),
            pl.BlockSpec((16, k), lambda i, j: (jnp.minimum((i + 1) * hb, last16), 0)),
            pl.BlockSpec((1, k), lambda i, j: (0, 0)),
            pl.BlockSpec((None, k, tn), lambda i, j: (layer, 0, j)),
            pl.BlockSpec((None, k, tn), lambda i, j: (layer, 0, nj + j)),
            pl.BlockSpec((3, tn), lambda i, j: (0, j)),
            pl.BlockSpec((1, tn), lambda i, j: (0, j)),
        ],
        out_specs=pl.BlockSpec((tm, tn), lambda i, j: (i, j)),
        out_shape=jax.ShapeDtypeStruct((s, D_FF), BF16),
        scratch_shapes=[pltpu.VMEM((tm, k), BF16), pltpu.VMEM((16, k), BF16)],
        compiler_params=_cparams(("parallel", "arbitrary")),
        name="ffn_up_glu",
    )(x, x, x, g.reshape(1, k), w_up, w_up, conv_w, conv_b.reshape(1, D_FF))


def _block_ones(n, group):
    idx = np.arange(n) // group
    return jnp.asarray(idx[:, None] == idx[None, :], BF16)


def _chunk_tri(n):
    idx = np.arange(n)
    same = (idx[:, None] // CHUNK) == (idx[None, :] // CHUNK)
    return jnp.asarray(same & (idx[:, None] >= idx[None, :]), BF16)


def _pad_rows(w, rows_before, total):
    n = w.shape[-1]
    out = jnp.zeros((total, n), w.dtype)
    return lax.dynamic_update_slice(out, w, (rows_before, 0))


def _pad_w_in(w):
    lora0 = COL_RKV + 3 * RWKV_DIM
    mem0 = lora0 + LORA_COLS
    gate0 = mem0 + MEM_DIM
    pad = jnp.zeros((w.shape[0], LORA_PAD - LORA_COLS), w.dtype)
    parts = [w[:, :lora0], w[:, gate0:], w[:, mem0:gate0], w[:, lora0:mem0], pad]
    return jnp.concatenate(parts, axis=1).astype(BF16)


def _tile_plan(s):
    rows = 1024 if s % 1024 == 0 else 512
    return dict(
        w_in=dict(tm=rows, tn=2304),
        mem_kv=dict(tm=N_MEM, tn=1024),
        diff_prep=256, rwkv_prep=256, rwkv_post=256, mem_attn=512,
        merge=dict(tm=512, tn=D_MODEL),
        w_out=dict(tm=512, tn=D_MODEL),
        ffn_up=dict(tm=rows, tn=512),
        ffn_down=dict(tm=rows, tn=512),
    )


def kernel(x, mem, attn_norm_g, w_in, diff_qk_g, diff_lambda, diff_subln_g, rwkv_mu, rwkv_w0,
           rwkv_w2, rwkv_a0, rwkv_a2, rwkv_g2, rwkv_k_k, rwkv_k_a, rwkv_r_k, rwkv_lnx_g,
           rwkv_lnx_b, mem_norm_g, w_mem_kv, mem_qk_g, w_branch, w_out, ffn_norm_g, w_ffn_up,
           ffn_conv_w, ffn_conv_b, w_ffn_down):
    b, s, d = x.shape
    assert b == 1 and d == D_MODEL and s % ATT_BQ == 0
    xs = x.reshape(s, d)
    mem2 = mem.reshape(N_MEM, d)
    tiles = _tile_plan(s)
    bd64 = _block_ones(GROUP_SLAB, 64)
    tri = _chunk_tri(tiles["rwkv_prep"])
    n_rwkv_main = 3 * RWKV_DIM
    w_mem_kv_b = w_mem_kv.astype(BF16)
    w_branch_b = w_branch.astype(BF16)
    w_out_b = w_out.astype(BF16)
    w_ffn_up_b = w_ffn_up.astype(BF16)
    w_ffn_down_b = w_ffn_down.astype(BF16)

    for l in range(DEPTH):
        lam_init = 0.8 - 0.6 * math.exp(-0.3 * l)
        p = _rms_mm(xs, attn_norm_g[l], _pad_w_in(w_in[l])[None], 0, name="rms_w_in",
                    **tiles["w_in"])

        gq = jnp.tile(diff_qk_g[l, 0].reshape(1, 128), (1, DIFF_HEADS))
        gk = jnp.tile(diff_qk_g[l, 1].reshape(1, 128), (1, DIFF_HEADS))
        qc, kc, score_bound = _attn_consts(diff_qk_g[l])
        qa, ka, vt = _diff_prep(p, gq, gk, qc, kc, bd64, tm=tiles["diff_prep"])
        o_diff = lax.cond(
            score_bound <= SCORE_BOUND_MAX,
            functools.partial(_diff_attn, lam_init=lam_init, online=False),
            functools.partial(_diff_attn, lam_init=lam_init, online=True),
            qa, ka, vt, diff_lambda[l], diff_subln_g[l])

        mu = rwkv_mu[l]
        mu_rkv = mu[:, :n_rwkv_main]
        mu_lora = jnp.pad(mu[:, n_rwkv_main:], ((0, 0), (0, LORA_PAD - LORA_COLS)))
        w2p = jnp.stack([_pad_rows(rwkv_w2[l, dd], 64 * dd, 128) for dd in range(2)]).astype(BF16)
        a2p = jnp.stack([_pad_rows(rwkv_a2[l, dd], 64 * dd, 128) for dd in range(2)]).astype(BF16)
        g2p = _pad_rows(rwkv_g2[l], 0, 256).astype(BF16)
        (v_h, g_tok, bonus, rb, kb, kt, bt, kh, bh, gc) = _rwkv_prep(
            p, mu_rkv, mu_lora, rwkv_w0[l], w2p, rwkv_a0[l], a2p, g2p,
            rwkv_k_k[l].reshape(1, RWKV_DIM), rwkv_k_a[l].reshape(1, RWKV_DIM),
            rwkv_r_k[l].reshape(1, RWKV_DIM), bd64, tri, tm=tiles["rwkv_prep"])
        y_fwd, y_rev = _wkv(v_h, rb, kb, kt, bt, kh, bh, gc)
        o_rwkv = _rwkv_post(y_fwd, y_rev, g_tok, bonus, rwkv_lnx_g[l], rwkv_lnx_b[l], bd64,
                            tm=tiles["rwkv_post"])

        kv = _rms_mm(mem2, mem_norm_g[l], w_mem_kv_b, l, name="rms_mem_kv", **tiles["mem_kv"])
        o_mem = _mem_attn(p, kv, mem_qk_g[l, 0].reshape(1, MEM_HEAD_DIM),
                          mem_qk_g[l, 1].reshape(1, MEM_HEAD_DIM), tm=tiles["mem_attn"])

        merged = _merge(o_diff, o_rwkv, o_mem, p, w_branch_b, l, **tiles["merge"])
        xs = _mm_res(merged, w_out_b, l, xs, name="w_out_res", **tiles["w_out"])

        act = _ffn_up_glu(xs, ffn_norm_g[l], w_ffn_up_b, l, ffn_conv_w[l], ffn_conv_b[l],
                          **tiles["ffn_up"])
        xs = _mm_res(act, w_ffn_down_b, l, xs, name="ffn_down_res", **tiles["ffn_down"])

    return xs.reshape(b, s, d)
```

```python
import functools
import math

import jax
import jax.numpy as jnp
import numpy as np
from jax import lax
from jax.experimental import pallas as pl
from jax.experimental.pallas import tpu as pltpu

F32 = jnp.float32
BF16 = jnp.bfloat16

D_MODEL = 2048
DEPTH = 2
DIFF_HEADS = 8
DIFF_QK_DIM = 64
DIFF_V_DIM = 128
DIFF_DIM = 1024
RWKV_HEADS = 16
RWKV_HEAD_DIM = 64
RWKV_DIM = 1024
DECAY_LORA = 64
AAA_LORA = 64
GATE_LORA = 160
LORA_COLS = 2 * DECAY_LORA + 2 * AAA_LORA + GATE_LORA
LORA_PAD = 512
N_MEM = 256
MEM_HEADS = 4
MEM_HEAD_DIM = 256
MEM_DIM = 1024
N_BRANCH = 3
D_FF = 5632
NORM_EPS = 1e-6
LNX_EPS = 64e-5

COL_DIFF = 0
COL_RKV = 3 * DIFF_DIM
COL_GATE = COL_RKV + 3 * RWKV_DIM
COL_MEM = COL_GATE + N_BRANCH * D_MODEL
COL_LORA = COL_MEM + MEM_DIM
N_IN_PAD = COL_LORA + LORA_PAD

CHUNK = 64
LOG2E = 1.4426950408889634
N_POS_COLS = 12
SCORE_BOUND_MAX = 30.0
ATT_BQ = 1024
ATT_BK = 512
V_AUG = 144
FAR_UNROLL = 10
VMEM_LIMIT = 56 * 1024 * 1024


def _cparams(sem):
    return pltpu.CompilerParams(dimension_semantics=sem, vmem_limit_bytes=VMEM_LIMIT)


def _split2(x):
    hi = x.astype(BF16)
    return hi, (x - hi.astype(F32)).astype(BF16)


def _dot(a, b):
    return jnp.dot(a, b, preferred_element_type=F32)


def _dot_exactish(a_bf16_exact, x_f32):
    hi, lo = _split2(x_f32)
    return _dot(a_bf16_exact, hi) + _dot(a_bf16_exact, lo)


def _rms_mm_kernel(x_ref, g_ref, w_ref, o_ref, h_ref):
    @pl.when(pl.program_id(1) == 0)
    def _():
        x = x_ref[...]
        ms = jnp.mean(x * x, axis=-1, keepdims=True)
        h_ref[...] = (x * lax.rsqrt(ms + NORM_EPS) * g_ref[...]).astype(BF16)

    o_ref[...] = _dot(h_ref[...], w_ref[...]).astype(o_ref.dtype)


def _rms_mm(x, g, w, layer, *, tm, tn, name):
    m, k = x.shape
    n = w.shape[2]
    return pl.pallas_call(
        _rms_mm_kernel,
        grid=(m // tm, n // tn),
        in_specs=[
            pl.BlockSpec((tm, k), lambda i, j: (i, 0)),
            pl.BlockSpec((1, k), lambda i, j: (0, 0)),
            pl.BlockSpec((None, k, tn), lambda i, j: (layer, 0, j)),
        ],
        out_specs=pl.BlockSpec((tm, tn), lambda i, j: (i, j)),
        out_shape=jax.ShapeDtypeStruct((m, n), BF16),
        scratch_shapes=[pltpu.VMEM((tm, k), BF16)],
        compiler_params=_cparams(("parallel", "arbitrary")),
        name=name,
    )(x, g.reshape(1, k), w)


def _mm_res_kernel(a_ref, w_ref, r_ref, o_ref):
    o_ref[...] = r_ref[...] + _dot(a_ref[...], w_ref[...])


def _mm_res(a, w, layer, res, *, tm, tn, name):
    m, k = a.shape
    n = w.shape[2]
    return pl.pallas_call(
        _mm_res_kernel,
        grid=(m // tm, n // tn),
        in_specs=[
            pl.BlockSpec((tm, k), lambda i, j: (i, 0)),
            pl.BlockSpec((None, k, tn), lambda i, j: (layer, 0, j)),
            pl.BlockSpec((tm, tn), lambda i, j: (i, j)),
        ],
        out_specs=pl.BlockSpec((tm, tn), lambda i, j: (i, j)),
        out_shape=jax.ShapeDtypeStruct((m, n), F32),
        compiler_params=_cparams(("parallel", "arbitrary")),
        name=name,
    )(a, w, res)


GROUP_SLAB = 256


def _group_sum(xb, bd_ref):
    bd = bd_ref[...]
    n = xb.shape[1]
    slabs = [_dot(xb[:, q:q + GROUP_SLAB], bd) for q in range(0, n, GROUP_SLAB)]
    return jnp.concatenate(slabs, axis=1)


def _group_mean_sq(x, bd_ref, group):
    return _group_sum((x * x).astype(BF16), bd_ref) * (1.0 / group)


def _aug_base(mp):
    return 64 if mp == 0 else 0


def _diff_prep_kernel(q_ref, k_ref, v_ref, gq_ref, gk_ref, qc_ref, kc_ref, bd_ref,
                      qa_ref, ka_ref, vt_ref):
    tm = q_ref.shape[0]
    row0 = pl.program_id(0) * tm
    lane = lax.broadcasted_iota(jnp.int32, (tm, 128), 1)
    pos = row0 + lax.broadcasted_iota(jnp.int32, (tm, 128), 0)
    pos_lo = (pos & 127).astype(F32)
    pos_hi = (pos >> 7).astype(F32)

    q = q_ref[...].astype(F32)
    qn = q * lax.rsqrt(_group_mean_sq(q, bd_ref, DIFF_QK_DIM) + NORM_EPS)
    qn = qn * gq_ref[...] * (DIFF_QK_DIM ** -0.5 * LOG2E)
    k = k_ref[...].astype(F32)
    kn = k * lax.rsqrt(_group_mean_sq(k, bd_ref, DIFF_QK_DIM) + NORM_EPS)
    kn = kn * gk_ref[...]

    for h in range(DIFF_HEADS):
        qh = qn[:, 128 * h:128 * (h + 1)]
        kh = kn[:, 128 * h:128 * (h + 1)]
        for mp in range(2):
            a0 = _aug_base(mp)
            g = 2 * h + mp
            is_data = (lane < 64) if mp == 0 else (lane >= 64)
            aug_q = jnp.where(lane < a0 + 3, pos_lo,
                              jnp.where(lane < a0 + 6, pos_hi, qc_ref[g:g + 1, :]))
            aug_k = jnp.where((lane >= a0 + 6) & (lane < a0 + 9), pos_lo,
                              jnp.where((lane >= a0 + 9) & (lane < a0 + 12), pos_hi,
                                        kc_ref[g:g + 1, :]))
            qa_ref[h, mp] = jnp.where(is_data, qh, aug_q).astype(BF16)
            ka_ref[h, mp] = jnp.where(is_data, kh, aug_k).astype(BF16)

    vt = v_ref[...].astype(F32).T
    sub = lax.broadcasted_iota(jnp.int32, (V_AUG - DIFF_V_DIM, tm), 0)
    ones_rows = jnp.where(sub == 0, 1.0, 0.0).astype(BF16)
    for h in range(DIFF_HEADS):
        vt_ref[h, 0:DIFF_V_DIM, :] = vt[128 * h:128 * (h + 1), :].astype(BF16)
        vt_ref[h, DIFF_V_DIM:V_AUG, :] = ones_rows


def _bf16_split3_const(x):
    parts = []
    for _ in range(3):
        part = float(np.asarray(x, np.float32).astype(BF16).astype(np.float32))
        parts.append(part)
        x = x - part
    return parts


def _attn_consts(qk_g):
    l_parts = _bf16_split3_const(LOG2E)
    m_nat = 8.0 * jnp.max(jnp.abs(qk_g[0]), axis=-1) * jnp.max(jnp.abs(qk_g[1]), axis=-1)
    qc = np.zeros((2 * DIFF_HEADS, 128), np.float32)
    kc = np.zeros((2 * DIFF_HEADS, 128), np.float32)
    bound_lane = np.zeros((2, 2 * DIFF_HEADS, 128), np.float32)
    for h in range(DIFF_HEADS):
        slope = 2.0 ** (-(h + 1))
        for mp in range(2):
            a0 = _aug_base(mp)
            g = 2 * h + mp
            for t, lp in enumerate(l_parts):
                qc[g, a0 + 6 + t] = slope * lp
                qc[g, a0 + 9 + t] = 128.0 * slope * lp
                kc[g, a0 + t] = -slope * lp
                kc[g, a0 + 3 + t] = -128.0 * slope * lp
            bound_lane[mp, g, a0 + N_POS_COLS] = 1.0
            kc[g, a0 + N_POS_COLS] = 1.0
    m2 = -m_nat * LOG2E
    qc = qc + m2[0] * bound_lane[0] + m2[1] * bound_lane[1]
    return qc, jnp.asarray(kc), jnp.max(m_nat)


def _diff_prep(p, gq, gk, qc, kc, bd, *, tm):
    s = p.shape[0]
    return pl.pallas_call(
        _diff_prep_kernel,
        grid=(s // tm,),
        in_specs=[
            pl.BlockSpec((tm, DIFF_DIM), lambda i: (i, 0)),
            pl.BlockSpec((tm, DIFF_DIM), lambda i: (i, 1)),
            pl.BlockSpec((tm, DIFF_DIM), lambda i: (i, 2)),
            pl.BlockSpec((1, DIFF_DIM), lambda i: (0, 0)),
            pl.BlockSpec((1, DIFF_DIM), lambda i: (0, 0)),
            pl.BlockSpec((2 * DIFF_HEADS, 128), lambda i: (0, 0)),
            pl.BlockSpec((2 * DIFF_HEADS, 128), lambda i: (0, 0)),
            pl.BlockSpec((GROUP_SLAB, GROUP_SLAB), lambda i: (0, 0)),
        ],
        out_specs=[
            pl.BlockSpec((DIFF_HEADS, 2, tm, 128), lambda i: (0, 0, i, 0)),
            pl.BlockSpec((DIFF_HEADS, 2, tm, 128), lambda i: (0, 0, i, 0)),
            pl.BlockSpec((DIFF_HEADS, V_AUG, tm), lambda i: (0, 0, i)),
        ],
        out_shape=[
            jax.ShapeDtypeStruct((DIFF_HEADS, 2, s, 128), BF16),
            jax.ShapeDtypeStruct((DIFF_HEADS, 2, s, 128), BF16),
            jax.ShapeDtypeStruct((DIFF_HEADS, V_AUG, s), BF16),
        ],
        compiler_params=_cparams(("parallel",)),
        name="diff_prep",
    )(p, p, p, gq, gk, qc, kc, bd)


def _diff_attn_kernel(lam_ref, sg_ref, dnear_ref, qa_ref, ka_ref, vt_ref, o_ref,
                      acc_ref, qv_ref, *mode_refs, lam_init, online):
    h = pl.program_id(0)
    i = pl.program_id(1)
    bq = qa_ref.shape[1]
    s_len = ka_ref.shape[1]
    bk = ATT_BK
    n_chunks = s_len // bk
    per_q = bq // bk
    n_far = n_chunks - per_q
    j_lo = i * per_q
    q0 = i * bq
    slope2 = jnp.exp2(-jnp.full((1, 1), h + 1, jnp.int32).astype(F32)) * LOG2E
    if online:
        m_ref, sa_ref, sb_ref, mxa_ref, mxb_ref = mode_refs
        m_ref[...] = jnp.full_like(m_ref, -1e30)
        bufs = ((sa_ref, mxa_ref), (sb_ref, mxb_ref))
    else:
        bufs = mode_refs
    acc_ref[...] = jnp.zeros_like(acc_ref)

    lane = lax.broadcasted_iota(jnp.int32, (1, 128), 1)
    for mp in range(2):
        q = qa_ref[mp]
        pos_lane = (lane >= _aug_base(mp)) & (lane < _aug_base(mp) + N_POS_COLS)
        qv_ref[mp, 0] = q
        qv_ref[mp, 1] = jnp.where(pos_lane, -q, q)
        qv_ref[mp, 2] = jnp.where(pos_lane, jnp.zeros_like(q), q)

    def stage_one(k0, variant, bias, buf):
        for mp in range(2):
            kc = ka_ref[mp, pl.ds(k0, bk), :]
            s = lax.dot_general(kc, qv_ref[mp, variant], (((1,), (1,)), ((), ())),
                                preferred_element_type=F32)
            if bias is not None:
                s = s + bias
            if online:
                buf[0][mp] = s
                buf[1][mp] = jnp.max(s, axis=0, keepdims=True)
            else:
                buf[mp] = jnp.exp2(s).astype(BF16)

    def stage_two(k0, buf):
        vt_c = vt_ref[:, pl.ds(k0, bk)]
        for mp in range(2):
            if online:
                m_old = m_ref[mp]
                m_new = jnp.maximum(m_old, buf[1][mp])
                p = jnp.exp2(buf[0][mp] - m_new).astype(BF16)
                acc_ref[mp] = acc_ref[mp] * jnp.exp2(m_old - m_new) + _dot(vt_c, p)
                m_ref[mp] = m_new
            else:
                acc_ref[mp] += _dot(vt_c, buf[mp])

    def near_start(t):
        return pl.multiple_of(q0 + t * bk, bk)

    def far_start(t):
        t = jnp.minimum(t, n_far - 1)
        j = jnp.where(t < j_lo, t, t + per_q)
        return pl.multiple_of(j * bk, bk), (t >= j_lo).astype(jnp.int32)

    def near_one(t):
        stage_one(near_start(t), 2, slope2 * dnear_ref[t], bufs[t % 2])

    def far_one(t, parity):
        k0, after = far_start(t)
        stage_one(k0, after, None, bufs[parity])

    def far_two(t, parity):
        stage_two(far_start(t)[0], bufs[parity])

    near_one(0)
    for t in range(1, per_q):
        near_one(t)
        stage_two(near_start(t - 1), bufs[(t - 1) % 2])
    far_one(0, per_q % 2)
    stage_two(near_start(per_q - 1), bufs[(per_q - 1) % 2])

    def far_body(u, c):
        for r in range(FAR_UNROLL):
            far_one(FAR_UNROLL * u + r + 1, (per_q + r + 1) % 2)
            far_two(FAR_UNROLL * u + r, (per_q + r) % 2)
        return c

    lax.fori_loop(0, n_far // FAR_UNROLL, far_body, 0)
    for t in range(n_far - n_far % FAR_UNROLL, n_far):
        if t + 1 < n_far:
            far_one(t + 1, (per_q + t + 1) % 2)
        far_two(t, (per_q + t) % 2)

    lp = lam_ref[...]
    lam = (jnp.exp(jnp.sum(lp[0:1] * lp[1:2], axis=-1, keepdims=True))
           - jnp.exp(jnp.sum(lp[2:3] * lp[3:4], axis=-1, keepdims=True)) + lam_init)
    a0 = acc_ref[0]
    a1 = acc_ref[1]
    o = (a0[0:DIFF_V_DIM] / a0[DIFF_V_DIM:DIFF_V_DIM + 1]
         - lam * (a1[0:DIFF_V_DIM] / a1[DIFF_V_DIM:DIFF_V_DIM + 1]))
    ms = jnp.mean(o * o, axis=0, keepdims=True)
    o = o * lax.rsqrt(ms + NORM_EPS) * sg_ref[...] * (1.0 - lam_init)
    o_ref[...] = o.T.astype(o_ref.dtype)


def _diff_attn(qa, ka, vt, lam_p, subln_g, *, lam_init, online):
    s = qa.shape[2]
    bq = ATT_BQ
    assert bq % ATT_BK == 0 and s > bq
    kern = functools.partial(_diff_attn_kernel, lam_init=lam_init, online=online)
    per_q = bq // ATT_BK
    key = np.arange(per_q * ATT_BK).reshape(per_q, ATT_BK, 1)
    dnear = jnp.asarray(-np.abs(np.arange(bq).reshape(1, 1, bq) - key), F32)
    scratch = [
        pltpu.VMEM((2, V_AUG, bq), F32),
        pltpu.VMEM((2, 3, bq, 128), BF16),
    ]
    if online:
        scratch += [
            pltpu.VMEM((2, 1, bq), F32),
            pltpu.VMEM((2, ATT_BK, bq), F32),
            pltpu.VMEM((2, ATT_BK, bq), F32),
            pltpu.VMEM((2, 1, bq), F32),
            pltpu.VMEM((2, 1, bq), F32),
        ]
    else:
        scratch += [pltpu.VMEM((2, ATT_BK, bq), BF16), pltpu.VMEM((2, ATT_BK, bq), BF16)]
    return pl.pallas_call(
        kern,
        grid=(DIFF_HEADS, s // bq),
        in_specs=[
            pl.BlockSpec((4, DIFF_QK_DIM), lambda h, i: (0, 0)),
            pl.BlockSpec((DIFF_V_DIM, 1), lambda h, i: (0, 0)),
            pl.BlockSpec((per_q, ATT_BK, bq), lambda h, i: (0, 0, 0)),
            pl.BlockSpec((None, 2, bq, 128), lambda h, i: (h, 0, i, 0)),
            pl.BlockSpec((None, 2, s, 128), lambda h, i: (h, 0, 0, 0)),
            pl.BlockSpec((None, V_AUG, s), lambda h, i: (h, 0, 0)),
        ],
        out_specs=pl.BlockSpec((bq, DIFF_V_DIM), lambda h, i: (i, h)),
        out_shape=jax.ShapeDtypeStruct((s, DIFF_DIM), BF16),
        scratch_shapes=scratch,
        compiler_params=_cparams(("parallel", "arbitrary")),
        name="diff_attn_online" if online else "diff_attn",
    )(lam_p, subln_g.reshape(DIFF_V_DIM, 1), dnear, qa, ka, vt)


def _shift_rows(x, prev_row, next_row):
    n = x.shape[0]
    row = lax.broadcasted_iota(jnp.int32, x.shape, 0)
    xp = jnp.where(row == 0, prev_row, pltpu.roll(x, 1, 0))
    xn = jnp.where(row == n - 1, next_row, pltpu.roll(x, n - 1, 0))
    return xp, xn


def _halo_rows(prev_ref, next_ref, i, n_blocks):
    hp = prev_ref.shape[0]
    prev_row = prev_ref[hp - 1:hp, :].astype(F32)
    next_row = next_ref[0:1, :].astype(F32)
    prev_row = jnp.where(i > 0, prev_row, 0.0)
    next_row = jnp.where(i < n_blocks - 1, next_row, 0.0)
    return prev_row, next_row


def _rwkv_prep_kernel(
        x_ref, xp_ref, xn_ref, l_ref, lp_ref, ln_ref,
        mu_ref, mul_ref, w0_ref, w2_ref, a0_ref, a2_ref, g2_ref, kk_ref, ka_ref, rk_ref,
        bd_ref, tri_ref,
        v_out, g_out, bonus_out, rb_out, kb_out, kt_out, bt_out, kh_out, bh_out, gc_out):
    i = pl.program_id(0)
    nb = pl.num_programs(0)
    tm = x_ref.shape[0]

    x = x_ref[...].astype(F32)
    prev_row, next_row = _halo_rows(xp_ref, xn_ref, i, nb)
    xp, xn = _shift_rows(x, prev_row, next_row)
    mu = mu_ref[...]
    x = x + mu[0:1] * (xp - x) + mu[1:2] * (xn - x)
    lo = l_ref[...].astype(F32)
    prev_row, next_row = _halo_rows(lp_ref, ln_ref, i, nb)
    lop, lon = _shift_rows(lo, prev_row, next_row)
    mul = mul_ref[...]
    lo = lo + mul[0:1] * (lop - lo) + mul[1:2] * (lon - lo)

    r = x[:, 0:RWKV_DIM]
    k = x[:, RWKV_DIM:2 * RWKV_DIM]
    v = x[:, 2 * RWKV_DIM:3 * RWKV_DIM]
    tw = jnp.tanh(lo[:, 0:128]).astype(BF16)
    la = lo[:, 128:256].astype(BF16)
    lg = jax.nn.sigmoid(lo[:, 256:512]).astype(BF16)

    g = _dot(lg, g2_ref[...])
    kk = k * kk_ref[...]
    ss = _group_mean_sq(kk, bd_ref, 1.0)
    kk = kk * lax.rsqrt(jnp.maximum(ss, 1e-24))
    bonus = _group_sum((r * k * rk_ref[...]).astype(BF16), bd_ref) * v

    g_out[...] = g.astype(g_out.dtype)
    bonus_out[...] = bonus.astype(bonus_out.dtype)
    v_out[...] = v.astype(BF16)

    tri = tri_ref[...]
    for d in range(2):
        wl = w0_ref[d:d + 1, :] + _dot(tw, w2_ref[d])
        logdec = -math.exp(-0.5) * jax.nn.sigmoid(wl)
        a = jax.nn.sigmoid(a0_ref[d:d + 1, :] + _dot(la, a2_ref[d]))
        k_d = k * (1.0 + (a - 1.0) * ka_ref[...])
        b_d = kk * a
        pre = _dot_exactish(tri, logdec)
        chunk_tot = jnp.broadcast_to(
            pre.reshape(tm // CHUNK, CHUNK, RWKV_DIM)[:, CHUNK - 1:CHUNK, :],
            (tm // CHUNK, CHUNK, RWKV_DIM)).reshape(tm, RWKV_DIM)
        suf = chunk_tot - pre
        if d == 0:
            lc, ex, rem = pre, pre - logdec, suf
        else:
            lc, ex, rem = suf + logdec, suf, pre - logdec
        e_neg = jnp.exp(-lc)
        e_rem = jnp.exp(rem)
        outs = (
            (rb_out, r * jnp.exp(lc)),
            (kb_out, kk * jnp.exp(ex)),
            (kt_out, k_d * e_neg),
            (bt_out, b_d * e_neg),
            (kh_out, k_d * e_rem),
            (bh_out, b_d * e_rem),
        )
        for ref, val in outs:
            ref[d] = val.astype(BF16)
        gc_out[d] = jnp.exp(chunk_tot).reshape(tm // 8, 8, RWKV_DIM)[:, 0, :]


def _rwkv_prep(p, mu_rkv, mu_lora, w0, w2p, a0, a2p, g2p, k_k, k_a, r_k, bd, tri, *, tm):
    s = p.shape[0]
    nb = s // tm
    hb = tm // 16
    last16 = s // 16 - 1
    rkv_w = 3 * RWKV_DIM
    assert COL_RKV % rkv_w == 0 and COL_LORA % LORA_PAD == 0
    c_rkv = COL_RKV // rkv_w
    c_lora = COL_LORA // LORA_PAD

    def prev_map(c):
        return lambda i: (jnp.maximum(i * hb - 1, 0), c)

    def next_map(c):
        return lambda i: (jnp.minimum((i + 1) * hb, last16), c)

    full = lambda *shape: pl.BlockSpec(shape, lambda i: (0,) * len(shape))
    hm = lambda: pl.BlockSpec((2, tm, RWKV_DIM), lambda i: (0, i, 0))
    hm_shape = jax.ShapeDtypeStruct((2, s, RWKV_DIM), BF16)
    return pl.pallas_call(
        _rwkv_prep_kernel,
        grid=(nb,),
        in_specs=[
            pl.BlockSpec((tm, rkv_w), lambda i: (i, c_rkv)),
            pl.BlockSpec((16, rkv_w), prev_map(c_rkv)),
            pl.BlockSpec((16, rkv_w), next_map(c_rkv)),
            pl.BlockSpec((tm, LORA_PAD), lambda i: (i, c_lora)),
            pl.BlockSpec((16, LORA_PAD), prev_map(c_lora)),
            pl.BlockSpec((16, LORA_PAD), next_map(c_lora)),
            full(2, rkv_w), full(2, LORA_PAD),
            full(2, RWKV_DIM), full(2, 128, RWKV_DIM),
            full(2, RWKV_DIM), full(2, 128, RWKV_DIM),
            full(256, RWKV_DIM),
            full(1, RWKV_DIM), full(1, RWKV_DIM), full(1, RWKV_DIM),
            full(GROUP_SLAB, GROUP_SLAB), full(tm, tm),
        ],
        out_specs=[
            pl.BlockSpec((tm, RWKV_DIM), lambda i: (i, 0)),
            pl.BlockSpec((tm, RWKV_DIM), lambda i: (i, 0)),
            pl.BlockSpec((tm, RWKV_DIM), lambda i: (i, 0)),
            hm(), hm(), hm(), hm(), hm(), hm(),
            pl.BlockSpec((2, tm // 8, RWKV_DIM), lambda i: (0, i, 0)),
        ],
        out_shape=[
            jax.ShapeDtypeStruct((s, RWKV_DIM), BF16),
            jax.ShapeDtypeStruct((s, RWKV_DIM), BF16),
            jax.ShapeDtypeStruct((s, RWKV_DIM), BF16),
            hm_shape, hm_shape, hm_shape, hm_shape, hm_shape, hm_shape,
            jax.ShapeDtypeStruct((2, s // 8, RWKV_DIM), F32),
        ],
        compiler_params=_cparams(("parallel",)),
        name="rwkv_prep",
    )(p, p, p, p, p, p, mu_rkv, mu_lora, w0, w2p, a0, a2p, g2p, k_k, k_a, r_k, bd, tri)


def _bdot(a, b):
    return lax.dot_general(a, b, (((2,), (1,)), ((0,), (0,))), preferred_element_type=F32)


def _bdot_nt(a, b):
    return lax.dot_general(a, b, (((2,), (2,)), ((0,), (0,))), preferred_element_type=F32)


def _bdot_tn(a, b):
    return lax.dot_general(a, b, (((1,), (1,)), ((0,), (0,))), preferred_element_type=F32)


WKV_CHUNKS_PER_STEP = 8


def _wkv_kernel(*refs):
    (vf_ref, vr_ref), ins, (gcf_ref, gcr_ref, yf_ref, yr_ref, s_ref) = refs[:2], refs[2:14], refs[14:]

    @pl.when(pl.program_id(0) == 0)
    def _():
        s_ref[...] = jnp.zeros_like(s_ref)

    def rows(ref, part, n):
        return ref.at[n * part:n * (part + 1), :]

    for part in range(WKV_CHUNKS_PER_STEP):
        back = WKV_CHUNKS_PER_STEP - 1 - part
        _wkv_chunk(False, rows(vf_ref, part, CHUNK), [rows(r, part, CHUNK) for r in ins[0:6]],
                   rows(gcf_ref, part, 8), rows(yf_ref, part, CHUNK), s_ref.at[0])
        _wkv_chunk(True, rows(vr_ref, back, CHUNK), [rows(r, back, CHUNK) for r in ins[6:12]],
                   rows(gcr_ref, back, 8), rows(yr_ref, back, CHUNK), s_ref.at[1])


WKV_GROUP = 2
GROUP_LANES = WKV_GROUP * RWKV_HEAD_DIM
N_GROUPS = RWKV_HEADS // WKV_GROUP


def _groups(x):
    return jnp.stack([x[:, GROUP_LANES * p:GROUP_LANES * (p + 1)] for p in range(N_GROUPS)])


def _head_of_lane(shape):
    return lax.broadcasted_iota(jnp.int32, shape, 2) // RWKV_HEAD_DIM


def _group_diag(y):
    head = _head_of_lane(y.shape)
    zero = jnp.zeros_like(y)
    return jnp.concatenate([jnp.where(head == h, y, zero) for h in range(WKV_GROUP)], axis=1)


def _group_tn(a, b):
    full = _bdot_tn(a, b)
    n = RWKV_HEAD_DIM
    head = _head_of_lane((N_GROUPS, n, GROUP_LANES))
    out = full[:, :n]
    for h in range(1, WKV_GROUP):
        out = jnp.where(head == h, full[:, n * h:n * (h + 1)], out)
    return out


def _wkv_chunk(reverse, v_ref, scaled_refs, gc_ref, y_ref, s_ref):
    c = CHUNK
    lanes = GROUP_LANES
    row = lax.broadcasted_iota(jnp.int32, (N_GROUPS, c, lanes), 1)
    col = lax.broadcasted_iota(jnp.int32, (N_GROUPS, c, lanes), 2) % RWKV_HEAD_DIM
    strict = (row < col) if reverse else (row > col)
    incl = (row <= col) if reverse else (row >= col)

    v = _groups(v_ref[...])
    rb, kb, kt, bt, kh, bh = [_groups(r[...]) for r in scaled_refs]

    kr = jnp.concatenate([kb, rb], axis=1)
    sc = _bdot_nt(kr, jnp.concatenate([_group_diag(kt), _group_diag(bt)], axis=1))
    sk = sc[:, :, :lanes]
    sb = sc[:, :, lanes:]
    a_kk = jnp.where(strict, sk[:, :c], 0.0)
    a_rk = jnp.where(incl, sk[:, c:], 0.0)
    l_mat = jnp.where(strict, sb[:, :c], 0.0)
    a_rb = jnp.where(incl, sb[:, c:], 0.0)

    eye = jnp.where(row == col, 1.0, 0.0)
    t_inv = eye - l_mat
    l_b = l_mat.astype(BF16)
    pw = _bdot(l_b, _group_diag(l_b))
    for step in range(5):
        pw_b = _group_diag(pw.astype(BF16))
        if step < 4:
            both = _bdot(jnp.concatenate([t_inv, pw], axis=1).astype(BF16), pw_b)
            t_inv = t_inv + both[:, :c]
            pw = both[:, c:]
        else:
            t_inv = t_inv + _bdot(t_inv.astype(BF16), pw_b)

    q12 = _bdot(jnp.concatenate([a_kk, a_rk], axis=1).astype(BF16), _group_diag(v))
    q1, q2 = q12[:, :c], q12[:, c:]
    t_b = t_inv.astype(BF16)
    kq = jnp.concatenate([_group_diag(kb), _group_diag(q1.astype(BF16))], axis=2)
    wkuv = _bdot(t_b, kq).astype(BF16)
    wk, uv = wkuv[:, :, :lanes], wkuv[:, :, lanes:]
    corr = _bdot(a_rb.astype(BF16), jnp.concatenate([_group_diag(wk), _group_diag(uv)], axis=2))
    rw = rb.astype(F32) - corr[:, :, :lanes]
    y0 = q2 - corr[:, :, lanes:]
    m_mat = _group_tn(wk, bh)
    j_t = _group_tn(jnp.concatenate([v, uv], axis=1), jnp.concatenate([kh, -bh], axis=1))

    n = RWKV_HEAD_DIM
    s_old = s_ref[...]
    s_hi = s_old.astype(BF16)
    s_lo = (s_old - s_hi.astype(F32)).astype(BF16)
    y = _bdot_nt(rw.astype(BF16), _group_diag(s_hi)) + y0
    sm = _bdot(jnp.concatenate([s_hi, s_lo], axis=1), _group_diag(m_mat.astype(BF16)))
    s_ref[...] = s_old * _groups(gc_ref[0:1, :]) - (sm[:, :n] + sm[:, n:]) + j_t
    for p in range(N_GROUPS):
        y_ref[:, lanes * p:lanes * (p + 1)] = y[p]


def _wkv(v, rb, kb, kt, bt, kh, bh, gc):
    s = v.shape[0]
    per = WKV_CHUNKS_PER_STEP
    assert s % (per * CHUNK) == 0
    nc = s // (per * CHUNK)
    scaled = (rb, kb, kt, bt, kh, bh)
    fwd = lambda rows: pl.BlockSpec((None, rows, RWKV_DIM), lambda c: (0, c, 0))
    rev = lambda rows: pl.BlockSpec((None, rows, RWKV_DIM), lambda c: (1, nc - 1 - c, 0))
    y_shape = jax.ShapeDtypeStruct((s, RWKV_DIM), F32)
    return pl.pallas_call(
        _wkv_kernel,
        grid=(nc,),
        in_specs=[
            pl.BlockSpec((per * CHUNK, RWKV_DIM), lambda c: (c, 0)),
            pl.BlockSpec((per * CHUNK, RWKV_DIM), lambda c: (nc - 1 - c, 0)),
            *[fwd(per * CHUNK) for _ in scaled], *[rev(per * CHUNK) for _ in scaled],
            fwd(per * 8), rev(per * 8),
        ],
        out_specs=[
            pl.BlockSpec((per * CHUNK, RWKV_DIM), lambda c: (c, 0)),
            pl.BlockSpec((per * CHUNK, RWKV_DIM), lambda c: (nc - 1 - c, 0)),
        ],
        out_shape=[y_shape, y_shape],
        scratch_shapes=[pltpu.VMEM((2, N_GROUPS, RWKV_HEAD_DIM, GROUP_LANES), F32)],
        compiler_params=_cparams(("arbitrary",)),
        name="wkv7_chunked",
    )(v, v, *scaled, *scaled, gc, gc)


def _rwkv_post_kernel(yf_ref, yr_ref, g_ref, bonus_ref, lg_ref, lb_ref, bd_ref, o_ref):
    y = yf_ref[...] + yr_ref[...]
    hi, lo = _split2(y)
    mean = (_group_sum(hi, bd_ref) + _group_sum(lo, bd_ref)) * (1.0 / RWKV_HEAD_DIM)
    yc = y - mean
    var = _group_mean_sq(yc, bd_ref, RWKV_HEAD_DIM)
    yn = yc * lax.rsqrt(var + LNX_EPS)
    out = (yn * lg_ref[...] + lb_ref[...] + bonus_ref[...].astype(F32)) * g_ref[...].astype(F32)
    o_ref[...] = out.astype(o_ref.dtype)


def _rwkv_post(y_fwd, y_rev, g, bonus, lnx_g, lnx_b, bd, *, tm):
    s = g.shape[0]
    return pl.pallas_call(
        _rwkv_post_kernel,
        grid=(s // tm,),
        in_specs=[
            pl.BlockSpec((tm, RWKV_DIM), lambda i: (i, 0)),
            pl.BlockSpec((tm, RWKV_DIM), lambda i: (i, 0)),
            pl.BlockSpec((tm, RWKV_DIM), lambda i: (i, 0)),
            pl.BlockSpec((tm, RWKV_DIM), lambda i: (i, 0)),
            pl.BlockSpec((1, RWKV_DIM), lambda i: (0, 0)),
            pl.BlockSpec((1, RWKV_DIM), lambda i: (0, 0)),
            pl.BlockSpec((GROUP_SLAB, GROUP_SLAB), lambda i: (0, 0)),
        ],
        out_specs=pl.BlockSpec((tm, RWKV_DIM), lambda i: (i, 0)),
        out_shape=jax.ShapeDtypeStruct((s, RWKV_DIM), BF16),
        compiler_params=_cparams(("parallel",)),
        name="rwkv_post",
    )(y_fwd, y_rev, g, bonus, lnx_g.reshape(1, RWKV_DIM), lnx_b.reshape(1, RWKV_DIM), bd)


def _mem_attn_kernel(q_ref, kv_ref, gq_ref, gk_ref, o_ref):
    for h in range(MEM_HEADS):
        sl = slice(MEM_HEAD_DIM * h, MEM_HEAD_DIM * (h + 1))
        q = q_ref[:, sl].astype(F32)
        q = q * lax.rsqrt(jnp.mean(q * q, axis=-1, keepdims=True) + NORM_EPS)
        q = q * gq_ref[...] * (MEM_HEAD_DIM ** -0.5)
        km = kv_ref[:, sl].astype(F32)
        km = km * lax.rsqrt(jnp.mean(km * km, axis=-1, keepdims=True) + NORM_EPS)
        km = km * gk_ref[...]
        vm = kv_ref[:, MEM_DIM + MEM_HEAD_DIM * h:MEM_DIM + MEM_HEAD_DIM * (h + 1)]
        s = lax.dot_general(q.astype(BF16), km.astype(BF16), (((1,), (1,)), ((), ())),
                            preferred_element_type=F32)
        s = s - jnp.max(s, axis=-1, keepdims=True)
        e = jnp.exp(s)
        pr = e / jnp.sum(e, axis=-1, keepdims=True)
        o_ref[:, sl] = _dot(pr.astype(BF16), vm).astype(o_ref.dtype)


def _mem_attn(p, kv, gq, gk, *, tm):
    s = p.shape[0]
    assert COL_MEM % MEM_DIM == 0
    return pl.pallas_call(
        _mem_attn_kernel,
        grid=(s // tm,),
        in_specs=[
            pl.BlockSpec((tm, MEM_DIM), lambda i: (i, COL_MEM // MEM_DIM)),
            pl.BlockSpec((N_MEM, 2 * MEM_DIM), lambda i: (0, 0)),
            pl.BlockSpec((1, MEM_HEAD_DIM), lambda i: (0, 0)),
            pl.BlockSpec((1, MEM_HEAD_DIM), lambda i: (0, 0)),
        ],
        out_specs=pl.BlockSpec((tm, MEM_DIM), lambda i: (i, 0)),
        out_shape=jax.ShapeDtypeStruct((s, MEM_DIM), BF16),
        compiler_params=_cparams(("parallel",)),
        name="mem_attn",
    )(p, kv, gq, gk)


def _merge_kernel(o0_ref, o1_ref, o2_ref, g0_ref, g1_ref, g2_ref, w_ref, m_ref):
    def gate(g_ref):
        return 0.5 * jnp.tanh(0.5 * g_ref[...].astype(F32)) + 0.5

    acc = gate(g0_ref) * _dot(o0_ref[...], w_ref[0])
    acc = acc + gate(g1_ref) * _dot(o1_ref[...], w_ref[1])
    acc = acc + gate(g2_ref) * _dot(o2_ref[...], w_ref[2])
    m_ref[...] = acc.astype(m_ref.dtype)


def _merge(o_diff, o_rwkv, o_mem, p, w_branch, layer, *, tm, tn):
    s = p.shape[0]
    assert COL_GATE % tn == 0 and D_MODEL % tn == 0
    gate_blk = COL_GATE // tn
    per = D_MODEL // tn
    o_spec = lambda: pl.BlockSpec((tm, 1024), lambda i, j: (i, 0))
    g_spec = lambda b: pl.BlockSpec((tm, tn), lambda i, j: (i, gate_blk + b * per + j))
    return pl.pallas_call(
        _merge_kernel,
        grid=(s // tm, per),
        in_specs=[
            o_spec(), o_spec(), o_spec(),
            g_spec(0), g_spec(1), g_spec(2),
            pl.BlockSpec((None, N_BRANCH, 1024, tn), lambda i, j: (layer, 0, 0, j)),
        ],
        out_specs=pl.BlockSpec((tm, tn), lambda i, j: (i, j)),
        out_shape=jax.ShapeDtypeStruct((s, D_MODEL), BF16),
        compiler_params=_cparams(("parallel", "arbitrary")),
        name="merge",
    )(o_diff, o_rwkv, o_mem, p, p, p, w_branch)


GLU_TILE = 256


def _ffn_up_glu_kernel(x_ref, xp_ref, xn_ref, g_ref, wg_ref, wv_ref, cw_ref, cb_ref, o_ref,
                       h_ref, halo_ref):
    i = pl.program_id(0)
    nb = pl.num_programs(0)

    def norm(x):
        ms = jnp.mean(x * x, axis=-1, keepdims=True)
        return x * lax.rsqrt(ms + NORM_EPS) * g_ref[...]

    @pl.when(pl.program_id(1) == 0)
    def _():
        h_ref[...] = norm(x_ref[...]).astype(BF16)
        row = lax.broadcasted_iota(jnp.int32, xp_ref.shape, 0)
        before = jnp.where(i > 0, pltpu.roll(norm(xp_ref[...]), 1, 0), 0.0)
        after = jnp.where(i < nb - 1, pltpu.roll(norm(xn_ref[...]), 1, 0), 0.0)
        halo_ref[...] = jnp.where(row == 0, before, jnp.where(row == 1, after, 0.0)).astype(BF16)

    h = h_ref[...]
    halo = halo_ref[...]
    tn = o_ref.shape[1]
    for c0 in range(0, tn, GLU_TILE):
        sl = slice(c0, c0 + GLU_TILE)
        wg = wg_ref[:, sl]
        ug = _dot(h, wg)
        uv = _dot(h, wv_ref[:, sl])
        edge = _dot(halo, wg)
        gp, gn = _shift_rows(ug, edge[0:1], edge[1:2])
        cw = cw_ref[:, sl]
        conv = cw[0:1] * gp + cw[1:2] * ug + cw[2:3] * gn + cb_ref[:, sl]
        half = 0.5 * conv
        o_ref[:, sl] = ((half + half * jnp.tanh(half)) * uv).astype(o_ref.dtype)


def _ffn_up_glu(x, g, w_up, layer, conv_w, conv_b, *, tm, tn):
    s, k = x.shape
    assert tn % GLU_TILE == 0
    nj = D_FF // tn
    hb = tm // 16
    last16 = s // 16 - 1
    return pl.pallas_call(
        _ffn_up_glu_kernel,
        grid=(s // tm, nj),
        in_specs=[
            pl.BlockSpec((tm, k), lambda i, j: (i, 0)),
            pl.BlockSpec((16, k), lambda i, j: (jnp.maximum(i * hb - 1, 0), 0)),
            pl.BlockSpec((16, k), lambda i, j: (jnp.minimum((i + 1) * hb, last16), 0)),
            pl.BlockSpec((1, k), lambda i, j: (0, 0)),
            pl.BlockSpec((None, k, tn), lambda i, j: (layer, 0, j)),
            pl.BlockSpec((None, k, tn), lambda i, j: (layer, 0, nj + j)),
            pl.BlockSpec((3, tn), lambda i, j: (0, j)),
            pl.BlockSpec((1, tn), lambda i, j: (0, j)),
        ],
        out_specs=pl.BlockSpec((tm, tn), lambda i, j: (i, j)),
        out_shape=jax.ShapeDtypeStruct((s, D_FF), BF16),
        scratch_shapes=[pltpu.VMEM((tm, k), BF16), pltpu.VMEM((16, k), BF16)],
        compiler_params=_cparams(("parallel", "arbitrary")),
        name="ffn_up_glu",
    )(x, x, x, g.reshape(1, k), w_up, w_up, conv_w, conv_b.reshape(1, D_FF))


def _block_ones(n, group):
    idx = np.arange(n) // group
    return jnp.asarray(idx[:, None] == idx[None, :], BF16)


def _chunk_tri(n):
    idx = np.arange(n)
    same = (idx[:, None] // CHUNK) == (idx[None, :] // CHUNK)
    return jnp.asarray(same & (idx[:, None] >= idx[None, :]), BF16)


def _pad_rows(w, rows_before, total):
    n = w.shape[-1]
    out = jnp.zeros((total, n), w.dtype)
    return lax.dynamic_update_slice(out, w, (rows_before, 0))


def _pad_w_in(w):
    lora0 = COL_RKV + 3 * RWKV_DIM
    mem0 = lora0 + LORA_COLS
    gate0 = mem0 + MEM_DIM
    pad = jnp.zeros((w.shape[0], LORA_PAD - LORA_COLS), w.dtype)
    parts = [w[:, :lora0], w[:, gate0:], w[:, mem0:gate0], w[:, lora0:mem0], pad]
    return jnp.concatenate(parts, axis=1).astype(BF16)


def _tile_plan(s):
    rows = 1024 if s % 1024 == 0 else 512
    return dict(
        w_in=dict(tm=rows, tn=2304),
        mem_kv=dict(tm=N_MEM, tn=1024),
        diff_prep=256, rwkv_prep=256, rwkv_post=256, mem_attn=512,
        merge=dict(tm=512, tn=D_MODEL),
        w_out=dict(tm=512, tn=D_MODEL),
        ffn_up=dict(tm=rows, tn=512),
        ffn_down=dict(tm=rows, tn=512),
    )


def kernel(x, mem, attn_norm_g, w_in, diff_qk_g, diff_lambda, diff_subln_g, rwkv_mu, rwkv_w0,
           rwkv_w2, rwkv_a0, rwkv_a2, rwkv_g2, rwkv_k_k, rwkv_k_a, rwkv_r_k, rwkv_lnx_g,
           rwkv_lnx_b, mem_norm_g, w_mem_kv, mem_qk_g, w_branch, w_out, ffn_norm_g, w_ffn_up,
           ffn_conv_w, ffn_conv_b, w_ffn_down):
    b, s, d = x.shape
    assert b == 1 and d == D_MODEL and s % ATT_BQ == 0
    xs = x.reshape(s, d)
    mem2 = mem.reshape(N_MEM, d)
    tiles = _tile_plan(s)
    bd64 = _block_ones(GROUP_SLAB, 64)
    tri = _chunk_tri(tiles["rwkv_prep"])
    n_rwkv_main = 3 * RWKV_DIM
    w_mem_kv_b = w_mem_kv.astype(BF16)
    w_branch_b = w_branch.astype(BF16)
    w_out_b = w_out.astype(BF16)
    w_ffn_up_b = w_ffn_up.astype(BF16)
    w_ffn_down_b = w_ffn_down.astype(BF16)

    for l in range(DEPTH):
        lam_init = 0.8 - 0.6 * math.exp(-0.3 * l)
        p = _rms_mm(xs, attn_norm_g[l], _pad_w_in(w_in[l])[None], 0, name="rms_w_in",
                    **tiles["w_in"])

        gq = jnp.tile(diff_qk_g[l, 0].reshape(1, 128), (1, DIFF_HEADS))
        gk = jnp.tile(diff_qk_g[l, 1].reshape(1, 128), (1, DIFF_HEADS))
        qc, kc, score_bound = _attn_consts(diff_qk_g[l])
        qa, ka, vt = _diff_prep(p, gq, gk, qc, kc, bd64, tm=tiles["diff_prep"])
        o_diff = lax.cond(
            score_bound <= SCORE_BOUND_MAX,
            functools.partial(_diff_attn, lam_init=lam_init, online=False),
            functools.partial(_diff_attn, lam_init=lam_init, online=True),
            qa, ka, vt, diff_lambda[l], diff_subln_g[l])

        mu = rwkv_mu[l]
        mu_rkv = mu[:, :n_rwkv_main]
        mu_lora = jnp.pad(mu[:, n_rwkv_main:], ((0, 0), (0, LORA_PAD - LORA_COLS)))
        w2p = jnp.stack([_pad_rows(rwkv_w2[l, dd], 64 * dd, 128) for dd in range(2)]).astype(BF16)
        a2p = jnp.stack([_pad_rows(rwkv_a2[l, dd], 64 * dd, 128) for dd in range(2)]).astype(BF16)
        g2p = _pad_rows(rwkv_g2[l], 0, 256).astype(BF16)
        (v_h, g_tok, bonus, rb, kb, kt, bt, kh, bh, gc) = _rwkv_prep(
            p, mu_rkv, mu_lora, rwkv_w0[l], w2p, rwkv_a0[l], a2p, g2p,
            rwkv_k_k[l].reshape(1, RWKV_DIM), rwkv_k_a[l].reshape(1, RWKV_DIM),
            rwkv_r_k[l].reshape(1, RWKV_DIM), bd64, tri, tm=tiles["rwkv_prep"])
        y_fwd, y_rev = _wkv(v_h, rb, kb, kt, bt, kh, bh, gc)
        o_rwkv = _rwkv_post(y_fwd, y_rev, g_tok, bonus, rwkv_lnx_g[l], rwkv_lnx_b[l], bd64,
                            tm=tiles["rwkv_post"])

        kv = _rms_mm(mem2, mem_norm_g[l], w_mem_kv_b, l, name="rms_mem_kv", **tiles["mem_kv"])
        o_mem = _mem_attn(p, kv, mem_qk_g[l, 0].reshape(1, MEM_HEAD_DIM),
                          mem_qk_g[l, 1].reshape(1, MEM_HEAD_DIM), tm=tiles["mem_attn"])

        merged = _merge(o_diff, o_rwkv, o_mem, p, w_branch_b, l, **tiles["merge"])
        xs = _mm_res(merged, w_out_b, l, xs, name="w_out_res", **tiles["w_out"])

        act = _ffn_up_glu(xs, ffn_norm_g[l], w_ffn_up_b, l, ffn_conv_w[l], ffn_conv_b[l],
                          **tiles["ffn_up"])
        xs = _mm_res(act, w_ffn_down_b, l, xs, name="ffn_down_res", **tiles["ffn_down"])

    return xs.reshape(b, s, d)
```

```python
import functools
import math

import jax
import jax.numpy as jnp
import numpy as np
from jax import lax
from jax.experimental import pallas as pl
from jax.experimental.pallas import tpu as pltpu

F32 = jnp.float32
BF16 = jnp.bfloat16

D_MODEL = 2048
DEPTH = 2
DIFF_HEADS = 8
DIFF_QK_DIM = 64
DIFF_V_DIM = 128
DIFF_DIM = 1024
RWKV_HEADS = 16
RWKV_HEAD_DIM = 64
RWKV_DIM = 1024
DECAY_LORA = 64
AAA_LORA = 64
GATE_LORA = 160
LORA_COLS = 2 * DECAY_LORA + 2 * AAA_LORA + GATE_LORA
LORA_PAD = 512
N_MEM = 256
MEM_HEADS = 4
MEM_HEAD_DIM = 256
MEM_DIM = 1024
N_BRANCH = 3
D_FF = 5632
NORM_EPS = 1e-6
LNX_EPS = 64e-5

COL_DIFF = 0
COL_RKV = 3 * DIFF_DIM
COL_GATE = COL_RKV + 3 * RWKV_DIM
COL_MEM = COL_GATE + N_BRANCH * D_MODEL
COL_LORA = COL_MEM + MEM_DIM
N_IN_PAD = COL_LORA + LORA_PAD

CHUNK = 64
LOG2E = 1.4426950408889634
N_POS_COLS = 12
SCORE_BOUND_MAX = 30.0
ATT_BQ = 1024
ATT_BK = 512
V_AUG = 144
FAR_UNROLL = 10
VMEM_LIMIT = 56 * 1024 * 1024


def _cparams(sem):
    return pltpu.CompilerParams(dimension_semantics=sem, vmem_limit_bytes=VMEM_LIMIT)


def _split2(x):
    hi = x.astype(BF16)
    return hi, (x - hi.astype(F32)).astype(BF16)


def _dot(a, b):
    return jnp.dot(a, b, preferred_element_type=F32)


def _dot_exactish(a_bf16_exact, x_f32):
    hi, lo = _split2(x_f32)
    return _dot(a_bf16_exact, hi) + _dot(a_bf16_exact, lo)


def _rms_mm_kernel(x_ref, g_ref, w_ref, o_ref, h_ref):
    @pl.when(pl.program_id(1) == 0)
    def _():
        x = x_ref[...]
        ms = jnp.mean(x * x, axis=-1, keepdims=True)
        h_ref[...] = (x * lax.rsqrt(ms + NORM_EPS) * g_ref[...]).astype(BF16)

    o_ref[...] = _dot(h_ref[...], w_ref[...]).astype(o_ref.dtype)


def _rms_mm(x, g, w, layer, *, tm, tn, name):
    m, k = x.shape
    n = w.shape[2]
    return pl.pallas_call(
        _rms_mm_kernel,
        grid=(m // tm, n // tn),
        in_specs=[
            pl.BlockSpec((tm, k), lambda i, j: (i, 0)),
            pl.BlockSpec((1, k), lambda i, j: (0, 0)),
            pl.BlockSpec((None, k, tn), lambda i, j: (layer, 0, j)),
        ],
        out_specs=pl.BlockSpec((tm, tn), lambda i, j: (i, j)),
        out_shape=jax.ShapeDtypeStruct((m, n), BF16),
        scratch_shapes=[pltpu.VMEM((tm, k), BF16)],
        compiler_params=_cparams(("parallel", "arbitrary")),
        name=name,
    )(x, g.reshape(1, k), w)


def _mm_res_kernel(a_ref, w_ref, r_ref, o_ref):
    o_ref[...] = r_ref[...] + _dot(a_ref[...], w_ref[...])


def _mm_res(a, w, layer, res, *, tm, tn, name):
    m, k = a.shape
    n = w.shape[2]
    return pl.pallas_call(
        _mm_res_kernel,
        grid=(m // tm, n // tn),
        in_specs=[
            pl.BlockSpec((tm, k), lambda i, j: (i, 0)),
            pl.BlockSpec((None, k, tn), lambda i, j: (layer, 0, j)),
            pl.BlockSpec((tm, tn), lambda i, j: (i, j)),
        ],
        out_specs=pl.BlockSpec((tm, tn), lambda i, j: (i, j)),
        out_shape=jax.ShapeDtypeStruct((m, n), F32),
        compiler_params=_cparams(("parallel", "arbitrary")),
        name=name,
    )(a, w, res)


GROUP_SLAB = 256


def _group_sum(xb, bd_ref):
    bd = bd_ref[...]
    n = xb.shape[1]
    slabs = [_dot(xb[:, q:q + GROUP_SLAB], bd) for q in range(0, n, GROUP_SLAB)]
    return jnp.concatenate(slabs, axis=1)


def _group_mean_sq(x, bd_ref, group):
    return _group_sum((x * x).astype(BF16), bd_ref) * (1.0 / group)


def _aug_base(mp):
    return 64 if mp == 0 else 0


def _diff_prep_kernel(q_ref, k_ref, v_ref, gq_ref, gk_ref, qc_ref, kc_ref, bd_ref,
                      qa_ref, ka_ref, vt_ref):
    tm = q_ref.shape[0]
    row0 = pl.program_id(0) * tm
    lane = lax.broadcasted_iota(jnp.int32, (tm, 128), 1)
    pos = row0 + lax.broadcasted_iota(jnp.int32, (tm, 128), 0)
    pos_lo = (pos & 127).astype(F32)
    pos_hi = (pos >> 7).astype(F32)

    q = q_ref[...].astype(F32)
    qn = q * lax.rsqrt(_group_mean_sq(q, bd_ref, DIFF_QK_DIM) + NORM_EPS)
    qn = qn * gq_ref[...] * (DIFF_QK_DIM ** -0.5 * LOG2E)
    k = k_ref[...].astype(F32)
    kn = k * lax.rsqrt(_group_mean_sq(k, bd_ref, DIFF_QK_DIM) + NORM_EPS)
    kn = kn * gk_ref[...]

    for h in range(DIFF_HEADS):
        qh = qn[:, 128 * h:128 * (h + 1)]
        kh = kn[:, 128 * h:128 * (h + 1)]
        for mp in range(2):
            a0 = _aug_base(mp)
            g = 2 * h + mp
            is_data = (lane < 64) if mp == 0 else (lane >= 64)
            aug_q = jnp.where(lane < a0 + 3, pos_lo,
                              jnp.where(lane < a0 + 6, pos_hi, qc_ref[g:g + 1, :]))
            aug_k = jnp.where((lane >= a0 + 6) & (lane < a0 + 9), pos_lo,
                              jnp.where((lane >= a0 + 9) & (lane < a0 + 12), pos_hi,
                                        kc_ref[g:g + 1, :]))
            qa_ref[h, mp] = jnp.where(is_data, qh, aug_q).astype(BF16)
            ka_ref[h, mp] = jnp.where(is_data, kh, aug_k).astype(BF16)

    vt = v_ref[...].astype(F32).T
    sub = lax.broadcasted_iota(jnp.int32, (V_AUG - DIFF_V_DIM, tm), 0)
    ones_rows = jnp.where(sub == 0, 1.0, 0.0).astype(BF16)
    for h in range(DIFF_HEADS):
        vt_ref[h, 0:DIFF_V_DIM, :] = vt[128 * h:128 * (h + 1), :].astype(BF16)
        vt_ref[h, DIFF_V_DIM:V_AUG, :] = ones_rows


def _bf16_split3_const(x):
    parts = []
    for _ in range(3):
        part = float(np.asarray(x, np.float32).astype(BF16).astype(np.float32))
        parts.append(part)
        x = x - part
    return parts


def _attn_consts(qk_g):
    l_parts = _bf16_split3_const(LOG2E)
    m_nat = 8.0 * jnp.max(jnp.abs(qk_g[0]), axis=-1) * jnp.max(jnp.abs(qk_g[1]), axis=-1)
    qc = np.zeros((2 * DIFF_HEADS, 128), np.float32)
    kc = np.zeros((2 * DIFF_HEADS, 128), np.float32)
    bound_lane = np.zeros((2, 2 * DIFF_HEADS, 128), np.float32)
    for h in range(DIFF_HEADS):
        slope = 2.0 ** (-(h + 1))
        for mp in range(2):
            a0 = _aug_base(mp)
            g = 2 * h + mp
            for t, lp in enumerate(l_parts):
                qc[g, a0 + 6 + t] = slope * lp
                qc[g, a0 + 9 + t] = 128.0 * slope * lp
                kc[g, a0 + t] = -slope * lp
                kc[g, a0 + 3 + t] = -128.0 * slope * lp
            bound_lane[mp, g, a0 + N_POS_COLS] = 1.0
            kc[g, a0 + N_POS_COLS] = 1.0
    m2 = -m_nat * LOG2E
    qc = qc + m2[0] * bound_lane[0] + m2[1] * bound_lane[1]
    return qc, jnp.asarray(kc), jnp.max(m_nat)


def _diff_prep(p, gq, gk, qc, kc, bd, *, tm):
    s = p.shape[0]
    return pl.pallas_call(
        _diff_prep_kernel,
        grid=(s // tm,),
        in_specs=[
            pl.BlockSpec((tm, DIFF_DIM), lambda i: (i, 0)),
            pl.BlockSpec((tm, DIFF_DIM), lambda i: (i, 1)),
            pl.BlockSpec((tm, DIFF_DIM), lambda i: (i, 2)),
            pl.BlockSpec((1, DIFF_DIM), lambda i: (0, 0)),
            pl.BlockSpec((1, DIFF_DIM), lambda i: (0, 0)),
            pl.BlockSpec((2 * DIFF_HEADS, 128), lambda i: (0, 0)),
            pl.BlockSpec((2 * DIFF_HEADS, 128), lambda i: (0, 0)),
            pl.BlockSpec((GROUP_SLAB, GROUP_SLAB), lambda i: (0, 0)),
        ],
        out_specs=[
            pl.BlockSpec((DIFF_HEADS, 2, tm, 128), lambda i: (0, 0, i, 0)),
            pl.BlockSpec((DIFF_HEADS, 2, tm, 128), lambda i: (0, 0, i, 0)),
            pl.BlockSpec((DIFF_HEADS, V_AUG, tm), lambda i: (0, 0, i)),
        ],
        out_shape=[
            jax.ShapeDtypeStruct((DIFF_HEADS, 2, s, 128), BF16),
            jax.ShapeDtypeStruct((DIFF_HEADS, 2, s, 128), BF16),
            jax.ShapeDtypeStruct((DIFF_HEADS, V_AUG, s), BF16),
        ],
        compiler_params=_cparams(("parallel",)),
        name="diff_prep",
    )(p, p, p, gq, gk, qc, kc, bd)


def _diff_attn_kernel(lam_ref, sg_ref, dnear_ref, qa_ref, ka_ref, vt_ref, o_ref,
                      acc_ref, qv_ref, *mode_refs, lam_init, online):
    h = pl.program_id(0)
    i = pl.program_id(1)
    bq = qa_ref.shape[1]
    s_len = ka_ref.shape[1]
    bk = ATT_BK
    n_chunks = s_len // bk
    per_q = bq // bk
    n_far = n_chunks - per_q
    j_lo = i * per_q
    q0 = i * bq
    slope2 = jnp.exp2(-jnp.full((1, 1), h + 1, jnp.int32).astype(F32)) * LOG2E
    if online:
        m_ref, sa_ref, sb_ref, mxa_ref, mxb_ref = mode_refs
        m_ref[...] = jnp.full_like(m_ref, -1e30)
        bufs = ((sa_ref, mxa_ref), (sb_ref, mxb_ref))
    else:
        bufs = mode_refs
    acc_ref[...] = jnp.zeros_like(acc_ref)

    lane = lax.broadcasted_iota(jnp.int32, (1, 128), 1)
    for mp in range(2):
        q = qa_ref[mp]
        pos_lane = (lane >= _aug_base(mp)) & (lane < _aug_base(mp) + N_POS_COLS)
        qv_ref[mp, 0] = q
        qv_ref[mp, 1] = jnp.where(pos_lane, -q, q)
        qv_ref[mp, 2] = jnp.where(pos_lane, jnp.zeros_like(q), q)

    def stage_one(k0, variant, bias, buf):
        for mp in range(2):
            kc = ka_ref[mp, pl.ds(k0, bk), :]
            s = lax.dot_general(kc, qv_ref[mp, variant], (((1,), (1,)), ((), ())),
                                preferred_element_type=F32)
            if bias is not None:
                s = s + bias
            if online:
                buf[0][mp] = s
                buf[1][mp] = jnp.max(s, axis=0, keepdims=True)
            else:
                buf[mp] = jnp.exp2(s).astype(BF16)

    def stage_two(k0, buf):
        vt_c = vt_ref[:, pl.ds(k0, bk)]
        for mp in range(2):
            if online:
                m_old = m_ref[mp]
                m_new = jnp.maximum(m_old, buf[1][mp])
                p = jnp.exp2(buf[0][mp] - m_new).astype(BF16)
                acc_ref[mp] = acc_ref[mp] * jnp.exp2(m_old - m_new) + _dot(vt_c, p)
                m_ref[mp] = m_new
            else:
                acc_ref[mp] += _dot(vt_c, buf[mp])

    def near_start(t):
        return pl.multiple_of(q0 + t * bk, bk)

    def far_start(t):
        t = jnp.minimum(t, n_far - 1)
        j = jnp.where(t < j_lo, t, t + per_q)
        return pl.multiple_of(j * bk, bk), (t >= j_lo).astype(jnp.int32)

    def near_one(t):
        stage_one(near_start(t), 2, slope2 * dnear_ref[t], bufs[t % 2])

    def far_one(t, parity):
        k0, after = far_start(t)
        stage_one(k0, after, None, bufs[parity])

    def far_two(t, parity):
        stage_two(far_start(t)[0], bufs[parity])

    near_one(0)
    for t in range(1, per_q):
        near_one(t)
        stage_two(near_start(t - 1), bufs[(t - 1) % 2])
    far_one(0, per_q % 2)
    stage_two(near_start(per_q - 1), bufs[(per_q - 1) % 2])

    def far_body(u, c):
        for r in range(FAR_UNROLL):
            far_one(FAR_UNROLL * u + r + 1, (per_q + r + 1) % 2)
            far_two(FAR_UNROLL * u + r, (per_q + r) % 2)
        return c

    lax.fori_loop(0, n_far // FAR_UNROLL, far_body, 0)
    for t in range(n_far - n_far % FAR_UNROLL, n_far):
        if t + 1 < n_far:
            far_one(t + 1, (per_q + t + 1) % 2)
        far_two(t, (per_q + t) % 2)

    lp = lam_ref[...]
    lam = (jnp.exp(jnp.sum(lp[0:1] * lp[1:2], axis=-1, keepdims=True))
           - jnp.exp(jnp.sum(lp[2:3] * lp[3:4], axis=-1, keepdims=True)) + lam_init)
    a0 = acc_ref[0]
    a1 = acc_ref[1]
    o = (a0[0:DIFF_V_DIM] / a0[DIFF_V_DIM:DIFF_V_DIM + 1]
         - lam * (a1[0:DIFF_V_DIM] / a1[DIFF_V_DIM:DIFF_V_DIM + 1]))
    ms = jnp.mean(o * o, axis=0, keepdims=True)
    o = o * lax.rsqrt(ms + NORM_EPS) * sg_ref[...] * (1.0 - lam_init)
    o_ref[...] = o.T.astype(o_ref.dtype)


def _diff_attn(qa, ka, vt, lam_p, subln_g, *, lam_init, online):
    s = qa.shape[2]
    bq = ATT_BQ
    assert bq % ATT_BK == 0 and s > bq
    kern = functools.partial(_diff_attn_kernel, lam_init=lam_init, online=online)
    per_q = bq // ATT_BK
    key = np.arange(per_q * ATT_BK).reshape(per_q, ATT_BK, 1)
    dnear = jnp.asarray(-np.abs(np.arange(bq).reshape(1, 1, bq) - key), F32)
    scratch = [
        pltpu.VMEM((2, V_AUG, bq), F32),
        pltpu.VMEM((2, 3, bq, 128), BF16),
    ]
    if online:
        scratch += [
            pltpu.VMEM((2, 1, bq), F32),
            pltpu.VMEM((2, ATT_BK, bq), F32),
            pltpu.VMEM((2, ATT_BK, bq), F32),
            pltpu.VMEM((2, 1, bq), F32),
            pltpu.VMEM((2, 1, bq), F32),
        ]
    else:
        scratch += [pltpu.VMEM((2, ATT_BK, bq), BF16), pltpu.VMEM((2, ATT_BK, bq), BF16)]
    return pl.pallas_call(
        kern,
        grid=(DIFF_HEADS, s // bq),
        in_specs=[
            pl.BlockSpec((4, DIFF_QK_DIM), lambda h, i: (0, 0)),
            pl.BlockSpec((DIFF_V_DIM, 1), lambda h, i: (0, 0)),
            pl.BlockSpec((per_q, ATT_BK, bq), lambda h, i: (0, 0, 0)),
            pl.BlockSpec((None, 2, bq, 128), lambda h, i: (h, 0, i, 0)),
            pl.BlockSpec((None, 2, s, 128), lambda h, i: (h, 0, 0, 0)),
            pl.BlockSpec((None, V_AUG, s), lambda h, i: (h, 0, 0)),
        ],
        out_specs=pl.BlockSpec((bq, DIFF_V_DIM), lambda h, i: (i, h)),
        out_shape=jax.ShapeDtypeStruct((s, DIFF_DIM), BF16),
        scratch_shapes=scratch,
        compiler_params=_cparams(("parallel", "arbitrary")),
        name="diff_attn_online" if online else "diff_attn",
    )(lam_p, subln_g.reshape(DIFF_V_DIM, 1), dnear, qa, ka, vt)


def _shift_rows(x, prev_row, next_row):
    n = x.shape[0]
    row = lax.broadcasted_iota(jnp.int32, x.shape, 0)
    xp = jnp.where(row == 0, prev_row, pltpu.roll(x, 1, 0))
    xn = jnp.where(row == n - 1, next_row, pltpu.roll(x, n - 1, 0))
    return xp, xn


def _halo_rows(prev_ref, next_ref, i, n_blocks):
    hp = prev_ref.shape[0]
    prev_row = prev_ref[hp - 1:hp, :].astype(F32)
    next_row = next_ref[0:1, :].astype(F32)
    prev_row = jnp.where(i > 0, prev_row, 0.0)
    next_row = jnp.where(i < n_blocks - 1, next_row, 0.0)
    return prev_row, next_row


def _rwkv_prep_kernel(
        x_ref, xp_ref, xn_ref, l_ref, lp_ref, ln_ref,
        mu_ref, mul_ref, w0_ref, w2_ref, a0_ref, a2_ref, g2_ref, kk_ref, ka_ref, rk_ref,
        bd_ref, tri_ref,
        v_out, g_out, bonus_out, rb_out, kb_out, kt_out, bt_out, kh_out, bh_out, gc_out):
    i = pl.program_id(0)
    nb = pl.num_programs(0)
    tm = x_ref.shape[0]

    x = x_ref[...].astype(F32)
    prev_row, next_row = _halo_rows(xp_ref, xn_ref, i, nb)
    xp, xn = _shift_rows(x, prev_row, next_row)
    mu = mu_ref[...]
    x = x + mu[0:1] * (xp - x) + mu[1:2] * (xn - x)
    lo = l_ref[...].astype(F32)
    prev_row, next_row = _halo_rows(lp_ref, ln_ref, i, nb)
    lop, lon = _shift_rows(lo, prev_row, next_row)
    mul = mul_ref[...]
    lo = lo + mul[0:1] * (lop - lo) + mul[1:2] * (lon - lo)

    r = x[:, 0:RWKV_DIM]
    k = x[:, RWKV_DIM:2 * RWKV_DIM]
    v = x[:, 2 * RWKV_DIM:3 * RWKV_DIM]
    tw = jnp.tanh(lo[:, 0:128]).astype(BF16)
    la = lo[:, 128:256].astype(BF16)
    lg = jax.nn.sigmoid(lo[:, 256:512]).astype(BF16)

    g = _dot(lg, g2_ref[...])
    kk = k * kk_ref[...]
    ss = _group_mean_sq(kk, bd_ref, 1.0)
    kk = kk * lax.rsqrt(jnp.maximum(ss, 1e-24))
    bonus = _group_sum((r * k * rk_ref[...]).astype(BF16), bd_ref) * v

    g_out[...] = g.astype(g_out.dtype)
    bonus_out[...] = bonus.astype(bonus_out.dtype)
    v_out[...] = v.astype(BF16)

    tri = tri_ref[...]
    for d in range(2):
        wl = w0_ref[d:d + 1, :] + _dot(tw, w2_ref[d])
        logdec = -math.exp(-0.5) * jax.nn.sigmoid(wl)
        a = jax.nn.sigmoid(a0_ref[d:d + 1, :] + _dot(la, a2_ref[d]))
        k_d = k * (1.0 + (a - 1.0) * ka_ref[...])
        b_d = kk * a
        pre = _dot_exactish(tri, logdec)
        chunk_tot = jnp.broadcast_to(
            pre.reshape(tm // CHUNK, CHUNK, RWKV_DIM)[:, CHUNK - 1:CHUNK, :],
            (tm // CHUNK, CHUNK, RWKV_DIM)).reshape(tm, RWKV_DIM)
        suf = chunk_tot - pre
        if d == 0:
            lc, ex, rem = pre, pre - logdec, suf
        else:
            lc, ex, rem = suf + logdec, suf, pre - logdec
        e_neg = jnp.exp(-lc)
        e_rem = jnp.exp(rem)
        outs = (
            (rb_out, r * jnp.exp(lc)),
            (kb_out, kk * jnp.exp(ex)),
            (kt_out, k_d * e_neg),
            (bt_out, b_d * e_neg),
            (kh_out, k_d * e_rem),
            (bh_out, b_d * e_rem),
        )
        for ref, val in outs:
            ref[d] = val.astype(BF16)
        gc_out[d] = jnp.exp(chunk_tot).reshape(tm // 8, 8, RWKV_DIM)[:, 0, :]


def _rwkv_prep(p, mu_rkv, mu_lora, w0, w2p, a0, a2p, g2p, k_k, k_a, r_k, bd, tri, *, tm):
    s = p.shape[0]
    nb = s // tm
    hb = tm // 16
    last16 = s // 16 - 1
    rkv_w = 3 * RWKV_DIM
    assert COL_RKV % rkv_w == 0 and COL_LORA % LORA_PAD == 0
    c_rkv = COL_RKV // rkv_w
    c_lora = COL_LORA // LORA_PAD

    def prev_map(c):
        return lambda i: (jnp.maximum(i * hb - 1, 0), c)

    def next_map(c):
        return lambda i: (jnp.minimum((i + 1) * hb, last16), c)

    full = lambda *shape: pl.BlockSpec(shape, lambda i: (0,) * len(shape))
    hm = lambda: pl.BlockSpec((2, tm, RWKV_DIM), lambda i: (0, i, 0))
    hm_shape = jax.ShapeDtypeStruct((2, s, RWKV_DIM), BF16)
    return pl.pallas_call(
        _rwkv_prep_kernel,
        grid=(nb,),
        in_specs=[
            pl.BlockSpec((tm, rkv_w), lambda i: (i, c_rkv)),
            pl.BlockSpec((16, rkv_w), prev_map(c_rkv)),
            pl.BlockSpec((16, rkv_w), next_map(c_rkv)),
            pl.BlockSpec((tm, LORA_PAD), lambda i: (i, c_lora)),
            pl.BlockSpec((16, LORA_PAD), prev_map(c_lora)),
            pl.BlockSpec((16, LORA_PAD), next_map(c_lora)),
            full(2, rkv_w), full(2, LORA_PAD),
            full(2, RWKV_DIM), full(2, 128, RWKV_DIM),
            full(2, RWKV_DIM), full(2, 128, RWKV_DIM),
            full(256, RWKV_DIM),
            full(1, RWKV_DIM), full(1, RWKV_DIM), full(1, RWKV_DIM),
            full(GROUP_SLAB, GROUP_SLAB), full(tm, tm),
        ],
        out_specs=[
            pl.BlockSpec((tm, RWKV_DIM), lambda i: (i, 0)),
            pl.BlockSpec((tm, RWKV_DIM), lambda i: (i, 0)),
            pl.BlockSpec((tm, RWKV_DIM), lambda i: (i, 0)),
            hm(), hm(), hm(), hm(), hm(), hm(),
            pl.BlockSpec((2, tm // 8, RWKV_DIM), lambda i: (0, i, 0)),
        ],
        out_shape=[
            jax.ShapeDtypeStruct((s, RWKV_DIM), BF16),
            jax.ShapeDtypeStruct((s, RWKV_DIM), BF16),
            jax.ShapeDtypeStruct((s, RWKV_DIM), BF16),
            hm_shape, hm_shape, hm_shape, hm_shape, hm_shape, hm_shape,
            jax.ShapeDtypeStruct((2, s // 8, RWKV_DIM), F32),
        ],
        compiler_params=_cparams(("parallel",)),
        name="rwkv_prep",
    )(p, p, p, p, p, p, mu_rkv, mu_lora, w0, w2p, a0, a2p, g2p, k_k, k_a, r_k, bd, tri)


def _bdot(a, b):
    return lax.dot_general(a, b, (((2,), (1,)), ((0,), (0,))), preferred_element_type=F32)


def _bdot_nt(a, b):
    return lax.dot_general(a, b, (((2,), (2,)), ((0,), (0,))), preferred_element_type=F32)


def _bdot_tn(a, b):
    return lax.dot_general(a, b, (((1,), (1,)), ((0,), (0,))), preferred_element_type=F32)


WKV_CHUNKS_PER_STEP = 8


def _wkv_kernel(*refs):
    (vf_ref, vr_ref), ins, (gcf_ref, gcr_ref, yf_ref, yr_ref, s_ref) = refs[:2], refs[2:14], refs[14:]

    @pl.when(pl.program_id(0) == 0)
    def _():
        s_ref[...] = jnp.zeros_like(s_ref)

    def rows(ref, part, n):
        return ref.at[n * part:n * (part + 1), :]

    for part in range(WKV_CHUNKS_PER_STEP):
        back = WKV_CHUNKS_PER_STEP - 1 - part
        _wkv_chunk(False, rows(vf_ref, part, CHUNK), [rows(r, part, CHUNK) for r in ins[0:6]],
                   rows(gcf_ref, part, 8), rows(yf_ref, part, CHUNK), s_ref.at[0])
        _wkv_chunk(True, rows(vr_ref, back, CHUNK), [rows(r, back, CHUNK) for r in ins[6:12]],
                   rows(gcr_ref, back, 8), rows(yr_ref, back, CHUNK), s_ref.at[1])


WKV_GROUP = 2
GROUP_LANES = WKV_GROUP * RWKV_HEAD_DIM
N_GROUPS = RWKV_HEADS // WKV_GROUP


def _groups(x):
    return jnp.stack([x[:, GROUP_LANES * p:GROUP_LANES * (p + 1)] for p in range(N_GROUPS)])


def _head_of_lane(shape):
    return lax.broadcasted_iota(jnp.int32, shape, 2) // RWKV_HEAD_DIM


def _group_diag(y):
    head = _head_of_lane(y.shape)
    zero = jnp.zeros_like(y)
    return jnp.concatenate([jnp.where(head == h, y, zero) for h in range(WKV_GROUP)], axis=1)


def _group_tn(a, b):
    full = _bdot_tn(a, b)
    n = RWKV_HEAD_DIM
    head = _head_of_lane((N_GROUPS, n, GROUP_LANES))
    out = full[:, :n]
    for h in range(1, WKV_GROUP):
        out = jnp.where(head == h, full[:, n * h:n * (h + 1)], out)
    return out


def _wkv_chunk(reverse, v_ref, scaled_refs, gc_ref, y_ref, s_ref):
    c = CHUNK
    lanes = GROUP_LANES
    row = lax.broadcasted_iota(jnp.int32, (N_GROUPS, c, lanes), 1)
    col = lax.broadcasted_iota(jnp.int32, (N_GROUPS, c, lanes), 2) % RWKV_HEAD_DIM
    strict = (row < col) if reverse else (row > col)
    incl = (row <= col) if reverse else (row >= col)

    v = _groups(v_ref[...])
    rb, kb, kt, bt, kh, bh = [_groups(r[...]) for r in scaled_refs]

    kr = jnp.concatenate([kb, rb], axis=1)
    sc = _bdot_nt(kr, jnp.concatenate([_group_diag(kt), _group_diag(bt)], axis=1))
    sk = sc[:, :, :lanes]
    sb = sc[:, :, lanes:]
    a_kk = jnp.where(strict, sk[:, :c], 0.0)
    a_rk = jnp.where(incl, sk[:, c:], 0.0)
    l_mat = jnp.where(strict, sb[:, :c], 0.0)
    a_rb = jnp.where(incl, sb[:, c:], 0.0)

    eye = jnp.where(row == col, 1.0, 0.0)
    t_inv = eye - l_mat
    l_b = l_mat.astype(BF16)
    pw = _bdot(l_b, _group_diag(l_b))
    for step in range(5):
        pw_b = _group_diag(pw.astype(BF16))
        if step < 4:
            both = _bdot(jnp.concatenate([t_inv, pw], axis=1).astype(BF16), pw_b)
            t_inv = t_inv + both[:, :c]
            pw = both[:, c:]
        else:
            t_inv = t_inv + _bdot(t_inv.astype(BF16), pw_b)

    q12 = _bdot(jnp.concatenate([a_kk, a_rk], axis=1).astype(BF16), _group_diag(v))
    q1, q2 = q12[:, :c], q12[:, c:]
    t_b = t_inv.astype(BF16)
    kq = jnp.concatenate([_group_diag(kb), _group_diag(q1.astype(BF16))], axis=2)
    wkuv = _bdot(t_b, kq).astype(BF16)
    wk, uv = wkuv[:, :, :lanes], wkuv[:, :, lanes:]
    corr = _bdot(a_rb.astype(BF16), jnp.concatenate([_group_diag(wk), _group_diag(uv)], axis=2))
    rw = rb.astype(F32) - corr[:, :, :lanes]
    y0 = q2 - corr[:, :, lanes:]
    m_mat = _group_tn(wk, bh)
    j_t = _group_tn(jnp.concatenate([v, uv], axis=1), jnp.concatenate([kh, -bh], axis=1))

    n = RWKV_HEAD_DIM
    s_old = s_ref[...]
    s_hi = s_old.astype(BF16)
    s_lo = (s_old - s_hi.astype(F32)).astype(BF16)
    y = _bdot_nt(rw.astype(BF16), _group_diag(s_hi)) + y0
    sm = _bdot(jnp.concatenate([s_hi, s_lo], axis=1), _group_diag(m_mat.astype(BF16)))
    s_ref[...] = s_old * _groups(gc_ref[0:1, :]) - (sm[:, :n] + sm[:, n:]) + j_t
    for p in range(N_GROUPS):
        y_ref[:, lanes * p:lanes * (p + 1)] = y[p]


def _wkv(v, rb, kb, kt, bt, kh, bh, gc):
    s = v.shape[0]
    per = WKV_CHUNKS_PER_STEP
    assert s % (per * CHUNK) == 0
    nc = s // (per * CHUNK)
    scaled = (rb, kb, kt, bt, kh, bh)
    fwd = lambda rows: pl.BlockSpec((None, rows, RWKV_DIM), lambda c: (0, c, 0))
    rev = lambda rows: pl.BlockSpec((None, rows, RWKV_DIM), lambda c: (1, nc - 1 - c, 0))
    y_shape = jax.ShapeDtypeStruct((s, RWKV_DIM), F32)
    return pl.pallas_call(
        _wkv_kernel,
        grid=(nc,),
        in_specs=[
            pl.BlockSpec((per * CHUNK, RWKV_DIM), lambda c: (c, 0)),
            pl.BlockSpec((per * CHUNK, RWKV_DIM), lambda c: (nc - 1 - c, 0)),
            *[fwd(per * CHUNK) for _ in scaled], *[rev(per * CHUNK) for _ in scaled],
            fwd(per * 8), rev(per * 8),
        ],
        out_specs=[
            pl.BlockSpec((per * CHUNK, RWKV_DIM), lambda c: (c, 0)),
            pl.BlockSpec((per * CHUNK, RWKV_DIM), lambda c: (nc - 1 - c, 0)),
        ],
        out_shape=[y_shape, y_shape],
        scratch_shapes=[pltpu.VMEM((2, N_GROUPS, RWKV_HEAD_DIM, GROUP_LANES), F32)],
        compiler_params=_cparams(("arbitrary",)),
        name="wkv7_chunked",
    )(v, v, *scaled, *scaled, gc, gc)


def _rwkv_post_kernel(yf_ref, yr_ref, g_ref, bonus_ref, lg_ref, lb_ref, bd_ref, o_ref):
    y = yf_ref[...] + yr_ref[...]
    hi, lo = _split2(y)
    mean = (_group_sum(hi, bd_ref) + _group_sum(lo, bd_ref)) * (1.0 / RWKV_HEAD_DIM)
    yc = y - mean
    var = _group_mean_sq(yc, bd_ref, RWKV_HEAD_DIM)
    yn = yc * lax.rsqrt(var + LNX_EPS)
    out = (yn * lg_ref[...] + lb_ref[...] + bonus_ref[...].astype(F32)) * g_ref[...].astype(F32)
    o_ref[...] = out.astype(o_ref.dtype)


def _rwkv_post(y_fwd, y_rev, g, bonus, lnx_g, lnx_b, bd, *, tm):
    s = g.shape[0]
    return pl.pallas_call(
        _rwkv_post_kernel,
        grid=(s // tm,),
        in_specs=[
            pl.BlockSpec((tm, RWKV_DIM), lambda i: (i, 0)),
            pl.BlockSpec((tm, RWKV_DIM), lambda i: (i, 0)),
            pl.BlockSpec((tm, RWKV_DIM), lambda i: (i, 0)),
            pl.BlockSpec((tm, RWKV_DIM), lambda i: (i, 0)),
            pl.BlockSpec((1, RWKV_DIM), lambda i: (0, 0)),
            pl.BlockSpec((1, RWKV_DIM), lambda i: (0, 0)),
            pl.BlockSpec((GROUP_SLAB, GROUP_SLAB), lambda i: (0, 0)),
        ],
        out_specs=pl.BlockSpec((tm, RWKV_DIM), lambda i: (i, 0)),
        out_shape=jax.ShapeDtypeStruct((s, RWKV_DIM), BF16),
        compiler_params=_cparams(("parallel",)),
        name="rwkv_post",
    )(y_fwd, y_rev, g, bonus, lnx_g.reshape(1, RWKV_DIM), lnx_b.reshape(1, RWKV_DIM), bd)


def _mem_attn_kernel(q_ref, kv_ref, gq_ref, gk_ref, o_ref):
    for h in range(MEM_HEADS):
        sl = slice(MEM_HEAD_DIM * h, MEM_HEAD_DIM * (h + 1))
        q = q_ref[:, sl].astype(F32)
        q = q * lax.rsqrt(jnp.mean(q * q, axis=-1, keepdims=True) + NORM_EPS)
        q = q * gq_ref[...] * (MEM_HEAD_DIM ** -0.5)
        km = kv_ref[:, sl].astype(F32)
        km = km * lax.rsqrt(jnp.mean(km * km, axis=-1, keepdims=True) + NORM_EPS)
        km = km * gk_ref[...]
        vm = kv_ref[:, MEM_DIM + MEM_HEAD_DIM * h:MEM_DIM + MEM_HEAD_DIM * (h + 1)]
        s = lax.dot_general(q.astype(BF16), km.astype(BF16), (((1,), (1,)), ((), ())),
                            preferred_element_type=F32)
        s = s - jnp.max(s, axis=-1, keepdims=True)
        e = jnp.exp(s)
        pr = e / jnp.sum(e, axis=-1, keepdims=True)
        o_ref[:, sl] = _dot(pr.astype(BF16), vm).astype(o_ref.dtype)


def _mem_attn(p, kv, gq, gk, *, tm):
    s = p.shape[0]
    assert COL_MEM % MEM_DIM == 0
    return pl.pallas_call(
        _mem_attn_kernel,
        grid=(s // tm,),
        in_specs=[
            pl.BlockSpec((tm, MEM_DIM), lambda i: (i, COL_MEM // MEM_DIM)),
            pl.BlockSpec((N_MEM, 2 * MEM_DIM), lambda i: (0, 0)),
            pl.BlockSpec((1, MEM_HEAD_DIM), lambda i: (0, 0)),
            pl.BlockSpec((1, MEM_HEAD_DIM), lambda i: (0, 0)),
        ],
        out_specs=pl.BlockSpec((tm, MEM_DIM), lambda i: (i, 0)),
        out_shape=jax.ShapeDtypeStruct((s, MEM_DIM), BF16),
        compiler_params=_cparams(("parallel",)),
        name="mem_attn",
    )(p, kv, gq, gk)


def _merge_kernel(o0_ref, o1_ref, o2_ref, g0_ref, g1_ref, g2_ref, w_ref, m_ref):
    def gate(g_ref):
        return 0.5 * jnp.tanh(0.5 * g_ref[...].astype(F32)) + 0.5

    acc = gate(g0_ref) * _dot(o0_ref[...], w_ref[0])
    acc = acc + gate(g1_ref) * _dot(o1_ref[...], w_ref[1])
    acc = acc + gate(g2_ref) * _dot(o2_ref[...], w_ref[2])
    m_ref[...] = acc.astype(m_ref.dtype)


def _merge(o_diff, o_rwkv, o_mem, p, w_branch, layer, *, tm, tn):
    s = p.shape[0]
    assert COL_GATE % tn == 0 and D_MODEL % tn == 0
    gate_blk = COL_GATE // tn
    per = D_MODEL // tn
    o_spec = lambda: pl.BlockSpec((tm, 1024), lambda i, j: (i, 0))
    g_spec = lambda b: pl.BlockSpec((tm, tn), lambda i, j: (i, gate_blk + b * per + j))
    return pl.pallas_call(
        _merge_kernel,
        grid=(s // tm, per),
        in_specs=[
            o_spec(), o_spec(), o_spec(),
            g_spec(0), g_spec(1), g_spec(2),
            pl.BlockSpec((None, N_BRANCH, 1024, tn), lambda i, j: (layer, 0, 0, j)),
        ],
        out_specs=pl.BlockSpec((tm, tn), lambda i, j: (i, j)),
        out_shape=jax.ShapeDtypeStruct((s, D_MODEL), BF16),
        compiler_params=_cparams(("parallel", "arbitrary")),
        name="merge",
    )(o_diff, o_rwkv, o_mem, p, p, p, w_branch)


GLU_TILE = 256


def _ffn_up_glu_kernel(x_ref, xp_ref, xn_ref, g_ref, wg_ref, wv_ref, cw_ref, cb_ref, o_ref,
                       h_ref, halo_ref):
    i = pl.program_id(0)
    nb = pl.num_programs(0)

    def norm(x):
        ms = jnp.mean(x * x, axis=-1, keepdims=True)
        return x * lax.rsqrt(ms + NORM_EPS) * g_ref[...]

    @pl.when(pl.program_id(1) == 0)
    def _():
        h_ref[...] = norm(x_ref[...]).astype(BF16)
        row = lax.broadcasted_iota(jnp.int32, xp_ref.shape, 0)
        before = jnp.where(i > 0, pltpu.roll(norm(xp_ref[...]), 1, 0), 0.0)
        after = jnp.where(i < nb - 1, pltpu.roll(norm(xn_ref[...]), 1, 0), 0.0)
        halo_ref[...] = jnp.where(row == 0, before, jnp.where(row == 1, after, 0.0)).astype(BF16)

    h = h_ref[...]
    halo = halo_ref[...]
    tn = o_ref.shape[1]
    for c0 in range(0, tn, GLU_TILE):
        sl = slice(c0, c0 + GLU_TILE)
        wg = wg_ref[:, sl]
        ug = _dot(h, wg)
        uv = _dot(h, wv_ref[:, sl])
        edge = _dot(halo, wg)
        gp, gn = _shift_rows(ug, edge[0:1], edge[1:2])
        cw = cw_ref[:, sl]
        conv = cw[0:1] * gp + cw[1:2] * ug + cw[2:3] * gn + cb_ref[:, sl]
        half = 0.5 * conv
        o_ref[:, sl] = ((half + half * jnp.tanh(half)) * uv).astype(o_ref.dtype)


def _ffn_up_glu(x, g, w_up, layer, conv_w, conv_b, *, tm, tn):
    s, k = x.shape
    assert tn % GLU_TILE == 0
    nj = D_FF // tn
    hb = tm // 16
    last16 = s // 16 - 1
    return pl.pallas_call(
        _ffn_up_glu_kernel,
        grid=(s // tm, nj),
        in_specs=[
            pl.BlockSpec((tm, k), lambda i, j: (i, 0)),
            pl.BlockSpec((16, k), lambda i, j: (jnp.maximum(i * hb - 1, 0), 0)),
            pl.BlockSpec((16, k), lambda i, j: (jnp.minimum((i + 1) * hb, last16), 0)),
            pl.BlockSpec((1, k), lambda i, j: (0, 0)),
            pl.BlockSpec((None, k, tn), lambda i, j: (layer, 0, j)),
            pl.BlockSpec((None, k, tn), lambda i, j: (layer, 0, nj + j)),
            pl.BlockSpec((3, tn), lambda i, j: (0, j)),
            pl.BlockSpec((1, tn), lambda i, j: (0, j)),
        ],
        out_specs=pl.BlockSpec((tm, tn), lambda i, j: (i, j)),
        out_shape=jax.ShapeDtypeStruct((s, D_FF), BF16),
        scratch_shapes=[pltpu.VMEM((tm, k), BF16), pltpu.VMEM((16, k), BF16)],
        compiler_params=_cparams(("parallel", "arbitrary")),
        name="ffn_up_glu",
    )(x, x, x, g.reshape(1, k), w_up, w_up, conv_w, conv_b.reshape(1, D_FF))


def _block_ones(n, group):
    idx = np.arange(n) // group
    return jnp.asarray(idx[:, None] == idx[None, :], BF16)


def _chunk_tri(n):
    idx = np.arange(n)
    same = (idx[:, None] // CHUNK) == (idx[None, :] // CHUNK)
    return jnp.asarray(same & (idx[:, None] >= idx[None, :]), BF16)


def _pad_rows(w, rows_before, total):
    n = w.shape[-1]
    out = jnp.zeros((total, n), w.dtype)
    return lax.dynamic_update_slice(out, w, (rows_before, 0))


def _pad_w_in(w):
    lora0 = COL_RKV + 3 * RWKV_DIM
    mem0 = lora0 + LORA_COLS
    gate0 = mem0 + MEM_DIM
    pad = jnp.zeros((w.shape[0], LORA_PAD - LORA_COLS), w.dtype)
    parts = [w[:, :lora0], w[:, gate0:], w[:, mem0:gate0], w[:, lora0:mem0], pad]
    return jnp.concatenate(parts, axis=1).astype(BF16)


def _tile_plan(s):
    rows = 1024 if s % 1024 == 0 else 512
    return dict(
        w_in=dict(tm=rows, tn=2304),
        mem_kv=dict(tm=N_MEM, tn=1024),
        diff_prep=512, rwkv_prep=256, rwkv_post=512, mem_attn=1024 if rows == 1024 else 512,
        merge=dict(tm=512, tn=D_MODEL),
        w_out=dict(tm=512, tn=D_MODEL),
        ffn_up=dict(tm=rows, tn=512),
        ffn_down=dict(tm=rows, tn=512),
    )


def kernel(x, mem, attn_norm_g, w_in, diff_qk_g, diff_lambda, diff_subln_g, rwkv_mu, rwkv_w0,
           rwkv_w2, rwkv_a0, rwkv_a2, rwkv_g2, rwkv_k_k, rwkv_k_a, rwkv_r_k, rwkv_lnx_g,
           rwkv_lnx_b, mem_norm_g, w_mem_kv, mem_qk_g, w_branch, w_out, ffn_norm_g, w_ffn_up,
           ffn_conv_w, ffn_conv_b, w_ffn_down):
    b, s, d = x.shape
    assert b == 1 and d == D_MODEL and s % ATT_BQ == 0
    xs = x.reshape(s, d)
    mem2 = mem.reshape(N_MEM, d)
    tiles = _tile_plan(s)
    bd64 = _block_ones(GROUP_SLAB, 64)
    tri = _chunk_tri(tiles["rwkv_prep"])
    n_rwkv_main = 3 * RWKV_DIM
    w_mem_kv_b = w_mem_kv.astype(BF16)
    w_branch_b = w_branch.astype(BF16)
    w_out_b = w_out.astype(BF16)
    w_ffn_up_b = w_ffn_up.astype(BF16)
    w_ffn_down_b = w_ffn_down.astype(BF16)

    for l in range(DEPTH):
        lam_init = 0.8 - 0.6 * math.exp(-0.3 * l)
        p = _rms_mm(xs, attn_norm_g[l], _pad_w_in(w_in[l])[None], 0, name="rms_w_in",
                    **tiles["w_in"])

        gq = jnp.tile(diff_qk_g[l, 0].reshape(1, 128), (1, DIFF_HEADS))
        gk = jnp.tile(diff_qk_g[l, 1].reshape(1, 128), (1, DIFF_HEADS))
        qc, kc, score_bound = _attn_consts(diff_qk_g[l])
        qa, ka, vt = _diff_prep(p, gq, gk, qc, kc, bd64, tm=tiles["diff_prep"])
        o_diff = lax.cond(
            score_bound <= SCORE_BOUND_MAX,
            functools.partial(_diff_attn, lam_init=lam_init, online=False),
            functools.partial(_diff_attn, lam_init=lam_init, online=True),
            qa, ka, vt, diff_lambda[l], diff_subln_g[l])

        mu = rwkv_mu[l]
        mu_rkv = mu[:, :n_rwkv_main]
        mu_lora = jnp.pad(mu[:, n_rwkv_main:], ((0, 0), (0, LORA_PAD - LORA_COLS)))
        w2p = jnp.stack([_pad_rows(rwkv_w2[l, dd], 64 * dd, 128) for dd in range(2)]).astype(BF16)
        a2p = jnp.stack([_pad_rows(rwkv_a2[l, dd], 64 * dd, 128) for dd in range(2)]).astype(BF16)
        g2p = _pad_rows(rwkv_g2[l], 0, 256).astype(BF16)
        (v_h, g_tok, bonus, rb, kb, kt, bt, kh, bh, gc) = _rwkv_prep(
            p, mu_rkv, mu_lora, rwkv_w0[l], w2p, rwkv_a0[l], a2p, g2p,
            rwkv_k_k[l].reshape(1, RWKV_DIM), rwkv_k_a[l].reshape(1, RWKV_DIM),
            rwkv_r_k[l].reshape(1, RWKV_DIM), bd64, tri, tm=tiles["rwkv_prep"])
        y_fwd, y_rev = _wkv(v_h, rb, kb, kt, bt, kh, bh, gc)
        o_rwkv = _rwkv_post(y_fwd, y_rev, g_tok, bonus, rwkv_lnx_g[l], rwkv_lnx_b[l], bd64,
                            tm=tiles["rwkv_post"])

        kv = _rms_mm(mem2, mem_norm_g[l], w_mem_kv_b, l, name="rms_mem_kv", **tiles["mem_kv"])
        o_mem = _mem_attn(p, kv, mem_qk_g[l, 0].reshape(1, MEM_HEAD_DIM),
                          mem_qk_g[l, 1].reshape(1, MEM_HEAD_DIM), tm=tiles["mem_attn"])

        merged = _merge(o_diff, o_rwkv, o_mem, p, w_branch_b, l, **tiles["merge"])
        xs = _mm_res(merged, w_out_b, l, xs, name="w_out_res", **tiles["w_out"])

        act = _ffn_up_glu(xs, ffn_norm_g[l], w_ffn_up_b, l, ffn_conv_w[l], ffn_conv_b[l],
                          **tiles["ffn_up"])
        xs = _mm_res(act, w_ffn_down_b, l, xs, name="ffn_down_res", **tiles["ffn_down"])

    return xs.reshape(b, s, d)
```
